```python
import math
import jax, jax.numpy as jnp
from jax import lax
import numpy as np

D_MODEL = 2048
BATCH = 8
SEQ = 4096
DEPTH = 2

D_FF = 5632
HEAD_DIM = 128
CONV_CH = 1024
CONV_WIDTH = 31
ATT_HEADS = 8
ATT_DIM = ATT_HEADS * HEAD_DIM
DIL_PATTERNS = ((128, 1), (512, 4), (2048, 16))
Q_BLOCK = 128
ROPE_THETA = 10000.0
HGRN_HEADS = 16
HGRN_KDIM = 128
HGRN_VDIM = 128
HGRN_WIDTH = HGRN_HEADS * HGRN_KDIM
CHUNK = 64
EPS = 1e-6
N_EVEN = (DEPTH + 1) // 2
N_ODD = DEPTH // 2
EVEN_IN = 2 * CONV_CH + 3 * ATT_DIM
EVEN_OUT = CONV_CH + ATT_DIM
ODD_IN = 4 * HGRN_WIDTH

kernel_name = "hybrid_conformer_dilated_hgrn2_macaron"

F32 = jnp.float32


def rmsnorm(x, g):
    xf = x.astype(F32)
    y = xf * lax.rsqrt(jnp.mean(xf * xf, axis=-1, keepdims=True) + EPS)
    return (y * g.astype(F32)).astype(x.dtype)


def layernorm(x, g, b):
    xf = x.astype(F32)
    mu = jnp.mean(xf, axis=-1, keepdims=True)
    xc = xf - mu
    var = jnp.mean(xc * xc, axis=-1, keepdims=True)
    return (xc * lax.rsqrt(var + EPS) * g.astype(F32) + b.astype(F32)).astype(x.dtype)


def swiglu(x, wg, wu, wd):
    return (jax.nn.silu(x @ wg) * (x @ wu)) @ wd


def rope(x, pos):
    half = HEAD_DIM // 2
    inv = jnp.exp(-math.log(ROPE_THETA) * jnp.arange(half, dtype=F32) / half)
    ang = pos.astype(F32)[:, None] * inv[None, :]
    cos = jnp.cos(ang)[None, :, None, :]
    sin = jnp.sin(ang)[None, :, None, :]
    xf = x.astype(F32)
    x1, x2 = xf[..., :half], xf[..., half:]
    return jnp.concatenate([x1 * cos - x2 * sin, x2 * cos + x1 * sin], axis=-1).astype(x.dtype)


def dilated_branch(q, k, v, dil, reach):
    b, s, h, hd = q.shape
    L = s // dil
    nb = -(-L // Q_BLOCK)
    Lp = nb * Q_BLOCK

    def to_classes(t):
        return t.reshape(b, L, dil, h, hd).transpose(0, 2, 3, 1, 4)

    qc, kc, vc = to_classes(q), to_classes(k), to_classes(v)
    pad_q = [(0, 0)] * 3 + [(0, Lp - L), (0, 0)]
    pad_kv = [(0, 0)] * 3 + [(Q_BLOCK, Lp - L), (0, 0)]
    qb = jnp.pad(qc, pad_q).reshape(b, dil, h, nb, Q_BLOCK, hd)
    kp = jnp.pad(kc, pad_kv).reshape(b, dil, h, nb + 1, Q_BLOCK, hd)
    vp = jnp.pad(vc, pad_kv).reshape(b, dil, h, nb + 1, Q_BLOCK, hd)
    kb = jnp.concatenate([kp[:, :, :, :-1], kp[:, :, :, 1:]], axis=-2)
    vb = jnp.concatenate([vp[:, :, :, :-1], vp[:, :, :, 1:]], axis=-2)
    scores = jnp.einsum('bchnqd,bchnkd->bchnqk', qb, kb, preferred_element_type=F32)
    qi = jnp.arange(Q_BLOCK)[:, None]
    km = jnp.arange(2 * Q_BLOCK)[None, :]
    dist = Q_BLOCK + qi - km
    band = (dist >= 0) & (dist <= reach)
    blk = jnp.arange(nb)[:, None, None]
    valid = band[None] & ((blk * Q_BLOCK + km[None] - Q_BLOCK) >= 0)
    scores = jnp.where(valid, scores, -jnp.inf)
    mx = jnp.max(scores, axis=-1, keepdims=True)
    p = jnp.exp(scores - mx)
    den = jnp.sum(p, axis=-1, keepdims=True)
    o = jnp.einsum('bchnqk,bchnkd->bchnqd', p, vb.astype(F32)) / den
    lse = (mx + jnp.log(den))[..., 0]
    o = o.reshape(b, dil, h, Lp, hd)[:, :, :, :L].transpose(0, 3, 1, 2, 4).reshape(b, s, h, hd)
    lse = lse.reshape(b, dil, h, Lp)[..., :L].transpose(0, 3, 1, 2).reshape(b, s, h)
    return o, lse


def conv_attn_mixer(hn, pos, w_in, conv_w, conv_b, cn_g, cn_b, qn_g, kn_g, w_out):
    b, s, _ = hn.shape
    u = hn @ w_in
    a_val = u[..., :CONV_CH]
    a_gate = u[..., CONV_CH:2 * CONV_CH]
    o0 = 2 * CONV_CH
    q = u[..., o0:o0 + ATT_DIM].reshape(b, s, ATT_HEADS, HEAD_DIM)
    k = u[..., o0 + ATT_DIM:o0 + 2 * ATT_DIM].reshape(b, s, ATT_HEADS, HEAD_DIM)
    v = u[..., o0 + 2 * ATT_DIM:].reshape(b, s, ATT_HEADS, HEAD_DIM)
    a = a_val * jax.nn.sigmoid(a_gate)
    a = lax.conv_general_dilated(
        a, conv_w[:, None, :].astype(a.dtype), window_strides=(1,),
        padding=[(CONV_WIDTH - 1, 0)], dimension_numbers=('NWC', 'WIO', 'NWC'),
        feature_group_count=CONV_CH) + conv_b
    a = jax.nn.silu(layernorm(a, cn_g, cn_b))
    scale = HEAD_DIM ** -0.5
    q = (rope(rmsnorm(q, qn_g), pos).astype(F32) * scale).astype(hn.dtype)
    k = rope(rmsnorm(k, kn_g), pos)
    outs, lses = [], []
    for window, dil in DIL_PATTERNS:
        o_i, l_i = dilated_branch(q, k, v, dil, window // dil)
        outs.append(o_i)
        lses.append(l_i)
    wts = jax.nn.softmax(jnp.stack(lses, axis=0), axis=0)
    o = jnp.einsum('pbsh,pbshd->bshd', wts, jnp.stack(outs, axis=0))
    o = o.reshape(b, s, ATT_DIM).astype(hn.dtype)
    return jnp.concatenate([a, o], axis=-1) @ w_out


def hgrn2_mixer(hn, w_in, lb, gn_g, w_out):
    b, s, _ = hn.shape
    u = (hn @ w_in).astype(F32)
    qz = u[..., :HGRN_WIDTH]
    fz = u[..., HGRN_WIDTH:2 * HGRN_WIDTH]
    iz = u[..., 2 * HGRN_WIDTH:3 * HGRN_WIDTH]
    gz = u[..., 3 * HGRN_WIDTH:]
    lb = lb.astype(F32)
    q = jax.nn.silu(qz)
    logf = jnp.logaddexp(jnp.log(lb), jnp.log1p(-lb) + jax.nn.log_sigmoid(fz))
    kk = (1.0 - lb) * jax.nn.sigmoid(-fz)
    nc = s // CHUNK

    def heads(t, d):
        return t.reshape(b, nc, CHUNK, HGRN_HEADS, d).transpose(1, 0, 3, 2, 4)

    qh, kh = heads(q, HGRN_KDIM), heads(kk, HGRN_KDIM)
    vh, gh = heads(iz, HGRN_VDIM), heads(logf, HGRN_KDIM)
    bcum = jnp.cumsum(gh, axis=-2)
    causal = jnp.tril(jnp.ones((CHUNK, CHUNK), dtype=bool))

    def step(S, xs):
        qc, kc, vc, bc = xs
        diff = bc[:, :, :, None, :] - bc[:, :, None, :, :]
        decay = jnp.exp(jnp.where(causal[:, :, None], diff, -jnp.inf))
        attn = jnp.einsum('bhtd,bhsd,bhtsd->bhts', qc, kc, decay)
        o = jnp.einsum('bhts,bhsv->bhtv', attn, vc) + \
            jnp.einsum('bhtd,bhdv->bhtv', qc * jnp.exp(bc), S)
        blast = bc[:, :, -1:, :]
        S = jnp.exp(blast[:, :, 0, :])[..., None] * S + \
            jnp.einsum('bhsd,bhsv->bhdv', kc * jnp.exp(blast - bc), vc)
        return S, o

    S0 = jnp.zeros((b, HGRN_HEADS, HGRN_KDIM, HGRN_VDIM), F32)
    _, o = lax.scan(step, S0, (qh, kh, vh, bcum))
    o = o.transpose(1, 0, 3, 2, 4).reshape(b, s, HGRN_HEADS, HGRN_VDIM)
    o = o * lax.rsqrt(jnp.mean(o * o, axis=-1, keepdims=True) + EPS)
    o = o.reshape(b, s, HGRN_WIDTH) * gn_g.astype(F32) * jax.nn.silu(gz)
    return o.astype(hn.dtype) @ w_out


def _fwd_setup_inputs(seed: int = 0) -> dict:
    key = jax.random.key(seed)
    ks = iter(jax.random.split(key, 32))

    def nrm(shape, fan_in):
        return jax.random.normal(next(ks), shape, F32) * (fan_in ** -0.5)

    def gain(shape):
        return 1.0 + 0.02 * jax.random.normal(next(ks), shape, F32)

    def bias(shape):
        return 0.01 * jax.random.normal(next(ks), shape, F32)

    return {
        "x": jax.random.normal(next(ks), (BATCH, SEQ, D_MODEL), F32),
        "norm_ffn1": gain((DEPTH, D_MODEL)),
        "ffn1_wg": nrm((DEPTH, D_MODEL, D_FF), D_MODEL),
        "ffn1_wu": nrm((DEPTH, D_MODEL, D_FF), D_MODEL),
        "ffn1_wd": nrm((DEPTH, D_FF, D_MODEL), D_FF),
        "norm_mix": gain((DEPTH, D_MODEL)),
        "norm_ffn2": gain((DEPTH, D_MODEL)),
        "ffn2_wg": nrm((DEPTH, D_MODEL, D_FF), D_MODEL),
        "ffn2_wu": nrm((DEPTH, D_MODEL, D_FF), D_MODEL),
        "ffn2_wd": nrm((DEPTH, D_FF, D_MODEL), D_FF),
        "ev_w_in": nrm((N_EVEN, D_MODEL, EVEN_IN), D_MODEL),
        "ev_conv_w": nrm((N_EVEN, CONV_WIDTH, CONV_CH), CONV_WIDTH),
        "ev_conv_b": bias((N_EVEN, CONV_CH)),
        "ev_cn_g": gain((N_EVEN, CONV_CH)),
        "ev_cn_b": bias((N_EVEN, CONV_CH)),
        "ev_qn_g": gain((N_EVEN, HEAD_DIM)),
        "ev_kn_g": gain((N_EVEN, HEAD_DIM)),
        "ev_w_out": nrm((N_EVEN, EVEN_OUT, D_MODEL), EVEN_OUT),
        "od_w_in": nrm((N_ODD, D_MODEL, ODD_IN), D_MODEL),
        "od_lb_logits": jax.random.normal(next(ks), (DEPTH, HGRN_WIDTH), F32),
        "od_gn_g": gain((N_ODD, HGRN_WIDTH)),
        "od_w_out": nrm((N_ODD, HGRN_WIDTH, D_MODEL), HGRN_WIDTH),
    }


def _fwd_reference(x, norm_ffn1, ffn1_wg, ffn1_wu, ffn1_wd, norm_mix, norm_ffn2, ffn2_wg,
              ffn2_wu, ffn2_wd, ev_w_in, ev_conv_w, ev_conv_b, ev_cn_g, ev_cn_b,
              ev_qn_g, ev_kn_g, ev_w_out, od_w_in, od_lb_logits, od_gn_g, od_w_out):
    pos = jnp.arange(x.shape[1], dtype=jnp.int32)
    p = jax.nn.softmax(od_lb_logits.astype(F32), axis=0)
    lower_bounds = jnp.cumsum(p, axis=0) - p[0:1]
    for l in range(DEPTH):
        j = l // 2
        x = x + 0.5 * swiglu(rmsnorm(x, norm_ffn1[l]), ffn1_wg[l], ffn1_wu[l], ffn1_wd[l])
        hn = rmsnorm(x, norm_mix[l])
        if l % 2 == 0:
            x = x + conv_attn_mixer(hn, pos, ev_w_in[j], ev_conv_w[j], ev_conv_b[j],
                                    ev_cn_g[j], ev_cn_b[j], ev_qn_g[j], ev_kn_g[j], ev_w_out[j])
        else:
            x = x + hgrn2_mixer(hn, od_w_in[j], lower_bounds[l], od_gn_g[j], od_w_out[j])
        x = x + 0.5 * swiglu(rmsnorm(x, norm_ffn2[l]), ffn2_wg[l], ffn2_wu[l], ffn2_wd[l])
    return x


import jax as _jax
import jax.numpy as _jnp

TWIN_FORMAT = 'train_step'
FWD_PARAMS = ['x', 'norm_ffn1', 'ffn1_wg', 'ffn1_wu', 'ffn1_wd', 'norm_mix', 'norm_ffn2', 'ffn2_wg', 'ffn2_wu', 'ffn2_wd', 'ev_w_in', 'ev_conv_w', 'ev_conv_b', 'ev_cn_g', 'ev_cn_b', 'ev_qn_g', 'ev_kn_g', 'ev_w_out', 'od_w_in', 'od_lb_logits', 'od_gn_g', 'od_w_out']
TWIN_WEIGHTS = ['norm_ffn1', 'ffn1_wg', 'ffn1_wu', 'ffn1_wd', 'norm_mix', 'norm_ffn2', 'ffn2_wg', 'ffn2_wu', 'ffn2_wd', 'ev_w_in', 'ev_conv_w', 'ev_conv_b', 'ev_cn_g', 'ev_cn_b', 'ev_qn_g', 'ev_kn_g', 'ev_w_out', 'od_w_in', 'od_lb_logits', 'od_gn_g', 'od_w_out']
TWIN_DIFF_INPUT = 'x'
TWIN_INPUTS = ['x', 'norm_ffn1', 'ffn1_wg', 'ffn1_wu', 'ffn1_wd', 'norm_mix', 'norm_ffn2', 'ffn2_wg', 'ffn2_wu', 'ffn2_wd', 'ev_w_in', 'ev_conv_w', 'ev_conv_b', 'ev_cn_g', 'ev_cn_b', 'ev_qn_g', 'ev_kn_g', 'ev_w_out', 'od_w_in', 'od_lb_logits', 'od_gn_g', 'od_w_out', 'loss_target', 'm_norm_ffn1', 'm_ffn1_wg', 'm_ffn1_wu', 'm_ffn1_wd', 'm_norm_mix', 'm_norm_ffn2', 'm_ffn2_wg', 'm_ffn2_wu', 'm_ffn2_wd', 'm_ev_w_in', 'm_ev_conv_w', 'm_ev_conv_b', 'm_ev_cn_g', 'm_ev_cn_b', 'm_ev_qn_g', 'm_ev_kn_g', 'm_ev_w_out', 'm_od_w_in', 'm_od_lb_logits', 'm_od_gn_g', 'm_od_w_out', 'v_norm_ffn1', 'v_ffn1_wg', 'v_ffn1_wu', 'v_ffn1_wd', 'v_norm_mix', 'v_norm_ffn2', 'v_ffn2_wg', 'v_ffn2_wu', 'v_ffn2_wd', 'v_ev_w_in', 'v_ev_conv_w', 'v_ev_conv_b', 'v_ev_cn_g', 'v_ev_cn_b', 'v_ev_qn_g', 'v_ev_kn_g', 'v_ev_w_out', 'v_od_w_in', 'v_od_lb_logits', 'v_od_gn_g', 'v_od_w_out']
TWIN_OUTPUTS = ['loss', 'grad_x', 'grad_norm_ffn1', 'grad_ffn1_wg', 'grad_ffn1_wu', 'grad_ffn1_wd', 'grad_norm_mix', 'grad_norm_ffn2', 'grad_ffn2_wg', 'grad_ffn2_wu', 'grad_ffn2_wd', 'grad_ev_w_in', 'grad_ev_conv_w', 'grad_ev_conv_b', 'grad_ev_cn_g', 'grad_ev_cn_b', 'grad_ev_qn_g', 'grad_ev_kn_g', 'grad_ev_w_out', 'grad_od_w_in', 'grad_od_lb_logits', 'grad_od_gn_g', 'grad_od_w_out', 'delta_norm_ffn1', 'delta_ffn1_wg', 'delta_ffn1_wu', 'delta_ffn1_wd', 'delta_norm_mix', 'delta_norm_ffn2', 'delta_ffn2_wg', 'delta_ffn2_wu', 'delta_ffn2_wd', 'delta_ev_w_in', 'delta_ev_conv_w', 'delta_ev_conv_b', 'delta_ev_cn_g', 'delta_ev_cn_b', 'delta_ev_qn_g', 'delta_ev_kn_g', 'delta_ev_w_out', 'delta_od_w_in', 'delta_od_lb_logits', 'delta_od_gn_g', 'delta_od_w_out', 'new_m_norm_ffn1', 'new_m_ffn1_wg', 'new_m_ffn1_wu', 'new_m_ffn1_wd', 'new_m_norm_mix', 'new_m_norm_ffn2', 'new_m_ffn2_wg', 'new_m_ffn2_wu', 'new_m_ffn2_wd', 'new_m_ev_w_in', 'new_m_ev_conv_w', 'new_m_ev_conv_b', 'new_m_ev_cn_g', 'new_m_ev_cn_b', 'new_m_ev_qn_g', 'new_m_ev_kn_g', 'new_m_ev_w_out', 'new_m_od_w_in', 'new_m_od_lb_logits', 'new_m_od_gn_g', 'new_m_od_w_out', 'new_v_norm_ffn1', 'new_v_ffn1_wg', 'new_v_ffn1_wu', 'new_v_ffn1_wd', 'new_v_norm_mix', 'new_v_norm_ffn2', 'new_v_ffn2_wg', 'new_v_ffn2_wu', 'new_v_ffn2_wd', 'new_v_ev_w_in', 'new_v_ev_conv_w', 'new_v_ev_conv_b', 'new_v_ev_cn_g', 'new_v_ev_cn_b', 'new_v_ev_qn_g', 'new_v_ev_kn_g', 'new_v_ev_w_out', 'new_v_od_w_in', 'new_v_od_lb_logits', 'new_v_od_gn_g', 'new_v_od_w_out']
TWIN_LEAF_KINDS = {'loss': 'loss', 'grad_x': 'grad_x', 'grad_norm_ffn1': 'grad_w', 'grad_ffn1_wg': 'grad_w', 'grad_ffn1_wu': 'grad_w', 'grad_ffn1_wd': 'grad_w', 'grad_norm_mix': 'grad_w', 'grad_norm_ffn2': 'grad_w', 'grad_ffn2_wg': 'grad_w', 'grad_ffn2_wu': 'grad_w', 'grad_ffn2_wd': 'grad_w', 'grad_ev_w_in': 'grad_w', 'grad_ev_conv_w': 'grad_w', 'grad_ev_conv_b': 'grad_w', 'grad_ev_cn_g': 'grad_w', 'grad_ev_cn_b': 'grad_w', 'grad_ev_qn_g': 'grad_w', 'grad_ev_kn_g': 'grad_w', 'grad_ev_w_out': 'grad_w', 'grad_od_w_in': 'grad_w', 'grad_od_lb_logits': 'grad_w', 'grad_od_gn_g': 'grad_w', 'grad_od_w_out': 'grad_w', 'delta_norm_ffn1': 'delta_w', 'delta_ffn1_wg': 'delta_w', 'delta_ffn1_wu': 'delta_w', 'delta_ffn1_wd': 'delta_w', 'delta_norm_mix': 'delta_w', 'delta_norm_ffn2': 'delta_w', 'delta_ffn2_wg': 'delta_w', 'delta_ffn2_wu': 'delta_w', 'delta_ffn2_wd': 'delta_w', 'delta_ev_w_in': 'delta_w', 'delta_ev_conv_w': 'delta_w', 'delta_ev_conv_b': 'delta_w', 'delta_ev_cn_g': 'delta_w', 'delta_ev_cn_b': 'delta_w', 'delta_ev_qn_g': 'delta_w', 'delta_ev_kn_g': 'delta_w', 'delta_ev_w_out': 'delta_w', 'delta_od_w_in': 'delta_w', 'delta_od_lb_logits': 'delta_w', 'delta_od_gn_g': 'delta_w', 'delta_od_w_out': 'delta_w', 'new_m_norm_ffn1': 'new_m', 'new_m_ffn1_wg': 'new_m', 'new_m_ffn1_wu': 'new_m', 'new_m_ffn1_wd': 'new_m', 'new_m_norm_mix': 'new_m', 'new_m_norm_ffn2': 'new_m', 'new_m_ffn2_wg': 'new_m', 'new_m_ffn2_wu': 'new_m', 'new_m_ffn2_wd': 'new_m', 'new_m_ev_w_in': 'new_m', 'new_m_ev_conv_w': 'new_m', 'new_m_ev_conv_b': 'new_m', 'new_m_ev_cn_g': 'new_m', 'new_m_ev_cn_b': 'new_m', 'new_m_ev_qn_g': 'new_m', 'new_m_ev_kn_g': 'new_m', 'new_m_ev_w_out': 'new_m', 'new_m_od_w_in': 'new_m', 'new_m_od_lb_logits': 'new_m', 'new_m_od_gn_g': 'new_m', 'new_m_od_w_out': 'new_m', 'new_v_norm_ffn1': 'new_v', 'new_v_ffn1_wg': 'new_v', 'new_v_ffn1_wu': 'new_v', 'new_v_ffn1_wd': 'new_v', 'new_v_norm_mix': 'new_v', 'new_v_norm_ffn2': 'new_v', 'new_v_ffn2_wg': 'new_v', 'new_v_ffn2_wu': 'new_v', 'new_v_ffn2_wd': 'new_v', 'new_v_ev_w_in': 'new_v', 'new_v_ev_conv_w': 'new_v', 'new_v_ev_conv_b': 'new_v', 'new_v_ev_cn_g': 'new_v', 'new_v_ev_cn_b': 'new_v', 'new_v_ev_qn_g': 'new_v', 'new_v_ev_kn_g': 'new_v', 'new_v_ev_w_out': 'new_v', 'new_v_od_w_in': 'new_v', 'new_v_od_lb_logits': 'new_v', 'new_v_od_gn_g': 'new_v', 'new_v_od_w_out': 'new_v'}


def _forward(args):
    return _fwd_reference(*[args[k] for k in FWD_PARAMS])


def _output_shape():
    def fwd():
        inp = _fwd_setup_inputs(0)
        return _fwd_reference(*[inp[k] for k in FWD_PARAMS])
    out = _jax.eval_shape(fwd)
    return out.shape, out.dtype

N_MICROBATCH = 1
ADAM_LR = 0.001
ADAM_B1 = 0.9
ADAM_B2 = 0.999
ADAM_EPS = 1e-08
ADAM_WD = 0.01
ADAM_STEP = 10
PER_EXAMPLE_BATCH_AXIS = {'x': 0, 'loss_target': 0}
SHARED_INPUTS = []
_WEIGHT_DTYPES = {'norm_ffn1': _jnp.float32, 'ffn1_wg': _jnp.float32, 'ffn1_wu': _jnp.float32, 'ffn1_wd': _jnp.float32, 'norm_mix': _jnp.float32, 'norm_ffn2': _jnp.float32, 'ffn2_wg': _jnp.float32, 'ffn2_wu': _jnp.float32, 'ffn2_wd': _jnp.float32, 'ev_w_in': _jnp.float32, 'ev_conv_w': _jnp.float32, 'ev_conv_b': _jnp.float32, 'ev_cn_g': _jnp.float32, 'ev_cn_b': _jnp.float32, 'ev_qn_g': _jnp.float32, 'ev_kn_g': _jnp.float32, 'ev_w_out': _jnp.float32, 'od_w_in': _jnp.float32, 'od_lb_logits': _jnp.float32, 'od_gn_g': _jnp.float32, 'od_w_out': _jnp.float32}
MOMENT_SCALE = {'norm_ffn1': 3.075541e+00, 'ffn1_wg': 5.657501e-02, 'ffn1_wu': 5.554267e-02, 'ffn1_wd': 9.149147e-02, 'norm_mix': 4.728500e+00, 'norm_ffn2': 3.077915e+00, 'ffn2_wg': 5.626102e-02, 'ffn2_wu': 5.138214e-02, 'ffn2_wd': 8.344987e-02, 'ev_w_in': 9.089093e-02, 'ev_conv_w': 3.002207e-01, 'ev_conv_b': 4.869713e+00, 'ev_cn_g': 7.246989e+00, 'ev_cn_b': 5.165349e+00, 'ev_qn_g': 1.186864e+00, 'ev_kn_g': 1.187794e+00, 'ev_w_out': 6.994086e-01, 'od_w_in': 2.895249e-01, 'od_lb_logits': 9.612130e-03, 'od_gn_g': 5.886687e+00, 'od_w_out': 2.276749e-01}


def _to_microbatches(a, axis):
    t = _jnp.moveaxis(a, axis, 0)
    t = t.reshape((N_MICROBATCH, t.shape[0] // N_MICROBATCH) + t.shape[1:])
    return _jnp.moveaxis(t, 1, axis + 1)


def setup_inputs(seed: int = 0) -> dict:
    inp = _fwd_setup_inputs(seed)
    key = _jax.random.fold_in(_jax.random.key(seed), 7919)
    shape, _ = _output_shape()
    out = dict(inp)
    out["loss_target"] = _jax.random.normal(_jax.random.fold_in(key, 0), shape, _jnp.float32)
    for i, name in enumerate(TWIN_WEIGHTS):
        w = inp[name].astype(_jnp.float32)
        if MOMENT_SCALE is None:
            s = _jnp.sqrt(_jnp.mean(_jnp.square(w)) + 1e-30)
        else:
            s = MOMENT_SCALE[name]
        km, kv = _jax.random.split(_jax.random.fold_in(key, i + 1))
        out[name] = w
        out["m_" + name] = s * _jax.random.normal(km, w.shape, _jnp.float32)
        out["v_" + name] = (s * s) * _jax.random.uniform(kv, w.shape, _jnp.float32, 0.5, 1.5)
    if N_MICROBATCH > 1:
        for name, axis in PER_EXAMPLE_BATCH_AXIS.items():
            out[name] = _to_microbatches(out[name], axis)
    return {'x': out['x'], 'norm_ffn1': out['norm_ffn1'], 'ffn1_wg': out['ffn1_wg'], 'ffn1_wu': out['ffn1_wu'], 'ffn1_wd': out['ffn1_wd'], 'norm_mix': out['norm_mix'], 'norm_ffn2': out['norm_ffn2'], 'ffn2_wg': out['ffn2_wg'], 'ffn2_wu': out['ffn2_wu'], 'ffn2_wd': out['ffn2_wd'], 'ev_w_in': out['ev_w_in'], 'ev_conv_w': out['ev_conv_w'], 'ev_conv_b': out['ev_conv_b'], 'ev_cn_g': out['ev_cn_g'], 'ev_cn_b': out['ev_cn_b'], 'ev_qn_g': out['ev_qn_g'], 'ev_kn_g': out['ev_kn_g'], 'ev_w_out': out['ev_w_out'], 'od_w_in': out['od_w_in'], 'od_lb_logits': out['od_lb_logits'], 'od_gn_g': out['od_gn_g'], 'od_w_out': out['od_w_out'], 'loss_target': out['loss_target'], 'm_norm_ffn1': out['m_norm_ffn1'], 'm_ffn1_wg': out['m_ffn1_wg'], 'm_ffn1_wu': out['m_ffn1_wu'], 'm_ffn1_wd': out['m_ffn1_wd'], 'm_norm_mix': out['m_norm_mix'], 'm_norm_ffn2': out['m_norm_ffn2'], 'm_ffn2_wg': out['m_ffn2_wg'], 'm_ffn2_wu': out['m_ffn2_wu'], 'm_ffn2_wd': out['m_ffn2_wd'], 'm_ev_w_in': out['m_ev_w_in'], 'm_ev_conv_w': out['m_ev_conv_w'], 'm_ev_conv_b': out['m_ev_conv_b'], 'm_ev_cn_g': out['m_ev_cn_g'], 'm_ev_cn_b': out['m_ev_cn_b'], 'm_ev_qn_g': out['m_ev_qn_g'], 'm_ev_kn_g': out['m_ev_kn_g'], 'm_ev_w_out': out['m_ev_w_out'], 'm_od_w_in': out['m_od_w_in'], 'm_od_lb_logits': out['m_od_lb_logits'], 'm_od_gn_g': out['m_od_gn_g'], 'm_od_w_out': out['m_od_w_out'], 'v_norm_ffn1': out['v_norm_ffn1'], 'v_ffn1_wg': out['v_ffn1_wg'], 'v_ffn1_wu': out['v_ffn1_wu'], 'v_ffn1_wd': out['v_ffn1_wd'], 'v_norm_mix': out['v_norm_mix'], 'v_norm_ffn2': out['v_norm_ffn2'], 'v_ffn2_wg': out['v_ffn2_wg'], 'v_ffn2_wu': out['v_ffn2_wu'], 'v_ffn2_wd': out['v_ffn2_wd'], 'v_ev_w_in': out['v_ev_w_in'], 'v_ev_conv_w': out['v_ev_conv_w'], 'v_ev_conv_b': out['v_ev_conv_b'], 'v_ev_cn_g': out['v_ev_cn_g'], 'v_ev_cn_b': out['v_ev_cn_b'], 'v_ev_qn_g': out['v_ev_qn_g'], 'v_ev_kn_g': out['v_ev_kn_g'], 'v_ev_w_out': out['v_ev_w_out'], 'v_od_w_in': out['v_od_w_in'], 'v_od_lb_logits': out['v_od_lb_logits'], 'v_od_gn_g': out['v_od_gn_g'], 'v_od_w_out': out['v_od_w_out']}


def _loss(weights, diff, rest, loss_target):
    with _jax.named_scope("forward"):
        args = {**rest, TWIN_DIFF_INPUT: diff, **{k: w.astype(_WEIGHT_DTYPES[k]) for k, w in weights.items()}}
        y = _forward(args)
    with _jax.named_scope("loss_head"):
        err = _jnp.square(y.astype(_jnp.float32) - loss_target)
        return 0.5 * _jnp.sum(_jnp.mean(err, axis=-1)) if err.ndim else 0.5 * err


def _adamw(w, g, m, v):
    m = ADAM_B1 * m + (1.0 - ADAM_B1) * g
    v = ADAM_B2 * v + (1.0 - ADAM_B2) * _jnp.square(g)
    m_hat = m / (1.0 - ADAM_B1 ** ADAM_STEP)
    v_hat = v / (1.0 - ADAM_B2 ** ADAM_STEP)
    delta = -ADAM_LR * (m_hat / (_jnp.sqrt(v_hat) + ADAM_EPS) + ADAM_WD * w)
    return delta, m, v


def reference(x, norm_ffn1, ffn1_wg, ffn1_wu, ffn1_wd, norm_mix, norm_ffn2, ffn2_wg, ffn2_wu, ffn2_wd, ev_w_in, ev_conv_w, ev_conv_b, ev_cn_g, ev_cn_b, ev_qn_g, ev_kn_g, ev_w_out, od_w_in, od_lb_logits, od_gn_g, od_w_out, loss_target, m_norm_ffn1, m_ffn1_wg, m_ffn1_wu, m_ffn1_wd, m_norm_mix, m_norm_ffn2, m_ffn2_wg, m_ffn2_wu, m_ffn2_wd, m_ev_w_in, m_ev_conv_w, m_ev_conv_b, m_ev_cn_g, m_ev_cn_b, m_ev_qn_g, m_ev_kn_g, m_ev_w_out, m_od_w_in, m_od_lb_logits, m_od_gn_g, m_od_w_out, v_norm_ffn1, v_ffn1_wg, v_ffn1_wu, v_ffn1_wd, v_norm_mix, v_norm_ffn2, v_ffn2_wg, v_ffn2_wu, v_ffn2_wd, v_ev_w_in, v_ev_conv_w, v_ev_conv_b, v_ev_cn_g, v_ev_cn_b, v_ev_qn_g, v_ev_kn_g, v_ev_w_out, v_od_w_in, v_od_lb_logits, v_od_gn_g, v_od_w_out):
    given = dict(x=x, norm_ffn1=norm_ffn1, ffn1_wg=ffn1_wg, ffn1_wu=ffn1_wu, ffn1_wd=ffn1_wd, norm_mix=norm_mix, norm_ffn2=norm_ffn2, ffn2_wg=ffn2_wg, ffn2_wu=ffn2_wu, ffn2_wd=ffn2_wd, ev_w_in=ev_w_in, ev_conv_w=ev_conv_w, ev_conv_b=ev_conv_b, ev_cn_g=ev_cn_g, ev_cn_b=ev_cn_b, ev_qn_g=ev_qn_g, ev_kn_g=ev_kn_g, ev_w_out=ev_w_out, od_w_in=od_w_in, od_lb_logits=od_lb_logits, od_gn_g=od_gn_g, od_w_out=od_w_out, loss_target=loss_target, m_norm_ffn1=m_norm_ffn1, m_ffn1_wg=m_ffn1_wg, m_ffn1_wu=m_ffn1_wu, m_ffn1_wd=m_ffn1_wd, m_norm_mix=m_norm_mix, m_norm_ffn2=m_norm_ffn2, m_ffn2_wg=m_ffn2_wg, m_ffn2_wu=m_ffn2_wu, m_ffn2_wd=m_ffn2_wd, m_ev_w_in=m_ev_w_in, m_ev_conv_w=m_ev_conv_w, m_ev_conv_b=m_ev_conv_b, m_ev_cn_g=m_ev_cn_g, m_ev_cn_b=m_ev_cn_b, m_ev_qn_g=m_ev_qn_g, m_ev_kn_g=m_ev_kn_g, m_ev_w_out=m_ev_w_out, m_od_w_in=m_od_w_in, m_od_lb_logits=m_od_lb_logits, m_od_gn_g=m_od_gn_g, m_od_w_out=m_od_w_out, v_norm_ffn1=v_norm_ffn1, v_ffn1_wg=v_ffn1_wg, v_ffn1_wu=v_ffn1_wu, v_ffn1_wd=v_ffn1_wd, v_norm_mix=v_norm_mix, v_norm_ffn2=v_norm_ffn2, v_ffn2_wg=v_ffn2_wg, v_ffn2_wu=v_ffn2_wu, v_ffn2_wd=v_ffn2_wd, v_ev_w_in=v_ev_w_in, v_ev_conv_w=v_ev_conv_w, v_ev_conv_b=v_ev_conv_b, v_ev_cn_g=v_ev_cn_g, v_ev_cn_b=v_ev_cn_b, v_ev_qn_g=v_ev_qn_g, v_ev_kn_g=v_ev_kn_g, v_ev_w_out=v_ev_w_out, v_od_w_in=v_od_w_in, v_od_lb_logits=v_od_lb_logits, v_od_gn_g=v_od_gn_g, v_od_w_out=v_od_w_out)
    weights = {n: given[n] for n in TWIN_WEIGHTS}
    shared = {n: given[n] for n in SHARED_INPUTS}
    per_example = {n: given[n] for n in ['x']}
    grad_fn = _jax.value_and_grad(_loss, argnums=(0, 1))

    def one_microbatch(ex, loss_target):
        ex = dict(ex)
        diff = ex.pop(TWIN_DIFF_INPUT)
        return grad_fn(weights, diff, {**shared, **ex}, loss_target)

    if N_MICROBATCH == 1:
        loss, (grad_w, grad_x) = one_microbatch(per_example, given["loss_target"])
    else:
        def body(carry, xs):
            loss_sum, grad_sum = carry
            l_k, (gw_k, gx_k) = one_microbatch(xs[0], xs[1])
            with _jax.named_scope("update"):
                return (loss_sum + l_k, _jax.tree.map(_jnp.add, grad_sum, gw_k)), gx_k

        init = (_jnp.zeros((), _jnp.float32), _jax.tree.map(_jnp.zeros_like, weights))
        (loss, grad_w), grad_x = _jax.lax.scan(body, init, (per_example, given["loss_target"]))
    with _jax.named_scope("update"):
        delta_w, new_m, new_v = {}, {}, {}
        for n in TWIN_WEIGHTS:
            delta_w[n], new_m[n], new_v[n] = _adamw(weights[n], grad_w[n], given["m_" + n], given["v_" + n])
    return (loss, grad_x, *[grad_w[n] for n in TWIN_WEIGHTS], *[delta_w[n] for n in TWIN_WEIGHTS],
            *[new_m[n] for n in TWIN_WEIGHTS], *[new_v[n] for n in TWIN_WEIGHTS])
```

```python
import functools
import math

import jax
import jax.numpy as jnp
from jax import lax
from jax.experimental import pallas as pl
from jax.experimental.pallas import tpu as pltpu

F32 = jnp.float32
BF16 = jnp.bfloat16
MESH = pl.DeviceIdType.MESH
N_DEV = 8

EPS = 1e-6
HEAD_DIM = 128
CONV_WIDTH = 31
DIL_PATTERNS = ((128, 1), (512, 4), (2048, 16))
Q_BLOCK = 128
ROPE_THETA = 10000.0
HGRN_KDIM = 128
HGRN_CHUNK = 256

ADAM_LR = 0.001
ADAM_B1 = 0.9
ADAM_B2 = 0.999
ADAM_EPS = 1e-08
ADAM_WD = 0.01
ADAM_STEP = 10

VMEM_LIMIT_BYTES = 56 * 1024 * 1024
LANE = 128
SUBLANE_BF16 = 16

ANY = pl.BlockSpec(memory_space=pl.ANY)


def _tile(n, pref, mult):
    t = (min(pref, n) // mult) * mult
    while t > 0:
        if n % t == 0:
            return t
        t -= mult
    return n


def _params(*sem):
    return pltpu.CompilerParams(dimension_semantics=sem, vmem_limit_bytes=VMEM_LIMIT_BYTES)


_DOT_DIMS = {
    "nn": (((1,), (0,)), ((), ())),
    "nt": (((1,), (1,)), ((), ())),
    "tn": (((0,), (0,)), ((), ())),
}


def _dot(a, b, mode):
    return lax.dot_general(a, b, _DOT_DIMS[mode], preferred_element_type=F32)


def _mm(a, b, mode, out_dtype, name, res=None, tm=1024, tn=1024, tk=2048):
    if mode == "nt":
        (M, K), N = a.shape, b.shape[0]
    elif mode == "nn":
        (M, K), N = a.shape, b.shape[1]
    else:
        (K, M), N = a.shape, b.shape[1]
    tm, tn, tk = _tile(M, tm, LANE), _tile(N, tn, LANE), _tile(K, tk, LANE)
    nk = K // tk

    def body(*refs):
        if res is None:
            a_ref, b_ref, o_ref, acc = refs
        else:
            a_ref, b_ref, r_ref, o_ref, acc = refs
        k = pl.program_id(2)

        @pl.when(k == 0)
        def _():
            acc[...] = jnp.zeros_like(acc)

        acc[...] += _dot(a_ref[...].astype(BF16), b_ref[...].astype(BF16), mode)

        @pl.when(k == nk - 1)
        def _():
            r = acc[...]
            if res is not None:
                r = r_ref[...] + r
            o_ref[...] = r.astype(out_dtype)

    a_spec = {"nt": pl.BlockSpec((tm, tk), lambda i, j, k: (i, k)),
              "nn": pl.BlockSpec((tm, tk), lambda i, j, k: (i, k)),
              "tn": pl.BlockSpec((tk, tm), lambda i, j, k: (k, i))}[mode]
    b_spec = {"nt": pl.BlockSpec((tn, tk), lambda i, j, k: (j, k)),
              "nn": pl.BlockSpec((tk, tn), lambda i, j, k: (k, j)),
              "tn": pl.BlockSpec((tk, tn), lambda i, j, k: (k, j))}[mode]
    o_spec = pl.BlockSpec((tm, tn), lambda i, j, k: (i, j))
    in_specs = [a_spec, b_spec] + ([o_spec] if res is not None else [])
    args = (a, b) + ((res,) if res is not None else ())
    return pl.pallas_call(
        body, name=name, grid=(M // tm, N // tn, nk),
        in_specs=in_specs, out_specs=o_spec,
        out_shape=jax.ShapeDtypeStruct((M, N), out_dtype),
        scratch_shapes=[pltpu.VMEM((tm, tn), F32)],
        compiler_params=_params("parallel", "parallel", "arbitrary"),
    )(*args)


def _rms_fwd(x, gain, name):
    S, D = x.shape
    tm = _tile(S, 512, SUBLANE_BF16)

    def body(x_ref, g_ref, o_ref):
        xv = x_ref[...]
        r = lax.rsqrt(jnp.mean(xv * xv, axis=-1, keepdims=True) + EPS)
        o_ref[...] = (xv * r * g_ref[...]).astype(BF16)

    return pl.pallas_call(
        body, name=name, grid=(S // tm,),
        in_specs=[pl.BlockSpec((tm, D), lambda i: (i, 0)), pl.BlockSpec((1, D), lambda i: (0, 0))],
        out_specs=pl.BlockSpec((tm, D), lambda i: (i, 0)),
        out_shape=jax.ShapeDtypeStruct((S, D), BF16),
        compiler_params=_params("parallel"),
    )(x, gain)


def _rms_bwd(x, gain, dxn, dy, name):
    S, D = x.shape
    tm = _tile(S, 512, 8)

    def body(x_ref, g_ref, dxn_ref, dy_ref, dx_ref, dg_ref):
        @pl.when(pl.program_id(0) == 0)
        def _():
            dg_ref[...] = jnp.zeros_like(dg_ref)

        xv = x_ref[...]
        r = lax.rsqrt(jnp.mean(xv * xv, axis=-1, keepdims=True) + EPS)
        xh = xv * r
        dxn_v = dxn_ref[...]
        dg_ref[...] += jnp.sum(dxn_v * xh, axis=0, keepdims=True)
        dxh = dxn_v * g_ref[...]
        dx_ref[...] = dy_ref[...] + r * (dxh - xh * jnp.mean(dxh * xh, axis=-1, keepdims=True))

    row = pl.BlockSpec((tm, D), lambda i: (i, 0))
    vec = pl.BlockSpec((1, D), lambda i: (0, 0))
    return pl.pallas_call(
        body, name=name, grid=(S // tm,),
        in_specs=[row, vec, row, row], out_specs=[row, vec],
        out_shape=[jax.ShapeDtypeStruct((S, D), F32), jax.ShapeDtypeStruct((1, D), F32)],
        compiler_params=_params("arbitrary"),
    )(x, gain, dxn, dy)


def _ffn_fwd(x, xn, w, name):
    S, D = x.shape
    F = w.shape[1]
    tm, tf = _tile(S, 512, SUBLANE_BF16), _tile(F, 512, LANE)
    nf = F // tf

    def body(x_ref, xn_ref, w_ref, o_ref, gu_ref, acc):
        f = pl.program_id(1)

        @pl.when(f == 0)
        def _():
            acc[...] = jnp.zeros_like(acc)

        xnv = xn_ref[...]
        g = _dot(xnv, w_ref[0], "nt")
        u = _dot(xnv, w_ref[1], "nt")
        gu_ref[0] = g.astype(BF16)
        gu_ref[1] = u.astype(BF16)
        h = (g * jax.nn.sigmoid(g) * u).astype(BF16)
        acc[...] += _dot(h, w_ref[2], "nn")

        @pl.when(f == nf - 1)
        def _():
            o_ref[...] = x_ref[...] + 0.5 * acc[...]

    row = pl.BlockSpec((tm, D), lambda i, f: (i, 0))
    return pl.pallas_call(
        body, name=name, grid=(S // tm, nf),
        in_specs=[row, row, pl.BlockSpec((3, tf, D), lambda i, f: (0, f, 0))],
        out_specs=[row, pl.BlockSpec((2, tm, tf), lambda i, f: (0, i, f))],
        out_shape=[jax.ShapeDtypeStruct((S, D), F32), jax.ShapeDtypeStruct((2, S, F), BF16)],
        scratch_shapes=[pltpu.VMEM((tm, D), F32)],
        compiler_params=_params("parallel", "arbitrary"),
    )(x, xn, w)


def _ffn_bwd_dx(dy, w, gu, name):
    S, D = dy.shape
    F = w.shape[1]
    tm, tf = _tile(S, 512, SUBLANE_BF16), _tile(F, 512, LANE)
    nf = F // tf

    def body(dy_ref, w_ref, gu_ref, dxn_ref, dout_ref, t_ref, acc):
        f = pl.program_id(1)

        @pl.when(f == 0)
        def _():
            acc[...] = jnp.zeros_like(acc)
            dout_ref[...] = (0.5 * dy_ref[...]).astype(BF16)

        dh = _dot(dout_ref[...], w_ref[2], "nt")
        g = gu_ref[0].astype(F32)
        u = gu_ref[1].astype(F32)
        sig = jax.nn.sigmoid(g)
        silu = g * sig
        dg = (dh * u * (sig * (1.0 + g * (1.0 - sig)))).astype(BF16)
        du = (dh * silu).astype(BF16)
        t_ref[0] = dg
        t_ref[1] = du
        t_ref[2] = (silu * u).astype(BF16)
        acc[...] += _dot(dg, w_ref[0], "nn") + _dot(du, w_ref[1], "nn")

        @pl.when(f == nf - 1)
        def _():
            dxn_ref[...] = acc[...]

    row = pl.BlockSpec((tm, D), lambda i, f: (i, 0))
    return pl.pallas_call(
        body, name=name, grid=(S // tm, nf),
        in_specs=[row, pl.BlockSpec((3, tf, D), lambda i, f: (0, f, 0)),
                  pl.BlockSpec((2, tm, tf), lambda i, f: (0, i, f))],
        out_specs=[row, row, pl.BlockSpec((3, tm, tf), lambda i, f: (0, i, f))],
        out_shape=[jax.ShapeDtypeStruct((S, D), F32), jax.ShapeDtypeStruct((S, D), BF16),
                   jax.ShapeDtypeStruct((3, S, F), BF16)],
        scratch_shapes=[pltpu.VMEM((tm, D), F32)],
        compiler_params=_params("parallel", "arbitrary"),
    )(dy, w, gu)


def _ffn_bwd_dw(xn, dout, t, name):
    S, D = xn.shape
    F = t.shape[2]
    ts, tf = _tile(S, 512, LANE), _tile(F, 512, LANE)
    ns = S // ts

    def body(xn_ref, dout_ref, t_ref, dw_ref, acc):
        s = pl.program_id(1)

        @pl.when(s == 0)
        def _():
            acc[...] = jnp.zeros_like(acc)

        xnv = xn_ref[...]
        acc[0] += _dot(t_ref[0], xnv, "tn")
        acc[1] += _dot(t_ref[1], xnv, "tn")
        acc[2] += _dot(t_ref[2], dout_ref[...], "tn")

        @pl.when(s == ns - 1)
        def _():
            dw_ref[...] = acc[...].astype(BF16)

    row = pl.BlockSpec((ts, D), lambda f, s: (s, 0))
    return pl.pallas_call(
        body, name=name, grid=(F // tf, ns),
        in_specs=[row, row, pl.BlockSpec((3, ts, tf), lambda f, s: (0, s, f))],
        out_specs=pl.BlockSpec((3, tf, D), lambda f, s: (0, f, 0)),
        out_shape=jax.ShapeDtypeStruct((3, F, D), BF16),
        scratch_shapes=[pltpu.VMEM((3, tf, D), F32)],
        compiler_params=_params("parallel", "arbitrary"),
    )(xn, dout, t)


def _loss_grad(y, target, name):
    S, D = y.shape
    tm = _tile(S, 512, 8)

    def body(y_ref, t_ref, dy_ref, l_ref):
        @pl.when(pl.program_id(0) == 0)
        def _():
            l_ref[...] = jnp.zeros_like(l_ref)

        e = y_ref[...] - t_ref[...]
        dy_ref[...] = e * (1.0 / D)
        l_ref[...] += 0.5 * jnp.sum(jnp.sum(e * e, axis=-1, keepdims=True) * (1.0 / D))

    row = pl.BlockSpec((tm, D), lambda i: (i, 0))
    one = pl.BlockSpec((8, LANE), lambda i: (0, 0))
    return pl.pallas_call(
        body, name=name, grid=(S // tm,),
        in_specs=[row, row], out_specs=[row, one],
        out_shape=[jax.ShapeDtypeStruct((S, D), F32), jax.ShapeDtypeStruct((8, LANE), F32)],
        compiler_params=_params("arbitrary"),
    )(y, target)


def _adamw(w, g, m, v, name):
    shape = w.shape
    C = shape[-1]
    R = math.prod(shape[:-1])
    tr = _tile(R, max(8, (1 << 19) // C // 8 * 8), 8)
    c1 = 1.0 / (1.0 - ADAM_B1 ** ADAM_STEP)
    c2 = 1.0 / (1.0 - ADAM_B2 ** ADAM_STEP)

    def body(w_ref, g_ref, m_ref, v_ref, d_ref, nm_ref, nv_ref):
        gv = g_ref[...]
        nm = ADAM_B1 * m_ref[...] + (1.0 - ADAM_B1) * gv
        nv = ADAM_B2 * v_ref[...] + (1.0 - ADAM_B2) * (gv * gv)
        nm_ref[...] = nm
        nv_ref[...] = nv
        d_ref[...] = -ADAM_LR * ((nm * c1) / (jnp.sqrt(nv * c2) + ADAM_EPS) + ADAM_WD * w_ref[...])

    blk = pl.BlockSpec((tr, C), lambda i: (i, 0))
    sds = jax.ShapeDtypeStruct((R, C), F32)
    outs = pl.pallas_call(
        body, name=name, grid=(R // tr,),
        in_specs=[blk] * 4, out_specs=[blk] * 3, out_shape=[sds] * 3,
        compiler_params=_params("parallel"),
    )(*(a.reshape(R, C) for a in (w, g, m, v)))
    return tuple(o.reshape(shape) for o in outs)


def _me():
    return lax.axis_index("x"), lax.axis_index("y"), lax.axis_index("c")


def _all_gather(shards, name):
    na = len(shards)

    def body(*refs):
        ins, outs = refs[:na], refs[na:2 * na]
        send_sems, recv_sems, local_sems = refs[2 * na:]
        x, y, c = _me()
        me, sibling = (x, y, c), (x, y, 1 - c)
        chips = [(1 - x, y), (x, 1 - y), (1 - x, 1 - y)]

        def blk(a, p):
            return outs[a].at[:, 4 * p[0] + 2 * p[1] + p[2]]

        def copy(a, k, block, to, src=None):
            return pltpu.make_async_remote_copy(
                src_ref=blk(a, block) if src is None else src, dst_ref=blk(a, block),
                send_sem=send_sems.at[a * 7 + k], recv_sem=recv_sems.at[a * 7 + k],
                device_id=to, device_id_type=MESH)

        mine, first, passed = [], [], []
        for a in range(na):
            mine.append(pltpu.make_async_copy(ins[a], blk(a, me), local_sems.at[a]))
            mine[a].start()
            first.append([copy(a, 0, me, sibling, src=ins[a])]
                         + [copy(a, 1 + j, me, (*chip, c), src=ins[a]) for j, chip in enumerate(chips)])
            for cp in first[a]:
                cp.start()
        for a in range(na):
            passed.append([copy(a, 4 + j, (*chip, c), sibling) for j, chip in enumerate(chips)])
            for j, chip in enumerate(chips):
                copy(a, 1 + j, (*chip, c), me).wait_recv()
                passed[a][j].start()
        for a in range(na):
            copy(a, 0, sibling, me).wait_recv()
            for j, chip in enumerate(chips):
                copy(a, 4 + j, (*chip, 1 - c), me).wait_recv()
        for a in range(na):
            for cp in first[a] + passed[a]:
                cp.wait_send()
            mine[a].wait()

    return pl.pallas_call(
        body, name=name,
        in_specs=[ANY] * na, out_specs=[ANY] * na,
        out_shape=[jax.ShapeDtypeStruct((s.shape[0], N_DEV) + s.shape[1:], s.dtype) for s in shards],
        scratch_shapes=[pltpu.SemaphoreType.DMA((7 * na,)), pltpu.SemaphoreType.DMA((7 * na,)),
                        pltpu.SemaphoreType.DMA((na,))],
    )(*shards)


def _exchange_core_halves(grads, name):
    na = len(grads)

    def body(*refs):
        ins, outs = refs[:na], refs[na:2 * na]
        send_sems, recv_sems = refs[2 * na:]
        x, y, c = _me()
        copies = [pltpu.make_async_remote_copy(
            src_ref=ins[a].at[:, :, 1 - c], dst_ref=outs[a],
            send_sem=send_sems.at[a], recv_sem=recv_sems.at[a],
            device_id=(x, y, 1 - c), device_id_type=MESH) for a in range(na)]
        for cp in copies:
            cp.start()
        for cp in copies:
            cp.wait()

    return pl.pallas_call(
        body, name=name,
        in_specs=[ANY] * na, out_specs=[ANY] * na,
        out_shape=[jax.ShapeDtypeStruct(g.shape[:2] + g.shape[3:], g.dtype) for g in grads],
        scratch_shapes=[pltpu.SemaphoreType.DMA((na,)), pltpu.SemaphoreType.DMA((na,))],
    )(*grads)


def _add_core_halves(grad, got, c_idx, name):
    n, nk, _, r, C = grad.shape
    tr = _tile(r, 1024, SUBLANE_BF16)

    def body(c_ref, g_ref, r_ref, o_ref):
        o_ref[...] = (g_ref[...].astype(F32) + r_ref[...].astype(F32)).astype(BF16)

    return pl.pallas_call(
        body, name=name,
        grid_spec=pltpu.PrefetchScalarGridSpec(
            num_scalar_prefetch=1, grid=(n, nk, r // tr),
            in_specs=[pl.BlockSpec((None, None, None, tr, C), lambda i, k, t, c: (i, k, c[0], t, 0)),
                      pl.BlockSpec((None, None, tr, C), lambda i, k, t, c: (i, k, t, 0))],
            out_specs=pl.BlockSpec((None, None, tr, C), lambda i, k, t, c: (i, k, t, 0))),
        out_shape=jax.ShapeDtypeStruct((n, nk, r, C), BF16),
        compiler_params=_params("parallel", "parallel", "parallel"),
    )(c_idx, grad, got)


def _exchange_chip_blocks(sums, name):
    na = len(sums)

    def body(*refs):
        ins, outs = refs[:na], refs[na:2 * na]
        send_sems, recv_sems = refs[2 * na:]
        x, y, c = _me()
        chips = [(1 - x, y), (x, 1 - y), (1 - x, 1 - y)]
        copies = [pltpu.make_async_remote_copy(
            src_ref=ins[a].at[:, 2 * chip[0] + chip[1]], dst_ref=outs[a].at[j],
            send_sem=send_sems.at[3 * a + j], recv_sem=recv_sems.at[3 * a + j],
            device_id=(*chip, c), device_id_type=MESH)
            for a in range(na) for j, chip in enumerate(chips)]
        for cp in copies:
            cp.start()
        for cp in copies:
            cp.wait()

    return pl.pallas_call(
        body, name=name,
        in_specs=[ANY] * na, out_specs=[ANY] * na,
        out_shape=[jax.ShapeDtypeStruct((3, s.shape[0]) + s.shape[2:], s.dtype) for s in sums],
        scratch_shapes=[pltpu.SemaphoreType.DMA((3 * na,)), pltpu.SemaphoreType.DMA((3 * na,))],
    )(*sums)


def _sum_chip_blocks(sums, got, k_idx, name):
    n, _, r, C = sums.shape
    tr = _tile(r, 512, SUBLANE_BF16)

    def body(k_ref, s_ref, r_ref, o_ref):
        acc = s_ref[...].astype(F32)
        for j in range(3):
            acc = acc + r_ref[j].astype(F32)
        o_ref[...] = acc

    return pl.pallas_call(
        body, name=name,
        grid_spec=pltpu.PrefetchScalarGridSpec(
            num_scalar_prefetch=1, grid=(n, r // tr),
            in_specs=[pl.BlockSpec((None, None, tr, C), lambda i, t, k: (i, k[0], t, 0)),
                      pl.BlockSpec((3, None, tr, C), lambda i, t, k: (0, i, t, 0))],
            out_specs=pl.BlockSpec((None, tr, C), lambda i, t, k: (i, t, 0))),
        out_shape=jax.ShapeDtypeStruct((n, r, C), F32),
        compiler_params=_params("parallel", "parallel"),
    )(k_idx, sums, got)


def _all_reduce_small(v, name):
    R = v.shape[0]

    def body(v_ref, o_ref, buf, send_sems, recv_sems):
        x, y, c = _me()
        me = 4 * x + 2 * y + c
        buf[me] = v_ref[...]
        copies = []
        for k in range(1, N_DEV):
            peer = (x ^ (k >> 2), y ^ ((k >> 1) & 1), c ^ (k & 1))
            copies.append(pltpu.make_async_remote_copy(
                src_ref=v_ref, dst_ref=buf.at[me],
                send_sem=send_sems.at[k - 1], recv_sem=recv_sems.at[k - 1],
                device_id=peer, device_id_type=MESH))
        for cp in copies:
            cp.start()
        for cp in copies:
            cp.wait()
        acc = buf[0]
        for d in range(1, N_DEV):
            acc = acc + buf[d]
        o_ref[...] = acc

    vm = pl.BlockSpec(memory_space=pltpu.VMEM)
    return pl.pallas_call(
        body, name=name, in_specs=[vm], out_specs=vm,
        out_shape=jax.ShapeDtypeStruct((R, LANE), F32),
        scratch_shapes=[pltpu.VMEM((N_DEV, R, LANE), F32),
                        pltpu.SemaphoreType.DMA((N_DEV - 1,)), pltpu.SemaphoreType.DMA((N_DEV - 1,))],
        compiler_params=pltpu.CompilerParams(vmem_limit_bytes=VMEM_LIMIT_BYTES),
    )(v)


def _pack_rows(parts):
    flat = jnp.concatenate([p.reshape(-1).astype(F32) for p in parts])
    n = flat.shape[0]
    rows = -(-n // (8 * LANE)) * 8
    flat = jnp.pad(flat, (0, rows * LANE - n))
    return flat.reshape(rows, LANE)


def _unpack_rows(packed, shapes):
    flat = packed.reshape(-1)
    out, off = [], 0
    for s in shapes:
        n = math.prod(s)
        out.append(flat[off:off + n].reshape(s))
        off += n
    return out


def _rmsnorm(x, g):
    return x * lax.rsqrt(jnp.mean(x * x, axis=-1, keepdims=True) + EPS) * g


def _layernorm(x, g, b):
    mu = jnp.mean(x, axis=-1, keepdims=True)
    xc = x - mu
    var = jnp.mean(xc * xc, axis=-1, keepdims=True)
    return xc * lax.rsqrt(var + EPS) * g + b


def _rope(x, pos):
    half = HEAD_DIM // 2
    inv = jnp.exp(-math.log(ROPE_THETA) * jnp.arange(half, dtype=F32) / half)
    ang = pos.astype(F32)[:, None] * inv[None, :]
    cos = jnp.cos(ang)[None, :, None, :]
    sin = jnp.sin(ang)[None, :, None, :]
    x1, x2 = x[..., :half], x[..., half:]
    return jnp.concatenate([x1 * cos - x2 * sin, x2 * cos + x1 * sin], axis=-1)


def _dilated_branch(q, k, v, dil, reach):
    b, s, h, hd = q.shape
    L = s // dil
    nb = -(-L // Q_BLOCK)
    Lp = nb * Q_BLOCK

    def to_classes(t):
        return t.reshape(b, L, dil, h, hd).transpose(0, 2, 3, 1, 4)

    qc, kc, vc = to_classes(q), to_classes(k), to_classes(v)
    pad_q = [(0, 0)] * 3 + [(0, Lp - L), (0, 0)]
    pad_kv = [(0, 0)] * 3 + [(Q_BLOCK, Lp - L), (0, 0)]
    qb = jnp.pad(qc, pad_q).reshape(b, dil, h, nb, Q_BLOCK, hd)
    kp = jnp.pad(kc, pad_kv).reshape(b, dil, h, nb + 1, Q_BLOCK, hd)
    vp = jnp.pad(vc, pad_kv).reshape(b, dil, h, nb + 1, Q_BLOCK, hd)
    kb = jnp.concatenate([kp[:, :, :, :-1], kp[:, :, :, 1:]], axis=-2)
    vb = jnp.concatenate([vp[:, :, :, :-1], vp[:, :, :, 1:]], axis=-2)
    scores = jnp.einsum('bchnqd,bchnkd->bchnqk', qb, kb, preferred_element_type=F32)
    qi = jnp.arange(Q_BLOCK)[:, None]
    km = jnp.arange(2 * Q_BLOCK)[None, :]
    dist = Q_BLOCK + qi - km
    band = (dist >= 0) & (dist <= reach)
    blk = jnp.arange(nb)[:, None, None]
    valid = band[None] & ((blk * Q_BLOCK + km[None] - Q_BLOCK) >= 0)
    scores = jnp.where(valid, scores, -jnp.inf)
    mx = jnp.max(scores, axis=-1, keepdims=True)
    p = jnp.exp(scores - mx)
    den = jnp.sum(p, axis=-1, keepdims=True)
    o = jnp.einsum('bchnqk,bchnkd->bchnqd', p, vb) / den
    lse = (mx + jnp.log(den))[..., 0]
    o = o.reshape(b, dil, h, Lp, hd)[:, :, :, :L].transpose(0, 3, 1, 2, 4).reshape(b, s, h, hd)
    lse = lse.reshape(b, dil, h, Lp)[..., :L].transpose(0, 3, 1, 2).reshape(b, s, h)
    return o, lse


def _even_core(u, conv_w, conv_b, cn_g, cn_b, qn_g, kn_g):
    s = u.shape[0]
    C = conv_w.shape[1]
    A = (u.shape[1] - 2 * C) // 3
    H = A // HEAD_DIM
    pos = jnp.arange(s, dtype=jnp.int32)
    u = u[None]
    a_val, a_gate = u[..., :C], u[..., C:2 * C]
    q = u[..., 2 * C:2 * C + A].reshape(1, s, H, HEAD_DIM)
    k = u[..., 2 * C + A:2 * C + 2 * A].reshape(1, s, H, HEAD_DIM)
    v = u[..., 2 * C + 2 * A:].reshape(1, s, H, HEAD_DIM)
    a = a_val * jax.nn.sigmoid(a_gate)
    a = lax.conv_general_dilated(
        a, conv_w[:, None, :], window_strides=(1,), padding=[(CONV_WIDTH - 1, 0)],
        dimension_numbers=('NWC', 'WIO', 'NWC'), feature_group_count=C) + conv_b
    a = jax.nn.silu(_layernorm(a, cn_g, cn_b))
    q = _rope(_rmsnorm(q, qn_g), pos) * (HEAD_DIM ** -0.5)
    k = _rope(_rmsnorm(k, kn_g), pos)
    outs, lses = [], []
    for window, dil in DIL_PATTERNS:
        o_i, l_i = _dilated_branch(q, k, v, dil, window // dil)
        outs.append(o_i)
        lses.append(l_i)
    wts = jax.nn.softmax(jnp.stack(lses, axis=0), axis=0)
    o = jnp.einsum('pbsh,pbshd->bshd', wts, jnp.stack(outs, axis=0)).reshape(1, s, A)
    return jnp.concatenate([a, o], axis=-1)[0]


_LEVELS = (128, 64, 32, 16, 8, 4, 2, 1)


def _chunk_cumsum(g, rows, reverse=False):
    C = g.shape[0]
    d = 1
    while d < C:
        if reverse:
            g = g + jnp.where(rows < C - d, pltpu.roll(g, C - d, 0), 0.0)
        else:
            g = g + jnp.where(rows >= d, pltpu.roll(g, d, 0), 0.0)
        d *= 2
    return g


def _level_ref(b, b_scr, rows, m):
    C = b.shape[0]
    if m >= 8:
        pieces = [jnp.broadcast_to(b_scr[2 * m * j + m - 1:2 * m * j + m, :], (2 * m, LANE)) for j in range(C // (2 * m))]
        return pieces[0] if len(pieces) == 1 else jnp.concatenate(pieces, axis=0)
    pos = rows & (2 * m - 1)
    ref = b
    for p in range(2 * m):
        if p != m - 1:
            ref = jnp.where(pos == p, pltpu.roll(b, (p - (m - 1)) % C, 0), ref)
    return ref


def _level_operands(q, k, b, b_scr, rows, m):
    ref = _level_ref(b, b_scr, rows, m)
    qs = (q * jnp.exp(jnp.minimum(b - ref, 0.0))).astype(BF16)
    ks = (k * jnp.exp(jnp.minimum(ref - b, 0.0))).astype(BF16)
    return qs, ks


def _split2(x):
    hi = x.astype(BF16)
    lo = (x - hi.astype(F32)).astype(BF16)
    return jnp.concatenate([hi, lo], axis=1)


def _level_mask(tt, ss, m):
    x = tt ^ ss
    return (tt > ss) & (x >= m) & (x < 2 * m)


def _hgrn_gates(qz, fz, la, lc, oml):
    sq = jax.nn.sigmoid(qz)
    q = qz * sq
    s = jax.nn.sigmoid(fz)
    c = lc + jnp.minimum(fz, 0.0) - jnp.log(1.0 + jnp.exp(-jnp.abs(fz)))
    mx = jnp.maximum(la, c)
    g = mx + jnp.log(1.0 + jnp.exp(-jnp.abs(la - c)))
    k = oml * (1.0 - s)
    return q, sq, k, s, g, c


def _hgrn_fwd(u, la, lc, oml, gn_g, name):
    S = u.shape[0]
    W = u.shape[1] // 4
    H = W // HGRN_KDIM
    C = min(HGRN_CHUNK, S)
    nc = S // C
    levels = [m for m in _LEVELS if m < C]

    def body(qz_ref, fz_ref, iz_ref, gz_ref, la_ref, lc_ref, oml_ref, gn_ref,
             z_ref, o_ref, a_ref, st_ref, state, b_scr):
        @pl.when(pl.program_id(1) == 0)
        def _():
            state[...] = jnp.zeros_like(state)

        rows = lax.broadcasted_iota(jnp.int32, (C, LANE), 0)
        tt = lax.broadcasted_iota(jnp.int32, (C, C), 0)
        ss = lax.broadcasted_iota(jnp.int32, (C, C), 1)
        q, _, k, _, g, _ = _hgrn_gates(qz_ref[...], fz_ref[...], la_ref[...], lc_ref[...], oml_ref[...])
        v = iz_ref[...].astype(BF16)
        b = _chunk_cumsum(g, rows)
        b_scr[...] = b
        a = jnp.where(tt == ss, jnp.sum(q * k, axis=-1, keepdims=True), 0.0)
        for m in levels:
            qs, ks = _level_operands(q, k, b, b_scr, rows, m)
            a = jnp.where(_level_mask(tt, ss, m), _dot(qs, ks, "nt"), a)
        ab = a.astype(BF16)
        a_ref[...] = ab
        st = state[...]
        st_ref[...] = st
        o = _dot(ab, v, "nn") + _dot((q * jnp.exp(b)).astype(BF16), st.astype(BF16), "nt")
        bl = b_scr[C - 1:C, :]
        kh = (k * jnp.exp(bl - b)).astype(BF16)
        state[...] = st * jnp.exp(bl) + _dot(v, kh, "tn")
        o_ref[...] = o
        r = lax.rsqrt(jnp.mean(o * o, axis=-1, keepdims=True) + EPS)
        gz = gz_ref[...]
        z_ref[...] = (o * r * gn_ref[...] * (gz * jax.nn.sigmoid(gz))).astype(BF16)

    def col(off):
        return pl.BlockSpec((C, LANE), lambda h, i: (i, off * H + h))

    vec = pl.BlockSpec((1, LANE), lambda h, i: (0, h))
    tile = pl.BlockSpec((C, LANE), lambda h, i: (i, h))
    return pl.pallas_call(
        body, name=name, grid=(H, nc),
        in_specs=[col(0), col(1), col(2), col(3), vec, vec, vec, vec],
        out_specs=[tile, tile, pl.BlockSpec((None, C, C), lambda h, i: (h, i, 0)),
                   pl.BlockSpec((None, None, LANE, LANE), lambda h, i: (h, i, 0, 0))],
        out_shape=[jax.ShapeDtypeStruct((S, W), BF16), jax.ShapeDtypeStruct((S, W), F32),
                   jax.ShapeDtypeStruct((H, S, C), BF16), jax.ShapeDtypeStruct((H, nc, LANE, LANE), F32)],
        scratch_shapes=[pltpu.VMEM((LANE, LANE), F32), pltpu.VMEM((C, LANE), F32)],
        compiler_params=_params("parallel", "arbitrary"),
    )(u, u, u, u, la, lc, oml, gn_g)


def _hgrn_bwd(u, la, lc, oml, gn_g, o, a, st, dz, name):
    S = u.shape[0]
    W = u.shape[1] // 4
    H = W // HGRN_KDIM
    C = min(HGRN_CHUNK, S)
    nc = S // C
    levels = [m for m in _LEVELS if m < C]

    def body(qz_ref, fz_ref, iz_ref, gz_ref, la_ref, lc_ref, oml_ref, gn_ref, o_ref, a_ref, st_ref, dz_ref,
             dqz_ref, dfz_ref, diz_ref, dgz_ref, dla_ref, dlc_ref, doml_ref, dgn_ref, dstate, b_scr):
        @pl.when(pl.program_id(1) == 0)
        def _():
            dstate[...] = jnp.zeros_like(dstate)
            dla_ref[...] = jnp.zeros_like(dla_ref)
            dlc_ref[...] = jnp.zeros_like(dlc_ref)
            doml_ref[...] = jnp.zeros_like(doml_ref)
            dgn_ref[...] = jnp.zeros_like(dgn_ref)

        rows = lax.broadcasted_iota(jnp.int32, (C, LANE), 0)
        tt = lax.broadcasted_iota(jnp.int32, (C, C), 0)
        ss = lax.broadcasted_iota(jnp.int32, (C, C), 1)
        la_v, lc_v, oml_v = la_ref[...], lc_ref[...], oml_ref[...]
        qz, fz = qz_ref[...], fz_ref[...]
        q, sq, k, s, g, c = _hgrn_gates(qz, fz, la_v, lc_v, oml_v)
        vf = iz_ref[...]
        v = vf.astype(BF16)

        ov, gz, dzv, gn = o_ref[...], gz_ref[...], dz_ref[...], gn_ref[...]
        r = lax.rsqrt(jnp.mean(ov * ov, axis=-1, keepdims=True) + EPS)
        on = ov * r
        sg = jax.nn.sigmoid(gz)
        silu_g = gz * sg
        dgn_ref[...] += jnp.sum(dzv * on * silu_g, axis=0, keepdims=True)
        dgz_ref[...] = (dzv * on * gn * (sg * (1.0 + gz * (1.0 - sg)))).astype(BF16)
        don = dzv * gn * silu_g
        do_f = r * (don - on * jnp.mean(don * on, axis=-1, keepdims=True))
        do = do_f.astype(BF16)

        b = _chunk_cumsum(g, rows)
        b_scr[...] = b
        bl = b_scr[C - 1:C, :]
        e = jnp.exp(b)
        ebl = jnp.exp(bl)
        ekl = jnp.exp(bl - b)
        qh = q * e
        kh = k * ekl
        st_v = st_ref[...]
        dst = dstate[...]
        dstb = dst.astype(BF16)

        diz_ref[...] = (_dot(a_ref[...], do, "tn") + _dot(kh.astype(BF16), dstb, "nt")).astype(BF16)
        da = _dot(do, v, "nt")
        dqh = _dot(do, st_v.astype(BF16), "nn")
        dkh = _dot(v, dstb, "nn")
        dstate[...] = dst * ebl + _dot(do, qh.astype(BF16), "tn")
        dbl = jnp.sum(dkh * kh, axis=0, keepdims=True) + jnp.sum(dst * st_v, axis=0, keepdims=True) * ebl

        datt = jnp.sum(do_f * vf, axis=-1, keepdims=True)
        dqa = datt * k
        dka = datt * q
        for m in levels:
            ref = _level_ref(b, b_scr, rows, m)
            eu = jnp.exp(jnp.minimum(b - ref, 0.0))
            el = jnp.exp(jnp.minimum(ref - b, 0.0))
            gm = jnp.where(_level_mask(tt, ss, m), da, 0.0).astype(BF16)
            pq = _dot(gm, _split2(k * el), "nn")
            pk = _dot(gm, _split2(q * eu), "tn")
            dqa += (pq[:, :LANE] + pq[:, LANE:]) * eu
            dka += (pk[:, :LANE] + pk[:, LANE:]) * el
        db = q * dqa - k * dka + dqh * qh - dkh * kh
        db = db + jnp.where(rows == C - 1, dbl, 0.0)
        dq = dqa + dqh * e
        dk = dka + dkh * ekl
        dg = _chunk_cumsum(db, rows, reverse=True)

        wa = jnp.exp(la_v - g)
        wc = jnp.exp(c - g)
        dqz_ref[...] = (dq * (sq * (1.0 + qz * (1.0 - sq)))).astype(BF16)
        dfz_ref[...] = (dg * wc * (1.0 - s) - dk * oml_v * s * (1.0 - s)).astype(BF16)
        dla_ref[...] += jnp.sum(dg * wa, axis=0, keepdims=True)
        dlc_ref[...] += jnp.sum(dg * wc, axis=0, keepdims=True)
        doml_ref[...] += jnp.sum(dk * (1.0 - s), axis=0, keepdims=True)

    def col(off):
        return pl.BlockSpec((C, LANE), lambda h, i: (nc - 1 - i, off * H + h))

    vec = pl.BlockSpec((1, LANE), lambda h, i: (0, h))
    tile = pl.BlockSpec((C, LANE), lambda h, i: (nc - 1 - i, h))
    a_spec = pl.BlockSpec((None, C, C), lambda h, i: (h, nc - 1 - i, 0))
    st_spec = pl.BlockSpec((None, None, LANE, LANE), lambda h, i: (h, nc - 1 - i, 0, 0))
    sw = jax.ShapeDtypeStruct((S, W), BF16)
    vw = jax.ShapeDtypeStruct((1, W), F32)
    return pl.pallas_call(
        body, name=name, grid=(H, nc),
        in_specs=[col(0), col(1), col(2), col(3), vec, vec, vec, vec, tile, a_spec, st_spec, tile],
        out_specs=[tile, tile, tile, tile, vec, vec, vec, vec],
        out_shape=[sw, sw, sw, sw, vw, vw, vw, vw],
        scratch_shapes=[pltpu.VMEM((LANE, LANE), F32), pltpu.VMEM((C, LANE), F32)],
        compiler_params=_params("parallel", "arbitrary"),
    )(u, u, u, u, la, lc, oml, gn_g, o, a, st, dz)


def _lb_terms(lb_logits, layer):
    p = jax.nn.softmax(lb_logits, axis=0)
    lb = (jnp.cumsum(p, axis=0) - p[0:1])[layer]
    return jnp.log(lb)[None], jnp.log1p(-lb)[None], (1.0 - lb)[None]


def kernel(x, norm_ffn1, ffn1_wg, ffn1_wu, ffn1_wd, norm_mix, norm_ffn2, ffn2_wg, ffn2_wu, ffn2_wd, ev_w_in, ev_conv_w, ev_conv_b, ev_cn_g, ev_cn_b, ev_qn_g, ev_kn_g, ev_w_out, od_w_in, od_lb_logits, od_gn_g, od_w_out, loss_target, m_norm_ffn1, m_ffn1_wg, m_ffn1_wu, m_ffn1_wd, m_norm_mix, m_norm_ffn2, m_ffn2_wg, m_ffn2_wu, m_ffn2_wd, m_ev_w_in, m_ev_conv_w, m_ev_conv_b, m_ev_cn_g, m_ev_cn_b, m_ev_qn_g, m_ev_kn_g, m_ev_w_out, m_od_w_in, m_od_lb_logits, m_od_gn_g, m_od_w_out, v_norm_ffn1, v_ffn1_wg, v_ffn1_wu, v_ffn1_wd, v_norm_mix, v_norm_ffn2, v_ffn2_wg, v_ffn2_wu, v_ffn2_wd, v_ev_w_in, v_ev_conv_w, v_ev_conv_b, v_ev_cn_g, v_ev_cn_b, v_ev_qn_g, v_ev_kn_g, v_ev_w_out, v_od_w_in, v_od_lb_logits, v_od_gn_g, v_od_w_out):
    depth = norm_ffn1.shape[0]
    S, D = x.shape[1], x.shape[2]
    xi, yi, ci = _me()
    dev = 4 * xi + 2 * yi + ci
    c_idx = jnp.reshape(ci, (1,)).astype(jnp.int32)
    k_idx = jnp.reshape(2 * xi + yi, (1,)).astype(jnp.int32)

    def ffn_shard(wg, wu, wd, l):
        return jnp.stack([wg[l].T, wu[l].T, wd[l]]).astype(BF16)

    ffn_local = [ffn_shard(ffn1_wg, ffn1_wu, ffn1_wd, l) for l in range(depth)] \
        + [ffn_shard(ffn2_wg, ffn2_wu, ffn2_wd, l) for l in range(depth)]
    mix_local = [ev_w_in[0].T.astype(BF16)[None], ev_w_out[0].astype(BF16)[None],
                 od_w_in[0].T.astype(BF16)[None], od_w_out[0].astype(BF16)[None]]
    gathered = _all_gather(ffn_local + mix_local, "all_gather_weights")
    full = [g.reshape(g.shape[0], N_DEV * g.shape[2], g.shape[3]) for g in gathered]
    w_ffn1, w_ffn2 = full[:depth], full[depth:2 * depth]
    ev_w_in_t, ev_w_out_f, od_w_in_t, od_w_out_f = (f[0] for f in full[2 * depth:])

    conv_w_sh, gn_g_sh = ev_conv_w[0], od_gn_g[0]
    cw, cs = conv_w_sh.shape[0], conv_w_sh.shape[1]
    gs = gn_g_sh.shape[0]
    conv_w_z = lax.dynamic_update_slice(jnp.zeros((cw, N_DEV * cs), F32), conv_w_sh, (0, dev * cs))
    gn_g_z = lax.dynamic_update_slice(jnp.zeros((N_DEV * gs,), F32), gn_g_sh, (dev * gs,))
    conv_w_full, gn_g_full = _unpack_rows(
        _all_reduce_small(_pack_rows([conv_w_z, gn_g_z]), "gather_small_params"),
        [conv_w_z.shape, gn_g_z.shape])

    def ffn_forward(h, gain, w, tag):
        hn = _rms_fwd(h, gain[None], "rms_" + tag)
        out, gu = _ffn_fwd(h, hn, w, "ffn_fwd_" + tag)
        return out, (h, hn, gu)

    def even_mixer(u):
        z, vjp = jax.vjp(_even_core, u, conv_w_full, ev_conv_b[0], ev_cn_g[0], ev_cn_b[0], ev_qn_g[0], ev_kn_g[0])

        def backward(dz):
            du, *gs = vjp(dz)
            return du.astype(BF16), gs

        return z.astype(BF16), backward

    def odd_mixer(u, l):
        (la, lc, oml), lb_vjp = jax.vjp(functools.partial(_lb_terms, layer=l), od_lb_logits)
        gn = gn_g_full[None]
        zb, o_raw, scores, states = _hgrn_fwd(u, la, lc, oml, gn, f"hgrn_fwd{l}")

        def backward(dz):
            dqz, dfz, diz, dgz, dla, dlc, doml, dgn = _hgrn_bwd(
                u, la, lc, oml, gn, o_raw, scores, states, dz, f"hgrn_bwd{l}")
            (g_lb,) = lb_vjp((dla, dlc, doml))
            return jnp.concatenate([dqz, dfz, diz, dgz], axis=1), [g_lb, dgn[0]]

        return zb, backward

    saved = []
    h = x[0]
    for l in range(depth):
        h, s1 = ffn_forward(h, norm_ffn1[l], w_ffn1[l], f"a{l}")
        hn = _rms_fwd(h, norm_mix[l][None], f"rms_mix{l}")
        if l % 2 == 0:
            u = _mm(hn, ev_w_in_t, "nt", F32, f"mix_in{l}")
            zb, core_vjp = even_mixer(u)
            w_out = ev_w_out_f
        else:
            u = _mm(hn, od_w_in_t, "nt", F32, f"mix_in{l}")
            zb, core_vjp = odd_mixer(u, l)
            w_out = od_w_out_f
        h_mix = h
        h = _mm(zb, w_out, "nn", F32, f"mix_out{l}", res=h)
        sm = (h_mix, hn, zb, core_vjp)
        h, s2 = ffn_forward(h, norm_ffn2[l], w_ffn2[l], f"b{l}")
        saved.append((s1, sm, s2))

    dy, loss_part = _loss_grad(h, loss_target[0], "loss_grad")

    def ffn_backward(dy, gain, w, sv, tag):
        h_in, hn, gu = sv
        dxn, dout, t = _ffn_bwd_dx(dy, w, gu, "ffn_bwd_dx_" + tag)
        dw = _ffn_bwd_dw(hn, dout, t, "ffn_bwd_dw_" + tag)
        dx, dgain = _rms_bwd(h_in, gain[None], dxn, dy, "rms_bwd_" + tag)
        return dx, dgain[0], dw

    g_norm1, g_norm2, g_normm = [None] * depth, [None] * depth, [None] * depth
    dw_ffn1, dw_ffn2 = [None] * depth, [None] * depth
    small, dw_mix = [None, None], [None, None]
    for l in reversed(range(depth)):
        s1, (h_mix, hn, zb, core_vjp), s2 = saved[l]
        dy, g_norm2[l], dw_ffn2[l] = ffn_backward(dy, norm_ffn2[l], w_ffn2[l], s2, f"b{l}")
        dyb = dy.astype(BF16)
        if l % 2 == 0:
            w_out, w_in_t = ev_w_out_f, ev_w_in_t
        else:
            w_out, w_in_t = od_w_out_f, od_w_in_t
        dz = _mm(dyb, w_out, "nt", F32, f"mix_out_dz{l}")
        dw_out = _mm(zb, dyb, "tn", BF16, f"mix_out_dw{l}")
        dub, small[l % 2] = core_vjp(dz)
        dw_mix[l % 2] = [_mm(dub, hn, "tn", BF16, f"mix_in_dw{l}"), dw_out]
        dhn = _mm(dub, w_in_t, "nn", F32, f"mix_in_dx{l}")
        dy, gm = _rms_bwd(h_mix, norm_mix[l][None], dhn, dy, f"rms_bwd_mix{l}")
        g_normm[l] = gm[0]
        dy, g_norm1[l], dw_ffn1[l] = ffn_backward(dy, norm_ffn1[l], w_ffn1[l], s1, f"a{l}")
    grad_x = dy[None]

    partial = dw_ffn1 + dw_ffn2 + [g[None] for g in dw_mix[0] + dw_mix[1]]
    partial = [g.reshape(g.shape[0], 4, 2, g.shape[1] // N_DEV, g.shape[2]) for g in partial]
    got = _exchange_core_halves(partial, "reduce_core_halves")
    sums = [_add_core_halves(g, r, c_idx, f"add_core_halves{a}") for a, (g, r) in enumerate(zip(partial, got))]
    got = _exchange_chip_blocks(sums, "reduce_chip_blocks")
    shard_g = [_sum_chip_blocks(s, r, k_idx, f"sum_chip_blocks{a}") for a, (s, r) in enumerate(zip(sums, got))]
    g_ffn1, g_ffn2 = shard_g[:depth], shard_g[depth:2 * depth]
    g_ev_in_t, g_ev_out, g_od_in_t, g_od_out = shard_g[2 * depth:]

    g_conv_w, g_conv_b, g_cn_g, g_cn_b, g_qn_g, g_kn_g = small[0]
    g_lb, g_gn = small[1]
    parts = [jnp.stack(g_norm1), jnp.stack(g_normm), jnp.stack(g_norm2), g_conv_b, g_cn_g, g_cn_b,
             g_qn_g, g_kn_g, g_lb, g_conv_w, g_gn, loss_part[0, :1]]
    red = _unpack_rows(_all_reduce_small(_pack_rows(parts), "reduce_small_grads"), [p.shape for p in parts])
    g_norm1, g_normm, g_norm2, g_conv_b, g_cn_g, g_cn_b, g_qn_g, g_kn_g, g_lb, g_conv_w, g_gn, loss = red
    g_conv_w = lax.dynamic_slice(g_conv_w, (0, dev * cs), (cw, cs))
    g_gn = lax.dynamic_slice(g_gn, (dev * gs,), (gs,))

    def ffn_grads(gl):
        return (jnp.stack([g[0].T for g in gl]), jnp.stack([g[1].T for g in gl]), jnp.stack([g[2] for g in gl]))

    g_ffn1_wg, g_ffn1_wu, g_ffn1_wd = ffn_grads(g_ffn1)
    g_ffn2_wg, g_ffn2_wu, g_ffn2_wd = ffn_grads(g_ffn2)
    grads = [g_norm1, g_ffn1_wg, g_ffn1_wu, g_ffn1_wd, g_normm, g_norm2, g_ffn2_wg, g_ffn2_wu, g_ffn2_wd,
             g_ev_in_t[0].T[None], g_conv_w[None], g_conv_b[None], g_cn_g[None], g_cn_b[None], g_qn_g[None],
             g_kn_g[None], g_ev_out, g_od_in_t[0].T[None], g_lb, g_gn[None], g_od_out]
    weights = [norm_ffn1, ffn1_wg, ffn1_wu, ffn1_wd, norm_mix, norm_ffn2, ffn2_wg, ffn2_wu, ffn2_wd, ev_w_in,
               ev_conv_w, ev_conv_b, ev_cn_g, ev_cn_b, ev_qn_g, ev_kn_g, ev_w_out, od_w_in, od_lb_logits,
               od_gn_g, od_w_out]
    moms = [m_norm_ffn1, m_ffn1_wg, m_ffn1_wu, m_ffn1_wd, m_norm_mix, m_norm_ffn2, m_ffn2_wg, m_ffn2_wu,
            m_ffn2_wd, m_ev_w_in, m_ev_conv_w, m_ev_conv_b, m_ev_cn_g, m_ev_cn_b, m_ev_qn_g, m_ev_kn_g,
            m_ev_w_out, m_od_w_in, m_od_lb_logits, m_od_gn_g, m_od_w_out]
    vars_ = [v_norm_ffn1, v_ffn1_wg, v_ffn1_wu, v_ffn1_wd, v_norm_mix, v_norm_ffn2, v_ffn2_wg, v_ffn2_wu,
             v_ffn2_wd, v_ev_w_in, v_ev_conv_w, v_ev_conv_b, v_ev_cn_g, v_ev_cn_b, v_ev_qn_g, v_ev_kn_g,
             v_ev_w_out, v_od_w_in, v_od_lb_logits, v_od_gn_g, v_od_w_out]
    deltas, new_m, new_v = [], [], []
    for i, (w, g, m, v) in enumerate(zip(weights, grads, moms, vars_)):
        d, nm, nv = _adamw(w, g, m, v, f"adamw{i}")
        deltas.append(d)
        new_m.append(nm)
        new_v.append(nv)
    return (loss[0], grad_x, *grads, *deltas, *new_m, *new_v)
```

```python
import functools
import math

import jax
import jax.numpy as jnp
from jax import lax
from jax.experimental import pallas as pl
from jax.experimental.pallas import tpu as pltpu

F32 = jnp.float32
BF16 = jnp.bfloat16
MESH = pl.DeviceIdType.MESH
N_DEV = 8

EPS = 1e-6
HEAD_DIM = 128
CONV_WIDTH = 31
DIL_PATTERNS = ((128, 1), (512, 4), (2048, 16))
Q_BLOCK = 128
ROPE_THETA = 10000.0
HGRN_KDIM = 128
HGRN_CHUNK = 256

ADAM_LR = 0.001
ADAM_B1 = 0.9
ADAM_B2 = 0.999
ADAM_EPS = 1e-08
ADAM_WD = 0.01
ADAM_STEP = 10

VMEM_LIMIT_BYTES = 56 * 1024 * 1024
LANE = 128
SUBLANE_BF16 = 16

ANY = pl.BlockSpec(memory_space=pl.ANY)


def _tile(n, pref, mult):
    t = (min(pref, n) // mult) * mult
    while t > 0:
        if n % t == 0:
            return t
        t -= mult
    return n


def _params(*sem):
    return pltpu.CompilerParams(dimension_semantics=sem, vmem_limit_bytes=VMEM_LIMIT_BYTES)


_DOT_DIMS = {
    "nn": (((1,), (0,)), ((), ())),
    "nt": (((1,), (1,)), ((), ())),
    "tn": (((0,), (0,)), ((), ())),
}


def _dot(a, b, mode):
    return lax.dot_general(a, b, _DOT_DIMS[mode], preferred_element_type=F32)


def _mm(a, b, mode, out_dtype, name, res=None, tm=1024, tn=1024, tk=2048):
    if mode == "nt":
        (M, K), N = a.shape, b.shape[0]
    elif mode == "nn":
        (M, K), N = a.shape, b.shape[1]
    else:
        (K, M), N = a.shape, b.shape[1]
    tm, tn, tk = _tile(M, tm, LANE), _tile(N, tn, LANE), _tile(K, tk, LANE)
    nk = K // tk

    def body(*refs):
        if res is None:
            a_ref, b_ref, o_ref, acc = refs
        else:
            a_ref, b_ref, r_ref, o_ref, acc = refs
        k = pl.program_id(2)

        @pl.when(k == 0)
        def _():
            acc[...] = jnp.zeros_like(acc)

        acc[...] += _dot(a_ref[...].astype(BF16), b_ref[...].astype(BF16), mode)

        @pl.when(k == nk - 1)
        def _():
            r = acc[...]
            if res is not None:
                r = r_ref[...] + r
            o_ref[...] = r.astype(out_dtype)

    a_spec = {"nt": pl.BlockSpec((tm, tk), lambda i, j, k: (i, k)),
              "nn": pl.BlockSpec((tm, tk), lambda i, j, k: (i, k)),
              "tn": pl.BlockSpec((tk, tm), lambda i, j, k: (k, i))}[mode]
    b_spec = {"nt": pl.BlockSpec((tn, tk), lambda i, j, k: (j, k)),
              "nn": pl.BlockSpec((tk, tn), lambda i, j, k: (k, j)),
              "tn": pl.BlockSpec((tk, tn), lambda i, j, k: (k, j))}[mode]
    o_spec = pl.BlockSpec((tm, tn), lambda i, j, k: (i, j))
    in_specs = [a_spec, b_spec] + ([o_spec] if res is not None else [])
    args = (a, b) + ((res,) if res is not None else ())
    return pl.pallas_call(
        body, name=name, grid=(M // tm, N // tn, nk),
        in_specs=in_specs, out_specs=o_spec,
        out_shape=jax.ShapeDtypeStruct((M, N), out_dtype),
        scratch_shapes=[pltpu.VMEM((tm, tn), F32)],
        compiler_params=_params("parallel", "parallel", "arbitrary"),
    )(*args)


def _rms_fwd(x, gain, name):
    S, D = x.shape
    tm = _tile(S, 512, SUBLANE_BF16)

    def body(x_ref, g_ref, o_ref):
        xv = x_ref[...]
        r = lax.rsqrt(jnp.mean(xv * xv, axis=-1, keepdims=True) + EPS)
        o_ref[...] = (xv * r * g_ref[...]).astype(BF16)

    return pl.pallas_call(
        body, name=name, grid=(S // tm,),
        in_specs=[pl.BlockSpec((tm, D), lambda i: (i, 0)), pl.BlockSpec((1, D), lambda i: (0, 0))],
        out_specs=pl.BlockSpec((tm, D), lambda i: (i, 0)),
        out_shape=jax.ShapeDtypeStruct((S, D), BF16),
        compiler_params=_params("parallel"),
    )(x, gain)


def _rms_bwd(x, gain, dxn, dy, name):
    S, D = x.shape
    tm = _tile(S, 512, 8)

    def body(x_ref, g_ref, dxn_ref, dy_ref, dx_ref, dg_ref):
        @pl.when(pl.program_id(0) == 0)
        def _():
            dg_ref[...] = jnp.zeros_like(dg_ref)

        xv = x_ref[...]
        r = lax.rsqrt(jnp.mean(xv * xv, axis=-1, keepdims=True) + EPS)
        xh = xv * r
        dxn_v = dxn_ref[...]
        dg_ref[...] += jnp.sum(dxn_v * xh, axis=0, keepdims=True)
        dxh = dxn_v * g_ref[...]
        dx_ref[...] = dy_ref[...] + r * (dxh - xh * jnp.mean(dxh * xh, axis=-1, keepdims=True))

    row = pl.BlockSpec((tm, D), lambda i: (i, 0))
    vec = pl.BlockSpec((1, D), lambda i: (0, 0))
    return pl.pallas_call(
        body, name=name, grid=(S // tm,),
        in_specs=[row, vec, row, row], out_specs=[row, vec],
        out_shape=[jax.ShapeDtypeStruct((S, D), F32), jax.ShapeDtypeStruct((1, D), F32)],
        compiler_params=_params("arbitrary"),
    )(x, gain, dxn, dy)


def _ffn_fwd(x, xn, w, name):
    S, D = x.shape
    F = w.shape[1]
    tm, tf = _tile(S, 512, SUBLANE_BF16), _tile(F, 512, LANE)
    nf = F // tf

    def body(x_ref, xn_ref, w_ref, o_ref, gu_ref, acc):
        f = pl.program_id(1)

        @pl.when(f == 0)
        def _():
            acc[...] = jnp.zeros_like(acc)

        xnv = xn_ref[...]
        g = _dot(xnv, w_ref[0], "nt")
        u = _dot(xnv, w_ref[1], "nt")
        gu_ref[0] = g.astype(BF16)
        gu_ref[1] = u.astype(BF16)
        h = (g * jax.nn.sigmoid(g) * u).astype(BF16)
        acc[...] += _dot(h, w_ref[2], "nn")

        @pl.when(f == nf - 1)
        def _():
            o_ref[...] = x_ref[...] + 0.5 * acc[...]

    row = pl.BlockSpec((tm, D), lambda i, f: (i, 0))
    return pl.pallas_call(
        body, name=name, grid=(S // tm, nf),
        in_specs=[row, row, pl.BlockSpec((3, tf, D), lambda i, f: (0, f, 0))],
        out_specs=[row, pl.BlockSpec((2, tm, tf), lambda i, f: (0, i, f))],
        out_shape=[jax.ShapeDtypeStruct((S, D), F32), jax.ShapeDtypeStruct((2, S, F), BF16)],
        scratch_shapes=[pltpu.VMEM((tm, D), F32)],
        compiler_params=_params("parallel", "arbitrary"),
    )(x, xn, w)


def _ffn_bwd_dx(dy, w, gu, name):
    S, D = dy.shape
    F = w.shape[1]
    tm, tf = _tile(S, 512, SUBLANE_BF16), _tile(F, 512, LANE)
    nf = F // tf

    def body(dy_ref, w_ref, gu_ref, dxn_ref, dout_ref, t_ref, acc):
        f = pl.program_id(1)

        @pl.when(f == 0)
        def _():
            acc[...] = jnp.zeros_like(acc)
            dout_ref[...] = (0.5 * dy_ref[...]).astype(BF16)

        dh = _dot(dout_ref[...], w_ref[2], "nt")
        g = gu_ref[0].astype(F32)
        u = gu_ref[1].astype(F32)
        sig = jax.nn.sigmoid(g)
        silu = g * sig
        dg = (dh * u * (sig * (1.0 + g * (1.0 - sig)))).astype(BF16)
        du = (dh * silu).astype(BF16)
        t_ref[0] = dg
        t_ref[1] = du
        t_ref[2] = (silu * u).astype(BF16)
        acc[...] += _dot(dg, w_ref[0], "nn") + _dot(du, w_ref[1], "nn")

        @pl.when(f == nf - 1)
        def _():
            dxn_ref[...] = acc[...]

    row = pl.BlockSpec((tm, D), lambda i, f: (i, 0))
    return pl.pallas_call(
        body, name=name, grid=(S // tm, nf),
        in_specs=[row, pl.BlockSpec((3, tf, D), lambda i, f: (0, f, 0)),
                  pl.BlockSpec((2, tm, tf), lambda i, f: (0, i, f))],
        out_specs=[row, row, pl.BlockSpec((3, tm, tf), lambda i, f: (0, i, f))],
        out_shape=[jax.ShapeDtypeStruct((S, D), F32), jax.ShapeDtypeStruct((S, D), BF16),
                   jax.ShapeDtypeStruct((3, S, F), BF16)],
        scratch_shapes=[pltpu.VMEM((tm, D), F32)],
        compiler_params=_params("parallel", "arbitrary"),
    )(dy, w, gu)


def _ffn_bwd_dw(xn, dout, t, name):
    S, D = xn.shape
    F = t.shape[2]
    ts, tf = _tile(S, 512, LANE), _tile(F, 512, LANE)
    ns = S // ts

    def body(xn_ref, dout_ref, t_ref, dw_ref, acc):
        s = pl.program_id(1)

        @pl.when(s == 0)
        def _():
            acc[...] = jnp.zeros_like(acc)

        xnv = xn_ref[...]
        acc[0] += _dot(t_ref[0], xnv, "tn")
        acc[1] += _dot(t_ref[1], xnv, "tn")
        acc[2] += _dot(t_ref[2], dout_ref[...], "tn")

        @pl.when(s == ns - 1)
        def _():
            dw_ref[...] = acc[...].astype(BF16)

    row = pl.BlockSpec((ts, D), lambda f, s: (s, 0))
    return pl.pallas_call(
        body, name=name, grid=(F // tf, ns),
        in_specs=[row, row, pl.BlockSpec((3, ts, tf), lambda f, s: (0, s, f))],
        out_specs=pl.BlockSpec((3, tf, D), lambda f, s: (0, f, 0)),
        out_shape=jax.ShapeDtypeStruct((3, F, D), BF16),
        scratch_shapes=[pltpu.VMEM((3, tf, D), F32)],
        compiler_params=_params("parallel", "arbitrary"),
    )(xn, dout, t)


def _loss_grad(y, target, name):
    S, D = y.shape
    tm = _tile(S, 512, 8)

    def body(y_ref, t_ref, dy_ref, l_ref):
        @pl.when(pl.program_id(0) == 0)
        def _():
            l_ref[...] = jnp.zeros_like(l_ref)

        e = y_ref[...] - t_ref[...]
        dy_ref[...] = e * (1.0 / D)
        l_ref[...] += 0.5 * jnp.sum(jnp.sum(e * e, axis=-1, keepdims=True) * (1.0 / D))

    row = pl.BlockSpec((tm, D), lambda i: (i, 0))
    one = pl.BlockSpec((8, LANE), lambda i: (0, 0))
    return pl.pallas_call(
        body, name=name, grid=(S // tm,),
        in_specs=[row, row], out_specs=[row, one],
        out_shape=[jax.ShapeDtypeStruct((S, D), F32), jax.ShapeDtypeStruct((8, LANE), F32)],
        compiler_params=_params("arbitrary"),
    )(y, target)


def _adamw(w, g, m, v, name):
    shape = w.shape
    C = shape[-1]
    R = math.prod(shape[:-1])
    tr = _tile(R, max(8, (1 << 19) // C // 8 * 8), 8)
    c1 = 1.0 / (1.0 - ADAM_B1 ** ADAM_STEP)
    c2 = 1.0 / (1.0 - ADAM_B2 ** ADAM_STEP)

    def body(w_ref, g_ref, m_ref, v_ref, d_ref, nm_ref, nv_ref):
        gv = g_ref[...]
        nm = ADAM_B1 * m_ref[...] + (1.0 - ADAM_B1) * gv
        nv = ADAM_B2 * v_ref[...] + (1.0 - ADAM_B2) * (gv * gv)
        nm_ref[...] = nm
        nv_ref[...] = nv
        d_ref[...] = -ADAM_LR * ((nm * c1) / (jnp.sqrt(nv * c2) + ADAM_EPS) + ADAM_WD * w_ref[...])

    blk = pl.BlockSpec((tr, C), lambda i: (i, 0))
    sds = jax.ShapeDtypeStruct((R, C), F32)
    outs = pl.pallas_call(
        body, name=name, grid=(R // tr,),
        in_specs=[blk] * 4, out_specs=[blk] * 3, out_shape=[sds] * 3,
        compiler_params=_params("parallel"),
    )(*(a.reshape(R, C) for a in (w, g, m, v)))
    return tuple(o.reshape(shape) for o in outs)


def _me():
    return lax.axis_index("x"), lax.axis_index("y"), lax.axis_index("c")


def _all_gather(shards, name):
    na = len(shards)

    def body(*refs):
        ins, outs = refs[:na], refs[na:2 * na]
        send_sems, recv_sems, local_sems = refs[2 * na:]
        x, y, c = _me()
        me, sibling = (x, y, c), (x, y, 1 - c)
        chips = [(1 - x, y), (x, 1 - y), (1 - x, 1 - y)]

        def blk(a, p):
            return outs[a].at[:, 4 * p[0] + 2 * p[1] + p[2]]

        def copy(a, k, block, to, src=None):
            return pltpu.make_async_remote_copy(
                src_ref=blk(a, block) if src is None else src, dst_ref=blk(a, block),
                send_sem=send_sems.at[a * 7 + k], recv_sem=recv_sems.at[a * 7 + k],
                device_id=to, device_id_type=MESH)

        mine, first, passed = [], [], []
        for a in range(na):
            mine.append(pltpu.make_async_copy(ins[a], blk(a, me), local_sems.at[a]))
            mine[a].start()
            first.append([copy(a, 0, me, sibling, src=ins[a])]
                         + [copy(a, 1 + j, me, (*chip, c), src=ins[a]) for j, chip in enumerate(chips)])
            for cp in first[a]:
                cp.start()
        for a in range(na):
            passed.append([copy(a, 4 + j, (*chip, c), sibling) for j, chip in enumerate(chips)])
            for j, chip in enumerate(chips):
                copy(a, 1 + j, (*chip, c), me).wait_recv()
                passed[a][j].start()
        for a in range(na):
            copy(a, 0, sibling, me).wait_recv()
            for j, chip in enumerate(chips):
                copy(a, 4 + j, (*chip, 1 - c), me).wait_recv()
        for a in range(na):
            for cp in first[a] + passed[a]:
                cp.wait_send()
            mine[a].wait()

    return pl.pallas_call(
        body, name=name,
        in_specs=[ANY] * na, out_specs=[ANY] * na,
        out_shape=[jax.ShapeDtypeStruct((s.shape[0], N_DEV) + s.shape[1:], s.dtype) for s in shards],
        scratch_shapes=[pltpu.SemaphoreType.DMA((7 * na,)), pltpu.SemaphoreType.DMA((7 * na,)),
                        pltpu.SemaphoreType.DMA((na,))],
    )(*shards)


def _exchange_core_halves(grads, name):
    na = len(grads)

    def body(*refs):
        ins, outs = refs[:na], refs[na:2 * na]
        send_sems, recv_sems = refs[2 * na:]
        x, y, c = _me()
        copies = [pltpu.make_async_remote_copy(
            src_ref=ins[a].at[:, :, 1 - c], dst_ref=outs[a],
            send_sem=send_sems.at[a], recv_sem=recv_sems.at[a],
            device_id=(x, y, 1 - c), device_id_type=MESH) for a in range(na)]
        for cp in copies:
            cp.start()
        for cp in copies:
            cp.wait()

    return pl.pallas_call(
        body, name=name,
        in_specs=[ANY] * na, out_specs=[ANY] * na,
        out_shape=[jax.ShapeDtypeStruct(g.shape[:2] + g.shape[3:], g.dtype) for g in grads],
        scratch_shapes=[pltpu.SemaphoreType.DMA((na,)), pltpu.SemaphoreType.DMA((na,))],
    )(*grads)


def _add_core_halves(grad, got, c_idx, name):
    n, nk, _, r, C = grad.shape
    tr = _tile(r, 1024, SUBLANE_BF16)

    def body(c_ref, g_ref, r_ref, o_ref):
        o_ref[...] = (g_ref[...].astype(F32) + r_ref[...].astype(F32)).astype(BF16)

    return pl.pallas_call(
        body, name=name,
        grid_spec=pltpu.PrefetchScalarGridSpec(
            num_scalar_prefetch=1, grid=(n, nk, r // tr),
            in_specs=[pl.BlockSpec((None, None, None, tr, C), lambda i, k, t, c: (i, k, c[0], t, 0)),
                      pl.BlockSpec((None, None, tr, C), lambda i, k, t, c: (i, k, t, 0))],
            out_specs=pl.BlockSpec((None, None, tr, C), lambda i, k, t, c: (i, k, t, 0))),
        out_shape=jax.ShapeDtypeStruct((n, nk, r, C), BF16),
        compiler_params=_params("parallel", "parallel", "parallel"),
    )(c_idx, grad, got)


def _exchange_chip_blocks(sums, name):
    na = len(sums)

    def body(*refs):
        ins, outs = refs[:na], refs[na:2 * na]
        send_sems, recv_sems = refs[2 * na:]
        x, y, c = _me()
        chips = [(1 - x, y), (x, 1 - y), (1 - x, 1 - y)]
        copies = [pltpu.make_async_remote_copy(
            src_ref=ins[a].at[:, 2 * chip[0] + chip[1]], dst_ref=outs[a].at[j],
            send_sem=send_sems.at[3 * a + j], recv_sem=recv_sems.at[3 * a + j],
            device_id=(*chip, c), device_id_type=MESH)
            for a in range(na) for j, chip in enumerate(chips)]
        for cp in copies:
            cp.start()
        for cp in copies:
            cp.wait()

    return pl.pallas_call(
        body, name=name,
        in_specs=[ANY] * na, out_specs=[ANY] * na,
        out_shape=[jax.ShapeDtypeStruct((3, s.shape[0]) + s.shape[2:], s.dtype) for s in sums],
        scratch_shapes=[pltpu.SemaphoreType.DMA((3 * na,)), pltpu.SemaphoreType.DMA((3 * na,))],
    )(*sums)


def _sum_chip_blocks(sums, got, k_idx, name):
    n, _, r, C = sums.shape
    tr = _tile(r, 512, SUBLANE_BF16)

    def body(k_ref, s_ref, r_ref, o_ref):
        acc = s_ref[...].astype(F32)
        for j in range(3):
            acc = acc + r_ref[j].astype(F32)
        o_ref[...] = acc

    return pl.pallas_call(
        body, name=name,
        grid_spec=pltpu.PrefetchScalarGridSpec(
            num_scalar_prefetch=1, grid=(n, r // tr),
            in_specs=[pl.BlockSpec((None, None, tr, C), lambda i, t, k: (i, k[0], t, 0)),
                      pl.BlockSpec((3, None, tr, C), lambda i, t, k: (0, i, t, 0))],
            out_specs=pl.BlockSpec((None, tr, C), lambda i, t, k: (i, t, 0))),
        out_shape=jax.ShapeDtypeStruct((n, r, C), F32),
        compiler_params=_params("parallel", "parallel"),
    )(k_idx, sums, got)


def _all_reduce_small(v, name):
    R = v.shape[0]

    def body(v_ref, o_ref, buf, send_sems, recv_sems):
        x, y, c = _me()
        me = 4 * x + 2 * y + c
        buf[me] = v_ref[...]
        copies = []
        for k in range(1, N_DEV):
            peer = (x ^ (k >> 2), y ^ ((k >> 1) & 1), c ^ (k & 1))
            copies.append(pltpu.make_async_remote_copy(
                src_ref=v_ref, dst_ref=buf.at[me],
                send_sem=send_sems.at[k - 1], recv_sem=recv_sems.at[k - 1],
                device_id=peer, device_id_type=MESH))
        for cp in copies:
            cp.start()
        for cp in copies:
            cp.wait()
        acc = buf[0]
        for d in range(1, N_DEV):
            acc = acc + buf[d]
        o_ref[...] = acc

    vm = pl.BlockSpec(memory_space=pltpu.VMEM)
    return pl.pallas_call(
        body, name=name, in_specs=[vm], out_specs=vm,
        out_shape=jax.ShapeDtypeStruct((R, LANE), F32),
        scratch_shapes=[pltpu.VMEM((N_DEV, R, LANE), F32),
                        pltpu.SemaphoreType.DMA((N_DEV - 1,)), pltpu.SemaphoreType.DMA((N_DEV - 1,))],
        compiler_params=pltpu.CompilerParams(vmem_limit_bytes=VMEM_LIMIT_BYTES),
    )(v)


def _pack_rows(parts):
    flat = jnp.concatenate([p.reshape(-1).astype(F32) for p in parts])
    n = flat.shape[0]
    rows = -(-n // (8 * LANE)) * 8
    flat = jnp.pad(flat, (0, rows * LANE - n))
    return flat.reshape(rows, LANE)


def _unpack_rows(packed, shapes):
    flat = packed.reshape(-1)
    out, off = [], 0
    for s in shapes:
        n = math.prod(s)
        out.append(flat[off:off + n].reshape(s))
        off += n
    return out


CONV_HALO = 32


def _conv_fwd(u, conv_w, conv_b, cn_g, cn_b, name):
    S = u.shape[0]
    C = conv_w.shape[1]
    T = _tile(S, 256, CONV_HALO)
    hb = T // CONV_HALO

    def body(av_ref, ag_ref, pv_ref, pg_ref, w_ref, b_ref, g_ref, bb_ref, out_ref, y_ref, scr):
        i = pl.program_id(0)
        prev = pv_ref[...] * jax.nn.sigmoid(pg_ref[...])
        scr[0:CONV_HALO, :] = jnp.where(i > 0, prev, 0.0)
        scr[CONV_HALO:CONV_HALO + T, :] = av_ref[...] * jax.nn.sigmoid(ag_ref[...])
        acc = jnp.broadcast_to(b_ref[...], (T, C))
        for j in range(CONV_WIDTH):
            acc = acc + w_ref[j:j + 1, :] * scr[pl.ds(CONV_HALO - (CONV_WIDTH - 1) + j, T), :]
        y_ref[...] = acc
        mu = jnp.mean(acc, axis=-1, keepdims=True)
        xc = acc - mu
        var = jnp.mean(xc * xc, axis=-1, keepdims=True)
        ln = xc * lax.rsqrt(var + EPS) * g_ref[...] + bb_ref[...]
        out_ref[...] = (ln * jax.nn.sigmoid(ln)).astype(BF16)

    def cur(cb):
        return pl.BlockSpec((T, C), lambda i: (i, cb))

    def halo(cb):
        return pl.BlockSpec((CONV_HALO, C), lambda i: (jnp.maximum(i * hb - 1, 0), cb))

    vec = pl.BlockSpec((1, C), lambda i: (0, 0))
    return pl.pallas_call(
        body, name=name, grid=(S // T,),
        in_specs=[cur(0), cur(1), halo(0), halo(1), pl.BlockSpec((CONV_WIDTH, C), lambda i: (0, 0)), vec, vec, vec],
        out_specs=[pl.BlockSpec((T, C), lambda i: (i, 0))] * 2,
        out_shape=[jax.ShapeDtypeStruct((S, C), BF16), jax.ShapeDtypeStruct((S, C), F32)],
        scratch_shapes=[pltpu.VMEM((T + CONV_HALO, C), F32)],
        compiler_params=_params("parallel"),
    )(u, u, u, u, conv_w, conv_b, cn_g, cn_b)


def _conv_bwd_norm(dz, y, cn_g, cn_b, name):
    S, C = y.shape
    T = _tile(S, 256, 8)

    def body(dz_ref, y_ref, g_ref, bb_ref, dy_ref, dg_ref, db_ref):
        @pl.when(pl.program_id(0) == 0)
        def _():
            dg_ref[...] = jnp.zeros_like(dg_ref)
            db_ref[...] = jnp.zeros_like(db_ref)

        yv = y_ref[...]
        mu = jnp.mean(yv, axis=-1, keepdims=True)
        xc = yv - mu
        rstd = lax.rsqrt(jnp.mean(xc * xc, axis=-1, keepdims=True) + EPS)
        xh = xc * rstd
        ln = xh * g_ref[...] + bb_ref[...]
        sg = jax.nn.sigmoid(ln)
        dln = dz_ref[...] * (sg * (1.0 + ln * (1.0 - sg)))
        dg_ref[...] += jnp.sum(dln * xh, axis=0, keepdims=True)
        db_ref[...] += jnp.sum(dln, axis=0, keepdims=True)
        dxh = dln * g_ref[...]
        dy_ref[...] = rstd * (dxh - jnp.mean(dxh, axis=-1, keepdims=True)
                              - xh * jnp.mean(dxh * xh, axis=-1, keepdims=True))

    row = pl.BlockSpec((T, C), lambda i: (i, 0))
    vec = pl.BlockSpec((1, C), lambda i: (0, 0))
    return pl.pallas_call(
        body, name=name, grid=(S // T,),
        in_specs=[row, row, vec, vec], out_specs=[row, vec, vec],
        out_shape=[jax.ShapeDtypeStruct((S, C), F32), jax.ShapeDtypeStruct((1, C), F32),
                   jax.ShapeDtypeStruct((1, C), F32)],
        compiler_params=_params("arbitrary"),
    )(dz, y, cn_g, cn_b)


def _conv_bwd_taps(u, dy, conv_w, name):
    S, C = dy.shape
    T = _tile(S, 256, CONV_HALO)
    hb = T // CONV_HALO
    nt = S // T
    W1 = CONV_WIDTH - 1

    def body(av_ref, ag_ref, pv_ref, pg_ref, dy_ref, dn_ref, w_ref, dv_ref, dg_ref, dw_ref, db_ref, a_scr, d_scr):
        i = pl.program_id(0)

        @pl.when(i == 0)
        def _():
            dw_ref[...] = jnp.zeros_like(dw_ref)
            db_ref[...] = jnp.zeros_like(db_ref)

        av, sg = av_ref[...], jax.nn.sigmoid(ag_ref[...])
        prev = pv_ref[...] * jax.nn.sigmoid(pg_ref[...])
        a_scr[0:CONV_HALO, :] = jnp.where(i > 0, prev, 0.0)
        a_scr[CONV_HALO:CONV_HALO + T, :] = av * sg
        dyv = dy_ref[...]
        d_scr[0:T, :] = dyv
        d_scr[T:T + CONV_HALO, :] = jnp.where(i < nt - 1, dn_ref[...], 0.0)
        da = jnp.zeros((T, C), F32)
        for j in range(CONV_WIDTH):
            da = da + w_ref[j:j + 1, :] * d_scr[pl.ds(W1 - j, T), :]
            dw_ref[j:j + 1, :] += jnp.sum(dyv * a_scr[pl.ds(CONV_HALO - W1 + j, T), :], axis=0, keepdims=True)
        db_ref[...] += jnp.sum(dyv, axis=0, keepdims=True)
        dv_ref[...] = (da * sg).astype(BF16)
        dg_ref[...] = (da * av * sg * (1.0 - sg)).astype(BF16)

    def cur(cb):
        return pl.BlockSpec((T, C), lambda i: (i, cb))

    def halo(cb):
        return pl.BlockSpec((CONV_HALO, C), lambda i: (jnp.maximum(i * hb - 1, 0), cb))

    nxt = pl.BlockSpec((CONV_HALO, C), lambda i: (jnp.minimum((i + 1) * hb, S // CONV_HALO - 1), 0))
    row = pl.BlockSpec((T, C), lambda i: (i, 0))
    return pl.pallas_call(
        body, name=name, grid=(nt,),
        in_specs=[cur(0), cur(1), halo(0), halo(1), row, nxt, pl.BlockSpec((CONV_WIDTH, C), lambda i: (0, 0))],
        out_specs=[row, row, pl.BlockSpec((CONV_HALO, C), lambda i: (0, 0)), pl.BlockSpec((1, C), lambda i: (0, 0))],
        out_shape=[jax.ShapeDtypeStruct((S, C), BF16), jax.ShapeDtypeStruct((S, C), BF16),
                   jax.ShapeDtypeStruct((CONV_HALO, C), F32), jax.ShapeDtypeStruct((1, C), F32)],
        scratch_shapes=[pltpu.VMEM((T + CONV_HALO, C), F32), pltpu.VMEM((T + CONV_HALO, C), F32)],
        compiler_params=_params("arbitrary"),
    )(u, u, u, u, dy, dy, conv_w)


def _rope_tables(S):
    half = HEAD_DIM // 2
    inv = jnp.exp(-math.log(ROPE_THETA) * jnp.arange(half, dtype=F32) / half)
    ang = jnp.arange(S, dtype=jnp.int32).astype(F32)[:, None] * inv[None, :]
    cos, sin = jnp.cos(ang), jnp.sin(ang)
    return jnp.concatenate([cos, cos], axis=1), jnp.concatenate([-sin, sin], axis=1)


def _qkv_prep(u, qn_g, kn_g, cos, sin, cb0, name):
    S = u.shape[0]
    A = (u.shape[1] // (cb0 + 3))
    H = A // HEAD_DIM
    T = _tile(S, 256, SUBLANE_BF16)
    scale = HEAD_DIM ** -0.5

    def body(q_ref, k_ref, v_ref, qg_ref, kg_ref, cos_ref, sin_ref, qo_ref, ko_ref, vo_ref):
        cosv, sinv = cos_ref[...], sin_ref[...]
        for h in range(H):
            sl = slice(h * HEAD_DIM, (h + 1) * HEAD_DIM)
            for x_ref, g_ref, o_ref, sc in ((q_ref, qg_ref, qo_ref, scale), (k_ref, kg_ref, ko_ref, 1.0)):
                xv = x_ref[:, sl]
                xn = xv * lax.rsqrt(jnp.mean(xv * xv, axis=-1, keepdims=True) + EPS) * g_ref[...]
                y = xn * cosv + pltpu.roll(xn, HEAD_DIM // 2, 1) * sinv
                o_ref[:, sl] = (y * sc).astype(BF16)
        vo_ref[...] = v_ref[...].astype(BF16)

    def col(cb):
        return pl.BlockSpec((T, A), lambda i: (i, cb))

    vec = pl.BlockSpec((1, HEAD_DIM), lambda i: (0, 0))
    tab = pl.BlockSpec((T, HEAD_DIM), lambda i: (i, 0))
    out = pl.BlockSpec((T, A), lambda i: (i, 0))
    return pl.pallas_call(
        body, name=name, grid=(S // T,),
        in_specs=[col(cb0), col(cb0 + 1), col(cb0 + 2), vec, vec, tab, tab],
        out_specs=[out] * 3, out_shape=[jax.ShapeDtypeStruct((S, A), BF16)] * 3,
        compiler_params=_params("parallel"),
    )(u, u, u, qn_g, kn_g, cos, sin)


def _qkv_prep_bwd(u, dqs, dks, dvs, qn_g, kn_g, cos, sin, cb0, name):
    S = u.shape[0]
    A = dqs[0].shape[1]
    H = A // HEAD_DIM
    T = _tile(S, 256, SUBLANE_BF16)
    nb = len(dqs)
    scale = HEAD_DIM ** -0.5

    def body(*refs):
        q_ref, k_ref, qg_ref, kg_ref, cos_ref, sin_ref = refs[:6]
        dq_refs, dk_refs, dv_refs = refs[6:6 + nb], refs[6 + nb:6 + 2 * nb], refs[6 + 2 * nb:6 + 3 * nb]
        dqo_ref, dko_ref, dvo_ref, dqg_ref, dkg_ref = refs[6 + 3 * nb:]

        @pl.when(pl.program_id(0) == 0)
        def _():
            dqg_ref[...] = jnp.zeros_like(dqg_ref)
            dkg_ref[...] = jnp.zeros_like(dkg_ref)

        cosv, sinv = cos_ref[...], sin_ref[...]
        for h in range(H):
            sl = slice(h * HEAD_DIM, (h + 1) * HEAD_DIM)
            for x_ref, g_ref, d_refs, o_ref, dg_ref, sc in ((q_ref, qg_ref, dq_refs, dqo_ref, dqg_ref, scale),
                                                          (k_ref, kg_ref, dk_refs, dko_ref, dkg_ref, 1.0)):
                dy = d_refs[0][:, sl]
                for r in d_refs[1:]:
                    dy = dy + r[:, sl]
                dy = dy * sc
                dxn = dy * cosv + pltpu.roll(dy * sinv, HEAD_DIM // 2, 1)
                xv = x_ref[:, sl]
                r = lax.rsqrt(jnp.mean(xv * xv, axis=-1, keepdims=True) + EPS)
                xh = xv * r
                dg_ref[...] += jnp.sum(dxn * xh, axis=0, keepdims=True)
                dxh = dxn * g_ref[...]
                o_ref[:, sl] = (r * (dxh - xh * jnp.mean(dxh * xh, axis=-1, keepdims=True))).astype(BF16)
        dv = dv_refs[0][...]
        for r in dv_refs[1:]:
            dv = dv + r[...]
        dvo_ref[...] = dv.astype(BF16)

    def col(cb):
        return pl.BlockSpec((T, A), lambda i: (i, cb))

    vec = pl.BlockSpec((1, HEAD_DIM), lambda i: (0, 0))
    tab = pl.BlockSpec((T, HEAD_DIM), lambda i: (i, 0))
    row = pl.BlockSpec((T, A), lambda i: (i, 0))
    return pl.pallas_call(
        body, name=name, grid=(S // T,),
        in_specs=[col(cb0), col(cb0 + 1), vec, vec, tab, tab] + [row] * (3 * nb),
        out_specs=[row, row, row, vec, vec],
        out_shape=[jax.ShapeDtypeStruct((S, A), BF16)] * 3 + [jax.ShapeDtypeStruct((1, HEAD_DIM), F32)] * 2,
        compiler_params=_params("arbitrary"),
    )(u, u, qn_g, kn_g, cos, sin, *dqs, *dks, *dvs)


def _band_masks(n):
    qi = lax.broadcasted_iota(jnp.int32, (Q_BLOCK, Q_BLOCK), 0)
    ki = lax.broadcasted_iota(jnp.int32, (Q_BLOCK, Q_BLOCK), 1)
    return ki <= qi, (ki >= qi) & (n > 0)


def _attn_fwd(q, k, v, n_cls, name):
    L = q.shape[0]
    nb = L // Q_BLOCK

    def body(q_ref, kc_ref, kp_ref, vc_ref, vp_ref, o_ref, l_ref):
        mc, mp = _band_masks(pl.program_id(1))
        qv = q_ref[...]
        sc = jnp.where(mc, _dot(qv, kc_ref[...], "nt"), -jnp.inf)
        sp = jnp.where(mp, _dot(qv, kp_ref[...], "nt"), -jnp.inf)
        mx = jnp.maximum(jnp.max(sc, axis=-1, keepdims=True), jnp.max(sp, axis=-1, keepdims=True))
        pc, pp = jnp.exp(sc - mx), jnp.exp(sp - mx)
        den = jnp.sum(pc, axis=-1, keepdims=True) + jnp.sum(pp, axis=-1, keepdims=True)
        o = _dot(pc.astype(BF16), vc_ref[...], "nn") + _dot(pp.astype(BF16), vp_ref[...], "nn")
        o_ref[...] = o / den
        l_ref[...] = jnp.broadcast_to(mx + jnp.log(den), (Q_BLOCK, LANE))

    cur = pl.BlockSpec((Q_BLOCK, LANE), lambda j, n: (n, j))
    prv = pl.BlockSpec((Q_BLOCK, LANE), lambda j, n: (jnp.maximum(n - 1, 0), j))
    sds = jax.ShapeDtypeStruct(q.shape, F32)
    return pl.pallas_call(
        body, name=name, grid=(n_cls, nb),
        in_specs=[cur, cur, prv, cur, prv], out_specs=[cur, cur], out_shape=[sds, sds],
        compiler_params=_params("parallel", "parallel"),
    )(q, k, k, v, v)


def _attn_merge(os_, ls_, name):
    S, A = os_[0].shape
    T = _tile(S, 256, SUBLANE_BF16)
    nb = len(os_)

    def body(*refs):
        o_refs, l_refs = refs[:nb], refs[nb:2 * nb]
        ob_ref, of_ref, lj_ref = refs[2 * nb:]
        ls = [r[...] for r in l_refs]
        mx = functools.reduce(jnp.maximum, ls)
        es = [jnp.exp(l - mx) for l in ls]
        den = functools.reduce(lambda a, b: a + b, es)
        o = functools.reduce(lambda a, b: a + b, [e * r[...] for e, r in zip(es, o_refs)]) / den
        ob_ref[...] = o.astype(BF16)
        of_ref[...] = o
        lj_ref[...] = mx + jnp.log(den)

    row = pl.BlockSpec((T, A), lambda i: (i, 0))
    return pl.pallas_call(
        body, name=name, grid=(S // T,),
        in_specs=[row] * (2 * nb), out_specs=[row] * 3,
        out_shape=[jax.ShapeDtypeStruct((S, A), BF16), jax.ShapeDtypeStruct((S, A), F32),
                   jax.ShapeDtypeStruct((S, A), F32)],
        compiler_params=_params("parallel"),
    )(*os_, *ls_)


def _attn_merge_bwd(dz, o, cb, name):
    S, A = o.shape
    H = A // HEAD_DIM
    T = _tile(S, 256, SUBLANE_BF16)

    def body(dz_ref, o_ref, dob_ref, dl_ref):
        dov = dz_ref[...]
        dob_ref[...] = dov.astype(BF16)
        prod = dov * o_ref[...]
        for h in range(H):
            sl = slice(h * HEAD_DIM, (h + 1) * HEAD_DIM)
            dl_ref[:, sl] = jnp.broadcast_to(jnp.sum(prod[:, sl], axis=-1, keepdims=True), (T, HEAD_DIM))

    row = pl.BlockSpec((T, A), lambda i: (i, 0))
    return pl.pallas_call(
        body, name=name, grid=(S // T,),
        in_specs=[pl.BlockSpec((T, A), lambda i: (i, cb)), row], out_specs=[row, row],
        out_shape=[jax.ShapeDtypeStruct((S, A), BF16), jax.ShapeDtypeStruct((S, A), F32)],
        compiler_params=_params("parallel"),
    )(dz, o)


def _attn_dq(q, k, v, do, lse, delta, n_cls, name):
    L = q.shape[0]
    nb = L // Q_BLOCK

    def body(q_ref, kc_ref, kp_ref, vc_ref, vp_ref, do_ref, l_ref, d_ref, dq_ref):
        mc, mp = _band_masks(pl.program_id(1))
        qv, dov = q_ref[...], do_ref[...]
        lv, dv = l_ref[:, 0:1], d_ref[:, 0:1]
        pc = jnp.where(mc, jnp.exp(_dot(qv, kc_ref[...], "nt") - lv), 0.0)
        pp = jnp.where(mp, jnp.exp(_dot(qv, kp_ref[...], "nt") - lv), 0.0)
        dsc = (pc * (_dot(dov, vc_ref[...], "nt") - dv)).astype(BF16)
        dsp = (pp * (_dot(dov, vp_ref[...], "nt") - dv)).astype(BF16)
        dq_ref[...] = _dot(dsc, kc_ref[...], "nn") + _dot(dsp, kp_ref[...], "nn")

    cur = pl.BlockSpec((Q_BLOCK, LANE), lambda j, n: (n, j))
    prv = pl.BlockSpec((Q_BLOCK, LANE), lambda j, n: (jnp.maximum(n - 1, 0), j))
    return pl.pallas_call(
        body, name=name, grid=(n_cls, nb),
        in_specs=[cur, cur, prv, cur, prv, cur, cur, cur], out_specs=cur,
        out_shape=jax.ShapeDtypeStruct(q.shape, F32),
        compiler_params=_params("parallel", "parallel"),
    )(q, k, k, v, v, do, lse, delta)


def _attn_dkv(q, k, v, do, lse, delta, n_cls, name):
    L = q.shape[0]
    nb = L // Q_BLOCK

    def body(k_ref, v_ref, qa_ref, qb_ref, doa_ref, dob_ref, la_ref, lb_ref, da_ref, db_ref, dk_ref, dv_ref):
        qi = lax.broadcasted_iota(jnp.int32, (Q_BLOCK, Q_BLOCK), 0)
        ki = lax.broadcasted_iota(jnp.int32, (Q_BLOCK, Q_BLOCK), 1)
        ma = ki <= qi
        mb = (ki >= qi) & (pl.program_id(1) < nb - 1)
        kv, vv = k_ref[...], v_ref[...]
        pa = jnp.where(ma, jnp.exp(_dot(qa_ref[...], kv, "nt") - la_ref[:, 0:1]), 0.0)
        pb = jnp.where(mb, jnp.exp(_dot(qb_ref[...], kv, "nt") - lb_ref[:, 0:1]), 0.0)
        dv_ref[...] = _dot(pa.astype(BF16), doa_ref[...], "tn") + _dot(pb.astype(BF16), dob_ref[...], "tn")
        dsa = (pa * (_dot(doa_ref[...], vv, "nt") - da_ref[:, 0:1])).astype(BF16)
        dsb = (pb * (_dot(dob_ref[...], vv, "nt") - db_ref[:, 0:1])).astype(BF16)
        dk_ref[...] = _dot(dsa, qa_ref[...], "tn") + _dot(dsb, qb_ref[...], "tn")

    cur = pl.BlockSpec((Q_BLOCK, LANE), lambda j, m: (m, j))
    nxt = pl.BlockSpec((Q_BLOCK, LANE), lambda j, m: (jnp.minimum(m + 1, nb - 1), j))
    sds = jax.ShapeDtypeStruct(q.shape, F32)
    return pl.pallas_call(
        body, name=name, grid=(n_cls, nb),
        in_specs=[cur, cur, cur, nxt, cur, nxt, cur, nxt, cur, nxt], out_specs=[cur, cur], out_shape=[sds, sds],
        compiler_params=_params("parallel", "parallel"),
    )(k, v, q, q, do, do, lse, lse, delta, delta)


def _class_view(x, dil):
    return x.reshape(x.shape[0] // dil, dil * x.shape[1])


def _even_mixer(u, conv_w, conv_b, cn_g, cn_b, qn_g, kn_g, tag):
    S = u.shape[0]
    C = conv_w.shape[1]
    A = (u.shape[1] - 2 * C) // 3
    assert A == C, "column-block addressing of u assumes equal conv and attention widths"
    assert all(window // dil == Q_BLOCK and S % (dil * Q_BLOCK) == 0 for window, dil in DIL_PATTERNS)
    H = A // HEAD_DIM
    cos, sin = _rope_tables(S)
    a_out, y = _conv_fwd(u, conv_w, conv_b, cn_g, cn_b, "conv_fwd" + tag)
    q, k, v = _qkv_prep(u, qn_g, kn_g, cos, sin, 2, "qkv_prep" + tag)
    views = []
    os_, ls_ = [], []
    for _, dil in DIL_PATTERNS:
        qv, kv, vv = _class_view(q, dil), _class_view(k, dil), _class_view(v, dil)
        o_i, l_i = _attn_fwd(qv, kv, vv, dil * H, f"attn_fwd{dil}" + tag)
        views.append((qv, kv, vv))
        os_.append(o_i.reshape(S, A))
        ls_.append(l_i.reshape(S, A))
    ob, of, lse = _attn_merge(os_, ls_, "attn_merge" + tag)
    z = jnp.concatenate([a_out, ob], axis=1)

    def backward(dz):
        dy, d_cn_g, d_cn_b = _conv_bwd_norm(dz, y, cn_g, cn_b, "conv_bwd_norm" + tag)
        d_val, d_gate, d_w, d_b = _conv_bwd_taps(u, dy, conv_w, "conv_bwd_taps" + tag)
        dob, delta = _attn_merge_bwd(dz, of, 1, "attn_merge_bwd" + tag)
        dqs, dks, dvs = [], [], []
        for (_, dil), (qv, kv, vv) in zip(DIL_PATTERNS, views):
            dov, lv, dlv = _class_view(dob, dil), _class_view(lse, dil), _class_view(delta, dil)
            dqs.append(_attn_dq(qv, kv, vv, dov, lv, dlv, dil * H, f"attn_dq{dil}" + tag).reshape(S, A))
            dk_i, dv_i = _attn_dkv(qv, kv, vv, dov, lv, dlv, dil * H, f"attn_dkv{dil}" + tag)
            dks.append(dk_i.reshape(S, A))
            dvs.append(dv_i.reshape(S, A))
        dq, dk, dv, d_qn, d_kn = _qkv_prep_bwd(u, dqs, dks, dvs, qn_g, kn_g, cos, sin, 2, "qkv_prep_bwd" + tag)
        du = jnp.concatenate([d_val, d_gate, dq, dk, dv], axis=1)
        return du, [d_w[:CONV_WIDTH], d_b[0], d_cn_g[0], d_cn_b[0], d_qn[0], d_kn[0]]

    return z, backward


_LEVELS = (128, 64, 32, 16, 8, 4, 2, 1)


def _chunk_cumsum(g, rows, reverse=False):
    C = g.shape[0]
    d = 1
    while d < C:
        if reverse:
            g = g + jnp.where(rows < C - d, pltpu.roll(g, C - d, 0), 0.0)
        else:
            g = g + jnp.where(rows >= d, pltpu.roll(g, d, 0), 0.0)
        d *= 2
    return g


def _level_ref(b, b_scr, rows, m):
    C = b.shape[0]
    if m >= 8:
        pieces = [jnp.broadcast_to(b_scr[2 * m * j + m - 1:2 * m * j + m, :], (2 * m, LANE)) for j in range(C // (2 * m))]
        return pieces[0] if len(pieces) == 1 else jnp.concatenate(pieces, axis=0)
    pos = rows & (2 * m - 1)
    ref = b
    for p in range(2 * m):
        if p != m - 1:
            ref = jnp.where(pos == p, pltpu.roll(b, (p - (m - 1)) % C, 0), ref)
    return ref


def _level_operands(q, k, b, b_scr, rows, m):
    ref = _level_ref(b, b_scr, rows, m)
    qs = (q * jnp.exp(jnp.minimum(b - ref, 0.0))).astype(BF16)
    ks = (k * jnp.exp(jnp.minimum(ref - b, 0.0))).astype(BF16)
    return qs, ks


def _split2(x):
    hi = x.astype(BF16)
    lo = (x - hi.astype(F32)).astype(BF16)
    return jnp.concatenate([hi, lo], axis=1)


def _level_mask(tt, ss, m):
    x = tt ^ ss
    return (tt > ss) & (x >= m) & (x < 2 * m)


def _hgrn_gates(qz, fz, la, lc, oml):
    sq = jax.nn.sigmoid(qz)
    q = qz * sq
    s = jax.nn.sigmoid(fz)
    c = lc + jnp.minimum(fz, 0.0) - jnp.log(1.0 + jnp.exp(-jnp.abs(fz)))
    mx = jnp.maximum(la, c)
    g = mx + jnp.log(1.0 + jnp.exp(-jnp.abs(la - c)))
    k = oml * (1.0 - s)
    return q, sq, k, s, g, c


def _hgrn_fwd(u, la, lc, oml, gn_g, name):
    S = u.shape[0]
    W = u.shape[1] // 4
    H = W // HGRN_KDIM
    C = min(HGRN_CHUNK, S)
    nc = S // C
    levels = [m for m in _LEVELS if m < C]

    def body(qz_ref, fz_ref, iz_ref, gz_ref, la_ref, lc_ref, oml_ref, gn_ref,
             z_ref, o_ref, a_ref, st_ref, state, b_scr):
        @pl.when(pl.program_id(1) == 0)
        def _():
            state[...] = jnp.zeros_like(state)

        rows = lax.broadcasted_iota(jnp.int32, (C, LANE), 0)
        tt = lax.broadcasted_iota(jnp.int32, (C, C), 0)
        ss = lax.broadcasted_iota(jnp.int32, (C, C), 1)
        q, _, k, _, g, _ = _hgrn_gates(qz_ref[...], fz_ref[...], la_ref[...], lc_ref[...], oml_ref[...])
        v = iz_ref[...].astype(BF16)
        b = _chunk_cumsum(g, rows)
        b_scr[...] = b
        a = jnp.where(tt == ss, jnp.sum(q * k, axis=-1, keepdims=True), 0.0)
        for m in levels:
            qs, ks = _level_operands(q, k, b, b_scr, rows, m)
            a = jnp.where(_level_mask(tt, ss, m), _dot(qs, ks, "nt"), a)
        ab = a.astype(BF16)
        a_ref[...] = ab
        st = state[...]
        st_ref[...] = st
        o = _dot(ab, v, "nn") + _dot((q * jnp.exp(b)).astype(BF16), st.astype(BF16), "nt")
        bl = b_scr[C - 1:C, :]
        kh = (k * jnp.exp(bl - b)).astype(BF16)
        state[...] = st * jnp.exp(bl) + _dot(v, kh, "tn")
        o_ref[...] = o
        r = lax.rsqrt(jnp.mean(o * o, axis=-1, keepdims=True) + EPS)
        gz = gz_ref[...]
        z_ref[...] = (o * r * gn_ref[...] * (gz * jax.nn.sigmoid(gz))).astype(BF16)

    def col(off):
        return pl.BlockSpec((C, LANE), lambda h, i: (i, off * H + h))

    vec = pl.BlockSpec((1, LANE), lambda h, i: (0, h))
    tile = pl.BlockSpec((C, LANE), lambda h, i: (i, h))
    return pl.pallas_call(
        body, name=name, grid=(H, nc),
        in_specs=[col(0), col(1), col(2), col(3), vec, vec, vec, vec],
        out_specs=[tile, tile, pl.BlockSpec((None, C, C), lambda h, i: (h, i, 0)),
                   pl.BlockSpec((None, None, LANE, LANE), lambda h, i: (h, i, 0, 0))],
        out_shape=[jax.ShapeDtypeStruct((S, W), BF16), jax.ShapeDtypeStruct((S, W), F32),
                   jax.ShapeDtypeStruct((H, S, C), BF16), jax.ShapeDtypeStruct((H, nc, LANE, LANE), F32)],
        scratch_shapes=[pltpu.VMEM((LANE, LANE), F32), pltpu.VMEM((C, LANE), F32)],
        compiler_params=_params("parallel", "arbitrary"),
    )(u, u, u, u, la, lc, oml, gn_g)


def _hgrn_bwd(u, la, lc, oml, gn_g, o, a, st, dz, name):
    S = u.shape[0]
    W = u.shape[1] // 4
    H = W // HGRN_KDIM
    C = min(HGRN_CHUNK, S)
    nc = S // C
    levels = [m for m in _LEVELS if m < C]

    def body(qz_ref, fz_ref, iz_ref, gz_ref, la_ref, lc_ref, oml_ref, gn_ref, o_ref, a_ref, st_ref, dz_ref,
             dqz_ref, dfz_ref, diz_ref, dgz_ref, dla_ref, dlc_ref, doml_ref, dgn_ref, dstate, b_scr):
        @pl.when(pl.program_id(1) == 0)
        def _():
            dstate[...] = jnp.zeros_like(dstate)
            dla_ref[...] = jnp.zeros_like(dla_ref)
            dlc_ref[...] = jnp.zeros_like(dlc_ref)
            doml_ref[...] = jnp.zeros_like(doml_ref)
            dgn_ref[...] = jnp.zeros_like(dgn_ref)

        rows = lax.broadcasted_iota(jnp.int32, (C, LANE), 0)
        tt = lax.broadcasted_iota(jnp.int32, (C, C), 0)
        ss = lax.broadcasted_iota(jnp.int32, (C, C), 1)
        la_v, lc_v, oml_v = la_ref[...], lc_ref[...], oml_ref[...]
        qz, fz = qz_ref[...], fz_ref[...]
        q, sq, k, s, g, c = _hgrn_gates(qz, fz, la_v, lc_v, oml_v)
        vf = iz_ref[...]
        v = vf.astype(BF16)

        ov, gz, dzv, gn = o_ref[...], gz_ref[...], dz_ref[...], gn_ref[...]
        r = lax.rsqrt(jnp.mean(ov * ov, axis=-1, keepdims=True) + EPS)
        on = ov * r
        sg = jax.nn.sigmoid(gz)
        silu_g = gz * sg
        dgn_ref[...] += jnp.sum(dzv * on * silu_g, axis=0, keepdims=True)
        dgz_ref[...] = (dzv * on * gn * (sg * (1.0 + gz * (1.0 - sg)))).astype(BF16)
        don = dzv * gn * silu_g
        do_f = r * (don - on * jnp.mean(don * on, axis=-1, keepdims=True))
        do = do_f.astype(BF16)

        b = _chunk_cumsum(g, rows)
        b_scr[...] = b
        bl = b_scr[C - 1:C, :]
        e = jnp.exp(b)
        ebl = jnp.exp(bl)
        ekl = jnp.exp(bl - b)
        qh = q * e
        kh = k * ekl
        st_v = st_ref[...]
        dst = dstate[...]
        dstb = dst.astype(BF16)

        diz_ref[...] = (_dot(a_ref[...], do, "tn") + _dot(kh.astype(BF16), dstb, "nt")).astype(BF16)
        da = _dot(do, v, "nt")
        dqh = _dot(do, st_v.astype(BF16), "nn")
        dkh = _dot(v, dstb, "nn")
        dstate[...] = dst * ebl + _dot(do, qh.astype(BF16), "tn")
        dbl = jnp.sum(dkh * kh, axis=0, keepdims=True) + jnp.sum(dst * st_v, axis=0, keepdims=True) * ebl

        datt = jnp.sum(do_f * vf, axis=-1, keepdims=True)
        dqa = datt * k
        dka = datt * q
        for m in levels:
            ref = _level_ref(b, b_scr, rows, m)
            eu = jnp.exp(jnp.minimum(b - ref, 0.0))
            el = jnp.exp(jnp.minimum(ref - b, 0.0))
            gm = jnp.where(_level_mask(tt, ss, m), da, 0.0).astype(BF16)
            pq = _dot(gm, _split2(k * el), "nn")
            pk = _dot(gm, _split2(q * eu), "tn")
            dqa += (pq[:, :LANE] + pq[:, LANE:]) * eu
            dka += (pk[:, :LANE] + pk[:, LANE:]) * el
        db = q * dqa - k * dka + dqh * qh - dkh * kh
        db = db + jnp.where(rows == C - 1, dbl, 0.0)
        dq = dqa + dqh * e
        dk = dka + dkh * ekl
        dg = _chunk_cumsum(db, rows, reverse=True)

        wa = jnp.exp(la_v - g)
        wc = jnp.exp(c - g)
        dqz_ref[...] = (dq * (sq * (1.0 + qz * (1.0 - sq)))).astype(BF16)
        dfz_ref[...] = (dg * wc * (1.0 - s) - dk * oml_v * s * (1.0 - s)).astype(BF16)
        dla_ref[...] += jnp.sum(dg * wa, axis=0, keepdims=True)
        dlc_ref[...] += jnp.sum(dg * wc, axis=0, keepdims=True)
        doml_ref[...] += jnp.sum(dk * (1.0 - s), axis=0, keepdims=True)

    def col(off):
        return pl.BlockSpec((C, LANE), lambda h, i: (nc - 1 - i, off * H + h))

    vec = pl.BlockSpec((1, LANE), lambda h, i: (0, h))
    tile = pl.BlockSpec((C, LANE), lambda h, i: (nc - 1 - i, h))
    a_spec = pl.BlockSpec((None, C, C), lambda h, i: (h, nc - 1 - i, 0))
    st_spec = pl.BlockSpec((None, None, LANE, LANE), lambda h, i: (h, nc - 1 - i, 0, 0))
    sw = jax.ShapeDtypeStruct((S, W), BF16)
    vw = jax.ShapeDtypeStruct((1, W), F32)
    return pl.pallas_call(
        body, name=name, grid=(H, nc),
        in_specs=[col(0), col(1), col(2), col(3), vec, vec, vec, vec, tile, a_spec, st_spec, tile],
        out_specs=[tile, tile, tile, tile, vec, vec, vec, vec],
        out_shape=[sw, sw, sw, sw, vw, vw, vw, vw],
        scratch_shapes=[pltpu.VMEM((LANE, LANE), F32), pltpu.VMEM((C, LANE), F32)],
        compiler_params=_params("parallel", "arbitrary"),
    )(u, u, u, u, la, lc, oml, gn_g, o, a, st, dz)


def _lb_terms(lb_logits, layer):
    p = jax.nn.softmax(lb_logits, axis=0)
    lb = (jnp.cumsum(p, axis=0) - p[0:1])[layer]
    return jnp.log(lb)[None], jnp.log1p(-lb)[None], (1.0 - lb)[None]


def kernel(x, norm_ffn1, ffn1_wg, ffn1_wu, ffn1_wd, norm_mix, norm_ffn2, ffn2_wg, ffn2_wu, ffn2_wd, ev_w_in, ev_conv_w, ev_conv_b, ev_cn_g, ev_cn_b, ev_qn_g, ev_kn_g, ev_w_out, od_w_in, od_lb_logits, od_gn_g, od_w_out, loss_target, m_norm_ffn1, m_ffn1_wg, m_ffn1_wu, m_ffn1_wd, m_norm_mix, m_norm_ffn2, m_ffn2_wg, m_ffn2_wu, m_ffn2_wd, m_ev_w_in, m_ev_conv_w, m_ev_conv_b, m_ev_cn_g, m_ev_cn_b, m_ev_qn_g, m_ev_kn_g, m_ev_w_out, m_od_w_in, m_od_lb_logits, m_od_gn_g, m_od_w_out, v_norm_ffn1, v_ffn1_wg, v_ffn1_wu, v_ffn1_wd, v_norm_mix, v_norm_ffn2, v_ffn2_wg, v_ffn2_wu, v_ffn2_wd, v_ev_w_in, v_ev_conv_w, v_ev_conv_b, v_ev_cn_g, v_ev_cn_b, v_ev_qn_g, v_ev_kn_g, v_ev_w_out, v_od_w_in, v_od_lb_logits, v_od_gn_g, v_od_w_out):
    depth = norm_ffn1.shape[0]
    S, D = x.shape[1], x.shape[2]
    xi, yi, ci = _me()
    dev = 4 * xi + 2 * yi + ci
    c_idx = jnp.reshape(ci, (1,)).astype(jnp.int32)
    k_idx = jnp.reshape(2 * xi + yi, (1,)).astype(jnp.int32)

    def ffn_shard(wg, wu, wd, l):
        return jnp.stack([wg[l].T, wu[l].T, wd[l]]).astype(BF16)

    ffn_local = [ffn_shard(ffn1_wg, ffn1_wu, ffn1_wd, l) for l in range(depth)] \
        + [ffn_shard(ffn2_wg, ffn2_wu, ffn2_wd, l) for l in range(depth)]
    mix_local = [ev_w_in[0].T.astype(BF16)[None], ev_w_out[0].astype(BF16)[None],
                 od_w_in[0].T.astype(BF16)[None], od_w_out[0].astype(BF16)[None]]
    gathered = _all_gather(ffn_local + mix_local, "all_gather_weights")
    full = [g.reshape(g.shape[0], N_DEV * g.shape[2], g.shape[3]) for g in gathered]
    w_ffn1, w_ffn2 = full[:depth], full[depth:2 * depth]
    ev_w_in_t, ev_w_out_f, od_w_in_t, od_w_out_f = (f[0] for f in full[2 * depth:])

    conv_w_sh, gn_g_sh = ev_conv_w[0], od_gn_g[0]
    cw, cs = conv_w_sh.shape[0], conv_w_sh.shape[1]
    gs = gn_g_sh.shape[0]
    conv_w_z = lax.dynamic_update_slice(jnp.zeros((cw, N_DEV * cs), F32), conv_w_sh, (0, dev * cs))
    gn_g_z = lax.dynamic_update_slice(jnp.zeros((N_DEV * gs,), F32), gn_g_sh, (dev * gs,))
    conv_w_full, gn_g_full = _unpack_rows(
        _all_reduce_small(_pack_rows([conv_w_z, gn_g_z]), "gather_small_params"),
        [conv_w_z.shape, gn_g_z.shape])

    def ffn_forward(h, gain, w, tag):
        hn = _rms_fwd(h, gain[None], "rms_" + tag)
        out, gu = _ffn_fwd(h, hn, w, "ffn_fwd_" + tag)
        return out, (h, hn, gu)

    def odd_mixer(u, l):
        (la, lc, oml), lb_vjp = jax.vjp(functools.partial(_lb_terms, layer=l), od_lb_logits)
        gn = gn_g_full[None]
        zb, o_raw, scores, states = _hgrn_fwd(u, la, lc, oml, gn, f"hgrn_fwd{l}")

        def backward(dz):
            dqz, dfz, diz, dgz, dla, dlc, doml, dgn = _hgrn_bwd(
                u, la, lc, oml, gn, o_raw, scores, states, dz, f"hgrn_bwd{l}")
            (g_lb,) = lb_vjp((dla, dlc, doml))
            return jnp.concatenate([dqz, dfz, diz, dgz], axis=1), [g_lb, dgn[0]]

        return zb, backward

    saved = []
    h = x[0]
    for l in range(depth):
        h, s1 = ffn_forward(h, norm_ffn1[l], w_ffn1[l], f"a{l}")
        hn = _rms_fwd(h, norm_mix[l][None], f"rms_mix{l}")
        if l % 2 == 0:
            u = _mm(hn, ev_w_in_t, "nt", F32, f"mix_in{l}")
            zb, core_vjp = _even_mixer(u, conv_w_full, ev_conv_b, ev_cn_g, ev_cn_b, ev_qn_g, ev_kn_g, str(l))
            w_out = ev_w_out_f
        else:
            u = _mm(hn, od_w_in_t, "nt", F32, f"mix_in{l}")
            zb, core_vjp = odd_mixer(u, l)
            w_out = od_w_out_f
        h_mix = h
        h = _mm(zb, w_out, "nn", F32, f"mix_out{l}", res=h)
        sm = (h_mix, hn, zb, core_vjp)
        h, s2 = ffn_forward(h, norm_ffn2[l], w_ffn2[l], f"b{l}")
        saved.append((s1, sm, s2))

    dy, loss_part = _loss_grad(h, loss_target[0], "loss_grad")

    def ffn_backward(dy, gain, w, sv, tag):
        h_in, hn, gu = sv
        dxn, dout, t = _ffn_bwd_dx(dy, w, gu, "ffn_bwd_dx_" + tag)
        dw = _ffn_bwd_dw(hn, dout, t, "ffn_bwd_dw_" + tag)
        dx, dgain = _rms_bwd(h_in, gain[None], dxn, dy, "rms_bwd_" + tag)
        return dx, dgain[0], dw

    g_norm1, g_norm2, g_normm = [None] * depth, [None] * depth, [None] * depth
    dw_ffn1, dw_ffn2 = [None] * depth, [None] * depth
    small, dw_mix = [None, None], [None, None]
    for l in reversed(range(depth)):
        s1, (h_mix, hn, zb, core_vjp), s2 = saved[l]
        dy, g_norm2[l], dw_ffn2[l] = ffn_backward(dy, norm_ffn2[l], w_ffn2[l], s2, f"b{l}")
        dyb = dy.astype(BF16)
        if l % 2 == 0:
            w_out, w_in_t = ev_w_out_f, ev_w_in_t
        else:
            w_out, w_in_t = od_w_out_f, od_w_in_t
        dz = _mm(dyb, w_out, "nt", F32, f"mix_out_dz{l}")
        dw_out = _mm(zb, dyb, "tn", BF16, f"mix_out_dw{l}")
        dub, small[l % 2] = core_vjp(dz)
        dw_mix[l % 2] = [_mm(dub, hn, "tn", BF16, f"mix_in_dw{l}"), dw_out]
        dhn = _mm(dub, w_in_t, "nn", F32, f"mix_in_dx{l}")
        dy, gm = _rms_bwd(h_mix, norm_mix[l][None], dhn, dy, f"rms_bwd_mix{l}")
        g_normm[l] = gm[0]
        dy, g_norm1[l], dw_ffn1[l] = ffn_backward(dy, norm_ffn1[l], w_ffn1[l], s1, f"a{l}")
    grad_x = dy[None]

    partial = dw_ffn1 + dw_ffn2 + [g[None] for g in dw_mix[0] + dw_mix[1]]
    partial = [g.reshape(g.shape[0], 4, 2, g.shape[1] // N_DEV, g.shape[2]) for g in partial]
    got = _exchange_core_halves(partial, "reduce_core_halves")
    sums = [_add_core_halves(g, r, c_idx, f"add_core_halves{a}") for a, (g, r) in enumerate(zip(partial, got))]
    got = _exchange_chip_blocks(sums, "reduce_chip_blocks")
    shard_g = [_sum_chip_blocks(s, r, k_idx, f"sum_chip_blocks{a}") for a, (s, r) in enumerate(zip(sums, got))]
    g_ffn1, g_ffn2 = shard_g[:depth], shard_g[depth:2 * depth]
    g_ev_in_t, g_ev_out, g_od_in_t, g_od_out = shard_g[2 * depth:]

    g_conv_w, g_conv_b, g_cn_g, g_cn_b, g_qn_g, g_kn_g = small[0]
    g_lb, g_gn = small[1]
    parts = [jnp.stack(g_norm1), jnp.stack(g_normm), jnp.stack(g_norm2), g_conv_b, g_cn_g, g_cn_b,
             g_qn_g, g_kn_g, g_lb, g_conv_w, g_gn, loss_part[0, :1]]
    red = _unpack_rows(_all_reduce_small(_pack_rows(parts), "reduce_small_grads"), [p.shape for p in parts])
    g_norm1, g_normm, g_norm2, g_conv_b, g_cn_g, g_cn_b, g_qn_g, g_kn_g, g_lb, g_conv_w, g_gn, loss = red
    g_conv_w = lax.dynamic_slice(g_conv_w, (0, dev * cs), (cw, cs))
    g_gn = lax.dynamic_slice(g_gn, (dev * gs,), (gs,))

    def ffn_grads(gl):
        return (jnp.stack([g[0].T for g in gl]), jnp.stack([g[1].T for g in gl]), jnp.stack([g[2] for g in gl]))

    g_ffn1_wg, g_ffn1_wu, g_ffn1_wd = ffn_grads(g_ffn1)
    g_ffn2_wg, g_ffn2_wu, g_ffn2_wd = ffn_grads(g_ffn2)
    grads = [g_norm1, g_ffn1_wg, g_ffn1_wu, g_ffn1_wd, g_normm, g_norm2, g_ffn2_wg, g_ffn2_wu, g_ffn2_wd,
             g_ev_in_t[0].T[None], g_conv_w[None], g_conv_b[None], g_cn_g[None], g_cn_b[None], g_qn_g[None],
             g_kn_g[None], g_ev_out, g_od_in_t[0].T[None], g_lb, g_gn[None], g_od_out]
    weights = [norm_ffn1, ffn1_wg, ffn1_wu, ffn1_wd, norm_mix, norm_ffn2, ffn2_wg, ffn2_wu, ffn2_wd, ev_w_in,
               ev_conv_w, ev_conv_b, ev_cn_g, ev_cn_b, ev_qn_g, ev_kn_g, ev_w_out, od_w_in, od_lb_logits,
               od_gn_g, od_w_out]
    moms = [m_norm_ffn1, m_ffn1_wg, m_ffn1_wu, m_ffn1_wd, m_norm_mix, m_norm_ffn2, m_ffn2_wg, m_ffn2_wu,
            m_ffn2_wd, m_ev_w_in, m_ev_conv_w, m_ev_conv_b, m_ev_cn_g, m_ev_cn_b, m_ev_qn_g, m_ev_kn_g,
            m_ev_w_out, m_od_w_in, m_od_lb_logits, m_od_gn_g, m_od_w_out]
    vars_ = [v_norm_ffn1, v_ffn1_wg, v_ffn1_wu, v_ffn1_wd, v_norm_mix, v_norm_ffn2, v_ffn2_wg, v_ffn2_wu,
             v_ffn2_wd, v_ev_w_in, v_ev_conv_w, v_ev_conv_b, v_ev_cn_g, v_ev_cn_b, v_ev_qn_g, v_ev_kn_g,
             v_ev_w_out, v_od_w_in, v_od_lb_logits, v_od_gn_g, v_od_w_out]
    deltas, new_m, new_v = [], [], []
    for i, (w, g, m, v) in enumerate(zip(weights, grads, moms, vars_)):
        d, nm, nv = _adamw(w, g, m, v, f"adamw{i}")
        deltas.append(d)
        new_m.append(nm)
        new_v.append(nv)
    return (loss[0], grad_x, *grads, *deltas, *new_m, *new_v)
```

```python
import functools
import math

import jax
import jax.numpy as jnp
from jax import lax
from jax.experimental import pallas as pl
from jax.experimental.pallas import tpu as pltpu

F32 = jnp.float32
BF16 = jnp.bfloat16
MESH = pl.DeviceIdType.MESH
N_DEV = 8

EPS = 1e-6
HEAD_DIM = 128
CONV_WIDTH = 31
DIL_PATTERNS = ((128, 1), (512, 4), (2048, 16))
Q_BLOCK = 128
ROPE_THETA = 10000.0
HGRN_KDIM = 128
HGRN_CHUNK = 256

ADAM_LR = 0.001
ADAM_B1 = 0.9
ADAM_B2 = 0.999
ADAM_EPS = 1e-08
ADAM_WD = 0.01
ADAM_STEP = 10

VMEM_LIMIT_BYTES = 56 * 1024 * 1024
LANE = 128
SUBLANE_BF16 = 16

ANY = pl.BlockSpec(memory_space=pl.ANY)


def _tile(n, pref, mult):
    t = (min(pref, n) // mult) * mult
    while t > 0:
        if n % t == 0:
            return t
        t -= mult
    return n


def _params(*sem):
    return pltpu.CompilerParams(dimension_semantics=sem, vmem_limit_bytes=VMEM_LIMIT_BYTES)


_DOT_DIMS = {
    "nn": (((1,), (0,)), ((), ())),
    "nt": (((1,), (1,)), ((), ())),
    "tn": (((0,), (0,)), ((), ())),
}


def _dot(a, b, mode):
    return lax.dot_general(a, b, _DOT_DIMS[mode], preferred_element_type=F32)


def _mm(a, b, mode, out_dtype, name, res=None, tm=1024, tn=1024, tk=2048):
    if mode == "nt":
        (M, K), N = a.shape, b.shape[0]
    elif mode == "nn":
        (M, K), N = a.shape, b.shape[1]
    else:
        (K, M), N = a.shape, b.shape[1]
    tm, tn, tk = _tile(M, tm, LANE), _tile(N, tn, LANE), _tile(K, tk, LANE)
    nk = K // tk

    def body(*refs):
        if res is None:
            a_ref, b_ref, o_ref, acc = refs
        else:
            a_ref, b_ref, r_ref, o_ref, acc = refs
        k = pl.program_id(2)

        @pl.when(k == 0)
        def _():
            acc[...] = jnp.zeros_like(acc)

        acc[...] += _dot(a_ref[...].astype(BF16), b_ref[...].astype(BF16), mode)

        @pl.when(k == nk - 1)
        def _():
            r = acc[...]
            if res is not None:
                r = r_ref[...] + r
            o_ref[...] = r.astype(out_dtype)

    a_spec = {"nt": pl.BlockSpec((tm, tk), lambda i, j, k: (i, k)),
              "nn": pl.BlockSpec((tm, tk), lambda i, j, k: (i, k)),
              "tn": pl.BlockSpec((tk, tm), lambda i, j, k: (k, i))}[mode]
    b_spec = {"nt": pl.BlockSpec((tn, tk), lambda i, j, k: (j, k)),
              "nn": pl.BlockSpec((tk, tn), lambda i, j, k: (k, j)),
              "tn": pl.BlockSpec((tk, tn), lambda i, j, k: (k, j))}[mode]
    o_spec = pl.BlockSpec((tm, tn), lambda i, j, k: (i, j))
    in_specs = [a_spec, b_spec] + ([o_spec] if res is not None else [])
    args = (a, b) + ((res,) if res is not None else ())
    return pl.pallas_call(
        body, name=name, grid=(M // tm, N // tn, nk),
        in_specs=in_specs, out_specs=o_spec,
        out_shape=jax.ShapeDtypeStruct((M, N), out_dtype),
        scratch_shapes=[pltpu.VMEM((tm, tn), F32)],
        compiler_params=_params("parallel", "parallel", "arbitrary"),
    )(*args)


def _rms_fwd(x, gain, name):
    S, D = x.shape
    tm = _tile(S, 512, SUBLANE_BF16)

    def body(x_ref, g_ref, o_ref):
        xv = x_ref[...]
        r = lax.rsqrt(jnp.mean(xv * xv, axis=-1, keepdims=True) + EPS)
        o_ref[...] = (xv * r * g_ref[...]).astype(BF16)

    return pl.pallas_call(
        body, name=name, grid=(S // tm,),
        in_specs=[pl.BlockSpec((tm, D), lambda i: (i, 0)), pl.BlockSpec((1, D), lambda i: (0, 0))],
        out_specs=pl.BlockSpec((tm, D), lambda i: (i, 0)),
        out_shape=jax.ShapeDtypeStruct((S, D), BF16),
        compiler_params=_params("parallel"),
    )(x, gain)


def _rms_bwd(x, gain, dxn, dy, name):
    S, D = x.shape
    tm = _tile(S, 512, 8)

    def body(x_ref, g_ref, dxn_ref, dy_ref, dx_ref, dg_ref):
        @pl.when(pl.program_id(0) == 0)
        def _():
            dg_ref[...] = jnp.zeros_like(dg_ref)

        xv = x_ref[...]
        r = lax.rsqrt(jnp.mean(xv * xv, axis=-1, keepdims=True) + EPS)
        xh = xv * r
        dxn_v = dxn_ref[...]
        dg_ref[...] += jnp.sum(dxn_v * xh, axis=0, keepdims=True)
        dxh = dxn_v * g_ref[...]
        dx_ref[...] = dy_ref[...] + r * (dxh - xh * jnp.mean(dxh * xh, axis=-1, keepdims=True))

    row = pl.BlockSpec((tm, D), lambda i: (i, 0))
    vec = pl.BlockSpec((1, D), lambda i: (0, 0))
    return pl.pallas_call(
        body, name=name, grid=(S // tm,),
        in_specs=[row, vec, row, row], out_specs=[row, vec],
        out_shape=[jax.ShapeDtypeStruct((S, D), F32), jax.ShapeDtypeStruct((1, D), F32)],
        compiler_params=_params("arbitrary"),
    )(x, gain, dxn, dy)


def _ffn_fwd(x, xn, w, name):
    S, D = x.shape
    F = w.shape[1]
    tm, tf = _tile(S, 512, SUBLANE_BF16), _tile(F, 512, LANE)
    nf = F // tf

    def body(x_ref, xn_ref, w_ref, o_ref, gu_ref, acc):
        f = pl.program_id(1)

        @pl.when(f == 0)
        def _():
            acc[...] = jnp.zeros_like(acc)

        xnv = xn_ref[...]
        g = _dot(xnv, w_ref[0], "nt")
        u = _dot(xnv, w_ref[1], "nt")
        gu_ref[0] = g.astype(BF16)
        gu_ref[1] = u.astype(BF16)
        h = (g * jax.nn.sigmoid(g) * u).astype(BF16)
        acc[...] += _dot(h, w_ref[2], "nn")

        @pl.when(f == nf - 1)
        def _():
            o_ref[...] = x_ref[...] + 0.5 * acc[...]

    row = pl.BlockSpec((tm, D), lambda i, f: (i, 0))
    return pl.pallas_call(
        body, name=name, grid=(S // tm, nf),
        in_specs=[row, row, pl.BlockSpec((3, tf, D), lambda i, f: (0, f, 0))],
        out_specs=[row, pl.BlockSpec((2, tm, tf), lambda i, f: (0, i, f))],
        out_shape=[jax.ShapeDtypeStruct((S, D), F32), jax.ShapeDtypeStruct((2, S, F), BF16)],
        scratch_shapes=[pltpu.VMEM((tm, D), F32)],
        compiler_params=_params("parallel", "arbitrary"),
    )(x, xn, w)


def _ffn_bwd_dx(dy, w, gu, name):
    S, D = dy.shape
    F = w.shape[1]
    tm, tf = _tile(S, 512, SUBLANE_BF16), _tile(F, 512, LANE)
    nf = F // tf

    def body(dy_ref, w_ref, gu_ref, dxn_ref, dout_ref, t_ref, acc):
        f = pl.program_id(1)

        @pl.when(f == 0)
        def _():
            acc[...] = jnp.zeros_like(acc)
            dout_ref[...] = (0.5 * dy_ref[...]).astype(BF16)

        dh = _dot(dout_ref[...], w_ref[2], "nt")
        g = gu_ref[0].astype(F32)
        u = gu_ref[1].astype(F32)
        sig = jax.nn.sigmoid(g)
        silu = g * sig
        dg = (dh * u * (sig * (1.0 + g * (1.0 - sig)))).astype(BF16)
        du = (dh * silu).astype(BF16)
        t_ref[0] = dg
        t_ref[1] = du
        t_ref[2] = (silu * u).astype(BF16)
        acc[...] += _dot(dg, w_ref[0], "nn") + _dot(du, w_ref[1], "nn")

        @pl.when(f == nf - 1)
        def _():
            dxn_ref[...] = acc[...]

    row = pl.BlockSpec((tm, D), lambda i, f: (i, 0))
    return pl.pallas_call(
        body, name=name, grid=(S // tm, nf),
        in_specs=[row, pl.BlockSpec((3, tf, D), lambda i, f: (0, f, 0)),
                  pl.BlockSpec((2, tm, tf), lambda i, f: (0, i, f))],
        out_specs=[row, row, pl.BlockSpec((3, tm, tf), lambda i, f: (0, i, f))],
        out_shape=[jax.ShapeDtypeStruct((S, D), F32), jax.ShapeDtypeStruct((S, D), BF16),
                   jax.ShapeDtypeStruct((3, S, F), BF16)],
        scratch_shapes=[pltpu.VMEM((tm, D), F32)],
        compiler_params=_params("parallel", "arbitrary"),
    )(dy, w, gu)


def _ffn_bwd_dw(xn, dout, t, name):
    S, D = xn.shape
    F = t.shape[2]
    ts, tf = _tile(S, 512, LANE), _tile(F, 512, LANE)
    ns = S // ts

    def body(xn_ref, dout_ref, t_ref, dw_ref, acc):
        s = pl.program_id(1)

        @pl.when(s == 0)
        def _():
            acc[...] = jnp.zeros_like(acc)

        xnv = xn_ref[...]
        acc[0] += _dot(t_ref[0], xnv, "tn")
        acc[1] += _dot(t_ref[1], xnv, "tn")
        acc[2] += _dot(t_ref[2], dout_ref[...], "tn")

        @pl.when(s == ns - 1)
        def _():
            dw_ref[...] = acc[...].astype(BF16)

    row = pl.BlockSpec((ts, D), lambda f, s: (s, 0))
    return pl.pallas_call(
        body, name=name, grid=(F // tf, ns),
        in_specs=[row, row, pl.BlockSpec((3, ts, tf), lambda f, s: (0, s, f))],
        out_specs=pl.BlockSpec((3, tf, D), lambda f, s: (0, f, 0)),
        out_shape=jax.ShapeDtypeStruct((3, F, D), BF16),
        scratch_shapes=[pltpu.VMEM((3, tf, D), F32)],
        compiler_params=_params("parallel", "arbitrary"),
    )(xn, dout, t)


def _loss_grad(y, target, name):
    S, D = y.shape
    tm = _tile(S, 512, 8)

    def body(y_ref, t_ref, dy_ref, l_ref):
        @pl.when(pl.program_id(0) == 0)
        def _():
            l_ref[...] = jnp.zeros_like(l_ref)

        e = y_ref[...] - t_ref[...]
        dy_ref[...] = e * (1.0 / D)
        l_ref[...] += 0.5 * jnp.sum(jnp.sum(e * e, axis=-1, keepdims=True) * (1.0 / D))

    row = pl.BlockSpec((tm, D), lambda i: (i, 0))
    one = pl.BlockSpec((8, LANE), lambda i: (0, 0))
    return pl.pallas_call(
        body, name=name, grid=(S // tm,),
        in_specs=[row, row], out_specs=[row, one],
        out_shape=[jax.ShapeDtypeStruct((S, D), F32), jax.ShapeDtypeStruct((8, LANE), F32)],
        compiler_params=_params("arbitrary"),
    )(y, target)


def _adamw(w, g, m, v, name):
    shape = w.shape
    C = shape[-1]
    R = math.prod(shape[:-1])
    tr = _tile(R, max(8, (1 << 19) // C // 8 * 8), 8)
    c1 = 1.0 / (1.0 - ADAM_B1 ** ADAM_STEP)
    c2 = 1.0 / (1.0 - ADAM_B2 ** ADAM_STEP)

    def body(w_ref, g_ref, m_ref, v_ref, d_ref, nm_ref, nv_ref):
        gv = g_ref[...]
        nm = ADAM_B1 * m_ref[...] + (1.0 - ADAM_B1) * gv
        nv = ADAM_B2 * v_ref[...] + (1.0 - ADAM_B2) * (gv * gv)
        nm_ref[...] = nm
        nv_ref[...] = nv
        d_ref[...] = -ADAM_LR * ((nm * c1) / (jnp.sqrt(nv * c2) + ADAM_EPS) + ADAM_WD * w_ref[...])

    blk = pl.BlockSpec((tr, C), lambda i: (i, 0))
    sds = jax.ShapeDtypeStruct((R, C), F32)
    outs = pl.pallas_call(
        body, name=name, grid=(R // tr,),
        in_specs=[blk] * 4, out_specs=[blk] * 3, out_shape=[sds] * 3,
        compiler_params=_params("parallel"),
    )(*(a.reshape(R, C) for a in (w, g, m, v)))
    return tuple(o.reshape(shape) for o in outs)


def _me():
    return lax.axis_index("x"), lax.axis_index("y"), lax.axis_index("c")


def _all_gather(shards, name):
    na = len(shards)

    def body(*refs):
        ins, outs = refs[:na], refs[na:2 * na]
        send_sems, recv_sems, local_sems = refs[2 * na:]
        x, y, c = _me()
        me, sibling = (x, y, c), (x, y, 1 - c)
        chips = [(1 - x, y), (x, 1 - y), (1 - x, 1 - y)]

        def blk(a, p):
            return outs[a].at[:, 4 * p[0] + 2 * p[1] + p[2]]

        def copy(a, k, block, to, src=None):
            return pltpu.make_async_remote_copy(
                src_ref=blk(a, block) if src is None else src, dst_ref=blk(a, block),
                send_sem=send_sems.at[a * 7 + k], recv_sem=recv_sems.at[a * 7 + k],
                device_id=to, device_id_type=MESH)

        mine, first, passed = [], [], []
        for a in range(na):
            mine.append(pltpu.make_async_copy(ins[a], blk(a, me), local_sems.at[a]))
            mine[a].start()
            first.append([copy(a, 0, me, sibling, src=ins[a])]
                         + [copy(a, 1 + j, me, (*chip, c), src=ins[a]) for j, chip in enumerate(chips)])
            for cp in first[a]:
                cp.start()
        for a in range(na):
            passed.append([copy(a, 4 + j, (*chip, c), sibling) for j, chip in enumerate(chips)])
            for j, chip in enumerate(chips):
                copy(a, 1 + j, (*chip, c), me).wait_recv()
                passed[a][j].start()
        for a in range(na):
            copy(a, 0, sibling, me).wait_recv()
            for j, chip in enumerate(chips):
                copy(a, 4 + j, (*chip, 1 - c), me).wait_recv()
        for a in range(na):
            for cp in first[a] + passed[a]:
                cp.wait_send()
            mine[a].wait()

    return pl.pallas_call(
        body, name=name,
        in_specs=[ANY] * na, out_specs=[ANY] * na,
        out_shape=[jax.ShapeDtypeStruct((s.shape[0], N_DEV) + s.shape[1:], s.dtype) for s in shards],
        scratch_shapes=[pltpu.SemaphoreType.DMA((7 * na,)), pltpu.SemaphoreType.DMA((7 * na,)),
                        pltpu.SemaphoreType.DMA((na,))],
    )(*shards)


def _exchange_core_halves(grads, name):
    na = len(grads)

    def body(*refs):
        ins, outs = refs[:na], refs[na:2 * na]
        send_sems, recv_sems = refs[2 * na:]
        x, y, c = _me()
        copies = [pltpu.make_async_remote_copy(
            src_ref=ins[a].at[:, :, 1 - c], dst_ref=outs[a],
            send_sem=send_sems.at[a], recv_sem=recv_sems.at[a],
            device_id=(x, y, 1 - c), device_id_type=MESH) for a in range(na)]
        for cp in copies:
            cp.start()
        for cp in copies:
            cp.wait()

    return pl.pallas_call(
        body, name=name,
        in_specs=[ANY] * na, out_specs=[ANY] * na,
        out_shape=[jax.ShapeDtypeStruct(g.shape[:2] + g.shape[3:], g.dtype) for g in grads],
        scratch_shapes=[pltpu.SemaphoreType.DMA((na,)), pltpu.SemaphoreType.DMA((na,))],
    )(*grads)


def _add_core_halves(grad, got, c_idx, name):
    n, nk, _, r, C = grad.shape
    tr = _tile(r, 1024, SUBLANE_BF16)

    def body(c_ref, g_ref, r_ref, o_ref):
        o_ref[...] = (g_ref[...].astype(F32) + r_ref[...].astype(F32)).astype(BF16)

    return pl.pallas_call(
        body, name=name,
        grid_spec=pltpu.PrefetchScalarGridSpec(
            num_scalar_prefetch=1, grid=(n, nk, r // tr),
            in_specs=[pl.BlockSpec((None, None, None, tr, C), lambda i, k, t, c: (i, k, c[0], t, 0)),
                      pl.BlockSpec((None, None, tr, C), lambda i, k, t, c: (i, k, t, 0))],
            out_specs=pl.BlockSpec((None, None, tr, C), lambda i, k, t, c: (i, k, t, 0))),
        out_shape=jax.ShapeDtypeStruct((n, nk, r, C), BF16),
        compiler_params=_params("parallel", "parallel", "parallel"),
    )(c_idx, grad, got)


def _exchange_chip_blocks(sums, name):
    na = len(sums)

    def body(*refs):
        ins, outs = refs[:na], refs[na:2 * na]
        send_sems, recv_sems = refs[2 * na:]
        x, y, c = _me()
        chips = [(1 - x, y), (x, 1 - y), (1 - x, 1 - y)]
        copies = [pltpu.make_async_remote_copy(
            src_ref=ins[a].at[:, 2 * chip[0] + chip[1]], dst_ref=outs[a].at[j],
            send_sem=send_sems.at[3 * a + j], recv_sem=recv_sems.at[3 * a + j],
            device_id=(*chip, c), device_id_type=MESH)
            for a in range(na) for j, chip in enumerate(chips)]
        for cp in copies:
            cp.start()
        for cp in copies:
            cp.wait()

    return pl.pallas_call(
        body, name=name,
        in_specs=[ANY] * na, out_specs=[ANY] * na,
        out_shape=[jax.ShapeDtypeStruct((3, s.shape[0]) + s.shape[2:], s.dtype) for s in sums],
        scratch_shapes=[pltpu.SemaphoreType.DMA((3 * na,)), pltpu.SemaphoreType.DMA((3 * na,))],
    )(*sums)


def _sum_chip_blocks(sums, got, k_idx, name):
    n, _, r, C = sums.shape
    tr = _tile(r, 512, SUBLANE_BF16)

    def body(k_ref, s_ref, r_ref, o_ref):
        acc = s_ref[...].astype(F32)
        for j in range(3):
            acc = acc + r_ref[j].astype(F32)
        o_ref[...] = acc

    return pl.pallas_call(
        body, name=name,
        grid_spec=pltpu.PrefetchScalarGridSpec(
            num_scalar_prefetch=1, grid=(n, r // tr),
            in_specs=[pl.BlockSpec((None, None, tr, C), lambda i, t, k: (i, k[0], t, 0)),
                      pl.BlockSpec((3, None, tr, C), lambda i, t, k: (0, i, t, 0))],
            out_specs=pl.BlockSpec((None, tr, C), lambda i, t, k: (i, t, 0))),
        out_shape=jax.ShapeDtypeStruct((n, r, C), F32),
        compiler_params=_params("parallel", "parallel"),
    )(k_idx, sums, got)


def _all_reduce_small(v, name):
    R = v.shape[0]

    def body(v_ref, o_ref, buf, send_sems, recv_sems):
        x, y, c = _me()
        me = 4 * x + 2 * y + c
        buf[me] = v_ref[...]
        copies = []
        for k in range(1, N_DEV):
            peer = (x ^ (k >> 2), y ^ ((k >> 1) & 1), c ^ (k & 1))
            copies.append(pltpu.make_async_remote_copy(
                src_ref=v_ref, dst_ref=buf.at[me],
                send_sem=send_sems.at[k - 1], recv_sem=recv_sems.at[k - 1],
                device_id=peer, device_id_type=MESH))
        for cp in copies:
            cp.start()
        for cp in copies:
            cp.wait()
        acc = buf[0]
        for d in range(1, N_DEV):
            acc = acc + buf[d]
        o_ref[...] = acc

    vm = pl.BlockSpec(memory_space=pltpu.VMEM)
    return pl.pallas_call(
        body, name=name, in_specs=[vm], out_specs=vm,
        out_shape=jax.ShapeDtypeStruct((R, LANE), F32),
        scratch_shapes=[pltpu.VMEM((N_DEV, R, LANE), F32),
                        pltpu.SemaphoreType.DMA((N_DEV - 1,)), pltpu.SemaphoreType.DMA((N_DEV - 1,))],
        compiler_params=pltpu.CompilerParams(vmem_limit_bytes=VMEM_LIMIT_BYTES),
    )(v)


def _pack_rows(parts):
    flat = jnp.concatenate([p.reshape(-1).astype(F32) for p in parts])
    n = flat.shape[0]
    rows = -(-n // (8 * LANE)) * 8
    flat = jnp.pad(flat, (0, rows * LANE - n))
    return flat.reshape(rows, LANE)


def _unpack_rows(packed, shapes):
    flat = packed.reshape(-1)
    out, off = [], 0
    for s in shapes:
        n = math.prod(s)
        out.append(flat[off:off + n].reshape(s))
        off += n
    return out


CONV_HALO = 32


def _conv_fwd(u, conv_w, conv_b, cn_g, cn_b, name):
    S = u.shape[0]
    C = conv_w.shape[1]
    T = _tile(S, 256, CONV_HALO)
    hb = T // CONV_HALO

    def body(av_ref, ag_ref, pv_ref, pg_ref, w_ref, b_ref, g_ref, bb_ref, out_ref, y_ref, scr):
        i = pl.program_id(0)
        prev = pv_ref[...] * jax.nn.sigmoid(pg_ref[...])
        scr[0:CONV_HALO, :] = jnp.where(i > 0, prev, 0.0)
        scr[CONV_HALO:CONV_HALO + T, :] = av_ref[...] * jax.nn.sigmoid(ag_ref[...])
        acc = jnp.broadcast_to(b_ref[...], (T, C))
        for j in range(CONV_WIDTH):
            acc = acc + w_ref[j:j + 1, :] * scr[pl.ds(CONV_HALO - (CONV_WIDTH - 1) + j, T), :]
        y_ref[...] = acc
        mu = jnp.mean(acc, axis=-1, keepdims=True)
        xc = acc - mu
        var = jnp.mean(xc * xc, axis=-1, keepdims=True)
        ln = xc * lax.rsqrt(var + EPS) * g_ref[...] + bb_ref[...]
        out_ref[...] = (ln * jax.nn.sigmoid(ln)).astype(BF16)

    def cur(cb):
        return pl.BlockSpec((T, C), lambda i: (i, cb))

    def halo(cb):
        return pl.BlockSpec((CONV_HALO, C), lambda i: (jnp.maximum(i * hb - 1, 0), cb))

    vec = pl.BlockSpec((1, C), lambda i: (0, 0))
    return pl.pallas_call(
        body, name=name, grid=(S // T,),
        in_specs=[cur(0), cur(1), halo(0), halo(1), pl.BlockSpec((CONV_WIDTH, C), lambda i: (0, 0)), vec, vec, vec],
        out_specs=[pl.BlockSpec((T, C), lambda i: (i, 0))] * 2,
        out_shape=[jax.ShapeDtypeStruct((S, C), BF16), jax.ShapeDtypeStruct((S, C), F32)],
        scratch_shapes=[pltpu.VMEM((T + CONV_HALO, C), F32)],
        compiler_params=_params("parallel"),
    )(u, u, u, u, conv_w, conv_b, cn_g, cn_b)


def _conv_bwd_norm(dz, y, cn_g, cn_b, name):
    S, C = y.shape
    T = _tile(S, 256, 8)

    def body(dz_ref, y_ref, g_ref, bb_ref, dy_ref, dg_ref, db_ref):
        @pl.when(pl.program_id(0) == 0)
        def _():
            dg_ref[...] = jnp.zeros_like(dg_ref)
            db_ref[...] = jnp.zeros_like(db_ref)

        yv = y_ref[...]
        mu = jnp.mean(yv, axis=-1, keepdims=True)
        xc = yv - mu
        rstd = lax.rsqrt(jnp.mean(xc * xc, axis=-1, keepdims=True) + EPS)
        xh = xc * rstd
        ln = xh * g_ref[...] + bb_ref[...]
        sg = jax.nn.sigmoid(ln)
        dln = dz_ref[...] * (sg * (1.0 + ln * (1.0 - sg)))
        dg_ref[...] += jnp.sum(dln * xh, axis=0, keepdims=True)
        db_ref[...] += jnp.sum(dln, axis=0, keepdims=True)
        dxh = dln * g_ref[...]
        dy_ref[...] = rstd * (dxh - jnp.mean(dxh, axis=-1, keepdims=True)
                              - xh * jnp.mean(dxh * xh, axis=-1, keepdims=True))

    row = pl.BlockSpec((T, C), lambda i: (i, 0))
    vec = pl.BlockSpec((1, C), lambda i: (0, 0))
    return pl.pallas_call(
        body, name=name, grid=(S // T,),
        in_specs=[row, row, vec, vec], out_specs=[row, vec, vec],
        out_shape=[jax.ShapeDtypeStruct((S, C), F32), jax.ShapeDtypeStruct((1, C), F32),
                   jax.ShapeDtypeStruct((1, C), F32)],
        compiler_params=_params("arbitrary"),
    )(dz, y, cn_g, cn_b)


def _conv_bwd_taps(u, dy, conv_w, name):
    S, C = dy.shape
    T = _tile(S, 256, CONV_HALO)
    hb = T // CONV_HALO
    nt = S // T
    W1 = CONV_WIDTH - 1

    def body(av_ref, ag_ref, pv_ref, pg_ref, dy_ref, dn_ref, w_ref, dv_ref, dg_ref, dw_ref, db_ref, a_scr, d_scr):
        i = pl.program_id(0)

        @pl.when(i == 0)
        def _():
            dw_ref[...] = jnp.zeros_like(dw_ref)
            db_ref[...] = jnp.zeros_like(db_ref)

        av, sg = av_ref[...], jax.nn.sigmoid(ag_ref[...])
        prev = pv_ref[...] * jax.nn.sigmoid(pg_ref[...])
        a_scr[0:CONV_HALO, :] = jnp.where(i > 0, prev, 0.0)
        a_scr[CONV_HALO:CONV_HALO + T, :] = av * sg
        dyv = dy_ref[...]
        d_scr[0:T, :] = dyv
        d_scr[T:T + CONV_HALO, :] = jnp.where(i < nt - 1, dn_ref[...], 0.0)
        da = jnp.zeros((T, C), F32)
        for j in range(CONV_WIDTH):
            da = da + w_ref[j:j + 1, :] * d_scr[pl.ds(W1 - j, T), :]
            dw_ref[j:j + 1, :] += jnp.sum(dyv * a_scr[pl.ds(CONV_HALO - W1 + j, T), :], axis=0, keepdims=True)
        db_ref[...] += jnp.sum(dyv, axis=0, keepdims=True)
        dv_ref[...] = (da * sg).astype(BF16)
        dg_ref[...] = (da * av * sg * (1.0 - sg)).astype(BF16)

    def cur(cb):
        return pl.BlockSpec((T, C), lambda i: (i, cb))

    def halo(cb):
        return pl.BlockSpec((CONV_HALO, C), lambda i: (jnp.maximum(i * hb - 1, 0), cb))

    nxt = pl.BlockSpec((CONV_HALO, C), lambda i: (jnp.minimum((i + 1) * hb, S // CONV_HALO - 1), 0))
    row = pl.BlockSpec((T, C), lambda i: (i, 0))
    return pl.pallas_call(
        body, name=name, grid=(nt,),
        in_specs=[cur(0), cur(1), halo(0), halo(1), row, nxt, pl.BlockSpec((CONV_WIDTH, C), lambda i: (0, 0))],
        out_specs=[row, row, pl.BlockSpec((CONV_HALO, C), lambda i: (0, 0)), pl.BlockSpec((1, C), lambda i: (0, 0))],
        out_shape=[jax.ShapeDtypeStruct((S, C), BF16), jax.ShapeDtypeStruct((S, C), BF16),
                   jax.ShapeDtypeStruct((CONV_HALO, C), F32), jax.ShapeDtypeStruct((1, C), F32)],
        scratch_shapes=[pltpu.VMEM((T + CONV_HALO, C), F32), pltpu.VMEM((T + CONV_HALO, C), F32)],
        compiler_params=_params("arbitrary"),
    )(u, u, u, u, dy, dy, conv_w)


def _rope_tables(S):
    half = HEAD_DIM // 2
    inv = jnp.exp(-math.log(ROPE_THETA) * jnp.arange(half, dtype=F32) / half)
    ang = jnp.arange(S, dtype=jnp.int32).astype(F32)[:, None] * inv[None, :]
    cos, sin = jnp.cos(ang), jnp.sin(ang)
    return jnp.concatenate([cos, cos], axis=1), jnp.concatenate([-sin, sin], axis=1)


def _qkv_prep(u, qn_g, kn_g, cos, sin, cb0, name):
    S = u.shape[0]
    A = (u.shape[1] // (cb0 + 3))
    H = A // HEAD_DIM
    T = _tile(S, 256, SUBLANE_BF16)
    scale = HEAD_DIM ** -0.5

    def body(q_ref, k_ref, v_ref, qg_ref, kg_ref, cos_ref, sin_ref, qo_ref, ko_ref, vo_ref):
        cosv, sinv = cos_ref[...], sin_ref[...]
        for h in range(H):
            sl = slice(h * HEAD_DIM, (h + 1) * HEAD_DIM)
            for x_ref, g_ref, o_ref, sc in ((q_ref, qg_ref, qo_ref, scale), (k_ref, kg_ref, ko_ref, 1.0)):
                xv = x_ref[:, sl]
                xn = xv * lax.rsqrt(jnp.mean(xv * xv, axis=-1, keepdims=True) + EPS) * g_ref[...]
                y = xn * cosv + pltpu.roll(xn, HEAD_DIM // 2, 1) * sinv
                o_ref[:, sl] = (y * sc).astype(BF16)
        vo_ref[...] = v_ref[...].astype(BF16)

    def col(cb):
        return pl.BlockSpec((T, A), lambda i: (i, cb))

    vec = pl.BlockSpec((1, HEAD_DIM), lambda i: (0, 0))
    tab = pl.BlockSpec((T, HEAD_DIM), lambda i: (i, 0))
    out = pl.BlockSpec((T, A), lambda i: (i, 0))
    return pl.pallas_call(
        body, name=name, grid=(S // T,),
        in_specs=[col(cb0), col(cb0 + 1), col(cb0 + 2), vec, vec, tab, tab],
        out_specs=[out] * 3, out_shape=[jax.ShapeDtypeStruct((S, A), BF16)] * 3,
        compiler_params=_params("parallel"),
    )(u, u, u, qn_g, kn_g, cos, sin)


def _qkv_prep_bwd(u, dqs, dks, dvs, qn_g, kn_g, cos, sin, cb0, name):
    S = u.shape[0]
    A = dqs[0].shape[1]
    H = A // HEAD_DIM
    T = _tile(S, 256, SUBLANE_BF16)
    nb = len(dqs)
    scale = HEAD_DIM ** -0.5

    def body(*refs):
        q_ref, k_ref, qg_ref, kg_ref, cos_ref, sin_ref = refs[:6]
        dq_refs, dk_refs, dv_refs = refs[6:6 + nb], refs[6 + nb:6 + 2 * nb], refs[6 + 2 * nb:6 + 3 * nb]
        dqo_ref, dko_ref, dvo_ref, dqg_ref, dkg_ref = refs[6 + 3 * nb:]

        @pl.when(pl.program_id(0) == 0)
        def _():
            dqg_ref[...] = jnp.zeros_like(dqg_ref)
            dkg_ref[...] = jnp.zeros_like(dkg_ref)

        cosv, sinv = cos_ref[...], sin_ref[...]
        for h in range(H):
            sl = slice(h * HEAD_DIM, (h + 1) * HEAD_DIM)
            for x_ref, g_ref, d_refs, o_ref, dg_ref, sc in ((q_ref, qg_ref, dq_refs, dqo_ref, dqg_ref, scale),
                                                          (k_ref, kg_ref, dk_refs, dko_ref, dkg_ref, 1.0)):
                dy = d_refs[0][:, sl]
                for r in d_refs[1:]:
                    dy = dy + r[:, sl]
                dy = dy * sc
                dxn = dy * cosv + pltpu.roll(dy * sinv, HEAD_DIM // 2, 1)
                xv = x_ref[:, sl]
                r = lax.rsqrt(jnp.mean(xv * xv, axis=-1, keepdims=True) + EPS)
                xh = xv * r
                dg_ref[...] += jnp.sum(dxn * xh, axis=0, keepdims=True)
                dxh = dxn * g_ref[...]
                o_ref[:, sl] = (r * (dxh - xh * jnp.mean(dxh * xh, axis=-1, keepdims=True))).astype(BF16)
        dv = dv_refs[0][...]
        for r in dv_refs[1:]:
            dv = dv + r[...]
        dvo_ref[...] = dv.astype(BF16)

    def col(cb):
        return pl.BlockSpec((T, A), lambda i: (i, cb))

    vec = pl.BlockSpec((1, HEAD_DIM), lambda i: (0, 0))
    tab = pl.BlockSpec((T, HEAD_DIM), lambda i: (i, 0))
    row = pl.BlockSpec((T, A), lambda i: (i, 0))
    return pl.pallas_call(
        body, name=name, grid=(S // T,),
        in_specs=[col(cb0), col(cb0 + 1), vec, vec, tab, tab] + [row] * (3 * nb),
        out_specs=[row, row, row, vec, vec],
        out_shape=[jax.ShapeDtypeStruct((S, A), BF16)] * 3 + [jax.ShapeDtypeStruct((1, HEAD_DIM), F32)] * 2,
        compiler_params=_params("arbitrary"),
    )(u, u, qn_g, kn_g, cos, sin, *dqs, *dks, *dvs)


ATT_TILE = 256
NEG = -1e30


def _attn_bias(tile):
    span = max(window for window, _ in DIL_PATTERNS)
    nw = -(-span // tile) + 1
    dist = (jnp.arange(nw)[:, None, None] * tile + jnp.arange(tile)[None, :, None] - jnp.arange(tile)[None, None, :])
    mult = sum(((dist >= 0) & (dist <= window) & (dist % dil == 0)).astype(F32) for window, dil in DIL_PATTERNS)
    return jnp.where(mult > 0, jnp.log(jnp.maximum(mult, 1.0)), NEG)


def _attn_fwd(q, k, v, bias, name):
    S, A = q.shape
    H = A // HEAD_DIM
    nw, T, _ = bias.shape
    nq = S // T

    def body(q_ref, k_ref, v_ref, b_ref, ob_ref, of_ref, l_ref, s_scr):
        i = pl.program_id(1)
        qv = q_ref[...]
        mx = jnp.full((T, 1), NEG, F32)
        for w in range(nw):
            blk = i - w
            start = pl.multiple_of(jnp.maximum(blk, 0) * T, T)
            s = _dot(qv, k_ref[pl.ds(start, T), :], "nt") + b_ref[w] + jnp.where(blk >= 0, 0.0, NEG)
            s_scr[w] = s
            mx = jnp.maximum(mx, jnp.max(s, axis=-1, keepdims=True))
        den = jnp.zeros((T, 1), F32)
        o = jnp.zeros((T, HEAD_DIM), F32)
        for w in range(nw):
            start = pl.multiple_of(jnp.maximum(i - w, 0) * T, T)
            p = jnp.exp(s_scr[w] - mx)
            den = den + jnp.sum(p, axis=-1, keepdims=True)
            o = o + _dot(p.astype(BF16), v_ref[pl.ds(start, T), :], "nn")
        o = o / den
        ob_ref[...] = o.astype(BF16)
        of_ref[...] = o
        l_ref[...] = mx + jnp.log(den)

    blk = pl.BlockSpec((T, HEAD_DIM), lambda h, i: (i, h))
    full = pl.BlockSpec((S, HEAD_DIM), lambda h, i: (0, h))
    return pl.pallas_call(
        body, name=name, grid=(H, nq),
        in_specs=[blk, full, full, pl.BlockSpec((nw, T, T), lambda h, i: (0, 0, 0))],
        out_specs=[blk, blk, pl.BlockSpec((None, T, 1), lambda h, i: (h, i, 0))],
        out_shape=[jax.ShapeDtypeStruct((S, A), BF16), jax.ShapeDtypeStruct((S, A), F32),
                   jax.ShapeDtypeStruct((H, S, 1), F32)],
        scratch_shapes=[pltpu.VMEM((nw, T, T), F32)],
        compiler_params=_params("parallel", "arbitrary"),
    )(q, k, v, bias)


def _attn_dq(q, k, v, dz, cb0, o, lse, bias, name):
    S, A = q.shape
    H = A // HEAD_DIM
    nw, T, _ = bias.shape
    nq = S // T

    def body(q_ref, k_ref, v_ref, do_ref, o_ref, l_ref, b_ref, dq_ref, d_ref):
        i = pl.program_id(1)
        qv, dof = q_ref[...], do_ref[...]
        dov = dof.astype(BF16)
        delta = jnp.sum(dof * o_ref[...], axis=-1, keepdims=True)
        d_ref[...] = delta
        lv = l_ref[...]
        dq = jnp.zeros((T, HEAD_DIM), F32)
        for w in range(nw):
            blk = i - w
            start = pl.multiple_of(jnp.maximum(blk, 0) * T, T)
            kv = k_ref[pl.ds(start, T), :]
            s = _dot(qv, kv, "nt") + b_ref[w] + jnp.where(blk >= 0, 0.0, NEG)
            p = jnp.exp(s - lv)
            ds = (p * (_dot(dov, v_ref[pl.ds(start, T), :], "nt") - delta)).astype(BF16)
            dq = dq + _dot(ds, kv, "nn")
        dq_ref[...] = dq

    blk = pl.BlockSpec((T, HEAD_DIM), lambda h, i: (i, h))
    full = pl.BlockSpec((S, HEAD_DIM), lambda h, i: (0, h))
    col = pl.BlockSpec((None, T, 1), lambda h, i: (h, i, 0))
    return pl.pallas_call(
        body, name=name, grid=(H, nq),
        in_specs=[blk, full, full, pl.BlockSpec((T, HEAD_DIM), lambda h, i: (i, cb0 + h)), blk, col,
                  pl.BlockSpec((nw, T, T), lambda h, i: (0, 0, 0))],
        out_specs=[blk, col],
        out_shape=[jax.ShapeDtypeStruct((S, A), F32), jax.ShapeDtypeStruct((H, S, 1), F32)],
        compiler_params=_params("parallel", "arbitrary"),
    )(q, k, v, dz, o, lse, bias)


def _attn_dkv(q, k, v, dz, cb0, lse, delta, bias, name):
    S, A = q.shape
    H = A // HEAD_DIM
    nw, T, _ = bias.shape
    nq = S // T

    def body(k_ref, v_ref, q_ref, do_ref, l_ref, d_ref, b_ref, dk_ref, dv_ref):
        m = pl.program_id(1)
        kv, vv = k_ref[...], v_ref[...]
        dk = jnp.zeros((T, HEAD_DIM), F32)
        dv = jnp.zeros((T, HEAD_DIM), F32)
        for w in range(nw):
            blk = m + w
            start = pl.multiple_of(jnp.minimum(blk, nq - 1) * T, T)
            qv = q_ref[pl.ds(start, T), :]
            dov = do_ref[pl.ds(start, T), :].astype(BF16)
            s = _dot(qv, kv, "nt") + b_ref[w] + jnp.where(blk < nq, 0.0, NEG)
            p = jnp.exp(s - l_ref[pl.ds(start, T), :])
            dv = dv + _dot(p.astype(BF16), dov, "tn")
            ds = (p * (_dot(dov, vv, "nt") - d_ref[pl.ds(start, T), :])).astype(BF16)
            dk = dk + _dot(ds, qv, "tn")
        dk_ref[...] = dk
        dv_ref[...] = dv

    blk = pl.BlockSpec((T, HEAD_DIM), lambda h, m: (m, h))
    full = pl.BlockSpec((S, HEAD_DIM), lambda h, m: (0, h))
    col = pl.BlockSpec((None, S, 1), lambda h, m: (h, 0, 0))
    sds = jax.ShapeDtypeStruct((S, A), F32)
    return pl.pallas_call(
        body, name=name, grid=(H, nq),
        in_specs=[blk, blk, full, pl.BlockSpec((S, HEAD_DIM), lambda h, m: (0, cb0 + h)), col, col,
                  pl.BlockSpec((nw, T, T), lambda h, m: (0, 0, 0))],
        out_specs=[blk, blk], out_shape=[sds, sds],
        compiler_params=_params("parallel", "arbitrary"),
    )(k, v, q, dz, lse, delta, bias)


def _even_mixer(u, conv_w, conv_b, cn_g, cn_b, qn_g, kn_g, tag):
    S = u.shape[0]
    C = conv_w.shape[1]
    A = (u.shape[1] - 2 * C) // 3
    assert A == C, "column-block addressing of u assumes equal conv and attention widths"
    T = _tile(S, ATT_TILE, LANE)
    cos, sin = _rope_tables(S)
    bias = _attn_bias(T)
    a_out, y = _conv_fwd(u, conv_w, conv_b, cn_g, cn_b, "conv_fwd" + tag)
    q, k, v = _qkv_prep(u, qn_g, kn_g, cos, sin, 2, "qkv_prep" + tag)
    ob, of, lse = _attn_fwd(q, k, v, bias, "attn_fwd" + tag)
    z = jnp.concatenate([a_out, ob], axis=1)

    def backward(dz):
        dy, d_cn_g, d_cn_b = _conv_bwd_norm(dz, y, cn_g, cn_b, "conv_bwd_norm" + tag)
        d_val, d_gate, d_w, d_b = _conv_bwd_taps(u, dy, conv_w, "conv_bwd_taps" + tag)
        dqp, delta = _attn_dq(q, k, v, dz, C // HEAD_DIM, of, lse, bias, "attn_dq" + tag)
        dkp, dvp = _attn_dkv(q, k, v, dz, C // HEAD_DIM, lse, delta, bias, "attn_dkv" + tag)
        dq, dk, dv, d_qn, d_kn = _qkv_prep_bwd(u, [dqp], [dkp], [dvp], qn_g, kn_g, cos, sin, 2, "qkv_prep_bwd" + tag)
        du = jnp.concatenate([d_val, d_gate, dq, dk, dv], axis=1)
        return du, [d_w[:CONV_WIDTH], d_b[0], d_cn_g[0], d_cn_b[0], d_qn[0], d_kn[0]]

    return z, backward


_LEVELS = (128, 64, 32, 16, 8, 4, 2, 1)


def _chunk_cumsum(g, rows, reverse=False):
    C = g.shape[0]
    d = 1
    while d < C:
        if reverse:
            g = g + jnp.where(rows < C - d, pltpu.roll(g, C - d, 0), 0.0)
        else:
            g = g + jnp.where(rows >= d, pltpu.roll(g, d, 0), 0.0)
        d *= 2
    return g


def _level_ref(b, b_scr, rows, m):
    C = b.shape[0]
    if m >= 8:
        pieces = [jnp.broadcast_to(b_scr[2 * m * j + m - 1:2 * m * j + m, :], (2 * m, LANE)) for j in range(C // (2 * m))]
        return pieces[0] if len(pieces) == 1 else jnp.concatenate(pieces, axis=0)
    pos = rows & (2 * m - 1)
    ref = b
    for p in range(2 * m):
        if p != m - 1:
            ref = jnp.where(pos == p, pltpu.roll(b, (p - (m - 1)) % C, 0), ref)
    return ref


def _level_operands(q, k, b, b_scr, rows, m):
    ref = _level_ref(b, b_scr, rows, m)
    qs = (q * jnp.exp(jnp.minimum(b - ref, 0.0))).astype(BF16)
    ks = (k * jnp.exp(jnp.minimum(ref - b, 0.0))).astype(BF16)
    return qs, ks


def _split2(x):
    hi = x.astype(BF16)
    lo = (x - hi.astype(F32)).astype(BF16)
    return jnp.concatenate([hi, lo], axis=1)


def _level_mask(tt, ss, m):
    x = tt ^ ss
    return (tt > ss) & (x >= m) & (x < 2 * m)


def _hgrn_gates(qz, fz, la, lc, oml):
    sq = jax.nn.sigmoid(qz)
    q = qz * sq
    s = jax.nn.sigmoid(fz)
    c = lc + jnp.minimum(fz, 0.0) - jnp.log(1.0 + jnp.exp(-jnp.abs(fz)))
    mx = jnp.maximum(la, c)
    g = mx + jnp.log(1.0 + jnp.exp(-jnp.abs(la - c)))
    k = oml * (1.0 - s)
    return q, sq, k, s, g, c


def _hgrn_fwd(u, la, lc, oml, gn_g, name):
    S = u.shape[0]
    W = u.shape[1] // 4
    H = W // HGRN_KDIM
    C = min(HGRN_CHUNK, S)
    nc = S // C
    levels = [m for m in _LEVELS if m < C]

    def body(qz_ref, fz_ref, iz_ref, gz_ref, la_ref, lc_ref, oml_ref, gn_ref,
             z_ref, o_ref, a_ref, st_ref, state, b_scr):
        @pl.when(pl.program_id(1) == 0)
        def _():
            state[...] = jnp.zeros_like(state)

        rows = lax.broadcasted_iota(jnp.int32, (C, LANE), 0)
        tt = lax.broadcasted_iota(jnp.int32, (C, C), 0)
        ss = lax.broadcasted_iota(jnp.int32, (C, C), 1)
        q, _, k, _, g, _ = _hgrn_gates(qz_ref[...], fz_ref[...], la_ref[...], lc_ref[...], oml_ref[...])
        v = iz_ref[...].astype(BF16)
        b = _chunk_cumsum(g, rows)
        b_scr[...] = b
        a = jnp.where(tt == ss, jnp.sum(q * k, axis=-1, keepdims=True), 0.0)
        for m in levels:
            qs, ks = _level_operands(q, k, b, b_scr, rows, m)
            a = jnp.where(_level_mask(tt, ss, m), _dot(qs, ks, "nt"), a)
        ab = a.astype(BF16)
        a_ref[...] = ab
        st = state[...]
        st_ref[...] = st
        o = _dot(ab, v, "nn") + _dot((q * jnp.exp(b)).astype(BF16), st.astype(BF16), "nt")
        bl = b_scr[C - 1:C, :]
        kh = (k * jnp.exp(bl - b)).astype(BF16)
        state[...] = st * jnp.exp(bl) + _dot(v, kh, "tn")
        o_ref[...] = o
        r = lax.rsqrt(jnp.mean(o * o, axis=-1, keepdims=True) + EPS)
        gz = gz_ref[...]
        z_ref[...] = (o * r * gn_ref[...] * (gz * jax.nn.sigmoid(gz))).astype(BF16)

    def col(off):
        return pl.BlockSpec((C, LANE), lambda h, i: (i, off * H + h))

    vec = pl.BlockSpec((1, LANE), lambda h, i: (0, h))
    tile = pl.BlockSpec((C, LANE), lambda h, i: (i, h))
    return pl.pallas_call(
        body, name=name, grid=(H, nc),
        in_specs=[col(0), col(1), col(2), col(3), vec, vec, vec, vec],
        out_specs=[tile, tile, pl.BlockSpec((None, C, C), lambda h, i: (h, i, 0)),
                   pl.BlockSpec((None, None, LANE, LANE), lambda h, i: (h, i, 0, 0))],
        out_shape=[jax.ShapeDtypeStruct((S, W), BF16), jax.ShapeDtypeStruct((S, W), F32),
                   jax.ShapeDtypeStruct((H, S, C), BF16), jax.ShapeDtypeStruct((H, nc, LANE, LANE), F32)],
        scratch_shapes=[pltpu.VMEM((LANE, LANE), F32), pltpu.VMEM((C, LANE), F32)],
        compiler_params=_params("parallel", "arbitrary"),
    )(u, u, u, u, la, lc, oml, gn_g)


def _hgrn_bwd(u, la, lc, oml, gn_g, o, a, st, dz, name):
    S = u.shape[0]
    W = u.shape[1] // 4
    H = W // HGRN_KDIM
    C = min(HGRN_CHUNK, S)
    nc = S // C
    levels = [m for m in _LEVELS if m < C]

    def body(qz_ref, fz_ref, iz_ref, gz_ref, la_ref, lc_ref, oml_ref, gn_ref, o_ref, a_ref, st_ref, dz_ref,
             dqz_ref, dfz_ref, diz_ref, dgz_ref, dla_ref, dlc_ref, doml_ref, dgn_ref, dstate, b_scr):
        @pl.when(pl.program_id(1) == 0)
        def _():
            dstate[...] = jnp.zeros_like(dstate)
            dla_ref[...] = jnp.zeros_like(dla_ref)
            dlc_ref[...] = jnp.zeros_like(dlc_ref)
            doml_ref[...] = jnp.zeros_like(doml_ref)
            dgn_ref[...] = jnp.zeros_like(dgn_ref)

        rows = lax.broadcasted_iota(jnp.int32, (C, LANE), 0)
        tt = lax.broadcasted_iota(jnp.int32, (C, C), 0)
        ss = lax.broadcasted_iota(jnp.int32, (C, C), 1)
        la_v, lc_v, oml_v = la_ref[...], lc_ref[...], oml_ref[...]
        qz, fz = qz_ref[...], fz_ref[...]
        q, sq, k, s, g, c = _hgrn_gates(qz, fz, la_v, lc_v, oml_v)
        vf = iz_ref[...]
        v = vf.astype(BF16)

        ov, gz, dzv, gn = o_ref[...], gz_ref[...], dz_ref[...], gn_ref[...]
        r = lax.rsqrt(jnp.mean(ov * ov, axis=-1, keepdims=True) + EPS)
        on = ov * r
        sg = jax.nn.sigmoid(gz)
        silu_g = gz * sg
        dgn_ref[...] += jnp.sum(dzv * on * silu_g, axis=0, keepdims=True)
        dgz_ref[...] = (dzv * on * gn * (sg * (1.0 + gz * (1.0 - sg)))).astype(BF16)
        don = dzv * gn * silu_g
        do_f = r * (don - on * jnp.mean(don * on, axis=-1, keepdims=True))
        do = do_f.astype(BF16)

        b = _chunk_cumsum(g, rows)
        b_scr[...] = b
        bl = b_scr[C - 1:C, :]
        e = jnp.exp(b)
        ebl = jnp.exp(bl)
        ekl = jnp.exp(bl - b)
        qh = q * e
        kh = k * ekl
        st_v = st_ref[...]
        dst = dstate[...]
        dstb = dst.astype(BF16)

        diz_ref[...] = (_dot(a_ref[...], do, "tn") + _dot(kh.astype(BF16), dstb, "nt")).astype(BF16)
        da = _dot(do, v, "nt")
        dqh = _dot(do, st_v.astype(BF16), "nn")
        dkh = _dot(v, dstb, "nn")
        dstate[...] = dst * ebl + _dot(do, qh.astype(BF16), "tn")
        dbl = jnp.sum(dkh * kh, axis=0, keepdims=True) + jnp.sum(dst * st_v, axis=0, keepdims=True) * ebl

        datt = jnp.sum(do_f * vf, axis=-1, keepdims=True)
        dqa = datt * k
        dka = datt * q
        for m in levels:
            ref = _level_ref(b, b_scr, rows, m)
            eu = jnp.exp(jnp.minimum(b - ref, 0.0))
            el = jnp.exp(jnp.minimum(ref - b, 0.0))
            gm = jnp.where(_level_mask(tt, ss, m), da, 0.0).astype(BF16)
            pq = _dot(gm, _split2(k * el), "nn")
            pk = _dot(gm, _split2(q * eu), "tn")
            dqa += (pq[:, :LANE] + pq[:, LANE:]) * eu
            dka += (pk[:, :LANE] + pk[:, LANE:]) * el
        db = q * dqa - k * dka + dqh * qh - dkh * kh
        db = db + jnp.where(rows == C - 1, dbl, 0.0)
        dq = dqa + dqh * e
        dk = dka + dkh * ekl
        dg = _chunk_cumsum(db, rows, reverse=True)

        wa = jnp.exp(la_v - g)
        wc = jnp.exp(c - g)
        dqz_ref[...] = (dq * (sq * (1.0 + qz * (1.0 - sq)))).astype(BF16)
        dfz_ref[...] = (dg * wc * (1.0 - s) - dk * oml_v * s * (1.0 - s)).astype(BF16)
        dla_ref[...] += jnp.sum(dg * wa, axis=0, keepdims=True)
        dlc_ref[...] += jnp.sum(dg * wc, axis=0, keepdims=True)
        doml_ref[...] += jnp.sum(dk * (1.0 - s), axis=0, keepdims=True)

    def col(off):
        return pl.BlockSpec((C, LANE), lambda h, i: (nc - 1 - i, off * H + h))

    vec = pl.BlockSpec((1, LANE), lambda h, i: (0, h))
    tile = pl.BlockSpec((C, LANE), lambda h, i: (nc - 1 - i, h))
    a_spec = pl.BlockSpec((None, C, C), lambda h, i: (h, nc - 1 - i, 0))
    st_spec = pl.BlockSpec((None, None, LANE, LANE), lambda h, i: (h, nc - 1 - i, 0, 0))
    sw = jax.ShapeDtypeStruct((S, W), BF16)
    vw = jax.ShapeDtypeStruct((1, W), F32)
    return pl.pallas_call(
        body, name=name, grid=(H, nc),
        in_specs=[col(0), col(1), col(2), col(3), vec, vec, vec, vec, tile, a_spec, st_spec, tile],
        out_specs=[tile, tile, tile, tile, vec, vec, vec, vec],
        out_shape=[sw, sw, sw, sw, vw, vw, vw, vw],
        scratch_shapes=[pltpu.VMEM((LANE, LANE), F32), pltpu.VMEM((C, LANE), F32)],
        compiler_params=_params("parallel", "arbitrary"),
    )(u, u, u, u, la, lc, oml, gn_g, o, a, st, dz)


def _lb_terms(lb_logits, layer):
    p = jax.nn.softmax(lb_logits, axis=0)
    lb = (jnp.cumsum(p, axis=0) - p[0:1])[layer]
    return jnp.log(lb)[None], jnp.log1p(-lb)[None], (1.0 - lb)[None]


def kernel(x, norm_ffn1, ffn1_wg, ffn1_wu, ffn1_wd, norm_mix, norm_ffn2, ffn2_wg, ffn2_wu, ffn2_wd, ev_w_in, ev_conv_w, ev_conv_b, ev_cn_g, ev_cn_b, ev_qn_g, ev_kn_g, ev_w_out, od_w_in, od_lb_logits, od_gn_g, od_w_out, loss_target, m_norm_ffn1, m_ffn1_wg, m_ffn1_wu, m_ffn1_wd, m_norm_mix, m_norm_ffn2, m_ffn2_wg, m_ffn2_wu, m_ffn2_wd, m_ev_w_in, m_ev_conv_w, m_ev_conv_b, m_ev_cn_g, m_ev_cn_b, m_ev_qn_g, m_ev_kn_g, m_ev_w_out, m_od_w_in, m_od_lb_logits, m_od_gn_g, m_od_w_out, v_norm_ffn1, v_ffn1_wg, v_ffn1_wu, v_ffn1_wd, v_norm_mix, v_norm_ffn2, v_ffn2_wg, v_ffn2_wu, v_ffn2_wd, v_ev_w_in, v_ev_conv_w, v_ev_conv_b, v_ev_cn_g, v_ev_cn_b, v_ev_qn_g, v_ev_kn_g, v_ev_w_out, v_od_w_in, v_od_lb_logits, v_od_gn_g, v_od_w_out):
    depth = norm_ffn1.shape[0]
    S, D = x.shape[1], x.shape[2]
    xi, yi, ci = _me()
    dev = 4 * xi + 2 * yi + ci
    c_idx = jnp.reshape(ci, (1,)).astype(jnp.int32)
    k_idx = jnp.reshape(2 * xi + yi, (1,)).astype(jnp.int32)

    def ffn_shard(wg, wu, wd, l):
        return jnp.stack([wg[l].T, wu[l].T, wd[l]]).astype(BF16)

    ffn_local = [ffn_shard(ffn1_wg, ffn1_wu, ffn1_wd, l) for l in range(depth)] \
        + [ffn_shard(ffn2_wg, ffn2_wu, ffn2_wd, l) for l in range(depth)]
    mix_local = [ev_w_in[0].T.astype(BF16)[None], ev_w_out[0].astype(BF16)[None],
                 od_w_in[0].T.astype(BF16)[None], od_w_out[0].astype(BF16)[None]]
    gathered = _all_gather(ffn_local + mix_local, "all_gather_weights")
    full = [g.reshape(g.shape[0], N_DEV * g.shape[2], g.shape[3]) for g in gathered]
    w_ffn1, w_ffn2 = full[:depth], full[depth:2 * depth]
    ev_w_in_t, ev_w_out_f, od_w_in_t, od_w_out_f = (f[0] for f in full[2 * depth:])

    conv_w_sh, gn_g_sh = ev_conv_w[0], od_gn_g[0]
    cw, cs = conv_w_sh.shape[0], conv_w_sh.shape[1]
    gs = gn_g_sh.shape[0]
    conv_w_z = lax.dynamic_update_slice(jnp.zeros((cw, N_DEV * cs), F32), conv_w_sh, (0, dev * cs))
    gn_g_z = lax.dynamic_update_slice(jnp.zeros((N_DEV * gs,), F32), gn_g_sh, (dev * gs,))
    conv_w_full, gn_g_full = _unpack_rows(
        _all_reduce_small(_pack_rows([conv_w_z, gn_g_z]), "gather_small_params"),
        [conv_w_z.shape, gn_g_z.shape])

    def ffn_forward(h, gain, w, tag):
        hn = _rms_fwd(h, gain[None], "rms_" + tag)
        out, gu = _ffn_fwd(h, hn, w, "ffn_fwd_" + tag)
        return out, (h, hn, gu)

    def odd_mixer(u, l):
        (la, lc, oml), lb_vjp = jax.vjp(functools.partial(_lb_terms, layer=l), od_lb_logits)
        gn = gn_g_full[None]
        zb, o_raw, scores, states = _hgrn_fwd(u, la, lc, oml, gn, f"hgrn_fwd{l}")

        def backward(dz):
            dqz, dfz, diz, dgz, dla, dlc, doml, dgn = _hgrn_bwd(
                u, la, lc, oml, gn, o_raw, scores, states, dz, f"hgrn_bwd{l}")
            (g_lb,) = lb_vjp((dla, dlc, doml))
            return jnp.concatenate([dqz, dfz, diz, dgz], axis=1), [g_lb, dgn[0]]

        return zb, backward

    saved = []
    h = x[0]
    for l in range(depth):
        h, s1 = ffn_forward(h, norm_ffn1[l], w_ffn1[l], f"a{l}")
        hn = _rms_fwd(h, norm_mix[l][None], f"rms_mix{l}")
        if l % 2 == 0:
            u = _mm(hn, ev_w_in_t, "nt", F32, f"mix_in{l}")
            zb, core_vjp = _even_mixer(u, conv_w_full, ev_conv_b, ev_cn_g, ev_cn_b, ev_qn_g, ev_kn_g, str(l))
            w_out = ev_w_out_f
        else:
            u = _mm(hn, od_w_in_t, "nt", F32, f"mix_in{l}")
            zb, core_vjp = odd_mixer(u, l)
            w_out = od_w_out_f
        h_mix = h
        h = _mm(zb, w_out, "nn", F32, f"mix_out{l}", res=h)
        sm = (h_mix, hn, zb, core_vjp)
        h, s2 = ffn_forward(h, norm_ffn2[l], w_ffn2[l], f"b{l}")
        saved.append((s1, sm, s2))

    dy, loss_part = _loss_grad(h, loss_target[0], "loss_grad")

    def ffn_backward(dy, gain, w, sv, tag):
        h_in, hn, gu = sv
        dxn, dout, t = _ffn_bwd_dx(dy, w, gu, "ffn_bwd_dx_" + tag)
        dw = _ffn_bwd_dw(hn, dout, t, "ffn_bwd_dw_" + tag)
        dx, dgain = _rms_bwd(h_in, gain[None], dxn, dy, "rms_bwd_" + tag)
        return dx, dgain[0], dw

    g_norm1, g_norm2, g_normm = [None] * depth, [None] * depth, [None] * depth
    dw_ffn1, dw_ffn2 = [None] * depth, [None] * depth
    small, dw_mix = [None, None], [None, None]
    for l in reversed(range(depth)):
        s1, (h_mix, hn, zb, core_vjp), s2 = saved[l]
        dy, g_norm2[l], dw_ffn2[l] = ffn_backward(dy, norm_ffn2[l], w_ffn2[l], s2, f"b{l}")
        dyb = dy.astype(BF16)
        if l % 2 == 0:
            w_out, w_in_t = ev_w_out_f, ev_w_in_t
        else:
            w_out, w_in_t = od_w_out_f, od_w_in_t
        dz = _mm(dyb, w_out, "nt", F32, f"mix_out_dz{l}")
        dw_out = _mm(zb, dyb, "tn", BF16, f"mix_out_dw{l}")
        dub, small[l % 2] = core_vjp(dz)
        dw_mix[l % 2] = [_mm(dub, hn, "tn", BF16, f"mix_in_dw{l}"), dw_out]
        dhn = _mm(dub, w_in_t, "nn", F32, f"mix_in_dx{l}")
        dy, gm = _rms_bwd(h_mix, norm_mix[l][None], dhn, dy, f"rms_bwd_mix{l}")
        g_normm[l] = gm[0]
        dy, g_norm1[l], dw_ffn1[l] = ffn_backward(dy, norm_ffn1[l], w_ffn1[l], s1, f"a{l}")
    grad_x = dy[None]

    partial = dw_ffn1 + dw_ffn2 + [g[None] for g in dw_mix[0] + dw_mix[1]]
    partial = [g.reshape(g.shape[0], 4, 2, g.shape[1] // N_DEV, g.shape[2]) for g in partial]
    got = _exchange_core_halves(partial, "reduce_core_halves")
    sums = [_add_core_halves(g, r, c_idx, f"add_core_halves{a}") for a, (g, r) in enumerate(zip(partial, got))]
    got = _exchange_chip_blocks(sums, "reduce_chip_blocks")
    shard_g = [_sum_chip_blocks(s, r, k_idx, f"sum_chip_blocks{a}") for a, (s, r) in enumerate(zip(sums, got))]
    g_ffn1, g_ffn2 = shard_g[:depth], shard_g[depth:2 * depth]
    g_ev_in_t, g_ev_out, g_od_in_t, g_od_out = shard_g[2 * depth:]

    g_conv_w, g_conv_b, g_cn_g, g_cn_b, g_qn_g, g_kn_g = small[0]
    g_lb, g_gn = small[1]
    parts = [jnp.stack(g_norm1), jnp.stack(g_normm), jnp.stack(g_norm2), g_conv_b, g_cn_g, g_cn_b,
             g_qn_g, g_kn_g, g_lb, g_conv_w, g_gn, loss_part[0, :1]]
    red = _unpack_rows(_all_reduce_small(_pack_rows(parts), "reduce_small_grads"), [p.shape for p in parts])
    g_norm1, g_normm, g_norm2, g_conv_b, g_cn_g, g_cn_b, g_qn_g, g_kn_g, g_lb, g_conv_w, g_gn, loss = red
    g_conv_w = lax.dynamic_slice(g_conv_w, (0, dev * cs), (cw, cs))
    g_gn = lax.dynamic_slice(g_gn, (dev * gs,), (gs,))

    def ffn_grads(gl):
        return (jnp.stack([g[0].T for g in gl]), jnp.stack([g[1].T for g in gl]), jnp.stack([g[2] for g in gl]))

    g_ffn1_wg, g_ffn1_wu, g_ffn1_wd = ffn_grads(g_ffn1)
    g_ffn2_wg, g_ffn2_wu, g_ffn2_wd = ffn_grads(g_ffn2)
    grads = [g_norm1, g_ffn1_wg, g_ffn1_wu, g_ffn1_wd, g_normm, g_norm2, g_ffn2_wg, g_ffn2_wu, g_ffn2_wd,
             g_ev_in_t[0].T[None], g_conv_w[None], g_conv_b[None], g_cn_g[None], g_cn_b[None], g_qn_g[None],
             g_kn_g[None], g_ev_out, g_od_in_t[0].T[None], g_lb, g_gn[None], g_od_out]
    weights = [norm_ffn1, ffn1_wg, ffn1_wu, ffn1_wd, norm_mix, norm_ffn2, ffn2_wg, ffn2_wu, ffn2_wd, ev_w_in,
               ev_conv_w, ev_conv_b, ev_cn_g, ev_cn_b, ev_qn_g, ev_kn_g, ev_w_out, od_w_in, od_lb_logits,
               od_gn_g, od_w_out]
    moms = [m_norm_ffn1, m_ffn1_wg, m_ffn1_wu, m_ffn1_wd, m_norm_mix, m_norm_ffn2, m_ffn2_wg, m_ffn2_wu,
            m_ffn2_wd, m_ev_w_in, m_ev_conv_w, m_ev_conv_b, m_ev_cn_g, m_ev_cn_b, m_ev_qn_g, m_ev_kn_g,
            m_ev_w_out, m_od_w_in, m_od_lb_logits, m_od_gn_g, m_od_w_out]
    vars_ = [v_norm_ffn1, v_ffn1_wg, v_ffn1_wu, v_ffn1_wd, v_norm_mix, v_norm_ffn2, v_ffn2_wg, v_ffn2_wu,
             v_ffn2_wd, v_ev_w_in, v_ev_conv_w, v_ev_conv_b, v_ev_cn_g, v_ev_cn_b, v_ev_qn_g, v_ev_kn_g,
             v_ev_w_out, v_od_w_in, v_od_lb_logits, v_od_gn_g, v_od_w_out]
    deltas, new_m, new_v = [], [], []
    for i, (w, g, m, v) in enumerate(zip(weights, grads, moms, vars_)):
        d, nm, nv = _adamw(w, g, m, v, f"adamw{i}")
        deltas.append(d)
        new_m.append(nm)
        new_v.append(nv)
    return (loss[0], grad_x, *grads, *deltas, *new_m, *new_v)
```

```python
import functools
import math

import jax
import jax.numpy as jnp
from jax import lax
from jax.experimental import pallas as pl
from jax.experimental.pallas import tpu as pltpu

F32 = jnp.float32
BF16 = jnp.bfloat16
MESH = pl.DeviceIdType.MESH
N_DEV = 8

EPS = 1e-6
HEAD_DIM = 128
CONV_WIDTH = 31
DIL_PATTERNS = ((128, 1), (512, 4), (2048, 16))
Q_BLOCK = 128
ROPE_THETA = 10000.0
HGRN_KDIM = 128
HGRN_CHUNK = 256

ADAM_LR = 0.001
ADAM_B1 = 0.9
ADAM_B2 = 0.999
ADAM_EPS = 1e-08
ADAM_WD = 0.01
ADAM_STEP = 10

VMEM_LIMIT_BYTES = 56 * 1024 * 1024
LANE = 128
SUBLANE_BF16 = 16

ANY = pl.BlockSpec(memory_space=pl.ANY)


def _tile(n, pref, mult):
    t = (min(pref, n) // mult) * mult
    while t > 0:
        if n % t == 0:
            return t
        t -= mult
    return n


def _params(*sem):
    return pltpu.CompilerParams(dimension_semantics=sem, vmem_limit_bytes=VMEM_LIMIT_BYTES)


_DOT_DIMS = {
    "nn": (((1,), (0,)), ((), ())),
    "nt": (((1,), (1,)), ((), ())),
    "tn": (((0,), (0,)), ((), ())),
}


def _dot(a, b, mode):
    return lax.dot_general(a, b, _DOT_DIMS[mode], preferred_element_type=F32)


def _mm(a, b, mode, out_dtype, name, res=None, tm=1024, tn=1024, tk=2048):
    if mode == "nt":
        (M, K), N = a.shape, b.shape[0]
    elif mode == "nn":
        (M, K), N = a.shape, b.shape[1]
    else:
        (K, M), N = a.shape, b.shape[1]
    tm, tn, tk = _tile(M, tm, LANE), _tile(N, tn, LANE), _tile(K, tk, LANE)
    nk = K // tk

    def body(*refs):
        if res is None:
            a_ref, b_ref, o_ref, acc = refs
        else:
            a_ref, b_ref, r_ref, o_ref, acc = refs
        k = pl.program_id(2)

        @pl.when(k == 0)
        def _():
            acc[...] = jnp.zeros_like(acc)

        acc[...] += _dot(a_ref[...].astype(BF16), b_ref[...].astype(BF16), mode)

        @pl.when(k == nk - 1)
        def _():
            r = acc[...]
            if res is not None:
                r = r_ref[...] + r
            o_ref[...] = r.astype(out_dtype)

    a_spec = {"nt": pl.BlockSpec((tm, tk), lambda i, j, k: (i, k)),
              "nn": pl.BlockSpec((tm, tk), lambda i, j, k: (i, k)),
              "tn": pl.BlockSpec((tk, tm), lambda i, j, k: (k, i))}[mode]
    b_spec = {"nt": pl.BlockSpec((tn, tk), lambda i, j, k: (j, k)),
              "nn": pl.BlockSpec((tk, tn), lambda i, j, k: (k, j)),
              "tn": pl.BlockSpec((tk, tn), lambda i, j, k: (k, j))}[mode]
    o_spec = pl.BlockSpec((tm, tn), lambda i, j, k: (i, j))
    in_specs = [a_spec, b_spec] + ([o_spec] if res is not None else [])
    args = (a, b) + ((res,) if res is not None else ())
    return pl.pallas_call(
        body, name=name, grid=(M // tm, N // tn, nk),
        in_specs=in_specs, out_specs=o_spec,
        out_shape=jax.ShapeDtypeStruct((M, N), out_dtype),
        scratch_shapes=[pltpu.VMEM((tm, tn), F32)],
        compiler_params=_params("parallel", "parallel", "arbitrary"),
    )(*args)


def _rms_fwd(x, gain, name):
    S, D = x.shape
    tm = _tile(S, 512, SUBLANE_BF16)

    def body(x_ref, g_ref, o_ref):
        xv = x_ref[...]
        r = lax.rsqrt(jnp.mean(xv * xv, axis=-1, keepdims=True) + EPS)
        o_ref[...] = (xv * r * g_ref[...]).astype(BF16)

    return pl.pallas_call(
        body, name=name, grid=(S // tm,),
        in_specs=[pl.BlockSpec((tm, D), lambda i: (i, 0)), pl.BlockSpec((1, D), lambda i: (0, 0))],
        out_specs=pl.BlockSpec((tm, D), lambda i: (i, 0)),
        out_shape=jax.ShapeDtypeStruct((S, D), BF16),
        compiler_params=_params("parallel"),
    )(x, gain)


def _rms_bwd(x, gain, dxn, dy, name):
    S, D = x.shape
    tm = _tile(S, 512, 8)

    def body(x_ref, g_ref, dxn_ref, dy_ref, dx_ref, dg_ref):
        @pl.when(pl.program_id(0) == 0)
        def _():
            dg_ref[...] = jnp.zeros_like(dg_ref)

        xv = x_ref[...]
        r = lax.rsqrt(jnp.mean(xv * xv, axis=-1, keepdims=True) + EPS)
        xh = xv * r
        dxn_v = dxn_ref[...]
        dg_ref[...] += jnp.sum(dxn_v * xh, axis=0, keepdims=True)
        dxh = dxn_v * g_ref[...]
        dx_ref[...] = dy_ref[...] + r * (dxh - xh * jnp.mean(dxh * xh, axis=-1, keepdims=True))

    row = pl.BlockSpec((tm, D), lambda i: (i, 0))
    vec = pl.BlockSpec((1, D), lambda i: (0, 0))
    return pl.pallas_call(
        body, name=name, grid=(S // tm,),
        in_specs=[row, vec, row, row], out_specs=[row, vec],
        out_shape=[jax.ShapeDtypeStruct((S, D), F32), jax.ShapeDtypeStruct((1, D), F32)],
        compiler_params=_params("arbitrary"),
    )(x, gain, dxn, dy)


def _ffn_fwd(x, xn, w, name):
    S, D = x.shape
    F = w.shape[1]
    tm, tf = _tile(S, 512, SUBLANE_BF16), _tile(F, 512, LANE)
    nf = F // tf

    def body(x_ref, xn_ref, w_ref, o_ref, gu_ref, acc):
        f = pl.program_id(1)

        @pl.when(f == 0)
        def _():
            acc[...] = jnp.zeros_like(acc)

        xnv = xn_ref[...]
        g = _dot(xnv, w_ref[0], "nt")
        u = _dot(xnv, w_ref[1], "nt")
        gu_ref[0] = g.astype(BF16)
        gu_ref[1] = u.astype(BF16)
        h = (g * jax.nn.sigmoid(g) * u).astype(BF16)
        acc[...] += _dot(h, w_ref[2], "nn")

        @pl.when(f == nf - 1)
        def _():
            o_ref[...] = x_ref[...] + 0.5 * acc[...]

    row = pl.BlockSpec((tm, D), lambda i, f: (i, 0))
    return pl.pallas_call(
        body, name=name, grid=(S // tm, nf),
        in_specs=[row, row, pl.BlockSpec((3, tf, D), lambda i, f: (0, f, 0))],
        out_specs=[row, pl.BlockSpec((2, tm, tf), lambda i, f: (0, i, f))],
        out_shape=[jax.ShapeDtypeStruct((S, D), F32), jax.ShapeDtypeStruct((2, S, F), BF16)],
        scratch_shapes=[pltpu.VMEM((tm, D), F32)],
        compiler_params=_params("parallel", "arbitrary"),
    )(x, xn, w)


def _ffn_bwd_dx(dy, w, gu, name):
    S, D = dy.shape
    F = w.shape[1]
    tm, tf = _tile(S, 512, SUBLANE_BF16), _tile(F, 512, LANE)
    nf = F // tf

    def body(dy_ref, w_ref, gu_ref, dxn_ref, dout_ref, t_ref, acc):
        f = pl.program_id(1)

        @pl.when(f == 0)
        def _():
            acc[...] = jnp.zeros_like(acc)
            dout_ref[...] = (0.5 * dy_ref[...]).astype(BF16)

        dh = _dot(dout_ref[...], w_ref[2], "nt")
        g = gu_ref[0].astype(F32)
        u = gu_ref[1].astype(F32)
        sig = jax.nn.sigmoid(g)
        silu = g * sig
        dg = (dh * u * (sig * (1.0 + g * (1.0 - sig)))).astype(BF16)
        du = (dh * silu).astype(BF16)
        t_ref[0] = dg
        t_ref[1] = du
        t_ref[2] = (silu * u).astype(BF16)
        acc[...] += _dot(dg, w_ref[0], "nn") + _dot(du, w_ref[1], "nn")

        @pl.when(f == nf - 1)
        def _():
            dxn_ref[...] = acc[...]

    row = pl.BlockSpec((tm, D), lambda i, f: (i, 0))
    return pl.pallas_call(
        body, name=name, grid=(S // tm, nf),
        in_specs=[row, pl.BlockSpec((3, tf, D), lambda i, f: (0, f, 0)),
                  pl.BlockSpec((2, tm, tf), lambda i, f: (0, i, f))],
        out_specs=[row, row, pl.BlockSpec((3, tm, tf), lambda i, f: (0, i, f))],
        out_shape=[jax.ShapeDtypeStruct((S, D), F32), jax.ShapeDtypeStruct((S, D), BF16),
                   jax.ShapeDtypeStruct((3, S, F), BF16)],
        scratch_shapes=[pltpu.VMEM((tm, D), F32)],
        compiler_params=_params("parallel", "arbitrary"),
    )(dy, w, gu)


def _ffn_bwd_dw(xn, dout, t, name):
    S, D = xn.shape
    F = t.shape[2]
    ts, tf = _tile(S, 512, LANE), _tile(F, 512, LANE)
    ns = S // ts

    def body(xn_ref, dout_ref, t_ref, dw_ref, acc):
        s = pl.program_id(1)

        @pl.when(s == 0)
        def _():
            acc[...] = jnp.zeros_like(acc)

        xnv = xn_ref[...]
        acc[0] += _dot(t_ref[0], xnv, "tn")
        acc[1] += _dot(t_ref[1], xnv, "tn")
        acc[2] += _dot(t_ref[2], dout_ref[...], "tn")

        @pl.when(s == ns - 1)
        def _():
            dw_ref[...] = acc[...].astype(BF16)

    row = pl.BlockSpec((ts, D), lambda f, s: (s, 0))
    return pl.pallas_call(
        body, name=name, grid=(F // tf, ns),
        in_specs=[row, row, pl.BlockSpec((3, ts, tf), lambda f, s: (0, s, f))],
        out_specs=pl.BlockSpec((3, tf, D), lambda f, s: (0, f, 0)),
        out_shape=jax.ShapeDtypeStruct((3, F, D), BF16),
        scratch_shapes=[pltpu.VMEM((3, tf, D), F32)],
        compiler_params=_params("parallel", "arbitrary"),
    )(xn, dout, t)


def _loss_grad(y, target, name):
    S, D = y.shape
    tm = _tile(S, 512, 8)

    def body(y_ref, t_ref, dy_ref, l_ref):
        @pl.when(pl.program_id(0) == 0)
        def _():
            l_ref[...] = jnp.zeros_like(l_ref)

        e = y_ref[...] - t_ref[...]
        dy_ref[...] = e * (1.0 / D)
        l_ref[...] += 0.5 * jnp.sum(jnp.sum(e * e, axis=-1, keepdims=True) * (1.0 / D))

    row = pl.BlockSpec((tm, D), lambda i: (i, 0))
    one = pl.BlockSpec((8, LANE), lambda i: (0, 0))
    return pl.pallas_call(
        body, name=name, grid=(S // tm,),
        in_specs=[row, row], out_specs=[row, one],
        out_shape=[jax.ShapeDtypeStruct((S, D), F32), jax.ShapeDtypeStruct((8, LANE), F32)],
        compiler_params=_params("arbitrary"),
    )(y, target)


def _adamw(w, g, m, v, name):
    shape = w.shape
    C = shape[-1]
    R = math.prod(shape[:-1])
    tr = _tile(R, max(8, (1 << 19) // C // 8 * 8), 8)
    c1 = 1.0 / (1.0 - ADAM_B1 ** ADAM_STEP)
    c2 = 1.0 / (1.0 - ADAM_B2 ** ADAM_STEP)

    def body(w_ref, g_ref, m_ref, v_ref, d_ref, nm_ref, nv_ref):
        gv = g_ref[...]
        nm = ADAM_B1 * m_ref[...] + (1.0 - ADAM_B1) * gv
        nv = ADAM_B2 * v_ref[...] + (1.0 - ADAM_B2) * (gv * gv)
        nm_ref[...] = nm
        nv_ref[...] = nv
        d_ref[...] = -ADAM_LR * ((nm * c1) / (jnp.sqrt(nv * c2) + ADAM_EPS) + ADAM_WD * w_ref[...])

    blk = pl.BlockSpec((tr, C), lambda i: (i, 0))
    sds = jax.ShapeDtypeStruct((R, C), F32)
    outs = pl.pallas_call(
        body, name=name, grid=(R // tr,),
        in_specs=[blk] * 4, out_specs=[blk] * 3, out_shape=[sds] * 3,
        compiler_params=_params("parallel"),
    )(*(a.reshape(R, C) for a in (w, g, m, v)))
    return tuple(o.reshape(shape) for o in outs)


def _me():
    return lax.axis_index("x"), lax.axis_index("y"), lax.axis_index("c")


def _all_gather(shards, name):
    na = len(shards)

    def body(*refs):
        ins, outs = refs[:na], refs[na:2 * na]
        send_sems, recv_sems, local_sems = refs[2 * na:]
        x, y, c = _me()
        me, sibling = (x, y, c), (x, y, 1 - c)
        chips = [(1 - x, y), (x, 1 - y), (1 - x, 1 - y)]

        def blk(a, p):
            return outs[a].at[:, 4 * p[0] + 2 * p[1] + p[2]]

        def copy(a, k, block, to, src=None):
            return pltpu.make_async_remote_copy(
                src_ref=blk(a, block) if src is None else src, dst_ref=blk(a, block),
                send_sem=send_sems.at[a * 7 + k], recv_sem=recv_sems.at[a * 7 + k],
                device_id=to, device_id_type=MESH)

        mine, first, passed = [], [], []
        for a in range(na):
            mine.append(pltpu.make_async_copy(ins[a], blk(a, me), local_sems.at[a]))
            mine[a].start()
            first.append([copy(a, 0, me, sibling, src=ins[a])]
                         + [copy(a, 1 + j, me, (*chip, c), src=ins[a]) for j, chip in enumerate(chips)])
            for cp in first[a]:
                cp.start()
        for a in range(na):
            passed.append([copy(a, 4 + j, (*chip, c), sibling) for j, chip in enumerate(chips)])
            for j, chip in enumerate(chips):
                copy(a, 1 + j, (*chip, c), me).wait_recv()
                passed[a][j].start()
        for a in range(na):
            copy(a, 0, sibling, me).wait_recv()
            for j, chip in enumerate(chips):
                copy(a, 4 + j, (*chip, 1 - c), me).wait_recv()
        for a in range(na):
            for cp in first[a] + passed[a]:
                cp.wait_send()
            mine[a].wait()

    return pl.pallas_call(
        body, name=name,
        in_specs=[ANY] * na, out_specs=[ANY] * na,
        out_shape=[jax.ShapeDtypeStruct((s.shape[0], N_DEV) + s.shape[1:], s.dtype) for s in shards],
        scratch_shapes=[pltpu.SemaphoreType.DMA((7 * na,)), pltpu.SemaphoreType.DMA((7 * na,)),
                        pltpu.SemaphoreType.DMA((na,))],
    )(*shards)


def _exchange_core_halves(grads, name):
    na = len(grads)

    def body(*refs):
        ins, outs = refs[:na], refs[na:2 * na]
        send_sems, recv_sems = refs[2 * na:]
        x, y, c = _me()
        copies = [pltpu.make_async_remote_copy(
            src_ref=ins[a].at[:, :, 1 - c], dst_ref=outs[a],
            send_sem=send_sems.at[a], recv_sem=recv_sems.at[a],
            device_id=(x, y, 1 - c), device_id_type=MESH) for a in range(na)]
        for cp in copies:
            cp.start()
        for cp in copies:
            cp.wait()

    return pl.pallas_call(
        body, name=name,
        in_specs=[ANY] * na, out_specs=[ANY] * na,
        out_shape=[jax.ShapeDtypeStruct(g.shape[:2] + g.shape[3:], g.dtype) for g in grads],
        scratch_shapes=[pltpu.SemaphoreType.DMA((na,)), pltpu.SemaphoreType.DMA((na,))],
    )(*grads)


def _add_core_halves(grad, got, c_idx, name):
    n, nk, _, r, C = grad.shape
    tr = _tile(r, 1024, SUBLANE_BF16)

    def body(c_ref, g_ref, r_ref, o_ref):
        o_ref[...] = (g_ref[...].astype(F32) + r_ref[...].astype(F32)).astype(BF16)

    return pl.pallas_call(
        body, name=name,
        grid_spec=pltpu.PrefetchScalarGridSpec(
            num_scalar_prefetch=1, grid=(n, nk, r // tr),
            in_specs=[pl.BlockSpec((None, None, None, tr, C), lambda i, k, t, c: (i, k, c[0], t, 0)),
                      pl.BlockSpec((None, None, tr, C), lambda i, k, t, c: (i, k, t, 0))],
            out_specs=pl.BlockSpec((None, None, tr, C), lambda i, k, t, c: (i, k, t, 0))),
        out_shape=jax.ShapeDtypeStruct((n, nk, r, C), BF16),
        compiler_params=_params("parallel", "parallel", "parallel"),
    )(c_idx, grad, got)


HBM = pl.BlockSpec(memory_space=pltpu.HBM)
SEM = pl.BlockSpec(memory_space=pltpu.SEMAPHORE)
EFFECT = pltpu.SideEffectType.DATAFLOW_SIDE_EFFECTING


def _push_start(srcs, lands, plan, after, name):
    ns, nl = len(srcs), len(lands)
    ncp = len(plan([None] * ns, [None] * nl, dry=True))
    extra = [] if after is None else [after]

    def body(*refs):
        src_refs, land_refs = refs[:ns], refs[ns:ns + nl]
        send_sems, recv_sems = refs[ns + nl + len(extra)], refs[ns + nl + len(extra) + 1]
        token = refs[-1]
        for i, (s, d, to) in enumerate(plan(src_refs, land_refs)):
            pltpu.make_async_remote_copy(src_ref=s, dst_ref=d, send_sem=send_sems.at[i], recv_sem=recv_sems.at[i],
                                         device_id=to, device_id_type=MESH).start()
        token[...] = jnp.zeros_like(token)

    out = pl.pallas_call(
        body, name=name,
        out_shape=(pltpu.SemaphoreType.DMA((ncp,)), pltpu.SemaphoreType.DMA((ncp,)),
                   *[pltpu.HBM(a.shape, a.dtype) for a in srcs], *[pltpu.HBM(a.shape, a.dtype) for a in lands],
                   jax.ShapeDtypeStruct((8, LANE), F32)),
        in_specs=[HBM] * (ns + nl) + [ANY] * len(extra),
        out_specs=(SEM, SEM, *[HBM] * (ns + nl), pl.BlockSpec(memory_space=pltpu.VMEM)),
        input_output_aliases={i: 2 + i for i in range(ns + nl)},
        compiler_params=pltpu.CompilerParams(has_side_effects=EFFECT),
    )(*[pltpu.with_memory_space_constraint(a, pltpu.HBM) for a in srcs + lands], *extra)
    return out[0], out[1], list(out[2:2 + ns]), list(out[2 + ns:2 + ns + nl]), out[-1]


def _push_wait(send_sems, recv_sems, srcs, lands, plan, after, name):
    ns, nl = len(srcs), len(lands)

    def body(*refs):
        src_refs, land_refs = refs[:ns], refs[ns:ns + nl]
        send, recv = refs[ns + nl], refs[ns + nl + 1]
        for i, (s, d, to) in enumerate(plan(src_refs, land_refs)):
            cp = pltpu.make_async_remote_copy(src_ref=s, dst_ref=d, send_sem=send.at[i], recv_sem=recv.at[i],
                                              device_id=to, device_id_type=MESH)
            cp.wait_send()
            cp.wait_recv()

    out = pl.pallas_call(
        body, name=name,
        out_shape=tuple(pltpu.HBM(a.shape, a.dtype) for a in srcs + lands),
        in_specs=[HBM] * (ns + nl) + [SEM, SEM, ANY],
        out_specs=tuple([HBM] * (ns + nl)),
        input_output_aliases={i: i for i in range(ns + nl)},
        compiler_params=pltpu.CompilerParams(has_side_effects=EFFECT),
    )(*srcs, *lands, send_sems, recv_sems, after)
    return list(out[:ns]), list(out[ns:])


def _gather_plan(src_refs, land_refs, dry=False):
    if dry:
        return [None] * (4 * len(src_refs))
    x, y, c = _me()
    me = 4 * x + 2 * y + c
    targets = [(x, y, 1 - c), (1 - x, y, c), (x, 1 - y, c), (1 - x, 1 - y, c)]
    return [(s, l.at[:, me], to) for s, l in zip(src_refs, land_refs) for to in targets]


def _chip_plan(src_refs, land_refs, dry=False):
    if dry:
        return [None] * (3 * len(src_refs))
    x, y, c = _me()
    chips = [(1 - x, y), (x, 1 - y), (1 - x, 1 - y)]
    return [(s.at[:, 2 * chip[0] + chip[1]], l.at[j], (*chip, c))
            for s, l in zip(src_refs, land_refs) for j, chip in enumerate(chips)]


def _gather_forward(lands, name):
    na = len(lands)

    def body(*refs):
        bufs = refs[na:2 * na]
        send_sems, recv_sems = refs[2 * na:]
        x, y, c = _me()
        chips = [(1 - x, y), (x, 1 - y), (1 - x, 1 - y)]

        def copy(a, j, pc):
            blk = bufs[a].at[:, 4 * chips[j][0] + 2 * chips[j][1] + pc]
            return pltpu.make_async_remote_copy(
                src_ref=blk, dst_ref=blk, send_sem=send_sems.at[3 * a + j], recv_sem=recv_sems.at[3 * a + j],
                device_id=(x, y, 1 - c), device_id_type=MESH)

        pairs = [(a, j) for a in range(na) for j in range(3)]
        for a, j in pairs:
            copy(a, j, c).start()
        for a, j in pairs:
            copy(a, j, 1 - c).wait_recv()
        for a, j in pairs:
            copy(a, j, c).wait_send()

    return pl.pallas_call(
        body, name=name,
        in_specs=[ANY] * na, out_specs=[ANY] * na,
        out_shape=[jax.ShapeDtypeStruct(a.shape, a.dtype) for a in lands],
        input_output_aliases={a: a for a in range(na)},
        scratch_shapes=[pltpu.SemaphoreType.DMA((3 * na,)), pltpu.SemaphoreType.DMA((3 * na,))],
    )(*lands)


def _sum_chip_blocks(sums, got, k_idx, name):
    n, _, r, C = sums.shape
    tr = _tile(r, 512, SUBLANE_BF16)

    def body(k_ref, s_ref, r_ref, o_ref):
        acc = s_ref[...].astype(F32)
        for j in range(3):
            acc = acc + r_ref[j].astype(F32)
        o_ref[...] = acc

    return pl.pallas_call(
        body, name=name,
        grid_spec=pltpu.PrefetchScalarGridSpec(
            num_scalar_prefetch=1, grid=(n, r // tr),
            in_specs=[pl.BlockSpec((None, None, tr, C), lambda i, t, k: (i, k[0], t, 0)),
                      pl.BlockSpec((3, None, tr, C), lambda i, t, k: (0, i, t, 0))],
            out_specs=pl.BlockSpec((None, tr, C), lambda i, t, k: (i, t, 0))),
        out_shape=jax.ShapeDtypeStruct((n, r, C), F32),
        compiler_params=_params("parallel", "parallel"),
    )(k_idx, sums, got)


def _all_reduce_small(v, name):
    R = v.shape[0]

    def body(v_ref, o_ref, buf, send_sems, recv_sems):
        x, y, c = _me()
        me = 4 * x + 2 * y + c
        buf[me] = v_ref[...]
        copies = []
        for k in range(1, N_DEV):
            peer = (x ^ (k >> 2), y ^ ((k >> 1) & 1), c ^ (k & 1))
            copies.append(pltpu.make_async_remote_copy(
                src_ref=v_ref, dst_ref=buf.at[me],
                send_sem=send_sems.at[k - 1], recv_sem=recv_sems.at[k - 1],
                device_id=peer, device_id_type=MESH))
        for cp in copies:
            cp.start()
        for cp in copies:
            cp.wait()
        acc = buf[0]
        for d in range(1, N_DEV):
            acc = acc + buf[d]
        o_ref[...] = acc

    vm = pl.BlockSpec(memory_space=pltpu.VMEM)
    return pl.pallas_call(
        body, name=name, in_specs=[vm], out_specs=vm,
        out_shape=jax.ShapeDtypeStruct((R, LANE), F32),
        scratch_shapes=[pltpu.VMEM((N_DEV, R, LANE), F32),
                        pltpu.SemaphoreType.DMA((N_DEV - 1,)), pltpu.SemaphoreType.DMA((N_DEV - 1,))],
        compiler_params=pltpu.CompilerParams(vmem_limit_bytes=VMEM_LIMIT_BYTES),
    )(v)


def _pack_rows(parts):
    flat = jnp.concatenate([p.reshape(-1).astype(F32) for p in parts])
    n = flat.shape[0]
    rows = -(-n // (8 * LANE)) * 8
    flat = jnp.pad(flat, (0, rows * LANE - n))
    return flat.reshape(rows, LANE)


def _unpack_rows(packed, shapes):
    flat = packed.reshape(-1)
    out, off = [], 0
    for s in shapes:
        n = math.prod(s)
        out.append(flat[off:off + n].reshape(s))
        off += n
    return out


CONV_HALO = 32


def _conv_fwd(u, conv_w, conv_b, cn_g, cn_b, name):
    S = u.shape[0]
    C = conv_w.shape[1]
    T = _tile(S, 256, CONV_HALO)
    hb = T // CONV_HALO

    def body(av_ref, ag_ref, pv_ref, pg_ref, w_ref, b_ref, g_ref, bb_ref, out_ref, y_ref, scr):
        i = pl.program_id(0)
        prev = pv_ref[...] * jax.nn.sigmoid(pg_ref[...])
        scr[0:CONV_HALO, :] = jnp.where(i > 0, prev, 0.0)
        scr[CONV_HALO:CONV_HALO + T, :] = av_ref[...] * jax.nn.sigmoid(ag_ref[...])
        acc = jnp.broadcast_to(b_ref[...], (T, C))
        for j in range(CONV_WIDTH):
            acc = acc + w_ref[j:j + 1, :] * scr[pl.ds(CONV_HALO - (CONV_WIDTH - 1) + j, T), :]
        y_ref[...] = acc
        mu = jnp.mean(acc, axis=-1, keepdims=True)
        xc = acc - mu
        var = jnp.mean(xc * xc, axis=-1, keepdims=True)
        ln = xc * lax.rsqrt(var + EPS) * g_ref[...] + bb_ref[...]
        out_ref[...] = (ln * jax.nn.sigmoid(ln)).astype(BF16)

    def cur(cb):
        return pl.BlockSpec((T, C), lambda i: (i, cb))

    def halo(cb):
        return pl.BlockSpec((CONV_HALO, C), lambda i: (jnp.maximum(i * hb - 1, 0), cb))

    vec = pl.BlockSpec((1, C), lambda i: (0, 0))
    return pl.pallas_call(
        body, name=name, grid=(S // T,),
        in_specs=[cur(0), cur(1), halo(0), halo(1), pl.BlockSpec((CONV_WIDTH, C), lambda i: (0, 0)), vec, vec, vec],
        out_specs=[pl.BlockSpec((T, C), lambda i: (i, 0))] * 2,
        out_shape=[jax.ShapeDtypeStruct((S, C), BF16), jax.ShapeDtypeStruct((S, C), F32)],
        scratch_shapes=[pltpu.VMEM((T + CONV_HALO, C), F32)],
        compiler_params=_params("parallel"),
    )(u, u, u, u, conv_w, conv_b, cn_g, cn_b)


def _conv_bwd_norm(dz, y, cn_g, cn_b, name):
    S, C = y.shape
    T = _tile(S, 256, 8)

    def body(dz_ref, y_ref, g_ref, bb_ref, dy_ref, dg_ref, db_ref):
        @pl.when(pl.program_id(0) == 0)
        def _():
            dg_ref[...] = jnp.zeros_like(dg_ref)
            db_ref[...] = jnp.zeros_like(db_ref)

        yv = y_ref[...]
        mu = jnp.mean(yv, axis=-1, keepdims=True)
        xc = yv - mu
        rstd = lax.rsqrt(jnp.mean(xc * xc, axis=-1, keepdims=True) + EPS)
        xh = xc * rstd
        ln = xh * g_ref[...] + bb_ref[...]
        sg = jax.nn.sigmoid(ln)
        dln = dz_ref[...] * (sg * (1.0 + ln * (1.0 - sg)))
        dg_ref[...] += jnp.sum(dln * xh, axis=0, keepdims=True)
        db_ref[...] += jnp.sum(dln, axis=0, keepdims=True)
        dxh = dln * g_ref[...]
        dy_ref[...] = rstd * (dxh - jnp.mean(dxh, axis=-1, keepdims=True)
                              - xh * jnp.mean(dxh * xh, axis=-1, keepdims=True))

    row = pl.BlockSpec((T, C), lambda i: (i, 0))
    vec = pl.BlockSpec((1, C), lambda i: (0, 0))
    return pl.pallas_call(
        body, name=name, grid=(S // T,),
        in_specs=[row, row, vec, vec], out_specs=[row, vec, vec],
        out_shape=[jax.ShapeDtypeStruct((S, C), F32), jax.ShapeDtypeStruct((1, C), F32),
                   jax.ShapeDtypeStruct((1, C), F32)],
        compiler_params=_params("arbitrary"),
    )(dz, y, cn_g, cn_b)


def _conv_bwd_taps(u, dy, conv_w, name):
    S, C = dy.shape
    T = _tile(S, 256, CONV_HALO)
    hb = T // CONV_HALO
    nt = S // T
    W1 = CONV_WIDTH - 1

    def body(av_ref, ag_ref, pv_ref, pg_ref, dy_ref, dn_ref, w_ref, dv_ref, dg_ref, dw_ref, db_ref, a_scr, d_scr):
        i = pl.program_id(0)

        @pl.when(i == 0)
        def _():
            dw_ref[...] = jnp.zeros_like(dw_ref)
            db_ref[...] = jnp.zeros_like(db_ref)

        av, sg = av_ref[...], jax.nn.sigmoid(ag_ref[...])
        prev = pv_ref[...] * jax.nn.sigmoid(pg_ref[...])
        a_scr[0:CONV_HALO, :] = jnp.where(i > 0, prev, 0.0)
        a_scr[CONV_HALO:CONV_HALO + T, :] = av * sg
        dyv = dy_ref[...]
        d_scr[0:T, :] = dyv
        d_scr[T:T + CONV_HALO, :] = jnp.where(i < nt - 1, dn_ref[...], 0.0)
        da = jnp.zeros((T, C), F32)
        for j in range(CONV_WIDTH):
            da = da + w_ref[j:j + 1, :] * d_scr[pl.ds(W1 - j, T), :]
            dw_ref[j:j + 1, :] += jnp.sum(dyv * a_scr[pl.ds(CONV_HALO - W1 + j, T), :], axis=0, keepdims=True)
        db_ref[...] += jnp.sum(dyv, axis=0, keepdims=True)
        dv_ref[...] = (da * sg).astype(BF16)
        dg_ref[...] = (da * av * sg * (1.0 - sg)).astype(BF16)

    def cur(cb):
        return pl.BlockSpec((T, C), lambda i: (i, cb))

    def halo(cb):
        return pl.BlockSpec((CONV_HALO, C), lambda i: (jnp.maximum(i * hb - 1, 0), cb))

    nxt = pl.BlockSpec((CONV_HALO, C), lambda i: (jnp.minimum((i + 1) * hb, S // CONV_HALO - 1), 0))
    row = pl.BlockSpec((T, C), lambda i: (i, 0))
    return pl.pallas_call(
        body, name=name, grid=(nt,),
        in_specs=[cur(0), cur(1), halo(0), halo(1), row, nxt, pl.BlockSpec((CONV_WIDTH, C), lambda i: (0, 0))],
        out_specs=[row, row, pl.BlockSpec((CONV_HALO, C), lambda i: (0, 0)), pl.BlockSpec((1, C), lambda i: (0, 0))],
        out_shape=[jax.ShapeDtypeStruct((S, C), BF16), jax.ShapeDtypeStruct((S, C), BF16),
                   jax.ShapeDtypeStruct((CONV_HALO, C), F32), jax.ShapeDtypeStruct((1, C), F32)],
        scratch_shapes=[pltpu.VMEM((T + CONV_HALO, C), F32), pltpu.VMEM((T + CONV_HALO, C), F32)],
        compiler_params=_params("arbitrary"),
    )(u, u, u, u, dy, dy, conv_w)


def _rope_tables(S):
    half = HEAD_DIM // 2
    inv = jnp.exp(-math.log(ROPE_THETA) * jnp.arange(half, dtype=F32) / half)
    ang = jnp.arange(S, dtype=jnp.int32).astype(F32)[:, None] * inv[None, :]
    cos, sin = jnp.cos(ang), jnp.sin(ang)
    return jnp.concatenate([cos, cos], axis=1), jnp.concatenate([-sin, sin], axis=1)


def _qkv_prep(u, qn_g, kn_g, cos, sin, cb0, name):
    S = u.shape[0]
    A = (u.shape[1] // (cb0 + 3))
    H = A // HEAD_DIM
    T = _tile(S, 256, SUBLANE_BF16)
    scale = HEAD_DIM ** -0.5

    def body(q_ref, k_ref, v_ref, qg_ref, kg_ref, cos_ref, sin_ref, qo_ref, ko_ref, vo_ref):
        cosv, sinv = cos_ref[...], sin_ref[...]
        for h in range(H):
            sl = slice(h * HEAD_DIM, (h + 1) * HEAD_DIM)
            for x_ref, g_ref, o_ref, sc in ((q_ref, qg_ref, qo_ref, scale), (k_ref, kg_ref, ko_ref, 1.0)):
                xv = x_ref[:, sl]
                xn = xv * lax.rsqrt(jnp.mean(xv * xv, axis=-1, keepdims=True) + EPS) * g_ref[...]
                y = xn * cosv + pltpu.roll(xn, HEAD_DIM // 2, 1) * sinv
                o_ref[:, sl] = (y * sc).astype(BF16)
        vo_ref[...] = v_ref[...].astype(BF16)

    def col(cb):
        return pl.BlockSpec((T, A), lambda i: (i, cb))

    vec = pl.BlockSpec((1, HEAD_DIM), lambda i: (0, 0))
    tab = pl.BlockSpec((T, HEAD_DIM), lambda i: (i, 0))
    out = pl.BlockSpec((T, A), lambda i: (i, 0))
    return pl.pallas_call(
        body, name=name, grid=(S // T,),
        in_specs=[col(cb0), col(cb0 + 1), col(cb0 + 2), vec, vec, tab, tab],
        out_specs=[out] * 3, out_shape=[jax.ShapeDtypeStruct((S, A), BF16)] * 3,
        compiler_params=_params("parallel"),
    )(u, u, u, qn_g, kn_g, cos, sin)


def _qkv_prep_bwd(u, dqs, dks, dvs, qn_g, kn_g, cos, sin, cb0, name):
    S = u.shape[0]
    A = dqs[0].shape[1]
    H = A // HEAD_DIM
    T = _tile(S, 256, SUBLANE_BF16)
    nb = len(dqs)
    scale = HEAD_DIM ** -0.5

    def body(*refs):
        q_ref, k_ref, qg_ref, kg_ref, cos_ref, sin_ref = refs[:6]
        dq_refs, dk_refs, dv_refs = refs[6:6 + nb], refs[6 + nb:6 + 2 * nb], refs[6 + 2 * nb:6 + 3 * nb]
        dqo_ref, dko_ref, dvo_ref, dqg_ref, dkg_ref = refs[6 + 3 * nb:]

        @pl.when(pl.program_id(0) == 0)
        def _():
            dqg_ref[...] = jnp.zeros_like(dqg_ref)
            dkg_ref[...] = jnp.zeros_like(dkg_ref)

        cosv, sinv = cos_ref[...], sin_ref[...]
        for h in range(H):
            sl = slice(h * HEAD_DIM, (h + 1) * HEAD_DIM)
            for x_ref, g_ref, d_refs, o_ref, dg_ref, sc in ((q_ref, qg_ref, dq_refs, dqo_ref, dqg_ref, scale),
                                                          (k_ref, kg_ref, dk_refs, dko_ref, dkg_ref, 1.0)):
                dy = d_refs[0][:, sl]
                for r in d_refs[1:]:
                    dy = dy + r[:, sl]
                dy = dy * sc
                dxn = dy * cosv + pltpu.roll(dy * sinv, HEAD_DIM // 2, 1)
                xv = x_ref[:, sl]
                r = lax.rsqrt(jnp.mean(xv * xv, axis=-1, keepdims=True) + EPS)
                xh = xv * r
                dg_ref[...] += jnp.sum(dxn * xh, axis=0, keepdims=True)
                dxh = dxn * g_ref[...]
                o_ref[:, sl] = (r * (dxh - xh * jnp.mean(dxh * xh, axis=-1, keepdims=True))).astype(BF16)
        dv = dv_refs[0][...]
        for r in dv_refs[1:]:
            dv = dv + r[...]
        dvo_ref[...] = dv.astype(BF16)

    def col(cb):
        return pl.BlockSpec((T, A), lambda i: (i, cb))

    vec = pl.BlockSpec((1, HEAD_DIM), lambda i: (0, 0))
    tab = pl.BlockSpec((T, HEAD_DIM), lambda i: (i, 0))
    row = pl.BlockSpec((T, A), lambda i: (i, 0))
    return pl.pallas_call(
        body, name=name, grid=(S // T,),
        in_specs=[col(cb0), col(cb0 + 1), vec, vec, tab, tab] + [row] * (3 * nb),
        out_specs=[row, row, row, vec, vec],
        out_shape=[jax.ShapeDtypeStruct((S, A), BF16)] * 3 + [jax.ShapeDtypeStruct((1, HEAD_DIM), F32)] * 2,
        compiler_params=_params("arbitrary"),
    )(u, u, qn_g, kn_g, cos, sin, *dqs, *dks, *dvs)


ATT_TILE = 256
NEG = -1e30


def _attn_bias(tile):
    span = max(window for window, _ in DIL_PATTERNS)
    nw = -(-span // tile) + 1
    dist = (jnp.arange(nw)[:, None, None] * tile + jnp.arange(tile)[None, :, None] - jnp.arange(tile)[None, None, :])
    mult = sum(((dist >= 0) & (dist <= window) & (dist % dil == 0)).astype(F32) for window, dil in DIL_PATTERNS)
    return jnp.where(mult > 0, jnp.log(jnp.maximum(mult, 1.0)), NEG)


def _attn_fwd(q, k, v, bias, name):
    S, A = q.shape
    H = A // HEAD_DIM
    nw, T, _ = bias.shape
    nq = S // T

    def body(q_ref, k_ref, v_ref, b_ref, ob_ref, of_ref, l_ref, s_scr):
        i = pl.program_id(1)
        qv = q_ref[...]
        mx = jnp.full((T, 1), NEG, F32)
        for w in range(nw):
            blk = i - w
            start = pl.multiple_of(jnp.maximum(blk, 0) * T, T)
            s = _dot(qv, k_ref[pl.ds(start, T), :], "nt") + b_ref[w] + jnp.where(blk >= 0, 0.0, NEG)
            s_scr[w] = s
            mx = jnp.maximum(mx, jnp.max(s, axis=-1, keepdims=True))
        den = jnp.zeros((T, 1), F32)
        o = jnp.zeros((T, HEAD_DIM), F32)
        for w in range(nw):
            start = pl.multiple_of(jnp.maximum(i - w, 0) * T, T)
            p = jnp.exp(s_scr[w] - mx)
            den = den + jnp.sum(p, axis=-1, keepdims=True)
            o = o + _dot(p.astype(BF16), v_ref[pl.ds(start, T), :], "nn")
        o = o / den
        ob_ref[...] = o.astype(BF16)
        of_ref[...] = o
        l_ref[...] = mx + jnp.log(den)

    blk = pl.BlockSpec((T, HEAD_DIM), lambda h, i: (i, h))
    full = pl.BlockSpec((S, HEAD_DIM), lambda h, i: (0, h))
    return pl.pallas_call(
        body, name=name, grid=(H, nq),
        in_specs=[blk, full, full, pl.BlockSpec((nw, T, T), lambda h, i: (0, 0, 0))],
        out_specs=[blk, blk, pl.BlockSpec((None, T, 1), lambda h, i: (h, i, 0))],
        out_shape=[jax.ShapeDtypeStruct((S, A), BF16), jax.ShapeDtypeStruct((S, A), F32),
                   jax.ShapeDtypeStruct((H, S, 1), F32)],
        scratch_shapes=[pltpu.VMEM((nw, T, T), F32)],
        compiler_params=_params("parallel", "arbitrary"),
    )(q, k, v, bias)


def _attn_dq(q, k, v, dz, cb0, o, lse, bias, name):
    S, A = q.shape
    H = A // HEAD_DIM
    nw, T, _ = bias.shape
    nq = S // T

    def body(q_ref, k_ref, v_ref, do_ref, o_ref, l_ref, b_ref, dq_ref, d_ref):
        i = pl.program_id(1)
        qv, dof = q_ref[...], do_ref[...]
        dov = dof.astype(BF16)
        delta = jnp.sum(dof * o_ref[...], axis=-1, keepdims=True)
        d_ref[...] = delta
        lv = l_ref[...]
        dq = jnp.zeros((T, HEAD_DIM), F32)
        for w in range(nw):
            blk = i - w
            start = pl.multiple_of(jnp.maximum(blk, 0) * T, T)
            kv = k_ref[pl.ds(start, T), :]
            s = _dot(qv, kv, "nt") + b_ref[w] + jnp.where(blk >= 0, 0.0, NEG)
            p = jnp.exp(s - lv)
            ds = (p * (_dot(dov, v_ref[pl.ds(start, T), :], "nt") - delta)).astype(BF16)
            dq = dq + _dot(ds, kv, "nn")
        dq_ref[...] = dq

    blk = pl.BlockSpec((T, HEAD_DIM), lambda h, i: (i, h))
    full = pl.BlockSpec((S, HEAD_DIM), lambda h, i: (0, h))
    col = pl.BlockSpec((None, T, 1), lambda h, i: (h, i, 0))
    return pl.pallas_call(
        body, name=name, grid=(H, nq),
        in_specs=[blk, full, full, pl.BlockSpec((T, HEAD_DIM), lambda h, i: (i, cb0 + h)), blk, col,
                  pl.BlockSpec((nw, T, T), lambda h, i: (0, 0, 0))],
        out_specs=[blk, col],
        out_shape=[jax.ShapeDtypeStruct((S, A), F32), jax.ShapeDtypeStruct((H, S, 1), F32)],
        compiler_params=_params("parallel", "arbitrary"),
    )(q, k, v, dz, o, lse, bias)


def _attn_dkv(q, k, v, dz, cb0, lse, delta, bias, name):
    S, A = q.shape
    H = A // HEAD_DIM
    nw, T, _ = bias.shape
    nq = S // T

    def body(k_ref, v_ref, q_ref, do_ref, l_ref, d_ref, b_ref, dk_ref, dv_ref):
        m = pl.program_id(1)
        kv, vv = k_ref[...], v_ref[...]
        dk = jnp.zeros((T, HEAD_DIM), F32)
        dv = jnp.zeros((T, HEAD_DIM), F32)
        for w in range(nw):
            blk = m + w
            start = pl.multiple_of(jnp.minimum(blk, nq - 1) * T, T)
            qv = q_ref[pl.ds(start, T), :]
            dov = do_ref[pl.ds(start, T), :].astype(BF16)
            s = _dot(qv, kv, "nt") + b_ref[w] + jnp.where(blk < nq, 0.0, NEG)
            p = jnp.exp(s - l_ref[pl.ds(start, T), :])
            dv = dv + _dot(p.astype(BF16), dov, "tn")
            ds = (p * (_dot(dov, vv, "nt") - d_ref[pl.ds(start, T), :])).astype(BF16)
            dk = dk + _dot(ds, qv, "tn")
        dk_ref[...] = dk
        dv_ref[...] = dv

    blk = pl.BlockSpec((T, HEAD_DIM), lambda h, m: (m, h))
    full = pl.BlockSpec((S, HEAD_DIM), lambda h, m: (0, h))
    col = pl.BlockSpec((None, S, 1), lambda h, m: (h, 0, 0))
    sds = jax.ShapeDtypeStruct((S, A), F32)
    return pl.pallas_call(
        body, name=name, grid=(H, nq),
        in_specs=[blk, blk, full, pl.BlockSpec((S, HEAD_DIM), lambda h, m: (0, cb0 + h)), col, col,
                  pl.BlockSpec((nw, T, T), lambda h, m: (0, 0, 0))],
        out_specs=[blk, blk], out_shape=[sds, sds],
        compiler_params=_params("parallel", "arbitrary"),
    )(k, v, q, dz, lse, delta, bias)


def _even_mixer(u, conv_w, conv_b, cn_g, cn_b, qn_g, kn_g, tag):
    S = u.shape[0]
    C = conv_w.shape[1]
    A = (u.shape[1] - 2 * C) // 3
    assert A == C, "column-block addressing of u assumes equal conv and attention widths"
    T = _tile(S, ATT_TILE, LANE)
    cos, sin = _rope_tables(S)
    bias = _attn_bias(T)
    a_out, y = _conv_fwd(u, conv_w, conv_b, cn_g, cn_b, "conv_fwd" + tag)
    q, k, v = _qkv_prep(u, qn_g, kn_g, cos, sin, 2, "qkv_prep" + tag)
    ob, of, lse = _attn_fwd(q, k, v, bias, "attn_fwd" + tag)
    z = jnp.concatenate([a_out, ob], axis=1)

    def backward(dz):
        dy, d_cn_g, d_cn_b = _conv_bwd_norm(dz, y, cn_g, cn_b, "conv_bwd_norm" + tag)
        d_val, d_gate, d_w, d_b = _conv_bwd_taps(u, dy, conv_w, "conv_bwd_taps" + tag)
        dqp, delta = _attn_dq(q, k, v, dz, C // HEAD_DIM, of, lse, bias, "attn_dq" + tag)
        dkp, dvp = _attn_dkv(q, k, v, dz, C // HEAD_DIM, lse, delta, bias, "attn_dkv" + tag)
        dq, dk, dv, d_qn, d_kn = _qkv_prep_bwd(u, [dqp], [dkp], [dvp], qn_g, kn_g, cos, sin, 2, "qkv_prep_bwd" + tag)
        du = jnp.concatenate([d_val, d_gate, dq, dk, dv], axis=1)
        return du, [d_w[:CONV_WIDTH], d_b[0], d_cn_g[0], d_cn_b[0], d_qn[0], d_kn[0]]

    return z, backward


_LEVELS = (128, 64, 32, 16, 8, 4, 2, 1)


def _chunk_cumsum(g, rows, reverse=False):
    C = g.shape[0]
    d = 1
    while d < C:
        if reverse:
            g = g + jnp.where(rows < C - d, pltpu.roll(g, C - d, 0), 0.0)
        else:
            g = g + jnp.where(rows >= d, pltpu.roll(g, d, 0), 0.0)
        d *= 2
    return g


def _level_ref(b, b_scr, rows, m):
    C = b.shape[0]
    if m >= 8:
        pieces = [jnp.broadcast_to(b_scr[2 * m * j + m - 1:2 * m * j + m, :], (2 * m, LANE)) for j in range(C // (2 * m))]
        return pieces[0] if len(pieces) == 1 else jnp.concatenate(pieces, axis=0)
    pos = rows & (2 * m - 1)
    ref = b
    for p in range(2 * m):
        if p != m - 1:
            ref = jnp.where(pos == p, pltpu.roll(b, (p - (m - 1)) % C, 0), ref)
    return ref


def _level_operands(q, k, b, b_scr, rows, m):
    ref = _level_ref(b, b_scr, rows, m)
    qs = (q * jnp.exp(jnp.minimum(b - ref, 0.0))).astype(BF16)
    ks = (k * jnp.exp(jnp.minimum(ref - b, 0.0))).astype(BF16)
    return qs, ks


def _split2(x):
    hi = x.astype(BF16)
    lo = (x - hi.astype(F32)).astype(BF16)
    return jnp.concatenate([hi, lo], axis=1)


def _level_mask(tt, ss, m):
    x = tt ^ ss
    return (tt > ss) & (x >= m) & (x < 2 * m)


def _hgrn_gates(qz, fz, la, lc, oml):
    sq = jax.nn.sigmoid(qz)
    q = qz * sq
    s = jax.nn.sigmoid(fz)
    c = lc + jnp.minimum(fz, 0.0) - jnp.log(1.0 + jnp.exp(-jnp.abs(fz)))
    mx = jnp.maximum(la, c)
    g = mx + jnp.log(1.0 + jnp.exp(-jnp.abs(la - c)))
    k = oml * (1.0 - s)
    return q, sq, k, s, g, c


def _hgrn_fwd(u, la, lc, oml, gn_g, name):
    S = u.shape[0]
    W = u.shape[1] // 4
    H = W // HGRN_KDIM
    C = min(HGRN_CHUNK, S)
    nc = S // C
    levels = [m for m in _LEVELS if m < C]

    def body(qz_ref, fz_ref, iz_ref, gz_ref, la_ref, lc_ref, oml_ref, gn_ref,
             z_ref, o_ref, a_ref, st_ref, state, b_scr):
        @pl.when(pl.program_id(1) == 0)
        def _():
            state[...] = jnp.zeros_like(state)

        rows = lax.broadcasted_iota(jnp.int32, (C, LANE), 0)
        tt = lax.broadcasted_iota(jnp.int32, (C, C), 0)
        ss = lax.broadcasted_iota(jnp.int32, (C, C), 1)
        q, _, k, _, g, _ = _hgrn_gates(qz_ref[...], fz_ref[...], la_ref[...], lc_ref[...], oml_ref[...])
        v = iz_ref[...].astype(BF16)
        b = _chunk_cumsum(g, rows)
        b_scr[...] = b
        a = jnp.where(tt == ss, jnp.sum(q * k, axis=-1, keepdims=True), 0.0)
        for m in levels:
            qs, ks = _level_operands(q, k, b, b_scr, rows, m)
            a = jnp.where(_level_mask(tt, ss, m), _dot(qs, ks, "nt"), a)
        ab = a.astype(BF16)
        a_ref[...] = ab
        st = state[...]
        st_ref[...] = st
        o = _dot(ab, v, "nn") + _dot((q * jnp.exp(b)).astype(BF16), st.astype(BF16), "nt")
        bl = b_scr[C - 1:C, :]
        kh = (k * jnp.exp(bl - b)).astype(BF16)
        state[...] = st * jnp.exp(bl) + _dot(v, kh, "tn")
        o_ref[...] = o
        r = lax.rsqrt(jnp.mean(o * o, axis=-1, keepdims=True) + EPS)
        gz = gz_ref[...]
        z_ref[...] = (o * r * gn_ref[...] * (gz * jax.nn.sigmoid(gz))).astype(BF16)

    def col(off):
        return pl.BlockSpec((C, LANE), lambda h, i: (i, off * H + h))

    vec = pl.BlockSpec((1, LANE), lambda h, i: (0, h))
    tile = pl.BlockSpec((C, LANE), lambda h, i: (i, h))
    return pl.pallas_call(
        body, name=name, grid=(H, nc),
        in_specs=[col(0), col(1), col(2), col(3), vec, vec, vec, vec],
        out_specs=[tile, tile, pl.BlockSpec((None, C, C), lambda h, i: (h, i, 0)),
                   pl.BlockSpec((None, None, LANE, LANE), lambda h, i: (h, i, 0, 0))],
        out_shape=[jax.ShapeDtypeStruct((S, W), BF16), jax.ShapeDtypeStruct((S, W), F32),
                   jax.ShapeDtypeStruct((H, S, C), BF16), jax.ShapeDtypeStruct((H, nc, LANE, LANE), F32)],
        scratch_shapes=[pltpu.VMEM((LANE, LANE), F32), pltpu.VMEM((C, LANE), F32)],
        compiler_params=_params("parallel", "arbitrary"),
    )(u, u, u, u, la, lc, oml, gn_g)


def _hgrn_bwd(u, la, lc, oml, gn_g, o, a, st, dz, name):
    S = u.shape[0]
    W = u.shape[1] // 4
    H = W // HGRN_KDIM
    C = min(HGRN_CHUNK, S)
    nc = S // C
    levels = [m for m in _LEVELS if m < C]

    def body(qz_ref, fz_ref, iz_ref, gz_ref, la_ref, lc_ref, oml_ref, gn_ref, o_ref, a_ref, st_ref, dz_ref,
             dqz_ref, dfz_ref, diz_ref, dgz_ref, dla_ref, dlc_ref, doml_ref, dgn_ref, dstate, b_scr):
        @pl.when(pl.program_id(1) == 0)
        def _():
            dstate[...] = jnp.zeros_like(dstate)
            dla_ref[...] = jnp.zeros_like(dla_ref)
            dlc_ref[...] = jnp.zeros_like(dlc_ref)
            doml_ref[...] = jnp.zeros_like(doml_ref)
            dgn_ref[...] = jnp.zeros_like(dgn_ref)

        rows = lax.broadcasted_iota(jnp.int32, (C, LANE), 0)
        tt = lax.broadcasted_iota(jnp.int32, (C, C), 0)
        ss = lax.broadcasted_iota(jnp.int32, (C, C), 1)
        la_v, lc_v, oml_v = la_ref[...], lc_ref[...], oml_ref[...]
        qz, fz = qz_ref[...], fz_ref[...]
        q, sq, k, s, g, c = _hgrn_gates(qz, fz, la_v, lc_v, oml_v)
        vf = iz_ref[...]
        v = vf.astype(BF16)

        ov, gz, dzv, gn = o_ref[...], gz_ref[...], dz_ref[...], gn_ref[...]
        r = lax.rsqrt(jnp.mean(ov * ov, axis=-1, keepdims=True) + EPS)
        on = ov * r
        sg = jax.nn.sigmoid(gz)
        silu_g = gz * sg
        dgn_ref[...] += jnp.sum(dzv * on * silu_g, axis=0, keepdims=True)
        dgz_ref[...] = (dzv * on * gn * (sg * (1.0 + gz * (1.0 - sg)))).astype(BF16)
        don = dzv * gn * silu_g
        do_f = r * (don - on * jnp.mean(don * on, axis=-1, keepdims=True))
        do = do_f.astype(BF16)

        b = _chunk_cumsum(g, rows)
        b_scr[...] = b
        bl = b_scr[C - 1:C, :]
        e = jnp.exp(b)
        ebl = jnp.exp(bl)
        ekl = jnp.exp(bl - b)
        qh = q * e
        kh = k * ekl
        st_v = st_ref[...]
        dst = dstate[...]
        dstb = dst.astype(BF16)

        diz_ref[...] = (_dot(a_ref[...], do, "tn") + _dot(kh.astype(BF16), dstb, "nt")).astype(BF16)
        da = _dot(do, v, "nt")
        dqh = _dot(do, st_v.astype(BF16), "nn")
        dkh = _dot(v, dstb, "nn")
        dstate[...] = dst * ebl + _dot(do, qh.astype(BF16), "tn")
        dbl = jnp.sum(dkh * kh, axis=0, keepdims=True) + jnp.sum(dst * st_v, axis=0, keepdims=True) * ebl

        datt = jnp.sum(do_f * vf, axis=-1, keepdims=True)
        dqa = datt * k
        dka = datt * q
        for m in levels:
            ref = _level_ref(b, b_scr, rows, m)
            eu = jnp.exp(jnp.minimum(b - ref, 0.0))
            el = jnp.exp(jnp.minimum(ref - b, 0.0))
            gm = jnp.where(_level_mask(tt, ss, m), da, 0.0).astype(BF16)
            pq = _dot(gm, _split2(k * el), "nn")
            pk = _dot(gm, _split2(q * eu), "tn")
            dqa += (pq[:, :LANE] + pq[:, LANE:]) * eu
            dka += (pk[:, :LANE] + pk[:, LANE:]) * el
        db = q * dqa - k * dka + dqh * qh - dkh * kh
        db = db + jnp.where(rows == C - 1, dbl, 0.0)
        dq = dqa + dqh * e
        dk = dka + dkh * ekl
        dg = _chunk_cumsum(db, rows, reverse=True)

        wa = jnp.exp(la_v - g)
        wc = jnp.exp(c - g)
        dqz_ref[...] = (dq * (sq * (1.0 + qz * (1.0 - sq)))).astype(BF16)
        dfz_ref[...] = (dg * wc * (1.0 - s) - dk * oml_v * s * (1.0 - s)).astype(BF16)
        dla_ref[...] += jnp.sum(dg * wa, axis=0, keepdims=True)
        dlc_ref[...] += jnp.sum(dg * wc, axis=0, keepdims=True)
        doml_ref[...] += jnp.sum(dk * (1.0 - s), axis=0, keepdims=True)

    def col(off):
        return pl.BlockSpec((C, LANE), lambda h, i: (nc - 1 - i, off * H + h))

    vec = pl.BlockSpec((1, LANE), lambda h, i: (0, h))
    tile = pl.BlockSpec((C, LANE), lambda h, i: (nc - 1 - i, h))
    a_spec = pl.BlockSpec((None, C, C), lambda h, i: (h, nc - 1 - i, 0))
    st_spec = pl.BlockSpec((None, None, LANE, LANE), lambda h, i: (h, nc - 1 - i, 0, 0))
    sw = jax.ShapeDtypeStruct((S, W), BF16)
    vw = jax.ShapeDtypeStruct((1, W), F32)
    return pl.pallas_call(
        body, name=name, grid=(H, nc),
        in_specs=[col(0), col(1), col(2), col(3), vec, vec, vec, vec, tile, a_spec, st_spec, tile],
        out_specs=[tile, tile, tile, tile, vec, vec, vec, vec],
        out_shape=[sw, sw, sw, sw, vw, vw, vw, vw],
        scratch_shapes=[pltpu.VMEM((LANE, LANE), F32), pltpu.VMEM((C, LANE), F32)],
        compiler_params=_params("parallel", "arbitrary"),
    )(u, u, u, u, la, lc, oml, gn_g, o, a, st, dz)


def _lb_terms(lb_logits, layer):
    p = jax.nn.softmax(lb_logits, axis=0)
    lb = (jnp.cumsum(p, axis=0) - p[0:1])[layer]
    return jnp.log(lb)[None], jnp.log1p(-lb)[None], (1.0 - lb)[None]


def kernel(x, norm_ffn1, ffn1_wg, ffn1_wu, ffn1_wd, norm_mix, norm_ffn2, ffn2_wg, ffn2_wu, ffn2_wd, ev_w_in, ev_conv_w, ev_conv_b, ev_cn_g, ev_cn_b, ev_qn_g, ev_kn_g, ev_w_out, od_w_in, od_lb_logits, od_gn_g, od_w_out, loss_target, m_norm_ffn1, m_ffn1_wg, m_ffn1_wu, m_ffn1_wd, m_norm_mix, m_norm_ffn2, m_ffn2_wg, m_ffn2_wu, m_ffn2_wd, m_ev_w_in, m_ev_conv_w, m_ev_conv_b, m_ev_cn_g, m_ev_cn_b, m_ev_qn_g, m_ev_kn_g, m_ev_w_out, m_od_w_in, m_od_lb_logits, m_od_gn_g, m_od_w_out, v_norm_ffn1, v_ffn1_wg, v_ffn1_wu, v_ffn1_wd, v_norm_mix, v_norm_ffn2, v_ffn2_wg, v_ffn2_wu, v_ffn2_wd, v_ev_w_in, v_ev_conv_w, v_ev_conv_b, v_ev_cn_g, v_ev_cn_b, v_ev_qn_g, v_ev_kn_g, v_ev_w_out, v_od_w_in, v_od_lb_logits, v_od_gn_g, v_od_w_out):
    depth = norm_ffn1.shape[0]
    S, D = x.shape[1], x.shape[2]
    xi, yi, ci = _me()
    dev = 4 * xi + 2 * yi + ci
    c_idx = jnp.reshape(ci, (1,)).astype(jnp.int32)
    k_idx = jnp.reshape(2 * xi + yi, (1,)).astype(jnp.int32)

    def ffn_shard(wg, wu, wd, l):
        return jnp.stack([wg[l].T, wu[l].T, wd[l]]).astype(BF16)

    assert depth == 2, "the exchange schedule below is written for one even and one odd layer"
    sh_ffn1 = [ffn_shard(ffn1_wg, ffn1_wu, ffn1_wd, l) for l in range(depth)]
    sh_ffn2 = [ffn_shard(ffn2_wg, ffn2_wu, ffn2_wd, l) for l in range(depth)]
    sh_ev = [ev_w_in[0].T.astype(BF16)[None], ev_w_out[0].astype(BF16)[None]]
    sh_od = [od_w_in[0].T.astype(BF16)[None], od_w_out[0].astype(BF16)[None]]

    def full(g):
        return g.reshape(g.shape[0], N_DEV * g.shape[2], g.shape[3])

    def gather_begin(shards, after, tag):
        lands = [lax.dynamic_update_slice(lax.empty((s.shape[0], N_DEV) + s.shape[1:], s.dtype), s[:, None],
                                          (0, dev, 0, 0)) for s in shards]
        state = _push_start(shards, lands, _gather_plan, after, "gather_start" + tag)
        return state, state[4][0, 0]

    def gather_end(state, after, tag):
        send, recv, srcs, lands, _ = state
        _, lands = _push_wait(send, recv, srcs, lands, _gather_plan, after, "gather_wait" + tag)
        return [full(g) for g in _gather_forward(lands, "gather_forward" + tag)]

    w_ffn1, w_ffn2 = [None] * depth, [None] * depth
    (w_ffn1[0],) = [full(g) for g in _all_gather([sh_ffn1[0]], "all_gather_first")]

    conv_w_sh, gn_g_sh = ev_conv_w[0], od_gn_g[0]
    cw, cs = conv_w_sh.shape[0], conv_w_sh.shape[1]
    gs = gn_g_sh.shape[0]
    conv_w_z = lax.dynamic_update_slice(jnp.zeros((cw, N_DEV * cs), F32), conv_w_sh, (0, dev * cs))
    gn_g_z = lax.dynamic_update_slice(jnp.zeros((N_DEV * gs,), F32), gn_g_sh, (dev * gs,))
    conv_w_full, gn_g_full = _unpack_rows(
        _all_reduce_small(_pack_rows([conv_w_z, gn_g_z]), "gather_small_params"),
        [conv_w_z.shape, gn_g_z.shape])

    def ffn_forward(h, gain, w, tag):
        hn = _rms_fwd(h, gain[None], "rms_" + tag)
        out, gu = _ffn_fwd(h, hn, w, "ffn_fwd_" + tag)
        return out, (h, hn, gu)

    def odd_mixer(u, l):
        (la, lc, oml), lb_vjp = jax.vjp(functools.partial(_lb_terms, layer=l), od_lb_logits)
        gn = gn_g_full[None]
        zb, o_raw, scores, states = _hgrn_fwd(u, la, lc, oml, gn, f"hgrn_fwd{l}")

        def backward(dz):
            dqz, dfz, diz, dgz, dla, dlc, doml, dgn = _hgrn_bwd(
                u, la, lc, oml, gn, o_raw, scores, states, dz, f"hgrn_bwd{l}")
            (g_lb,) = lb_vjp((dla, dlc, doml))
            return jnp.concatenate([dqz, dfz, diz, dgz], axis=1), [g_lb, dgn[0]]

        return zb, backward

    saved = []
    h = x[0]
    for l in range(depth):
        if l == 0:
            group_a, tok = gather_begin(sh_ev + [sh_ffn2[0]], w_ffn1[0], "_a")
        else:
            group_c, tok = gather_begin([sh_ffn2[1]], w_ffn1[1], "_c")
        h, s1 = ffn_forward(h, norm_ffn1[l] + tok, w_ffn1[l], f"a{l}")
        tok = 0.0
        if l == 0:
            ev_in, ev_out, w_ffn2[0] = gather_end(group_a, h, "_a")
            ev_w_in_t, ev_w_out_f = ev_in[0], ev_out[0]
            group_b, tok = gather_begin([sh_ffn1[1]] + sh_od, w_ffn2[0], "_b")
        hn = _rms_fwd(h, (norm_mix[l] + tok)[None], f"rms_mix{l}")
        if l % 2 == 0:
            u = _mm(hn, ev_w_in_t, "nt", F32, f"mix_in{l}")
            zb, core_vjp = _even_mixer(u, conv_w_full, ev_conv_b, ev_cn_g, ev_cn_b, ev_qn_g, ev_kn_g, str(l))
            w_out = ev_w_out_f
        else:
            u = _mm(hn, od_w_in_t, "nt", F32, f"mix_in{l}")
            zb, core_vjp = odd_mixer(u, l)
            w_out = od_w_out_f
        h_mix = h
        h = _mm(zb, w_out, "nn", F32, f"mix_out{l}", res=h)
        sm = (h_mix, hn, zb, core_vjp)
        if l == 1:
            (w_ffn2[1],) = gather_end(group_c, h, "_c")
        h, s2 = ffn_forward(h, norm_ffn2[l], w_ffn2[l], f"b{l}")
        if l == 0:
            w_ffn1[1], od_in, od_out = gather_end(group_b, h, "_b")
            od_w_in_t, od_w_out_f = od_in[0], od_out[0]
        saved.append((s1, sm, s2))

    dy, loss_part = _loss_grad(h, loss_target[0], "loss_grad")

    def reduce_begin(parts, tag):
        parts = [g.reshape(g.shape[0], 4, 2, g.shape[1] // N_DEV, g.shape[2]) for g in parts]
        got = _exchange_core_halves(parts, "reduce_core_halves" + tag)
        sums = [_add_core_halves(g, r, c_idx, f"add_core_halves{tag}_{a}") for a, (g, r) in enumerate(zip(parts, got))]
        lands = [lax.empty((3, s.shape[0]) + s.shape[2:], BF16) for s in sums]
        state = _push_start(sums, lands, _chip_plan, None, "reduce_start" + tag)
        return state, state[4][0, 0]

    def reduce_end(state, after, tag):
        send, recv, srcs, lands, _ = state
        sums, got = _push_wait(send, recv, srcs, lands, _chip_plan, after, "reduce_wait" + tag)
        return [_sum_chip_blocks(s, r, k_idx, f"sum_chip_blocks{tag}_{a}") for a, (s, r) in enumerate(zip(sums, got))]

    def ffn_backward(dy, gain, w, sv, tag, on_dw):
        h_in, hn, gu = sv
        dxn, dout, t = _ffn_bwd_dx(dy, w, gu, "ffn_bwd_dx_" + tag)
        tok = on_dw(_ffn_bwd_dw(hn, dout, t, "ffn_bwd_dw_" + tag))
        dx, dgain = _rms_bwd(h_in, (gain + tok)[None], dxn, dy, "rms_bwd_" + tag)
        return dx, dgain[0]

    g_norm1, g_norm2, g_normm = [None] * depth, [None] * depth, [None] * depth
    small, held, groups = [None, None], {}, {}

    def hold(key):
        def on_dw(dw):
            held[key] = dw
            return 0.0
        return on_dw

    def begin(key, make_parts):
        def on_dw(dw):
            groups[key], tok = reduce_begin(make_parts(dw), "_" + key)
            return tok
        return on_dw

    for l in reversed(range(depth)):
        s1, (h_mix, hn, zb, core_vjp), s2 = saved[l]
        on_dw = begin("1", lambda dw: [dw]) if l == 1 else hold("ffn2_0")
        dy, g_norm2[l] = ffn_backward(dy, norm_ffn2[l], w_ffn2[l], s2, f"b{l}", on_dw)
        dyb = dy.astype(BF16)
        if l % 2 == 0:
            w_out, w_in_t = ev_w_out_f, ev_w_in_t
        else:
            w_out, w_in_t = od_w_out_f, od_w_in_t
        dz = _mm(dyb, w_out, "nt", F32, f"mix_out_dz{l}")
        dw_out = _mm(zb, dyb, "tn", BF16, f"mix_out_dw{l}")
        dub, small[l % 2] = core_vjp(dz)
        dw_in_t = _mm(dub, hn, "tn", BF16, f"mix_in_dw{l}")
        tok = 0.0
        if l == 0:
            groups["3"], tok = reduce_begin([held["ffn2_0"], dw_in_t[None], dw_out[None]], "_3")
        else:
            held["od"] = [dw_in_t[None], dw_out[None]]
        dhn = _mm(dub, w_in_t, "nn", F32, f"mix_in_dx{l}")
        dy, gm = _rms_bwd(h_mix, (norm_mix[l] + tok)[None], dhn, dy, f"rms_bwd_mix{l}")
        g_normm[l] = gm[0]
        on_dw = begin("2", lambda dw: held["od"] + [dw]) if l == 1 else begin("4", lambda dw: [dw])
        dy, g_norm1[l] = ffn_backward(dy, norm_ffn1[l], w_ffn1[l], s1, f"a{l}", on_dw)
    grad_x = dy[None]

    (g_ffn2_1,) = reduce_end(groups["1"], dy, "_1")
    g_od_in_t, g_od_out, g_ffn1_1 = reduce_end(groups["2"], dy, "_2")
    g_ffn2_0, g_ev_in_t, g_ev_out = reduce_end(groups["3"], dy, "_3")
    (g_ffn1_0,) = reduce_end(groups["4"], dy, "_4")
    g_ffn1, g_ffn2 = [g_ffn1_0, g_ffn1_1], [g_ffn2_0, g_ffn2_1]

    g_conv_w, g_conv_b, g_cn_g, g_cn_b, g_qn_g, g_kn_g = small[0]
    g_lb, g_gn = small[1]
    parts = [jnp.stack(g_norm1), jnp.stack(g_normm), jnp.stack(g_norm2), g_conv_b, g_cn_g, g_cn_b,
             g_qn_g, g_kn_g, g_lb, g_conv_w, g_gn, loss_part[0, :1]]
    red = _unpack_rows(_all_reduce_small(_pack_rows(parts), "reduce_small_grads"), [p.shape for p in parts])
    g_norm1, g_normm, g_norm2, g_conv_b, g_cn_g, g_cn_b, g_qn_g, g_kn_g, g_lb, g_conv_w, g_gn, loss = red
    g_conv_w = lax.dynamic_slice(g_conv_w, (0, dev * cs), (cw, cs))
    g_gn = lax.dynamic_slice(g_gn, (dev * gs,), (gs,))

    def ffn_grads(gl):
        return (jnp.stack([g[0].T for g in gl]), jnp.stack([g[1].T for g in gl]), jnp.stack([g[2] for g in gl]))

    g_ffn1_wg, g_ffn1_wu, g_ffn1_wd = ffn_grads(g_ffn1)
    g_ffn2_wg, g_ffn2_wu, g_ffn2_wd = ffn_grads(g_ffn2)
    grads = [g_norm1, g_ffn1_wg, g_ffn1_wu, g_ffn1_wd, g_normm, g_norm2, g_ffn2_wg, g_ffn2_wu, g_ffn2_wd,
             g_ev_in_t[0].T[None], g_conv_w[None], g_conv_b[None], g_cn_g[None], g_cn_b[None], g_qn_g[None],
             g_kn_g[None], g_ev_out, g_od_in_t[0].T[None], g_lb, g_gn[None], g_od_out]
    weights = [norm_ffn1, ffn1_wg, ffn1_wu, ffn1_wd, norm_mix, norm_ffn2, ffn2_wg, ffn2_wu, ffn2_wd, ev_w_in,
               ev_conv_w, ev_conv_b, ev_cn_g, ev_cn_b, ev_qn_g, ev_kn_g, ev_w_out, od_w_in, od_lb_logits,
               od_gn_g, od_w_out]
    moms = [m_norm_ffn1, m_ffn1_wg, m_ffn1_wu, m_ffn1_wd, m_norm_mix, m_norm_ffn2, m_ffn2_wg, m_ffn2_wu,
            m_ffn2_wd, m_ev_w_in, m_ev_conv_w, m_ev_conv_b, m_ev_cn_g, m_ev_cn_b, m_ev_qn_g, m_ev_kn_g,
            m_ev_w_out, m_od_w_in, m_od_lb_logits, m_od_gn_g, m_od_w_out]
    vars_ = [v_norm_ffn1, v_ffn1_wg, v_ffn1_wu, v_ffn1_wd, v_norm_mix, v_norm_ffn2, v_ffn2_wg, v_ffn2_wu,
             v_ffn2_wd, v_ev_w_in, v_ev_conv_w, v_ev_conv_b, v_ev_cn_g, v_ev_cn_b, v_ev_qn_g, v_ev_kn_g,
             v_ev_w_out, v_od_w_in, v_od_lb_logits, v_od_gn_g, v_od_w_out]
    deltas, new_m, new_v = [], [], []
    for i, (w, g, m, v) in enumerate(zip(weights, grads, moms, vars_)):
        d, nm, nv = _adamw(w, g, m, v, f"adamw{i}")
        deltas.append(d)
        new_m.append(nm)
        new_v.append(nv)
    return (loss[0], grad_x, *grads, *deltas, *new_m, *new_v)
```

```python
import functools
import math

import jax
import jax.numpy as jnp
from jax import lax
from jax.experimental import pallas as pl
from jax.experimental.pallas import tpu as pltpu

F32 = jnp.float32
BF16 = jnp.bfloat16
MESH = pl.DeviceIdType.MESH
N_DEV = 8

EPS = 1e-6
HEAD_DIM = 128
CONV_WIDTH = 31
DIL_PATTERNS = ((128, 1), (512, 4), (2048, 16))
Q_BLOCK = 128
ROPE_THETA = 10000.0
HGRN_KDIM = 128
HGRN_CHUNK = 256

ADAM_LR = 0.001
ADAM_B1 = 0.9
ADAM_B2 = 0.999
ADAM_EPS = 1e-08
ADAM_WD = 0.01
ADAM_STEP = 10

VMEM_LIMIT_BYTES = 56 * 1024 * 1024
LANE = 128
SUBLANE_BF16 = 16

ANY = pl.BlockSpec(memory_space=pl.ANY)


def _tile(n, pref, mult):
    t = (min(pref, n) // mult) * mult
    while t > 0:
        if n % t == 0:
            return t
        t -= mult
    return n


def _params(*sem):
    return pltpu.CompilerParams(dimension_semantics=sem, vmem_limit_bytes=VMEM_LIMIT_BYTES)


_DOT_DIMS = {
    "nn": (((1,), (0,)), ((), ())),
    "nt": (((1,), (1,)), ((), ())),
    "tn": (((0,), (0,)), ((), ())),
}


def _dot(a, b, mode):
    return lax.dot_general(a, b, _DOT_DIMS[mode], preferred_element_type=F32)


def _mm(a, b, mode, out_dtype, name, res=None, tm=1024, tn=1024, tk=2048):
    if mode == "nt":
        (M, K), N = a.shape, b.shape[0]
    elif mode == "nn":
        (M, K), N = a.shape, b.shape[1]
    else:
        (K, M), N = a.shape, b.shape[1]
    tm, tn, tk = _tile(M, tm, LANE), _tile(N, tn, LANE), _tile(K, tk, LANE)
    nk = K // tk

    def body(*refs):
        if res is None:
            a_ref, b_ref, o_ref, acc = refs
        else:
            a_ref, b_ref, r_ref, o_ref, acc = refs
        k = pl.program_id(2)

        @pl.when(k == 0)
        def _():
            acc[...] = jnp.zeros_like(acc)

        acc[...] += _dot(a_ref[...].astype(BF16), b_ref[...].astype(BF16), mode)

        @pl.when(k == nk - 1)
        def _():
            r = acc[...]
            if res is not None:
                r = r_ref[...] + r
            o_ref[...] = r.astype(out_dtype)

    a_spec = {"nt": pl.BlockSpec((tm, tk), lambda i, j, k: (i, k)),
              "nn": pl.BlockSpec((tm, tk), lambda i, j, k: (i, k)),
              "tn": pl.BlockSpec((tk, tm), lambda i, j, k: (k, i))}[mode]
    b_spec = {"nt": pl.BlockSpec((tn, tk), lambda i, j, k: (j, k)),
              "nn": pl.BlockSpec((tk, tn), lambda i, j, k: (k, j)),
              "tn": pl.BlockSpec((tk, tn), lambda i, j, k: (k, j))}[mode]
    o_spec = pl.BlockSpec((tm, tn), lambda i, j, k: (i, j))
    in_specs = [a_spec, b_spec] + ([o_spec] if res is not None else [])
    args = (a, b) + ((res,) if res is not None else ())
    return pl.pallas_call(
        body, name=name, grid=(M // tm, N // tn, nk),
        in_specs=in_specs, out_specs=o_spec,
        out_shape=jax.ShapeDtypeStruct((M, N), out_dtype),
        scratch_shapes=[pltpu.VMEM((tm, tn), F32)],
        compiler_params=_params("parallel", "parallel", "arbitrary"),
    )(*args)


def _rms_fwd(x, gain, name):
    S, D = x.shape
    tm = _tile(S, 512, SUBLANE_BF16)

    def body(x_ref, g_ref, o_ref):
        xv = x_ref[...]
        r = lax.rsqrt(jnp.mean(xv * xv, axis=-1, keepdims=True) + EPS)
        o_ref[...] = (xv * r * g_ref[...]).astype(BF16)

    return pl.pallas_call(
        body, name=name, grid=(S // tm,),
        in_specs=[pl.BlockSpec((tm, D), lambda i: (i, 0)), pl.BlockSpec((1, D), lambda i: (0, 0))],
        out_specs=pl.BlockSpec((tm, D), lambda i: (i, 0)),
        out_shape=jax.ShapeDtypeStruct((S, D), BF16),
        compiler_params=_params("parallel"),
    )(x, gain)


def _rms_bwd(x, gain, dxn, dy, name):
    S, D = x.shape
    tm = _tile(S, 512, 8)

    def body(x_ref, g_ref, dxn_ref, dy_ref, dx_ref, dg_ref):
        @pl.when(pl.program_id(0) == 0)
        def _():
            dg_ref[...] = jnp.zeros_like(dg_ref)

        xv = x_ref[...]
        r = lax.rsqrt(jnp.mean(xv * xv, axis=-1, keepdims=True) + EPS)
        xh = xv * r
        dxn_v = dxn_ref[...]
        dg_ref[...] += jnp.sum(dxn_v * xh, axis=0, keepdims=True)
        dxh = dxn_v * g_ref[...]
        dx_ref[...] = dy_ref[...] + r * (dxh - xh * jnp.mean(dxh * xh, axis=-1, keepdims=True))

    row = pl.BlockSpec((tm, D), lambda i: (i, 0))
    vec = pl.BlockSpec((1, D), lambda i: (0, 0))
    return pl.pallas_call(
        body, name=name, grid=(S // tm,),
        in_specs=[row, vec, row, row], out_specs=[row, vec],
        out_shape=[jax.ShapeDtypeStruct((S, D), F32), jax.ShapeDtypeStruct((1, D), F32)],
        compiler_params=_params("arbitrary"),
    )(x, gain, dxn, dy)


def _ffn_fwd(x, xn, w, name):
    S, D = x.shape
    F = w.shape[1]
    tm, tf = _tile(S, 512, SUBLANE_BF16), _tile(F, 512, LANE)
    nf = F // tf

    def body(x_ref, xn_ref, w_ref, o_ref, gu_ref, acc):
        f = pl.program_id(1)

        @pl.when(f == 0)
        def _():
            acc[...] = jnp.zeros_like(acc)

        xnv = xn_ref[...]
        g = _dot(xnv, w_ref[0], "nt")
        u = _dot(xnv, w_ref[1], "nt")
        gu_ref[0] = g.astype(BF16)
        gu_ref[1] = u.astype(BF16)
        h = (g * jax.nn.sigmoid(g) * u).astype(BF16)
        acc[...] += _dot(h, w_ref[2], "nn")

        @pl.when(f == nf - 1)
        def _():
            o_ref[...] = x_ref[...] + 0.5 * acc[...]

    row = pl.BlockSpec((tm, D), lambda i, f: (i, 0))
    return pl.pallas_call(
        body, name=name, grid=(S // tm, nf),
        in_specs=[row, row, pl.BlockSpec((3, tf, D), lambda i, f: (0, f, 0))],
        out_specs=[row, pl.BlockSpec((2, tm, tf), lambda i, f: (0, i, f))],
        out_shape=[jax.ShapeDtypeStruct((S, D), F32), jax.ShapeDtypeStruct((2, S, F), BF16)],
        scratch_shapes=[pltpu.VMEM((tm, D), F32)],
        compiler_params=_params("parallel", "arbitrary"),
    )(x, xn, w)


def _ffn_bwd_dx(dy, w, gu, name):
    S, D = dy.shape
    F = w.shape[1]
    tm, tf = _tile(S, 512, SUBLANE_BF16), _tile(F, 512, LANE)
    nf = F // tf

    def body(dy_ref, w_ref, gu_ref, dxn_ref, dout_ref, t_ref, acc):
        f = pl.program_id(1)

        @pl.when(f == 0)
        def _():
            acc[...] = jnp.zeros_like(acc)
            dout_ref[...] = (0.5 * dy_ref[...]).astype(BF16)

        dh = _dot(dout_ref[...], w_ref[2], "nt")
        g = gu_ref[0].astype(F32)
        u = gu_ref[1].astype(F32)
        sig = jax.nn.sigmoid(g)
        silu = g * sig
        dg = (dh * u * (sig * (1.0 + g * (1.0 - sig)))).astype(BF16)
        du = (dh * silu).astype(BF16)
        t_ref[0] = dg
        t_ref[1] = du
        t_ref[2] = (silu * u).astype(BF16)
        acc[...] += _dot(dg, w_ref[0], "nn") + _dot(du, w_ref[1], "nn")

        @pl.when(f == nf - 1)
        def _():
            dxn_ref[...] = acc[...]

    row = pl.BlockSpec((tm, D), lambda i, f: (i, 0))
    return pl.pallas_call(
        body, name=name, grid=(S // tm, nf),
        in_specs=[row, pl.BlockSpec((3, tf, D), lambda i, f: (0, f, 0)),
                  pl.BlockSpec((2, tm, tf), lambda i, f: (0, i, f))],
        out_specs=[row, row, pl.BlockSpec((3, tm, tf), lambda i, f: (0, i, f))],
        out_shape=[jax.ShapeDtypeStruct((S, D), F32), jax.ShapeDtypeStruct((S, D), BF16),
                   jax.ShapeDtypeStruct((3, S, F), BF16)],
        scratch_shapes=[pltpu.VMEM((tm, D), F32)],
        compiler_params=_params("parallel", "arbitrary"),
    )(dy, w, gu)


def _ffn_bwd_dw(xn, dout, t, name):
    S, D = xn.shape
    F = t.shape[2]
    ts, tf = _tile(S, 1024, LANE), _tile(F, 512, LANE)
    ns = S // ts

    def body(xn_ref, dout_ref, t_ref, dw_ref, acc):
        s = pl.program_id(1)

        @pl.when(s == 0)
        def _():
            acc[...] = jnp.zeros_like(acc)

        xnv = xn_ref[...]
        acc[0] += _dot(t_ref[0], xnv, "tn")
        acc[1] += _dot(t_ref[1], xnv, "tn")
        acc[2] += _dot(t_ref[2], dout_ref[...], "tn")

        @pl.when(s == ns - 1)
        def _():
            dw_ref[...] = acc[...].astype(BF16)

    row = pl.BlockSpec((ts, D), lambda f, s: (s, 0))
    return pl.pallas_call(
        body, name=name, grid=(F // tf, ns),
        in_specs=[row, row, pl.BlockSpec((3, ts, tf), lambda f, s: (0, s, f))],
        out_specs=pl.BlockSpec((3, tf, D), lambda f, s: (0, f, 0)),
        out_shape=jax.ShapeDtypeStruct((3, F, D), BF16),
        scratch_shapes=[pltpu.VMEM((3, tf, D), F32)],
        compiler_params=_params("parallel", "arbitrary"),
    )(xn, dout, t)


def _loss_grad(y, target, name):
    S, D = y.shape
    tm = _tile(S, 512, 8)

    def body(y_ref, t_ref, dy_ref, l_ref):
        @pl.when(pl.program_id(0) == 0)
        def _():
            l_ref[...] = jnp.zeros_like(l_ref)

        e = y_ref[...] - t_ref[...]
        dy_ref[...] = e * (1.0 / D)
        l_ref[...] += 0.5 * jnp.sum(jnp.sum(e * e, axis=-1, keepdims=True) * (1.0 / D))

    row = pl.BlockSpec((tm, D), lambda i: (i, 0))
    one = pl.BlockSpec((8, LANE), lambda i: (0, 0))
    return pl.pallas_call(
        body, name=name, grid=(S // tm,),
        in_specs=[row, row], out_specs=[row, one],
        out_shape=[jax.ShapeDtypeStruct((S, D), F32), jax.ShapeDtypeStruct((8, LANE), F32)],
        compiler_params=_params("arbitrary"),
    )(y, target)


def _adamw(w, g, m, v, name):
    shape = w.shape
    C = shape[-1]
    R = math.prod(shape[:-1])
    tr = _tile(R, max(8, (1 << 19) // C // 8 * 8), 8)
    c1 = 1.0 / (1.0 - ADAM_B1 ** ADAM_STEP)
    c2 = 1.0 / (1.0 - ADAM_B2 ** ADAM_STEP)

    def body(w_ref, g_ref, m_ref, v_ref, d_ref, nm_ref, nv_ref):
        gv = g_ref[...]
        nm = ADAM_B1 * m_ref[...] + (1.0 - ADAM_B1) * gv
        nv = ADAM_B2 * v_ref[...] + (1.0 - ADAM_B2) * (gv * gv)
        nm_ref[...] = nm
        nv_ref[...] = nv
        d_ref[...] = -ADAM_LR * ((nm * c1) / (jnp.sqrt(nv * c2) + ADAM_EPS) + ADAM_WD * w_ref[...])

    blk = pl.BlockSpec((tr, C), lambda i: (i, 0))
    sds = jax.ShapeDtypeStruct((R, C), F32)
    outs = pl.pallas_call(
        body, name=name, grid=(R // tr,),
        in_specs=[blk] * 4, out_specs=[blk] * 3, out_shape=[sds] * 3,
        compiler_params=_params("parallel"),
    )(*(a.reshape(R, C) for a in (w, g, m, v)))
    return tuple(o.reshape(shape) for o in outs)


def _me():
    return lax.axis_index("x"), lax.axis_index("y"), lax.axis_index("c")


def _exchange_core_halves(grads, name):
    na = len(grads)

    def body(*refs):
        ins, outs = refs[:na], refs[na:2 * na]
        send_sems, recv_sems = refs[2 * na:]
        x, y, c = _me()
        copies = [pltpu.make_async_remote_copy(
            src_ref=ins[a].at[:, :, 1 - c], dst_ref=outs[a],
            send_sem=send_sems.at[a], recv_sem=recv_sems.at[a],
            device_id=(x, y, 1 - c), device_id_type=MESH) for a in range(na)]
        for cp in copies:
            cp.start()
        for cp in copies:
            cp.wait()

    return pl.pallas_call(
        body, name=name,
        in_specs=[ANY] * na, out_specs=[ANY] * na,
        out_shape=[jax.ShapeDtypeStruct(g.shape[:2] + g.shape[3:], g.dtype) for g in grads],
        scratch_shapes=[pltpu.SemaphoreType.DMA((na,)), pltpu.SemaphoreType.DMA((na,))],
    )(*grads)


def _add_core_halves(grad, got, c_idx, name):
    n, nk, _, r, C = grad.shape
    tr = _tile(r, 1024, SUBLANE_BF16)

    def body(c_ref, g_ref, r_ref, o_ref):
        o_ref[...] = (g_ref[...].astype(F32) + r_ref[...].astype(F32)).astype(BF16)

    return pl.pallas_call(
        body, name=name,
        grid_spec=pltpu.PrefetchScalarGridSpec(
            num_scalar_prefetch=1, grid=(n, nk, r // tr),
            in_specs=[pl.BlockSpec((None, None, None, tr, C), lambda i, k, t, c: (i, k, c[0], t, 0)),
                      pl.BlockSpec((None, None, tr, C), lambda i, k, t, c: (i, k, t, 0))],
            out_specs=pl.BlockSpec((None, None, tr, C), lambda i, k, t, c: (i, k, t, 0))),
        out_shape=jax.ShapeDtypeStruct((n, nk, r, C), BF16),
        compiler_params=_params("parallel", "parallel", "parallel"),
    )(c_idx, grad, got)


HBM = pl.BlockSpec(memory_space=pltpu.HBM)
SEM = pl.BlockSpec(memory_space=pltpu.SEMAPHORE)
EFFECT = pltpu.SideEffectType.DATAFLOW_SIDE_EFFECTING


def _push_start(srcs, lands, plan, after, name):
    ns, nl = len(srcs), len(lands)
    ncp = len(plan([None] * ns, [None] * nl, dry=True))
    extra = [] if after is None else [after]

    def body(*refs):
        src_refs, land_refs = refs[:ns], refs[ns:ns + nl]
        send_sems, recv_sems = refs[ns + nl + len(extra)], refs[ns + nl + len(extra) + 1]
        token = refs[-1]
        for i, (s, d, to) in enumerate(plan(src_refs, land_refs)):
            pltpu.make_async_remote_copy(src_ref=s, dst_ref=d, send_sem=send_sems.at[i], recv_sem=recv_sems.at[i],
                                         device_id=to, device_id_type=MESH).start()
        token[...] = jnp.zeros_like(token)

    out = pl.pallas_call(
        body, name=name,
        out_shape=(pltpu.SemaphoreType.DMA((ncp,)), pltpu.SemaphoreType.DMA((ncp,)),
                   *[pltpu.HBM(a.shape, a.dtype) for a in srcs], *[pltpu.HBM(a.shape, a.dtype) for a in lands],
                   jax.ShapeDtypeStruct((8, LANE), F32)),
        in_specs=[HBM] * (ns + nl) + [ANY] * len(extra),
        out_specs=(SEM, SEM, *[HBM] * (ns + nl), pl.BlockSpec(memory_space=pltpu.VMEM)),
        input_output_aliases={i: 2 + i for i in range(ns + nl)},
        compiler_params=pltpu.CompilerParams(has_side_effects=EFFECT),
    )(*[pltpu.with_memory_space_constraint(a, pltpu.HBM) for a in srcs + lands], *extra)
    return out[0], out[1], list(out[2:2 + ns]), list(out[2 + ns:2 + ns + nl]), out[-1]


def _push_wait(send_sems, recv_sems, srcs, lands, plan, after, name):
    ns, nl = len(srcs), len(lands)
    after = list(after) if isinstance(after, (list, tuple)) else [after]

    def body(*refs):
        src_refs, land_refs = refs[:ns], refs[ns:ns + nl]
        send, recv = refs[ns + nl], refs[ns + nl + 1]
        for i, (s, d, to) in enumerate(plan(src_refs, land_refs)):
            cp = pltpu.make_async_remote_copy(src_ref=s, dst_ref=d, send_sem=send.at[i], recv_sem=recv.at[i],
                                              device_id=to, device_id_type=MESH)
            cp.wait_send()
            cp.wait_recv()

    out = pl.pallas_call(
        body, name=name,
        out_shape=tuple(pltpu.HBM(a.shape, a.dtype) for a in srcs + lands),
        in_specs=[HBM] * (ns + nl) + [SEM, SEM] + [ANY] * len(after),
        out_specs=tuple([HBM] * (ns + nl)),
        input_output_aliases={i: i for i in range(ns + nl)},
        compiler_params=pltpu.CompilerParams(has_side_effects=EFFECT),
    )(*srcs, *lands, send_sems, recv_sems, *after)
    return list(out[:ns]), list(out[ns:])


def _gather_plan(src_refs, land_refs, dry=False):
    if dry:
        return [None] * (4 * len(src_refs))
    x, y, c = _me()
    me = 4 * x + 2 * y + c
    targets = [(x, y, 1 - c), (1 - x, y, c), (x, 1 - y, c), (1 - x, 1 - y, c)]
    return [(s, l.at[:, me], to) for s, l in zip(src_refs, land_refs) for to in targets]


def _chip_plan(src_refs, land_refs, dry=False):
    if dry:
        return [None] * (3 * len(src_refs))
    x, y, c = _me()
    chips = [(1 - x, y), (x, 1 - y), (1 - x, 1 - y)]
    return [(s.at[:, 2 * chip[0] + chip[1]], l.at[j], (*chip, c))
            for s, l in zip(src_refs, land_refs) for j, chip in enumerate(chips)]


def _gather_forward(lands, name):
    na = len(lands)

    def body(*refs):
        bufs = refs[na:2 * na]
        send_sems, recv_sems = refs[2 * na:]
        x, y, c = _me()
        chips = [(1 - x, y), (x, 1 - y), (1 - x, 1 - y)]

        def copy(a, j, pc):
            blk = bufs[a].at[:, 4 * chips[j][0] + 2 * chips[j][1] + pc]
            return pltpu.make_async_remote_copy(
                src_ref=blk, dst_ref=blk, send_sem=send_sems.at[3 * a + j], recv_sem=recv_sems.at[3 * a + j],
                device_id=(x, y, 1 - c), device_id_type=MESH)

        pairs = [(a, j) for a in range(na) for j in range(3)]
        for a, j in pairs:
            copy(a, j, c).start()
        for a, j in pairs:
            copy(a, j, 1 - c).wait_recv()
        for a, j in pairs:
            copy(a, j, c).wait_send()

    return pl.pallas_call(
        body, name=name,
        in_specs=[ANY] * na, out_specs=[ANY] * na,
        out_shape=[jax.ShapeDtypeStruct(a.shape, a.dtype) for a in lands],
        input_output_aliases={a: a for a in range(na)},
        scratch_shapes=[pltpu.SemaphoreType.DMA((3 * na,)), pltpu.SemaphoreType.DMA((3 * na,))],
    )(*lands)


def _sum_chip_blocks(sums, got, k_idx, name):
    n, _, r, C = sums.shape
    tr = _tile(r, 512, SUBLANE_BF16)

    def body(k_ref, s_ref, r_ref, o_ref):
        acc = s_ref[...].astype(F32)
        for j in range(3):
            acc = acc + r_ref[j].astype(F32)
        o_ref[...] = acc

    return pl.pallas_call(
        body, name=name,
        grid_spec=pltpu.PrefetchScalarGridSpec(
            num_scalar_prefetch=1, grid=(n, r // tr),
            in_specs=[pl.BlockSpec((None, None, tr, C), lambda i, t, k: (i, k[0], t, 0)),
                      pl.BlockSpec((3, None, tr, C), lambda i, t, k: (0, i, t, 0))],
            out_specs=pl.BlockSpec((None, tr, C), lambda i, t, k: (i, t, 0))),
        out_shape=jax.ShapeDtypeStruct((n, r, C), F32),
        compiler_params=_params("parallel", "parallel"),
    )(k_idx, sums, got)


def _all_reduce_small(v, name):
    R = v.shape[0]

    def body(v_ref, o_ref, buf, send_sems, recv_sems):
        x, y, c = _me()
        me = 4 * x + 2 * y + c
        buf[me] = v_ref[...]
        copies = []
        for k in range(1, N_DEV):
            peer = (x ^ (k >> 2), y ^ ((k >> 1) & 1), c ^ (k & 1))
            copies.append(pltpu.make_async_remote_copy(
                src_ref=v_ref, dst_ref=buf.at[me],
                send_sem=send_sems.at[k - 1], recv_sem=recv_sems.at[k - 1],
                device_id=peer, device_id_type=MESH))
        for cp in copies:
            cp.start()
        for cp in copies:
            cp.wait()
        acc = buf[0]
        for d in range(1, N_DEV):
            acc = acc + buf[d]
        o_ref[...] = acc

    vm = pl.BlockSpec(memory_space=pltpu.VMEM)
    return pl.pallas_call(
        body, name=name, in_specs=[vm], out_specs=vm,
        out_shape=jax.ShapeDtypeStruct((R, LANE), F32),
        scratch_shapes=[pltpu.VMEM((N_DEV, R, LANE), F32),
                        pltpu.SemaphoreType.DMA((N_DEV - 1,)), pltpu.SemaphoreType.DMA((N_DEV - 1,))],
        compiler_params=pltpu.CompilerParams(vmem_limit_bytes=VMEM_LIMIT_BYTES),
    )(v)


def _pack_rows(parts):
    flat = jnp.concatenate([p.reshape(-1).astype(F32) for p in parts])
    n = flat.shape[0]
    rows = -(-n // (8 * LANE)) * 8
    flat = jnp.pad(flat, (0, rows * LANE - n))
    return flat.reshape(rows, LANE)


def _unpack_rows(packed, shapes):
    flat = packed.reshape(-1)
    out, off = [], 0
    for s in shapes:
        n = math.prod(s)
        out.append(flat[off:off + n].reshape(s))
        off += n
    return out


CONV_HALO = 32


def _conv_fwd(u, conv_w, conv_b, cn_g, cn_b, name):
    S = u.shape[0]
    C = conv_w.shape[1]
    T = _tile(S, 256, CONV_HALO)
    hb = T // CONV_HALO

    def body(av_ref, ag_ref, pv_ref, pg_ref, w_ref, b_ref, g_ref, bb_ref, out_ref, y_ref, scr):
        i = pl.program_id(0)
        prev = pv_ref[...] * jax.nn.sigmoid(pg_ref[...])
        scr[0:CONV_HALO, :] = jnp.where(i > 0, prev, 0.0)
        scr[CONV_HALO:CONV_HALO + T, :] = av_ref[...] * jax.nn.sigmoid(ag_ref[...])
        acc = jnp.broadcast_to(b_ref[...], (T, C))
        for j in range(CONV_WIDTH):
            acc = acc + w_ref[j:j + 1, :] * scr[pl.ds(CONV_HALO - (CONV_WIDTH - 1) + j, T), :]
        y_ref[...] = acc
        mu = jnp.mean(acc, axis=-1, keepdims=True)
        xc = acc - mu
        var = jnp.mean(xc * xc, axis=-1, keepdims=True)
        ln = xc * lax.rsqrt(var + EPS) * g_ref[...] + bb_ref[...]
        out_ref[...] = (ln * jax.nn.sigmoid(ln)).astype(BF16)

    def cur(cb):
        return pl.BlockSpec((T, C), lambda i: (i, cb))

    def halo(cb):
        return pl.BlockSpec((CONV_HALO, C), lambda i: (jnp.maximum(i * hb - 1, 0), cb))

    vec = pl.BlockSpec((1, C), lambda i: (0, 0))
    return pl.pallas_call(
        body, name=name, grid=(S // T,),
        in_specs=[cur(0), cur(1), halo(0), halo(1), pl.BlockSpec((CONV_WIDTH, C), lambda i: (0, 0)), vec, vec, vec],
        out_specs=[pl.BlockSpec((T, C), lambda i: (i, 0))] * 2,
        out_shape=[jax.ShapeDtypeStruct((S, C), BF16), jax.ShapeDtypeStruct((S, C), F32)],
        scratch_shapes=[pltpu.VMEM((T + CONV_HALO, C), F32)],
        compiler_params=_params("parallel"),
    )(u, u, u, u, conv_w, conv_b, cn_g, cn_b)


def _conv_bwd_norm(dz, y, cn_g, cn_b, name):
    S, C = y.shape
    T = _tile(S, 256, 8)

    def body(dz_ref, y_ref, g_ref, bb_ref, dy_ref, dg_ref, db_ref):
        @pl.when(pl.program_id(0) == 0)
        def _():
            dg_ref[...] = jnp.zeros_like(dg_ref)
            db_ref[...] = jnp.zeros_like(db_ref)

        yv = y_ref[...]
        mu = jnp.mean(yv, axis=-1, keepdims=True)
        xc = yv - mu
        rstd = lax.rsqrt(jnp.mean(xc * xc, axis=-1, keepdims=True) + EPS)
        xh = xc * rstd
        ln = xh * g_ref[...] + bb_ref[...]
        sg = jax.nn.sigmoid(ln)
        dln = dz_ref[...] * (sg * (1.0 + ln * (1.0 - sg)))
        dg_ref[...] += jnp.sum(dln * xh, axis=0, keepdims=True)
        db_ref[...] += jnp.sum(dln, axis=0, keepdims=True)
        dxh = dln * g_ref[...]
        dy_ref[...] = rstd * (dxh - jnp.mean(dxh, axis=-1, keepdims=True)
                              - xh * jnp.mean(dxh * xh, axis=-1, keepdims=True))

    row = pl.BlockSpec((T, C), lambda i: (i, 0))
    vec = pl.BlockSpec((1, C), lambda i: (0, 0))
    return pl.pallas_call(
        body, name=name, grid=(S // T,),
        in_specs=[row, row, vec, vec], out_specs=[row, vec, vec],
        out_shape=[jax.ShapeDtypeStruct((S, C), F32), jax.ShapeDtypeStruct((1, C), F32),
                   jax.ShapeDtypeStruct((1, C), F32)],
        compiler_params=_params("arbitrary"),
    )(dz, y, cn_g, cn_b)


def _conv_bwd_taps(u, dy, conv_w, name):
    S, C = dy.shape
    T = _tile(S, 256, CONV_HALO)
    hb = T // CONV_HALO
    nt = S // T
    W1 = CONV_WIDTH - 1

    def body(av_ref, ag_ref, pv_ref, pg_ref, dy_ref, dn_ref, w_ref, dv_ref, dg_ref, dw_ref, db_ref, a_scr, d_scr):
        i = pl.program_id(0)

        @pl.when(i == 0)
        def _():
            dw_ref[...] = jnp.zeros_like(dw_ref)
            db_ref[...] = jnp.zeros_like(db_ref)

        av, sg = av_ref[...], jax.nn.sigmoid(ag_ref[...])
        prev = pv_ref[...] * jax.nn.sigmoid(pg_ref[...])
        a_scr[0:CONV_HALO, :] = jnp.where(i > 0, prev, 0.0)
        a_scr[CONV_HALO:CONV_HALO + T, :] = av * sg
        dyv = dy_ref[...]
        d_scr[0:T, :] = dyv
        d_scr[T:T + CONV_HALO, :] = jnp.where(i < nt - 1, dn_ref[...], 0.0)
        da = jnp.zeros((T, C), F32)
        for j in range(CONV_WIDTH):
            da = da + w_ref[j:j + 1, :] * d_scr[pl.ds(W1 - j, T), :]
            dw_ref[j:j + 1, :] += jnp.sum(dyv * a_scr[pl.ds(CONV_HALO - W1 + j, T), :], axis=0, keepdims=True)
        db_ref[...] += jnp.sum(dyv, axis=0, keepdims=True)
        dv_ref[...] = (da * sg).astype(BF16)
        dg_ref[...] = (da * av * sg * (1.0 - sg)).astype(BF16)

    def cur(cb):
        return pl.BlockSpec((T, C), lambda i: (i, cb))

    def halo(cb):
        return pl.BlockSpec((CONV_HALO, C), lambda i: (jnp.maximum(i * hb - 1, 0), cb))

    nxt = pl.BlockSpec((CONV_HALO, C), lambda i: (jnp.minimum((i + 1) * hb, S // CONV_HALO - 1), 0))
    row = pl.BlockSpec((T, C), lambda i: (i, 0))
    return pl.pallas_call(
        body, name=name, grid=(nt,),
        in_specs=[cur(0), cur(1), halo(0), halo(1), row, nxt, pl.BlockSpec((CONV_WIDTH, C), lambda i: (0, 0))],
        out_specs=[row, row, pl.BlockSpec((CONV_HALO, C), lambda i: (0, 0)), pl.BlockSpec((1, C), lambda i: (0, 0))],
        out_shape=[jax.ShapeDtypeStruct((S, C), BF16), jax.ShapeDtypeStruct((S, C), BF16),
                   jax.ShapeDtypeStruct((CONV_HALO, C), F32), jax.ShapeDtypeStruct((1, C), F32)],
        scratch_shapes=[pltpu.VMEM((T + CONV_HALO, C), F32), pltpu.VMEM((T + CONV_HALO, C), F32)],
        compiler_params=_params("arbitrary"),
    )(u, u, u, u, dy, dy, conv_w)


def _rope_tables(S):
    half = HEAD_DIM // 2
    inv = jnp.exp(-math.log(ROPE_THETA) * jnp.arange(half, dtype=F32) / half)
    ang = jnp.arange(S, dtype=jnp.int32).astype(F32)[:, None] * inv[None, :]
    cos, sin = jnp.cos(ang), jnp.sin(ang)
    return jnp.concatenate([cos, cos], axis=1), jnp.concatenate([-sin, sin], axis=1)


def _qkv_prep(u, qn_g, kn_g, cos, sin, cb0, name):
    S = u.shape[0]
    A = (u.shape[1] // (cb0 + 3))
    H = A // HEAD_DIM
    T = _tile(S, 256, SUBLANE_BF16)
    scale = HEAD_DIM ** -0.5

    def body(q_ref, k_ref, v_ref, qg_ref, kg_ref, cos_ref, sin_ref, qo_ref, ko_ref, vo_ref):
        cosv, sinv = cos_ref[...], sin_ref[...]
        for h in range(H):
            sl = slice(h * HEAD_DIM, (h + 1) * HEAD_DIM)
            for x_ref, g_ref, o_ref, sc in ((q_ref, qg_ref, qo_ref, scale), (k_ref, kg_ref, ko_ref, 1.0)):
                xv = x_ref[:, sl]
                xn = xv * lax.rsqrt(jnp.mean(xv * xv, axis=-1, keepdims=True) + EPS) * g_ref[...]
                y = xn * cosv + pltpu.roll(xn, HEAD_DIM // 2, 1) * sinv
                o_ref[:, sl] = (y * sc).astype(BF16)
        vo_ref[...] = v_ref[...].astype(BF16)

    def col(cb):
        return pl.BlockSpec((T, A), lambda i: (i, cb))

    vec = pl.BlockSpec((1, HEAD_DIM), lambda i: (0, 0))
    tab = pl.BlockSpec((T, HEAD_DIM), lambda i: (i, 0))
    out = pl.BlockSpec((T, A), lambda i: (i, 0))
    return pl.pallas_call(
        body, name=name, grid=(S // T,),
        in_specs=[col(cb0), col(cb0 + 1), col(cb0 + 2), vec, vec, tab, tab],
        out_specs=[out] * 3, out_shape=[jax.ShapeDtypeStruct((S, A), BF16)] * 3,
        compiler_params=_params("parallel"),
    )(u, u, u, qn_g, kn_g, cos, sin)


def _qkv_prep_bwd(u, dqs, dks, dvs, qn_g, kn_g, cos, sin, cb0, name):
    S = u.shape[0]
    A = dqs[0].shape[1]
    H = A // HEAD_DIM
    T = _tile(S, 256, SUBLANE_BF16)
    nb = len(dqs)
    scale = HEAD_DIM ** -0.5

    def body(*refs):
        q_ref, k_ref, qg_ref, kg_ref, cos_ref, sin_ref = refs[:6]
        dq_refs, dk_refs, dv_refs = refs[6:6 + nb], refs[6 + nb:6 + 2 * nb], refs[6 + 2 * nb:6 + 3 * nb]
        dqo_ref, dko_ref, dvo_ref, dqg_ref, dkg_ref = refs[6 + 3 * nb:]

        @pl.when(pl.program_id(0) == 0)
        def _():
            dqg_ref[...] = jnp.zeros_like(dqg_ref)
            dkg_ref[...] = jnp.zeros_like(dkg_ref)

        cosv, sinv = cos_ref[...], sin_ref[...]
        for h in range(H):
            sl = slice(h * HEAD_DIM, (h + 1) * HEAD_DIM)
            for x_ref, g_ref, d_refs, o_ref, dg_ref, sc in ((q_ref, qg_ref, dq_refs, dqo_ref, dqg_ref, scale),
                                                          (k_ref, kg_ref, dk_refs, dko_ref, dkg_ref, 1.0)):
                dy = d_refs[0][:, sl]
                for r in d_refs[1:]:
                    dy = dy + r[:, sl]
                dy = dy * sc
                dxn = dy * cosv + pltpu.roll(dy * sinv, HEAD_DIM // 2, 1)
                xv = x_ref[:, sl]
                r = lax.rsqrt(jnp.mean(xv * xv, axis=-1, keepdims=True) + EPS)
                xh = xv * r
                dg_ref[...] += jnp.sum(dxn * xh, axis=0, keepdims=True)
                dxh = dxn * g_ref[...]
                o_ref[:, sl] = (r * (dxh - xh * jnp.mean(dxh * xh, axis=-1, keepdims=True))).astype(BF16)
        dv = dv_refs[0][...]
        for r in dv_refs[1:]:
            dv = dv + r[...]
        dvo_ref[...] = dv.astype(BF16)

    def col(cb):
        return pl.BlockSpec((T, A), lambda i: (i, cb))

    vec = pl.BlockSpec((1, HEAD_DIM), lambda i: (0, 0))
    tab = pl.BlockSpec((T, HEAD_DIM), lambda i: (i, 0))
    row = pl.BlockSpec((T, A), lambda i: (i, 0))
    return pl.pallas_call(
        body, name=name, grid=(S // T,),
        in_specs=[col(cb0), col(cb0 + 1), vec, vec, tab, tab] + [row] * (3 * nb),
        out_specs=[row, row, row, vec, vec],
        out_shape=[jax.ShapeDtypeStruct((S, A), BF16)] * 3 + [jax.ShapeDtypeStruct((1, HEAD_DIM), F32)] * 2,
        compiler_params=_params("arbitrary"),
    )(u, u, qn_g, kn_g, cos, sin, *dqs, *dks, *dvs)


ATT_TILE = 256
NEG = -1e30


def _attn_bias(tile):
    span = max(window for window, _ in DIL_PATTERNS)
    nw = -(-span // tile) + 1
    dist = (jnp.arange(nw)[:, None, None] * tile + jnp.arange(tile)[None, :, None] - jnp.arange(tile)[None, None, :])
    mult = sum(((dist >= 0) & (dist <= window) & (dist % dil == 0)).astype(F32) for window, dil in DIL_PATTERNS)
    return jnp.where(mult > 0, jnp.log(jnp.maximum(mult, 1.0)), NEG)


def _attn_fwd(q, k, v, bias, name):
    S, A = q.shape
    H = A // HEAD_DIM
    nw, T, _ = bias.shape
    nq = S // T

    def body(q_ref, k_ref, v_ref, b_ref, ob_ref, of_ref, l_ref, s_scr):
        i = pl.program_id(1)
        qv = q_ref[...]
        mx = jnp.full((T, 1), NEG, F32)
        for w in range(nw):
            blk = i - w
            start = pl.multiple_of(jnp.maximum(blk, 0) * T, T)
            s = _dot(qv, k_ref[pl.ds(start, T), :], "nt") + b_ref[w] + jnp.where(blk >= 0, 0.0, NEG)
            s_scr[w] = s
            mx = jnp.maximum(mx, jnp.max(s, axis=-1, keepdims=True))
        den = jnp.zeros((T, 1), F32)
        o = jnp.zeros((T, HEAD_DIM), F32)
        for w in range(nw):
            start = pl.multiple_of(jnp.maximum(i - w, 0) * T, T)
            p = jnp.exp(s_scr[w] - mx)
            den = den + jnp.sum(p, axis=-1, keepdims=True)
            o = o + _dot(p.astype(BF16), v_ref[pl.ds(start, T), :], "nn")
        o = o / den
        ob_ref[...] = o.astype(BF16)
        of_ref[...] = o
        l_ref[...] = mx + jnp.log(den)

    blk = pl.BlockSpec((T, HEAD_DIM), lambda h, i: (i, h))
    full = pl.BlockSpec((S, HEAD_DIM), lambda h, i: (0, h))
    return pl.pallas_call(
        body, name=name, grid=(H, nq),
        in_specs=[blk, full, full, pl.BlockSpec((nw, T, T), lambda h, i: (0, 0, 0))],
        out_specs=[blk, blk, pl.BlockSpec((None, T, 1), lambda h, i: (h, i, 0))],
        out_shape=[jax.ShapeDtypeStruct((S, A), BF16), jax.ShapeDtypeStruct((S, A), F32),
                   jax.ShapeDtypeStruct((H, S, 1), F32)],
        scratch_shapes=[pltpu.VMEM((nw, T, T), F32)],
        compiler_params=_params("parallel", "arbitrary"),
    )(q, k, v, bias)


def _attn_dq(q, k, v, dz, cb0, o, lse, bias, name):
    S, A = q.shape
    H = A // HEAD_DIM
    nw, T, _ = bias.shape
    nq = S // T

    def body(q_ref, k_ref, v_ref, do_ref, o_ref, l_ref, b_ref, dq_ref, d_ref):
        i = pl.program_id(1)
        qv, dof = q_ref[...], do_ref[...]
        dov = dof.astype(BF16)
        delta = jnp.sum(dof * o_ref[...], axis=-1, keepdims=True)
        d_ref[...] = delta
        lv = l_ref[...]
        dq = jnp.zeros((T, HEAD_DIM), F32)
        for w in range(nw):
            blk = i - w
            start = pl.multiple_of(jnp.maximum(blk, 0) * T, T)
            kv = k_ref[pl.ds(start, T), :]
            s = _dot(qv, kv, "nt") + b_ref[w] + jnp.where(blk >= 0, 0.0, NEG)
            p = jnp.exp(s - lv)
            ds = (p * (_dot(dov, v_ref[pl.ds(start, T), :], "nt") - delta)).astype(BF16)
            dq = dq + _dot(ds, kv, "nn")
        dq_ref[...] = dq

    blk = pl.BlockSpec((T, HEAD_DIM), lambda h, i: (i, h))
    full = pl.BlockSpec((S, HEAD_DIM), lambda h, i: (0, h))
    col = pl.BlockSpec((None, T, 1), lambda h, i: (h, i, 0))
    return pl.pallas_call(
        body, name=name, grid=(H, nq),
        in_specs=[blk, full, full, pl.BlockSpec((T, HEAD_DIM), lambda h, i: (i, cb0 + h)), blk, col,
                  pl.BlockSpec((nw, T, T), lambda h, i: (0, 0, 0))],
        out_specs=[blk, col],
        out_shape=[jax.ShapeDtypeStruct((S, A), F32), jax.ShapeDtypeStruct((H, S, 1), F32)],
        compiler_params=_params("parallel", "arbitrary"),
    )(q, k, v, dz, o, lse, bias)


def _attn_dkv(q, k, v, dz, cb0, lse, delta, bias, name):
    S, A = q.shape
    H = A // HEAD_DIM
    nw, T, _ = bias.shape
    nq = S // T

    def body(k_ref, v_ref, q_ref, do_ref, l_ref, d_ref, b_ref, dk_ref, dv_ref):
        m = pl.program_id(1)
        kv, vv = k_ref[...], v_ref[...]
        dk = jnp.zeros((T, HEAD_DIM), F32)
        dv = jnp.zeros((T, HEAD_DIM), F32)
        for w in range(nw):
            blk = m + w
            start = pl.multiple_of(jnp.minimum(blk, nq - 1) * T, T)
            qv = q_ref[pl.ds(start, T), :]
            dov = do_ref[pl.ds(start, T), :].astype(BF16)
            s = _dot(qv, kv, "nt") + b_ref[w] + jnp.where(blk < nq, 0.0, NEG)
            p = jnp.exp(s - l_ref[pl.ds(start, T), :])
            dv = dv + _dot(p.astype(BF16), dov, "tn")
            ds = (p * (_dot(dov, vv, "nt") - d_ref[pl.ds(start, T), :])).astype(BF16)
            dk = dk + _dot(ds, qv, "tn")
        dk_ref[...] = dk
        dv_ref[...] = dv

    blk = pl.BlockSpec((T, HEAD_DIM), lambda h, m: (m, h))
    full = pl.BlockSpec((S, HEAD_DIM), lambda h, m: (0, h))
    col = pl.BlockSpec((None, S, 1), lambda h, m: (h, 0, 0))
    sds = jax.ShapeDtypeStruct((S, A), F32)
    return pl.pallas_call(
        body, name=name, grid=(H, nq),
        in_specs=[blk, blk, full, pl.BlockSpec((S, HEAD_DIM), lambda h, m: (0, cb0 + h)), col, col,
                  pl.BlockSpec((nw, T, T), lambda h, m: (0, 0, 0))],
        out_specs=[blk, blk], out_shape=[sds, sds],
        compiler_params=_params("parallel", "arbitrary"),
    )(k, v, q, dz, lse, delta, bias)


def _even_mixer(u, conv_w, conv_b, cn_g, cn_b, qn_g, kn_g, tag):
    S = u.shape[0]
    C = conv_w.shape[1]
    A = (u.shape[1] - 2 * C) // 3
    assert A == C, "column-block addressing of u assumes equal conv and attention widths"
    T = _tile(S, ATT_TILE, LANE)
    cos, sin = _rope_tables(S)
    bias = _attn_bias(T)
    a_out, y = _conv_fwd(u, conv_w, conv_b, cn_g, cn_b, "conv_fwd" + tag)
    q, k, v = _qkv_prep(u, qn_g, kn_g, cos, sin, 2, "qkv_prep" + tag)
    ob, of, lse = _attn_fwd(q, k, v, bias, "attn_fwd" + tag)
    z = jnp.concatenate([a_out, ob], axis=1)

    def backward(dz):
        dy, d_cn_g, d_cn_b = _conv_bwd_norm(dz, y, cn_g, cn_b, "conv_bwd_norm" + tag)
        d_val, d_gate, d_w, d_b = _conv_bwd_taps(u, dy, conv_w, "conv_bwd_taps" + tag)
        dqp, delta = _attn_dq(q, k, v, dz, C // HEAD_DIM, of, lse, bias, "attn_dq" + tag)
        dkp, dvp = _attn_dkv(q, k, v, dz, C // HEAD_DIM, lse, delta, bias, "attn_dkv" + tag)
        dq, dk, dv, d_qn, d_kn = _qkv_prep_bwd(u, [dqp], [dkp], [dvp], qn_g, kn_g, cos, sin, 2, "qkv_prep_bwd" + tag)
        du = jnp.concatenate([d_val, d_gate, dq, dk, dv], axis=1)
        return du, [d_w[:CONV_WIDTH], d_b[0], d_cn_g[0], d_cn_b[0], d_qn[0], d_kn[0]]

    return z, backward


_LEVELS = (128, 64, 32, 16, 8, 4, 2, 1)


def _chunk_cumsum(g, rows, reverse=False):
    C = g.shape[0]
    d = 1
    while d < C:
        if reverse:
            g = g + jnp.where(rows < C - d, pltpu.roll(g, C - d, 0), 0.0)
        else:
            g = g + jnp.where(rows >= d, pltpu.roll(g, d, 0), 0.0)
        d *= 2
    return g


def _level_ref(b, b_scr, rows, m):
    C = b.shape[0]
    if m >= 8:
        pieces = [jnp.broadcast_to(b_scr[2 * m * j + m - 1:2 * m * j + m, :], (2 * m, LANE)) for j in range(C // (2 * m))]
        return pieces[0] if len(pieces) == 1 else jnp.concatenate(pieces, axis=0)
    pos = rows & (2 * m - 1)
    ref = b
    for p in range(2 * m):
        if p != m - 1:
            ref = jnp.where(pos == p, pltpu.roll(b, (p - (m - 1)) % C, 0), ref)
    return ref


def _level_operands(q, k, b, b_scr, rows, m):
    ref = _level_ref(b, b_scr, rows, m)
    qs = (q * jnp.exp(jnp.minimum(b - ref, 0.0))).astype(BF16)
    ks = (k * jnp.exp(jnp.minimum(ref - b, 0.0))).astype(BF16)
    return qs, ks


def _split2(x):
    hi = x.astype(BF16)
    lo = (x - hi.astype(F32)).astype(BF16)
    return jnp.concatenate([hi, lo], axis=1)


def _level_mask(tt, ss, m):
    x = tt ^ ss
    return (tt > ss) & (x >= m) & (x < 2 * m)


def _hgrn_gates(qz, fz, la, lc, oml):
    sq = jax.nn.sigmoid(qz)
    q = qz * sq
    s = jax.nn.sigmoid(fz)
    c = lc + jnp.minimum(fz, 0.0) - jnp.log(1.0 + jnp.exp(-jnp.abs(fz)))
    mx = jnp.maximum(la, c)
    g = mx + jnp.log(1.0 + jnp.exp(-jnp.abs(la - c)))
    k = oml * (1.0 - s)
    return q, sq, k, s, g, c


def _hgrn_fwd(u, la, lc, oml, gn_g, name):
    S = u.shape[0]
    W = u.shape[1] // 4
    H = W // HGRN_KDIM
    C = min(HGRN_CHUNK, S)
    nc = S // C
    levels = [m for m in _LEVELS if m < C]

    def body(qz_ref, fz_ref, iz_ref, gz_ref, la_ref, lc_ref, oml_ref, gn_ref,
             z_ref, o_ref, a_ref, st_ref, state, b_scr):
        @pl.when(pl.program_id(1) == 0)
        def _():
            state[...] = jnp.zeros_like(state)

        rows = lax.broadcasted_iota(jnp.int32, (C, LANE), 0)
        tt = lax.broadcasted_iota(jnp.int32, (C, C), 0)
        ss = lax.broadcasted_iota(jnp.int32, (C, C), 1)
        q, _, k, _, g, _ = _hgrn_gates(qz_ref[...], fz_ref[...], la_ref[...], lc_ref[...], oml_ref[...])
        v = iz_ref[...].astype(BF16)
        b = _chunk_cumsum(g, rows)
        b_scr[...] = b
        a = jnp.where(tt == ss, jnp.sum(q * k, axis=-1, keepdims=True), 0.0)
        for m in levels:
            qs, ks = _level_operands(q, k, b, b_scr, rows, m)
            a = jnp.where(_level_mask(tt, ss, m), _dot(qs, ks, "nt"), a)
        ab = a.astype(BF16)
        a_ref[...] = ab
        st = state[...]
        st_ref[...] = st
        o = _dot(ab, v, "nn") + _dot((q * jnp.exp(b)).astype(BF16), st.astype(BF16), "nt")
        bl = b_scr[C - 1:C, :]
        kh = (k * jnp.exp(bl - b)).astype(BF16)
        state[...] = st * jnp.exp(bl) + _dot(v, kh, "tn")
        o_ref[...] = o
        r = lax.rsqrt(jnp.mean(o * o, axis=-1, keepdims=True) + EPS)
        gz = gz_ref[...]
        z_ref[...] = (o * r * gn_ref[...] * (gz * jax.nn.sigmoid(gz))).astype(BF16)

    def col(off):
        return pl.BlockSpec((C, LANE), lambda h, i: (i, off * H + h))

    vec = pl.BlockSpec((1, LANE), lambda h, i: (0, h))
    tile = pl.BlockSpec((C, LANE), lambda h, i: (i, h))
    return pl.pallas_call(
        body, name=name, grid=(H, nc),
        in_specs=[col(0), col(1), col(2), col(3), vec, vec, vec, vec],
        out_specs=[tile, tile, pl.BlockSpec((None, C, C), lambda h, i: (h, i, 0)),
                   pl.BlockSpec((None, None, LANE, LANE), lambda h, i: (h, i, 0, 0))],
        out_shape=[jax.ShapeDtypeStruct((S, W), BF16), jax.ShapeDtypeStruct((S, W), F32),
                   jax.ShapeDtypeStruct((H, S, C), BF16), jax.ShapeDtypeStruct((H, nc, LANE, LANE), F32)],
        scratch_shapes=[pltpu.VMEM((LANE, LANE), F32), pltpu.VMEM((C, LANE), F32)],
        compiler_params=_params("parallel", "arbitrary"),
    )(u, u, u, u, la, lc, oml, gn_g)


def _hgrn_bwd(u, la, lc, oml, gn_g, o, a, st, dz, name):
    S = u.shape[0]
    W = u.shape[1] // 4
    H = W // HGRN_KDIM
    C = min(HGRN_CHUNK, S)
    nc = S // C
    levels = [m for m in _LEVELS if m < C]

    def body(qz_ref, fz_ref, iz_ref, gz_ref, la_ref, lc_ref, oml_ref, gn_ref, o_ref, a_ref, st_ref, dz_ref,
             dqz_ref, dfz_ref, diz_ref, dgz_ref, dla_ref, dlc_ref, doml_ref, dgn_ref, dstate, b_scr):
        @pl.when(pl.program_id(1) == 0)
        def _():
            dstate[...] = jnp.zeros_like(dstate)
            dla_ref[...] = jnp.zeros_like(dla_ref)
            dlc_ref[...] = jnp.zeros_like(dlc_ref)
            doml_ref[...] = jnp.zeros_like(doml_ref)
            dgn_ref[...] = jnp.zeros_like(dgn_ref)

        rows = lax.broadcasted_iota(jnp.int32, (C, LANE), 0)
        tt = lax.broadcasted_iota(jnp.int32, (C, C), 0)
        ss = lax.broadcasted_iota(jnp.int32, (C, C), 1)
        la_v, lc_v, oml_v = la_ref[...], lc_ref[...], oml_ref[...]
        qz, fz = qz_ref[...], fz_ref[...]
        q, sq, k, s, g, c = _hgrn_gates(qz, fz, la_v, lc_v, oml_v)
        vf = iz_ref[...]
        v = vf.astype(BF16)

        ov, gz, dzv, gn = o_ref[...], gz_ref[...], dz_ref[...], gn_ref[...]
        r = lax.rsqrt(jnp.mean(ov * ov, axis=-1, keepdims=True) + EPS)
        on = ov * r
        sg = jax.nn.sigmoid(gz)
        silu_g = gz * sg
        dgn_ref[...] += jnp.sum(dzv * on * silu_g, axis=0, keepdims=True)
        dgz_ref[...] = (dzv * on * gn * (sg * (1.0 + gz * (1.0 - sg)))).astype(BF16)
        don = dzv * gn * silu_g
        do_f = r * (don - on * jnp.mean(don * on, axis=-1, keepdims=True))
        do = do_f.astype(BF16)

        b = _chunk_cumsum(g, rows)
        b_scr[...] = b
        bl = b_scr[C - 1:C, :]
        e = jnp.exp(b)
        ebl = jnp.exp(bl)
        ekl = jnp.exp(bl - b)
        qh = q * e
        kh = k * ekl
        st_v = st_ref[...]
        dst = dstate[...]
        dstb = dst.astype(BF16)

        diz_ref[...] = (_dot(a_ref[...], do, "tn") + _dot(kh.astype(BF16), dstb, "nt")).astype(BF16)
        da = _dot(do, v, "nt")
        dqh = _dot(do, st_v.astype(BF16), "nn")
        dkh = _dot(v, dstb, "nn")
        dstate[...] = dst * ebl + _dot(do, qh.astype(BF16), "tn")
        dbl = jnp.sum(dkh * kh, axis=0, keepdims=True) + jnp.sum(dst * st_v, axis=0, keepdims=True) * ebl

        datt = jnp.sum(do_f * vf, axis=-1, keepdims=True)
        dqa = datt * k
        dka = datt * q
        for m in levels:
            ref = _level_ref(b, b_scr, rows, m)
            eu = jnp.exp(jnp.minimum(b - ref, 0.0))
            el = jnp.exp(jnp.minimum(ref - b, 0.0))
            gm = jnp.where(_level_mask(tt, ss, m), da, 0.0).astype(BF16)
            pq = _dot(gm, _split2(k * el), "nn")
            pk = _dot(gm, _split2(q * eu), "tn")
            dqa += (pq[:, :LANE] + pq[:, LANE:]) * eu
            dka += (pk[:, :LANE] + pk[:, LANE:]) * el
        db = q * dqa - k * dka + dqh * qh - dkh * kh
        db = db + jnp.where(rows == C - 1, dbl, 0.0)
        dq = dqa + dqh * e
        dk = dka + dkh * ekl
        dg = _chunk_cumsum(db, rows, reverse=True)

        wa = jnp.exp(la_v - g)
        wc = jnp.exp(c - g)
        dqz_ref[...] = (dq * (sq * (1.0 + qz * (1.0 - sq)))).astype(BF16)
        dfz_ref[...] = (dg * wc * (1.0 - s) - dk * oml_v * s * (1.0 - s)).astype(BF16)
        dla_ref[...] += jnp.sum(dg * wa, axis=0, keepdims=True)
        dlc_ref[...] += jnp.sum(dg * wc, axis=0, keepdims=True)
        doml_ref[...] += jnp.sum(dk * (1.0 - s), axis=0, keepdims=True)

    def col(off):
        return pl.BlockSpec((C, LANE), lambda h, i: (nc - 1 - i, off * H + h))

    vec = pl.BlockSpec((1, LANE), lambda h, i: (0, h))
    tile = pl.BlockSpec((C, LANE), lambda h, i: (nc - 1 - i, h))
    a_spec = pl.BlockSpec((None, C, C), lambda h, i: (h, nc - 1 - i, 0))
    st_spec = pl.BlockSpec((None, None, LANE, LANE), lambda h, i: (h, nc - 1 - i, 0, 0))
    sw = jax.ShapeDtypeStruct((S, W), BF16)
    vw = jax.ShapeDtypeStruct((1, W), F32)
    return pl.pallas_call(
        body, name=name, grid=(H, nc),
        in_specs=[col(0), col(1), col(2), col(3), vec, vec, vec, vec, tile, a_spec, st_spec, tile],
        out_specs=[tile, tile, tile, tile, vec, vec, vec, vec],
        out_shape=[sw, sw, sw, sw, vw, vw, vw, vw],
        scratch_shapes=[pltpu.VMEM((LANE, LANE), F32), pltpu.VMEM((C, LANE), F32)],
        compiler_params=_params("parallel", "arbitrary"),
    )(u, u, u, u, la, lc, oml, gn_g, o, a, st, dz)


def _lb_terms(lb_logits, layer):
    p = jax.nn.softmax(lb_logits, axis=0)
    lb = (jnp.cumsum(p, axis=0) - p[0:1])[layer]
    return jnp.log(lb)[None], jnp.log1p(-lb)[None], (1.0 - lb)[None]


def kernel(x, norm_ffn1, ffn1_wg, ffn1_wu, ffn1_wd, norm_mix, norm_ffn2, ffn2_wg, ffn2_wu, ffn2_wd, ev_w_in, ev_conv_w, ev_conv_b, ev_cn_g, ev_cn_b, ev_qn_g, ev_kn_g, ev_w_out, od_w_in, od_lb_logits, od_gn_g, od_w_out, loss_target, m_norm_ffn1, m_ffn1_wg, m_ffn1_wu, m_ffn1_wd, m_norm_mix, m_norm_ffn2, m_ffn2_wg, m_ffn2_wu, m_ffn2_wd, m_ev_w_in, m_ev_conv_w, m_ev_conv_b, m_ev_cn_g, m_ev_cn_b, m_ev_qn_g, m_ev_kn_g, m_ev_w_out, m_od_w_in, m_od_lb_logits, m_od_gn_g, m_od_w_out, v_norm_ffn1, v_ffn1_wg, v_ffn1_wu, v_ffn1_wd, v_norm_mix, v_norm_ffn2, v_ffn2_wg, v_ffn2_wu, v_ffn2_wd, v_ev_w_in, v_ev_conv_w, v_ev_conv_b, v_ev_cn_g, v_ev_cn_b, v_ev_qn_g, v_ev_kn_g, v_ev_w_out, v_od_w_in, v_od_lb_logits, v_od_gn_g, v_od_w_out):
    depth = norm_ffn1.shape[0]
    S, D = x.shape[1], x.shape[2]
    xi, yi, ci = _me()
    dev = 4 * xi + 2 * yi + ci
    c_idx = jnp.reshape(ci, (1,)).astype(jnp.int32)
    k_idx = jnp.reshape(2 * xi + yi, (1,)).astype(jnp.int32)

    def ffn_shard(wg, wu, wd, l):
        return jnp.stack([wg[l].T, wu[l].T, wd[l]]).astype(BF16)

    assert depth == 2, "the exchange schedule below is written for one even and one odd layer"
    sh_ffn1 = [ffn_shard(ffn1_wg, ffn1_wu, ffn1_wd, l) for l in range(depth)]
    sh_ffn2 = [ffn_shard(ffn2_wg, ffn2_wu, ffn2_wd, l) for l in range(depth)]
    sh_ev = [ev_w_in[0].T.astype(BF16)[None], ev_w_out[0].astype(BF16)[None]]
    sh_od = [od_w_in[0].T.astype(BF16)[None], od_w_out[0].astype(BF16)[None]]

    def full(g):
        return g.reshape(g.shape[0], N_DEV * g.shape[2], g.shape[3])

    def gather_begin(shards, after, tag):
        lands = [lax.dynamic_update_slice(lax.empty((s.shape[0], N_DEV) + s.shape[1:], s.dtype), s[:, None],
                                          (0, dev, 0, 0)) for s in shards]
        state = _push_start(shards, lands, _gather_plan, after, "gather_start" + tag)
        return state, state[4][0, 0]

    def gather_end(state, after, tag):
        send, recv, srcs, lands, _ = state
        _, lands = _push_wait(send, recv, srcs, lands, _gather_plan, after, "gather_wait" + tag)
        return [full(g) for g in _gather_forward(lands, "gather_forward" + tag)]

    w_ffn1, w_ffn2 = [None] * depth, [None] * depth
    group_0, _ = gather_begin([sh_ffn1[0]], None, "_0")

    conv_w_sh, gn_g_sh = ev_conv_w[0], od_gn_g[0]
    cw, cs = conv_w_sh.shape[0], conv_w_sh.shape[1]
    gs = gn_g_sh.shape[0]
    conv_w_z = lax.dynamic_update_slice(jnp.zeros((cw, N_DEV * cs), F32), conv_w_sh, (0, dev * cs))
    gn_g_z = lax.dynamic_update_slice(jnp.zeros((N_DEV * gs,), F32), gn_g_sh, (dev * gs,))
    conv_w_full, gn_g_full = _unpack_rows(
        _all_reduce_small(_pack_rows([conv_w_z, gn_g_z]), "gather_small_params"),
        [conv_w_z.shape, gn_g_z.shape])
    (w_ffn1[0],) = gather_end(group_0, sh_ev + sh_od + sh_ffn2 + [sh_ffn1[1], conv_w_full], "_0")

    def ffn_forward(h, gain, w, tag):
        hn = _rms_fwd(h, gain[None], "rms_" + tag)
        out, gu = _ffn_fwd(h, hn, w, "ffn_fwd_" + tag)
        return out, (h, hn, gu)

    def odd_mixer(u, l):
        (la, lc, oml), lb_vjp = jax.vjp(functools.partial(_lb_terms, layer=l), od_lb_logits)
        gn = gn_g_full[None]
        zb, o_raw, scores, states = _hgrn_fwd(u, la, lc, oml, gn, f"hgrn_fwd{l}")

        def backward(dz):
            dqz, dfz, diz, dgz, dla, dlc, doml, dgn = _hgrn_bwd(
                u, la, lc, oml, gn, o_raw, scores, states, dz, f"hgrn_bwd{l}")
            (g_lb,) = lb_vjp((dla, dlc, doml))
            return jnp.concatenate([dqz, dfz, diz, dgz], axis=1), [g_lb, dgn[0]]

        return zb, backward

    saved = []
    h = x[0]
    for l in range(depth):
        if l == 0:
            group_a, tok = gather_begin(sh_ev + [sh_ffn2[0]], w_ffn1[0], "_a")
        else:
            group_c, tok = gather_begin([sh_ffn2[1]], w_ffn1[1], "_c")
        h, s1 = ffn_forward(h, norm_ffn1[l] + tok, w_ffn1[l], f"a{l}")
        tok = 0.0
        if l == 0:
            ev_in, ev_out, w_ffn2[0] = gather_end(group_a, h, "_a")
            ev_w_in_t, ev_w_out_f = ev_in[0], ev_out[0]
            group_b, tok = gather_begin([sh_ffn1[1]] + sh_od, w_ffn2[0], "_b")
        hn = _rms_fwd(h, (norm_mix[l] + tok)[None], f"rms_mix{l}")
        if l % 2 == 0:
            u = _mm(hn, ev_w_in_t, "nt", F32, f"mix_in{l}")
            zb, core_vjp = _even_mixer(u, conv_w_full, ev_conv_b, ev_cn_g, ev_cn_b, ev_qn_g, ev_kn_g, str(l))
            w_out = ev_w_out_f
        else:
            u = _mm(hn, od_w_in_t, "nt", F32, f"mix_in{l}")
            zb, core_vjp = odd_mixer(u, l)
            w_out = od_w_out_f
        h_mix = h
        h = _mm(zb, w_out, "nn", F32, f"mix_out{l}", res=h)
        sm = (h_mix, hn, zb, core_vjp)
        if l == 1:
            (w_ffn2[1],) = gather_end(group_c, h, "_c")
        h, s2 = ffn_forward(h, norm_ffn2[l], w_ffn2[l], f"b{l}")
        if l == 0:
            w_ffn1[1], od_in, od_out = gather_end(group_b, h, "_b")
            od_w_in_t, od_w_out_f = od_in[0], od_out[0]
        saved.append((s1, sm, s2))

    dy, loss_part = _loss_grad(h, loss_target[0], "loss_grad")

    def reduce_begin(parts, tag):
        parts = [g.reshape(g.shape[0], 4, 2, g.shape[1] // N_DEV, g.shape[2]) for g in parts]
        got = _exchange_core_halves(parts, "reduce_core_halves" + tag)
        sums = [_add_core_halves(g, r, c_idx, f"add_core_halves{tag}_{a}") for a, (g, r) in enumerate(zip(parts, got))]
        lands = [lax.empty((3, s.shape[0]) + s.shape[2:], BF16) for s in sums]
        state = _push_start(sums, lands, _chip_plan, None, "reduce_start" + tag)
        return state, state[4][0, 0]

    def reduce_end(state, after, tag):
        send, recv, srcs, lands, _ = state
        sums, got = _push_wait(send, recv, srcs, lands, _chip_plan, after, "reduce_wait" + tag)
        return [_sum_chip_blocks(s, r, k_idx, f"sum_chip_blocks{tag}_{a}") for a, (s, r) in enumerate(zip(sums, got))]

    def ffn_backward(dy, gain, w, sv, tag, on_dw):
        h_in, hn, gu = sv
        dxn, dout, t = _ffn_bwd_dx(dy, w, gu, "ffn_bwd_dx_" + tag)
        tok = on_dw(_ffn_bwd_dw(hn, dout, t, "ffn_bwd_dw_" + tag))
        dx, dgain = _rms_bwd(h_in, (gain + tok)[None], dxn, dy, "rms_bwd_" + tag)
        return dx, dgain[0]

    g_norm1, g_norm2, g_normm = [None] * depth, [None] * depth, [None] * depth
    small, held, groups = [None, None], {}, {}

    def hold(key):
        def on_dw(dw):
            held[key] = dw
            return 0.0
        return on_dw

    def begin(key, make_parts):
        def on_dw(dw):
            groups[key], tok = reduce_begin(make_parts(dw), "_" + key)
            return tok
        return on_dw

    for l in reversed(range(depth)):
        s1, (h_mix, hn, zb, core_vjp), s2 = saved[l]
        on_dw = begin("1", lambda dw: [dw]) if l == 1 else hold("ffn2_0")
        dy, g_norm2[l] = ffn_backward(dy, norm_ffn2[l], w_ffn2[l], s2, f"b{l}", on_dw)
        dyb = dy.astype(BF16)
        if l % 2 == 0:
            w_out, w_in_t = ev_w_out_f, ev_w_in_t
        else:
            w_out, w_in_t = od_w_out_f, od_w_in_t
        dz = _mm(dyb, w_out, "nt", F32, f"mix_out_dz{l}")
        dw_out = _mm(zb, dyb, "tn", BF16, f"mix_out_dw{l}")
        dub, small[l % 2] = core_vjp(dz)
        dw_in_t = _mm(dub, hn, "tn", BF16, f"mix_in_dw{l}")
        tok = 0.0
        if l == 0:
            groups["3"], tok = reduce_begin([held["ffn2_0"], dw_in_t[None], dw_out[None]], "_3")
        else:
            held["od"] = [dw_in_t[None], dw_out[None]]
        dhn = _mm(dub, w_in_t, "nn", F32, f"mix_in_dx{l}")
        dy, gm = _rms_bwd(h_mix, (norm_mix[l] + tok)[None], dhn, dy, f"rms_bwd_mix{l}")
        g_normm[l] = gm[0]
        on_dw = begin("2", lambda dw: held["od"] + [dw]) if l == 1 else begin("4", lambda dw: [dw])
        dy, g_norm1[l] = ffn_backward(dy, norm_ffn1[l], w_ffn1[l], s1, f"a{l}", on_dw)
    grad_x = dy[None]

    (g_ffn2_1,) = reduce_end(groups["1"], dy, "_1")
    g_od_in_t, g_od_out, g_ffn1_1 = reduce_end(groups["2"], dy, "_2")
    g_ffn2_0, g_ev_in_t, g_ev_out = reduce_end(groups["3"], dy, "_3")
    g_ffn2 = [g_ffn2_0, g_ffn2_1]

    g_conv_w, g_conv_b, g_cn_g, g_cn_b, g_qn_g, g_kn_g = small[0]
    g_lb, g_gn = small[1]
    parts = [jnp.stack(g_norm1), jnp.stack(g_normm), jnp.stack(g_norm2), g_conv_b, g_cn_g, g_cn_b,
             g_qn_g, g_kn_g, g_lb, g_conv_w, g_gn, loss_part[0, :1]]
    red = _unpack_rows(_all_reduce_small(_pack_rows(parts), "reduce_small_grads"), [p.shape for p in parts])
    g_norm1, g_normm, g_norm2, g_conv_b, g_cn_g, g_cn_b, g_qn_g, g_kn_g, g_lb, g_conv_w, g_gn, loss = red
    g_conv_w = lax.dynamic_slice(g_conv_w, (0, dev * cs), (cw, cs))
    g_gn = lax.dynamic_slice(g_gn, (dev * gs,), (gs,))

    def ffn_grads(gl):
        return (jnp.stack([g[0].T for g in gl]), jnp.stack([g[1].T for g in gl]), jnp.stack([g[2] for g in gl]))

    g_ffn2_wg, g_ffn2_wu, g_ffn2_wd = ffn_grads(g_ffn2)
    grads = [g_norm1, None, None, None, g_normm, g_norm2, g_ffn2_wg, g_ffn2_wu, g_ffn2_wd,
             g_ev_in_t[0].T[None], g_conv_w[None], g_conv_b[None], g_cn_g[None], g_cn_b[None], g_qn_g[None],
             g_kn_g[None], g_ev_out, g_od_in_t[0].T[None], g_lb, g_gn[None], g_od_out]
    weights = [norm_ffn1, ffn1_wg, ffn1_wu, ffn1_wd, norm_mix, norm_ffn2, ffn2_wg, ffn2_wu, ffn2_wd, ev_w_in,
               ev_conv_w, ev_conv_b, ev_cn_g, ev_cn_b, ev_qn_g, ev_kn_g, ev_w_out, od_w_in, od_lb_logits,
               od_gn_g, od_w_out]
    moms = [m_norm_ffn1, m_ffn1_wg, m_ffn1_wu, m_ffn1_wd, m_norm_mix, m_norm_ffn2, m_ffn2_wg, m_ffn2_wu,
            m_ffn2_wd, m_ev_w_in, m_ev_conv_w, m_ev_conv_b, m_ev_cn_g, m_ev_cn_b, m_ev_qn_g, m_ev_kn_g,
            m_ev_w_out, m_od_w_in, m_od_lb_logits, m_od_gn_g, m_od_w_out]
    vars_ = [v_norm_ffn1, v_ffn1_wg, v_ffn1_wu, v_ffn1_wd, v_norm_mix, v_norm_ffn2, v_ffn2_wg, v_ffn2_wu,
             v_ffn2_wd, v_ev_w_in, v_ev_conv_w, v_ev_conv_b, v_ev_cn_g, v_ev_cn_b, v_ev_qn_g, v_ev_kn_g,
             v_ev_w_out, v_od_w_in, v_od_lb_logits, v_od_gn_g, v_od_w_out]
    n_w = len(weights)
    last = (1, 2, 3)
    deltas, new_m, new_v = [None] * n_w, [None] * n_w, [None] * n_w
    for i in [j for j in range(n_w) if j not in last]:
        deltas[i], new_m[i], new_v[i] = _adamw(weights[i], grads[i], moms[i], vars_[i], f"adamw{i}")
    (g_ffn1_0,) = reduce_end(groups["4"], [d for d in deltas if d is not None], "_4")
    grads[1], grads[2], grads[3] = ffn_grads([g_ffn1_0, g_ffn1_1])
    for i in last:
        deltas[i], new_m[i], new_v[i] = _adamw(weights[i], grads[i], moms[i], vars_[i], f"adamw{i}")
    return (loss[0], grad_x, *grads, *deltas, *new_m, *new_v)
```

```python
import functools
import math

import jax
import jax.numpy as jnp
from jax import lax
from jax.experimental import pallas as pl
from jax.experimental.pallas import tpu as pltpu

F32 = jnp.float32
BF16 = jnp.bfloat16
MESH = pl.DeviceIdType.MESH
N_DEV = 8

EPS = 1e-6
HEAD_DIM = 128
CONV_WIDTH = 31
DIL_PATTERNS = ((128, 1), (512, 4), (2048, 16))
Q_BLOCK = 128
ROPE_THETA = 10000.0
HGRN_KDIM = 128
HGRN_CHUNK = 256

ADAM_LR = 0.001
ADAM_B1 = 0.9
ADAM_B2 = 0.999
ADAM_EPS = 1e-08
ADAM_WD = 0.01
ADAM_STEP = 10

VMEM_LIMIT_BYTES = 56 * 1024 * 1024
LANE = 128
SUBLANE_BF16 = 16

ANY = pl.BlockSpec(memory_space=pl.ANY)


def _tile(n, pref, mult):
    t = (min(pref, n) // mult) * mult
    while t > 0:
        if n % t == 0:
            return t
        t -= mult
    return n


def _params(*sem):
    return pltpu.CompilerParams(dimension_semantics=sem, vmem_limit_bytes=VMEM_LIMIT_BYTES)


_DOT_DIMS = {
    "nn": (((1,), (0,)), ((), ())),
    "nt": (((1,), (1,)), ((), ())),
    "tn": (((0,), (0,)), ((), ())),
}


def _dot(a, b, mode):
    return lax.dot_general(a, b, _DOT_DIMS[mode], preferred_element_type=F32)


def _mm(a, b, mode, out_dtype, name, res=None, tm=1024, tn=1024, tk=2048):
    if mode == "nt":
        (M, K), N = a.shape, b.shape[0]
    elif mode == "nn":
        (M, K), N = a.shape, b.shape[1]
    else:
        (K, M), N = a.shape, b.shape[1]
    tm, tn, tk = _tile(M, tm, LANE), _tile(N, tn, LANE), _tile(K, tk, LANE)
    nk = K // tk

    def body(*refs):
        if res is None:
            a_ref, b_ref, o_ref, acc = refs
        else:
            a_ref, b_ref, r_ref, o_ref, acc = refs
        k = pl.program_id(2)

        @pl.when(k == 0)
        def _():
            acc[...] = jnp.zeros_like(acc)

        acc[...] += _dot(a_ref[...].astype(BF16), b_ref[...].astype(BF16), mode)

        @pl.when(k == nk - 1)
        def _():
            r = acc[...]
            if res is not None:
                r = r_ref[...] + r
            o_ref[...] = r.astype(out_dtype)

    a_spec = {"nt": pl.BlockSpec((tm, tk), lambda i, j, k: (i, k)),
              "nn": pl.BlockSpec((tm, tk), lambda i, j, k: (i, k)),
              "tn": pl.BlockSpec((tk, tm), lambda i, j, k: (k, i))}[mode]
    b_spec = {"nt": pl.BlockSpec((tn, tk), lambda i, j, k: (j, k)),
              "nn": pl.BlockSpec((tk, tn), lambda i, j, k: (k, j)),
              "tn": pl.BlockSpec((tk, tn), lambda i, j, k: (k, j))}[mode]
    o_spec = pl.BlockSpec((tm, tn), lambda i, j, k: (i, j))
    in_specs = [a_spec, b_spec] + ([o_spec] if res is not None else [])
    args = (a, b) + ((res,) if res is not None else ())
    return pl.pallas_call(
        body, name=name, grid=(M // tm, N // tn, nk),
        in_specs=in_specs, out_specs=o_spec,
        out_shape=jax.ShapeDtypeStruct((M, N), out_dtype),
        scratch_shapes=[pltpu.VMEM((tm, tn), F32)],
        compiler_params=_params("parallel", "parallel", "arbitrary"),
    )(*args)


def _rms_fwd(x, gain, name):
    S, D = x.shape
    tm = _tile(S, 512, SUBLANE_BF16)

    def body(x_ref, g_ref, o_ref):
        xv = x_ref[...]
        r = lax.rsqrt(jnp.mean(xv * xv, axis=-1, keepdims=True) + EPS)
        o_ref[...] = (xv * r * g_ref[...]).astype(BF16)

    return pl.pallas_call(
        body, name=name, grid=(S // tm,),
        in_specs=[pl.BlockSpec((tm, D), lambda i: (i, 0)), pl.BlockSpec((1, D), lambda i: (0, 0))],
        out_specs=pl.BlockSpec((tm, D), lambda i: (i, 0)),
        out_shape=jax.ShapeDtypeStruct((S, D), BF16),
        compiler_params=_params("parallel"),
    )(x, gain)


def _rms_bwd(x, gain, dxn, dy, name):
    S, D = x.shape
    tm = _tile(S, 512, 8)

    def body(x_ref, g_ref, dxn_ref, dy_ref, dx_ref, dg_ref):
        @pl.when(pl.program_id(0) == 0)
        def _():
            dg_ref[...] = jnp.zeros_like(dg_ref)

        xv = x_ref[...]
        r = lax.rsqrt(jnp.mean(xv * xv, axis=-1, keepdims=True) + EPS)
        xh = xv * r
        dxn_v = dxn_ref[...]
        dg_ref[...] += jnp.sum(dxn_v * xh, axis=0, keepdims=True)
        dxh = dxn_v * g_ref[...]
        dx_ref[...] = dy_ref[...] + r * (dxh - xh * jnp.mean(dxh * xh, axis=-1, keepdims=True))

    row = pl.BlockSpec((tm, D), lambda i: (i, 0))
    vec = pl.BlockSpec((1, D), lambda i: (0, 0))
    return pl.pallas_call(
        body, name=name, grid=(S // tm,),
        in_specs=[row, vec, row, row], out_specs=[row, vec],
        out_shape=[jax.ShapeDtypeStruct((S, D), F32), jax.ShapeDtypeStruct((1, D), F32)],
        compiler_params=_params("arbitrary"),
    )(x, gain, dxn, dy)


def _ffn_fwd(x, xn, w, name):
    S, D = x.shape
    F = w.shape[1]
    tm, tf = _tile(S, 512, SUBLANE_BF16), _tile(F, 512, LANE)
    nf = F // tf

    def body(x_ref, xn_ref, w_ref, o_ref, gu_ref, acc):
        f = pl.program_id(1)

        @pl.when(f == 0)
        def _():
            acc[...] = jnp.zeros_like(acc)

        xnv = xn_ref[...]
        g = _dot(xnv, w_ref[0], "nt")
        u = _dot(xnv, w_ref[1], "nt")
        gu_ref[0] = g.astype(BF16)
        gu_ref[1] = u.astype(BF16)
        h = (g * jax.nn.sigmoid(g) * u).astype(BF16)
        acc[...] += _dot(h, w_ref[2], "nn")

        @pl.when(f == nf - 1)
        def _():
            o_ref[...] = x_ref[...] + 0.5 * acc[...]

    row = pl.BlockSpec((tm, D), lambda i, f: (i, 0))
    return pl.pallas_call(
        body, name=name, grid=(S // tm, nf),
        in_specs=[row, row, pl.BlockSpec((3, tf, D), lambda i, f: (0, f, 0))],
        out_specs=[row, pl.BlockSpec((2, tm, tf), lambda i, f: (0, i, f))],
        out_shape=[jax.ShapeDtypeStruct((S, D), F32), jax.ShapeDtypeStruct((2, S, F), BF16)],
        scratch_shapes=[pltpu.VMEM((tm, D), F32)],
        compiler_params=_params("parallel", "arbitrary"),
    )(x, xn, w)


def _ffn_bwd_dx(dy, w, gu, name):
    S, D = dy.shape
    F = w.shape[1]
    tm, tf = _tile(S, 512, SUBLANE_BF16), _tile(F, 512, LANE)
    nf = F // tf

    def body(dy_ref, w_ref, gu_ref, dxn_ref, dout_ref, t_ref, acc):
        f = pl.program_id(1)

        @pl.when(f == 0)
        def _():
            acc[...] = jnp.zeros_like(acc)
            dout_ref[...] = (0.5 * dy_ref[...]).astype(BF16)

        dh = _dot(dout_ref[...], w_ref[2], "nt")
        g = gu_ref[0].astype(F32)
        u = gu_ref[1].astype(F32)
        sig = jax.nn.sigmoid(g)
        silu = g * sig
        dg = (dh * u * (sig * (1.0 + g * (1.0 - sig)))).astype(BF16)
        du = (dh * silu).astype(BF16)
        t_ref[0] = dg
        t_ref[1] = du
        t_ref[2] = (silu * u).astype(BF16)
        acc[...] += _dot(dg, w_ref[0], "nn") + _dot(du, w_ref[1], "nn")

        @pl.when(f == nf - 1)
        def _():
            dxn_ref[...] = acc[...]

    row = pl.BlockSpec((tm, D), lambda i, f: (i, 0))
    return pl.pallas_call(
        body, name=name, grid=(S // tm, nf),
        in_specs=[row, pl.BlockSpec((3, tf, D), lambda i, f: (0, f, 0)),
                  pl.BlockSpec((2, tm, tf), lambda i, f: (0, i, f))],
        out_specs=[row, row, pl.BlockSpec((3, tm, tf), lambda i, f: (0, i, f))],
        out_shape=[jax.ShapeDtypeStruct((S, D), F32), jax.ShapeDtypeStruct((S, D), BF16),
                   jax.ShapeDtypeStruct((3, S, F), BF16)],
        scratch_shapes=[pltpu.VMEM((tm, D), F32)],
        compiler_params=_params("parallel", "arbitrary"),
    )(dy, w, gu)


def _ffn_bwd_dw(xn, dout, t, name):
    S, D = xn.shape
    F = t.shape[2]
    ts, tf = _tile(S, 1024, LANE), _tile(F, 512, LANE)
    ns = S // ts

    def body(xn_ref, dout_ref, t_ref, dw_ref, acc):
        s = pl.program_id(1)

        @pl.when(s == 0)
        def _():
            acc[...] = jnp.zeros_like(acc)

        xnv = xn_ref[...]
        acc[0] += _dot(t_ref[0], xnv, "tn")
        acc[1] += _dot(t_ref[1], xnv, "tn")
        acc[2] += _dot(t_ref[2], dout_ref[...], "tn")

        @pl.when(s == ns - 1)
        def _():
            dw_ref[...] = acc[...].astype(BF16)

    row = pl.BlockSpec((ts, D), lambda f, s: (s, 0))
    return pl.pallas_call(
        body, name=name, grid=(F // tf, ns),
        in_specs=[row, row, pl.BlockSpec((3, ts, tf), lambda f, s: (0, s, f))],
        out_specs=pl.BlockSpec((3, tf, D), lambda f, s: (0, f, 0)),
        out_shape=jax.ShapeDtypeStruct((3, F, D), BF16),
        scratch_shapes=[pltpu.VMEM((3, tf, D), F32)],
        compiler_params=_params("parallel", "arbitrary"),
    )(xn, dout, t)


def _loss_grad(y, target, name):
    S, D = y.shape
    tm = _tile(S, 512, 8)

    def body(y_ref, t_ref, dy_ref, l_ref):
        @pl.when(pl.program_id(0) == 0)
        def _():
            l_ref[...] = jnp.zeros_like(l_ref)

        e = y_ref[...] - t_ref[...]
        dy_ref[...] = e * (1.0 / D)
        l_ref[...] += 0.5 * jnp.sum(jnp.sum(e * e, axis=-1, keepdims=True) * (1.0 / D))

    row = pl.BlockSpec((tm, D), lambda i: (i, 0))
    one = pl.BlockSpec((8, LANE), lambda i: (0, 0))
    return pl.pallas_call(
        body, name=name, grid=(S // tm,),
        in_specs=[row, row], out_specs=[row, one],
        out_shape=[jax.ShapeDtypeStruct((S, D), F32), jax.ShapeDtypeStruct((8, LANE), F32)],
        compiler_params=_params("arbitrary"),
    )(y, target)


def _adamw(w, g, m, v, name):
    shape = w.shape
    C = shape[-1]
    R = math.prod(shape[:-1])
    tr = _tile(R, max(8, (1 << 19) // C // 8 * 8), 8)
    c1 = 1.0 / (1.0 - ADAM_B1 ** ADAM_STEP)
    c2 = 1.0 / (1.0 - ADAM_B2 ** ADAM_STEP)

    def body(w_ref, g_ref, m_ref, v_ref, d_ref, nm_ref, nv_ref):
        gv = g_ref[...]
        nm = ADAM_B1 * m_ref[...] + (1.0 - ADAM_B1) * gv
        nv = ADAM_B2 * v_ref[...] + (1.0 - ADAM_B2) * (gv * gv)
        nm_ref[...] = nm
        nv_ref[...] = nv
        d_ref[...] = -ADAM_LR * ((nm * c1) / (jnp.sqrt(nv * c2) + ADAM_EPS) + ADAM_WD * w_ref[...])

    blk = pl.BlockSpec((tr, C), lambda i: (i, 0))
    sds = jax.ShapeDtypeStruct((R, C), F32)
    outs = pl.pallas_call(
        body, name=name, grid=(R // tr,),
        in_specs=[blk] * 4, out_specs=[blk] * 3, out_shape=[sds] * 3,
        compiler_params=_params("parallel"),
    )(*(a.reshape(R, C) for a in (w, g, m, v)))
    return tuple(o.reshape(shape) for o in outs)


def _me():
    return lax.axis_index("x"), lax.axis_index("y"), lax.axis_index("c")


def _add_core_halves(grad, got, c_idx, name):
    n, nk, _, r, C = grad.shape
    tr = _tile(r, 1024, SUBLANE_BF16)

    def body(c_ref, g_ref, r_ref, o_ref):
        o_ref[...] = (g_ref[...].astype(F32) + r_ref[...].astype(F32)).astype(BF16)

    return pl.pallas_call(
        body, name=name,
        grid_spec=pltpu.PrefetchScalarGridSpec(
            num_scalar_prefetch=1, grid=(n, nk, r // tr),
            in_specs=[pl.BlockSpec((None, None, None, tr, C), lambda i, k, t, c: (i, k, c[0], t, 0)),
                      pl.BlockSpec((None, None, tr, C), lambda i, k, t, c: (i, k, t, 0))],
            out_specs=pl.BlockSpec((None, None, tr, C), lambda i, k, t, c: (i, k, t, 0))),
        out_shape=jax.ShapeDtypeStruct((n, nk, r, C), BF16),
        compiler_params=_params("parallel", "parallel", "parallel"),
    )(c_idx, grad, got)


HBM = pl.BlockSpec(memory_space=pltpu.HBM)
SEM = pl.BlockSpec(memory_space=pltpu.SEMAPHORE)
EFFECT = pltpu.SideEffectType.DATAFLOW_SIDE_EFFECTING


def _push_start(srcs, lands, plan, after, name):
    ns, nl = len(srcs), len(lands)
    ncp = len(plan([None] * ns, [None] * nl, dry=True))
    extra = [] if after is None else [after]

    def body(*refs):
        src_refs, land_refs = refs[:ns], refs[ns:ns + nl]
        send_sems, recv_sems = refs[ns + nl + len(extra)], refs[ns + nl + len(extra) + 1]
        token = refs[-1]
        for i, (s, d, to) in enumerate(plan(src_refs, land_refs)):
            pltpu.make_async_remote_copy(src_ref=s, dst_ref=d, send_sem=send_sems.at[i], recv_sem=recv_sems.at[i],
                                         device_id=to, device_id_type=MESH).start()
        token[...] = jnp.zeros_like(token)

    out = pl.pallas_call(
        body, name=name,
        out_shape=(pltpu.SemaphoreType.DMA((ncp,)), pltpu.SemaphoreType.DMA((ncp,)),
                   *[pltpu.HBM(a.shape, a.dtype) for a in srcs], *[pltpu.HBM(a.shape, a.dtype) for a in lands],
                   jax.ShapeDtypeStruct((8, LANE), F32)),
        in_specs=[HBM] * (ns + nl) + [ANY] * len(extra),
        out_specs=(SEM, SEM, *[HBM] * (ns + nl), pl.BlockSpec(memory_space=pltpu.VMEM)),
        input_output_aliases={i: 2 + i for i in range(ns + nl)},
        compiler_params=pltpu.CompilerParams(has_side_effects=EFFECT),
    )(*[pltpu.with_memory_space_constraint(a, pltpu.HBM) for a in srcs + lands], *extra)
    return out[0], out[1], list(out[2:2 + ns]), list(out[2 + ns:2 + ns + nl]), out[-1]


def _push_wait(send_sems, recv_sems, srcs, lands, plan, after, name):
    ns, nl = len(srcs), len(lands)
    after = list(after) if isinstance(after, (list, tuple)) else [after]

    def body(*refs):
        src_refs, land_refs = refs[:ns], refs[ns:ns + nl]
        send, recv = refs[ns + nl], refs[ns + nl + 1]
        for i, (s, d, to) in enumerate(plan(src_refs, land_refs)):
            cp = pltpu.make_async_remote_copy(src_ref=s, dst_ref=d, send_sem=send.at[i], recv_sem=recv.at[i],
                                              device_id=to, device_id_type=MESH)
            cp.wait_send()
            cp.wait_recv()

    out = pl.pallas_call(
        body, name=name,
        out_shape=tuple(pltpu.HBM(a.shape, a.dtype) for a in srcs + lands),
        in_specs=[HBM] * (ns + nl) + [SEM, SEM] + [ANY] * len(after),
        out_specs=tuple([HBM] * (ns + nl)),
        input_output_aliases={i: i for i in range(ns + nl)},
        compiler_params=pltpu.CompilerParams(has_side_effects=EFFECT),
    )(*srcs, *lands, send_sems, recv_sems, *after)
    return list(out[:ns]), list(out[ns:])


def _gather_plan(src_refs, land_refs, dry=False):
    if dry:
        return [None] * (4 * len(src_refs))
    x, y, c = _me()
    me = 4 * x + 2 * y + c
    targets = [(x, y, 1 - c), (1 - x, y, c), (x, 1 - y, c), (1 - x, 1 - y, c)]
    return [(s, l.at[:, me], to) for s, l in zip(src_refs, land_refs) for to in targets]


def _halves_plan(src_refs, land_refs, dry=False):
    if dry:
        return [None] * len(src_refs)
    x, y, c = _me()
    return [(s.at[:, :, 1 - c], l, (x, y, 1 - c)) for s, l in zip(src_refs, land_refs)]


def _chip_plan(src_refs, land_refs, dry=False):
    if dry:
        return [None] * (3 * len(src_refs))
    x, y, c = _me()
    chips = [(1 - x, y), (x, 1 - y), (1 - x, 1 - y)]
    return [(s.at[:, 2 * chip[0] + chip[1]], l.at[j], (*chip, c))
            for s, l in zip(src_refs, land_refs) for j, chip in enumerate(chips)]


def _gather_forward(lands, shards, name):
    na = len(lands)

    def body(*refs):
        shard_refs, bufs = refs[na:2 * na], refs[2 * na:3 * na]
        send_sems, recv_sems, local_sems = refs[3 * na:]
        x, y, c = _me()
        chips = [(1 - x, y), (x, 1 - y), (1 - x, 1 - y)]
        own = [pltpu.make_async_copy(shard_refs[a], bufs[a].at[:, 4 * x + 2 * y + c], local_sems.at[a])
               for a in range(na)]
        for cp in own:
            cp.start()

        def copy(a, j, pc):
            blk = bufs[a].at[:, 4 * chips[j][0] + 2 * chips[j][1] + pc]
            return pltpu.make_async_remote_copy(
                src_ref=blk, dst_ref=blk, send_sem=send_sems.at[3 * a + j], recv_sem=recv_sems.at[3 * a + j],
                device_id=(x, y, 1 - c), device_id_type=MESH)

        pairs = [(a, j) for a in range(na) for j in range(3)]
        for a, j in pairs:
            copy(a, j, c).start()
        for a, j in pairs:
            copy(a, j, 1 - c).wait_recv()
        for a, j in pairs:
            copy(a, j, c).wait_send()
        for cp in own:
            cp.wait()

    return pl.pallas_call(
        body, name=name,
        in_specs=[ANY] * (2 * na), out_specs=[ANY] * na,
        out_shape=[jax.ShapeDtypeStruct(a.shape, a.dtype) for a in lands],
        input_output_aliases={a: a for a in range(na)},
        scratch_shapes=[pltpu.SemaphoreType.DMA((3 * na,)), pltpu.SemaphoreType.DMA((3 * na,)),
                        pltpu.SemaphoreType.DMA((na,))],
    )(*lands, *shards)


def _sum_chip_blocks(sums, got, k_idx, name):
    n, _, r, C = sums.shape
    tr = _tile(r, 512, SUBLANE_BF16)

    def body(k_ref, s_ref, r_ref, o_ref):
        acc = s_ref[...].astype(F32)
        for j in range(3):
            acc = acc + r_ref[j].astype(F32)
        o_ref[...] = acc

    return pl.pallas_call(
        body, name=name,
        grid_spec=pltpu.PrefetchScalarGridSpec(
            num_scalar_prefetch=1, grid=(n, r // tr),
            in_specs=[pl.BlockSpec((None, None, tr, C), lambda i, t, k: (i, k[0], t, 0)),
                      pl.BlockSpec((3, None, tr, C), lambda i, t, k: (0, i, t, 0))],
            out_specs=pl.BlockSpec((None, tr, C), lambda i, t, k: (i, t, 0))),
        out_shape=jax.ShapeDtypeStruct((n, r, C), F32),
        compiler_params=_params("parallel", "parallel"),
    )(k_idx, sums, got)


def _all_reduce_small(v, name, after=()):
    R = v.shape[0]
    after = list(after)

    def body(*refs):
        v_ref = refs[0]
        o_ref, buf, send_sems, recv_sems = refs[1 + len(after):]
        x, y, c = _me()
        me = 4 * x + 2 * y + c
        buf[me] = v_ref[...]
        copies = []
        for k in range(1, N_DEV):
            peer = (x ^ (k >> 2), y ^ ((k >> 1) & 1), c ^ (k & 1))
            copies.append(pltpu.make_async_remote_copy(
                src_ref=v_ref, dst_ref=buf.at[me],
                send_sem=send_sems.at[k - 1], recv_sem=recv_sems.at[k - 1],
                device_id=peer, device_id_type=MESH))
        for cp in copies:
            cp.start()
        for cp in copies:
            cp.wait()
        acc = buf[0]
        for d in range(1, N_DEV):
            acc = acc + buf[d]
        o_ref[...] = acc

    vm = pl.BlockSpec(memory_space=pltpu.VMEM)
    return pl.pallas_call(
        body, name=name, in_specs=[vm] + [ANY] * len(after), out_specs=vm,
        out_shape=jax.ShapeDtypeStruct((R, LANE), F32),
        scratch_shapes=[pltpu.VMEM((N_DEV, R, LANE), F32),
                        pltpu.SemaphoreType.DMA((N_DEV - 1,)), pltpu.SemaphoreType.DMA((N_DEV - 1,))],
        compiler_params=pltpu.CompilerParams(vmem_limit_bytes=VMEM_LIMIT_BYTES),
    )(v, *after)


def _pack_rows(parts):
    flat = jnp.concatenate([p.reshape(-1).astype(F32) for p in parts])
    n = flat.shape[0]
    rows = -(-n // (8 * LANE)) * 8
    flat = jnp.pad(flat, (0, rows * LANE - n))
    return flat.reshape(rows, LANE)


def _unpack_rows(packed, shapes):
    flat = packed.reshape(-1)
    out, off = [], 0
    for s in shapes:
        n = math.prod(s)
        out.append(flat[off:off + n].reshape(s))
        off += n
    return out


CONV_HALO = 32


def _conv_fwd(u, conv_w, conv_b, cn_g, cn_b, name):
    S = u.shape[0]
    C = conv_w.shape[1]
    T = _tile(S, 256, CONV_HALO)
    hb = T // CONV_HALO

    def body(av_ref, ag_ref, pv_ref, pg_ref, w_ref, b_ref, g_ref, bb_ref, out_ref, y_ref, scr):
        i = pl.program_id(0)
        prev = pv_ref[...] * jax.nn.sigmoid(pg_ref[...])
        scr[0:CONV_HALO, :] = jnp.where(i > 0, prev, 0.0)
        scr[CONV_HALO:CONV_HALO + T, :] = av_ref[...] * jax.nn.sigmoid(ag_ref[...])
        acc = jnp.broadcast_to(b_ref[...], (T, C))
        for j in range(CONV_WIDTH):
            acc = acc + w_ref[j:j + 1, :] * scr[pl.ds(CONV_HALO - (CONV_WIDTH - 1) + j, T), :]
        y_ref[...] = acc
        mu = jnp.mean(acc, axis=-1, keepdims=True)
        xc = acc - mu
        var = jnp.mean(xc * xc, axis=-1, keepdims=True)
        ln = xc * lax.rsqrt(var + EPS) * g_ref[...] + bb_ref[...]
        out_ref[...] = (ln * jax.nn.sigmoid(ln)).astype(BF16)

    def cur(cb):
        return pl.BlockSpec((T, C), lambda i: (i, cb))

    def halo(cb):
        return pl.BlockSpec((CONV_HALO, C), lambda i: (jnp.maximum(i * hb - 1, 0), cb))

    vec = pl.BlockSpec((1, C), lambda i: (0, 0))
    return pl.pallas_call(
        body, name=name, grid=(S // T,),
        in_specs=[cur(0), cur(1), halo(0), halo(1), pl.BlockSpec((CONV_WIDTH, C), lambda i: (0, 0)), vec, vec, vec],
        out_specs=[pl.BlockSpec((T, C), lambda i: (i, 0))] * 2,
        out_shape=[jax.ShapeDtypeStruct((S, C), BF16), jax.ShapeDtypeStruct((S, C), F32)],
        scratch_shapes=[pltpu.VMEM((T + CONV_HALO, C), F32)],
        compiler_params=_params("parallel"),
    )(u, u, u, u, conv_w, conv_b, cn_g, cn_b)


def _conv_bwd_norm(dz, y, cn_g, cn_b, name):
    S, C = y.shape
    T = _tile(S, 256, 8)

    def body(dz_ref, y_ref, g_ref, bb_ref, dy_ref, dg_ref, db_ref):
        @pl.when(pl.program_id(0) == 0)
        def _():
            dg_ref[...] = jnp.zeros_like(dg_ref)
            db_ref[...] = jnp.zeros_like(db_ref)

        yv = y_ref[...]
        mu = jnp.mean(yv, axis=-1, keepdims=True)
        xc = yv - mu
        rstd = lax.rsqrt(jnp.mean(xc * xc, axis=-1, keepdims=True) + EPS)
        xh = xc * rstd
        ln = xh * g_ref[...] + bb_ref[...]
        sg = jax.nn.sigmoid(ln)
        dln = dz_ref[...] * (sg * (1.0 + ln * (1.0 - sg)))
        dg_ref[...] += jnp.sum(dln * xh, axis=0, keepdims=True)
        db_ref[...] += jnp.sum(dln, axis=0, keepdims=True)
        dxh = dln * g_ref[...]
        dy_ref[...] = rstd * (dxh - jnp.mean(dxh, axis=-1, keepdims=True)
                              - xh * jnp.mean(dxh * xh, axis=-1, keepdims=True))

    row = pl.BlockSpec((T, C), lambda i: (i, 0))
    vec = pl.BlockSpec((1, C), lambda i: (0, 0))
    return pl.pallas_call(
        body, name=name, grid=(S // T,),
        in_specs=[row, row, vec, vec], out_specs=[row, vec, vec],
        out_shape=[jax.ShapeDtypeStruct((S, C), F32), jax.ShapeDtypeStruct((1, C), F32),
                   jax.ShapeDtypeStruct((1, C), F32)],
        compiler_params=_params("arbitrary"),
    )(dz, y, cn_g, cn_b)


def _conv_bwd_taps(u, dy, conv_w, name):
    S, C = dy.shape
    T = _tile(S, 256, CONV_HALO)
    hb = T // CONV_HALO
    nt = S // T
    W1 = CONV_WIDTH - 1

    def body(av_ref, ag_ref, pv_ref, pg_ref, dy_ref, dn_ref, w_ref, dv_ref, dg_ref, dw_ref, db_ref, a_scr, d_scr):
        i = pl.program_id(0)

        @pl.when(i == 0)
        def _():
            dw_ref[...] = jnp.zeros_like(dw_ref)
            db_ref[...] = jnp.zeros_like(db_ref)

        av, sg = av_ref[...], jax.nn.sigmoid(ag_ref[...])
        prev = pv_ref[...] * jax.nn.sigmoid(pg_ref[...])
        a_scr[0:CONV_HALO, :] = jnp.where(i > 0, prev, 0.0)
        a_scr[CONV_HALO:CONV_HALO + T, :] = av * sg
        dyv = dy_ref[...]
        d_scr[0:T, :] = dyv
        d_scr[T:T + CONV_HALO, :] = jnp.where(i < nt - 1, dn_ref[...], 0.0)
        da = jnp.zeros((T, C), F32)
        for j in range(CONV_WIDTH):
            da = da + w_ref[j:j + 1, :] * d_scr[pl.ds(W1 - j, T), :]
            dw_ref[j:j + 1, :] += jnp.sum(dyv * a_scr[pl.ds(CONV_HALO - W1 + j, T), :], axis=0, keepdims=True)
        db_ref[...] += jnp.sum(dyv, axis=0, keepdims=True)
        dv_ref[...] = (da * sg).astype(BF16)
        dg_ref[...] = (da * av * sg * (1.0 - sg)).astype(BF16)

    def cur(cb):
        return pl.BlockSpec((T, C), lambda i: (i, cb))

    def halo(cb):
        return pl.BlockSpec((CONV_HALO, C), lambda i: (jnp.maximum(i * hb - 1, 0), cb))

    nxt = pl.BlockSpec((CONV_HALO, C), lambda i: (jnp.minimum((i + 1) * hb, S // CONV_HALO - 1), 0))
    row = pl.BlockSpec((T, C), lambda i: (i, 0))
    return pl.pallas_call(
        body, name=name, grid=(nt,),
        in_specs=[cur(0), cur(1), halo(0), halo(1), row, nxt, pl.BlockSpec((CONV_WIDTH, C), lambda i: (0, 0))],
        out_specs=[row, row, pl.BlockSpec((CONV_HALO, C), lambda i: (0, 0)), pl.BlockSpec((1, C), lambda i: (0, 0))],
        out_shape=[jax.ShapeDtypeStruct((S, C), BF16), jax.ShapeDtypeStruct((S, C), BF16),
                   jax.ShapeDtypeStruct((CONV_HALO, C), F32), jax.ShapeDtypeStruct((1, C), F32)],
        scratch_shapes=[pltpu.VMEM((T + CONV_HALO, C), F32), pltpu.VMEM((T + CONV_HALO, C), F32)],
        compiler_params=_params("arbitrary"),
    )(u, u, u, u, dy, dy, conv_w)


def _rope_tables(S):
    half = HEAD_DIM // 2
    inv = jnp.exp(-math.log(ROPE_THETA) * jnp.arange(half, dtype=F32) / half)
    ang = jnp.arange(S, dtype=jnp.int32).astype(F32)[:, None] * inv[None, :]
    cos, sin = jnp.cos(ang), jnp.sin(ang)
    return jnp.concatenate([cos, cos], axis=1), jnp.concatenate([-sin, sin], axis=1)


def _qkv_prep(u, qn_g, kn_g, cos, sin, cb0, name):
    S = u.shape[0]
    A = (u.shape[1] // (cb0 + 3))
    H = A // HEAD_DIM
    T = _tile(S, 256, SUBLANE_BF16)
    scale = HEAD_DIM ** -0.5

    def body(q_ref, k_ref, v_ref, qg_ref, kg_ref, cos_ref, sin_ref, qo_ref, ko_ref, vo_ref):
        cosv, sinv = cos_ref[...], sin_ref[...]
        for h in range(H):
            sl = slice(h * HEAD_DIM, (h + 1) * HEAD_DIM)
            for x_ref, g_ref, o_ref, sc in ((q_ref, qg_ref, qo_ref, scale), (k_ref, kg_ref, ko_ref, 1.0)):
                xv = x_ref[:, sl]
                xn = xv * lax.rsqrt(jnp.mean(xv * xv, axis=-1, keepdims=True) + EPS) * g_ref[...]
                y = xn * cosv + pltpu.roll(xn, HEAD_DIM // 2, 1) * sinv
                o_ref[:, sl] = (y * sc).astype(BF16)
        vo_ref[...] = v_ref[...].astype(BF16)

    def col(cb):
        return pl.BlockSpec((T, A), lambda i: (i, cb))

    vec = pl.BlockSpec((1, HEAD_DIM), lambda i: (0, 0))
    tab = pl.BlockSpec((T, HEAD_DIM), lambda i: (i, 0))
    out = pl.BlockSpec((T, A), lambda i: (i, 0))
    return pl.pallas_call(
        body, name=name, grid=(S // T,),
        in_specs=[col(cb0), col(cb0 + 1), col(cb0 + 2), vec, vec, tab, tab],
        out_specs=[out] * 3, out_shape=[jax.ShapeDtypeStruct((S, A), BF16)] * 3,
        compiler_params=_params("parallel"),
    )(u, u, u, qn_g, kn_g, cos, sin)


def _qkv_prep_bwd(u, dqs, dks, dvs, qn_g, kn_g, cos, sin, cb0, name):
    S = u.shape[0]
    A = dqs[0].shape[1]
    H = A // HEAD_DIM
    T = _tile(S, 256, SUBLANE_BF16)
    nb = len(dqs)
    scale = HEAD_DIM ** -0.5

    def body(*refs):
        q_ref, k_ref, qg_ref, kg_ref, cos_ref, sin_ref = refs[:6]
        dq_refs, dk_refs, dv_refs = refs[6:6 + nb], refs[6 + nb:6 + 2 * nb], refs[6 + 2 * nb:6 + 3 * nb]
        dqo_ref, dko_ref, dvo_ref, dqg_ref, dkg_ref = refs[6 + 3 * nb:]

        @pl.when(pl.program_id(0) == 0)
        def _():
            dqg_ref[...] = jnp.zeros_like(dqg_ref)
            dkg_ref[...] = jnp.zeros_like(dkg_ref)

        cosv, sinv = cos_ref[...], sin_ref[...]
        for h in range(H):
            sl = slice(h * HEAD_DIM, (h + 1) * HEAD_DIM)
            for x_ref, g_ref, d_refs, o_ref, dg_ref, sc in ((q_ref, qg_ref, dq_refs, dqo_ref, dqg_ref, scale),
                                                          (k_ref, kg_ref, dk_refs, dko_ref, dkg_ref, 1.0)):
                dy = d_refs[0][:, sl]
                for r in d_refs[1:]:
                    dy = dy + r[:, sl]
                dy = dy * sc
                dxn = dy * cosv + pltpu.roll(dy * sinv, HEAD_DIM // 2, 1)
                xv = x_ref[:, sl]
                r = lax.rsqrt(jnp.mean(xv * xv, axis=-1, keepdims=True) + EPS)
                xh = xv * r
                dg_ref[...] += jnp.sum(dxn * xh, axis=0, keepdims=True)
                dxh = dxn * g_ref[...]
                o_ref[:, sl] = (r * (dxh - xh * jnp.mean(dxh * xh, axis=-1, keepdims=True))).astype(BF16)
        dv = dv_refs[0][...]
        for r in dv_refs[1:]:
            dv = dv + r[...]
        dvo_ref[...] = dv.astype(BF16)

    def col(cb):
        return pl.BlockSpec((T, A), lambda i: (i, cb))

    vec = pl.BlockSpec((1, HEAD_DIM), lambda i: (0, 0))
    tab = pl.BlockSpec((T, HEAD_DIM), lambda i: (i, 0))
    row = pl.BlockSpec((T, A), lambda i: (i, 0))
    return pl.pallas_call(
        body, name=name, grid=(S // T,),
        in_specs=[col(cb0), col(cb0 + 1), vec, vec, tab, tab] + [row] * (3 * nb),
        out_specs=[row, row, row, vec, vec],
        out_shape=[jax.ShapeDtypeStruct((S, A), BF16)] * 3 + [jax.ShapeDtypeStruct((1, HEAD_DIM), F32)] * 2,
        compiler_params=_params("arbitrary"),
    )(u, u, qn_g, kn_g, cos, sin, *dqs, *dks, *dvs)


ATT_TILE = 256
NEG = -1e30


def _attn_bias(tile):
    span = max(window for window, _ in DIL_PATTERNS)
    nw = -(-span // tile) + 1
    dist = (jnp.arange(nw)[:, None, None] * tile + jnp.arange(tile)[None, :, None] - jnp.arange(tile)[None, None, :])
    mult = sum(((dist >= 0) & (dist <= window) & (dist % dil == 0)).astype(F32) for window, dil in DIL_PATTERNS)
    return jnp.where(mult > 0, jnp.log(jnp.maximum(mult, 1.0)), NEG)


def _attn_fwd(q, k, v, bias, name):
    S, A = q.shape
    H = A // HEAD_DIM
    nw, T, _ = bias.shape
    nq = S // T

    def body(q_ref, k_ref, v_ref, b_ref, ob_ref, of_ref, l_ref, s_scr):
        i = pl.program_id(1)
        qv = q_ref[...]
        mx = jnp.full((T, 1), NEG, F32)
        for w in range(nw):
            blk = i - w
            start = pl.multiple_of(jnp.maximum(blk, 0) * T, T)
            s = _dot(qv, k_ref[pl.ds(start, T), :], "nt") + b_ref[w] + jnp.where(blk >= 0, 0.0, NEG)
            s_scr[w] = s
            mx = jnp.maximum(mx, jnp.max(s, axis=-1, keepdims=True))
        den = jnp.zeros((T, 1), F32)
        o = jnp.zeros((T, HEAD_DIM), F32)
        for w in range(nw):
            start = pl.multiple_of(jnp.maximum(i - w, 0) * T, T)
            p = jnp.exp(s_scr[w] - mx)
            den = den + jnp.sum(p, axis=-1, keepdims=True)
            o = o + _dot(p.astype(BF16), v_ref[pl.ds(start, T), :], "nn")
        o = o / den
        ob_ref[...] = o.astype(BF16)
        of_ref[...] = o
        l_ref[...] = mx + jnp.log(den)

    blk = pl.BlockSpec((T, HEAD_DIM), lambda h, i: (i, h))
    full = pl.BlockSpec((S, HEAD_DIM), lambda h, i: (0, h))
    return pl.pallas_call(
        body, name=name, grid=(H, nq),
        in_specs=[blk, full, full, pl.BlockSpec((nw, T, T), lambda h, i: (0, 0, 0))],
        out_specs=[blk, blk, pl.BlockSpec((None, T, 1), lambda h, i: (h, i, 0))],
        out_shape=[jax.ShapeDtypeStruct((S, A), BF16), jax.ShapeDtypeStruct((S, A), F32),
                   jax.ShapeDtypeStruct((H, S, 1), F32)],
        scratch_shapes=[pltpu.VMEM((nw, T, T), F32)],
        compiler_params=_params("parallel", "arbitrary"),
    )(q, k, v, bias)


def _attn_dq(q, k, v, dz, cb0, o, lse, bias, name):
    S, A = q.shape
    H = A // HEAD_DIM
    nw, T, _ = bias.shape
    nq = S // T

    def body(q_ref, k_ref, v_ref, do_ref, o_ref, l_ref, b_ref, dq_ref, d_ref):
        i = pl.program_id(1)
        qv, dof = q_ref[...], do_ref[...]
        dov = dof.astype(BF16)
        delta = jnp.sum(dof * o_ref[...], axis=-1, keepdims=True)
        d_ref[...] = delta
        lv = l_ref[...]
        dq = jnp.zeros((T, HEAD_DIM), F32)
        for w in range(nw):
            blk = i - w
            start = pl.multiple_of(jnp.maximum(blk, 0) * T, T)
            kv = k_ref[pl.ds(start, T), :]
            s = _dot(qv, kv, "nt") + b_ref[w] + jnp.where(blk >= 0, 0.0, NEG)
            p = jnp.exp(s - lv)
            ds = (p * (_dot(dov, v_ref[pl.ds(start, T), :], "nt") - delta)).astype(BF16)
            dq = dq + _dot(ds, kv, "nn")
        dq_ref[...] = dq

    blk = pl.BlockSpec((T, HEAD_DIM), lambda h, i: (i, h))
    full = pl.BlockSpec((S, HEAD_DIM), lambda h, i: (0, h))
    col = pl.BlockSpec((None, T, 1), lambda h, i: (h, i, 0))
    return pl.pallas_call(
        body, name=name, grid=(H, nq),
        in_specs=[blk, full, full, pl.BlockSpec((T, HEAD_DIM), lambda h, i: (i, cb0 + h)), blk, col,
                  pl.BlockSpec((nw, T, T), lambda h, i: (0, 0, 0))],
        out_specs=[blk, col],
        out_shape=[jax.ShapeDtypeStruct((S, A), F32), jax.ShapeDtypeStruct((H, S, 1), F32)],
        compiler_params=_params("parallel", "arbitrary"),
    )(q, k, v, dz, o, lse, bias)


def _attn_dkv(q, k, v, dz, cb0, lse, delta, bias, name):
    S, A = q.shape
    H = A // HEAD_DIM
    nw, T, _ = bias.shape
    nq = S // T

    def body(k_ref, v_ref, q_ref, do_ref, l_ref, d_ref, b_ref, dk_ref, dv_ref):
        m = pl.program_id(1)
        kv, vv = k_ref[...], v_ref[...]
        dk = jnp.zeros((T, HEAD_DIM), F32)
        dv = jnp.zeros((T, HEAD_DIM), F32)
        for w in range(nw):
            blk = m + w
            start = pl.multiple_of(jnp.minimum(blk, nq - 1) * T, T)
            qv = q_ref[pl.ds(start, T), :]
            dov = do_ref[pl.ds(start, T), :].astype(BF16)
            s = _dot(qv, kv, "nt") + b_ref[w] + jnp.where(blk < nq, 0.0, NEG)
            p = jnp.exp(s - l_ref[pl.ds(start, T), :])
            dv = dv + _dot(p.astype(BF16), dov, "tn")
            ds = (p * (_dot(dov, vv, "nt") - d_ref[pl.ds(start, T), :])).astype(BF16)
            dk = dk + _dot(ds, qv, "tn")
        dk_ref[...] = dk
        dv_ref[...] = dv

    blk = pl.BlockSpec((T, HEAD_DIM), lambda h, m: (m, h))
    full = pl.BlockSpec((S, HEAD_DIM), lambda h, m: (0, h))
    col = pl.BlockSpec((None, S, 1), lambda h, m: (h, 0, 0))
    sds = jax.ShapeDtypeStruct((S, A), F32)
    return pl.pallas_call(
        body, name=name, grid=(H, nq),
        in_specs=[blk, blk, full, pl.BlockSpec((S, HEAD_DIM), lambda h, m: (0, cb0 + h)), col, col,
                  pl.BlockSpec((nw, T, T), lambda h, m: (0, 0, 0))],
        out_specs=[blk, blk], out_shape=[sds, sds],
        compiler_params=_params("parallel", "arbitrary"),
    )(k, v, q, dz, lse, delta, bias)


def _even_mixer(u, conv_w, conv_b, cn_g, cn_b, qn_g, kn_g, tag):
    S = u.shape[0]
    C = conv_w.shape[1]
    A = (u.shape[1] - 2 * C) // 3
    assert A == C, "column-block addressing of u assumes equal conv and attention widths"
    T = _tile(S, ATT_TILE, LANE)
    cos, sin = _rope_tables(S)
    bias = _attn_bias(T)
    a_out, y = _conv_fwd(u, conv_w, conv_b, cn_g, cn_b, "conv_fwd" + tag)
    q, k, v = _qkv_prep(u, qn_g, kn_g, cos, sin, 2, "qkv_prep" + tag)
    ob, of, lse = _attn_fwd(q, k, v, bias, "attn_fwd" + tag)
    z = jnp.concatenate([a_out, ob], axis=1)

    def backward(dz):
        dy, d_cn_g, d_cn_b = _conv_bwd_norm(dz, y, cn_g, cn_b, "conv_bwd_norm" + tag)
        d_val, d_gate, d_w, d_b = _conv_bwd_taps(u, dy, conv_w, "conv_bwd_taps" + tag)
        dqp, delta = _attn_dq(q, k, v, dz, C // HEAD_DIM, of, lse, bias, "attn_dq" + tag)
        dkp, dvp = _attn_dkv(q, k, v, dz, C // HEAD_DIM, lse, delta, bias, "attn_dkv" + tag)
        dq, dk, dv, d_qn, d_kn = _qkv_prep_bwd(u, [dqp], [dkp], [dvp], qn_g, kn_g, cos, sin, 2, "qkv_prep_bwd" + tag)
        du = jnp.concatenate([d_val, d_gate, dq, dk, dv], axis=1)
        return du, [d_w[:CONV_WIDTH], d_b[0], d_cn_g[0], d_cn_b[0], d_qn[0], d_kn[0]]

    return z, backward


_LEVELS = (128, 64, 32, 16, 8, 4, 2, 1)


def _chunk_cumsum(g, rows, reverse=False):
    C = g.shape[0]
    d = 1
    while d < C:
        if reverse:
            g = g + jnp.where(rows < C - d, pltpu.roll(g, C - d, 0), 0.0)
        else:
            g = g + jnp.where(rows >= d, pltpu.roll(g, d, 0), 0.0)
        d *= 2
    return g


def _level_ref(b, b_scr, rows, m):
    C = b.shape[0]
    if m >= 8:
        pieces = [jnp.broadcast_to(b_scr[2 * m * j + m - 1:2 * m * j + m, :], (2 * m, LANE)) for j in range(C // (2 * m))]
        return pieces[0] if len(pieces) == 1 else jnp.concatenate(pieces, axis=0)
    pos = rows & (2 * m - 1)
    ref = b
    for p in range(2 * m):
        if p != m - 1:
            ref = jnp.where(pos == p, pltpu.roll(b, (p - (m - 1)) % C, 0), ref)
    return ref


def _level_operands(q, k, b, b_scr, rows, m):
    ref = _level_ref(b, b_scr, rows, m)
    qs = (q * jnp.exp(jnp.minimum(b - ref, 0.0))).astype(BF16)
    ks = (k * jnp.exp(jnp.minimum(ref - b, 0.0))).astype(BF16)
    return qs, ks


def _split2(x):
    hi = x.astype(BF16)
    lo = (x - hi.astype(F32)).astype(BF16)
    return jnp.concatenate([hi, lo], axis=1)


def _level_mask(tt, ss, m):
    x = tt ^ ss
    return (tt > ss) & (x >= m) & (x < 2 * m)


def _hgrn_gates(qz, fz, la, lc, oml):
    sq = jax.nn.sigmoid(qz)
    q = qz * sq
    s = jax.nn.sigmoid(fz)
    c = lc + jnp.minimum(fz, 0.0) - jnp.log(1.0 + jnp.exp(-jnp.abs(fz)))
    mx = jnp.maximum(la, c)
    g = mx + jnp.log(1.0 + jnp.exp(-jnp.abs(la - c)))
    k = oml * (1.0 - s)
    return q, sq, k, s, g, c


def _hgrn_fwd(u, la, lc, oml, gn_g, name):
    S = u.shape[0]
    W = u.shape[1] // 4
    H = W // HGRN_KDIM
    C = min(HGRN_CHUNK, S)
    nc = S // C
    levels = [m for m in _LEVELS if m < C]

    def body(qz_ref, fz_ref, iz_ref, gz_ref, la_ref, lc_ref, oml_ref, gn_ref,
             z_ref, o_ref, a_ref, st_ref, state, b_scr):
        @pl.when(pl.program_id(1) == 0)
        def _():
            state[...] = jnp.zeros_like(state)

        rows = lax.broadcasted_iota(jnp.int32, (C, LANE), 0)
        tt = lax.broadcasted_iota(jnp.int32, (C, C), 0)
        ss = lax.broadcasted_iota(jnp.int32, (C, C), 1)
        q, _, k, _, g, _ = _hgrn_gates(qz_ref[...], fz_ref[...], la_ref[...], lc_ref[...], oml_ref[...])
        v = iz_ref[...].astype(BF16)
        b = _chunk_cumsum(g, rows)
        b_scr[...] = b
        a = jnp.where(tt == ss, jnp.sum(q * k, axis=-1, keepdims=True), 0.0)
        for m in levels:
            qs, ks = _level_operands(q, k, b, b_scr, rows, m)
            a = jnp.where(_level_mask(tt, ss, m), _dot(qs, ks, "nt"), a)
        ab = a.astype(BF16)
        a_ref[...] = ab
        st = state[...]
        st_ref[...] = st
        o = _dot(ab, v, "nn") + _dot((q * jnp.exp(b)).astype(BF16), st.astype(BF16), "nt")
        bl = b_scr[C - 1:C, :]
        kh = (k * jnp.exp(bl - b)).astype(BF16)
        state[...] = st * jnp.exp(bl) + _dot(v, kh, "tn")
        o_ref[...] = o
        r = lax.rsqrt(jnp.mean(o * o, axis=-1, keepdims=True) + EPS)
        gz = gz_ref[...]
        z_ref[...] = (o * r * gn_ref[...] * (gz * jax.nn.sigmoid(gz))).astype(BF16)

    def col(off):
        return pl.BlockSpec((C, LANE), lambda h, i: (i, off * H + h))

    vec = pl.BlockSpec((1, LANE), lambda h, i: (0, h))
    tile = pl.BlockSpec((C, LANE), lambda h, i: (i, h))
    return pl.pallas_call(
        body, name=name, grid=(H, nc),
        in_specs=[col(0), col(1), col(2), col(3), vec, vec, vec, vec],
        out_specs=[tile, tile, pl.BlockSpec((None, C, C), lambda h, i: (h, i, 0)),
                   pl.BlockSpec((None, None, LANE, LANE), lambda h, i: (h, i, 0, 0))],
        out_shape=[jax.ShapeDtypeStruct((S, W), BF16), jax.ShapeDtypeStruct((S, W), F32),
                   jax.ShapeDtypeStruct((H, S, C), BF16), jax.ShapeDtypeStruct((H, nc, LANE, LANE), F32)],
        scratch_shapes=[pltpu.VMEM((LANE, LANE), F32), pltpu.VMEM((C, LANE), F32)],
        compiler_params=_params("parallel", "arbitrary"),
    )(u, u, u, u, la, lc, oml, gn_g)


def _hgrn_bwd(u, la, lc, oml, gn_g, o, a, st, dz, name):
    S = u.shape[0]
    W = u.shape[1] // 4
    H = W // HGRN_KDIM
    C = min(HGRN_CHUNK, S)
    nc = S // C
    levels = [m for m in _LEVELS if m < C]

    def body(qz_ref, fz_ref, iz_ref, gz_ref, la_ref, lc_ref, oml_ref, gn_ref, o_ref, a_ref, st_ref, dz_ref,
             dqz_ref, dfz_ref, diz_ref, dgz_ref, dla_ref, dlc_ref, doml_ref, dgn_ref, dstate, b_scr):
        @pl.when(pl.program_id(1) == 0)
        def _():
            dstate[...] = jnp.zeros_like(dstate)
            dla_ref[...] = jnp.zeros_like(dla_ref)
            dlc_ref[...] = jnp.zeros_like(dlc_ref)
            doml_ref[...] = jnp.zeros_like(doml_ref)
            dgn_ref[...] = jnp.zeros_like(dgn_ref)

        rows = lax.broadcasted_iota(jnp.int32, (C, LANE), 0)
        tt = lax.broadcasted_iota(jnp.int32, (C, C), 0)
        ss = lax.broadcasted_iota(jnp.int32, (C, C), 1)
        la_v, lc_v, oml_v = la_ref[...], lc_ref[...], oml_ref[...]
        qz, fz = qz_ref[...], fz_ref[...]
        q, sq, k, s, g, c = _hgrn_gates(qz, fz, la_v, lc_v, oml_v)
        vf = iz_ref[...]
        v = vf.astype(BF16)

        ov, gz, dzv, gn = o_ref[...], gz_ref[...], dz_ref[...], gn_ref[...]
        r = lax.rsqrt(jnp.mean(ov * ov, axis=-1, keepdims=True) + EPS)
        on = ov * r
        sg = jax.nn.sigmoid(gz)
        silu_g = gz * sg
        dgn_ref[...] += jnp.sum(dzv * on * silu_g, axis=0, keepdims=True)
        dgz_ref[...] = (dzv * on * gn * (sg * (1.0 + gz * (1.0 - sg)))).astype(BF16)
        don = dzv * gn * silu_g
        do_f = r * (don - on * jnp.mean(don * on, axis=-1, keepdims=True))
        do = do_f.astype(BF16)

        b = _chunk_cumsum(g, rows)
        b_scr[...] = b
        bl = b_scr[C - 1:C, :]
        e = jnp.exp(b)
        ebl = jnp.exp(bl)
        ekl = jnp.exp(bl - b)
        qh = q * e
        kh = k * ekl
        st_v = st_ref[...]
        dst = dstate[...]
        dstb = dst.astype(BF16)

        diz_ref[...] = (_dot(a_ref[...], do, "tn") + _dot(kh.astype(BF16), dstb, "nt")).astype(BF16)
        da = _dot(do, v, "nt")
        dqh = _dot(do, st_v.astype(BF16), "nn")
        dkh = _dot(v, dstb, "nn")
        dstate[...] = dst * ebl + _dot(do, qh.astype(BF16), "tn")
        dbl = jnp.sum(dkh * kh, axis=0, keepdims=True) + jnp.sum(dst * st_v, axis=0, keepdims=True) * ebl

        datt = jnp.sum(do_f * vf, axis=-1, keepdims=True)
        dqa = datt * k
        dka = datt * q
        for m in levels:
            ref = _level_ref(b, b_scr, rows, m)
            eu = jnp.exp(jnp.minimum(b - ref, 0.0))
            el = jnp.exp(jnp.minimum(ref - b, 0.0))
            gm = jnp.where(_level_mask(tt, ss, m), da, 0.0).astype(BF16)
            pq = _dot(gm, _split2(k * el), "nn")
            pk = _dot(gm, _split2(q * eu), "tn")
            dqa += (pq[:, :LANE] + pq[:, LANE:]) * eu
            dka += (pk[:, :LANE] + pk[:, LANE:]) * el
        db = q * dqa - k * dka + dqh * qh - dkh * kh
        db = db + jnp.where(rows == C - 1, dbl, 0.0)
        dq = dqa + dqh * e
        dk = dka + dkh * ekl
        dg = _chunk_cumsum(db, rows, reverse=True)

        wa = jnp.exp(la_v - g)
        wc = jnp.exp(c - g)
        dqz_ref[...] = (dq * (sq * (1.0 + qz * (1.0 - sq)))).astype(BF16)
        dfz_ref[...] = (dg * wc * (1.0 - s) - dk * oml_v * s * (1.0 - s)).astype(BF16)
        dla_ref[...] += jnp.sum(dg * wa, axis=0, keepdims=True)
        dlc_ref[...] += jnp.sum(dg * wc, axis=0, keepdims=True)
        doml_ref[...] += jnp.sum(dk * (1.0 - s), axis=0, keepdims=True)

    def col(off):
        return pl.BlockSpec((C, LANE), lambda h, i: (nc - 1 - i, off * H + h))

    vec = pl.BlockSpec((1, LANE), lambda h, i: (0, h))
    tile = pl.BlockSpec((C, LANE), lambda h, i: (nc - 1 - i, h))
    a_spec = pl.BlockSpec((None, C, C), lambda h, i: (h, nc - 1 - i, 0))
    st_spec = pl.BlockSpec((None, None, LANE, LANE), lambda h, i: (h, nc - 1 - i, 0, 0))
    sw = jax.ShapeDtypeStruct((S, W), BF16)
    vw = jax.ShapeDtypeStruct((1, W), F32)
    return pl.pallas_call(
        body, name=name, grid=(H, nc),
        in_specs=[col(0), col(1), col(2), col(3), vec, vec, vec, vec, tile, a_spec, st_spec, tile],
        out_specs=[tile, tile, tile, tile, vec, vec, vec, vec],
        out_shape=[sw, sw, sw, sw, vw, vw, vw, vw],
        scratch_shapes=[pltpu.VMEM((LANE, LANE), F32), pltpu.VMEM((C, LANE), F32)],
        compiler_params=_params("parallel", "arbitrary"),
    )(u, u, u, u, la, lc, oml, gn_g, o, a, st, dz)


def _lb_terms(lb_logits, layer):
    p = jax.nn.softmax(lb_logits, axis=0)
    lb = (jnp.cumsum(p, axis=0) - p[0:1])[layer]
    return jnp.log(lb)[None], jnp.log1p(-lb)[None], (1.0 - lb)[None]


def kernel(x, norm_ffn1, ffn1_wg, ffn1_wu, ffn1_wd, norm_mix, norm_ffn2, ffn2_wg, ffn2_wu, ffn2_wd, ev_w_in, ev_conv_w, ev_conv_b, ev_cn_g, ev_cn_b, ev_qn_g, ev_kn_g, ev_w_out, od_w_in, od_lb_logits, od_gn_g, od_w_out, loss_target, m_norm_ffn1, m_ffn1_wg, m_ffn1_wu, m_ffn1_wd, m_norm_mix, m_norm_ffn2, m_ffn2_wg, m_ffn2_wu, m_ffn2_wd, m_ev_w_in, m_ev_conv_w, m_ev_conv_b, m_ev_cn_g, m_ev_cn_b, m_ev_qn_g, m_ev_kn_g, m_ev_w_out, m_od_w_in, m_od_lb_logits, m_od_gn_g, m_od_w_out, v_norm_ffn1, v_ffn1_wg, v_ffn1_wu, v_ffn1_wd, v_norm_mix, v_norm_ffn2, v_ffn2_wg, v_ffn2_wu, v_ffn2_wd, v_ev_w_in, v_ev_conv_w, v_ev_conv_b, v_ev_cn_g, v_ev_cn_b, v_ev_qn_g, v_ev_kn_g, v_ev_w_out, v_od_w_in, v_od_lb_logits, v_od_gn_g, v_od_w_out):
    depth = norm_ffn1.shape[0]
    S, D = x.shape[1], x.shape[2]
    xi, yi, ci = _me()
    dev = 4 * xi + 2 * yi + ci
    c_idx = jnp.reshape(ci, (1,)).astype(jnp.int32)
    k_idx = jnp.reshape(2 * xi + yi, (1,)).astype(jnp.int32)

    def ffn_shard(wg, wu, wd, l):
        return jnp.stack([wg[l].T, wu[l].T, wd[l]]).astype(BF16)

    assert depth == 2, "the exchange schedule below is written for one even and one odd layer"
    sh_ffn1 = [ffn_shard(ffn1_wg, ffn1_wu, ffn1_wd, l) for l in range(depth)]
    sh_ffn2 = [ffn_shard(ffn2_wg, ffn2_wu, ffn2_wd, l) for l in range(depth)]
    sh_ev = [ev_w_in[0].T.astype(BF16)[None], ev_w_out[0].astype(BF16)[None]]
    sh_od = [od_w_in[0].T.astype(BF16)[None], od_w_out[0].astype(BF16)[None]]

    def full(g):
        return g.reshape(g.shape[0], N_DEV * g.shape[2], g.shape[3])

    def gather_begin(shards, after, tag):
        lands = [lax.empty((s.shape[0], N_DEV) + s.shape[1:], s.dtype) for s in shards]
        state = _push_start(shards, lands, _gather_plan, after, "gather_start" + tag)
        return state, state[4][0, 0]

    def gather_end(state, after, tag):
        send, recv, srcs, lands, _ = state
        srcs, lands = _push_wait(send, recv, srcs, lands, _gather_plan, after, "gather_wait" + tag)
        return [full(g) for g in _gather_forward(lands, srcs, "gather_forward" + tag)]

    w_ffn1, w_ffn2 = [None] * depth, [None] * depth
    group_0, _ = gather_begin([sh_ffn1[0]], None, "_0")

    conv_w_sh, gn_g_sh = ev_conv_w[0], od_gn_g[0]
    cw, cs = conv_w_sh.shape[0], conv_w_sh.shape[1]
    gs = gn_g_sh.shape[0]
    conv_w_z = lax.dynamic_update_slice(jnp.zeros((cw, N_DEV * cs), F32), conv_w_sh, (0, dev * cs))
    gn_g_z = lax.dynamic_update_slice(jnp.zeros((N_DEV * gs,), F32), gn_g_sh, (dev * gs,))
    conv_w_full, gn_g_full = _unpack_rows(
        _all_reduce_small(_pack_rows([conv_w_z, gn_g_z]), "gather_small_params"),
        [conv_w_z.shape, gn_g_z.shape])
    (w_ffn1[0],) = gather_end(group_0, sh_ev + sh_od + sh_ffn2 + [sh_ffn1[1], conv_w_full], "_0")

    def ffn_forward(h, gain, w, tag):
        hn = _rms_fwd(h, gain[None], "rms_" + tag)
        out, gu = _ffn_fwd(h, hn, w, "ffn_fwd_" + tag)
        return out, (h, hn, gu)

    def odd_mixer(u, l):
        (la, lc, oml), lb_vjp = jax.vjp(functools.partial(_lb_terms, layer=l), od_lb_logits)
        gn = gn_g_full[None]
        zb, o_raw, scores, states = _hgrn_fwd(u, la, lc, oml, gn, f"hgrn_fwd{l}")

        def backward(dz):
            dqz, dfz, diz, dgz, dla, dlc, doml, dgn = _hgrn_bwd(
                u, la, lc, oml, gn, o_raw, scores, states, dz, f"hgrn_bwd{l}")
            (g_lb,) = lb_vjp((dla, dlc, doml))
            return jnp.concatenate([dqz, dfz, diz, dgz], axis=1), [g_lb, dgn[0]]

        return zb, backward

    saved = []
    h = x[0]
    for l in range(depth):
        if l == 0:
            group_a, tok = gather_begin(sh_ev + [sh_ffn2[0]], w_ffn1[0], "_a")
        else:
            group_c, tok = gather_begin([sh_ffn2[1]], w_ffn1[1], "_c")
        h, s1 = ffn_forward(h, norm_ffn1[l] + tok, w_ffn1[l], f"a{l}")
        tok = 0.0
        if l == 0:
            ev_in, ev_out, w_ffn2[0] = gather_end(group_a, h, "_a")
            ev_w_in_t, ev_w_out_f = ev_in[0], ev_out[0]
            group_b, tok = gather_begin([sh_ffn1[1]] + sh_od, w_ffn2[0], "_b")
        hn = _rms_fwd(h, (norm_mix[l] + tok)[None], f"rms_mix{l}")
        if l % 2 == 0:
            u = _mm(hn, ev_w_in_t, "nt", F32, f"mix_in{l}")
            zb, core_vjp = _even_mixer(u, conv_w_full, ev_conv_b, ev_cn_g, ev_cn_b, ev_qn_g, ev_kn_g, str(l))
            w_out = ev_w_out_f
        else:
            u = _mm(hn, od_w_in_t, "nt", F32, f"mix_in{l}")
            zb, core_vjp = odd_mixer(u, l)
            w_out = od_w_out_f
        h_mix = h
        h = _mm(zb, w_out, "nn", F32, f"mix_out{l}", res=h)
        sm = (h_mix, hn, zb, core_vjp)
        if l == 1:
            (w_ffn2[1],) = gather_end(group_c, h, "_c")
        h, s2 = ffn_forward(h, norm_ffn2[l], w_ffn2[l], f"b{l}")
        if l == 0:
            w_ffn1[1], od_in, od_out = gather_end(group_b, h, "_b")
            od_w_in_t, od_w_out_f = od_in[0], od_out[0]
        saved.append((s1, sm, s2))

    dy, loss_part = _loss_grad(h, loss_target[0], "loss_grad")

    def halves_begin(parts, tag):
        parts = [g.reshape(g.shape[0], 4, 2, g.shape[1] // N_DEV, g.shape[2]) for g in parts]
        lands = [lax.empty(g.shape[:2] + g.shape[3:], BF16) for g in parts]
        state = _push_start(parts, lands, _halves_plan, None, "halves_start" + tag)
        return state, state[4][0, 0]

    def chips_begin(state, after, tag):
        send, recv, srcs, lands, _ = state
        parts, got = _push_wait(send, recv, srcs, lands, _halves_plan, after, "halves_wait" + tag)
        sums = [_add_core_halves(g, r, c_idx, f"add_core_halves{tag}_{a}") for a, (g, r) in enumerate(zip(parts, got))]
        lands = [lax.empty((3, s.shape[0]) + s.shape[2:], BF16) for s in sums]
        state = _push_start(sums, lands, _chip_plan, None, "reduce_start" + tag)
        return state, state[4][0, 0]

    def reduce_end(state, after, tag):
        send, recv, srcs, lands, _ = state
        sums, got = _push_wait(send, recv, srcs, lands, _chip_plan, after, "reduce_wait" + tag)
        return [_sum_chip_blocks(s, r, k_idx, f"sum_chip_blocks{tag}_{a}") for a, (s, r) in enumerate(zip(sums, got))]

    def ffn_backward(dy, gain, w, sv, tag, on_dw):
        h_in, hn, gu = sv
        dxn, dout, t = _ffn_bwd_dx(dy, w, gu, "ffn_bwd_dx_" + tag)
        tok = on_dw(_ffn_bwd_dw(hn, dout, t, "ffn_bwd_dw_" + tag))
        dx, dgain = _rms_bwd(h_in, (gain + tok)[None], dxn, dy, "rms_bwd_" + tag)
        return dx, dgain[0]

    g_norm1, g_norm2, g_normm = [None] * depth, [None] * depth, [None] * depth
    small, held, halves, groups = [None, None], {}, {}, {}

    def on_dw_ffn2_l1(dw):
        halves["1"], tok = halves_begin([dw], "_1")
        return tok

    def on_dw_ffn1_l1(dw):
        halves["2"], tok = halves_begin(held["od"] + [dw], "_2")
        return tok

    def on_dw_ffn2_l0(dw):
        held["ffn2_0"] = dw
        groups["2"], tok = chips_begin(halves.pop("2"), dw, "_2")
        return tok

    def on_dw_ffn1_l0(dw):
        groups["3"], tok3 = chips_begin(halves.pop("3"), dw, "_3")
        halves["4"], tok4 = halves_begin([dw], "_4")
        return tok3 + tok4

    for l in reversed(range(depth)):
        s1, (h_mix, hn, zb, core_vjp), s2 = saved[l]
        dy, g_norm2[l] = ffn_backward(dy, norm_ffn2[l], w_ffn2[l], s2, f"b{l}",
                                      on_dw_ffn2_l1 if l == 1 else on_dw_ffn2_l0)
        dyb = dy.astype(BF16)
        if l % 2 == 0:
            w_out, w_in_t = ev_w_out_f, ev_w_in_t
        else:
            w_out, w_in_t = od_w_out_f, od_w_in_t
        dz = _mm(dyb, w_out, "nt", F32, f"mix_out_dz{l}")
        dw_out = _mm(zb, dyb, "tn", BF16, f"mix_out_dw{l}")
        dub, small[l % 2] = core_vjp(dz)
        dw_in_t = _mm(dub, hn, "tn", BF16, f"mix_in_dw{l}")
        if l == 0:
            halves["3"], tok = halves_begin([held["ffn2_0"], dw_in_t[None], dw_out[None]], "_3")
        else:
            held["od"] = [dw_in_t[None], dw_out[None]]
            groups["1"], tok = chips_begin(halves.pop("1"), dw_in_t, "_1")
        dhn = _mm(dub, w_in_t, "nn", F32, f"mix_in_dx{l}")
        dy, gm = _rms_bwd(h_mix, (norm_mix[l] + tok)[None], dhn, dy, f"rms_bwd_mix{l}")
        g_normm[l] = gm[0]
        dy, g_norm1[l] = ffn_backward(dy, norm_ffn1[l], w_ffn1[l], s1, f"a{l}",
                                      on_dw_ffn1_l1 if l == 1 else on_dw_ffn1_l0)
    grad_x = dy[None]
    groups["4"], _ = chips_begin(halves.pop("4"), dy, "_4")

    done = [dy, groups["4"][4]]
    (g_ffn2_1,) = reduce_end(groups["1"], done, "_1")
    g_od_in_t, g_od_out, g_ffn1_1 = reduce_end(groups["2"], done, "_2")
    g_ffn2_0, g_ev_in_t, g_ev_out = reduce_end(groups["3"], done, "_3")
    g_ffn2 = [g_ffn2_0, g_ffn2_1]

    def ffn_grads(gl):
        return (jnp.stack([g[0].T for g in gl]), jnp.stack([g[1].T for g in gl]), jnp.stack([g[2] for g in gl]))

    g_ffn2_wg, g_ffn2_wu, g_ffn2_wd = ffn_grads(g_ffn2)
    grads = [None, None, None, None, None, None, g_ffn2_wg, g_ffn2_wu, g_ffn2_wd,
             g_ev_in_t[0].T[None], None, None, None, None, None,
             None, g_ev_out, g_od_in_t[0].T[None], None, None, g_od_out]
    weights = [norm_ffn1, ffn1_wg, ffn1_wu, ffn1_wd, norm_mix, norm_ffn2, ffn2_wg, ffn2_wu, ffn2_wd, ev_w_in,
               ev_conv_w, ev_conv_b, ev_cn_g, ev_cn_b, ev_qn_g, ev_kn_g, ev_w_out, od_w_in, od_lb_logits,
               od_gn_g, od_w_out]
    moms = [m_norm_ffn1, m_ffn1_wg, m_ffn1_wu, m_ffn1_wd, m_norm_mix, m_norm_ffn2, m_ffn2_wg, m_ffn2_wu,
            m_ffn2_wd, m_ev_w_in, m_ev_conv_w, m_ev_conv_b, m_ev_cn_g, m_ev_cn_b, m_ev_qn_g, m_ev_kn_g,
            m_ev_w_out, m_od_w_in, m_od_lb_logits, m_od_gn_g, m_od_w_out]
    vars_ = [v_norm_ffn1, v_ffn1_wg, v_ffn1_wu, v_ffn1_wd, v_norm_mix, v_norm_ffn2, v_ffn2_wg, v_ffn2_wu,
             v_ffn2_wd, v_ev_w_in, v_ev_conv_w, v_ev_conv_b, v_ev_cn_g, v_ev_cn_b, v_ev_qn_g, v_ev_kn_g,
             v_ev_w_out, v_od_w_in, v_od_lb_logits, v_od_gn_g, v_od_w_out]
    n_w = len(weights)
    deltas, new_m, new_v = [None] * n_w, [None] * n_w, [None] * n_w

    def update(idx):
        for i in idx:
            deltas[i], new_m[i], new_v[i] = _adamw(weights[i], grads[i], moms[i], vars_[i], f"adamw{i}")

    update([i for i in range(n_w) if grads[i] is not None])
    g_conv_w, g_conv_b, g_cn_g, g_cn_b, g_qn_g, g_kn_g = small[0]
    g_lb, g_gn = small[1]
    parts = [jnp.stack(g_norm1), jnp.stack(g_normm), jnp.stack(g_norm2), g_conv_b, g_cn_g, g_cn_b,
             g_qn_g, g_kn_g, g_lb, g_conv_w, g_gn, loss_part[0, :1]]
    red = _unpack_rows(_all_reduce_small(_pack_rows(parts), "reduce_small_grads", [d for d in deltas if d is not None]),
                       [p.shape for p in parts])
    g_norm1, g_normm, g_norm2, g_conv_b, g_cn_g, g_cn_b, g_qn_g, g_kn_g, g_lb, g_conv_w, g_gn, loss = red
    g_conv_w = lax.dynamic_slice(g_conv_w, (0, dev * cs), (cw, cs))
    g_gn = lax.dynamic_slice(g_gn, (dev * gs,), (gs,))
    small_idx = {0: g_norm1, 4: g_normm, 5: g_norm2, 10: g_conv_w[None], 11: g_conv_b[None], 12: g_cn_g[None],
                 13: g_cn_b[None], 14: g_qn_g[None], 15: g_kn_g[None], 18: g_lb, 19: g_gn[None]}
    for i, g in small_idx.items():
        grads[i] = g
    update(small_idx)
    (g_ffn1_0,) = reduce_end(groups["4"], [d for d in deltas if d is not None], "_4")
    grads[1], grads[2], grads[3] = ffn_grads([g_ffn1_0, g_ffn1_1])
    update((1, 2, 3))
    return (loss[0], grad_x, *grads, *deltas, *new_m, *new_v)
```

```python
import functools
import math

import jax
import jax.numpy as jnp
from jax import lax
from jax.experimental import pallas as pl
from jax.experimental.pallas import tpu as pltpu

F32 = jnp.float32
BF16 = jnp.bfloat16
MESH = pl.DeviceIdType.MESH
N_DEV = 8

EPS = 1e-6
HEAD_DIM = 128
CONV_WIDTH = 31
DIL_PATTERNS = ((128, 1), (512, 4), (2048, 16))
Q_BLOCK = 128
ROPE_THETA = 10000.0
HGRN_KDIM = 128
HGRN_CHUNK = 256

ADAM_LR = 0.001
ADAM_B1 = 0.9
ADAM_B2 = 0.999
ADAM_EPS = 1e-08
ADAM_WD = 0.01
ADAM_STEP = 10

VMEM_LIMIT_BYTES = 56 * 1024 * 1024
LANE = 128
SUBLANE_BF16 = 16

ANY = pl.BlockSpec(memory_space=pl.ANY)


def _tile(n, pref, mult):
    t = (min(pref, n) // mult) * mult
    while t > 0:
        if n % t == 0:
            return t
        t -= mult
    return n


def _params(*sem):
    return pltpu.CompilerParams(dimension_semantics=sem, vmem_limit_bytes=VMEM_LIMIT_BYTES)


_DOT_DIMS = {
    "nn": (((1,), (0,)), ((), ())),
    "nt": (((1,), (1,)), ((), ())),
    "tn": (((0,), (0,)), ((), ())),
}


def _dot(a, b, mode):
    return lax.dot_general(a, b, _DOT_DIMS[mode], preferred_element_type=F32)


def _mm(a, b, mode, out_dtype, name, res=None, tm=1024, tn=1024, tk=2048):
    if mode == "nt":
        (M, K), N = a.shape, b.shape[0]
    elif mode == "nn":
        (M, K), N = a.shape, b.shape[1]
    else:
        (K, M), N = a.shape, b.shape[1]
    tm, tn, tk = _tile(M, tm, LANE), _tile(N, tn, LANE), _tile(K, tk, LANE)
    nk = K // tk

    def body(*refs):
        if res is None:
            a_ref, b_ref, o_ref, acc = refs
        else:
            a_ref, b_ref, r_ref, o_ref, acc = refs
        k = pl.program_id(2)

        @pl.when(k == 0)
        def _():
            acc[...] = jnp.zeros_like(acc)

        acc[...] += _dot(a_ref[...].astype(BF16), b_ref[...].astype(BF16), mode)

        @pl.when(k == nk - 1)
        def _():
            r = acc[...]
            if res is not None:
                r = r_ref[...] + r
            o_ref[...] = r.astype(out_dtype)

    a_spec = {"nt": pl.BlockSpec((tm, tk), lambda i, j, k: (i, k)),
              "nn": pl.BlockSpec((tm, tk), lambda i, j, k: (i, k)),
              "tn": pl.BlockSpec((tk, tm), lambda i, j, k: (k, i))}[mode]
    b_spec = {"nt": pl.BlockSpec((tn, tk), lambda i, j, k: (j, k)),
              "nn": pl.BlockSpec((tk, tn), lambda i, j, k: (k, j)),
              "tn": pl.BlockSpec((tk, tn), lambda i, j, k: (k, j))}[mode]
    o_spec = pl.BlockSpec((tm, tn), lambda i, j, k: (i, j))
    in_specs = [a_spec, b_spec] + ([o_spec] if res is not None else [])
    args = (a, b) + ((res,) if res is not None else ())
    return pl.pallas_call(
        body, name=name, grid=(M // tm, N // tn, nk),
        in_specs=in_specs, out_specs=o_spec,
        out_shape=jax.ShapeDtypeStruct((M, N), out_dtype),
        scratch_shapes=[pltpu.VMEM((tm, tn), F32)],
        compiler_params=_params("parallel", "parallel", "arbitrary"),
    )(*args)


def _rms_fwd(x, gain, name):
    S, D = x.shape
    tm = _tile(S, 512, SUBLANE_BF16)

    def body(x_ref, g_ref, o_ref):
        xv = x_ref[...]
        r = lax.rsqrt(jnp.mean(xv * xv, axis=-1, keepdims=True) + EPS)
        o_ref[...] = (xv * r * g_ref[...]).astype(BF16)

    return pl.pallas_call(
        body, name=name, grid=(S // tm,),
        in_specs=[pl.BlockSpec((tm, D), lambda i: (i, 0)), pl.BlockSpec((1, D), lambda i: (0, 0))],
        out_specs=pl.BlockSpec((tm, D), lambda i: (i, 0)),
        out_shape=jax.ShapeDtypeStruct((S, D), BF16),
        compiler_params=_params("parallel"),
    )(x, gain)


def _rms_bwd(x, gain, dxn, dy, name):
    S, D = x.shape
    tm = _tile(S, 512, 8)

    def body(x_ref, g_ref, dxn_ref, dy_ref, dx_ref, dg_ref):
        @pl.when(pl.program_id(0) == 0)
        def _():
            dg_ref[...] = jnp.zeros_like(dg_ref)

        xv = x_ref[...]
        r = lax.rsqrt(jnp.mean(xv * xv, axis=-1, keepdims=True) + EPS)
        xh = xv * r
        dxn_v = dxn_ref[...]
        dg_ref[...] += jnp.sum(dxn_v * xh, axis=0, keepdims=True)
        dxh = dxn_v * g_ref[...]
        dx_ref[...] = dy_ref[...] + r * (dxh - xh * jnp.mean(dxh * xh, axis=-1, keepdims=True))

    row = pl.BlockSpec((tm, D), lambda i: (i, 0))
    vec = pl.BlockSpec((1, D), lambda i: (0, 0))
    return pl.pallas_call(
        body, name=name, grid=(S // tm,),
        in_specs=[row, vec, row, row], out_specs=[row, vec],
        out_shape=[jax.ShapeDtypeStruct((S, D), F32), jax.ShapeDtypeStruct((1, D), F32)],
        compiler_params=_params("arbitrary"),
    )(x, gain, dxn, dy)


def _ffn_fwd(x, xn, w, name):
    S, D = x.shape
    F = w.shape[1]
    tm, tf = _tile(S, 512, SUBLANE_BF16), _tile(F, 512, LANE)
    nf = F // tf

    def body(x_ref, xn_ref, w_ref, o_ref, gu_ref, acc):
        f = pl.program_id(1)

        @pl.when(f == 0)
        def _():
            acc[...] = jnp.zeros_like(acc)

        xnv = xn_ref[...]
        g = _dot(xnv, w_ref[0], "nt")
        u = _dot(xnv, w_ref[1], "nt")
        gu_ref[0] = g.astype(BF16)
        gu_ref[1] = u.astype(BF16)
        h = (g * jax.nn.sigmoid(g) * u).astype(BF16)
        acc[...] += _dot(h, w_ref[2], "nn")

        @pl.when(f == nf - 1)
        def _():
            o_ref[...] = x_ref[...] + 0.5 * acc[...]

    row = pl.BlockSpec((tm, D), lambda i, f: (i, 0))
    return pl.pallas_call(
        body, name=name, grid=(S // tm, nf),
        in_specs=[row, row, pl.BlockSpec((3, tf, D), lambda i, f: (0, f, 0))],
        out_specs=[row, pl.BlockSpec((2, tm, tf), lambda i, f: (0, i, f))],
        out_shape=[jax.ShapeDtypeStruct((S, D), F32), jax.ShapeDtypeStruct((2, S, F), BF16)],
        scratch_shapes=[pltpu.VMEM((tm, D), F32)],
        compiler_params=_params("parallel", "arbitrary"),
    )(x, xn, w)


def _ffn_bwd_dx(dy, w, gu, name):
    S, D = dy.shape
    F = w.shape[1]
    tm, tf = _tile(S, 512, SUBLANE_BF16), _tile(F, 512, LANE)
    nf = F // tf

    def body(dy_ref, w_ref, gu_ref, dxn_ref, dout_ref, t_ref, acc):
        f = pl.program_id(1)

        @pl.when(f == 0)
        def _():
            acc[...] = jnp.zeros_like(acc)
            dout_ref[...] = (0.5 * dy_ref[...]).astype(BF16)

        dh = _dot(dout_ref[...], w_ref[2], "nt")
        g = gu_ref[0].astype(F32)
        u = gu_ref[1].astype(F32)
        sig = jax.nn.sigmoid(g)
        silu = g * sig
        dg = (dh * u * (sig * (1.0 + g * (1.0 - sig)))).astype(BF16)
        du = (dh * silu).astype(BF16)
        t_ref[0] = dg
        t_ref[1] = du
        t_ref[2] = (silu * u).astype(BF16)
        acc[...] += _dot(dg, w_ref[0], "nn") + _dot(du, w_ref[1], "nn")

        @pl.when(f == nf - 1)
        def _():
            dxn_ref[...] = acc[...]

    row = pl.BlockSpec((tm, D), lambda i, f: (i, 0))
    return pl.pallas_call(
        body, name=name, grid=(S // tm, nf),
        in_specs=[row, pl.BlockSpec((3, tf, D), lambda i, f: (0, f, 0)),
                  pl.BlockSpec((2, tm, tf), lambda i, f: (0, i, f))],
        out_specs=[row, row, pl.BlockSpec((3, tm, tf), lambda i, f: (0, i, f))],
        out_shape=[jax.ShapeDtypeStruct((S, D), F32), jax.ShapeDtypeStruct((S, D), BF16),
                   jax.ShapeDtypeStruct((3, S, F), BF16)],
        scratch_shapes=[pltpu.VMEM((tm, D), F32)],
        compiler_params=_params("parallel", "arbitrary"),
    )(dy, w, gu)


def _ffn_bwd_dw(xn, dout, t, name):
    S, D = xn.shape
    F = t.shape[2]
    ts, tf = _tile(S, 1024, LANE), _tile(F, 512, LANE)
    ns = S // ts

    def body(xn_ref, dout_ref, t_ref, dw_ref, acc):
        s = pl.program_id(1)

        @pl.when(s == 0)
        def _():
            acc[...] = jnp.zeros_like(acc)

        xnv = xn_ref[...]
        acc[0] += _dot(t_ref[0], xnv, "tn")
        acc[1] += _dot(t_ref[1], xnv, "tn")
        acc[2] += _dot(t_ref[2], dout_ref[...], "tn")

        @pl.when(s == ns - 1)
        def _():
            dw_ref[...] = acc[...].astype(BF16)

    row = pl.BlockSpec((ts, D), lambda f, s: (s, 0))
    return pl.pallas_call(
        body, name=name, grid=(F // tf, ns),
        in_specs=[row, row, pl.BlockSpec((3, ts, tf), lambda f, s: (0, s, f))],
        out_specs=pl.BlockSpec((3, tf, D), lambda f, s: (0, f, 0)),
        out_shape=jax.ShapeDtypeStruct((3, F, D), BF16),
        scratch_shapes=[pltpu.VMEM((3, tf, D), F32)],
        compiler_params=_params("parallel", "arbitrary"),
    )(xn, dout, t)


def _loss_grad(y, target, name):
    S, D = y.shape
    tm = _tile(S, 512, 8)

    def body(y_ref, t_ref, dy_ref, l_ref):
        @pl.when(pl.program_id(0) == 0)
        def _():
            l_ref[...] = jnp.zeros_like(l_ref)

        e = y_ref[...] - t_ref[...]
        dy_ref[...] = e * (1.0 / D)
        l_ref[...] += 0.5 * jnp.sum(jnp.sum(e * e, axis=-1, keepdims=True) * (1.0 / D))

    row = pl.BlockSpec((tm, D), lambda i: (i, 0))
    one = pl.BlockSpec((8, LANE), lambda i: (0, 0))
    return pl.pallas_call(
        body, name=name, grid=(S // tm,),
        in_specs=[row, row], out_specs=[row, one],
        out_shape=[jax.ShapeDtypeStruct((S, D), F32), jax.ShapeDtypeStruct((8, LANE), F32)],
        compiler_params=_params("arbitrary"),
    )(y, target)


def _adamw(w, g, m, v, name):
    shape = w.shape
    C = shape[-1]
    R = math.prod(shape[:-1])
    tr = _tile(R, max(8, (1 << 19) // C // 8 * 8), 8)
    c1 = 1.0 / (1.0 - ADAM_B1 ** ADAM_STEP)
    c2 = 1.0 / (1.0 - ADAM_B2 ** ADAM_STEP)

    def body(w_ref, g_ref, m_ref, v_ref, d_ref, nm_ref, nv_ref):
        gv = g_ref[...]
        nm = ADAM_B1 * m_ref[...] + (1.0 - ADAM_B1) * gv
        nv = ADAM_B2 * v_ref[...] + (1.0 - ADAM_B2) * (gv * gv)
        nm_ref[...] = nm
        nv_ref[...] = nv
        d_ref[...] = -ADAM_LR * ((nm * c1) / (jnp.sqrt(nv * c2) + ADAM_EPS) + ADAM_WD * w_ref[...])

    blk = pl.BlockSpec((tr, C), lambda i: (i, 0))
    sds = jax.ShapeDtypeStruct((R, C), F32)
    outs = pl.pallas_call(
        body, name=name, grid=(R // tr,),
        in_specs=[blk] * 4, out_specs=[blk] * 3, out_shape=[sds] * 3,
        compiler_params=_params("parallel"),
    )(*(a.reshape(R, C) for a in (w, g, m, v)))
    return tuple(o.reshape(shape) for o in outs)


def _me():
    return lax.axis_index("x"), lax.axis_index("y"), lax.axis_index("c")


def _add_core_halves(grad, got, c_idx, name):
    n, nk, _, r, C = grad.shape
    tr = _tile(r, 1024, SUBLANE_BF16)

    def body(c_ref, g_ref, r_ref, o_ref):
        o_ref[...] = (g_ref[...].astype(F32) + r_ref[...].astype(F32)).astype(BF16)

    return pl.pallas_call(
        body, name=name,
        grid_spec=pltpu.PrefetchScalarGridSpec(
            num_scalar_prefetch=1, grid=(n, nk, r // tr),
            in_specs=[pl.BlockSpec((None, None, None, tr, C), lambda i, k, t, c: (i, k, c[0], t, 0)),
                      pl.BlockSpec((None, None, tr, C), lambda i, k, t, c: (i, k, t, 0))],
            out_specs=pl.BlockSpec((None, None, tr, C), lambda i, k, t, c: (i, k, t, 0))),
        out_shape=jax.ShapeDtypeStruct((n, nk, r, C), BF16),
        compiler_params=_params("parallel", "parallel", "parallel"),
    )(c_idx, grad, got)


HBM = pl.BlockSpec(memory_space=pltpu.HBM)
SEM = pl.BlockSpec(memory_space=pltpu.SEMAPHORE)
EFFECT = pltpu.SideEffectType.DATAFLOW_SIDE_EFFECTING


def _push_start(srcs, lands, plan, after, name):
    ns, nl = len(srcs), len(lands)
    ncp = len(plan([None] * ns, [None] * nl, dry=True))
    extra = [] if after is None else [after]

    def body(*refs):
        src_refs, land_refs = refs[:ns], refs[ns:ns + nl]
        send_sems, recv_sems = refs[ns + nl + len(extra)], refs[ns + nl + len(extra) + 1]
        token = refs[-1]
        for i, (s, d, to) in enumerate(plan(src_refs, land_refs)):
            pltpu.make_async_remote_copy(src_ref=s, dst_ref=d, send_sem=send_sems.at[i], recv_sem=recv_sems.at[i],
                                         device_id=to, device_id_type=MESH).start()
        token[...] = jnp.zeros_like(token)

    out = pl.pallas_call(
        body, name=name,
        out_shape=(pltpu.SemaphoreType.DMA((ncp,)), pltpu.SemaphoreType.DMA((ncp,)),
                   *[pltpu.HBM(a.shape, a.dtype) for a in srcs], *[pltpu.HBM(a.shape, a.dtype) for a in lands],
                   jax.ShapeDtypeStruct((8, LANE), F32)),
        in_specs=[HBM] * (ns + nl) + [ANY] * len(extra),
        out_specs=(SEM, SEM, *[HBM] * (ns + nl), pl.BlockSpec(memory_space=pltpu.VMEM)),
        input_output_aliases={i: 2 + i for i in range(ns + nl)},
        compiler_params=pltpu.CompilerParams(has_side_effects=EFFECT),
    )(*[pltpu.with_memory_space_constraint(a, pltpu.HBM) for a in srcs + lands], *extra)
    return out[0], out[1], list(out[2:2 + ns]), list(out[2 + ns:2 + ns + nl]), out[-1]


def _push_wait(send_sems, recv_sems, srcs, lands, plan, after, name):
    ns, nl = len(srcs), len(lands)
    after = list(after) if isinstance(after, (list, tuple)) else [after]

    def body(*refs):
        src_refs, land_refs = refs[:ns], refs[ns:ns + nl]
        send, recv = refs[ns + nl], refs[ns + nl + 1]
        for i, (s, d, to) in enumerate(plan(src_refs, land_refs)):
            cp = pltpu.make_async_remote_copy(src_ref=s, dst_ref=d, send_sem=send.at[i], recv_sem=recv.at[i],
                                              device_id=to, device_id_type=MESH)
            cp.wait_send()
            cp.wait_recv()

    out = pl.pallas_call(
        body, name=name,
        out_shape=tuple(pltpu.HBM(a.shape, a.dtype) for a in srcs + lands),
        in_specs=[HBM] * (ns + nl) + [SEM, SEM] + [ANY] * len(after),
        out_specs=tuple([HBM] * (ns + nl)),
        input_output_aliases={i: i for i in range(ns + nl)},
        compiler_params=pltpu.CompilerParams(has_side_effects=EFFECT),
    )(*srcs, *lands, send_sems, recv_sems, *after)
    return list(out[:ns]), list(out[ns:])


def _gather_plan(src_refs, land_refs, dry=False):
    if dry:
        return [None] * (4 * len(src_refs))
    x, y, c = _me()
    me = 4 * x + 2 * y + c
    targets = [(x, y, 1 - c), (1 - x, y, c), (x, 1 - y, c), (1 - x, 1 - y, c)]
    return [(s, l.at[:, me], to) for s, l in zip(src_refs, land_refs) for to in targets]


def _halves_plan(src_refs, land_refs, dry=False):
    if dry:
        return [None] * len(src_refs)
    x, y, c = _me()
    return [(s.at[:, :, 1 - c], l, (x, y, 1 - c)) for s, l in zip(src_refs, land_refs)]


def _chip_plan(src_refs, land_refs, dry=False):
    if dry:
        return [None] * (3 * len(src_refs))
    x, y, c = _me()
    chips = [(1 - x, y), (x, 1 - y), (1 - x, 1 - y)]
    return [(s.at[:, 2 * chip[0] + chip[1]], l.at[j], (*chip, c))
            for s, l in zip(src_refs, land_refs) for j, chip in enumerate(chips)]


def _gather_forward(lands, name):
    na = len(lands)

    def body(*refs):
        bufs = refs[na:2 * na]
        send_sems, recv_sems = refs[2 * na:]
        x, y, c = _me()
        chips = [(1 - x, y), (x, 1 - y), (1 - x, 1 - y)]

        def copy(a, j, pc):
            blk = bufs[a].at[:, 4 * chips[j][0] + 2 * chips[j][1] + pc]
            return pltpu.make_async_remote_copy(
                src_ref=blk, dst_ref=blk, send_sem=send_sems.at[3 * a + j], recv_sem=recv_sems.at[3 * a + j],
                device_id=(x, y, 1 - c), device_id_type=MESH)

        pairs = [(a, j) for a in range(na) for j in range(3)]
        for a, j in pairs:
            copy(a, j, c).start()
        for a, j in pairs:
            copy(a, j, 1 - c).wait_recv()
        for a, j in pairs:
            copy(a, j, c).wait_send()

    return pl.pallas_call(
        body, name=name,
        in_specs=[ANY] * na, out_specs=[ANY] * na,
        out_shape=[jax.ShapeDtypeStruct(a.shape, a.dtype) for a in lands],
        input_output_aliases={a: a for a in range(na)},
        scratch_shapes=[pltpu.SemaphoreType.DMA((3 * na,)), pltpu.SemaphoreType.DMA((3 * na,))],
    )(*lands)


def _sum_chip_blocks(sums, got, k_idx, name):
    n, _, r, C = sums.shape
    tr = _tile(r, 512, SUBLANE_BF16)

    def body(k_ref, s_ref, r_ref, o_ref):
        acc = s_ref[...].astype(F32)
        for j in range(3):
            acc = acc + r_ref[j].astype(F32)
        o_ref[...] = acc

    return pl.pallas_call(
        body, name=name,
        grid_spec=pltpu.PrefetchScalarGridSpec(
            num_scalar_prefetch=1, grid=(n, r // tr),
            in_specs=[pl.BlockSpec((None, None, tr, C), lambda i, t, k: (i, k[0], t, 0)),
                      pl.BlockSpec((3, None, tr, C), lambda i, t, k: (0, i, t, 0))],
            out_specs=pl.BlockSpec((None, tr, C), lambda i, t, k: (i, t, 0))),
        out_shape=jax.ShapeDtypeStruct((n, r, C), F32),
        compiler_params=_params("parallel", "parallel"),
    )(k_idx, sums, got)


def _all_reduce_small(v, name, after=()):
    R = v.shape[0]
    after = list(after)

    def body(*refs):
        v_ref = refs[0]
        o_ref, buf, send_sems, recv_sems = refs[1 + len(after):]
        x, y, c = _me()
        me = 4 * x + 2 * y + c
        buf[me] = v_ref[...]
        copies = []
        for k in range(1, N_DEV):
            peer = (x ^ (k >> 2), y ^ ((k >> 1) & 1), c ^ (k & 1))
            copies.append(pltpu.make_async_remote_copy(
                src_ref=v_ref, dst_ref=buf.at[me],
                send_sem=send_sems.at[k - 1], recv_sem=recv_sems.at[k - 1],
                device_id=peer, device_id_type=MESH))
        for cp in copies:
            cp.start()
        for cp in copies:
            cp.wait()
        acc = buf[0]
        for d in range(1, N_DEV):
            acc = acc + buf[d]
        o_ref[...] = acc

    vm = pl.BlockSpec(memory_space=pltpu.VMEM)
    return pl.pallas_call(
        body, name=name, in_specs=[vm] + [ANY] * len(after), out_specs=vm,
        out_shape=jax.ShapeDtypeStruct((R, LANE), F32),
        scratch_shapes=[pltpu.VMEM((N_DEV, R, LANE), F32),
                        pltpu.SemaphoreType.DMA((N_DEV - 1,)), pltpu.SemaphoreType.DMA((N_DEV - 1,))],
        compiler_params=pltpu.CompilerParams(vmem_limit_bytes=VMEM_LIMIT_BYTES),
    )(v, *after)


def _pack_rows(parts):
    flat = jnp.concatenate([p.reshape(-1).astype(F32) for p in parts])
    n = flat.shape[0]
    rows = -(-n // (8 * LANE)) * 8
    flat = jnp.pad(flat, (0, rows * LANE - n))
    return flat.reshape(rows, LANE)


def _unpack_rows(packed, shapes):
    flat = packed.reshape(-1)
    out, off = [], 0
    for s in shapes:
        n = math.prod(s)
        out.append(flat[off:off + n].reshape(s))
        off += n
    return out


CONV_HALO = 32


def _conv_fwd(u, conv_w, conv_b, cn_g, cn_b, name):
    S = u.shape[0]
    C = conv_w.shape[1]
    T = _tile(S, 256, CONV_HALO)
    hb = T // CONV_HALO

    def body(av_ref, ag_ref, pv_ref, pg_ref, w_ref, b_ref, g_ref, bb_ref, out_ref, y_ref, scr):
        i = pl.program_id(0)
        prev = pv_ref[...] * jax.nn.sigmoid(pg_ref[...])
        scr[0:CONV_HALO, :] = jnp.where(i > 0, prev, 0.0)
        scr[CONV_HALO:CONV_HALO + T, :] = av_ref[...] * jax.nn.sigmoid(ag_ref[...])
        acc = jnp.broadcast_to(b_ref[...], (T, C))
        for j in range(CONV_WIDTH):
            acc = acc + w_ref[j:j + 1, :] * scr[pl.ds(CONV_HALO - (CONV_WIDTH - 1) + j, T), :]
        y_ref[...] = acc
        mu = jnp.mean(acc, axis=-1, keepdims=True)
        xc = acc - mu
        var = jnp.mean(xc * xc, axis=-1, keepdims=True)
        ln = xc * lax.rsqrt(var + EPS) * g_ref[...] + bb_ref[...]
        out_ref[...] = (ln * jax.nn.sigmoid(ln)).astype(BF16)

    def cur(cb):
        return pl.BlockSpec((T, C), lambda i: (i, cb))

    def halo(cb):
        return pl.BlockSpec((CONV_HALO, C), lambda i: (jnp.maximum(i * hb - 1, 0), cb))

    vec = pl.BlockSpec((1, C), lambda i: (0, 0))
    return pl.pallas_call(
        body, name=name, grid=(S // T,),
        in_specs=[cur(0), cur(1), halo(0), halo(1), pl.BlockSpec((CONV_WIDTH, C), lambda i: (0, 0)), vec, vec, vec],
        out_specs=[pl.BlockSpec((T, C), lambda i: (i, 0))] * 2,
        out_shape=[jax.ShapeDtypeStruct((S, C), BF16), jax.ShapeDtypeStruct((S, C), F32)],
        scratch_shapes=[pltpu.VMEM((T + CONV_HALO, C), F32)],
        compiler_params=_params("parallel"),
    )(u, u, u, u, conv_w, conv_b, cn_g, cn_b)


def _conv_bwd_norm(dz, y, cn_g, cn_b, name):
    S, C = y.shape
    T = _tile(S, 256, 8)

    def body(dz_ref, y_ref, g_ref, bb_ref, dy_ref, dg_ref, db_ref):
        @pl.when(pl.program_id(0) == 0)
        def _():
            dg_ref[...] = jnp.zeros_like(dg_ref)
            db_ref[...] = jnp.zeros_like(db_ref)

        yv = y_ref[...]
        mu = jnp.mean(yv, axis=-1, keepdims=True)
        xc = yv - mu
        rstd = lax.rsqrt(jnp.mean(xc * xc, axis=-1, keepdims=True) + EPS)
        xh = xc * rstd
        ln = xh * g_ref[...] + bb_ref[...]
        sg = jax.nn.sigmoid(ln)
        dln = dz_ref[...] * (sg * (1.0 + ln * (1.0 - sg)))
        dg_ref[...] += jnp.sum(dln * xh, axis=0, keepdims=True)
        db_ref[...] += jnp.sum(dln, axis=0, keepdims=True)
        dxh = dln * g_ref[...]
        dy_ref[...] = rstd * (dxh - jnp.mean(dxh, axis=-1, keepdims=True)
                              - xh * jnp.mean(dxh * xh, axis=-1, keepdims=True))

    row = pl.BlockSpec((T, C), lambda i: (i, 0))
    vec = pl.BlockSpec((1, C), lambda i: (0, 0))
    return pl.pallas_call(
        body, name=name, grid=(S // T,),
        in_specs=[row, row, vec, vec], out_specs=[row, vec, vec],
        out_shape=[jax.ShapeDtypeStruct((S, C), F32), jax.ShapeDtypeStruct((1, C), F32),
                   jax.ShapeDtypeStruct((1, C), F32)],
        compiler_params=_params("arbitrary"),
    )(dz, y, cn_g, cn_b)


def _conv_bwd_taps(u, dy, conv_w, name):
    S, C = dy.shape
    T = _tile(S, 256, CONV_HALO)
    hb = T // CONV_HALO
    nt = S // T
    W1 = CONV_WIDTH - 1

    def body(av_ref, ag_ref, pv_ref, pg_ref, dy_ref, dn_ref, w_ref, dv_ref, dg_ref, dw_ref, db_ref, a_scr, d_scr):
        i = pl.program_id(0)

        @pl.when(i == 0)
        def _():
            dw_ref[...] = jnp.zeros_like(dw_ref)
            db_ref[...] = jnp.zeros_like(db_ref)

        av, sg = av_ref[...], jax.nn.sigmoid(ag_ref[...])
        prev = pv_ref[...] * jax.nn.sigmoid(pg_ref[...])
        a_scr[0:CONV_HALO, :] = jnp.where(i > 0, prev, 0.0)
        a_scr[CONV_HALO:CONV_HALO + T, :] = av * sg
        dyv = dy_ref[...]
        d_scr[0:T, :] = dyv
        d_scr[T:T + CONV_HALO, :] = jnp.where(i < nt - 1, dn_ref[...], 0.0)
        da = jnp.zeros((T, C), F32)
        for j in range(CONV_WIDTH):
            da = da + w_ref[j:j + 1, :] * d_scr[pl.ds(W1 - j, T), :]
            dw_ref[j:j + 1, :] += jnp.sum(dyv * a_scr[pl.ds(CONV_HALO - W1 + j, T), :], axis=0, keepdims=True)
        db_ref[...] += jnp.sum(dyv, axis=0, keepdims=True)
        dv_ref[...] = (da * sg).astype(BF16)
        dg_ref[...] = (da * av * sg * (1.0 - sg)).astype(BF16)

    def cur(cb):
        return pl.BlockSpec((T, C), lambda i: (i, cb))

    def halo(cb):
        return pl.BlockSpec((CONV_HALO, C), lambda i: (jnp.maximum(i * hb - 1, 0), cb))

    nxt = pl.BlockSpec((CONV_HALO, C), lambda i: (jnp.minimum((i + 1) * hb, S // CONV_HALO - 1), 0))
    row = pl.BlockSpec((T, C), lambda i: (i, 0))
    return pl.pallas_call(
        body, name=name, grid=(nt,),
        in_specs=[cur(0), cur(1), halo(0), halo(1), row, nxt, pl.BlockSpec((CONV_WIDTH, C), lambda i: (0, 0))],
        out_specs=[row, row, pl.BlockSpec((CONV_HALO, C), lambda i: (0, 0)), pl.BlockSpec((1, C), lambda i: (0, 0))],
        out_shape=[jax.ShapeDtypeStruct((S, C), BF16), jax.ShapeDtypeStruct((S, C), BF16),
                   jax.ShapeDtypeStruct((CONV_HALO, C), F32), jax.ShapeDtypeStruct((1, C), F32)],
        scratch_shapes=[pltpu.VMEM((T + CONV_HALO, C), F32), pltpu.VMEM((T + CONV_HALO, C), F32)],
        compiler_params=_params("arbitrary"),
    )(u, u, u, u, dy, dy, conv_w)


def _rope_tables(S):
    half = HEAD_DIM // 2
    inv = jnp.exp(-math.log(ROPE_THETA) * jnp.arange(half, dtype=F32) / half)
    ang = jnp.arange(S, dtype=jnp.int32).astype(F32)[:, None] * inv[None, :]
    cos, sin = jnp.cos(ang), jnp.sin(ang)
    return jnp.concatenate([cos, cos], axis=1), jnp.concatenate([-sin, sin], axis=1)


def _qkv_prep(u, qn_g, kn_g, cos, sin, cb0, name):
    S = u.shape[0]
    A = (u.shape[1] // (cb0 + 3))
    H = A // HEAD_DIM
    T = _tile(S, 256, SUBLANE_BF16)
    scale = HEAD_DIM ** -0.5

    def body(q_ref, k_ref, v_ref, qg_ref, kg_ref, cos_ref, sin_ref, qo_ref, ko_ref, vo_ref):
        cosv, sinv = cos_ref[...], sin_ref[...]
        for h in range(H):
            sl = slice(h * HEAD_DIM, (h + 1) * HEAD_DIM)
            for x_ref, g_ref, o_ref, sc in ((q_ref, qg_ref, qo_ref, scale), (k_ref, kg_ref, ko_ref, 1.0)):
                xv = x_ref[:, sl]
                xn = xv * lax.rsqrt(jnp.mean(xv * xv, axis=-1, keepdims=True) + EPS) * g_ref[...]
                y = xn * cosv + pltpu.roll(xn, HEAD_DIM // 2, 1) * sinv
                o_ref[:, sl] = (y * sc).astype(BF16)
        vo_ref[...] = v_ref[...].astype(BF16)

    def col(cb):
        return pl.BlockSpec((T, A), lambda i: (i, cb))

    vec = pl.BlockSpec((1, HEAD_DIM), lambda i: (0, 0))
    tab = pl.BlockSpec((T, HEAD_DIM), lambda i: (i, 0))
    out = pl.BlockSpec((T, A), lambda i: (i, 0))
    return pl.pallas_call(
        body, name=name, grid=(S // T,),
        in_specs=[col(cb0), col(cb0 + 1), col(cb0 + 2), vec, vec, tab, tab],
        out_specs=[out] * 3, out_shape=[jax.ShapeDtypeStruct((S, A), BF16)] * 3,
        compiler_params=_params("parallel"),
    )(u, u, u, qn_g, kn_g, cos, sin)


def _qkv_prep_bwd(u, dqs, dks, dvs, qn_g, kn_g, cos, sin, cb0, name):
    S = u.shape[0]
    A = dqs[0].shape[1]
    H = A // HEAD_DIM
    T = _tile(S, 256, SUBLANE_BF16)
    nb = len(dqs)
    scale = HEAD_DIM ** -0.5

    def body(*refs):
        q_ref, k_ref, qg_ref, kg_ref, cos_ref, sin_ref = refs[:6]
        dq_refs, dk_refs, dv_refs = refs[6:6 + nb], refs[6 + nb:6 + 2 * nb], refs[6 + 2 * nb:6 + 3 * nb]
        dqo_ref, dko_ref, dvo_ref, dqg_ref, dkg_ref = refs[6 + 3 * nb:]

        @pl.when(pl.program_id(0) == 0)
        def _():
            dqg_ref[...] = jnp.zeros_like(dqg_ref)
            dkg_ref[...] = jnp.zeros_like(dkg_ref)

        cosv, sinv = cos_ref[...], sin_ref[...]
        for h in range(H):
            sl = slice(h * HEAD_DIM, (h + 1) * HEAD_DIM)
            for x_ref, g_ref, d_refs, o_ref, dg_ref, sc in ((q_ref, qg_ref, dq_refs, dqo_ref, dqg_ref, scale),
                                                          (k_ref, kg_ref, dk_refs, dko_ref, dkg_ref, 1.0)):
                dy = d_refs[0][:, sl]
                for r in d_refs[1:]:
                    dy = dy + r[:, sl]
                dy = dy * sc
                dxn = dy * cosv + pltpu.roll(dy * sinv, HEAD_DIM // 2, 1)
                xv = x_ref[:, sl]
                r = lax.rsqrt(jnp.mean(xv * xv, axis=-1, keepdims=True) + EPS)
                xh = xv * r
                dg_ref[...] += jnp.sum(dxn * xh, axis=0, keepdims=True)
                dxh = dxn * g_ref[...]
                o_ref[:, sl] = (r * (dxh - xh * jnp.mean(dxh * xh, axis=-1, keepdims=True))).astype(BF16)
        dv = dv_refs[0][...]
        for r in dv_refs[1:]:
            dv = dv + r[...]
        dvo_ref[...] = dv.astype(BF16)

    def col(cb):
        return pl.BlockSpec((T, A), lambda i: (i, cb))

    vec = pl.BlockSpec((1, HEAD_DIM), lambda i: (0, 0))
    tab = pl.BlockSpec((T, HEAD_DIM), lambda i: (i, 0))
    row = pl.BlockSpec((T, A), lambda i: (i, 0))
    return pl.pallas_call(
        body, name=name, grid=(S // T,),
        in_specs=[col(cb0), col(cb0 + 1), vec, vec, tab, tab] + [row] * (3 * nb),
        out_specs=[row, row, row, vec, vec],
        out_shape=[jax.ShapeDtypeStruct((S, A), BF16)] * 3 + [jax.ShapeDtypeStruct((1, HEAD_DIM), F32)] * 2,
        compiler_params=_params("arbitrary"),
    )(u, u, qn_g, kn_g, cos, sin, *dqs, *dks, *dvs)


ATT_TILE = 256
NEG = -1e30


def _attn_bias(tile):
    span = max(window for window, _ in DIL_PATTERNS)
    nw = -(-span // tile) + 1
    dist = (jnp.arange(nw)[:, None, None] * tile + jnp.arange(tile)[None, :, None] - jnp.arange(tile)[None, None, :])
    mult = sum(((dist >= 0) & (dist <= window) & (dist % dil == 0)).astype(F32) for window, dil in DIL_PATTERNS)
    return jnp.where(mult > 0, jnp.log(jnp.maximum(mult, 1.0)), NEG)


def _attn_fwd(q, k, v, bias, name):
    S, A = q.shape
    H = A // HEAD_DIM
    nw, T, _ = bias.shape
    nq = S // T

    def body(q_ref, k_ref, v_ref, b_ref, ob_ref, of_ref, l_ref, s_scr):
        i = pl.program_id(1)
        qv = q_ref[...]
        mx = jnp.full((T, 1), NEG, F32)
        for w in range(nw):
            blk = i - w
            start = pl.multiple_of(jnp.maximum(blk, 0) * T, T)
            s = _dot(qv, k_ref[pl.ds(start, T), :], "nt") + b_ref[w] + jnp.where(blk >= 0, 0.0, NEG)
            s_scr[w] = s
            mx = jnp.maximum(mx, jnp.max(s, axis=-1, keepdims=True))
        den = jnp.zeros((T, 1), F32)
        o = jnp.zeros((T, HEAD_DIM), F32)
        for w in range(nw):
            start = pl.multiple_of(jnp.maximum(i - w, 0) * T, T)
            p = jnp.exp(s_scr[w] - mx)
            den = den + jnp.sum(p, axis=-1, keepdims=True)
            o = o + _dot(p.astype(BF16), v_ref[pl.ds(start, T), :], "nn")
        o = o / den
        ob_ref[...] = o.astype(BF16)
        of_ref[...] = o
        l_ref[...] = mx + jnp.log(den)

    blk = pl.BlockSpec((T, HEAD_DIM), lambda h, i: (i, h))
    full = pl.BlockSpec((S, HEAD_DIM), lambda h, i: (0, h))
    return pl.pallas_call(
        body, name=name, grid=(H, nq),
        in_specs=[blk, full, full, pl.BlockSpec((nw, T, T), lambda h, i: (0, 0, 0))],
        out_specs=[blk, blk, pl.BlockSpec((None, T, 1), lambda h, i: (h, i, 0))],
        out_shape=[jax.ShapeDtypeStruct((S, A), BF16), jax.ShapeDtypeStruct((S, A), F32),
                   jax.ShapeDtypeStruct((H, S, 1), F32)],
        scratch_shapes=[pltpu.VMEM((nw, T, T), F32)],
        compiler_params=_params("parallel", "arbitrary"),
    )(q, k, v, bias)


def _attn_dq(q, k, v, dz, cb0, o, lse, bias, name):
    S, A = q.shape
    H = A // HEAD_DIM
    nw, T, _ = bias.shape
    nq = S // T

    def body(q_ref, k_ref, v_ref, do_ref, o_ref, l_ref, b_ref, dq_ref, d_ref):
        i = pl.program_id(1)
        qv, dof = q_ref[...], do_ref[...]
        dov = dof.astype(BF16)
        delta = jnp.sum(dof * o_ref[...], axis=-1, keepdims=True)
        d_ref[...] = delta
        lv = l_ref[...]
        dq = jnp.zeros((T, HEAD_DIM), F32)
        for w in range(nw):
            blk = i - w
            start = pl.multiple_of(jnp.maximum(blk, 0) * T, T)
            kv = k_ref[pl.ds(start, T), :]
            s = _dot(qv, kv, "nt") + b_ref[w] + jnp.where(blk >= 0, 0.0, NEG)
            p = jnp.exp(s - lv)
            ds = (p * (_dot(dov, v_ref[pl.ds(start, T), :], "nt") - delta)).astype(BF16)
            dq = dq + _dot(ds, kv, "nn")
        dq_ref[...] = dq

    blk = pl.BlockSpec((T, HEAD_DIM), lambda h, i: (i, h))
    full = pl.BlockSpec((S, HEAD_DIM), lambda h, i: (0, h))
    col = pl.BlockSpec((None, T, 1), lambda h, i: (h, i, 0))
    return pl.pallas_call(
        body, name=name, grid=(H, nq),
        in_specs=[blk, full, full, pl.BlockSpec((T, HEAD_DIM), lambda h, i: (i, cb0 + h)), blk, col,
                  pl.BlockSpec((nw, T, T), lambda h, i: (0, 0, 0))],
        out_specs=[blk, col],
        out_shape=[jax.ShapeDtypeStruct((S, A), F32), jax.ShapeDtypeStruct((H, S, 1), F32)],
        compiler_params=_params("parallel", "arbitrary"),
    )(q, k, v, dz, o, lse, bias)


def _attn_dkv(q, k, v, dz, cb0, lse, delta, bias, name):
    S, A = q.shape
    H = A // HEAD_DIM
    nw, T, _ = bias.shape
    nq = S // T

    def body(k_ref, v_ref, q_ref, do_ref, l_ref, d_ref, b_ref, dk_ref, dv_ref):
        m = pl.program_id(1)
        kv, vv = k_ref[...], v_ref[...]
        dk = jnp.zeros((T, HEAD_DIM), F32)
        dv = jnp.zeros((T, HEAD_DIM), F32)
        for w in range(nw):
            blk = m + w
            start = pl.multiple_of(jnp.minimum(blk, nq - 1) * T, T)
            qv = q_ref[pl.ds(start, T), :]
            dov = do_ref[pl.ds(start, T), :].astype(BF16)
            s = _dot(qv, kv, "nt") + b_ref[w] + jnp.where(blk < nq, 0.0, NEG)
            p = jnp.exp(s - l_ref[pl.ds(start, T), :])
            dv = dv + _dot(p.astype(BF16), dov, "tn")
            ds = (p * (_dot(dov, vv, "nt") - d_ref[pl.ds(start, T), :])).astype(BF16)
            dk = dk + _dot(ds, qv, "tn")
        dk_ref[...] = dk
        dv_ref[...] = dv

    blk = pl.BlockSpec((T, HEAD_DIM), lambda h, m: (m, h))
    full = pl.BlockSpec((S, HEAD_DIM), lambda h, m: (0, h))
    col = pl.BlockSpec((None, S, 1), lambda h, m: (h, 0, 0))
    sds = jax.ShapeDtypeStruct((S, A), F32)
    return pl.pallas_call(
        body, name=name, grid=(H, nq),
        in_specs=[blk, blk, full, pl.BlockSpec((S, HEAD_DIM), lambda h, m: (0, cb0 + h)), col, col,
                  pl.BlockSpec((nw, T, T), lambda h, m: (0, 0, 0))],
        out_specs=[blk, blk], out_shape=[sds, sds],
        compiler_params=_params("parallel", "arbitrary"),
    )(k, v, q, dz, lse, delta, bias)


def _even_mixer(u, conv_w, conv_b, cn_g, cn_b, qn_g, kn_g, tag):
    S = u.shape[0]
    C = conv_w.shape[1]
    A = (u.shape[1] - 2 * C) // 3
    assert A == C, "column-block addressing of u assumes equal conv and attention widths"
    T = _tile(S, ATT_TILE, LANE)
    cos, sin = _rope_tables(S)
    bias = _attn_bias(T)
    a_out, y = _conv_fwd(u, conv_w, conv_b, cn_g, cn_b, "conv_fwd" + tag)
    q, k, v = _qkv_prep(u, qn_g, kn_g, cos, sin, 2, "qkv_prep" + tag)
    ob, of, lse = _attn_fwd(q, k, v, bias, "attn_fwd" + tag)
    z = jnp.concatenate([a_out, ob], axis=1)

    def backward(dz):
        dy, d_cn_g, d_cn_b = _conv_bwd_norm(dz, y, cn_g, cn_b, "conv_bwd_norm" + tag)
        d_val, d_gate, d_w, d_b = _conv_bwd_taps(u, dy, conv_w, "conv_bwd_taps" + tag)
        dqp, delta = _attn_dq(q, k, v, dz, C // HEAD_DIM, of, lse, bias, "attn_dq" + tag)
        dkp, dvp = _attn_dkv(q, k, v, dz, C // HEAD_DIM, lse, delta, bias, "attn_dkv" + tag)
        dq, dk, dv, d_qn, d_kn = _qkv_prep_bwd(u, [dqp], [dkp], [dvp], qn_g, kn_g, cos, sin, 2, "qkv_prep_bwd" + tag)
        du = jnp.concatenate([d_val, d_gate, dq, dk, dv], axis=1)
        return du, [d_w[:CONV_WIDTH], d_b[0], d_cn_g[0], d_cn_b[0], d_qn[0], d_kn[0]]

    return z, backward


_LEVELS = (128, 64, 32, 16, 8, 4, 2, 1)


def _chunk_cumsum(g, rows, reverse=False):
    C = g.shape[0]
    d = 1
    while d < C:
        if reverse:
            g = g + jnp.where(rows < C - d, pltpu.roll(g, C - d, 0), 0.0)
        else:
            g = g + jnp.where(rows >= d, pltpu.roll(g, d, 0), 0.0)
        d *= 2
    return g


def _level_ref(b, b_scr, rows, m):
    C = b.shape[0]
    if m >= 8:
        pieces = [jnp.broadcast_to(b_scr[2 * m * j + m - 1:2 * m * j + m, :], (2 * m, LANE)) for j in range(C // (2 * m))]
        return pieces[0] if len(pieces) == 1 else jnp.concatenate(pieces, axis=0)
    pos = rows & (2 * m - 1)
    ref = b
    for p in range(2 * m):
        if p != m - 1:
            ref = jnp.where(pos == p, pltpu.roll(b, (p - (m - 1)) % C, 0), ref)
    return ref


def _level_operands(q, k, b, b_scr, rows, m):
    ref = _level_ref(b, b_scr, rows, m)
    qs = (q * jnp.exp(jnp.minimum(b - ref, 0.0))).astype(BF16)
    ks = (k * jnp.exp(jnp.minimum(ref - b, 0.0))).astype(BF16)
    return qs, ks


def _split2(x):
    hi = x.astype(BF16)
    lo = (x - hi.astype(F32)).astype(BF16)
    return jnp.concatenate([hi, lo], axis=1)


def _level_mask(tt, ss, m):
    x = tt ^ ss
    return (tt > ss) & (x >= m) & (x < 2 * m)


def _hgrn_gates(qz, fz, la, lc, oml):
    sq = jax.nn.sigmoid(qz)
    q = qz * sq
    s = jax.nn.sigmoid(fz)
    c = lc + jnp.minimum(fz, 0.0) - jnp.log(1.0 + jnp.exp(-jnp.abs(fz)))
    mx = jnp.maximum(la, c)
    g = mx + jnp.log(1.0 + jnp.exp(-jnp.abs(la - c)))
    k = oml * (1.0 - s)
    return q, sq, k, s, g, c


def _hgrn_fwd(u, la, lc, oml, gn_g, name):
    S = u.shape[0]
    W = u.shape[1] // 4
    H = W // HGRN_KDIM
    C = min(HGRN_CHUNK, S)
    nc = S // C
    levels = [m for m in _LEVELS if m < C]

    def body(qz_ref, fz_ref, iz_ref, gz_ref, la_ref, lc_ref, oml_ref, gn_ref,
             z_ref, o_ref, a_ref, st_ref, state, b_scr):
        @pl.when(pl.program_id(1) == 0)
        def _():
            state[...] = jnp.zeros_like(state)

        rows = lax.broadcasted_iota(jnp.int32, (C, LANE), 0)
        tt = lax.broadcasted_iota(jnp.int32, (C, C), 0)
        ss = lax.broadcasted_iota(jnp.int32, (C, C), 1)
        q, _, k, _, g, _ = _hgrn_gates(qz_ref[...], fz_ref[...], la_ref[...], lc_ref[...], oml_ref[...])
        v = iz_ref[...].astype(BF16)
        b = _chunk_cumsum(g, rows)
        b_scr[...] = b
        a = jnp.where(tt == ss, jnp.sum(q * k, axis=-1, keepdims=True), 0.0)
        for m in levels:
            qs, ks = _level_operands(q, k, b, b_scr, rows, m)
            a = jnp.where(_level_mask(tt, ss, m), _dot(qs, ks, "nt"), a)
        ab = a.astype(BF16)
        a_ref[...] = ab
        st = state[...]
        st_ref[...] = st
        o = _dot(ab, v, "nn") + _dot((q * jnp.exp(b)).astype(BF16), st.astype(BF16), "nt")
        bl = b_scr[C - 1:C, :]
        kh = (k * jnp.exp(bl - b)).astype(BF16)
        state[...] = st * jnp.exp(bl) + _dot(v, kh, "tn")
        o_ref[...] = o
        r = lax.rsqrt(jnp.mean(o * o, axis=-1, keepdims=True) + EPS)
        gz = gz_ref[...]
        z_ref[...] = (o * r * gn_ref[...] * (gz * jax.nn.sigmoid(gz))).astype(BF16)

    def col(off):
        return pl.BlockSpec((C, LANE), lambda h, i: (i, off * H + h))

    vec = pl.BlockSpec((1, LANE), lambda h, i: (0, h))
    tile = pl.BlockSpec((C, LANE), lambda h, i: (i, h))
    return pl.pallas_call(
        body, name=name, grid=(H, nc),
        in_specs=[col(0), col(1), col(2), col(3), vec, vec, vec, vec],
        out_specs=[tile, tile, pl.BlockSpec((None, C, C), lambda h, i: (h, i, 0)),
                   pl.BlockSpec((None, None, LANE, LANE), lambda h, i: (h, i, 0, 0))],
        out_shape=[jax.ShapeDtypeStruct((S, W), BF16), jax.ShapeDtypeStruct((S, W), F32),
                   jax.ShapeDtypeStruct((H, S, C), BF16), jax.ShapeDtypeStruct((H, nc, LANE, LANE), F32)],
        scratch_shapes=[pltpu.VMEM((LANE, LANE), F32), pltpu.VMEM((C, LANE), F32)],
        compiler_params=_params("parallel", "arbitrary"),
    )(u, u, u, u, la, lc, oml, gn_g)


def _hgrn_bwd(u, la, lc, oml, gn_g, o, a, st, dz, name):
    S = u.shape[0]
    W = u.shape[1] // 4
    H = W // HGRN_KDIM
    C = min(HGRN_CHUNK, S)
    nc = S // C
    levels = [m for m in _LEVELS if m < C]

    def body(qz_ref, fz_ref, iz_ref, gz_ref, la_ref, lc_ref, oml_ref, gn_ref, o_ref, a_ref, st_ref, dz_ref,
             dqz_ref, dfz_ref, diz_ref, dgz_ref, dla_ref, dlc_ref, doml_ref, dgn_ref, dstate, b_scr):
        @pl.when(pl.program_id(1) == 0)
        def _():
            dstate[...] = jnp.zeros_like(dstate)
            dla_ref[...] = jnp.zeros_like(dla_ref)
            dlc_ref[...] = jnp.zeros_like(dlc_ref)
            doml_ref[...] = jnp.zeros_like(doml_ref)
            dgn_ref[...] = jnp.zeros_like(dgn_ref)

        rows = lax.broadcasted_iota(jnp.int32, (C, LANE), 0)
        tt = lax.broadcasted_iota(jnp.int32, (C, C), 0)
        ss = lax.broadcasted_iota(jnp.int32, (C, C), 1)
        la_v, lc_v, oml_v = la_ref[...], lc_ref[...], oml_ref[...]
        qz, fz = qz_ref[...], fz_ref[...]
        q, sq, k, s, g, c = _hgrn_gates(qz, fz, la_v, lc_v, oml_v)
        vf = iz_ref[...]
        v = vf.astype(BF16)

        ov, gz, dzv, gn = o_ref[...], gz_ref[...], dz_ref[...], gn_ref[...]
        r = lax.rsqrt(jnp.mean(ov * ov, axis=-1, keepdims=True) + EPS)
        on = ov * r
        sg = jax.nn.sigmoid(gz)
        silu_g = gz * sg
        dgn_ref[...] += jnp.sum(dzv * on * silu_g, axis=0, keepdims=True)
        dgz_ref[...] = (dzv * on * gn * (sg * (1.0 + gz * (1.0 - sg)))).astype(BF16)
        don = dzv * gn * silu_g
        do_f = r * (don - on * jnp.mean(don * on, axis=-1, keepdims=True))
        do = do_f.astype(BF16)

        b = _chunk_cumsum(g, rows)
        b_scr[...] = b
        bl = b_scr[C - 1:C, :]
        e = jnp.exp(b)
        ebl = jnp.exp(bl)
        ekl = jnp.exp(bl - b)
        qh = q * e
        kh = k * ekl
        st_v = st_ref[...]
        dst = dstate[...]
        dstb = dst.astype(BF16)

        diz_ref[...] = (_dot(a_ref[...], do, "tn") + _dot(kh.astype(BF16), dstb, "nt")).astype(BF16)
        da = _dot(do, v, "nt")
        dqh = _dot(do, st_v.astype(BF16), "nn")
        dkh = _dot(v, dstb, "nn")
        dstate[...] = dst * ebl + _dot(do, qh.astype(BF16), "tn")
        dbl = jnp.sum(dkh * kh, axis=0, keepdims=True) + jnp.sum(dst * st_v, axis=0, keepdims=True) * ebl

        datt = jnp.sum(do_f * vf, axis=-1, keepdims=True)
        dqa = datt * k
        dka = datt * q
        for m in levels:
            ref = _level_ref(b, b_scr, rows, m)
            eu = jnp.exp(jnp.minimum(b - ref, 0.0))
            el = jnp.exp(jnp.minimum(ref - b, 0.0))
            gm = jnp.where(_level_mask(tt, ss, m), da, 0.0).astype(BF16)
            pq = _dot(gm, _split2(k * el), "nn")
            pk = _dot(gm, _split2(q * eu), "tn")
            dqa += (pq[:, :LANE] + pq[:, LANE:]) * eu
            dka += (pk[:, :LANE] + pk[:, LANE:]) * el
        db = q * dqa - k * dka + dqh * qh - dkh * kh
        db = db + jnp.where(rows == C - 1, dbl, 0.0)
        dq = dqa + dqh * e
        dk = dka + dkh * ekl
        dg = _chunk_cumsum(db, rows, reverse=True)

        wa = jnp.exp(la_v - g)
        wc = jnp.exp(c - g)
        dqz_ref[...] = (dq * (sq * (1.0 + qz * (1.0 - sq)))).astype(BF16)
        dfz_ref[...] = (dg * wc * (1.0 - s) - dk * oml_v * s * (1.0 - s)).astype(BF16)
        dla_ref[...] += jnp.sum(dg * wa, axis=0, keepdims=True)
        dlc_ref[...] += jnp.sum(dg * wc, axis=0, keepdims=True)
        doml_ref[...] += jnp.sum(dk * (1.0 - s), axis=0, keepdims=True)

    def col(off):
        return pl.BlockSpec((C, LANE), lambda h, i: (nc - 1 - i, off * H + h))

    vec = pl.BlockSpec((1, LANE), lambda h, i: (0, h))
    tile = pl.BlockSpec((C, LANE), lambda h, i: (nc - 1 - i, h))
    a_spec = pl.BlockSpec((None, C, C), lambda h, i: (h, nc - 1 - i, 0))
    st_spec = pl.BlockSpec((None, None, LANE, LANE), lambda h, i: (h, nc - 1 - i, 0, 0))
    sw = jax.ShapeDtypeStruct((S, W), BF16)
    vw = jax.ShapeDtypeStruct((1, W), F32)
    return pl.pallas_call(
        body, name=name, grid=(H, nc),
        in_specs=[col(0), col(1), col(2), col(3), vec, vec, vec, vec, tile, a_spec, st_spec, tile],
        out_specs=[tile, tile, tile, tile, vec, vec, vec, vec],
        out_shape=[sw, sw, sw, sw, vw, vw, vw, vw],
        scratch_shapes=[pltpu.VMEM((LANE, LANE), F32), pltpu.VMEM((C, LANE), F32)],
        compiler_params=_params("parallel", "arbitrary"),
    )(u, u, u, u, la, lc, oml, gn_g, o, a, st, dz)


def _lb_terms(lb_logits, layer):
    p = jax.nn.softmax(lb_logits, axis=0)
    lb = (jnp.cumsum(p, axis=0) - p[0:1])[layer]
    return jnp.log(lb)[None], jnp.log1p(-lb)[None], (1.0 - lb)[None]


def kernel(x, norm_ffn1, ffn1_wg, ffn1_wu, ffn1_wd, norm_mix, norm_ffn2, ffn2_wg, ffn2_wu, ffn2_wd, ev_w_in, ev_conv_w, ev_conv_b, ev_cn_g, ev_cn_b, ev_qn_g, ev_kn_g, ev_w_out, od_w_in, od_lb_logits, od_gn_g, od_w_out, loss_target, m_norm_ffn1, m_ffn1_wg, m_ffn1_wu, m_ffn1_wd, m_norm_mix, m_norm_ffn2, m_ffn2_wg, m_ffn2_wu, m_ffn2_wd, m_ev_w_in, m_ev_conv_w, m_ev_conv_b, m_ev_cn_g, m_ev_cn_b, m_ev_qn_g, m_ev_kn_g, m_ev_w_out, m_od_w_in, m_od_lb_logits, m_od_gn_g, m_od_w_out, v_norm_ffn1, v_ffn1_wg, v_ffn1_wu, v_ffn1_wd, v_norm_mix, v_norm_ffn2, v_ffn2_wg, v_ffn2_wu, v_ffn2_wd, v_ev_w_in, v_ev_conv_w, v_ev_conv_b, v_ev_cn_g, v_ev_cn_b, v_ev_qn_g, v_ev_kn_g, v_ev_w_out, v_od_w_in, v_od_lb_logits, v_od_gn_g, v_od_w_out):
    depth = norm_ffn1.shape[0]
    S, D = x.shape[1], x.shape[2]
    xi, yi, ci = _me()
    dev = 4 * xi + 2 * yi + ci
    c_idx = jnp.reshape(ci, (1,)).astype(jnp.int32)
    k_idx = jnp.reshape(2 * xi + yi, (1,)).astype(jnp.int32)

    def ffn_shard(wg, wu, wd, l):
        return jnp.stack([wg[l].T, wu[l].T, wd[l]]).astype(BF16)

    assert depth == 2, "the exchange schedule below is written for one even and one odd layer"
    sh_ffn1 = [ffn_shard(ffn1_wg, ffn1_wu, ffn1_wd, l) for l in range(depth)]
    sh_ffn2 = [ffn_shard(ffn2_wg, ffn2_wu, ffn2_wd, l) for l in range(depth)]
    sh_ev = [ev_w_in[0].T.astype(BF16)[None], ev_w_out[0].astype(BF16)[None]]
    sh_od = [od_w_in[0].T.astype(BF16)[None], od_w_out[0].astype(BF16)[None]]

    def full(g):
        return g.reshape(g.shape[0], N_DEV * g.shape[2], g.shape[3])

    def gather_begin(shards, after, tag):
        lands = [lax.dynamic_update_slice(lax.empty((s.shape[0], N_DEV) + s.shape[1:], s.dtype), s[:, None],
                                          (0, dev, 0, 0)) for s in shards]
        state = _push_start(shards, lands, _gather_plan, after, "gather_start" + tag)
        return state, state[4][0, 0]

    def gather_end(state, after, tag):
        send, recv, srcs, lands, _ = state
        _, lands = _push_wait(send, recv, srcs, lands, _gather_plan, after, "gather_wait" + tag)
        return [full(g) for g in _gather_forward(lands, "gather_forward" + tag)]

    w_ffn1, w_ffn2 = [None] * depth, [None] * depth
    group_0, _ = gather_begin([sh_ffn1[0]], None, "_0")

    conv_w_sh, gn_g_sh = ev_conv_w[0], od_gn_g[0]
    cw, cs = conv_w_sh.shape[0], conv_w_sh.shape[1]
    gs = gn_g_sh.shape[0]
    conv_w_z = lax.dynamic_update_slice(jnp.zeros((cw, N_DEV * cs), F32), conv_w_sh, (0, dev * cs))
    gn_g_z = lax.dynamic_update_slice(jnp.zeros((N_DEV * gs,), F32), gn_g_sh, (dev * gs,))
    conv_w_full, gn_g_full = _unpack_rows(
        _all_reduce_small(_pack_rows([conv_w_z, gn_g_z]), "gather_small_params"),
        [conv_w_z.shape, gn_g_z.shape])
    (w_ffn1[0],) = gather_end(group_0, sh_ev + sh_od + sh_ffn2 + [sh_ffn1[1], conv_w_full], "_0")

    def ffn_forward(h, gain, w, tag):
        hn = _rms_fwd(h, gain[None], "rms_" + tag)
        out, gu = _ffn_fwd(h, hn, w, "ffn_fwd_" + tag)
        return out, (h, hn, gu)

    def odd_mixer(u, l):
        (la, lc, oml), lb_vjp = jax.vjp(functools.partial(_lb_terms, layer=l), od_lb_logits)
        gn = gn_g_full[None]
        zb, o_raw, scores, states = _hgrn_fwd(u, la, lc, oml, gn, f"hgrn_fwd{l}")

        def backward(dz):
            dqz, dfz, diz, dgz, dla, dlc, doml, dgn = _hgrn_bwd(
                u, la, lc, oml, gn, o_raw, scores, states, dz, f"hgrn_bwd{l}")
            (g_lb,) = lb_vjp((dla, dlc, doml))
            return jnp.concatenate([dqz, dfz, diz, dgz], axis=1), [g_lb, dgn[0]]

        return zb, backward

    saved = []
    h = x[0]
    for l in range(depth):
        if l == 0:
            group_a, tok = gather_begin(sh_ev + [sh_ffn2[0]], w_ffn1[0], "_a")
        else:
            group_c, tok = gather_begin([sh_ffn2[1]], w_ffn1[1], "_c")
        h, s1 = ffn_forward(h, norm_ffn1[l] + tok, w_ffn1[l], f"a{l}")
        tok = 0.0
        if l == 0:
            ev_in, ev_out, w_ffn2[0] = gather_end(group_a, h, "_a")
            ev_w_in_t, ev_w_out_f = ev_in[0], ev_out[0]
            group_b, tok = gather_begin([sh_ffn1[1]] + sh_od, w_ffn2[0], "_b")
        hn = _rms_fwd(h, (norm_mix[l] + tok)[None], f"rms_mix{l}")
        if l % 2 == 0:
            u = _mm(hn, ev_w_in_t, "nt", F32, f"mix_in{l}")
            zb, core_vjp = _even_mixer(u, conv_w_full, ev_conv_b, ev_cn_g, ev_cn_b, ev_qn_g, ev_kn_g, str(l))
            w_out = ev_w_out_f
        else:
            u = _mm(hn, od_w_in_t, "nt", F32, f"mix_in{l}")
            zb, core_vjp = odd_mixer(u, l)
            w_out = od_w_out_f
        h_mix = h
        h = _mm(zb, w_out, "nn", F32, f"mix_out{l}", res=h)
        sm = (h_mix, hn, zb, core_vjp)
        if l == 1:
            (w_ffn2[1],) = gather_end(group_c, h, "_c")
        h, s2 = ffn_forward(h, norm_ffn2[l], w_ffn2[l], f"b{l}")
        if l == 0:
            w_ffn1[1], od_in, od_out = gather_end(group_b, h, "_b")
            od_w_in_t, od_w_out_f = od_in[0], od_out[0]
        saved.append((s1, sm, s2))

    dy, loss_part = _loss_grad(h, loss_target[0], "loss_grad")

    def halves_begin(parts, tag):
        parts = [g.reshape(g.shape[0], 4, 2, g.shape[1] // N_DEV, g.shape[2]) for g in parts]
        lands = [lax.empty(g.shape[:2] + g.shape[3:], BF16) for g in parts]
        state = _push_start(parts, lands, _halves_plan, None, "halves_start" + tag)
        return state, state[4][0, 0]

    def chips_begin(state, after, tag):
        send, recv, srcs, lands, _ = state
        parts, got = _push_wait(send, recv, srcs, lands, _halves_plan, after, "halves_wait" + tag)
        sums = [_add_core_halves(g, r, c_idx, f"add_core_halves{tag}_{a}") for a, (g, r) in enumerate(zip(parts, got))]
        lands = [lax.empty((3, s.shape[0]) + s.shape[2:], BF16) for s in sums]
        state = _push_start(sums, lands, _chip_plan, None, "reduce_start" + tag)
        return state, state[4][0, 0]

    def reduce_end(state, after, tag):
        send, recv, srcs, lands, _ = state
        sums, got = _push_wait(send, recv, srcs, lands, _chip_plan, after, "reduce_wait" + tag)
        return [_sum_chip_blocks(s, r, k_idx, f"sum_chip_blocks{tag}_{a}") for a, (s, r) in enumerate(zip(sums, got))]

    def ffn_backward(dy, gain, w, sv, tag, on_dw, on_dx=None):
        h_in, hn, gu = sv
        dxn, dout, t = _ffn_bwd_dx(dy, w, gu, "ffn_bwd_dx_" + tag)
        tok = 0.0 if on_dx is None else on_dx(dxn)
        tok = tok + on_dw(_ffn_bwd_dw(hn, dout, t, "ffn_bwd_dw_" + tag))
        dx, dgain = _rms_bwd(h_in, (gain + tok)[None], dxn, dy, "rms_bwd_" + tag)
        return dx, dgain[0]

    g_norm1, g_norm2, g_normm = [None] * depth, [None] * depth, [None] * depth
    small, halves, groups = [None, None], {}, {}

    def start_halves(key, make_parts):
        def hook(dw):
            halves[key], tok = halves_begin(make_parts(dw), "_" + key)
            return tok
        return hook

    def start_chips(key):
        def hook(after):
            groups[key], tok = chips_begin(halves.pop(key), after, "_" + key)
            return tok
        return hook

    for l in reversed(range(depth)):
        s1, (h_mix, hn, zb, core_vjp), s2 = saved[l]
        if l == 1:
            dy, g_norm2[l] = ffn_backward(dy, norm_ffn2[l], w_ffn2[l], s2, f"b{l}", start_halves("1", lambda dw: [dw]))
        else:
            dy, g_norm2[l] = ffn_backward(dy, norm_ffn2[l], w_ffn2[l], s2, f"b{l}", start_halves("3", lambda dw: [dw]),
                                          start_chips("2"))
        dyb = dy.astype(BF16)
        if l % 2 == 0:
            w_out, w_in_t = ev_w_out_f, ev_w_in_t
        else:
            w_out, w_in_t = od_w_out_f, od_w_in_t
        dz = _mm(dyb, w_out, "nt", F32, f"mix_out_dz{l}")
        dw_out = _mm(zb, dyb, "tn", BF16, f"mix_out_dw{l}")
        dub, small[l % 2] = core_vjp(dz)
        dw_in_t = _mm(dub, hn, "tn", BF16, f"mix_in_dw{l}")
        mix_parts = [dw_in_t[None], dw_out[None]]
        if l == 1:
            tok = start_chips("1")(dw_in_t)
        else:
            tok = start_chips("3")(dw_in_t) + start_halves("4", lambda _: mix_parts)(None)
        dhn = _mm(dub, w_in_t, "nn", F32, f"mix_in_dx{l}")
        dy, gm = _rms_bwd(h_mix, (norm_mix[l] + tok)[None], dhn, dy, f"rms_bwd_mix{l}")
        g_normm[l] = gm[0]
        if l == 1:
            dy, g_norm1[l] = ffn_backward(dy, norm_ffn1[l], w_ffn1[l], s1, f"a{l}",
                                          start_halves("2", lambda dw, od=mix_parts: od + [dw]))
        else:
            dy, g_norm1[l] = ffn_backward(dy, norm_ffn1[l], w_ffn1[l], s1, f"a{l}", start_halves("5", lambda dw: [dw]),
                                          start_chips("4"))
    grad_x = dy[None]
    start_chips("5")(dy)

    done = [dy, groups["5"][4]]
    (g_ffn2_1,) = reduce_end(groups["1"], done, "_1")
    g_od_in_t, g_od_out, g_ffn1_1 = reduce_end(groups["2"], done, "_2")
    (g_ffn2_0,) = reduce_end(groups["3"], done, "_3")
    g_ev_in_t, g_ev_out = reduce_end(groups["4"], done, "_4")
    g_ffn2 = [g_ffn2_0, g_ffn2_1]

    def ffn_grads(gl):
        return (jnp.stack([g[0].T for g in gl]), jnp.stack([g[1].T for g in gl]), jnp.stack([g[2] for g in gl]))

    g_ffn2_wg, g_ffn2_wu, g_ffn2_wd = ffn_grads(g_ffn2)
    grads = [None, None, None, None, None, None, g_ffn2_wg, g_ffn2_wu, g_ffn2_wd,
             g_ev_in_t[0].T[None], None, None, None, None, None,
             None, g_ev_out, g_od_in_t[0].T[None], None, None, g_od_out]
    weights = [norm_ffn1, ffn1_wg, ffn1_wu, ffn1_wd, norm_mix, norm_ffn2, ffn2_wg, ffn2_wu, ffn2_wd, ev_w_in,
               ev_conv_w, ev_conv_b, ev_cn_g, ev_cn_b, ev_qn_g, ev_kn_g, ev_w_out, od_w_in, od_lb_logits,
               od_gn_g, od_w_out]
    moms = [m_norm_ffn1, m_ffn1_wg, m_ffn1_wu, m_ffn1_wd, m_norm_mix, m_norm_ffn2, m_ffn2_wg, m_ffn2_wu,
            m_ffn2_wd, m_ev_w_in, m_ev_conv_w, m_ev_conv_b, m_ev_cn_g, m_ev_cn_b, m_ev_qn_g, m_ev_kn_g,
            m_ev_w_out, m_od_w_in, m_od_lb_logits, m_od_gn_g, m_od_w_out]
    vars_ = [v_norm_ffn1, v_ffn1_wg, v_ffn1_wu, v_ffn1_wd, v_norm_mix, v_norm_ffn2, v_ffn2_wg, v_ffn2_wu,
             v_ffn2_wd, v_ev_w_in, v_ev_conv_w, v_ev_conv_b, v_ev_cn_g, v_ev_cn_b, v_ev_qn_g, v_ev_kn_g,
             v_ev_w_out, v_od_w_in, v_od_lb_logits, v_od_gn_g, v_od_w_out]
    n_w = len(weights)
    deltas, new_m, new_v = [None] * n_w, [None] * n_w, [None] * n_w

    def update(idx):
        for i in idx:
            deltas[i], new_m[i], new_v[i] = _adamw(weights[i], grads[i], moms[i], vars_[i], f"adamw{i}")

    update([i for i in range(n_w) if grads[i] is not None])
    g_conv_w, g_conv_b, g_cn_g, g_cn_b, g_qn_g, g_kn_g = small[0]
    g_lb, g_gn = small[1]
    parts = [jnp.stack(g_norm1), jnp.stack(g_normm), jnp.stack(g_norm2), g_conv_b, g_cn_g, g_cn_b,
             g_qn_g, g_kn_g, g_lb, g_conv_w, g_gn, loss_part[0, :1]]
    red = _unpack_rows(_all_reduce_small(_pack_rows(parts), "reduce_small_grads", [d for d in deltas if d is not None]),
                       [p.shape for p in parts])
    g_norm1, g_normm, g_norm2, g_conv_b, g_cn_g, g_cn_b, g_qn_g, g_kn_g, g_lb, g_conv_w, g_gn, loss = red
    g_conv_w = lax.dynamic_slice(g_conv_w, (0, dev * cs), (cw, cs))
    g_gn = lax.dynamic_slice(g_gn, (dev * gs,), (gs,))
    small_idx = {0: g_norm1, 4: g_normm, 5: g_norm2, 10: g_conv_w[None], 11: g_conv_b[None], 12: g_cn_g[None],
                 13: g_cn_b[None], 14: g_qn_g[None], 15: g_kn_g[None], 18: g_lb, 19: g_gn[None]}
    for i, g in small_idx.items():
        grads[i] = g
    update(small_idx)
    (g_ffn1_0,) = reduce_end(groups["5"], [d for d in deltas if d is not None], "_5")
    grads[1], grads[2], grads[3] = ffn_grads([g_ffn1_0, g_ffn1_1])
    update((1, 2, 3))
    return (loss[0], grad_x, *grads, *deltas, *new_m, *new_v)
```

```python
import functools
import math

import jax
import jax.numpy as jnp
from jax import lax
from jax.experimental import pallas as pl
from jax.experimental.pallas import tpu as pltpu

F32 = jnp.float32
BF16 = jnp.bfloat16
MESH = pl.DeviceIdType.MESH
N_DEV = 8

EPS = 1e-6
HEAD_DIM = 128
CONV_WIDTH = 31
DIL_PATTERNS = ((128, 1), (512, 4), (2048, 16))
Q_BLOCK = 128
ROPE_THETA = 10000.0
HGRN_KDIM = 128
HGRN_CHUNK = 256

ADAM_LR = 0.001
ADAM_B1 = 0.9
ADAM_B2 = 0.999
ADAM_EPS = 1e-08
ADAM_WD = 0.01
ADAM_STEP = 10

VMEM_LIMIT_BYTES = 56 * 1024 * 1024
LANE = 128
SUBLANE_BF16 = 16

ANY = pl.BlockSpec(memory_space=pl.ANY)


def _tile(n, pref, mult):
    t = (min(pref, n) // mult) * mult
    while t > 0:
        if n % t == 0:
            return t
        t -= mult
    return n


def _params(*sem):
    return pltpu.CompilerParams(dimension_semantics=sem, vmem_limit_bytes=VMEM_LIMIT_BYTES)


_DOT_DIMS = {
    "nn": (((1,), (0,)), ((), ())),
    "nt": (((1,), (1,)), ((), ())),
    "tn": (((0,), (0,)), ((), ())),
}


def _dot(a, b, mode):
    return lax.dot_general(a, b, _DOT_DIMS[mode], preferred_element_type=F32)


def _mm(a, b, mode, out_dtype, name, res=None, tm=1024, tn=1024, tk=2048):
    if mode == "nt":
        (M, K), N = a.shape, b.shape[0]
    elif mode == "nn":
        (M, K), N = a.shape, b.shape[1]
    else:
        (K, M), N = a.shape, b.shape[1]
    tm, tn, tk = _tile(M, tm, LANE), _tile(N, tn, LANE), _tile(K, tk, LANE)
    nk = K // tk

    def body(*refs):
        if res is None:
            a_ref, b_ref, o_ref, acc = refs
        else:
            a_ref, b_ref, r_ref, o_ref, acc = refs
        k = pl.program_id(2)

        @pl.when(k == 0)
        def _():
            acc[...] = jnp.zeros_like(acc)

        acc[...] += _dot(a_ref[...].astype(BF16), b_ref[...].astype(BF16), mode)

        @pl.when(k == nk - 1)
        def _():
            r = acc[...]
            if res is not None:
                r = r_ref[...] + r
            o_ref[...] = r.astype(out_dtype)

    a_spec = {"nt": pl.BlockSpec((tm, tk), lambda i, j, k: (i, k)),
              "nn": pl.BlockSpec((tm, tk), lambda i, j, k: (i, k)),
              "tn": pl.BlockSpec((tk, tm), lambda i, j, k: (k, i))}[mode]
    b_spec = {"nt": pl.BlockSpec((tn, tk), lambda i, j, k: (j, k)),
              "nn": pl.BlockSpec((tk, tn), lambda i, j, k: (k, j)),
              "tn": pl.BlockSpec((tk, tn), lambda i, j, k: (k, j))}[mode]
    o_spec = pl.BlockSpec((tm, tn), lambda i, j, k: (i, j))
    in_specs = [a_spec, b_spec] + ([o_spec] if res is not None else [])
    args = (a, b) + ((res,) if res is not None else ())
    return pl.pallas_call(
        body, name=name, grid=(M // tm, N // tn, nk),
        in_specs=in_specs, out_specs=o_spec,
        out_shape=jax.ShapeDtypeStruct((M, N), out_dtype),
        scratch_shapes=[pltpu.VMEM((tm, tn), F32)],
        compiler_params=_params("parallel", "parallel", "arbitrary"),
    )(*args)


def _rms_fwd(x, gain, name):
    S, D = x.shape
    tm = _tile(S, 512, SUBLANE_BF16)

    def body(x_ref, g_ref, o_ref):
        xv = x_ref[...]
        r = lax.rsqrt(jnp.mean(xv * xv, axis=-1, keepdims=True) + EPS)
        o_ref[...] = (xv * r * g_ref[...]).astype(BF16)

    return pl.pallas_call(
        body, name=name, grid=(S // tm,),
        in_specs=[pl.BlockSpec((tm, D), lambda i: (i, 0)), pl.BlockSpec((1, D), lambda i: (0, 0))],
        out_specs=pl.BlockSpec((tm, D), lambda i: (i, 0)),
        out_shape=jax.ShapeDtypeStruct((S, D), BF16),
        compiler_params=_params("parallel"),
    )(x, gain)


def _rms_bwd(x, gain, dxn, dy, name):
    S, D = x.shape
    tm = _tile(S, 512, 8)

    def body(x_ref, g_ref, dxn_ref, dy_ref, dx_ref, dg_ref):
        @pl.when(pl.program_id(0) == 0)
        def _():
            dg_ref[...] = jnp.zeros_like(dg_ref)

        xv = x_ref[...]
        r = lax.rsqrt(jnp.mean(xv * xv, axis=-1, keepdims=True) + EPS)
        xh = xv * r
        dxn_v = dxn_ref[...]
        dg_ref[...] += jnp.sum(dxn_v * xh, axis=0, keepdims=True)
        dxh = dxn_v * g_ref[...]
        dx_ref[...] = dy_ref[...] + r * (dxh - xh * jnp.mean(dxh * xh, axis=-1, keepdims=True))

    row = pl.BlockSpec((tm, D), lambda i: (i, 0))
    vec = pl.BlockSpec((1, D), lambda i: (0, 0))
    return pl.pallas_call(
        body, name=name, grid=(S // tm,),
        in_specs=[row, vec, row, row], out_specs=[row, vec],
        out_shape=[jax.ShapeDtypeStruct((S, D), F32), jax.ShapeDtypeStruct((1, D), F32)],
        compiler_params=_params("arbitrary"),
    )(x, gain, dxn, dy)


def _ffn_fwd(x, xn, w, name):
    S, D = x.shape
    F = w.shape[1]
    tm, tf = _tile(S, 512, SUBLANE_BF16), _tile(F, 512, LANE)
    nf = F // tf

    def body(x_ref, xn_ref, w_ref, o_ref, gu_ref, acc):
        f = pl.program_id(1)

        @pl.when(f == 0)
        def _():
            acc[...] = jnp.zeros_like(acc)

        xnv = xn_ref[...]
        g = _dot(xnv, w_ref[0], "nt")
        u = _dot(xnv, w_ref[1], "nt")
        gu_ref[0] = g.astype(BF16)
        gu_ref[1] = u.astype(BF16)
        h = (g * jax.nn.sigmoid(g) * u).astype(BF16)
        acc[...] += _dot(h, w_ref[2], "nn")

        @pl.when(f == nf - 1)
        def _():
            o_ref[...] = x_ref[...] + 0.5 * acc[...]

    row = pl.BlockSpec((tm, D), lambda i, f: (i, 0))
    return pl.pallas_call(
        body, name=name, grid=(S // tm, nf),
        in_specs=[row, row, pl.BlockSpec((3, tf, D), lambda i, f: (0, f, 0))],
        out_specs=[row, pl.BlockSpec((2, tm, tf), lambda i, f: (0, i, f))],
        out_shape=[jax.ShapeDtypeStruct((S, D), F32), jax.ShapeDtypeStruct((2, S, F), BF16)],
        scratch_shapes=[pltpu.VMEM((tm, D), F32)],
        compiler_params=_params("parallel", "arbitrary"),
    )(x, xn, w)


def _ffn_bwd_dx(dy, w, gu, name):
    S, D = dy.shape
    F = w.shape[1]
    tm, tf = _tile(S, 512, SUBLANE_BF16), _tile(F, 512, LANE)
    nf = F // tf

    def body(dy_ref, w_ref, gu_ref, dxn_ref, dout_ref, t_ref, acc):
        f = pl.program_id(1)

        @pl.when(f == 0)
        def _():
            acc[...] = jnp.zeros_like(acc)
            dout_ref[...] = (0.5 * dy_ref[...]).astype(BF16)

        dh = _dot(dout_ref[...], w_ref[2], "nt")
        g = gu_ref[0].astype(F32)
        u = gu_ref[1].astype(F32)
        sig = jax.nn.sigmoid(g)
        silu = g * sig
        dg = (dh * u * (sig * (1.0 + g * (1.0 - sig)))).astype(BF16)
        du = (dh * silu).astype(BF16)
        t_ref[0] = dg
        t_ref[1] = du
        t_ref[2] = (silu * u).astype(BF16)
        acc[...] += _dot(dg, w_ref[0], "nn") + _dot(du, w_ref[1], "nn")

        @pl.when(f == nf - 1)
        def _():
            dxn_ref[...] = acc[...]

    row = pl.BlockSpec((tm, D), lambda i, f: (i, 0))
    return pl.pallas_call(
        body, name=name, grid=(S // tm, nf),
        in_specs=[row, pl.BlockSpec((3, tf, D), lambda i, f: (0, f, 0)),
                  pl.BlockSpec((2, tm, tf), lambda i, f: (0, i, f))],
        out_specs=[row, row, pl.BlockSpec((3, tm, tf), lambda i, f: (0, i, f))],
        out_shape=[jax.ShapeDtypeStruct((S, D), F32), jax.ShapeDtypeStruct((S, D), BF16),
                   jax.ShapeDtypeStruct((3, S, F), BF16)],
        scratch_shapes=[pltpu.VMEM((tm, D), F32)],
        compiler_params=_params("parallel", "arbitrary"),
    )(dy, w, gu)


def _ffn_bwd_dw(xn, dout, t, name):
    S, D = xn.shape
    F = t.shape[2]
    ts, tf = _tile(S, 1024, LANE), _tile(F, 512, LANE)
    ns = S // ts

    def body(xn_ref, dout_ref, t_ref, dw_ref, acc):
        s = pl.program_id(1)

        @pl.when(s == 0)
        def _():
            acc[...] = jnp.zeros_like(acc)

        xnv = xn_ref[...]
        acc[0] += _dot(t_ref[0], xnv, "tn")
        acc[1] += _dot(t_ref[1], xnv, "tn")
        acc[2] += _dot(t_ref[2], dout_ref[...], "tn")

        @pl.when(s == ns - 1)
        def _():
            dw_ref[...] = acc[...].astype(BF16)

    row = pl.BlockSpec((ts, D), lambda f, s: (s, 0))
    return pl.pallas_call(
        body, name=name, grid=(F // tf, ns),
        in_specs=[row, row, pl.BlockSpec((3, ts, tf), lambda f, s: (0, s, f))],
        out_specs=pl.BlockSpec((3, tf, D), lambda f, s: (0, f, 0)),
        out_shape=jax.ShapeDtypeStruct((3, F, D), BF16),
        scratch_shapes=[pltpu.VMEM((3, tf, D), F32)],
        compiler_params=_params("parallel", "arbitrary"),
    )(xn, dout, t)


def _loss_grad(y, target, name):
    S, D = y.shape
    tm = _tile(S, 512, 8)

    def body(y_ref, t_ref, dy_ref, l_ref):
        @pl.when(pl.program_id(0) == 0)
        def _():
            l_ref[...] = jnp.zeros_like(l_ref)

        e = y_ref[...] - t_ref[...]
        dy_ref[...] = e * (1.0 / D)
        l_ref[...] += 0.5 * jnp.sum(jnp.sum(e * e, axis=-1, keepdims=True) * (1.0 / D))

    row = pl.BlockSpec((tm, D), lambda i: (i, 0))
    one = pl.BlockSpec((8, LANE), lambda i: (0, 0))
    return pl.pallas_call(
        body, name=name, grid=(S // tm,),
        in_specs=[row, row], out_specs=[row, one],
        out_shape=[jax.ShapeDtypeStruct((S, D), F32), jax.ShapeDtypeStruct((8, LANE), F32)],
        compiler_params=_params("arbitrary"),
    )(y, target)


def _adamw(w, g, m, v, name):
    shape = w.shape
    C = shape[-1]
    R = math.prod(shape[:-1])
    tr = _tile(R, max(8, (1 << 19) // C // 8 * 8), 8)
    c1 = 1.0 / (1.0 - ADAM_B1 ** ADAM_STEP)
    c2 = 1.0 / (1.0 - ADAM_B2 ** ADAM_STEP)

    def body(w_ref, g_ref, m_ref, v_ref, d_ref, nm_ref, nv_ref):
        gv = g_ref[...]
        nm = ADAM_B1 * m_ref[...] + (1.0 - ADAM_B1) * gv
        nv = ADAM_B2 * v_ref[...] + (1.0 - ADAM_B2) * (gv * gv)
        nm_ref[...] = nm
        nv_ref[...] = nv
        d_ref[...] = -ADAM_LR * ((nm * c1) / (jnp.sqrt(nv * c2) + ADAM_EPS) + ADAM_WD * w_ref[...])

    blk = pl.BlockSpec((tr, C), lambda i: (i, 0))
    sds = jax.ShapeDtypeStruct((R, C), F32)
    outs = pl.pallas_call(
        body, name=name, grid=(R // tr,),
        in_specs=[blk] * 4, out_specs=[blk] * 3, out_shape=[sds] * 3,
        compiler_params=_params("parallel"),
    )(*(a.reshape(R, C) for a in (w, g, m, v)))
    return tuple(o.reshape(shape) for o in outs)


def _me():
    return lax.axis_index("x"), lax.axis_index("y"), lax.axis_index("c")


def _add_core_halves(grad, got, c_idx, name):
    n, nk, _, r, C = grad.shape
    tr = _tile(r, 1024, SUBLANE_BF16)

    def body(c_ref, g_ref, r_ref, o_ref):
        o_ref[...] = (g_ref[...].astype(F32) + r_ref[...].astype(F32)).astype(BF16)

    return pl.pallas_call(
        body, name=name,
        grid_spec=pltpu.PrefetchScalarGridSpec(
            num_scalar_prefetch=1, grid=(n, nk, r // tr),
            in_specs=[pl.BlockSpec((None, None, None, tr, C), lambda i, k, t, c: (i, k, c[0], t, 0)),
                      pl.BlockSpec((None, None, tr, C), lambda i, k, t, c: (i, k, t, 0))],
            out_specs=pl.BlockSpec((None, None, tr, C), lambda i, k, t, c: (i, k, t, 0))),
        out_shape=jax.ShapeDtypeStruct((n, nk, r, C), BF16),
        compiler_params=_params("parallel", "parallel", "parallel"),
    )(c_idx, grad, got)


HBM = pl.BlockSpec(memory_space=pltpu.HBM)
SEM = pl.BlockSpec(memory_space=pltpu.SEMAPHORE)
EFFECT = pltpu.SideEffectType.DATAFLOW_SIDE_EFFECTING


def _push_start(srcs, lands, plan, after, name):
    ns, nl = len(srcs), len(lands)
    ncp = len(plan([None] * ns, [None] * nl, dry=True))
    extra = [] if after is None else [after]

    def body(*refs):
        src_refs, land_refs = refs[:ns], refs[ns:ns + nl]
        send_sems, recv_sems = refs[ns + nl + len(extra)], refs[ns + nl + len(extra) + 1]
        token = refs[-1]
        for i, (s, d, to) in enumerate(plan(src_refs, land_refs)):
            pltpu.make_async_remote_copy(src_ref=s, dst_ref=d, send_sem=send_sems.at[i], recv_sem=recv_sems.at[i],
                                         device_id=to, device_id_type=MESH).start()
        token[...] = jnp.zeros_like(token)

    out = pl.pallas_call(
        body, name=name,
        out_shape=(pltpu.SemaphoreType.DMA((ncp,)), pltpu.SemaphoreType.DMA((ncp,)),
                   *[pltpu.HBM(a.shape, a.dtype) for a in srcs], *[pltpu.HBM(a.shape, a.dtype) for a in lands],
                   jax.ShapeDtypeStruct((8, LANE), F32)),
        in_specs=[HBM] * (ns + nl) + [ANY] * len(extra),
        out_specs=(SEM, SEM, *[HBM] * (ns + nl), pl.BlockSpec(memory_space=pltpu.VMEM)),
        input_output_aliases={i: 2 + i for i in range(ns + nl)},
        compiler_params=pltpu.CompilerParams(has_side_effects=EFFECT),
    )(*[pltpu.with_memory_space_constraint(a, pltpu.HBM) for a in srcs + lands], *extra)
    return out[0], out[1], list(out[2:2 + ns]), list(out[2 + ns:2 + ns + nl]), out[-1]


def _push_wait(send_sems, recv_sems, srcs, lands, plan, after, name):
    ns, nl = len(srcs), len(lands)
    after = list(after) if isinstance(after, (list, tuple)) else [after]

    def body(*refs):
        src_refs, land_refs = refs[:ns], refs[ns:ns + nl]
        send, recv = refs[ns + nl], refs[ns + nl + 1]
        for i, (s, d, to) in enumerate(plan(src_refs, land_refs)):
            cp = pltpu.make_async_remote_copy(src_ref=s, dst_ref=d, send_sem=send.at[i], recv_sem=recv.at[i],
                                              device_id=to, device_id_type=MESH)
            cp.wait_send()
            cp.wait_recv()

    out = pl.pallas_call(
        body, name=name,
        out_shape=tuple(pltpu.HBM(a.shape, a.dtype) for a in srcs + lands),
        in_specs=[HBM] * (ns + nl) + [SEM, SEM] + [ANY] * len(after),
        out_specs=tuple([HBM] * (ns + nl)),
        input_output_aliases={i: i for i in range(ns + nl)},
        compiler_params=pltpu.CompilerParams(has_side_effects=EFFECT),
    )(*srcs, *lands, send_sems, recv_sems, *after)
    return list(out[:ns]), list(out[ns:])


def _gather_plan(src_refs, land_refs, dry=False):
    if dry:
        return [None] * (4 * len(src_refs))
    x, y, c = _me()
    me = 4 * x + 2 * y + c
    targets = [(x, y, 1 - c), (1 - x, y, c), (x, 1 - y, c), (1 - x, 1 - y, c)]
    return [(s, l.at[:, me], to) for s, l in zip(src_refs, land_refs) for to in targets]


def _halves_plan(src_refs, land_refs, dry=False):
    if dry:
        return [None] * len(src_refs)
    x, y, c = _me()
    return [(s.at[:, :, 1 - c], l, (x, y, 1 - c)) for s, l in zip(src_refs, land_refs)]


def _chip_plan(src_refs, land_refs, dry=False):
    if dry:
        return [None] * (3 * len(src_refs))
    x, y, c = _me()
    chips = [(1 - x, y), (x, 1 - y), (1 - x, 1 - y)]
    return [(s.at[:, 2 * chip[0] + chip[1]], l.at[j], (*chip, c))
            for s, l in zip(src_refs, land_refs) for j, chip in enumerate(chips)]


def _gather_forward(lands, name):
    na = len(lands)

    def body(*refs):
        bufs = refs[na:2 * na]
        send_sems, recv_sems = refs[2 * na:]
        x, y, c = _me()
        chips = [(1 - x, y), (x, 1 - y), (1 - x, 1 - y)]

        def copy(a, j, pc):
            blk = bufs[a].at[:, 4 * chips[j][0] + 2 * chips[j][1] + pc]
            return pltpu.make_async_remote_copy(
                src_ref=blk, dst_ref=blk, send_sem=send_sems.at[3 * a + j], recv_sem=recv_sems.at[3 * a + j],
                device_id=(x, y, 1 - c), device_id_type=MESH)

        pairs = [(a, j) for a in range(na) for j in range(3)]
        for a, j in pairs:
            copy(a, j, c).start()
        for a, j in pairs:
            copy(a, j, 1 - c).wait_recv()
        for a, j in pairs:
            copy(a, j, c).wait_send()

    return pl.pallas_call(
        body, name=name,
        in_specs=[ANY] * na, out_specs=[ANY] * na,
        out_shape=[jax.ShapeDtypeStruct(a.shape, a.dtype) for a in lands],
        input_output_aliases={a: a for a in range(na)},
        scratch_shapes=[pltpu.SemaphoreType.DMA((3 * na,)), pltpu.SemaphoreType.DMA((3 * na,))],
    )(*lands)


def _sum_chip_blocks(sums, got, k_idx, name):
    n, _, r, C = sums.shape
    tr = _tile(r, 512, SUBLANE_BF16)

    def body(k_ref, s_ref, r_ref, o_ref):
        acc = s_ref[...].astype(F32)
        for j in range(3):
            acc = acc + r_ref[j].astype(F32)
        o_ref[...] = acc

    return pl.pallas_call(
        body, name=name,
        grid_spec=pltpu.PrefetchScalarGridSpec(
            num_scalar_prefetch=1, grid=(n, r // tr),
            in_specs=[pl.BlockSpec((None, None, tr, C), lambda i, t, k: (i, k[0], t, 0)),
                      pl.BlockSpec((3, None, tr, C), lambda i, t, k: (0, i, t, 0))],
            out_specs=pl.BlockSpec((None, tr, C), lambda i, t, k: (i, t, 0))),
        out_shape=jax.ShapeDtypeStruct((n, r, C), F32),
        compiler_params=_params("parallel", "parallel"),
    )(k_idx, sums, got)


def _all_reduce_small(v, name, after=()):
    R = v.shape[0]
    after = list(after)

    def body(*refs):
        v_ref = refs[0]
        o_ref, buf, send_sems, recv_sems = refs[1 + len(after):]
        x, y, c = _me()
        me = 4 * x + 2 * y + c
        buf[me] = v_ref[...]
        copies = []
        for k in range(1, N_DEV):
            peer = (x ^ (k >> 2), y ^ ((k >> 1) & 1), c ^ (k & 1))
            copies.append(pltpu.make_async_remote_copy(
                src_ref=v_ref, dst_ref=buf.at[me],
                send_sem=send_sems.at[k - 1], recv_sem=recv_sems.at[k - 1],
                device_id=peer, device_id_type=MESH))
        for cp in copies:
            cp.start()
        for cp in copies:
            cp.wait()
        acc = buf[0]
        for d in range(1, N_DEV):
            acc = acc + buf[d]
        o_ref[...] = acc

    vm = pl.BlockSpec(memory_space=pltpu.VMEM)
    return pl.pallas_call(
        body, name=name, in_specs=[vm] + [ANY] * len(after), out_specs=vm,
        out_shape=jax.ShapeDtypeStruct((R, LANE), F32),
        scratch_shapes=[pltpu.VMEM((N_DEV, R, LANE), F32),
                        pltpu.SemaphoreType.DMA((N_DEV - 1,)), pltpu.SemaphoreType.DMA((N_DEV - 1,))],
        compiler_params=pltpu.CompilerParams(vmem_limit_bytes=VMEM_LIMIT_BYTES),
    )(v, *after)


def _pack_rows(parts):
    flat = jnp.concatenate([p.reshape(-1).astype(F32) for p in parts])
    n = flat.shape[0]
    rows = -(-n // (8 * LANE)) * 8
    flat = jnp.pad(flat, (0, rows * LANE - n))
    return flat.reshape(rows, LANE)


def _unpack_rows(packed, shapes):
    flat = packed.reshape(-1)
    out, off = [], 0
    for s in shapes:
        n = math.prod(s)
        out.append(flat[off:off + n].reshape(s))
        off += n
    return out


CONV_HALO = 32


def _conv_fwd(u, conv_w, conv_b, cn_g, cn_b, name):
    S = u.shape[0]
    C = conv_w.shape[1]
    T = _tile(S, 256, CONV_HALO)
    hb = T // CONV_HALO

    def body(av_ref, ag_ref, pv_ref, pg_ref, w_ref, b_ref, g_ref, bb_ref, out_ref, y_ref, scr):
        i = pl.program_id(0)
        prev = pv_ref[...] * jax.nn.sigmoid(pg_ref[...])
        scr[0:CONV_HALO, :] = jnp.where(i > 0, prev, 0.0)
        scr[CONV_HALO:CONV_HALO + T, :] = av_ref[...] * jax.nn.sigmoid(ag_ref[...])
        for s in range(C // LANE):
            sl = slice(s * LANE, (s + 1) * LANE)
            acc = jnp.broadcast_to(b_ref[:, sl], (T, LANE))
            for j in range(CONV_WIDTH):
                acc = acc + w_ref[j:j + 1, sl] * scr[pl.ds(CONV_HALO - (CONV_WIDTH - 1) + j, T), sl]
            y_ref[:, sl] = acc
        acc = y_ref[...]
        mu = jnp.mean(acc, axis=-1, keepdims=True)
        xc = acc - mu
        var = jnp.mean(xc * xc, axis=-1, keepdims=True)
        ln = xc * lax.rsqrt(var + EPS) * g_ref[...] + bb_ref[...]
        out_ref[...] = (ln * jax.nn.sigmoid(ln)).astype(BF16)

    def cur(cb):
        return pl.BlockSpec((T, C), lambda i: (i, cb))

    def halo(cb):
        return pl.BlockSpec((CONV_HALO, C), lambda i: (jnp.maximum(i * hb - 1, 0), cb))

    vec = pl.BlockSpec((1, C), lambda i: (0, 0))
    return pl.pallas_call(
        body, name=name, grid=(S // T,),
        in_specs=[cur(0), cur(1), halo(0), halo(1), pl.BlockSpec((CONV_WIDTH, C), lambda i: (0, 0)), vec, vec, vec],
        out_specs=[pl.BlockSpec((T, C), lambda i: (i, 0))] * 2,
        out_shape=[jax.ShapeDtypeStruct((S, C), BF16), jax.ShapeDtypeStruct((S, C), F32)],
        scratch_shapes=[pltpu.VMEM((T + CONV_HALO, C), F32)],
        compiler_params=_params("parallel"),
    )(u, u, u, u, conv_w, conv_b, cn_g, cn_b)


def _conv_bwd_norm(dz, y, cn_g, cn_b, name):
    S, C = y.shape
    T = _tile(S, 256, 8)

    def body(dz_ref, y_ref, g_ref, bb_ref, dy_ref, dg_ref, db_ref):
        @pl.when(pl.program_id(0) == 0)
        def _():
            dg_ref[...] = jnp.zeros_like(dg_ref)
            db_ref[...] = jnp.zeros_like(db_ref)

        yv = y_ref[...]
        mu = jnp.mean(yv, axis=-1, keepdims=True)
        xc = yv - mu
        rstd = lax.rsqrt(jnp.mean(xc * xc, axis=-1, keepdims=True) + EPS)
        xh = xc * rstd
        ln = xh * g_ref[...] + bb_ref[...]
        sg = jax.nn.sigmoid(ln)
        dln = dz_ref[...] * (sg * (1.0 + ln * (1.0 - sg)))
        dg_ref[...] += jnp.sum(dln * xh, axis=0, keepdims=True)
        db_ref[...] += jnp.sum(dln, axis=0, keepdims=True)
        dxh = dln * g_ref[...]
        dy_ref[...] = rstd * (dxh - jnp.mean(dxh, axis=-1, keepdims=True)
                              - xh * jnp.mean(dxh * xh, axis=-1, keepdims=True))

    row = pl.BlockSpec((T, C), lambda i: (i, 0))
    vec = pl.BlockSpec((1, C), lambda i: (0, 0))
    return pl.pallas_call(
        body, name=name, grid=(S // T,),
        in_specs=[row, row, vec, vec], out_specs=[row, vec, vec],
        out_shape=[jax.ShapeDtypeStruct((S, C), F32), jax.ShapeDtypeStruct((1, C), F32),
                   jax.ShapeDtypeStruct((1, C), F32)],
        compiler_params=_params("arbitrary"),
    )(dz, y, cn_g, cn_b)


def _conv_bwd_taps(u, dy, conv_w, name):
    S, C = dy.shape
    T = _tile(S, 256, CONV_HALO)
    hb = T // CONV_HALO
    nt = S // T
    ns = C // LANE
    W1 = CONV_WIDTH - 1

    def body(av_ref, ag_ref, pv_ref, pg_ref, dy_ref, dn_ref, w_ref, dv_ref, dg_ref, dw_ref, db_ref, a_scr, d_scr):
        i = pl.program_id(1)

        @pl.when(i == 0)
        def _():
            dw_ref[...] = jnp.zeros_like(dw_ref)
            db_ref[...] = jnp.zeros_like(db_ref)

        av, sg = av_ref[...], jax.nn.sigmoid(ag_ref[...])
        prev = pv_ref[...] * jax.nn.sigmoid(pg_ref[...])
        a_scr[0:CONV_HALO, :] = jnp.where(i > 0, prev, 0.0)
        a_scr[CONV_HALO:CONV_HALO + T, :] = av * sg
        dyv = dy_ref[...]
        d_scr[0:T, :] = dyv
        d_scr[T:T + CONV_HALO, :] = jnp.where(i < nt - 1, dn_ref[...], 0.0)
        da = jnp.zeros((T, LANE), F32)
        for j in range(CONV_WIDTH):
            da = da + w_ref[j:j + 1, :] * d_scr[pl.ds(W1 - j, T), :]
            dw_ref[j:j + 1, :] += jnp.sum(dyv * a_scr[pl.ds(CONV_HALO - W1 + j, T), :], axis=0, keepdims=True)
        db_ref[...] += jnp.sum(dyv, axis=0, keepdims=True)
        dv_ref[...] = (da * sg).astype(BF16)
        dg_ref[...] = (da * av * sg * (1.0 - sg)).astype(BF16)

    def cur(part):
        return pl.BlockSpec((T, LANE), lambda cb, i: (i, part * ns + cb))

    def halo(part):
        return pl.BlockSpec((CONV_HALO, LANE), lambda cb, i: (jnp.maximum(i * hb - 1, 0), part * ns + cb))

    nxt = pl.BlockSpec((CONV_HALO, LANE), lambda cb, i: (jnp.minimum((i + 1) * hb, S // CONV_HALO - 1), cb))
    row = pl.BlockSpec((T, LANE), lambda cb, i: (i, cb))
    return pl.pallas_call(
        body, name=name, grid=(ns, nt),
        in_specs=[cur(0), cur(1), halo(0), halo(1), row, nxt, pl.BlockSpec((CONV_WIDTH, LANE), lambda cb, i: (0, cb))],
        out_specs=[row, row, pl.BlockSpec((CONV_HALO, LANE), lambda cb, i: (0, cb)),
                   pl.BlockSpec((1, LANE), lambda cb, i: (0, cb))],
        out_shape=[jax.ShapeDtypeStruct((S, C), BF16), jax.ShapeDtypeStruct((S, C), BF16),
                   jax.ShapeDtypeStruct((CONV_HALO, C), F32), jax.ShapeDtypeStruct((1, C), F32)],
        scratch_shapes=[pltpu.VMEM((T + CONV_HALO, LANE), F32), pltpu.VMEM((T + CONV_HALO, LANE), F32)],
        compiler_params=_params("parallel", "arbitrary"),
    )(u, u, u, u, dy, dy, conv_w)


def _rope_tables(S):
    half = HEAD_DIM // 2
    inv = jnp.exp(-math.log(ROPE_THETA) * jnp.arange(half, dtype=F32) / half)
    ang = jnp.arange(S, dtype=jnp.int32).astype(F32)[:, None] * inv[None, :]
    cos, sin = jnp.cos(ang), jnp.sin(ang)
    return jnp.concatenate([cos, cos], axis=1), jnp.concatenate([-sin, sin], axis=1)


def _qkv_prep(u, qn_g, kn_g, cos, sin, cb0, name):
    S = u.shape[0]
    A = (u.shape[1] // (cb0 + 3))
    H = A // HEAD_DIM
    T = _tile(S, 256, SUBLANE_BF16)
    scale = HEAD_DIM ** -0.5

    def body(q_ref, k_ref, v_ref, qg_ref, kg_ref, cos_ref, sin_ref, qo_ref, ko_ref, vo_ref):
        cosv, sinv = cos_ref[...], sin_ref[...]
        for h in range(H):
            sl = slice(h * HEAD_DIM, (h + 1) * HEAD_DIM)
            for x_ref, g_ref, o_ref, sc in ((q_ref, qg_ref, qo_ref, scale), (k_ref, kg_ref, ko_ref, 1.0)):
                xv = x_ref[:, sl]
                xn = xv * lax.rsqrt(jnp.mean(xv * xv, axis=-1, keepdims=True) + EPS) * g_ref[...]
                y = xn * cosv + pltpu.roll(xn, HEAD_DIM // 2, 1) * sinv
                o_ref[:, sl] = (y * sc).astype(BF16)
        vo_ref[...] = v_ref[...].astype(BF16)

    def col(cb):
        return pl.BlockSpec((T, A), lambda i: (i, cb))

    vec = pl.BlockSpec((1, HEAD_DIM), lambda i: (0, 0))
    tab = pl.BlockSpec((T, HEAD_DIM), lambda i: (i, 0))
    out = pl.BlockSpec((T, A), lambda i: (i, 0))
    return pl.pallas_call(
        body, name=name, grid=(S // T,),
        in_specs=[col(cb0), col(cb0 + 1), col(cb0 + 2), vec, vec, tab, tab],
        out_specs=[out] * 3, out_shape=[jax.ShapeDtypeStruct((S, A), BF16)] * 3,
        compiler_params=_params("parallel"),
    )(u, u, u, qn_g, kn_g, cos, sin)


def _qkv_prep_bwd(u, dqs, dks, dvs, qn_g, kn_g, cos, sin, cb0, name):
    S = u.shape[0]
    A = dqs[0].shape[1]
    H = A // HEAD_DIM
    T = _tile(S, 256, SUBLANE_BF16)
    nb = len(dqs)
    scale = HEAD_DIM ** -0.5

    def body(*refs):
        q_ref, k_ref, qg_ref, kg_ref, cos_ref, sin_ref = refs[:6]
        dq_refs, dk_refs, dv_refs = refs[6:6 + nb], refs[6 + nb:6 + 2 * nb], refs[6 + 2 * nb:6 + 3 * nb]
        dqo_ref, dko_ref, dvo_ref, dqg_ref, dkg_ref = refs[6 + 3 * nb:]

        @pl.when(pl.program_id(0) == 0)
        def _():
            dqg_ref[...] = jnp.zeros_like(dqg_ref)
            dkg_ref[...] = jnp.zeros_like(dkg_ref)

        cosv, sinv = cos_ref[...], sin_ref[...]
        for h in range(H):
            sl = slice(h * HEAD_DIM, (h + 1) * HEAD_DIM)
            for x_ref, g_ref, d_refs, o_ref, dg_ref, sc in ((q_ref, qg_ref, dq_refs, dqo_ref, dqg_ref, scale),
                                                          (k_ref, kg_ref, dk_refs, dko_ref, dkg_ref, 1.0)):
                dy = d_refs[0][:, sl]
                for r in d_refs[1:]:
                    dy = dy + r[:, sl]
                dy = dy * sc
                dxn = dy * cosv + pltpu.roll(dy * sinv, HEAD_DIM // 2, 1)
                xv = x_ref[:, sl]
                r = lax.rsqrt(jnp.mean(xv * xv, axis=-1, keepdims=True) + EPS)
                xh = xv * r
                dg_ref[...] += jnp.sum(dxn * xh, axis=0, keepdims=True)
                dxh = dxn * g_ref[...]
                o_ref[:, sl] = (r * (dxh - xh * jnp.mean(dxh * xh, axis=-1, keepdims=True))).astype(BF16)
        dv = dv_refs[0][...]
        for r in dv_refs[1:]:
            dv = dv + r[...]
        dvo_ref[...] = dv.astype(BF16)

    def col(cb):
        return pl.BlockSpec((T, A), lambda i: (i, cb))

    vec = pl.BlockSpec((1, HEAD_DIM), lambda i: (0, 0))
    tab = pl.BlockSpec((T, HEAD_DIM), lambda i: (i, 0))
    row = pl.BlockSpec((T, A), lambda i: (i, 0))
    return pl.pallas_call(
        body, name=name, grid=(S // T,),
        in_specs=[col(cb0), col(cb0 + 1), vec, vec, tab, tab] + [row] * (3 * nb),
        out_specs=[row, row, row, vec, vec],
        out_shape=[jax.ShapeDtypeStruct((S, A), BF16)] * 3 + [jax.ShapeDtypeStruct((1, HEAD_DIM), F32)] * 2,
        compiler_params=_params("arbitrary"),
    )(u, u, qn_g, kn_g, cos, sin, *dqs, *dks, *dvs)


ATT_TILE = 256
NEG = -1e30


def _attn_bias(tile):
    span = max(window for window, _ in DIL_PATTERNS)
    nw = -(-span // tile) + 1
    dist = (jnp.arange(nw)[:, None, None] * tile + jnp.arange(tile)[None, :, None] - jnp.arange(tile)[None, None, :])
    mult = sum(((dist >= 0) & (dist <= window) & (dist % dil == 0)).astype(F32) for window, dil in DIL_PATTERNS)
    return jnp.where(mult > 0, jnp.log(jnp.maximum(mult, 1.0)), NEG)


def _attn_fwd(q, k, v, bias, name):
    S, A = q.shape
    H = A // HEAD_DIM
    nw, T, _ = bias.shape
    nq = S // T

    def body(q_ref, k_ref, v_ref, b_ref, ob_ref, of_ref, l_ref, s_scr):
        i = pl.program_id(1)
        qv = q_ref[...]
        mx = jnp.full((T, 1), NEG, F32)
        for w in range(nw):
            blk = i - w
            start = pl.multiple_of(jnp.maximum(blk, 0) * T, T)
            s = _dot(qv, k_ref[pl.ds(start, T), :], "nt") + b_ref[w] + jnp.where(blk >= 0, 0.0, NEG)
            s_scr[w] = s
            mx = jnp.maximum(mx, jnp.max(s, axis=-1, keepdims=True))
        den = jnp.zeros((T, 1), F32)
        o = jnp.zeros((T, HEAD_DIM), F32)
        for w in range(nw):
            start = pl.multiple_of(jnp.maximum(i - w, 0) * T, T)
            p = jnp.exp(s_scr[w] - mx)
            den = den + jnp.sum(p, axis=-1, keepdims=True)
            o = o + _dot(p.astype(BF16), v_ref[pl.ds(start, T), :], "nn")
        o = o / den
        ob_ref[...] = o.astype(BF16)
        of_ref[...] = o
        l_ref[...] = mx + jnp.log(den)

    blk = pl.BlockSpec((T, HEAD_DIM), lambda h, i: (i, h))
    full = pl.BlockSpec((S, HEAD_DIM), lambda h, i: (0, h))
    return pl.pallas_call(
        body, name=name, grid=(H, nq),
        in_specs=[blk, full, full, pl.BlockSpec((nw, T, T), lambda h, i: (0, 0, 0))],
        out_specs=[blk, blk, pl.BlockSpec((None, T, 1), lambda h, i: (h, i, 0))],
        out_shape=[jax.ShapeDtypeStruct((S, A), BF16), jax.ShapeDtypeStruct((S, A), F32),
                   jax.ShapeDtypeStruct((H, S, 1), F32)],
        scratch_shapes=[pltpu.VMEM((nw, T, T), F32)],
        compiler_params=_params("parallel", "arbitrary"),
    )(q, k, v, bias)


def _attn_dq(q, k, v, dz, cb0, o, lse, bias, name):
    S, A = q.shape
    H = A // HEAD_DIM
    nw, T, _ = bias.shape
    nq = S // T

    def body(q_ref, k_ref, v_ref, do_ref, o_ref, l_ref, b_ref, dq_ref, d_ref):
        i = pl.program_id(1)
        qv, dof = q_ref[...], do_ref[...]
        dov = dof.astype(BF16)
        delta = jnp.sum(dof * o_ref[...], axis=-1, keepdims=True)
        d_ref[...] = delta
        lv = l_ref[...]
        dq = jnp.zeros((T, HEAD_DIM), F32)
        for w in range(nw):
            blk = i - w
            start = pl.multiple_of(jnp.maximum(blk, 0) * T, T)
            kv = k_ref[pl.ds(start, T), :]
            s = _dot(qv, kv, "nt") + b_ref[w] + jnp.where(blk >= 0, 0.0, NEG)
            p = jnp.exp(s - lv)
            ds = (p * (_dot(dov, v_ref[pl.ds(start, T), :], "nt") - delta)).astype(BF16)
            dq = dq + _dot(ds, kv, "nn")
        dq_ref[...] = dq

    blk = pl.BlockSpec((T, HEAD_DIM), lambda h, i: (i, h))
    full = pl.BlockSpec((S, HEAD_DIM), lambda h, i: (0, h))
    col = pl.BlockSpec((None, T, 1), lambda h, i: (h, i, 0))
    return pl.pallas_call(
        body, name=name, grid=(H, nq),
        in_specs=[blk, full, full, pl.BlockSpec((T, HEAD_DIM), lambda h, i: (i, cb0 + h)), blk, col,
                  pl.BlockSpec((nw, T, T), lambda h, i: (0, 0, 0))],
        out_specs=[blk, col],
        out_shape=[jax.ShapeDtypeStruct((S, A), F32), jax.ShapeDtypeStruct((H, S, 1), F32)],
        compiler_params=_params("parallel", "arbitrary"),
    )(q, k, v, dz, o, lse, bias)


def _attn_dkv(q, k, v, dz, cb0, lse, delta, bias, name):
    S, A = q.shape
    H = A // HEAD_DIM
    nw, T, _ = bias.shape
    nq = S // T

    def body(k_ref, v_ref, q_ref, do_ref, l_ref, d_ref, b_ref, dk_ref, dv_ref):
        m = pl.program_id(1)
        kv, vv = k_ref[...], v_ref[...]
        dk = jnp.zeros((T, HEAD_DIM), F32)
        dv = jnp.zeros((T, HEAD_DIM), F32)
        for w in range(nw):
            blk = m + w
            start = pl.multiple_of(jnp.minimum(blk, nq - 1) * T, T)
            qv = q_ref[pl.ds(start, T), :]
            dov = do_ref[pl.ds(start, T), :].astype(BF16)
            s = _dot(qv, kv, "nt") + b_ref[w] + jnp.where(blk < nq, 0.0, NEG)
            p = jnp.exp(s - l_ref[pl.ds(start, T), :])
            dv = dv + _dot(p.astype(BF16), dov, "tn")
            ds = (p * (_dot(dov, vv, "nt") - d_ref[pl.ds(start, T), :])).astype(BF16)
            dk = dk + _dot(ds, qv, "tn")
        dk_ref[...] = dk
        dv_ref[...] = dv

    blk = pl.BlockSpec((T, HEAD_DIM), lambda h, m: (m, h))
    full = pl.BlockSpec((S, HEAD_DIM), lambda h, m: (0, h))
    col = pl.BlockSpec((None, S, 1), lambda h, m: (h, 0, 0))
    sds = jax.ShapeDtypeStruct((S, A), F32)
    return pl.pallas_call(
        body, name=name, grid=(H, nq),
        in_specs=[blk, blk, full, pl.BlockSpec((S, HEAD_DIM), lambda h, m: (0, cb0 + h)), col, col,
                  pl.BlockSpec((nw, T, T), lambda h, m: (0, 0, 0))],
        out_specs=[blk, blk], out_shape=[sds, sds],
        compiler_params=_params("parallel", "arbitrary"),
    )(k, v, q, dz, lse, delta, bias)


def _even_mixer(u, conv_w, conv_b, cn_g, cn_b, qn_g, kn_g, tag):
    S = u.shape[0]
    C = conv_w.shape[1]
    A = (u.shape[1] - 2 * C) // 3
    assert A == C, "column-block addressing of u assumes equal conv and attention widths"
    T = _tile(S, ATT_TILE, LANE)
    cos, sin = _rope_tables(S)
    bias = _attn_bias(T)
    a_out, y = _conv_fwd(u, conv_w, conv_b, cn_g, cn_b, "conv_fwd" + tag)
    q, k, v = _qkv_prep(u, qn_g, kn_g, cos, sin, 2, "qkv_prep" + tag)
    ob, of, lse = _attn_fwd(q, k, v, bias, "attn_fwd" + tag)
    z = jnp.concatenate([a_out, ob], axis=1)

    def backward(dz):
        dy, d_cn_g, d_cn_b = _conv_bwd_norm(dz, y, cn_g, cn_b, "conv_bwd_norm" + tag)
        d_val, d_gate, d_w, d_b = _conv_bwd_taps(u, dy, conv_w, "conv_bwd_taps" + tag)
        dqp, delta = _attn_dq(q, k, v, dz, C // HEAD_DIM, of, lse, bias, "attn_dq" + tag)
        dkp, dvp = _attn_dkv(q, k, v, dz, C // HEAD_DIM, lse, delta, bias, "attn_dkv" + tag)
        dq, dk, dv, d_qn, d_kn = _qkv_prep_bwd(u, [dqp], [dkp], [dvp], qn_g, kn_g, cos, sin, 2, "qkv_prep_bwd" + tag)
        du = jnp.concatenate([d_val, d_gate, dq, dk, dv], axis=1)
        return du, [d_w[:CONV_WIDTH], d_b[0], d_cn_g[0], d_cn_b[0], d_qn[0], d_kn[0]]

    return z, backward


_LEVELS = (128, 64, 32, 16, 8, 4, 2, 1)


def _chunk_cumsum(g, rows, reverse=False):
    C = g.shape[0]
    d = 1
    while d < C:
        if reverse:
            g = g + jnp.where(rows < C - d, pltpu.roll(g, C - d, 0), 0.0)
        else:
            g = g + jnp.where(rows >= d, pltpu.roll(g, d, 0), 0.0)
        d *= 2
    return g


def _level_ref(b, b_scr, rows, m):
    C = b.shape[0]
    if m >= 8:
        pieces = [jnp.broadcast_to(b_scr[2 * m * j + m - 1:2 * m * j + m, :], (2 * m, LANE)) for j in range(C // (2 * m))]
        return pieces[0] if len(pieces) == 1 else jnp.concatenate(pieces, axis=0)
    pos = rows & (2 * m - 1)
    ref = b
    for p in range(2 * m):
        if p != m - 1:
            ref = jnp.where(pos == p, pltpu.roll(b, (p - (m - 1)) % C, 0), ref)
    return ref


def _level_operands(q, k, b, b_scr, rows, m):
    ref = _level_ref(b, b_scr, rows, m)
    qs = (q * jnp.exp(jnp.minimum(b - ref, 0.0))).astype(BF16)
    ks = (k * jnp.exp(jnp.minimum(ref - b, 0.0))).astype(BF16)
    return qs, ks


def _split2(x):
    hi = x.astype(BF16)
    lo = (x - hi.astype(F32)).astype(BF16)
    return jnp.concatenate([hi, lo], axis=1)


def _level_table(n):
    t = jnp.arange(n, dtype=jnp.int32)[:, None]
    s = jnp.arange(n, dtype=jnp.int32)[None, :]
    x = t ^ s
    lvl = sum((x >= (1 << j)).astype(jnp.int32) for j in range(1, n.bit_length()))
    return jnp.where(t > s, lvl, jnp.where(t == s, -1, -2))


def _hgrn_gates(qz, fz, la, lc, oml):
    sq = jax.nn.sigmoid(qz)
    q = qz * sq
    s = jax.nn.sigmoid(fz)
    c = lc + jnp.minimum(fz, 0.0) - jnp.log(1.0 + jnp.exp(-jnp.abs(fz)))
    mx = jnp.maximum(la, c)
    g = mx + jnp.log(1.0 + jnp.exp(-jnp.abs(la - c)))
    k = oml * (1.0 - s)
    return q, sq, k, s, g, c


def _hgrn_fwd(u, la, lc, oml, gn_g, name):
    S = u.shape[0]
    W = u.shape[1] // 4
    H = W // HGRN_KDIM
    C = min(HGRN_CHUNK, S)
    nc = S // C
    levels = [m for m in _LEVELS if m < C]
    HB = C // 2

    def body(qz_ref, fz_ref, iz_ref, gz_ref, la_ref, lc_ref, oml_ref, gn_ref, lvl_ref,
             z_ref, o_ref, a_ref, st_ref, state, b_scr):
        @pl.when(pl.program_id(1) == 0)
        def _():
            state[...] = jnp.zeros_like(state)

        rows = lax.broadcasted_iota(jnp.int32, (C, LANE), 0)
        q, _, k, _, g, _ = _hgrn_gates(qz_ref[...], fz_ref[...], la_ref[...], lc_ref[...], oml_ref[...])
        v = iz_ref[...].astype(BF16)
        b = _chunk_cumsum(g, rows)
        b_scr[...] = b
        lvl = lvl_ref[...]
        qk = jnp.sum(q * k, axis=-1, keepdims=True)
        diag = [jnp.where(lvl == -1, qk[r * HB:(r + 1) * HB], 0.0) for r in range(2)]
        for m in levels[1:]:
            qs, ks = _level_operands(q, k, b, b_scr, rows, m)
            for r in range(2):
                sl = slice(r * HB, (r + 1) * HB)
                diag[r] = jnp.where(lvl == m.bit_length() - 1, _dot(qs[sl], ks[sl], "nt"), diag[r])
        qs, ks = _level_operands(q, k, b, b_scr, rows, HB)
        low = _dot(qs[HB:], ks[:HB], "nt")
        a = jnp.concatenate([jnp.concatenate([diag[0], jnp.zeros((HB, HB), F32)], axis=1),
                             jnp.concatenate([low, diag[1]], axis=1)], axis=0)
        ab = a.astype(BF16)
        a_ref[...] = ab
        st = state[...]
        st_ref[...] = st
        o = _dot(ab, v, "nn") + _dot((q * jnp.exp(b)).astype(BF16), st.astype(BF16), "nt")
        bl = b_scr[C - 1:C, :]
        kh = (k * jnp.exp(bl - b)).astype(BF16)
        state[...] = st * jnp.exp(bl) + _dot(v, kh, "tn")
        o_ref[...] = o
        r = lax.rsqrt(jnp.mean(o * o, axis=-1, keepdims=True) + EPS)
        gz = gz_ref[...]
        z_ref[...] = (o * r * gn_ref[...] * (gz * jax.nn.sigmoid(gz))).astype(BF16)

    def col(off):
        return pl.BlockSpec((C, LANE), lambda h, i: (i, off * H + h))

    vec = pl.BlockSpec((1, LANE), lambda h, i: (0, h))
    tile = pl.BlockSpec((C, LANE), lambda h, i: (i, h))
    return pl.pallas_call(
        body, name=name, grid=(H, nc),
        in_specs=[col(0), col(1), col(2), col(3), vec, vec, vec, vec, pl.BlockSpec((HB, HB), lambda h, i: (0, 0))],
        out_specs=[tile, tile, pl.BlockSpec((None, C, C), lambda h, i: (h, i, 0)),
                   pl.BlockSpec((None, None, LANE, LANE), lambda h, i: (h, i, 0, 0))],
        out_shape=[jax.ShapeDtypeStruct((S, W), BF16), jax.ShapeDtypeStruct((S, W), F32),
                   jax.ShapeDtypeStruct((H, S, C), BF16), jax.ShapeDtypeStruct((H, nc, LANE, LANE), F32)],
        scratch_shapes=[pltpu.VMEM((LANE, LANE), F32), pltpu.VMEM((C, LANE), F32)],
        compiler_params=_params("parallel", "arbitrary"),
    )(u, u, u, u, la, lc, oml, gn_g, _level_table(HB))


def _hgrn_bwd(u, la, lc, oml, gn_g, o, a, st, dz, name):
    S = u.shape[0]
    W = u.shape[1] // 4
    H = W // HGRN_KDIM
    C = min(HGRN_CHUNK, S)
    nc = S // C
    levels = [m for m in _LEVELS if m < C]
    HB = C // 2

    def body(qz_ref, fz_ref, iz_ref, gz_ref, la_ref, lc_ref, oml_ref, gn_ref, o_ref, a_ref, st_ref, dz_ref, lvl_ref,
             dqz_ref, dfz_ref, diz_ref, dgz_ref, dla_ref, dlc_ref, doml_ref, dgn_ref, dstate, b_scr):
        @pl.when(pl.program_id(1) == 0)
        def _():
            dstate[...] = jnp.zeros_like(dstate)
            dla_ref[...] = jnp.zeros_like(dla_ref)
            dlc_ref[...] = jnp.zeros_like(dlc_ref)
            doml_ref[...] = jnp.zeros_like(doml_ref)
            dgn_ref[...] = jnp.zeros_like(dgn_ref)

        rows = lax.broadcasted_iota(jnp.int32, (C, LANE), 0)
        la_v, lc_v, oml_v = la_ref[...], lc_ref[...], oml_ref[...]
        qz, fz = qz_ref[...], fz_ref[...]
        q, sq, k, s, g, c = _hgrn_gates(qz, fz, la_v, lc_v, oml_v)
        vf = iz_ref[...]
        v = vf.astype(BF16)

        ov, gz, dzv, gn = o_ref[...], gz_ref[...], dz_ref[...], gn_ref[...]
        r = lax.rsqrt(jnp.mean(ov * ov, axis=-1, keepdims=True) + EPS)
        on = ov * r
        sg = jax.nn.sigmoid(gz)
        silu_g = gz * sg
        dgn_ref[...] += jnp.sum(dzv * on * silu_g, axis=0, keepdims=True)
        dgz_ref[...] = (dzv * on * gn * (sg * (1.0 + gz * (1.0 - sg)))).astype(BF16)
        don = dzv * gn * silu_g
        do_f = r * (don - on * jnp.mean(don * on, axis=-1, keepdims=True))
        do = do_f.astype(BF16)

        b = _chunk_cumsum(g, rows)
        b_scr[...] = b
        bl = b_scr[C - 1:C, :]
        e = jnp.exp(b)
        ebl = jnp.exp(bl)
        ekl = jnp.exp(bl - b)
        qh = q * e
        kh = k * ekl
        st_v = st_ref[...]
        dst = dstate[...]
        dstb = dst.astype(BF16)

        diz_ref[...] = (_dot(a_ref[...], do, "tn") + _dot(kh.astype(BF16), dstb, "nt")).astype(BF16)
        da = _dot(do, v, "nt")
        dqh = _dot(do, st_v.astype(BF16), "nn")
        dkh = _dot(v, dstb, "nn")
        dstate[...] = dst * ebl + _dot(do, qh.astype(BF16), "tn")
        dbl = jnp.sum(dkh * kh, axis=0, keepdims=True) + jnp.sum(dst * st_v, axis=0, keepdims=True) * ebl

        datt = jnp.sum(do_f * vf, axis=-1, keepdims=True)
        dqa = datt * k
        dka = datt * q
        lvl = lvl_ref[...]
        for m in levels:
            ref = _level_ref(b, b_scr, rows, m)
            eu = jnp.exp(jnp.minimum(b - ref, 0.0))
            el = jnp.exp(jnp.minimum(ref - b, 0.0))
            ks2, qs2 = _split2(k * el), _split2(q * eu)
            if m == HB:
                gm = da[HB:, :HB].astype(BF16)
                pq = jnp.concatenate([jnp.zeros((HB, 2 * LANE), F32), _dot(gm, ks2[:HB], "nn")], axis=0)
                pk = jnp.concatenate([_dot(gm, qs2[HB:], "tn"), jnp.zeros((HB, 2 * LANE), F32)], axis=0)
            else:
                gms = [jnp.where(lvl == m.bit_length() - 1, da[r * HB:(r + 1) * HB, r * HB:(r + 1) * HB], 0.0).astype(BF16)
                       for r in range(2)]
                pq = jnp.concatenate([_dot(gms[r], ks2[r * HB:(r + 1) * HB], "nn") for r in range(2)], axis=0)
                pk = jnp.concatenate([_dot(gms[r], qs2[r * HB:(r + 1) * HB], "tn") for r in range(2)], axis=0)
            dqa += (pq[:, :LANE] + pq[:, LANE:]) * eu
            dka += (pk[:, :LANE] + pk[:, LANE:]) * el
        db = q * dqa - k * dka + dqh * qh - dkh * kh
        db = db + jnp.where(rows == C - 1, dbl, 0.0)
        dq = dqa + dqh * e
        dk = dka + dkh * ekl
        dg = _chunk_cumsum(db, rows, reverse=True)

        wa = jnp.exp(la_v - g)
        wc = jnp.exp(c - g)
        dqz_ref[...] = (dq * (sq * (1.0 + qz * (1.0 - sq)))).astype(BF16)
        dfz_ref[...] = (dg * wc * (1.0 - s) - dk * oml_v * s * (1.0 - s)).astype(BF16)
        dla_ref[...] += jnp.sum(dg * wa, axis=0, keepdims=True)
        dlc_ref[...] += jnp.sum(dg * wc, axis=0, keepdims=True)
        doml_ref[...] += jnp.sum(dk * (1.0 - s), axis=0, keepdims=True)

    def col(off):
        return pl.BlockSpec((C, LANE), lambda h, i: (nc - 1 - i, off * H + h))

    vec = pl.BlockSpec((1, LANE), lambda h, i: (0, h))
    tile = pl.BlockSpec((C, LANE), lambda h, i: (nc - 1 - i, h))
    a_spec = pl.BlockSpec((None, C, C), lambda h, i: (h, nc - 1 - i, 0))
    st_spec = pl.BlockSpec((None, None, LANE, LANE), lambda h, i: (h, nc - 1 - i, 0, 0))
    sw = jax.ShapeDtypeStruct((S, W), BF16)
    vw = jax.ShapeDtypeStruct((1, W), F32)
    return pl.pallas_call(
        body, name=name, grid=(H, nc),
        in_specs=[col(0), col(1), col(2), col(3), vec, vec, vec, vec, tile, a_spec, st_spec, tile,
                  pl.BlockSpec((HB, HB), lambda h, i: (0, 0))],
        out_specs=[tile, tile, tile, tile, vec, vec, vec, vec],
        out_shape=[sw, sw, sw, sw, vw, vw, vw, vw],
        scratch_shapes=[pltpu.VMEM((LANE, LANE), F32), pltpu.VMEM((C, LANE), F32)],
        compiler_params=_params("parallel", "arbitrary"),
    )(u, u, u, u, la, lc, oml, gn_g, o, a, st, dz, _level_table(HB))


def _lb_terms(lb_logits, layer):
    p = jax.nn.softmax(lb_logits, axis=0)
    lb = (jnp.cumsum(p, axis=0) - p[0:1])[layer]
    return jnp.log(lb)[None], jnp.log1p(-lb)[None], (1.0 - lb)[None]


def kernel(x, norm_ffn1, ffn1_wg, ffn1_wu, ffn1_wd, norm_mix, norm_ffn2, ffn2_wg, ffn2_wu, ffn2_wd, ev_w_in, ev_conv_w, ev_conv_b, ev_cn_g, ev_cn_b, ev_qn_g, ev_kn_g, ev_w_out, od_w_in, od_lb_logits, od_gn_g, od_w_out, loss_target, m_norm_ffn1, m_ffn1_wg, m_ffn1_wu, m_ffn1_wd, m_norm_mix, m_norm_ffn2, m_ffn2_wg, m_ffn2_wu, m_ffn2_wd, m_ev_w_in, m_ev_conv_w, m_ev_conv_b, m_ev_cn_g, m_ev_cn_b, m_ev_qn_g, m_ev_kn_g, m_ev_w_out, m_od_w_in, m_od_lb_logits, m_od_gn_g, m_od_w_out, v_norm_ffn1, v_ffn1_wg, v_ffn1_wu, v_ffn1_wd, v_norm_mix, v_norm_ffn2, v_ffn2_wg, v_ffn2_wu, v_ffn2_wd, v_ev_w_in, v_ev_conv_w, v_ev_conv_b, v_ev_cn_g, v_ev_cn_b, v_ev_qn_g, v_ev_kn_g, v_ev_w_out, v_od_w_in, v_od_lb_logits, v_od_gn_g, v_od_w_out):
    depth = norm_ffn1.shape[0]
    S, D = x.shape[1], x.shape[2]
    xi, yi, ci = _me()
    dev = 4 * xi + 2 * yi + ci
    c_idx = jnp.reshape(ci, (1,)).astype(jnp.int32)
    k_idx = jnp.reshape(2 * xi + yi, (1,)).astype(jnp.int32)

    def ffn_shard(wg, wu, wd, l):
        return jnp.stack([wg[l].T, wu[l].T, wd[l]]).astype(BF16)

    assert depth == 2, "the exchange schedule below is written for one even and one odd layer"
    sh_ffn1 = [ffn_shard(ffn1_wg, ffn1_wu, ffn1_wd, l) for l in range(depth)]
    sh_ffn2 = [ffn_shard(ffn2_wg, ffn2_wu, ffn2_wd, l) for l in range(depth)]
    sh_ev = [ev_w_in[0].T.astype(BF16)[None], ev_w_out[0].astype(BF16)[None]]
    sh_od = [od_w_in[0].T.astype(BF16)[None], od_w_out[0].astype(BF16)[None]]

    def full(g):
        return g.reshape(g.shape[0], N_DEV * g.shape[2], g.shape[3])

    def gather_begin(shards, after, tag):
        lands = [lax.dynamic_update_slice(lax.empty((s.shape[0], N_DEV) + s.shape[1:], s.dtype), s[:, None],
                                          (0, dev, 0, 0)) for s in shards]
        state = _push_start(shards, lands, _gather_plan, after, "gather_start" + tag)
        return state, state[4][0, 0]

    def gather_end(state, after, tag):
        send, recv, srcs, lands, _ = state
        _, lands = _push_wait(send, recv, srcs, lands, _gather_plan, after, "gather_wait" + tag)
        return [full(g) for g in _gather_forward(lands, "gather_forward" + tag)]

    w_ffn1, w_ffn2 = [None] * depth, [None] * depth
    pending, after = {}, None
    for key, shards in (("0", [sh_ffn1[0]]), ("1", sh_ev), ("2", [sh_ffn2[0]]), ("3", [sh_ffn1[1]]), ("4", sh_od),
                        ("5", [sh_ffn2[1]])):
        pending[key], _ = gather_begin(shards, after, "_" + key)
        after = pending[key][4]
    start_tok = after[0, 0]

    conv_w_sh, gn_g_sh = ev_conv_w[0], od_gn_g[0]
    cw, cs = conv_w_sh.shape[0], conv_w_sh.shape[1]
    gs = gn_g_sh.shape[0]
    conv_w_z = lax.dynamic_update_slice(jnp.zeros((cw, N_DEV * cs), F32), conv_w_sh, (0, dev * cs))
    gn_g_z = lax.dynamic_update_slice(jnp.zeros((N_DEV * gs,), F32), gn_g_sh, (dev * gs,))
    conv_w_full, gn_g_full = _unpack_rows(
        _all_reduce_small(_pack_rows([conv_w_z, gn_g_z]), "gather_small_params"),
        [conv_w_z.shape, gn_g_z.shape])
    (w_ffn1[0],) = gather_end(pending.pop("0"), [after, conv_w_full], "_0")

    def ffn_forward(h, gain, w, tag):
        hn = _rms_fwd(h, gain[None], "rms_" + tag)
        out, gu = _ffn_fwd(h, hn, w, "ffn_fwd_" + tag)
        return out, (h, hn, gu)

    def odd_mixer(u, l):
        (la, lc, oml), lb_vjp = jax.vjp(functools.partial(_lb_terms, layer=l), od_lb_logits)
        gn = gn_g_full[None]
        zb, o_raw, scores, states = _hgrn_fwd(u, la, lc, oml, gn, f"hgrn_fwd{l}")

        def backward(dz):
            dqz, dfz, diz, dgz, dla, dlc, doml, dgn = _hgrn_bwd(
                u, la, lc, oml, gn, o_raw, scores, states, dz, f"hgrn_bwd{l}")
            (g_lb,) = lb_vjp((dla, dlc, doml))
            return jnp.concatenate([dqz, dfz, diz, dgz], axis=1), [g_lb, dgn[0]]

        return zb, backward

    saved = []
    h = x[0]
    for l in range(depth):
        if l == 1:
            (w_ffn1[1],) = gather_end(pending.pop("3"), h, "_3")
        h, s1 = ffn_forward(h, norm_ffn1[l] + (start_tok if l == 0 else 0.0), w_ffn1[l], f"a{l}")
        w_in, w_out = gather_end(pending.pop("1" if l == 0 else "4"), h, "_1" if l == 0 else "_4")
        w_in_t, w_out = w_in[0], w_out[0]
        hn = _rms_fwd(h, norm_mix[l][None], f"rms_mix{l}")
        u = _mm(hn, w_in_t, "nt", F32, f"mix_in{l}")
        if l % 2 == 0:
            zb, core_vjp = _even_mixer(u, conv_w_full, ev_conv_b, ev_cn_g, ev_cn_b, ev_qn_g, ev_kn_g, str(l))
        else:
            zb, core_vjp = odd_mixer(u, l)
        h_mix = h
        h = _mm(zb, w_out, "nn", F32, f"mix_out{l}", res=h)
        sm = (h_mix, hn, zb, core_vjp, w_in_t, w_out)
        (w_ffn2[l],) = gather_end(pending.pop("2" if l == 0 else "5"), h, "_2" if l == 0 else "_5")
        h, s2 = ffn_forward(h, norm_ffn2[l], w_ffn2[l], f"b{l}")
        saved.append((s1, sm, s2))

    dy, loss_part = _loss_grad(h, loss_target[0], "loss_grad")

    def halves_begin(parts, tag):
        parts = [g.reshape(g.shape[0], 4, 2, g.shape[1] // N_DEV, g.shape[2]) for g in parts]
        lands = [lax.empty(g.shape[:2] + g.shape[3:], BF16) for g in parts]
        state = _push_start(parts, lands, _halves_plan, None, "halves_start" + tag)
        return state, state[4][0, 0]

    def chips_begin(state, after, tag):
        send, recv, srcs, lands, _ = state
        parts, got = _push_wait(send, recv, srcs, lands, _halves_plan, after, "halves_wait" + tag)
        sums = [_add_core_halves(g, r, c_idx, f"add_core_halves{tag}_{a}") for a, (g, r) in enumerate(zip(parts, got))]
        lands = [lax.empty((3, s.shape[0]) + s.shape[2:], BF16) for s in sums]
        state = _push_start(sums, lands, _chip_plan, None, "reduce_start" + tag)
        return state, state[4][0, 0]

    def reduce_end(state, after, tag):
        send, recv, srcs, lands, _ = state
        sums, got = _push_wait(send, recv, srcs, lands, _chip_plan, after, "reduce_wait" + tag)
        return [_sum_chip_blocks(s, r, k_idx, f"sum_chip_blocks{tag}_{a}") for a, (s, r) in enumerate(zip(sums, got))]

    def ffn_backward(dy, gain, w, sv, tag, on_dw, on_dx=None):
        h_in, hn, gu = sv
        dxn, dout, t = _ffn_bwd_dx(dy, w, gu, "ffn_bwd_dx_" + tag)
        tok = 0.0 if on_dx is None else on_dx(dxn)
        tok = tok + on_dw(_ffn_bwd_dw(hn, dout, t, "ffn_bwd_dw_" + tag))
        dx, dgain = _rms_bwd(h_in, (gain + tok)[None], dxn, dy, "rms_bwd_" + tag)
        return dx, dgain[0]

    g_norm1, g_norm2, g_normm = [None] * depth, [None] * depth, [None] * depth
    small, halves, groups = [None, None], {}, {}

    def start_halves(key, make_parts):
        def hook(dw):
            halves[key], tok = halves_begin(make_parts(dw), "_" + key)
            return tok
        return hook

    def start_chips(key):
        def hook(after):
            groups[key], tok = chips_begin(halves.pop(key), after, "_" + key)
            return tok
        return hook

    for l in reversed(range(depth)):
        s1, (h_mix, hn, zb, core_vjp, w_in_t, w_out), s2 = saved[l]
        if l == 1:
            dy, g_norm2[l] = ffn_backward(dy, norm_ffn2[l], w_ffn2[l], s2, f"b{l}", start_halves("1", lambda dw: [dw]))
        else:
            dy, g_norm2[l] = ffn_backward(dy, norm_ffn2[l], w_ffn2[l], s2, f"b{l}", start_halves("3", lambda dw: [dw]),
                                          start_chips("2"))
        dyb = dy.astype(BF16)
        dz = _mm(dyb, w_out, "nt", F32, f"mix_out_dz{l}")
        dw_out = _mm(zb, dyb, "tn", BF16, f"mix_out_dw{l}")
        dub, small[l % 2] = core_vjp(dz)
        dw_in_t = _mm(dub, hn, "tn", BF16, f"mix_in_dw{l}")
        mix_parts = [dw_in_t[None], dw_out[None]]
        if l == 1:
            tok = start_chips("1")(dw_in_t)
        else:
            tok = start_chips("3")(dw_in_t) + start_halves("4", lambda _: mix_parts)(None)
        dhn = _mm(dub, w_in_t, "nn", F32, f"mix_in_dx{l}")
        dy, gm = _rms_bwd(h_mix, (norm_mix[l] + tok)[None], dhn, dy, f"rms_bwd_mix{l}")
        g_normm[l] = gm[0]
        if l == 1:
            dy, g_norm1[l] = ffn_backward(dy, norm_ffn1[l], w_ffn1[l], s1, f"a{l}",
                                          start_halves("2", lambda dw, od=mix_parts: od + [dw]))
        else:
            dy, g_norm1[l] = ffn_backward(dy, norm_ffn1[l], w_ffn1[l], s1, f"a{l}", start_halves("5", lambda dw: [dw]),
                                          start_chips("4"))
    grad_x = dy[None]
    start_chips("5")(dy)

    done = [dy, groups["5"][4]]
    (g_ffn2_1,) = reduce_end(groups["1"], done, "_1")
    g_od_in_t, g_od_out, g_ffn1_1 = reduce_end(groups["2"], done, "_2")
    (g_ffn2_0,) = reduce_end(groups["3"], done, "_3")
    g_ev_in_t, g_ev_out = reduce_end(groups["4"], done, "_4")
    g_ffn2 = [g_ffn2_0, g_ffn2_1]

    def ffn_grads(gl):
        return (jnp.stack([g[0].T for g in gl]), jnp.stack([g[1].T for g in gl]), jnp.stack([g[2] for g in gl]))

    g_ffn2_wg, g_ffn2_wu, g_ffn2_wd = ffn_grads(g_ffn2)
    grads = [None, None, None, None, None, None, g_ffn2_wg, g_ffn2_wu, g_ffn2_wd,
             g_ev_in_t[0].T[None], None, None, None, None, None,
             None, g_ev_out, g_od_in_t[0].T[None], None, None, g_od_out]
    weights = [norm_ffn1, ffn1_wg, ffn1_wu, ffn1_wd, norm_mix, norm_ffn2, ffn2_wg, ffn2_wu, ffn2_wd, ev_w_in,
               ev_conv_w, ev_conv_b, ev_cn_g, ev_cn_b, ev_qn_g, ev_kn_g, ev_w_out, od_w_in, od_lb_logits,
               od_gn_g, od_w_out]
    moms = [m_norm_ffn1, m_ffn1_wg, m_ffn1_wu, m_ffn1_wd, m_norm_mix, m_norm_ffn2, m_ffn2_wg, m_ffn2_wu,
            m_ffn2_wd, m_ev_w_in, m_ev_conv_w, m_ev_conv_b, m_ev_cn_g, m_ev_cn_b, m_ev_qn_g, m_ev_kn_g,
            m_ev_w_out, m_od_w_in, m_od_lb_logits, m_od_gn_g, m_od_w_out]
    vars_ = [v_norm_ffn1, v_ffn1_wg, v_ffn1_wu, v_ffn1_wd, v_norm_mix, v_norm_ffn2, v_ffn2_wg, v_ffn2_wu,
             v_ffn2_wd, v_ev_w_in, v_ev_conv_w, v_ev_conv_b, v_ev_cn_g, v_ev_cn_b, v_ev_qn_g, v_ev_kn_g,
             v_ev_w_out, v_od_w_in, v_od_lb_logits, v_od_gn_g, v_od_w_out]
    n_w = len(weights)
    deltas, new_m, new_v = [None] * n_w, [None] * n_w, [None] * n_w

    def update(idx):
        for i in idx:
            deltas[i], new_m[i], new_v[i] = _adamw(weights[i], grads[i], moms[i], vars_[i], f"adamw{i}")

    update([i for i in range(n_w) if grads[i] is not None])
    g_conv_w, g_conv_b, g_cn_g, g_cn_b, g_qn_g, g_kn_g = small[0]
    g_lb, g_gn = small[1]
    parts = [jnp.stack(g_norm1), jnp.stack(g_normm), jnp.stack(g_norm2), g_conv_b, g_cn_g, g_cn_b,
             g_qn_g, g_kn_g, g_lb, g_conv_w, g_gn, loss_part[0, :1]]
    red = _unpack_rows(_all_reduce_small(_pack_rows(parts), "reduce_small_grads", [d for d in deltas if d is not None]),
                       [p.shape for p in parts])
    g_norm1, g_normm, g_norm2, g_conv_b, g_cn_g, g_cn_b, g_qn_g, g_kn_g, g_lb, g_conv_w, g_gn, loss = red
    g_conv_w = lax.dynamic_slice(g_conv_w, (0, dev * cs), (cw, cs))
    g_gn = lax.dynamic_slice(g_gn, (dev * gs,), (gs,))
    small_idx = {0: g_norm1, 4: g_normm, 5: g_norm2, 10: g_conv_w[None], 11: g_conv_b[None], 12: g_cn_g[None],
                 13: g_cn_b[None], 14: g_qn_g[None], 15: g_kn_g[None], 18: g_lb, 19: g_gn[None]}
    for i, g in small_idx.items():
        grads[i] = g
    update(small_idx)
    (g_ffn1_0,) = reduce_end(groups["5"], [d for d in deltas if d is not None], "_5")
    grads[1], grads[2], grads[3] = ffn_grads([g_ffn1_0, g_ffn1_1])
    update((1, 2, 3))
    return (loss[0], grad_x, *grads, *deltas, *new_m, *new_v)
```

```python
import functools
import math

import jax
import jax.numpy as jnp
from jax import lax
from jax.experimental import pallas as pl
from jax.experimental.pallas import tpu as pltpu

F32 = jnp.float32
BF16 = jnp.bfloat16
MESH = pl.DeviceIdType.MESH
N_DEV = 8

EPS = 1e-6
HEAD_DIM = 128
CONV_WIDTH = 31
DIL_PATTERNS = ((128, 1), (512, 4), (2048, 16))
Q_BLOCK = 128
ROPE_THETA = 10000.0
HGRN_KDIM = 128
HGRN_CHUNK = 256

ADAM_LR = 0.001
ADAM_B1 = 0.9
ADAM_B2 = 0.999
ADAM_EPS = 1e-08
ADAM_WD = 0.01
ADAM_STEP = 10

VMEM_LIMIT_BYTES = 56 * 1024 * 1024
LANE = 128
SUBLANE_BF16 = 16

ANY = pl.BlockSpec(memory_space=pl.ANY)


def _tile(n, pref, mult):
    t = (min(pref, n) // mult) * mult
    while t > 0:
        if n % t == 0:
            return t
        t -= mult
    return n


def _params(*sem):
    return pltpu.CompilerParams(dimension_semantics=sem, vmem_limit_bytes=VMEM_LIMIT_BYTES)


_DOT_DIMS = {
    "nn": (((1,), (0,)), ((), ())),
    "nt": (((1,), (1,)), ((), ())),
    "tn": (((0,), (0,)), ((), ())),
}


def _dot(a, b, mode):
    return lax.dot_general(a, b, _DOT_DIMS[mode], preferred_element_type=F32)


def _mm(a, b, mode, out_dtype, name, res=None, tm=1024, tn=1024, tk=2048):
    if mode == "nt":
        (M, K), N = a.shape, b.shape[0]
    elif mode == "nn":
        (M, K), N = a.shape, b.shape[1]
    else:
        (K, M), N = a.shape, b.shape[1]
    tm, tn, tk = _tile(M, tm, LANE), _tile(N, tn, LANE), _tile(K, tk, LANE)
    nk = K // tk

    def body(*refs):
        if res is None:
            a_ref, b_ref, o_ref, acc = refs
        else:
            a_ref, b_ref, r_ref, o_ref, acc = refs
        k = pl.program_id(2)

        @pl.when(k == 0)
        def _():
            acc[...] = jnp.zeros_like(acc)

        acc[...] += _dot(a_ref[...].astype(BF16), b_ref[...].astype(BF16), mode)

        @pl.when(k == nk - 1)
        def _():
            r = acc[...]
            if res is not None:
                r = r_ref[...] + r
            o_ref[...] = r.astype(out_dtype)

    a_spec = {"nt": pl.BlockSpec((tm, tk), lambda i, j, k: (i, k)),
              "nn": pl.BlockSpec((tm, tk), lambda i, j, k: (i, k)),
              "tn": pl.BlockSpec((tk, tm), lambda i, j, k: (k, i))}[mode]
    b_spec = {"nt": pl.BlockSpec((tn, tk), lambda i, j, k: (j, k)),
              "nn": pl.BlockSpec((tk, tn), lambda i, j, k: (k, j)),
              "tn": pl.BlockSpec((tk, tn), lambda i, j, k: (k, j))}[mode]
    o_spec = pl.BlockSpec((tm, tn), lambda i, j, k: (i, j))
    in_specs = [a_spec, b_spec] + ([o_spec] if res is not None else [])
    args = (a, b) + ((res,) if res is not None else ())
    return pl.pallas_call(
        body, name=name, grid=(M // tm, N // tn, nk),
        in_specs=in_specs, out_specs=o_spec,
        out_shape=jax.ShapeDtypeStruct((M, N), out_dtype),
        scratch_shapes=[pltpu.VMEM((tm, tn), F32)],
        compiler_params=_params("parallel", "parallel", "arbitrary"),
    )(*args)


def _rms_fwd(x, gain, name):
    S, D = x.shape
    tm = _tile(S, 512, SUBLANE_BF16)

    def body(x_ref, g_ref, o_ref):
        xv = x_ref[...]
        r = lax.rsqrt(jnp.mean(xv * xv, axis=-1, keepdims=True) + EPS)
        o_ref[...] = (xv * r * g_ref[...]).astype(BF16)

    return pl.pallas_call(
        body, name=name, grid=(S // tm,),
        in_specs=[pl.BlockSpec((tm, D), lambda i: (i, 0)), pl.BlockSpec((1, D), lambda i: (0, 0))],
        out_specs=pl.BlockSpec((tm, D), lambda i: (i, 0)),
        out_shape=jax.ShapeDtypeStruct((S, D), BF16),
        compiler_params=_params("parallel"),
    )(x, gain)


def _resid_rms(x, ffn, gain, name):
    S, D = x.shape
    tm = _tile(S, 512, SUBLANE_BF16)

    def body(x_ref, f_ref, g_ref, h_ref, o_ref):
        hv = x_ref[...] + 0.5 * f_ref[...]
        h_ref[...] = hv
        r = lax.rsqrt(jnp.mean(hv * hv, axis=-1, keepdims=True) + EPS)
        o_ref[...] = (hv * r * g_ref[...]).astype(BF16)

    row = pl.BlockSpec((tm, D), lambda i: (i, 0))
    return pl.pallas_call(
        body, name=name, grid=(S // tm,),
        in_specs=[row, row, pl.BlockSpec((1, D), lambda i: (0, 0))], out_specs=[row, row],
        out_shape=[jax.ShapeDtypeStruct((S, D), F32), jax.ShapeDtypeStruct((S, D), BF16)],
        compiler_params=_params("parallel"),
    )(x, ffn, gain)


def _rms_bwd(x, gain, dxn, dy, name):
    S, D = x.shape
    tm = _tile(S, 512, 8)

    def body(x_ref, g_ref, dxn_ref, dy_ref, dx_ref, dg_ref):
        @pl.when(pl.program_id(0) == 0)
        def _():
            dg_ref[...] = jnp.zeros_like(dg_ref)

        xv = x_ref[...]
        r = lax.rsqrt(jnp.mean(xv * xv, axis=-1, keepdims=True) + EPS)
        xh = xv * r
        dxn_v = dxn_ref[...]
        dg_ref[...] += jnp.sum(dxn_v * xh, axis=0, keepdims=True)
        dxh = dxn_v * g_ref[...]
        dx_ref[...] = dy_ref[...] + r * (dxh - xh * jnp.mean(dxh * xh, axis=-1, keepdims=True))

    row = pl.BlockSpec((tm, D), lambda i: (i, 0))
    vec = pl.BlockSpec((1, D), lambda i: (0, 0))
    return pl.pallas_call(
        body, name=name, grid=(S // tm,),
        in_specs=[row, vec, row, row], out_specs=[row, vec],
        out_shape=[jax.ShapeDtypeStruct((S, D), F32), jax.ShapeDtypeStruct((1, D), F32)],
        compiler_params=_params("arbitrary"),
    )(x, gain, dxn, dy)


def _ffn_fwd(xn, w, name):
    S, D = xn.shape
    F = w.shape[1]
    tm, tf = _tile(S, 1024, SUBLANE_BF16), _tile(F, 512, LANE)
    nf = F // tf

    def body(xn_ref, w_ref, o_ref, gu_ref):
        @pl.when(pl.program_id(1) == 0)
        def _():
            o_ref[...] = jnp.zeros_like(o_ref)

        xnv = xn_ref[...]
        g = _dot(xnv, w_ref[0], "nt")
        u = _dot(xnv, w_ref[1], "nt")
        gu_ref[0] = g.astype(BF16)
        gu_ref[1] = u.astype(BF16)
        h = (g * jax.nn.sigmoid(g) * u).astype(BF16)
        o_ref[...] += _dot(h, w_ref[2], "nn")

    row = pl.BlockSpec((tm, D), lambda i, f: (i, 0))
    return pl.pallas_call(
        body, name=name, grid=(S // tm, nf),
        in_specs=[row, pl.BlockSpec((3, tf, D), lambda i, f: (0, f, 0))],
        out_specs=[row, pl.BlockSpec((2, tm, tf), lambda i, f: (0, i, f))],
        out_shape=[jax.ShapeDtypeStruct((S, D), F32), jax.ShapeDtypeStruct((2, S, F), BF16)],
        compiler_params=_params("parallel", "arbitrary"),
    )(xn, w)


def _ffn_bwd_dx(dy, w, gu, name):
    S, D = dy.shape
    F = w.shape[1]
    tm, tf = _tile(S, 512, SUBLANE_BF16), _tile(F, 512, LANE)
    nf = F // tf

    def body(dy_ref, w_ref, gu_ref, dxn_ref, dout_ref, t_ref, acc):
        f = pl.program_id(1)

        @pl.when(f == 0)
        def _():
            acc[...] = jnp.zeros_like(acc)
            dout_ref[...] = (0.5 * dy_ref[...]).astype(BF16)

        dh = _dot(dout_ref[...], w_ref[2], "nt")
        g = gu_ref[0].astype(F32)
        u = gu_ref[1].astype(F32)
        sig = jax.nn.sigmoid(g)
        silu = g * sig
        dg = (dh * u * (sig * (1.0 + g * (1.0 - sig)))).astype(BF16)
        du = (dh * silu).astype(BF16)
        t_ref[0] = dg
        t_ref[1] = du
        t_ref[2] = (silu * u).astype(BF16)
        acc[...] += _dot(dg, w_ref[0], "nn") + _dot(du, w_ref[1], "nn")

        @pl.when(f == nf - 1)
        def _():
            dxn_ref[...] = acc[...]

    row = pl.BlockSpec((tm, D), lambda i, f: (i, 0))
    return pl.pallas_call(
        body, name=name, grid=(S // tm, nf),
        in_specs=[row, pl.BlockSpec((3, tf, D), lambda i, f: (0, f, 0)),
                  pl.BlockSpec((2, tm, tf), lambda i, f: (0, i, f))],
        out_specs=[row, row, pl.BlockSpec((3, tm, tf), lambda i, f: (0, i, f))],
        out_shape=[jax.ShapeDtypeStruct((S, D), F32), jax.ShapeDtypeStruct((S, D), BF16),
                   jax.ShapeDtypeStruct((3, S, F), BF16)],
        scratch_shapes=[pltpu.VMEM((tm, D), F32)],
        compiler_params=_params("parallel", "arbitrary"),
    )(dy, w, gu)


def _ffn_bwd_dw(xn, dout, t, name):
    S, D = xn.shape
    F = t.shape[2]
    ts, tf = _tile(S, 1024, LANE), _tile(F, 512, LANE)
    ns = S // ts

    def body(xn_ref, dout_ref, t_ref, dw_ref, acc):
        s = pl.program_id(1)

        @pl.when(s == 0)
        def _():
            acc[...] = jnp.zeros_like(acc)

        xnv = xn_ref[...]
        acc[0] += _dot(t_ref[0], xnv, "tn")
        acc[1] += _dot(t_ref[1], xnv, "tn")
        acc[2] += _dot(t_ref[2], dout_ref[...], "tn")

        @pl.when(s == ns - 1)
        def _():
            dw_ref[...] = acc[...].astype(BF16)

    row = pl.BlockSpec((ts, D), lambda f, s: (s, 0))
    return pl.pallas_call(
        body, name=name, grid=(F // tf, ns),
        in_specs=[row, row, pl.BlockSpec((3, ts, tf), lambda f, s: (0, s, f))],
        out_specs=pl.BlockSpec((3, tf, D), lambda f, s: (0, f, 0)),
        out_shape=jax.ShapeDtypeStruct((3, F, D), BF16),
        scratch_shapes=[pltpu.VMEM((3, tf, D), F32)],
        compiler_params=_params("parallel", "arbitrary"),
    )(xn, dout, t)


def _loss_grad(x, ffn, target, name):
    S, D = x.shape
    tm = _tile(S, 512, 8)

    def body(x_ref, f_ref, t_ref, dy_ref, l_ref):
        @pl.when(pl.program_id(0) == 0)
        def _():
            l_ref[...] = jnp.zeros_like(l_ref)

        e = (x_ref[...] + 0.5 * f_ref[...]) - t_ref[...]
        dy_ref[...] = e * (1.0 / D)
        l_ref[...] += 0.5 * jnp.sum(jnp.sum(e * e, axis=-1, keepdims=True) * (1.0 / D))

    row = pl.BlockSpec((tm, D), lambda i: (i, 0))
    one = pl.BlockSpec((8, LANE), lambda i: (0, 0))
    return pl.pallas_call(
        body, name=name, grid=(S // tm,),
        in_specs=[row, row, row], out_specs=[row, one],
        out_shape=[jax.ShapeDtypeStruct((S, D), F32), jax.ShapeDtypeStruct((8, LANE), F32)],
        compiler_params=_params("arbitrary"),
    )(x, ffn, target)


def _adamw(w, g, m, v, name):
    shape = w.shape
    C = shape[-1]
    R = math.prod(shape[:-1])
    tr = _tile(R, max(8, (1 << 19) // C // 8 * 8), 8)
    c1 = 1.0 / (1.0 - ADAM_B1 ** ADAM_STEP)
    c2 = 1.0 / (1.0 - ADAM_B2 ** ADAM_STEP)

    def body(w_ref, g_ref, m_ref, v_ref, d_ref, nm_ref, nv_ref):
        gv = g_ref[...]
        nm = ADAM_B1 * m_ref[...] + (1.0 - ADAM_B1) * gv
        nv = ADAM_B2 * v_ref[...] + (1.0 - ADAM_B2) * (gv * gv)
        nm_ref[...] = nm
        nv_ref[...] = nv
        d_ref[...] = -ADAM_LR * ((nm * c1) / (jnp.sqrt(nv * c2) + ADAM_EPS) + ADAM_WD * w_ref[...])

    blk = pl.BlockSpec((tr, C), lambda i: (i, 0))
    sds = jax.ShapeDtypeStruct((R, C), F32)
    outs = pl.pallas_call(
        body, name=name, grid=(R // tr,),
        in_specs=[blk] * 4, out_specs=[blk] * 3, out_shape=[sds] * 3,
        compiler_params=_params("parallel"),
    )(*(a.reshape(R, C) for a in (w, g, m, v)))
    return tuple(o.reshape(shape) for o in outs)


def _me():
    return lax.axis_index("x"), lax.axis_index("y"), lax.axis_index("c")


def _add_core_halves(grad, got, c_idx, name):
    n, nk, _, r, C = grad.shape
    tr = _tile(r, 1024, SUBLANE_BF16)

    def body(c_ref, g_ref, r_ref, o_ref):
        o_ref[...] = (g_ref[...].astype(F32) + r_ref[...].astype(F32)).astype(BF16)

    return pl.pallas_call(
        body, name=name,
        grid_spec=pltpu.PrefetchScalarGridSpec(
            num_scalar_prefetch=1, grid=(n, nk, r // tr),
            in_specs=[pl.BlockSpec((None, None, None, tr, C), lambda i, k, t, c: (i, k, c[0], t, 0)),
                      pl.BlockSpec((None, None, tr, C), lambda i, k, t, c: (i, k, t, 0))],
            out_specs=pl.BlockSpec((None, None, tr, C), lambda i, k, t, c: (i, k, t, 0))),
        out_shape=jax.ShapeDtypeStruct((n, nk, r, C), BF16),
        compiler_params=_params("parallel", "parallel", "parallel"),
    )(c_idx, grad, got)


HBM = pl.BlockSpec(memory_space=pltpu.HBM)
SEM = pl.BlockSpec(memory_space=pltpu.SEMAPHORE)
EFFECT = pltpu.SideEffectType.DATAFLOW_SIDE_EFFECTING


def _push_start(srcs, lands, plan, after, name):
    ns, nl = len(srcs), len(lands)
    ncp = len(plan([None] * ns, [None] * nl, dry=True))
    extra = [] if after is None else [after]

    def body(*refs):
        src_refs, land_refs = refs[:ns], refs[ns:ns + nl]
        send_sems, recv_sems = refs[ns + nl + len(extra)], refs[ns + nl + len(extra) + 1]
        token = refs[-1]
        for i, (s, d, to) in enumerate(plan(src_refs, land_refs)):
            pltpu.make_async_remote_copy(src_ref=s, dst_ref=d, send_sem=send_sems.at[i], recv_sem=recv_sems.at[i],
                                         device_id=to, device_id_type=MESH).start()
        token[...] = jnp.zeros_like(token)

    out = pl.pallas_call(
        body, name=name,
        out_shape=(pltpu.SemaphoreType.DMA((ncp,)), pltpu.SemaphoreType.DMA((ncp,)),
                   *[pltpu.HBM(a.shape, a.dtype) for a in srcs], *[pltpu.HBM(a.shape, a.dtype) for a in lands],
                   jax.ShapeDtypeStruct((8, LANE), F32)),
        in_specs=[HBM] * (ns + nl) + [ANY] * len(extra),
        out_specs=(SEM, SEM, *[HBM] * (ns + nl), pl.BlockSpec(memory_space=pltpu.VMEM)),
        input_output_aliases={i: 2 + i for i in range(ns + nl)},
        compiler_params=pltpu.CompilerParams(has_side_effects=EFFECT),
    )(*[pltpu.with_memory_space_constraint(a, pltpu.HBM) for a in srcs + lands], *extra)
    return out[0], out[1], list(out[2:2 + ns]), list(out[2 + ns:2 + ns + nl]), out[-1]


def _push_wait(send_sems, recv_sems, srcs, lands, plan, after, name):
    ns, nl = len(srcs), len(lands)
    after = list(after) if isinstance(after, (list, tuple)) else [after]

    def body(*refs):
        src_refs, land_refs = refs[:ns], refs[ns:ns + nl]
        send, recv = refs[ns + nl], refs[ns + nl + 1]
        for i, (s, d, to) in enumerate(plan(src_refs, land_refs)):
            cp = pltpu.make_async_remote_copy(src_ref=s, dst_ref=d, send_sem=send.at[i], recv_sem=recv.at[i],
                                              device_id=to, device_id_type=MESH)
            cp.wait_send()
            cp.wait_recv()

    out = pl.pallas_call(
        body, name=name,
        out_shape=tuple(pltpu.HBM(a.shape, a.dtype) for a in srcs + lands),
        in_specs=[HBM] * (ns + nl) + [SEM, SEM] + [ANY] * len(after),
        out_specs=tuple([HBM] * (ns + nl)),
        input_output_aliases={i: i for i in range(ns + nl)},
        compiler_params=pltpu.CompilerParams(has_side_effects=EFFECT),
    )(*srcs, *lands, send_sems, recv_sems, *after)
    return list(out[:ns]), list(out[ns:])


def _gather_plan(src_refs, land_refs, dry=False):
    if dry:
        return [None] * (4 * len(src_refs))
    x, y, c = _me()
    me = 4 * x + 2 * y + c
    targets = [(x, y, 1 - c), (1 - x, y, c), (x, 1 - y, c), (1 - x, 1 - y, c)]
    return [(s, l.at[:, me], to) for s, l in zip(src_refs, land_refs) for to in targets]


def _halves_plan(src_refs, land_refs, dry=False):
    if dry:
        return [None] * len(src_refs)
    x, y, c = _me()
    return [(s.at[:, :, 1 - c], l, (x, y, 1 - c)) for s, l in zip(src_refs, land_refs)]


def _chip_plan(src_refs, land_refs, dry=False):
    if dry:
        return [None] * (3 * len(src_refs))
    x, y, c = _me()
    chips = [(1 - x, y), (x, 1 - y), (1 - x, 1 - y)]
    return [(s.at[:, 2 * chip[0] + chip[1]], l.at[j], (*chip, c))
            for s, l in zip(src_refs, land_refs) for j, chip in enumerate(chips)]


def _forward_plan(src_refs, land_refs, dry=False):
    if dry:
        return [None] * (3 * len(land_refs))
    x, y, c = _me()
    chips = [(1 - x, y), (x, 1 - y), (1 - x, 1 - y)]
    plan = []
    for l in land_refs:
        for chip in chips:
            blk = l.at[:, 4 * chip[0] + 2 * chip[1] + c]
            plan.append((blk, blk, (x, y, 1 - c)))
    return plan


def _sum_chip_blocks(sums, got, k_idx, name):
    n, _, r, C = sums.shape
    tr = _tile(r, 512, SUBLANE_BF16)

    def body(k_ref, s_ref, r_ref, o_ref):
        acc = s_ref[...].astype(F32)
        for j in range(3):
            acc = acc + r_ref[j].astype(F32)
        o_ref[...] = acc

    return pl.pallas_call(
        body, name=name,
        grid_spec=pltpu.PrefetchScalarGridSpec(
            num_scalar_prefetch=1, grid=(n, r // tr),
            in_specs=[pl.BlockSpec((None, None, tr, C), lambda i, t, k: (i, k[0], t, 0)),
                      pl.BlockSpec((3, None, tr, C), lambda i, t, k: (0, i, t, 0))],
            out_specs=pl.BlockSpec((None, tr, C), lambda i, t, k: (i, t, 0))),
        out_shape=jax.ShapeDtypeStruct((n, r, C), F32),
        compiler_params=_params("parallel", "parallel"),
    )(k_idx, sums, got)


def _all_reduce_small(v, name, after=()):
    R = v.shape[0]
    after = list(after)

    def body(*refs):
        v_ref = refs[0]
        o_ref, buf, send_sems, recv_sems = refs[1 + len(after):]
        x, y, c = _me()
        me = 4 * x + 2 * y + c
        buf[me] = v_ref[...]
        copies = []
        for k in range(1, N_DEV):
            peer = (x ^ (k >> 2), y ^ ((k >> 1) & 1), c ^ (k & 1))
            copies.append(pltpu.make_async_remote_copy(
                src_ref=v_ref, dst_ref=buf.at[me],
                send_sem=send_sems.at[k - 1], recv_sem=recv_sems.at[k - 1],
                device_id=peer, device_id_type=MESH))
        for cp in copies:
            cp.start()
        for cp in copies:
            cp.wait()
        acc = buf[0]
        for d in range(1, N_DEV):
            acc = acc + buf[d]
        o_ref[...] = acc

    vm = pl.BlockSpec(memory_space=pltpu.VMEM)
    return pl.pallas_call(
        body, name=name, in_specs=[vm] + [ANY] * len(after), out_specs=vm,
        out_shape=jax.ShapeDtypeStruct((R, LANE), F32),
        scratch_shapes=[pltpu.VMEM((N_DEV, R, LANE), F32),
                        pltpu.SemaphoreType.DMA((N_DEV - 1,)), pltpu.SemaphoreType.DMA((N_DEV - 1,))],
        compiler_params=pltpu.CompilerParams(vmem_limit_bytes=VMEM_LIMIT_BYTES),
    )(v, *after)


def _pack_rows(parts):
    flat = jnp.concatenate([p.reshape(-1).astype(F32) for p in parts])
    n = flat.shape[0]
    rows = -(-n // (8 * LANE)) * 8
    flat = jnp.pad(flat, (0, rows * LANE - n))
    return flat.reshape(rows, LANE)


def _unpack_rows(packed, shapes):
    flat = packed.reshape(-1)
    out, off = [], 0
    for s in shapes:
        n = math.prod(s)
        out.append(flat[off:off + n].reshape(s))
        off += n
    return out


CONV_HALO = 32


def _conv_fwd(u, conv_w, conv_b, cn_g, cn_b, name):
    S = u.shape[0]
    C = conv_w.shape[1]
    T = _tile(S, 256, CONV_HALO)
    hb = T // CONV_HALO

    def body(av_ref, ag_ref, pv_ref, pg_ref, w_ref, b_ref, g_ref, bb_ref, out_ref, y_ref, scr):
        i = pl.program_id(0)
        prev = pv_ref[...] * jax.nn.sigmoid(pg_ref[...])
        scr[0:CONV_HALO, :] = jnp.where(i > 0, prev, 0.0)
        scr[CONV_HALO:CONV_HALO + T, :] = av_ref[...] * jax.nn.sigmoid(ag_ref[...])
        for s in range(C // LANE):
            sl = slice(s * LANE, (s + 1) * LANE)
            acc = jnp.broadcast_to(b_ref[:, sl], (T, LANE))
            for j in range(CONV_WIDTH):
                acc = acc + w_ref[j:j + 1, sl] * scr[pl.ds(CONV_HALO - (CONV_WIDTH - 1) + j, T), sl]
            y_ref[:, sl] = acc
        acc = y_ref[...]
        mu = jnp.mean(acc, axis=-1, keepdims=True)
        xc = acc - mu
        var = jnp.mean(xc * xc, axis=-1, keepdims=True)
        ln = xc * lax.rsqrt(var + EPS) * g_ref[...] + bb_ref[...]
        out_ref[...] = (ln * jax.nn.sigmoid(ln)).astype(BF16)

    def cur(cb):
        return pl.BlockSpec((T, C), lambda i: (i, cb))

    def halo(cb):
        return pl.BlockSpec((CONV_HALO, C), lambda i: (jnp.maximum(i * hb - 1, 0), cb))

    vec = pl.BlockSpec((1, C), lambda i: (0, 0))
    return pl.pallas_call(
        body, name=name, grid=(S // T,),
        in_specs=[cur(0), cur(1), halo(0), halo(1), pl.BlockSpec((CONV_WIDTH, C), lambda i: (0, 0)), vec, vec, vec],
        out_specs=[pl.BlockSpec((T, C), lambda i: (i, 0))] * 2,
        out_shape=[jax.ShapeDtypeStruct((S, C), BF16), jax.ShapeDtypeStruct((S, C), F32)],
        scratch_shapes=[pltpu.VMEM((T + CONV_HALO, C), F32)],
        compiler_params=_params("parallel"),
    )(u, u, u, u, conv_w, conv_b, cn_g, cn_b)


def _conv_bwd_norm(dz, y, cn_g, cn_b, name):
    S, C = y.shape
    T = _tile(S, 256, 8)

    def body(dz_ref, y_ref, g_ref, bb_ref, dy_ref, dg_ref, db_ref):
        @pl.when(pl.program_id(0) == 0)
        def _():
            dg_ref[...] = jnp.zeros_like(dg_ref)
            db_ref[...] = jnp.zeros_like(db_ref)

        yv = y_ref[...]
        mu = jnp.mean(yv, axis=-1, keepdims=True)
        xc = yv - mu
        rstd = lax.rsqrt(jnp.mean(xc * xc, axis=-1, keepdims=True) + EPS)
        xh = xc * rstd
        ln = xh * g_ref[...] + bb_ref[...]
        sg = jax.nn.sigmoid(ln)
        dln = dz_ref[...] * (sg * (1.0 + ln * (1.0 - sg)))
        dg_ref[...] += jnp.sum(dln * xh, axis=0, keepdims=True)
        db_ref[...] += jnp.sum(dln, axis=0, keepdims=True)
        dxh = dln * g_ref[...]
        dy_ref[...] = rstd * (dxh - jnp.mean(dxh, axis=-1, keepdims=True)
                              - xh * jnp.mean(dxh * xh, axis=-1, keepdims=True))

    row = pl.BlockSpec((T, C), lambda i: (i, 0))
    vec = pl.BlockSpec((1, C), lambda i: (0, 0))
    return pl.pallas_call(
        body, name=name, grid=(S // T,),
        in_specs=[row, row, vec, vec], out_specs=[row, vec, vec],
        out_shape=[jax.ShapeDtypeStruct((S, C), F32), jax.ShapeDtypeStruct((1, C), F32),
                   jax.ShapeDtypeStruct((1, C), F32)],
        compiler_params=_params("arbitrary"),
    )(dz, y, cn_g, cn_b)


def _conv_bwd_taps(u, dy, conv_w, name):
    S, C = dy.shape
    T = _tile(S, 256, CONV_HALO)
    hb = T // CONV_HALO
    nt = S // T
    ns = C // LANE
    W1 = CONV_WIDTH - 1

    def body(av_ref, ag_ref, pv_ref, pg_ref, dy_ref, dn_ref, w_ref, dv_ref, dg_ref, dw_ref, db_ref, a_scr, d_scr):
        i = pl.program_id(1)

        @pl.when(i == 0)
        def _():
            dw_ref[...] = jnp.zeros_like(dw_ref)
            db_ref[...] = jnp.zeros_like(db_ref)

        av, sg = av_ref[...], jax.nn.sigmoid(ag_ref[...])
        prev = pv_ref[...] * jax.nn.sigmoid(pg_ref[...])
        a_scr[0:CONV_HALO, :] = jnp.where(i > 0, prev, 0.0)
        a_scr[CONV_HALO:CONV_HALO + T, :] = av * sg
        dyv = dy_ref[...]
        d_scr[0:T, :] = dyv
        d_scr[T:T + CONV_HALO, :] = jnp.where(i < nt - 1, dn_ref[...], 0.0)
        da = jnp.zeros((T, LANE), F32)
        for j in range(CONV_WIDTH):
            da = da + w_ref[j:j + 1, :] * d_scr[pl.ds(W1 - j, T), :]
            dw_ref[j:j + 1, :] += jnp.sum(dyv * a_scr[pl.ds(CONV_HALO - W1 + j, T), :], axis=0, keepdims=True)
        db_ref[...] += jnp.sum(dyv, axis=0, keepdims=True)
        dv_ref[...] = (da * sg).astype(BF16)
        dg_ref[...] = (da * av * sg * (1.0 - sg)).astype(BF16)

    def cur(part):
        return pl.BlockSpec((T, LANE), lambda cb, i: (i, part * ns + cb))

    def halo(part):
        return pl.BlockSpec((CONV_HALO, LANE), lambda cb, i: (jnp.maximum(i * hb - 1, 0), part * ns + cb))

    nxt = pl.BlockSpec((CONV_HALO, LANE), lambda cb, i: (jnp.minimum((i + 1) * hb, S // CONV_HALO - 1), cb))
    row = pl.BlockSpec((T, LANE), lambda cb, i: (i, cb))
    return pl.pallas_call(
        body, name=name, grid=(ns, nt),
        in_specs=[cur(0), cur(1), halo(0), halo(1), row, nxt, pl.BlockSpec((CONV_WIDTH, LANE), lambda cb, i: (0, cb))],
        out_specs=[row, row, pl.BlockSpec((CONV_HALO, LANE), lambda cb, i: (0, cb)),
                   pl.BlockSpec((1, LANE), lambda cb, i: (0, cb))],
        out_shape=[jax.ShapeDtypeStruct((S, C), BF16), jax.ShapeDtypeStruct((S, C), BF16),
                   jax.ShapeDtypeStruct((CONV_HALO, C), F32), jax.ShapeDtypeStruct((1, C), F32)],
        scratch_shapes=[pltpu.VMEM((T + CONV_HALO, LANE), F32), pltpu.VMEM((T + CONV_HALO, LANE), F32)],
        compiler_params=_params("parallel", "arbitrary"),
    )(u, u, u, u, dy, dy, conv_w)


def _rope_tables(S):
    half = HEAD_DIM // 2
    inv = jnp.exp(-math.log(ROPE_THETA) * jnp.arange(half, dtype=F32) / half)
    ang = jnp.arange(S, dtype=jnp.int32).astype(F32)[:, None] * inv[None, :]
    cos, sin = jnp.cos(ang), jnp.sin(ang)
    return jnp.concatenate([cos, cos], axis=1), jnp.concatenate([-sin, sin], axis=1)


def _qkv_prep(u, qn_g, kn_g, cos, sin, cb0, name):
    S = u.shape[0]
    A = (u.shape[1] // (cb0 + 3))
    H = A // HEAD_DIM
    T = _tile(S, 256, SUBLANE_BF16)
    scale = HEAD_DIM ** -0.5

    def body(q_ref, k_ref, v_ref, qg_ref, kg_ref, cos_ref, sin_ref, qo_ref, ko_ref, vo_ref):
        cosv, sinv = cos_ref[...], sin_ref[...]
        for h in range(H):
            sl = slice(h * HEAD_DIM, (h + 1) * HEAD_DIM)
            for x_ref, g_ref, o_ref, sc in ((q_ref, qg_ref, qo_ref, scale), (k_ref, kg_ref, ko_ref, 1.0)):
                xv = x_ref[:, sl]
                xn = xv * lax.rsqrt(jnp.mean(xv * xv, axis=-1, keepdims=True) + EPS) * g_ref[...]
                y = xn * cosv + pltpu.roll(xn, HEAD_DIM // 2, 1) * sinv
                o_ref[:, sl] = (y * sc).astype(BF16)
        vo_ref[...] = v_ref[...].astype(BF16)

    def col(cb):
        return pl.BlockSpec((T, A), lambda i: (i, cb))

    vec = pl.BlockSpec((1, HEAD_DIM), lambda i: (0, 0))
    tab = pl.BlockSpec((T, HEAD_DIM), lambda i: (i, 0))
    out = pl.BlockSpec((T, A), lambda i: (i, 0))
    return pl.pallas_call(
        body, name=name, grid=(S // T,),
        in_specs=[col(cb0), col(cb0 + 1), col(cb0 + 2), vec, vec, tab, tab],
        out_specs=[out] * 3, out_shape=[jax.ShapeDtypeStruct((S, A), BF16)] * 3,
        compiler_params=_params("parallel"),
    )(u, u, u, qn_g, kn_g, cos, sin)


def _qkv_prep_bwd(u, dqs, dks, dvs, qn_g, kn_g, cos, sin, cb0, name):
    S = u.shape[0]
    A = dqs[0].shape[1]
    H = A // HEAD_DIM
    T = _tile(S, 256, SUBLANE_BF16)
    nb = len(dqs)
    scale = HEAD_DIM ** -0.5

    def body(*refs):
        q_ref, k_ref, qg_ref, kg_ref, cos_ref, sin_ref = refs[:6]
        dq_refs, dk_refs, dv_refs = refs[6:6 + nb], refs[6 + nb:6 + 2 * nb], refs[6 + 2 * nb:6 + 3 * nb]
        dqo_ref, dko_ref, dvo_ref, dqg_ref, dkg_ref = refs[6 + 3 * nb:]

        @pl.when(pl.program_id(0) == 0)
        def _():
            dqg_ref[...] = jnp.zeros_like(dqg_ref)
            dkg_ref[...] = jnp.zeros_like(dkg_ref)

        cosv, sinv = cos_ref[...], sin_ref[...]
        for h in range(H):
            sl = slice(h * HEAD_DIM, (h + 1) * HEAD_DIM)
            for x_ref, g_ref, d_refs, o_ref, dg_ref, sc in ((q_ref, qg_ref, dq_refs, dqo_ref, dqg_ref, scale),
                                                          (k_ref, kg_ref, dk_refs, dko_ref, dkg_ref, 1.0)):
                dy = d_refs[0][:, sl]
                for r in d_refs[1:]:
                    dy = dy + r[:, sl]
                dy = dy * sc
                dxn = dy * cosv + pltpu.roll(dy * sinv, HEAD_DIM // 2, 1)
                xv = x_ref[:, sl]
                r = lax.rsqrt(jnp.mean(xv * xv, axis=-1, keepdims=True) + EPS)
                xh = xv * r
                dg_ref[...] += jnp.sum(dxn * xh, axis=0, keepdims=True)
                dxh = dxn * g_ref[...]
                o_ref[:, sl] = (r * (dxh - xh * jnp.mean(dxh * xh, axis=-1, keepdims=True))).astype(BF16)
        dv = dv_refs[0][...]
        for r in dv_refs[1:]:
            dv = dv + r[...]
        dvo_ref[...] = dv.astype(BF16)

    def col(cb):
        return pl.BlockSpec((T, A), lambda i: (i, cb))

    vec = pl.BlockSpec((1, HEAD_DIM), lambda i: (0, 0))
    tab = pl.BlockSpec((T, HEAD_DIM), lambda i: (i, 0))
    row = pl.BlockSpec((T, A), lambda i: (i, 0))
    return pl.pallas_call(
        body, name=name, grid=(S // T,),
        in_specs=[col(cb0), col(cb0 + 1), vec, vec, tab, tab] + [row] * (3 * nb),
        out_specs=[row, row, row, vec, vec],
        out_shape=[jax.ShapeDtypeStruct((S, A), BF16)] * 3 + [jax.ShapeDtypeStruct((1, HEAD_DIM), F32)] * 2,
        compiler_params=_params("arbitrary"),
    )(u, u, qn_g, kn_g, cos, sin, *dqs, *dks, *dvs)


ATT_TILE = 256
NEG = -1e30


def _attn_bias(tile):
    span = max(window for window, _ in DIL_PATTERNS)
    nw = -(-span // tile) + 1
    dist = (jnp.arange(nw)[:, None, None] * tile + jnp.arange(tile)[None, :, None] - jnp.arange(tile)[None, None, :])
    mult = sum(((dist >= 0) & (dist <= window) & (dist % dil == 0)).astype(F32) for window, dil in DIL_PATTERNS)
    return jnp.where(mult > 0, jnp.log(jnp.maximum(mult, 1.0)), NEG)


def _attn_fwd(q, k, v, bias, name):
    S, A = q.shape
    H = A // HEAD_DIM
    nw, T, _ = bias.shape
    nq = S // T

    def body(q_ref, k_ref, v_ref, b_ref, ob_ref, of_ref, l_ref, s_scr):
        i = pl.program_id(1)
        qv = q_ref[...]
        mx = jnp.full((T, 1), NEG, F32)
        for w in range(nw):
            blk = i - w
            start = pl.multiple_of(jnp.maximum(blk, 0) * T, T)
            s = _dot(qv, k_ref[pl.ds(start, T), :], "nt") + b_ref[w] + jnp.where(blk >= 0, 0.0, NEG)
            s_scr[w] = s
            mx = jnp.maximum(mx, jnp.max(s, axis=-1, keepdims=True))
        den = jnp.zeros((T, 1), F32)
        o = jnp.zeros((T, HEAD_DIM), F32)
        for w in range(nw):
            start = pl.multiple_of(jnp.maximum(i - w, 0) * T, T)
            p = jnp.exp(s_scr[w] - mx)
            den = den + jnp.sum(p, axis=-1, keepdims=True)
            o = o + _dot(p.astype(BF16), v_ref[pl.ds(start, T), :], "nn")
        o = o / den
        ob_ref[...] = o.astype(BF16)
        of_ref[...] = o
        l_ref[...] = mx + jnp.log(den)

    blk = pl.BlockSpec((T, HEAD_DIM), lambda h, i: (i, h))
    full = pl.BlockSpec((S, HEAD_DIM), lambda h, i: (0, h))
    return pl.pallas_call(
        body, name=name, grid=(H, nq),
        in_specs=[blk, full, full, pl.BlockSpec((nw, T, T), lambda h, i: (0, 0, 0))],
        out_specs=[blk, blk, pl.BlockSpec((None, T, 1), lambda h, i: (h, i, 0))],
        out_shape=[jax.ShapeDtypeStruct((S, A), BF16), jax.ShapeDtypeStruct((S, A), F32),
                   jax.ShapeDtypeStruct((H, S, 1), F32)],
        scratch_shapes=[pltpu.VMEM((nw, T, T), F32)],
        compiler_params=_params("parallel", "arbitrary"),
    )(q, k, v, bias)


def _attn_dq(q, k, v, dz, cb0, o, lse, bias, name):
    S, A = q.shape
    H = A // HEAD_DIM
    nw, T, _ = bias.shape
    nq = S // T

    def body(q_ref, k_ref, v_ref, do_ref, o_ref, l_ref, b_ref, dq_ref, d_ref):
        i = pl.program_id(1)
        qv, dof = q_ref[...], do_ref[...]
        dov = dof.astype(BF16)
        delta = jnp.sum(dof * o_ref[...], axis=-1, keepdims=True)
        d_ref[...] = delta
        lv = l_ref[...]
        dq = jnp.zeros((T, HEAD_DIM), F32)
        for w in range(nw):
            blk = i - w
            start = pl.multiple_of(jnp.maximum(blk, 0) * T, T)
            kv = k_ref[pl.ds(start, T), :]
            s = _dot(qv, kv, "nt") + b_ref[w] + jnp.where(blk >= 0, 0.0, NEG)
            p = jnp.exp(s - lv)
            ds = (p * (_dot(dov, v_ref[pl.ds(start, T), :], "nt") - delta)).astype(BF16)
            dq = dq + _dot(ds, kv, "nn")
        dq_ref[...] = dq

    blk = pl.BlockSpec((T, HEAD_DIM), lambda h, i: (i, h))
    full = pl.BlockSpec((S, HEAD_DIM), lambda h, i: (0, h))
    col = pl.BlockSpec((None, T, 1), lambda h, i: (h, i, 0))
    return pl.pallas_call(
        body, name=name, grid=(H, nq),
        in_specs=[blk, full, full, pl.BlockSpec((T, HEAD_DIM), lambda h, i: (i, cb0 + h)), blk, col,
                  pl.BlockSpec((nw, T, T), lambda h, i: (0, 0, 0))],
        out_specs=[blk, col],
        out_shape=[jax.ShapeDtypeStruct((S, A), F32), jax.ShapeDtypeStruct((H, S, 1), F32)],
        compiler_params=_params("parallel", "arbitrary"),
    )(q, k, v, dz, o, lse, bias)


def _attn_dkv(q, k, v, dz, cb0, lse, delta, bias, name):
    S, A = q.shape
    H = A // HEAD_DIM
    nw, T, _ = bias.shape
    nq = S // T

    def body(k_ref, v_ref, q_ref, do_ref, l_ref, d_ref, b_ref, dk_ref, dv_ref):
        m = pl.program_id(1)
        kv, vv = k_ref[...], v_ref[...]
        dk = jnp.zeros((T, HEAD_DIM), F32)
        dv = jnp.zeros((T, HEAD_DIM), F32)
        for w in range(nw):
            blk = m + w
            start = pl.multiple_of(jnp.minimum(blk, nq - 1) * T, T)
            qv = q_ref[pl.ds(start, T), :]
            dov = do_ref[pl.ds(start, T), :].astype(BF16)
            s = _dot(qv, kv, "nt") + b_ref[w] + jnp.where(blk < nq, 0.0, NEG)
            p = jnp.exp(s - l_ref[pl.ds(start, T), :])
            dv = dv + _dot(p.astype(BF16), dov, "tn")
            ds = (p * (_dot(dov, vv, "nt") - d_ref[pl.ds(start, T), :])).astype(BF16)
            dk = dk + _dot(ds, qv, "tn")
        dk_ref[...] = dk
        dv_ref[...] = dv

    blk = pl.BlockSpec((T, HEAD_DIM), lambda h, m: (m, h))
    full = pl.BlockSpec((S, HEAD_DIM), lambda h, m: (0, h))
    col = pl.BlockSpec((None, S, 1), lambda h, m: (h, 0, 0))
    sds = jax.ShapeDtypeStruct((S, A), F32)
    return pl.pallas_call(
        body, name=name, grid=(H, nq),
        in_specs=[blk, blk, full, pl.BlockSpec((S, HEAD_DIM), lambda h, m: (0, cb0 + h)), col, col,
                  pl.BlockSpec((nw, T, T), lambda h, m: (0, 0, 0))],
        out_specs=[blk, blk], out_shape=[sds, sds],
        compiler_params=_params("parallel", "arbitrary"),
    )(k, v, q, dz, lse, delta, bias)


def _even_mixer(u, conv_w, conv_b, cn_g, cn_b, qn_g, kn_g, tag):
    S = u.shape[0]
    C = conv_w.shape[1]
    A = (u.shape[1] - 2 * C) // 3
    assert A == C, "column-block addressing of u assumes equal conv and attention widths"
    T = _tile(S, ATT_TILE, LANE)
    cos, sin = _rope_tables(S)
    bias = _attn_bias(T)
    a_out, y = _conv_fwd(u, conv_w, conv_b, cn_g, cn_b, "conv_fwd" + tag)
    q, k, v = _qkv_prep(u, qn_g, kn_g, cos, sin, 2, "qkv_prep" + tag)
    ob, of, lse = _attn_fwd(q, k, v, bias, "attn_fwd" + tag)
    z = jnp.concatenate([a_out, ob], axis=1)

    def backward(dz):
        dy, d_cn_g, d_cn_b = _conv_bwd_norm(dz, y, cn_g, cn_b, "conv_bwd_norm" + tag)
        d_val, d_gate, d_w, d_b = _conv_bwd_taps(u, dy, conv_w, "conv_bwd_taps" + tag)
        dqp, delta = _attn_dq(q, k, v, dz, C // HEAD_DIM, of, lse, bias, "attn_dq" + tag)
        dkp, dvp = _attn_dkv(q, k, v, dz, C // HEAD_DIM, lse, delta, bias, "attn_dkv" + tag)
        dq, dk, dv, d_qn, d_kn = _qkv_prep_bwd(u, [dqp], [dkp], [dvp], qn_g, kn_g, cos, sin, 2, "qkv_prep_bwd" + tag)
        du = jnp.concatenate([d_val, d_gate, dq, dk, dv], axis=1)
        return du, [d_w[:CONV_WIDTH], d_b[0], d_cn_g[0], d_cn_b[0], d_qn[0], d_kn[0]]

    return z, backward


_LEVELS = (128, 64, 32, 16, 8, 4, 2, 1)


def _chunk_cumsum(g, rows, reverse=False):
    C = g.shape[0]
    d = 1
    while d < C:
        if reverse:
            g = g + jnp.where(rows < C - d, pltpu.roll(g, C - d, 0), 0.0)
        else:
            g = g + jnp.where(rows >= d, pltpu.roll(g, d, 0), 0.0)
        d *= 2
    return g


def _level_ref(b, b_scr, rows, m):
    C = b.shape[0]
    if m >= 8:
        pieces = [jnp.broadcast_to(b_scr[2 * m * j + m - 1:2 * m * j + m, :], (2 * m, LANE)) for j in range(C // (2 * m))]
        return pieces[0] if len(pieces) == 1 else jnp.concatenate(pieces, axis=0)
    pos = rows & (2 * m - 1)
    ref = b
    for p in range(2 * m):
        if p != m - 1:
            ref = jnp.where(pos == p, pltpu.roll(b, (p - (m - 1)) % C, 0), ref)
    return ref


def _level_operands(q, k, b, b_scr, rows, m):
    ref = _level_ref(b, b_scr, rows, m)
    qs = (q * jnp.exp(jnp.minimum(b - ref, 0.0))).astype(BF16)
    ks = (k * jnp.exp(jnp.minimum(ref - b, 0.0))).astype(BF16)
    return qs, ks


def _split2(x):
    hi = x.astype(BF16)
    lo = (x - hi.astype(F32)).astype(BF16)
    return jnp.concatenate([hi, lo], axis=1)


def _level_table(n):
    t = jnp.arange(n, dtype=jnp.int32)[:, None]
    s = jnp.arange(n, dtype=jnp.int32)[None, :]
    x = t ^ s
    lvl = sum((x >= (1 << j)).astype(jnp.int32) for j in range(1, n.bit_length()))
    return jnp.where(t > s, lvl, jnp.where(t == s, -1, -2))


def _hgrn_gates(qz, fz, la, lc, oml):
    sq = jax.nn.sigmoid(qz)
    q = qz * sq
    s = jax.nn.sigmoid(fz)
    c = lc + jnp.minimum(fz, 0.0) - jnp.log(1.0 + jnp.exp(-jnp.abs(fz)))
    mx = jnp.maximum(la, c)
    g = mx + jnp.log(1.0 + jnp.exp(-jnp.abs(la - c)))
    k = oml * (1.0 - s)
    return q, sq, k, s, g, c


def _hgrn_fwd(u, la, lc, oml, gn_g, name):
    S = u.shape[0]
    W = u.shape[1] // 4
    H = W // HGRN_KDIM
    C = min(HGRN_CHUNK, S)
    nc = S // C
    levels = [m for m in _LEVELS if m < C]
    HB = C // 2

    def body(qz_ref, fz_ref, iz_ref, gz_ref, la_ref, lc_ref, oml_ref, gn_ref, lvl_ref,
             z_ref, o_ref, a_ref, st_ref, state, b_scr):
        @pl.when(pl.program_id(1) == 0)
        def _():
            state[...] = jnp.zeros_like(state)

        rows = lax.broadcasted_iota(jnp.int32, (C, LANE), 0)
        q, _, k, _, g, _ = _hgrn_gates(qz_ref[...], fz_ref[...], la_ref[...], lc_ref[...], oml_ref[...])
        v = iz_ref[...].astype(BF16)
        b = _chunk_cumsum(g, rows)
        b_scr[...] = b
        lvl = lvl_ref[...]
        qk = jnp.sum(q * k, axis=-1, keepdims=True)
        diag = [jnp.where(lvl == -1, qk[r * HB:(r + 1) * HB], 0.0) for r in range(2)]
        for m in levels[1:]:
            qs, ks = _level_operands(q, k, b, b_scr, rows, m)
            for r in range(2):
                sl = slice(r * HB, (r + 1) * HB)
                diag[r] = jnp.where(lvl == m.bit_length() - 1, _dot(qs[sl], ks[sl], "nt"), diag[r])
        qs, ks = _level_operands(q, k, b, b_scr, rows, HB)
        low = _dot(qs[HB:], ks[:HB], "nt")
        a = jnp.concatenate([jnp.concatenate([diag[0], jnp.zeros((HB, HB), F32)], axis=1),
                             jnp.concatenate([low, diag[1]], axis=1)], axis=0)
        ab = a.astype(BF16)
        a_ref[...] = ab
        st = state[...]
        st_ref[...] = st
        o = _dot(ab, v, "nn") + _dot((q * jnp.exp(b)).astype(BF16), st.astype(BF16), "nt")
        bl = b_scr[C - 1:C, :]
        kh = (k * jnp.exp(bl - b)).astype(BF16)
        state[...] = st * jnp.exp(bl) + _dot(v, kh, "tn")
        o_ref[...] = o
        r = lax.rsqrt(jnp.mean(o * o, axis=-1, keepdims=True) + EPS)
        gz = gz_ref[...]
        z_ref[...] = (o * r * gn_ref[...] * (gz * jax.nn.sigmoid(gz))).astype(BF16)

    def col(off):
        return pl.BlockSpec((C, LANE), lambda h, i: (i, off * H + h))

    vec = pl.BlockSpec((1, LANE), lambda h, i: (0, h))
    tile = pl.BlockSpec((C, LANE), lambda h, i: (i, h))
    return pl.pallas_call(
        body, name=name, grid=(H, nc),
        in_specs=[col(0), col(1), col(2), col(3), vec, vec, vec, vec, pl.BlockSpec((HB, HB), lambda h, i: (0, 0))],
        out_specs=[tile, tile, pl.BlockSpec((None, C, C), lambda h, i: (h, i, 0)),
                   pl.BlockSpec((None, None, LANE, LANE), lambda h, i: (h, i, 0, 0))],
        out_shape=[jax.ShapeDtypeStruct((S, W), BF16), jax.ShapeDtypeStruct((S, W), F32),
                   jax.ShapeDtypeStruct((H, S, C), BF16), jax.ShapeDtypeStruct((H, nc, LANE, LANE), F32)],
        scratch_shapes=[pltpu.VMEM((LANE, LANE), F32), pltpu.VMEM((C, LANE), F32)],
        compiler_params=_params("parallel", "arbitrary"),
    )(u, u, u, u, la, lc, oml, gn_g, _level_table(HB))


def _hgrn_bwd(u, la, lc, oml, gn_g, o, a, st, dz, name):
    S = u.shape[0]
    W = u.shape[1] // 4
    H = W // HGRN_KDIM
    C = min(HGRN_CHUNK, S)
    nc = S // C
    levels = [m for m in _LEVELS if m < C]
    HB = C // 2

    def body(qz_ref, fz_ref, iz_ref, gz_ref, la_ref, lc_ref, oml_ref, gn_ref, o_ref, a_ref, st_ref, dz_ref, lvl_ref,
             dqz_ref, dfz_ref, diz_ref, dgz_ref, dla_ref, dlc_ref, doml_ref, dgn_ref, dstate, b_scr):
        @pl.when(pl.program_id(1) == 0)
        def _():
            dstate[...] = jnp.zeros_like(dstate)
            dla_ref[...] = jnp.zeros_like(dla_ref)
            dlc_ref[...] = jnp.zeros_like(dlc_ref)
            doml_ref[...] = jnp.zeros_like(doml_ref)
            dgn_ref[...] = jnp.zeros_like(dgn_ref)

        rows = lax.broadcasted_iota(jnp.int32, (C, LANE), 0)
        la_v, lc_v, oml_v = la_ref[...], lc_ref[...], oml_ref[...]
        qz, fz = qz_ref[...], fz_ref[...]
        q, sq, k, s, g, c = _hgrn_gates(qz, fz, la_v, lc_v, oml_v)
        vf = iz_ref[...]
        v = vf.astype(BF16)

        ov, gz, dzv, gn = o_ref[...], gz_ref[...], dz_ref[...], gn_ref[...]
        r = lax.rsqrt(jnp.mean(ov * ov, axis=-1, keepdims=True) + EPS)
        on = ov * r
        sg = jax.nn.sigmoid(gz)
        silu_g = gz * sg
        dgn_ref[...] += jnp.sum(dzv * on * silu_g, axis=0, keepdims=True)
        dgz_ref[...] = (dzv * on * gn * (sg * (1.0 + gz * (1.0 - sg)))).astype(BF16)
        don = dzv * gn * silu_g
        do_f = r * (don - on * jnp.mean(don * on, axis=-1, keepdims=True))
        do = do_f.astype(BF16)

        b = _chunk_cumsum(g, rows)
        b_scr[...] = b
        bl = b_scr[C - 1:C, :]
        e = jnp.exp(b)
        ebl = jnp.exp(bl)
        ekl = jnp.exp(bl - b)
        qh = q * e
        kh = k * ekl
        st_v = st_ref[...]
        dst = dstate[...]
        dstb = dst.astype(BF16)

        diz_ref[...] = (_dot(a_ref[...], do, "tn") + _dot(kh.astype(BF16), dstb, "nt")).astype(BF16)
        da = _dot(do, v, "nt")
        dqh = _dot(do, st_v.astype(BF16), "nn")
        dkh = _dot(v, dstb, "nn")
        dstate[...] = dst * ebl + _dot(do, qh.astype(BF16), "tn")
        dbl = jnp.sum(dkh * kh, axis=0, keepdims=True) + jnp.sum(dst * st_v, axis=0, keepdims=True) * ebl

        datt = jnp.sum(do_f * vf, axis=-1, keepdims=True)
        dqa = datt * k
        dka = datt * q
        lvl = lvl_ref[...]
        for m in levels:
            ref = _level_ref(b, b_scr, rows, m)
            eu = jnp.exp(jnp.minimum(b - ref, 0.0))
            el = jnp.exp(jnp.minimum(ref - b, 0.0))
            ks2, qs2 = _split2(k * el), _split2(q * eu)
            if m == HB:
                gm = da[HB:, :HB].astype(BF16)
                pq = jnp.concatenate([jnp.zeros((HB, 2 * LANE), F32), _dot(gm, ks2[:HB], "nn")], axis=0)
                pk = jnp.concatenate([_dot(gm, qs2[HB:], "tn"), jnp.zeros((HB, 2 * LANE), F32)], axis=0)
            else:
                gms = [jnp.where(lvl == m.bit_length() - 1, da[r * HB:(r + 1) * HB, r * HB:(r + 1) * HB], 0.0).astype(BF16)
                       for r in range(2)]
                pq = jnp.concatenate([_dot(gms[r], ks2[r * HB:(r + 1) * HB], "nn") for r in range(2)], axis=0)
                pk = jnp.concatenate([_dot(gms[r], qs2[r * HB:(r + 1) * HB], "tn") for r in range(2)], axis=0)
            dqa += (pq[:, :LANE] + pq[:, LANE:]) * eu
            dka += (pk[:, :LANE] + pk[:, LANE:]) * el
        db = q * dqa - k * dka + dqh * qh - dkh * kh
        db = db + jnp.where(rows == C - 1, dbl, 0.0)
        dq = dqa + dqh * e
        dk = dka + dkh * ekl
        dg = _chunk_cumsum(db, rows, reverse=True)

        wa = jnp.exp(la_v - g)
        wc = jnp.exp(c - g)
        dqz_ref[...] = (dq * (sq * (1.0 + qz * (1.0 - sq)))).astype(BF16)
        dfz_ref[...] = (dg * wc * (1.0 - s) - dk * oml_v * s * (1.0 - s)).astype(BF16)
        dla_ref[...] += jnp.sum(dg * wa, axis=0, keepdims=True)
        dlc_ref[...] += jnp.sum(dg * wc, axis=0, keepdims=True)
        doml_ref[...] += jnp.sum(dk * (1.0 - s), axis=0, keepdims=True)

    def col(off):
        return pl.BlockSpec((C, LANE), lambda h, i: (nc - 1 - i, off * H + h))

    vec = pl.BlockSpec((1, LANE), lambda h, i: (0, h))
    tile = pl.BlockSpec((C, LANE), lambda h, i: (nc - 1 - i, h))
    a_spec = pl.BlockSpec((None, C, C), lambda h, i: (h, nc - 1 - i, 0))
    st_spec = pl.BlockSpec((None, None, LANE, LANE), lambda h, i: (h, nc - 1 - i, 0, 0))
    sw = jax.ShapeDtypeStruct((S, W), BF16)
    vw = jax.ShapeDtypeStruct((1, W), F32)
    return pl.pallas_call(
        body, name=name, grid=(H, nc),
        in_specs=[col(0), col(1), col(2), col(3), vec, vec, vec, vec, tile, a_spec, st_spec, tile,
                  pl.BlockSpec((HB, HB), lambda h, i: (0, 0))],
        out_specs=[tile, tile, tile, tile, vec, vec, vec, vec],
        out_shape=[sw, sw, sw, sw, vw, vw, vw, vw],
        scratch_shapes=[pltpu.VMEM((LANE, LANE), F32), pltpu.VMEM((C, LANE), F32)],
        compiler_params=_params("parallel", "arbitrary"),
    )(u, u, u, u, la, lc, oml, gn_g, o, a, st, dz, _level_table(HB))


def _lb_terms(lb_logits, layer):
    p = jax.nn.softmax(lb_logits, axis=0)
    lb = (jnp.cumsum(p, axis=0) - p[0:1])[layer]
    return jnp.log(lb)[None], jnp.log1p(-lb)[None], (1.0 - lb)[None]


def kernel(x, norm_ffn1, ffn1_wg, ffn1_wu, ffn1_wd, norm_mix, norm_ffn2, ffn2_wg, ffn2_wu, ffn2_wd, ev_w_in, ev_conv_w, ev_conv_b, ev_cn_g, ev_cn_b, ev_qn_g, ev_kn_g, ev_w_out, od_w_in, od_lb_logits, od_gn_g, od_w_out, loss_target, m_norm_ffn1, m_ffn1_wg, m_ffn1_wu, m_ffn1_wd, m_norm_mix, m_norm_ffn2, m_ffn2_wg, m_ffn2_wu, m_ffn2_wd, m_ev_w_in, m_ev_conv_w, m_ev_conv_b, m_ev_cn_g, m_ev_cn_b, m_ev_qn_g, m_ev_kn_g, m_ev_w_out, m_od_w_in, m_od_lb_logits, m_od_gn_g, m_od_w_out, v_norm_ffn1, v_ffn1_wg, v_ffn1_wu, v_ffn1_wd, v_norm_mix, v_norm_ffn2, v_ffn2_wg, v_ffn2_wu, v_ffn2_wd, v_ev_w_in, v_ev_conv_w, v_ev_conv_b, v_ev_cn_g, v_ev_cn_b, v_ev_qn_g, v_ev_kn_g, v_ev_w_out, v_od_w_in, v_od_lb_logits, v_od_gn_g, v_od_w_out):
    depth = norm_ffn1.shape[0]
    S, D = x.shape[1], x.shape[2]
    xi, yi, ci = _me()
    dev = 4 * xi + 2 * yi + ci
    c_idx = jnp.reshape(ci, (1,)).astype(jnp.int32)
    k_idx = jnp.reshape(2 * xi + yi, (1,)).astype(jnp.int32)

    def ffn_shard(wg, wu, wd, l):
        return jnp.stack([wg[l].T, wu[l].T, wd[l]]).astype(BF16)

    assert depth == 2, "the exchange schedule below is written for one even and one odd layer"
    sh_ffn1 = [ffn_shard(ffn1_wg, ffn1_wu, ffn1_wd, l) for l in range(depth)]
    sh_ffn2 = [ffn_shard(ffn2_wg, ffn2_wu, ffn2_wd, l) for l in range(depth)]
    sh_ev = [ev_w_in[0].T.astype(BF16)[None], ev_w_out[0].astype(BF16)[None]]
    sh_od = [od_w_in[0].T.astype(BF16)[None], od_w_out[0].astype(BF16)[None]]

    def full(g):
        return g.reshape(g.shape[0], N_DEV * g.shape[2], g.shape[3])

    def gather_begin(shards, after, tag):
        lands = [lax.dynamic_update_slice(lax.empty((s.shape[0], N_DEV) + s.shape[1:], s.dtype), s[:, None],
                                          (0, dev, 0, 0)) for s in shards]
        state = _push_start(shards, lands, _gather_plan, after, "gather_start" + tag)
        return state, state[4][0, 0]

    def gather_arrived(state, after, tag):
        send, recv, srcs, lands, _ = state
        _, lands = _push_wait(send, recv, srcs, lands, _gather_plan, after, "gather_wait" + tag)
        state = _push_start([], lands, _forward_plan, None, "forward_start" + tag)
        return state, state[4][0, 0]

    def gather_done(state, after, tag):
        send, recv, _, lands, _ = state
        _, lands = _push_wait(send, recv, [], lands, _forward_plan, after, "forward_wait" + tag)
        return [full(g) for g in lands]

    def gather_end(state, after, tag):
        state, _ = gather_arrived(state, after, tag)
        return gather_done(state, state[4], tag)

    w_ffn1, w_ffn2 = [None] * depth, [None] * depth
    pending, after = {}, None
    for key, shards in (("0", [sh_ffn1[0]]), ("1", sh_ev), ("2", [sh_ffn2[0]]), ("3", [sh_ffn1[1]]), ("4", sh_od),
                        ("5", [sh_ffn2[1]])):
        pending[key], _ = gather_begin(shards, after, "_" + key)
        after = pending[key][4]
    start_tok = after[0, 0]

    conv_w_sh, gn_g_sh = ev_conv_w[0], od_gn_g[0]
    cw, cs = conv_w_sh.shape[0], conv_w_sh.shape[1]
    gs = gn_g_sh.shape[0]
    conv_w_z = lax.dynamic_update_slice(jnp.zeros((cw, N_DEV * cs), F32), conv_w_sh, (0, dev * cs))
    gn_g_z = lax.dynamic_update_slice(jnp.zeros((N_DEV * gs,), F32), gn_g_sh, (dev * gs,))
    conv_w_full, gn_g_full = _unpack_rows(
        _all_reduce_small(_pack_rows([conv_w_z, gn_g_z]), "gather_small_params"),
        [conv_w_z.shape, gn_g_z.shape])
    (w_ffn1[0],) = gather_end(pending.pop("0"), [after, conv_w_full], "_0")

    def odd_mixer(u, l, tok):
        (la, lc, oml), lb_vjp = jax.vjp(functools.partial(_lb_terms, layer=l), od_lb_logits)
        gn = (gn_g_full + tok)[None]
        zb, o_raw, scores, states = _hgrn_fwd(u, la, lc, oml, gn, f"hgrn_fwd{l}")

        def backward(dz):
            dqz, dfz, diz, dgz, dla, dlc, doml, dgn = _hgrn_bwd(
                u, la, lc, oml, gn, o_raw, scores, states, dz, f"hgrn_bwd{l}")
            (g_lb,) = lb_vjp((dla, dlc, doml))
            return jnp.concatenate([dqz, dfz, diz, dgz], axis=1), [g_lb, dgn[0]]

        return zb, backward

    saved = []
    h = x[0]
    hn = _rms_fwd(h, (norm_ffn1[0] + start_tok)[None], "rms_a0")
    for l in range(depth):
        ffn, gu = _ffn_fwd(hn, w_ffn1[l], f"ffn_fwd_a{l}")
        s1 = (h, hn, gu)
        if l == 0:
            w_in, w_out = gather_end(pending.pop("1"), ffn, "_1")
        else:
            w_in, w_out = gather_done(pending.pop("4"), ffn, "_4")
        w_in_t, w_out = w_in[0], w_out[0]
        h, hn = _resid_rms(h, ffn, norm_mix[l][None], f"rms_mix{l}")
        u = _mm(hn, w_in_t, "nt", F32, f"mix_in{l}")
        key = "2" if l == 0 else "5"
        passing, tok = gather_arrived(pending.pop(key), u, "_" + key)
        if l % 2 == 0:
            zb, core_vjp = _even_mixer(u, conv_w_full, ev_conv_b + tok, ev_cn_g, ev_cn_b, ev_qn_g, ev_kn_g, str(l))
        else:
            zb, core_vjp = odd_mixer(u, l, tok)
        h_mix = h
        h = _mm(zb, w_out, "nn", F32, f"mix_out{l}", res=h)
        sm = (h_mix, hn, zb, core_vjp, w_in_t, w_out)
        (w_ffn2[l],) = gather_done(passing, h, "_" + key)
        tok = 0.0
        if l + 1 < depth:
            passing, tok = gather_arrived(pending.pop("3"), w_ffn2[l], "_3")
        hn = _rms_fwd(h, (norm_ffn2[l] + tok)[None], f"rms_b{l}")
        ffn, gu = _ffn_fwd(hn, w_ffn2[l], f"ffn_fwd_b{l}")
        saved.append((s1, sm, (h, hn, gu)))
        if l + 1 < depth:
            (w_ffn1[l + 1],) = gather_done(passing, ffn, "_3")
            pending["4"], tok = gather_arrived(pending.pop("4"), w_ffn1[l + 1], "_4")
            h, hn = _resid_rms(h, ffn, (norm_ffn1[l + 1] + tok)[None], f"rms_a{l + 1}")

    dy, loss_part = _loss_grad(h, ffn, loss_target[0], "loss_grad")

    def halves_begin(parts, tag):
        parts = [g.reshape(g.shape[0], 4, 2, g.shape[1] // N_DEV, g.shape[2]) for g in parts]
        lands = [lax.empty(g.shape[:2] + g.shape[3:], BF16) for g in parts]
        state = _push_start(parts, lands, _halves_plan, None, "halves_start" + tag)
        return state, state[4][0, 0]

    def chips_begin(state, after, tag):
        send, recv, srcs, lands, _ = state
        parts, got = _push_wait(send, recv, srcs, lands, _halves_plan, after, "halves_wait" + tag)
        sums = [_add_core_halves(g, r, c_idx, f"add_core_halves{tag}_{a}") for a, (g, r) in enumerate(zip(parts, got))]
        lands = [lax.empty((3, s.shape[0]) + s.shape[2:], BF16) for s in sums]
        state = _push_start(sums, lands, _chip_plan, None, "reduce_start" + tag)
        return state, state[4][0, 0]

    def reduce_end(state, after, tag):
        send, recv, srcs, lands, _ = state
        sums, got = _push_wait(send, recv, srcs, lands, _chip_plan, after, "reduce_wait" + tag)
        return [_sum_chip_blocks(s, r, k_idx, f"sum_chip_blocks{tag}_{a}") for a, (s, r) in enumerate(zip(sums, got))]

    def ffn_backward(dy, gain, w, sv, tag, on_dw, on_dx=None):
        h_in, hn, gu = sv
        dxn, dout, t = _ffn_bwd_dx(dy, w, gu, "ffn_bwd_dx_" + tag)
        tok = 0.0 if on_dx is None else on_dx(dxn)
        tok = tok + on_dw(_ffn_bwd_dw(hn, dout, t, "ffn_bwd_dw_" + tag))
        dx, dgain = _rms_bwd(h_in, (gain + tok)[None], dxn, dy, "rms_bwd_" + tag)
        return dx, dgain[0]

    g_norm1, g_norm2, g_normm = [None] * depth, [None] * depth, [None] * depth
    small, halves, groups = [None, None], {}, {}

    def start_halves(key, make_parts):
        def hook(dw):
            halves[key], tok = halves_begin(make_parts(dw), "_" + key)
            return tok
        return hook

    def start_chips(key):
        def hook(after):
            groups[key], tok = chips_begin(halves.pop(key), after, "_" + key)
            return tok
        return hook

    for l in reversed(range(depth)):
        s1, (h_mix, hn, zb, core_vjp, w_in_t, w_out), s2 = saved[l]
        if l == 1:
            dy, g_norm2[l] = ffn_backward(dy, norm_ffn2[l], w_ffn2[l], s2, f"b{l}", start_halves("1", lambda dw: [dw]))
        else:
            dy, g_norm2[l] = ffn_backward(dy, norm_ffn2[l], w_ffn2[l], s2, f"b{l}", start_halves("3", lambda dw: [dw]),
                                          start_chips("2"))
        dyb = dy.astype(BF16)
        dz = _mm(dyb, w_out, "nt", F32, f"mix_out_dz{l}")
        dw_out = _mm(zb, dyb, "tn", BF16, f"mix_out_dw{l}")
        dub, small[l % 2] = core_vjp(dz)
        dw_in_t = _mm(dub, hn, "tn", BF16, f"mix_in_dw{l}")
        mix_parts = [dw_in_t[None], dw_out[None]]
        if l == 1:
            tok = start_chips("1")(dw_in_t)
        else:
            tok = start_chips("3")(dw_in_t) + start_halves("4", lambda _: mix_parts)(None)
        dhn = _mm(dub, w_in_t, "nn", F32, f"mix_in_dx{l}")
        dy, gm = _rms_bwd(h_mix, (norm_mix[l] + tok)[None], dhn, dy, f"rms_bwd_mix{l}")
        g_normm[l] = gm[0]
        if l == 1:
            dy, g_norm1[l] = ffn_backward(dy, norm_ffn1[l], w_ffn1[l], s1, f"a{l}",
                                          start_halves("2", lambda dw, od=mix_parts: od + [dw]))
        else:
            dy, g_norm1[l] = ffn_backward(dy, norm_ffn1[l], w_ffn1[l], s1, f"a{l}", start_halves("5", lambda dw: [dw]),
                                          start_chips("4"))
    grad_x = dy[None]
    start_chips("5")(dy)

    done = [dy, groups["5"][4]]
    (g_ffn2_1,) = reduce_end(groups["1"], done, "_1")
    g_od_in_t, g_od_out, g_ffn1_1 = reduce_end(groups["2"], done, "_2")
    (g_ffn2_0,) = reduce_end(groups["3"], done, "_3")
    g_ev_in_t, g_ev_out = reduce_end(groups["4"], done, "_4")
    g_ffn2 = [g_ffn2_0, g_ffn2_1]

    def ffn_grads(gl):
        return (jnp.stack([g[0].T for g in gl]), jnp.stack([g[1].T for g in gl]), jnp.stack([g[2] for g in gl]))

    g_ffn2_wg, g_ffn2_wu, g_ffn2_wd = ffn_grads(g_ffn2)
    grads = [None, None, None, None, None, None, g_ffn2_wg, g_ffn2_wu, g_ffn2_wd,
             g_ev_in_t[0].T[None], None, None, None, None, None,
             None, g_ev_out, g_od_in_t[0].T[None], None, None, g_od_out]
    weights = [norm_ffn1, ffn1_wg, ffn1_wu, ffn1_wd, norm_mix, norm_ffn2, ffn2_wg, ffn2_wu, ffn2_wd, ev_w_in,
               ev_conv_w, ev_conv_b, ev_cn_g, ev_cn_b, ev_qn_g, ev_kn_g, ev_w_out, od_w_in, od_lb_logits,
               od_gn_g, od_w_out]
    moms = [m_norm_ffn1, m_ffn1_wg, m_ffn1_wu, m_ffn1_wd, m_norm_mix, m_norm_ffn2, m_ffn2_wg, m_ffn2_wu,
            m_ffn2_wd, m_ev_w_in, m_ev_conv_w, m_ev_conv_b, m_ev_cn_g, m_ev_cn_b, m_ev_qn_g, m_ev_kn_g,
            m_ev_w_out, m_od_w_in, m_od_lb_logits, m_od_gn_g, m_od_w_out]
    vars_ = [v_norm_ffn1, v_ffn1_wg, v_ffn1_wu, v_ffn1_wd, v_norm_mix, v_norm_ffn2, v_ffn2_wg, v_ffn2_wu,
             v_ffn2_wd, v_ev_w_in, v_ev_conv_w, v_ev_conv_b, v_ev_cn_g, v_ev_cn_b, v_ev_qn_g, v_ev_kn_g,
             v_ev_w_out, v_od_w_in, v_od_lb_logits, v_od_gn_g, v_od_w_out]
    n_w = len(weights)
    deltas, new_m, new_v = [None] * n_w, [None] * n_w, [None] * n_w

    def update(idx):
        for i in idx:
            deltas[i], new_m[i], new_v[i] = _adamw(weights[i], grads[i], moms[i], vars_[i], f"adamw{i}")

    update([i for i in range(n_w) if grads[i] is not None])
    g_conv_w, g_conv_b, g_cn_g, g_cn_b, g_qn_g, g_kn_g = small[0]
    g_lb, g_gn = small[1]
    parts = [jnp.stack(g_norm1), jnp.stack(g_normm), jnp.stack(g_norm2), g_conv_b, g_cn_g, g_cn_b,
             g_qn_g, g_kn_g, g_lb, g_conv_w, g_gn, loss_part[0, :1]]
    red = _unpack_rows(_all_reduce_small(_pack_rows(parts), "reduce_small_grads", [d for d in deltas if d is not None]),
                       [p.shape for p in parts])
    g_norm1, g_normm, g_norm2, g_conv_b, g_cn_g, g_cn_b, g_qn_g, g_kn_g, g_lb, g_conv_w, g_gn, loss = red
    g_conv_w = lax.dynamic_slice(g_conv_w, (0, dev * cs), (cw, cs))
    g_gn = lax.dynamic_slice(g_gn, (dev * gs,), (gs,))
    small_idx = {0: g_norm1, 4: g_normm, 5: g_norm2, 10: g_conv_w[None], 11: g_conv_b[None], 12: g_cn_g[None],
                 13: g_cn_b[None], 14: g_qn_g[None], 15: g_kn_g[None], 18: g_lb, 19: g_gn[None]}
    for i, g in small_idx.items():
        grads[i] = g
    update(small_idx)
    (g_ffn1_0,) = reduce_end(groups["5"], [d for d in deltas if d is not None], "_5")
    grads[1], grads[2], grads[3] = ffn_grads([g_ffn1_0, g_ffn1_1])
    update((1, 2, 3))
    return (loss[0], grad_x, *grads, *deltas, *new_m, *new_v)
```

```python
import functools
import math

import jax
import jax.numpy as jnp
from jax import lax
from jax.experimental import pallas as pl
from jax.experimental.pallas import tpu as pltpu

F32 = jnp.float32
BF16 = jnp.bfloat16
MESH = pl.DeviceIdType.MESH
N_DEV = 8

EPS = 1e-6
HEAD_DIM = 128
CONV_WIDTH = 31
DIL_PATTERNS = ((128, 1), (512, 4), (2048, 16))
Q_BLOCK = 128
ROPE_THETA = 10000.0
HGRN_KDIM = 128
HGRN_CHUNK = 256

ADAM_LR = 0.001
ADAM_B1 = 0.9
ADAM_B2 = 0.999
ADAM_EPS = 1e-08
ADAM_WD = 0.01
ADAM_STEP = 10

VMEM_LIMIT_BYTES = 56 * 1024 * 1024
LANE = 128
SUBLANE_BF16 = 16

ANY = pl.BlockSpec(memory_space=pl.ANY)


def _tile(n, pref, mult):
    t = (min(pref, n) // mult) * mult
    while t > 0:
        if n % t == 0:
            return t
        t -= mult
    return n


def _params(*sem):
    return pltpu.CompilerParams(dimension_semantics=sem, vmem_limit_bytes=VMEM_LIMIT_BYTES)


_DOT_DIMS = {
    "nn": (((1,), (0,)), ((), ())),
    "nt": (((1,), (1,)), ((), ())),
    "tn": (((0,), (0,)), ((), ())),
}


def _dot(a, b, mode):
    return lax.dot_general(a, b, _DOT_DIMS[mode], preferred_element_type=F32)


def _mm(a, b, mode, out_dtype, name, res=None, tm=1024, tn=1024, tk=2048):
    if mode == "nt":
        (M, K), N = a.shape, b.shape[0]
    elif mode == "nn":
        (M, K), N = a.shape, b.shape[1]
    else:
        (K, M), N = a.shape, b.shape[1]
    tm, tn, tk = _tile(M, tm, LANE), _tile(N, tn, LANE), _tile(K, tk, LANE)
    nk = K // tk

    def body(*refs):
        if res is None:
            a_ref, b_ref, o_ref, acc = refs
        else:
            a_ref, b_ref, r_ref, o_ref, acc = refs
        k = pl.program_id(2)

        @pl.when(k == 0)
        def _():
            acc[...] = jnp.zeros_like(acc)

        acc[...] += _dot(a_ref[...].astype(BF16), b_ref[...].astype(BF16), mode)

        @pl.when(k == nk - 1)
        def _():
            r = acc[...]
            if res is not None:
                r = r_ref[...] + r
            o_ref[...] = r.astype(out_dtype)

    a_spec = {"nt": pl.BlockSpec((tm, tk), lambda i, j, k: (i, k)),
              "nn": pl.BlockSpec((tm, tk), lambda i, j, k: (i, k)),
              "tn": pl.BlockSpec((tk, tm), lambda i, j, k: (k, i))}[mode]
    b_spec = {"nt": pl.BlockSpec((tn, tk), lambda i, j, k: (j, k)),
              "nn": pl.BlockSpec((tk, tn), lambda i, j, k: (k, j)),
              "tn": pl.BlockSpec((tk, tn), lambda i, j, k: (k, j))}[mode]
    o_spec = pl.BlockSpec((tm, tn), lambda i, j, k: (i, j))
    in_specs = [a_spec, b_spec] + ([o_spec] if res is not None else [])
    args = (a, b) + ((res,) if res is not None else ())
    return pl.pallas_call(
        body, name=name, grid=(M // tm, N // tn, nk),
        in_specs=in_specs, out_specs=o_spec,
        out_shape=jax.ShapeDtypeStruct((M, N), out_dtype),
        scratch_shapes=[pltpu.VMEM((tm, tn), F32)],
        compiler_params=_params("parallel", "parallel", "arbitrary"),
    )(*args)


def _rms_fwd(x, gain, name):
    S, D = x.shape
    tm = _tile(S, 512, SUBLANE_BF16)

    def body(x_ref, g_ref, o_ref):
        xv = x_ref[...]
        r = lax.rsqrt(jnp.mean(xv * xv, axis=-1, keepdims=True) + EPS)
        o_ref[...] = (xv * r * g_ref[...]).astype(BF16)

    return pl.pallas_call(
        body, name=name, grid=(S // tm,),
        in_specs=[pl.BlockSpec((tm, D), lambda i: (i, 0)), pl.BlockSpec((1, D), lambda i: (0, 0))],
        out_specs=pl.BlockSpec((tm, D), lambda i: (i, 0)),
        out_shape=jax.ShapeDtypeStruct((S, D), BF16),
        compiler_params=_params("parallel"),
    )(x, gain)


def _resid_rms(x, ffn, gain, name):
    S, D = x.shape
    tm = _tile(S, 512, SUBLANE_BF16)

    def body(x_ref, f_ref, g_ref, h_ref, o_ref):
        hv = x_ref[...] + 0.5 * f_ref[...]
        h_ref[...] = hv
        r = lax.rsqrt(jnp.mean(hv * hv, axis=-1, keepdims=True) + EPS)
        o_ref[...] = (hv * r * g_ref[...]).astype(BF16)

    row = pl.BlockSpec((tm, D), lambda i: (i, 0))
    return pl.pallas_call(
        body, name=name, grid=(S // tm,),
        in_specs=[row, row, pl.BlockSpec((1, D), lambda i: (0, 0))], out_specs=[row, row],
        out_shape=[jax.ShapeDtypeStruct((S, D), F32), jax.ShapeDtypeStruct((S, D), BF16)],
        compiler_params=_params("parallel"),
    )(x, ffn, gain)


def _rms_bwd(x, gain, dxn, dy, name):
    S, D = x.shape
    tm = _tile(S, 512, 8)

    def body(x_ref, g_ref, dxn_ref, dy_ref, dx_ref, dg_ref):
        @pl.when(pl.program_id(0) == 0)
        def _():
            dg_ref[...] = jnp.zeros_like(dg_ref)

        xv = x_ref[...]
        r = lax.rsqrt(jnp.mean(xv * xv, axis=-1, keepdims=True) + EPS)
        xh = xv * r
        dxn_v = dxn_ref[...]
        dg_ref[...] += jnp.sum(dxn_v * xh, axis=0, keepdims=True)
        dxh = dxn_v * g_ref[...]
        dx_ref[...] = dy_ref[...] + r * (dxh - xh * jnp.mean(dxh * xh, axis=-1, keepdims=True))

    row = pl.BlockSpec((tm, D), lambda i: (i, 0))
    vec = pl.BlockSpec((1, D), lambda i: (0, 0))
    return pl.pallas_call(
        body, name=name, grid=(S // tm,),
        in_specs=[row, vec, row, row], out_specs=[row, vec],
        out_shape=[jax.ShapeDtypeStruct((S, D), F32), jax.ShapeDtypeStruct((1, D), F32)],
        compiler_params=_params("arbitrary"),
    )(x, gain, dxn, dy)


def _ffn_fwd(xn, w, name):
    S, D = xn.shape
    F = w.shape[1]
    tm, tf = _tile(S, 1024, SUBLANE_BF16), _tile(F, 512, LANE)
    nf = F // tf

    def body(xn_ref, w_ref, o_ref, gu_ref):
        @pl.when(pl.program_id(1) == 0)
        def _():
            o_ref[...] = jnp.zeros_like(o_ref)

        xnv = xn_ref[...]
        g = _dot(xnv, w_ref[0], "nt")
        u = _dot(xnv, w_ref[1], "nt")
        gu_ref[0] = g.astype(BF16)
        gu_ref[1] = u.astype(BF16)
        h = (g * jax.nn.sigmoid(g) * u).astype(BF16)
        o_ref[...] += _dot(h, w_ref[2], "nn")

    row = pl.BlockSpec((tm, D), lambda i, f: (i, 0))
    return pl.pallas_call(
        body, name=name, grid=(S // tm, nf),
        in_specs=[row, pl.BlockSpec((3, tf, D), lambda i, f: (0, f, 0))],
        out_specs=[row, pl.BlockSpec((2, tm, tf), lambda i, f: (0, i, f))],
        out_shape=[jax.ShapeDtypeStruct((S, D), F32), jax.ShapeDtypeStruct((2, S, F), BF16)],
        compiler_params=_params("parallel", "arbitrary"),
    )(xn, w)


def _ffn_bwd_dx(dy, w, gu, name):
    S, D = dy.shape
    F = w.shape[1]
    tm, tf = _tile(S, 512, SUBLANE_BF16), _tile(F, 512, LANE)
    nf = F // tf

    def body(dy_ref, w_ref, gu_ref, dxn_ref, dout_ref, t_ref, acc):
        f = pl.program_id(1)

        @pl.when(f == 0)
        def _():
            acc[...] = jnp.zeros_like(acc)
            dout_ref[...] = (0.5 * dy_ref[...]).astype(BF16)

        dh = _dot(dout_ref[...], w_ref[2], "nt")
        g = gu_ref[0].astype(F32)
        u = gu_ref[1].astype(F32)
        sig = jax.nn.sigmoid(g)
        silu = g * sig
        dg = (dh * u * (sig * (1.0 + g * (1.0 - sig)))).astype(BF16)
        du = (dh * silu).astype(BF16)
        t_ref[0] = dg
        t_ref[1] = du
        t_ref[2] = (silu * u).astype(BF16)
        acc[...] += _dot(dg, w_ref[0], "nn") + _dot(du, w_ref[1], "nn")

        @pl.when(f == nf - 1)
        def _():
            dxn_ref[...] = acc[...]

    row = pl.BlockSpec((tm, D), lambda i, f: (i, 0))
    return pl.pallas_call(
        body, name=name, grid=(S // tm, nf),
        in_specs=[row, pl.BlockSpec((3, tf, D), lambda i, f: (0, f, 0)),
                  pl.BlockSpec((2, tm, tf), lambda i, f: (0, i, f))],
        out_specs=[row, row, pl.BlockSpec((3, tm, tf), lambda i, f: (0, i, f))],
        out_shape=[jax.ShapeDtypeStruct((S, D), F32), jax.ShapeDtypeStruct((S, D), BF16),
                   jax.ShapeDtypeStruct((3, S, F), BF16)],
        scratch_shapes=[pltpu.VMEM((tm, D), F32)],
        compiler_params=_params("parallel", "arbitrary"),
    )(dy, w, gu)


def _ffn_bwd_dw(xn, dout, t, name):
    S, D = xn.shape
    F = t.shape[2]
    ts, tf = _tile(S, 1024, LANE), _tile(F, 512, LANE)
    ns = S // ts

    def body(xn_ref, dout_ref, t_ref, dw_ref, acc):
        s = pl.program_id(1)

        @pl.when(s == 0)
        def _():
            acc[...] = jnp.zeros_like(acc)

        xnv = xn_ref[...]
        acc[0] += _dot(t_ref[0], xnv, "tn")
        acc[1] += _dot(t_ref[1], xnv, "tn")
        acc[2] += _dot(t_ref[2], dout_ref[...], "tn")

        @pl.when(s == ns - 1)
        def _():
            dw_ref[...] = acc[...].astype(BF16)

    row = pl.BlockSpec((ts, D), lambda f, s: (s, 0))
    return pl.pallas_call(
        body, name=name, grid=(F // tf, ns),
        in_specs=[row, row, pl.BlockSpec((3, ts, tf), lambda f, s: (0, s, f))],
        out_specs=pl.BlockSpec((3, tf, D), lambda f, s: (0, f, 0)),
        out_shape=jax.ShapeDtypeStruct((3, F, D), BF16),
        scratch_shapes=[pltpu.VMEM((3, tf, D), F32)],
        compiler_params=_params("parallel", "arbitrary"),
    )(xn, dout, t)


def _loss_grad(x, ffn, target, name):
    S, D = x.shape
    tm = _tile(S, 512, 8)

    def body(x_ref, f_ref, t_ref, dy_ref, l_ref):
        @pl.when(pl.program_id(0) == 0)
        def _():
            l_ref[...] = jnp.zeros_like(l_ref)

        e = (x_ref[...] + 0.5 * f_ref[...]) - t_ref[...]
        dy_ref[...] = e * (1.0 / D)
        l_ref[...] += 0.5 * jnp.sum(jnp.sum(e * e, axis=-1, keepdims=True) * (1.0 / D))

    row = pl.BlockSpec((tm, D), lambda i: (i, 0))
    one = pl.BlockSpec((8, LANE), lambda i: (0, 0))
    return pl.pallas_call(
        body, name=name, grid=(S // tm,),
        in_specs=[row, row, row], out_specs=[row, one],
        out_shape=[jax.ShapeDtypeStruct((S, D), F32), jax.ShapeDtypeStruct((8, LANE), F32)],
        compiler_params=_params("arbitrary"),
    )(x, ffn, target)


def _adamw(w, g, m, v, name):
    shape = w.shape
    C = shape[-1]
    R = math.prod(shape[:-1])
    tr = _tile(R, max(8, (1 << 19) // C // 8 * 8), 8)
    c1 = 1.0 / (1.0 - ADAM_B1 ** ADAM_STEP)
    c2 = 1.0 / (1.0 - ADAM_B2 ** ADAM_STEP)

    def body(w_ref, g_ref, m_ref, v_ref, d_ref, nm_ref, nv_ref):
        gv = g_ref[...]
        nm = ADAM_B1 * m_ref[...] + (1.0 - ADAM_B1) * gv
        nv = ADAM_B2 * v_ref[...] + (1.0 - ADAM_B2) * (gv * gv)
        nm_ref[...] = nm
        nv_ref[...] = nv
        d_ref[...] = -ADAM_LR * ((nm * c1) / (jnp.sqrt(nv * c2) + ADAM_EPS) + ADAM_WD * w_ref[...])

    blk = pl.BlockSpec((tr, C), lambda i: (i, 0))
    sds = jax.ShapeDtypeStruct((R, C), F32)
    outs = pl.pallas_call(
        body, name=name, grid=(R // tr,),
        in_specs=[blk] * 4, out_specs=[blk] * 3, out_shape=[sds] * 3,
        compiler_params=_params("parallel"),
    )(*(a.reshape(R, C) for a in (w, g, m, v)))
    return tuple(o.reshape(shape) for o in outs)


def _me():
    return lax.axis_index("x"), lax.axis_index("y"), lax.axis_index("c")


def _add_core_halves(grad, got, c_idx, name):
    n, nk, _, r, C = grad.shape
    tr = _tile(r, 1024, SUBLANE_BF16)

    def body(c_ref, g_ref, r_ref, o_ref):
        o_ref[...] = (g_ref[...].astype(F32) + r_ref[...].astype(F32)).astype(BF16)

    return pl.pallas_call(
        body, name=name,
        grid_spec=pltpu.PrefetchScalarGridSpec(
            num_scalar_prefetch=1, grid=(n, nk, r // tr),
            in_specs=[pl.BlockSpec((None, None, None, tr, C), lambda i, k, t, c: (i, k, c[0], t, 0)),
                      pl.BlockSpec((None, None, tr, C), lambda i, k, t, c: (i, k, t, 0))],
            out_specs=pl.BlockSpec((None, None, tr, C), lambda i, k, t, c: (i, k, t, 0))),
        out_shape=jax.ShapeDtypeStruct((n, nk, r, C), BF16),
        compiler_params=_params("parallel", "parallel", "parallel"),
    )(c_idx, grad, got)


HBM = pl.BlockSpec(memory_space=pltpu.HBM)
SEM = pl.BlockSpec(memory_space=pltpu.SEMAPHORE)
EFFECT = pltpu.SideEffectType.DATAFLOW_SIDE_EFFECTING


def _push_start(srcs, lands, plan, after, name):
    ns, nl = len(srcs), len(lands)
    ncp = len(plan([None] * ns, [None] * nl, dry=True))
    extra = [] if after is None else [after]

    def body(*refs):
        src_refs, land_refs = refs[:ns], refs[ns:ns + nl]
        send_sems, recv_sems = refs[ns + nl + len(extra)], refs[ns + nl + len(extra) + 1]
        token = refs[-1]
        for i, (s, d, to) in enumerate(plan(src_refs, land_refs)):
            pltpu.make_async_remote_copy(src_ref=s, dst_ref=d, send_sem=send_sems.at[i], recv_sem=recv_sems.at[i],
                                         device_id=to, device_id_type=MESH).start()
        token[...] = jnp.zeros_like(token)

    out = pl.pallas_call(
        body, name=name,
        out_shape=(pltpu.SemaphoreType.DMA((ncp,)), pltpu.SemaphoreType.DMA((ncp,)),
                   *[pltpu.HBM(a.shape, a.dtype) for a in srcs], *[pltpu.HBM(a.shape, a.dtype) for a in lands],
                   jax.ShapeDtypeStruct((8, LANE), F32)),
        in_specs=[HBM] * (ns + nl) + [ANY] * len(extra),
        out_specs=(SEM, SEM, *[HBM] * (ns + nl), pl.BlockSpec(memory_space=pltpu.VMEM)),
        input_output_aliases={i: 2 + i for i in range(ns + nl)},
        compiler_params=pltpu.CompilerParams(has_side_effects=EFFECT),
    )(*[pltpu.with_memory_space_constraint(a, pltpu.HBM) for a in srcs + lands], *extra)
    return out[0], out[1], list(out[2:2 + ns]), list(out[2 + ns:2 + ns + nl]), out[-1]


def _push_wait(send_sems, recv_sems, srcs, lands, plan, after, name):
    ns, nl = len(srcs), len(lands)
    after = list(after) if isinstance(after, (list, tuple)) else [after]

    def body(*refs):
        src_refs, land_refs = refs[:ns], refs[ns:ns + nl]
        send, recv = refs[ns + nl], refs[ns + nl + 1]
        for i, (s, d, to) in enumerate(plan(src_refs, land_refs)):
            cp = pltpu.make_async_remote_copy(src_ref=s, dst_ref=d, send_sem=send.at[i], recv_sem=recv.at[i],
                                              device_id=to, device_id_type=MESH)
            cp.wait_send()
            cp.wait_recv()

    out = pl.pallas_call(
        body, name=name,
        out_shape=tuple(pltpu.HBM(a.shape, a.dtype) for a in srcs + lands),
        in_specs=[HBM] * (ns + nl) + [SEM, SEM] + [ANY] * len(after),
        out_specs=tuple([HBM] * (ns + nl)),
        input_output_aliases={i: i for i in range(ns + nl)},
        compiler_params=pltpu.CompilerParams(has_side_effects=EFFECT),
    )(*srcs, *lands, send_sems, recv_sems, *after)
    return list(out[:ns]), list(out[ns:])


def _gather_plan(src_refs, land_refs, dry=False):
    if dry:
        return [None] * (4 * len(src_refs))
    x, y, c = _me()
    me = 4 * x + 2 * y + c
    targets = [(x, y, 1 - c), (1 - x, y, c), (x, 1 - y, c), (1 - x, 1 - y, c)]
    return [(s, l.at[:, me], to) for s, l in zip(src_refs, land_refs) for to in targets]


def _halves_plan(src_refs, land_refs, dry=False):
    if dry:
        return [None] * len(src_refs)
    x, y, c = _me()
    return [(s.at[:, :, 1 - c], l, (x, y, 1 - c)) for s, l in zip(src_refs, land_refs)]


def _chip_plan(src_refs, land_refs, dry=False):
    if dry:
        return [None] * (3 * len(src_refs))
    x, y, c = _me()
    chips = [(1 - x, y), (x, 1 - y), (1 - x, 1 - y)]
    return [(s.at[:, 2 * chip[0] + chip[1]], l.at[j], (*chip, c))
            for s, l in zip(src_refs, land_refs) for j, chip in enumerate(chips)]


def _forward_plan(src_refs, land_refs, dry=False):
    if dry:
        return [None] * (3 * len(land_refs))
    x, y, c = _me()
    chips = [(1 - x, y), (x, 1 - y), (1 - x, 1 - y)]
    plan = []
    for l in land_refs:
        for chip in chips:
            blk = l.at[:, 4 * chip[0] + 2 * chip[1] + c]
            plan.append((blk, blk, (x, y, 1 - c)))
    return plan


def _sum_chip_blocks(sums, got, k_idx, name):
    n, _, r, C = sums.shape
    tr = _tile(r, 512, SUBLANE_BF16)

    def body(k_ref, s_ref, r_ref, o_ref):
        acc = s_ref[...].astype(F32)
        for j in range(3):
            acc = acc + r_ref[j].astype(F32)
        o_ref[...] = acc

    return pl.pallas_call(
        body, name=name,
        grid_spec=pltpu.PrefetchScalarGridSpec(
            num_scalar_prefetch=1, grid=(n, r // tr),
            in_specs=[pl.BlockSpec((None, None, tr, C), lambda i, t, k: (i, k[0], t, 0)),
                      pl.BlockSpec((3, None, tr, C), lambda i, t, k: (0, i, t, 0))],
            out_specs=pl.BlockSpec((None, tr, C), lambda i, t, k: (i, t, 0))),
        out_shape=jax.ShapeDtypeStruct((n, r, C), F32),
        compiler_params=_params("parallel", "parallel"),
    )(k_idx, sums, got)


def _all_reduce_small(v, name, after=()):
    R = v.shape[0]
    after = list(after)

    def body(*refs):
        v_ref = refs[0]
        o_ref, buf, send_sems, recv_sems = refs[1 + len(after):]
        x, y, c = _me()
        me = 4 * x + 2 * y + c
        buf[me] = v_ref[...]
        copies = []
        for k in range(1, N_DEV):
            peer = (x ^ (k >> 2), y ^ ((k >> 1) & 1), c ^ (k & 1))
            copies.append(pltpu.make_async_remote_copy(
                src_ref=v_ref, dst_ref=buf.at[me],
                send_sem=send_sems.at[k - 1], recv_sem=recv_sems.at[k - 1],
                device_id=peer, device_id_type=MESH))
        for cp in copies:
            cp.start()
        for cp in copies:
            cp.wait()
        acc = buf[0]
        for d in range(1, N_DEV):
            acc = acc + buf[d]
        o_ref[...] = acc

    vm = pl.BlockSpec(memory_space=pltpu.VMEM)
    return pl.pallas_call(
        body, name=name, in_specs=[vm] + [ANY] * len(after), out_specs=vm,
        out_shape=jax.ShapeDtypeStruct((R, LANE), F32),
        scratch_shapes=[pltpu.VMEM((N_DEV, R, LANE), F32),
                        pltpu.SemaphoreType.DMA((N_DEV - 1,)), pltpu.SemaphoreType.DMA((N_DEV - 1,))],
        compiler_params=pltpu.CompilerParams(vmem_limit_bytes=VMEM_LIMIT_BYTES),
    )(v, *after)


def _pack_rows(parts):
    flat = jnp.concatenate([p.reshape(-1).astype(F32) for p in parts])
    n = flat.shape[0]
    rows = -(-n // (8 * LANE)) * 8
    flat = jnp.pad(flat, (0, rows * LANE - n))
    return flat.reshape(rows, LANE)


def _unpack_rows(packed, shapes):
    flat = packed.reshape(-1)
    out, off = [], 0
    for s in shapes:
        n = math.prod(s)
        out.append(flat[off:off + n].reshape(s))
        off += n
    return out


CONV_HALO = 32


def _conv_fwd(u, conv_w, conv_b, cn_g, cn_b, name):
    S = u.shape[0]
    C = conv_w.shape[1]
    T = _tile(S, 256, CONV_HALO)
    hb = T // CONV_HALO

    def body(av_ref, ag_ref, pv_ref, pg_ref, w_ref, b_ref, g_ref, bb_ref, out_ref, y_ref, scr):
        i = pl.program_id(0)
        prev = pv_ref[...] * jax.nn.sigmoid(pg_ref[...])
        scr[0:CONV_HALO, :] = jnp.where(i > 0, prev, 0.0)
        scr[CONV_HALO:CONV_HALO + T, :] = av_ref[...] * jax.nn.sigmoid(ag_ref[...])
        for s in range(C // LANE):
            sl = slice(s * LANE, (s + 1) * LANE)
            acc = jnp.broadcast_to(b_ref[:, sl], (T, LANE))
            for j in range(CONV_WIDTH):
                acc = acc + w_ref[j:j + 1, sl] * scr[pl.ds(CONV_HALO - (CONV_WIDTH - 1) + j, T), sl]
            y_ref[:, sl] = acc
        acc = y_ref[...]
        mu = jnp.mean(acc, axis=-1, keepdims=True)
        xc = acc - mu
        var = jnp.mean(xc * xc, axis=-1, keepdims=True)
        ln = xc * lax.rsqrt(var + EPS) * g_ref[...] + bb_ref[...]
        out_ref[...] = (ln * jax.nn.sigmoid(ln)).astype(BF16)

    def cur(cb):
        return pl.BlockSpec((T, C), lambda i: (i, cb))

    def halo(cb):
        return pl.BlockSpec((CONV_HALO, C), lambda i: (jnp.maximum(i * hb - 1, 0), cb))

    vec = pl.BlockSpec((1, C), lambda i: (0, 0))
    return pl.pallas_call(
        body, name=name, grid=(S // T,),
        in_specs=[cur(0), cur(1), halo(0), halo(1), pl.BlockSpec((CONV_WIDTH, C), lambda i: (0, 0)), vec, vec, vec],
        out_specs=[pl.BlockSpec((T, C), lambda i: (i, 0))] * 2,
        out_shape=[jax.ShapeDtypeStruct((S, C), BF16), jax.ShapeDtypeStruct((S, C), F32)],
        scratch_shapes=[pltpu.VMEM((T + CONV_HALO, C), F32)],
        compiler_params=_params("parallel"),
    )(u, u, u, u, conv_w, conv_b, cn_g, cn_b)


def _conv_bwd_norm(dz, y, cn_g, cn_b, name):
    S, C = y.shape
    T = _tile(S, 256, 8)

    def body(dz_ref, y_ref, g_ref, bb_ref, dy_ref, dg_ref, db_ref):
        @pl.when(pl.program_id(0) == 0)
        def _():
            dg_ref[...] = jnp.zeros_like(dg_ref)
            db_ref[...] = jnp.zeros_like(db_ref)

        yv = y_ref[...]
        mu = jnp.mean(yv, axis=-1, keepdims=True)
        xc = yv - mu
        rstd = lax.rsqrt(jnp.mean(xc * xc, axis=-1, keepdims=True) + EPS)
        xh = xc * rstd
        ln = xh * g_ref[...] + bb_ref[...]
        sg = jax.nn.sigmoid(ln)
        dln = dz_ref[...] * (sg * (1.0 + ln * (1.0 - sg)))
        dg_ref[...] += jnp.sum(dln * xh, axis=0, keepdims=True)
        db_ref[...] += jnp.sum(dln, axis=0, keepdims=True)
        dxh = dln * g_ref[...]
        dy_ref[...] = rstd * (dxh - jnp.mean(dxh, axis=-1, keepdims=True)
                              - xh * jnp.mean(dxh * xh, axis=-1, keepdims=True))

    row = pl.BlockSpec((T, C), lambda i: (i, 0))
    vec = pl.BlockSpec((1, C), lambda i: (0, 0))
    return pl.pallas_call(
        body, name=name, grid=(S // T,),
        in_specs=[row, row, vec, vec], out_specs=[row, vec, vec],
        out_shape=[jax.ShapeDtypeStruct((S, C), F32), jax.ShapeDtypeStruct((1, C), F32),
                   jax.ShapeDtypeStruct((1, C), F32)],
        compiler_params=_params("arbitrary"),
    )(dz, y, cn_g, cn_b)


def _conv_bwd_taps(u, dy, conv_w, name):
    S, C = dy.shape
    T = _tile(S, 256, CONV_HALO)
    hb = T // CONV_HALO
    nt = S // T
    ns = C // LANE
    W1 = CONV_WIDTH - 1

    def body(av_ref, ag_ref, pv_ref, pg_ref, dy_ref, dn_ref, w_ref, dv_ref, dg_ref, dw_ref, db_ref, a_scr, d_scr):
        i = pl.program_id(1)

        @pl.when(i == 0)
        def _():
            dw_ref[...] = jnp.zeros_like(dw_ref)
            db_ref[...] = jnp.zeros_like(db_ref)

        av, sg = av_ref[...], jax.nn.sigmoid(ag_ref[...])
        prev = pv_ref[...] * jax.nn.sigmoid(pg_ref[...])
        a_scr[0:CONV_HALO, :] = jnp.where(i > 0, prev, 0.0)
        a_scr[CONV_HALO:CONV_HALO + T, :] = av * sg
        dyv = dy_ref[...]
        d_scr[0:T, :] = dyv
        d_scr[T:T + CONV_HALO, :] = jnp.where(i < nt - 1, dn_ref[...], 0.0)
        da = jnp.zeros((T, LANE), F32)
        for j in range(CONV_WIDTH):
            da = da + w_ref[j:j + 1, :] * d_scr[pl.ds(W1 - j, T), :]
            dw_ref[j:j + 1, :] += jnp.sum(dyv * a_scr[pl.ds(CONV_HALO - W1 + j, T), :], axis=0, keepdims=True)
        db_ref[...] += jnp.sum(dyv, axis=0, keepdims=True)
        dv_ref[...] = (da * sg).astype(BF16)
        dg_ref[...] = (da * av * sg * (1.0 - sg)).astype(BF16)

    def cur(part):
        return pl.BlockSpec((T, LANE), lambda cb, i: (i, part * ns + cb))

    def halo(part):
        return pl.BlockSpec((CONV_HALO, LANE), lambda cb, i: (jnp.maximum(i * hb - 1, 0), part * ns + cb))

    nxt = pl.BlockSpec((CONV_HALO, LANE), lambda cb, i: (jnp.minimum((i + 1) * hb, S // CONV_HALO - 1), cb))
    row = pl.BlockSpec((T, LANE), lambda cb, i: (i, cb))
    return pl.pallas_call(
        body, name=name, grid=(ns, nt),
        in_specs=[cur(0), cur(1), halo(0), halo(1), row, nxt, pl.BlockSpec((CONV_WIDTH, LANE), lambda cb, i: (0, cb))],
        out_specs=[row, row, pl.BlockSpec((CONV_HALO, LANE), lambda cb, i: (0, cb)),
                   pl.BlockSpec((1, LANE), lambda cb, i: (0, cb))],
        out_shape=[jax.ShapeDtypeStruct((S, C), BF16), jax.ShapeDtypeStruct((S, C), BF16),
                   jax.ShapeDtypeStruct((CONV_HALO, C), F32), jax.ShapeDtypeStruct((1, C), F32)],
        scratch_shapes=[pltpu.VMEM((T + CONV_HALO, LANE), F32), pltpu.VMEM((T + CONV_HALO, LANE), F32)],
        compiler_params=_params("parallel", "arbitrary"),
    )(u, u, u, u, dy, dy, conv_w)


def _rope_tables(S):
    half = HEAD_DIM // 2
    inv = jnp.exp(-math.log(ROPE_THETA) * jnp.arange(half, dtype=F32) / half)
    ang = jnp.arange(S, dtype=jnp.int32).astype(F32)[:, None] * inv[None, :]
    cos, sin = jnp.cos(ang), jnp.sin(ang)
    return jnp.concatenate([cos, cos], axis=1), jnp.concatenate([-sin, sin], axis=1)


def _qkv_prep(u, qn_g, kn_g, cos, sin, cb0, name):
    S = u.shape[0]
    A = (u.shape[1] // (cb0 + 3))
    H = A // HEAD_DIM
    T = _tile(S, 256, SUBLANE_BF16)
    scale = HEAD_DIM ** -0.5

    def body(q_ref, k_ref, v_ref, qg_ref, kg_ref, cos_ref, sin_ref, qo_ref, ko_ref, vo_ref):
        cosv, sinv = cos_ref[...], sin_ref[...]
        for h in range(H):
            sl = slice(h * HEAD_DIM, (h + 1) * HEAD_DIM)
            for x_ref, g_ref, o_ref, sc in ((q_ref, qg_ref, qo_ref, scale), (k_ref, kg_ref, ko_ref, 1.0)):
                xv = x_ref[:, sl]
                xn = xv * lax.rsqrt(jnp.mean(xv * xv, axis=-1, keepdims=True) + EPS) * g_ref[...]
                y = xn * cosv + pltpu.roll(xn, HEAD_DIM // 2, 1) * sinv
                o_ref[:, sl] = (y * sc).astype(BF16)
        vo_ref[...] = v_ref[...].astype(BF16)

    def col(cb):
        return pl.BlockSpec((T, A), lambda i: (i, cb))

    vec = pl.BlockSpec((1, HEAD_DIM), lambda i: (0, 0))
    tab = pl.BlockSpec((T, HEAD_DIM), lambda i: (i, 0))
    out = pl.BlockSpec((T, A), lambda i: (i, 0))
    return pl.pallas_call(
        body, name=name, grid=(S // T,),
        in_specs=[col(cb0), col(cb0 + 1), col(cb0 + 2), vec, vec, tab, tab],
        out_specs=[out] * 3, out_shape=[jax.ShapeDtypeStruct((S, A), BF16)] * 3,
        compiler_params=_params("parallel"),
    )(u, u, u, qn_g, kn_g, cos, sin)


def _qkv_prep_bwd(u, dqs, dks, dvs, qn_g, kn_g, cos, sin, cb0, name):
    S = u.shape[0]
    A = dqs[0].shape[1]
    H = A // HEAD_DIM
    T = _tile(S, 256, SUBLANE_BF16)
    nb = len(dqs)
    scale = HEAD_DIM ** -0.5

    def body(*refs):
        q_ref, k_ref, qg_ref, kg_ref, cos_ref, sin_ref = refs[:6]
        dq_refs, dk_refs, dv_refs = refs[6:6 + nb], refs[6 + nb:6 + 2 * nb], refs[6 + 2 * nb:6 + 3 * nb]
        dqo_ref, dko_ref, dvo_ref, dqg_ref, dkg_ref = refs[6 + 3 * nb:]

        @pl.when(pl.program_id(0) == 0)
        def _():
            dqg_ref[...] = jnp.zeros_like(dqg_ref)
            dkg_ref[...] = jnp.zeros_like(dkg_ref)

        cosv, sinv = cos_ref[...], sin_ref[...]
        for h in range(H):
            sl = slice(h * HEAD_DIM, (h + 1) * HEAD_DIM)
            for x_ref, g_ref, d_refs, o_ref, dg_ref, sc in ((q_ref, qg_ref, dq_refs, dqo_ref, dqg_ref, scale),
                                                          (k_ref, kg_ref, dk_refs, dko_ref, dkg_ref, 1.0)):
                dy = d_refs[0][:, sl]
                for r in d_refs[1:]:
                    dy = dy + r[:, sl]
                dy = dy * sc
                dxn = dy * cosv + pltpu.roll(dy * sinv, HEAD_DIM // 2, 1)
                xv = x_ref[:, sl]
                r = lax.rsqrt(jnp.mean(xv * xv, axis=-1, keepdims=True) + EPS)
                xh = xv * r
                dg_ref[...] += jnp.sum(dxn * xh, axis=0, keepdims=True)
                dxh = dxn * g_ref[...]
                o_ref[:, sl] = (r * (dxh - xh * jnp.mean(dxh * xh, axis=-1, keepdims=True))).astype(BF16)
        dv = dv_refs[0][...]
        for r in dv_refs[1:]:
            dv = dv + r[...]
        dvo_ref[...] = dv.astype(BF16)

    def col(cb):
        return pl.BlockSpec((T, A), lambda i: (i, cb))

    vec = pl.BlockSpec((1, HEAD_DIM), lambda i: (0, 0))
    tab = pl.BlockSpec((T, HEAD_DIM), lambda i: (i, 0))
    row = pl.BlockSpec((T, A), lambda i: (i, 0))
    return pl.pallas_call(
        body, name=name, grid=(S // T,),
        in_specs=[col(cb0), col(cb0 + 1), vec, vec, tab, tab] + [row] * (3 * nb),
        out_specs=[row, row, row, vec, vec],
        out_shape=[jax.ShapeDtypeStruct((S, A), BF16)] * 3 + [jax.ShapeDtypeStruct((1, HEAD_DIM), F32)] * 2,
        compiler_params=_params("arbitrary"),
    )(u, u, qn_g, kn_g, cos, sin, *dqs, *dks, *dvs)


ATT_TILE = 256
NEG = -1e30


def _attn_bias(tile):
    span = max(window for window, _ in DIL_PATTERNS)
    nw = -(-span // tile) + 1
    dist = (jnp.arange(nw)[:, None, None] * tile + jnp.arange(tile)[None, :, None] - jnp.arange(tile)[None, None, :])
    mult = sum(((dist >= 0) & (dist <= window) & (dist % dil == 0)).astype(F32) for window, dil in DIL_PATTERNS)
    return jnp.where(mult > 0, jnp.log(jnp.maximum(mult, 1.0)), NEG)


def _attn_fwd(q, k, v, bias, name):
    S, A = q.shape
    H = A // HEAD_DIM
    nw, T, _ = bias.shape
    nq = S // T

    def body(q_ref, k_ref, v_ref, b_ref, ob_ref, of_ref, l_ref, s_scr):
        i = pl.program_id(1)
        qv = q_ref[...]
        mx = jnp.full((T, 1), NEG, F32)
        for w in range(nw):
            blk = i - w
            start = pl.multiple_of(jnp.maximum(blk, 0) * T, T)
            s = _dot(qv, k_ref[pl.ds(start, T), :], "nt") + b_ref[w] + jnp.where(blk >= 0, 0.0, NEG)
            s_scr[w] = s
            mx = jnp.maximum(mx, jnp.max(s, axis=-1, keepdims=True))
        den = jnp.zeros((T, 1), F32)
        o = jnp.zeros((T, HEAD_DIM), F32)
        for w in range(nw):
            start = pl.multiple_of(jnp.maximum(i - w, 0) * T, T)
            p = jnp.exp(s_scr[w] - mx)
            den = den + jnp.sum(p, axis=-1, keepdims=True)
            o = o + _dot(p.astype(BF16), v_ref[pl.ds(start, T), :], "nn")
        o = o / den
        ob_ref[...] = o.astype(BF16)
        of_ref[...] = o
        l_ref[...] = mx + jnp.log(den)

    blk = pl.BlockSpec((T, HEAD_DIM), lambda h, i: (i, h))
    full = pl.BlockSpec((S, HEAD_DIM), lambda h, i: (0, h))
    return pl.pallas_call(
        body, name=name, grid=(H, nq),
        in_specs=[blk, full, full, pl.BlockSpec((nw, T, T), lambda h, i: (0, 0, 0))],
        out_specs=[blk, blk, pl.BlockSpec((None, T, 1), lambda h, i: (h, i, 0))],
        out_shape=[jax.ShapeDtypeStruct((S, A), BF16), jax.ShapeDtypeStruct((S, A), F32),
                   jax.ShapeDtypeStruct((H, S, 1), F32)],
        scratch_shapes=[pltpu.VMEM((nw, T, T), F32)],
        compiler_params=_params("parallel", "arbitrary"),
    )(q, k, v, bias)


def _attn_dq(q, k, v, dz, cb0, o, lse, bias, name):
    S, A = q.shape
    H = A // HEAD_DIM
    nw, T, _ = bias.shape
    nq = S // T

    def body(q_ref, k_ref, v_ref, do_ref, o_ref, l_ref, b_ref, dq_ref, d_ref):
        i = pl.program_id(1)
        qv, dof = q_ref[...], do_ref[...]
        dov = dof.astype(BF16)
        delta = jnp.sum(dof * o_ref[...], axis=-1, keepdims=True)
        d_ref[...] = delta
        lv = l_ref[...]
        dq = jnp.zeros((T, HEAD_DIM), F32)
        for w in range(nw):
            blk = i - w
            start = pl.multiple_of(jnp.maximum(blk, 0) * T, T)
            kv = k_ref[pl.ds(start, T), :]
            s = _dot(qv, kv, "nt") + b_ref[w] + jnp.where(blk >= 0, 0.0, NEG)
            p = jnp.exp(s - lv)
            ds = (p * (_dot(dov, v_ref[pl.ds(start, T), :], "nt") - delta)).astype(BF16)
            dq = dq + _dot(ds, kv, "nn")
        dq_ref[...] = dq

    blk = pl.BlockSpec((T, HEAD_DIM), lambda h, i: (i, h))
    full = pl.BlockSpec((S, HEAD_DIM), lambda h, i: (0, h))
    col = pl.BlockSpec((None, T, 1), lambda h, i: (h, i, 0))
    return pl.pallas_call(
        body, name=name, grid=(H, nq),
        in_specs=[blk, full, full, pl.BlockSpec((T, HEAD_DIM), lambda h, i: (i, cb0 + h)), blk, col,
                  pl.BlockSpec((nw, T, T), lambda h, i: (0, 0, 0))],
        out_specs=[blk, col],
        out_shape=[jax.ShapeDtypeStruct((S, A), F32), jax.ShapeDtypeStruct((H, S, 1), F32)],
        compiler_params=_params("parallel", "arbitrary"),
    )(q, k, v, dz, o, lse, bias)


def _attn_dkv(q, k, v, dz, cb0, lse, delta, bias, name):
    S, A = q.shape
    H = A // HEAD_DIM
    nw, T, _ = bias.shape
    nq = S // T

    def body(k_ref, v_ref, q_ref, do_ref, l_ref, d_ref, b_ref, dk_ref, dv_ref):
        m = pl.program_id(1)
        kv, vv = k_ref[...], v_ref[...]
        dk = jnp.zeros((T, HEAD_DIM), F32)
        dv = jnp.zeros((T, HEAD_DIM), F32)
        for w in range(nw):
            blk = m + w
            start = pl.multiple_of(jnp.minimum(blk, nq - 1) * T, T)
            qv = q_ref[pl.ds(start, T), :]
            dov = do_ref[pl.ds(start, T), :].astype(BF16)
            s = _dot(qv, kv, "nt") + b_ref[w] + jnp.where(blk < nq, 0.0, NEG)
            p = jnp.exp(s - l_ref[pl.ds(start, T), :])
            dv = dv + _dot(p.astype(BF16), dov, "tn")
            ds = (p * (_dot(dov, vv, "nt") - d_ref[pl.ds(start, T), :])).astype(BF16)
            dk = dk + _dot(ds, qv, "tn")
        dk_ref[...] = dk
        dv_ref[...] = dv

    blk = pl.BlockSpec((T, HEAD_DIM), lambda h, m: (m, h))
    full = pl.BlockSpec((S, HEAD_DIM), lambda h, m: (0, h))
    col = pl.BlockSpec((None, S, 1), lambda h, m: (h, 0, 0))
    sds = jax.ShapeDtypeStruct((S, A), F32)
    return pl.pallas_call(
        body, name=name, grid=(H, nq),
        in_specs=[blk, blk, full, pl.BlockSpec((S, HEAD_DIM), lambda h, m: (0, cb0 + h)), col, col,
                  pl.BlockSpec((nw, T, T), lambda h, m: (0, 0, 0))],
        out_specs=[blk, blk], out_shape=[sds, sds],
        compiler_params=_params("parallel", "arbitrary"),
    )(k, v, q, dz, lse, delta, bias)


def _even_mixer(u, conv_w, conv_b, cn_g, cn_b, qn_g, kn_g, tag):
    S = u.shape[0]
    C = conv_w.shape[1]
    A = (u.shape[1] - 2 * C) // 3
    assert A == C, "column-block addressing of u assumes equal conv and attention widths"
    T = _tile(S, ATT_TILE, LANE)
    cos, sin = _rope_tables(S)
    bias = _attn_bias(T)
    a_out, y = _conv_fwd(u, conv_w, conv_b, cn_g, cn_b, "conv_fwd" + tag)
    q, k, v = _qkv_prep(u, qn_g, kn_g, cos, sin, 2, "qkv_prep" + tag)
    ob, of, lse = _attn_fwd(q, k, v, bias, "attn_fwd" + tag)
    z = jnp.concatenate([a_out, ob], axis=1)

    def backward(dz):
        dy, d_cn_g, d_cn_b = _conv_bwd_norm(dz, y, cn_g, cn_b, "conv_bwd_norm" + tag)
        d_val, d_gate, d_w, d_b = _conv_bwd_taps(u, dy, conv_w, "conv_bwd_taps" + tag)
        dqp, delta = _attn_dq(q, k, v, dz, C // HEAD_DIM, of, lse, bias, "attn_dq" + tag)
        dkp, dvp = _attn_dkv(q, k, v, dz, C // HEAD_DIM, lse, delta, bias, "attn_dkv" + tag)
        dq, dk, dv, d_qn, d_kn = _qkv_prep_bwd(u, [dqp], [dkp], [dvp], qn_g, kn_g, cos, sin, 2, "qkv_prep_bwd" + tag)
        du = jnp.concatenate([d_val, d_gate, dq, dk, dv], axis=1)
        return du, [d_w[:CONV_WIDTH], d_b[0], d_cn_g[0], d_cn_b[0], d_qn[0], d_kn[0]]

    return z, backward


_LEVELS = (128, 64, 32, 16, 8, 4, 2, 1)


def _chunk_cumsum(g, rows, reverse=False):
    C = g.shape[0]
    d = 1
    while d < C:
        if reverse:
            g = g + jnp.where(rows < C - d, pltpu.roll(g, C - d, 0), 0.0)
        else:
            g = g + jnp.where(rows >= d, pltpu.roll(g, d, 0), 0.0)
        d *= 2
    return g


def _level_ref(b, b_scr, rows, m):
    C = b.shape[0]
    if m >= 8:
        pieces = [jnp.broadcast_to(b_scr[2 * m * j + m - 1:2 * m * j + m, :], (2 * m, LANE)) for j in range(C // (2 * m))]
        return pieces[0] if len(pieces) == 1 else jnp.concatenate(pieces, axis=0)
    pos = rows & (2 * m - 1)
    ref = b
    for p in range(2 * m):
        if p != m - 1:
            ref = jnp.where(pos == p, pltpu.roll(b, (p - (m - 1)) % C, 0), ref)
    return ref


def _level_operands(q, k, b, b_scr, rows, m):
    ref = _level_ref(b, b_scr, rows, m)
    qs = (q * jnp.exp(jnp.minimum(b - ref, 0.0))).astype(BF16)
    ks = (k * jnp.exp(jnp.minimum(ref - b, 0.0))).astype(BF16)
    return qs, ks


def _split2(x):
    hi = x.astype(BF16)
    lo = (x - hi.astype(F32)).astype(BF16)
    return jnp.concatenate([hi, lo], axis=1)


def _level_table(n):
    t = jnp.arange(n, dtype=jnp.int32)[:, None]
    s = jnp.arange(n, dtype=jnp.int32)[None, :]
    x = t ^ s
    lvl = sum((x >= (1 << j)).astype(jnp.int32) for j in range(1, n.bit_length()))
    return jnp.where(t > s, lvl, jnp.where(t == s, -1, -2))


def _hgrn_gates(qz, fz, la, lc, oml):
    sq = jax.nn.sigmoid(qz)
    q = qz * sq
    s = jax.nn.sigmoid(fz)
    c = lc + jnp.minimum(fz, 0.0) - jnp.log(1.0 + jnp.exp(-jnp.abs(fz)))
    mx = jnp.maximum(la, c)
    g = mx + jnp.log(1.0 + jnp.exp(-jnp.abs(la - c)))
    k = oml * (1.0 - s)
    return q, sq, k, s, g, c


def _hgrn_fwd(u, la, lc, oml, gn_g, name):
    S = u.shape[0]
    W = u.shape[1] // 4
    H = W // HGRN_KDIM
    C = min(HGRN_CHUNK, S)
    nc = S // C
    levels = [m for m in _LEVELS if m < C]
    HB = C // 2

    def body(qz_ref, fz_ref, iz_ref, gz_ref, la_ref, lc_ref, oml_ref, gn_ref, lvl_ref,
             z_ref, o_ref, a_ref, st_ref, state, b_scr):
        @pl.when(pl.program_id(1) == 0)
        def _():
            state[...] = jnp.zeros_like(state)

        rows = lax.broadcasted_iota(jnp.int32, (C, LANE), 0)
        q, _, k, _, g, _ = _hgrn_gates(qz_ref[...], fz_ref[...], la_ref[...], lc_ref[...], oml_ref[...])
        v = iz_ref[...].astype(BF16)
        b = _chunk_cumsum(g, rows)
        b_scr[...] = b
        lvl = lvl_ref[...]
        qk = jnp.sum(q * k, axis=-1, keepdims=True)
        diag = [jnp.where(lvl == -1, qk[r * HB:(r + 1) * HB], 0.0) for r in range(2)]
        for m in levels[1:]:
            qs, ks = _level_operands(q, k, b, b_scr, rows, m)
            for r in range(2):
                sl = slice(r * HB, (r + 1) * HB)
                diag[r] = jnp.where(lvl == m.bit_length() - 1, _dot(qs[sl], ks[sl], "nt"), diag[r])
        qs, ks = _level_operands(q, k, b, b_scr, rows, HB)
        low = _dot(qs[HB:], ks[:HB], "nt")
        a = jnp.concatenate([jnp.concatenate([diag[0], jnp.zeros((HB, HB), F32)], axis=1),
                             jnp.concatenate([low, diag[1]], axis=1)], axis=0)
        ab = a.astype(BF16)
        a_ref[...] = ab
        st = state[...]
        st_ref[...] = st
        o = _dot(ab, v, "nn") + _dot((q * jnp.exp(b)).astype(BF16), st.astype(BF16), "nt")
        bl = b_scr[C - 1:C, :]
        kh = (k * jnp.exp(bl - b)).astype(BF16)
        state[...] = st * jnp.exp(bl) + _dot(v, kh, "tn")
        o_ref[...] = o
        r = lax.rsqrt(jnp.mean(o * o, axis=-1, keepdims=True) + EPS)
        gz = gz_ref[...]
        z_ref[...] = (o * r * gn_ref[...] * (gz * jax.nn.sigmoid(gz))).astype(BF16)

    def col(off):
        return pl.BlockSpec((C, LANE), lambda h, i: (i, off * H + h))

    vec = pl.BlockSpec((1, LANE), lambda h, i: (0, h))
    tile = pl.BlockSpec((C, LANE), lambda h, i: (i, h))
    return pl.pallas_call(
        body, name=name, grid=(H, nc),
        in_specs=[col(0), col(1), col(2), col(3), vec, vec, vec, vec, pl.BlockSpec((HB, HB), lambda h, i: (0, 0))],
        out_specs=[tile, tile, pl.BlockSpec((None, C, C), lambda h, i: (h, i, 0)),
                   pl.BlockSpec((None, None, LANE, LANE), lambda h, i: (h, i, 0, 0))],
        out_shape=[jax.ShapeDtypeStruct((S, W), BF16), jax.ShapeDtypeStruct((S, W), F32),
                   jax.ShapeDtypeStruct((H, S, C), BF16), jax.ShapeDtypeStruct((H, nc, LANE, LANE), F32)],
        scratch_shapes=[pltpu.VMEM((LANE, LANE), F32), pltpu.VMEM((C, LANE), F32)],
        compiler_params=_params("parallel", "arbitrary"),
    )(u, u, u, u, la, lc, oml, gn_g, _level_table(HB))


def _hgrn_bwd(u, la, lc, oml, gn_g, o, a, st, dz, name):
    S = u.shape[0]
    W = u.shape[1] // 4
    H = W // HGRN_KDIM
    C = min(HGRN_CHUNK, S)
    nc = S // C
    levels = [m for m in _LEVELS if m < C]
    HB = C // 2

    def body(qz_ref, fz_ref, iz_ref, gz_ref, la_ref, lc_ref, oml_ref, gn_ref, o_ref, a_ref, st_ref, dz_ref, lvl_ref,
             dqz_ref, dfz_ref, diz_ref, dgz_ref, dla_ref, dlc_ref, doml_ref, dgn_ref, dstate, b_scr):
        @pl.when(pl.program_id(1) == 0)
        def _():
            dstate[...] = jnp.zeros_like(dstate)
            dla_ref[...] = jnp.zeros_like(dla_ref)
            dlc_ref[...] = jnp.zeros_like(dlc_ref)
            doml_ref[...] = jnp.zeros_like(doml_ref)
            dgn_ref[...] = jnp.zeros_like(dgn_ref)

        rows = lax.broadcasted_iota(jnp.int32, (C, LANE), 0)
        la_v, lc_v, oml_v = la_ref[...], lc_ref[...], oml_ref[...]
        qz, fz = qz_ref[...], fz_ref[...]
        q, sq, k, s, g, c = _hgrn_gates(qz, fz, la_v, lc_v, oml_v)
        vf = iz_ref[...]
        v = vf.astype(BF16)

        ov, gz, dzv, gn = o_ref[...], gz_ref[...], dz_ref[...], gn_ref[...]
        r = lax.rsqrt(jnp.mean(ov * ov, axis=-1, keepdims=True) + EPS)
        on = ov * r
        sg = jax.nn.sigmoid(gz)
        silu_g = gz * sg
        dgn_ref[...] += jnp.sum(dzv * on * silu_g, axis=0, keepdims=True)
        dgz_ref[...] = (dzv * on * gn * (sg * (1.0 + gz * (1.0 - sg)))).astype(BF16)
        don = dzv * gn * silu_g
        do_f = r * (don - on * jnp.mean(don * on, axis=-1, keepdims=True))
        do = do_f.astype(BF16)

        b = _chunk_cumsum(g, rows)
        b_scr[...] = b
        bl = b_scr[C - 1:C, :]
        e = jnp.exp(b)
        ebl = jnp.exp(bl)
        ekl = jnp.exp(bl - b)
        qh = q * e
        kh = k * ekl
        st_v = st_ref[...]
        dst = dstate[...]
        dstb = dst.astype(BF16)

        diz_ref[...] = (_dot(a_ref[...], do, "tn") + _dot(kh.astype(BF16), dstb, "nt")).astype(BF16)
        da = _dot(do, v, "nt")
        dqh = _dot(do, st_v.astype(BF16), "nn")
        dkh = _dot(v, dstb, "nn")
        dstate[...] = dst * ebl + _dot(do, qh.astype(BF16), "tn")
        dbl = jnp.sum(dkh * kh, axis=0, keepdims=True) + jnp.sum(dst * st_v, axis=0, keepdims=True) * ebl

        datt = jnp.sum(do_f * vf, axis=-1, keepdims=True)
        dqa = datt * k
        dka = datt * q
        lvl = lvl_ref[...]
        for m in levels:
            ref = _level_ref(b, b_scr, rows, m)
            eu = jnp.exp(jnp.minimum(b - ref, 0.0))
            el = jnp.exp(jnp.minimum(ref - b, 0.0))
            ks2, qs2 = _split2(k * el), _split2(q * eu)
            if m == HB:
                gm = da[HB:, :HB].astype(BF16)
                pq = jnp.concatenate([jnp.zeros((HB, 2 * LANE), F32), _dot(gm, ks2[:HB], "nn")], axis=0)
                pk = jnp.concatenate([_dot(gm, qs2[HB:], "tn"), jnp.zeros((HB, 2 * LANE), F32)], axis=0)
            else:
                gms = [jnp.where(lvl == m.bit_length() - 1, da[r * HB:(r + 1) * HB, r * HB:(r + 1) * HB], 0.0).astype(BF16)
                       for r in range(2)]
                pq = jnp.concatenate([_dot(gms[r], ks2[r * HB:(r + 1) * HB], "nn") for r in range(2)], axis=0)
                pk = jnp.concatenate([_dot(gms[r], qs2[r * HB:(r + 1) * HB], "tn") for r in range(2)], axis=0)
            dqa += (pq[:, :LANE] + pq[:, LANE:]) * eu
            dka += (pk[:, :LANE] + pk[:, LANE:]) * el
        db = q * dqa - k * dka + dqh * qh - dkh * kh
        db = db + jnp.where(rows == C - 1, dbl, 0.0)
        dq = dqa + dqh * e
        dk = dka + dkh * ekl
        dg = _chunk_cumsum(db, rows, reverse=True)

        wa = jnp.exp(la_v - g)
        wc = jnp.exp(c - g)
        dqz_ref[...] = (dq * (sq * (1.0 + qz * (1.0 - sq)))).astype(BF16)
        dfz_ref[...] = (dg * wc * (1.0 - s) - dk * oml_v * s * (1.0 - s)).astype(BF16)
        dla_ref[...] += jnp.sum(dg * wa, axis=0, keepdims=True)
        dlc_ref[...] += jnp.sum(dg * wc, axis=0, keepdims=True)
        doml_ref[...] += jnp.sum(dk * (1.0 - s), axis=0, keepdims=True)

    def col(off):
        return pl.BlockSpec((C, LANE), lambda h, i: (nc - 1 - i, off * H + h))

    vec = pl.BlockSpec((1, LANE), lambda h, i: (0, h))
    tile = pl.BlockSpec((C, LANE), lambda h, i: (nc - 1 - i, h))
    a_spec = pl.BlockSpec((None, C, C), lambda h, i: (h, nc - 1 - i, 0))
    st_spec = pl.BlockSpec((None, None, LANE, LANE), lambda h, i: (h, nc - 1 - i, 0, 0))
    sw = jax.ShapeDtypeStruct((S, W), BF16)
    vw = jax.ShapeDtypeStruct((1, W), F32)
    return pl.pallas_call(
        body, name=name, grid=(H, nc),
        in_specs=[col(0), col(1), col(2), col(3), vec, vec, vec, vec, tile, a_spec, st_spec, tile,
                  pl.BlockSpec((HB, HB), lambda h, i: (0, 0))],
        out_specs=[tile, tile, tile, tile, vec, vec, vec, vec],
        out_shape=[sw, sw, sw, sw, vw, vw, vw, vw],
        scratch_shapes=[pltpu.VMEM((LANE, LANE), F32), pltpu.VMEM((C, LANE), F32)],
        compiler_params=_params("parallel", "arbitrary"),
    )(u, u, u, u, la, lc, oml, gn_g, o, a, st, dz, _level_table(HB))


def _lb_terms(lb_logits, layer):
    p = jax.nn.softmax(lb_logits, axis=0)
    lb = (jnp.cumsum(p, axis=0) - p[0:1])[layer]
    return jnp.log(lb)[None], jnp.log1p(-lb)[None], (1.0 - lb)[None]


def kernel(x, norm_ffn1, ffn1_wg, ffn1_wu, ffn1_wd, norm_mix, norm_ffn2, ffn2_wg, ffn2_wu, ffn2_wd, ev_w_in, ev_conv_w, ev_conv_b, ev_cn_g, ev_cn_b, ev_qn_g, ev_kn_g, ev_w_out, od_w_in, od_lb_logits, od_gn_g, od_w_out, loss_target, m_norm_ffn1, m_ffn1_wg, m_ffn1_wu, m_ffn1_wd, m_norm_mix, m_norm_ffn2, m_ffn2_wg, m_ffn2_wu, m_ffn2_wd, m_ev_w_in, m_ev_conv_w, m_ev_conv_b, m_ev_cn_g, m_ev_cn_b, m_ev_qn_g, m_ev_kn_g, m_ev_w_out, m_od_w_in, m_od_lb_logits, m_od_gn_g, m_od_w_out, v_norm_ffn1, v_ffn1_wg, v_ffn1_wu, v_ffn1_wd, v_norm_mix, v_norm_ffn2, v_ffn2_wg, v_ffn2_wu, v_ffn2_wd, v_ev_w_in, v_ev_conv_w, v_ev_conv_b, v_ev_cn_g, v_ev_cn_b, v_ev_qn_g, v_ev_kn_g, v_ev_w_out, v_od_w_in, v_od_lb_logits, v_od_gn_g, v_od_w_out):
    depth = norm_ffn1.shape[0]
    S, D = x.shape[1], x.shape[2]
    xi, yi, ci = _me()
    dev = 4 * xi + 2 * yi + ci
    c_idx = jnp.reshape(ci, (1,)).astype(jnp.int32)
    k_idx = jnp.reshape(2 * xi + yi, (1,)).astype(jnp.int32)

    def ffn_shard(wg, wu, wd, l):
        return jnp.stack([wg[l].T, wu[l].T, wd[l]]).astype(BF16)

    assert depth == 2, "the exchange schedule below is written for one even and one odd layer"
    sh_ffn1 = [ffn_shard(ffn1_wg, ffn1_wu, ffn1_wd, l) for l in range(depth)]
    sh_ffn2 = [ffn_shard(ffn2_wg, ffn2_wu, ffn2_wd, l) for l in range(depth)]
    sh_ev = [ev_w_in[0].T.astype(BF16)[None], ev_w_out[0].astype(BF16)[None]]
    sh_od = [od_w_in[0].T.astype(BF16)[None], od_w_out[0].astype(BF16)[None]]

    def full(g):
        return g.reshape(g.shape[0], N_DEV * g.shape[2], g.shape[3])

    def gather_begin(shards, after, tag):
        lands = [lax.dynamic_update_slice(lax.empty((s.shape[0], N_DEV) + s.shape[1:], s.dtype), s[:, None],
                                          (0, dev, 0, 0)) for s in shards]
        state = _push_start(shards, lands, _gather_plan, after, "gather_start" + tag)
        return state, state[4][0, 0]

    def gather_arrived(state, after, tag):
        send, recv, srcs, lands, _ = state
        _, lands = _push_wait(send, recv, srcs, lands, _gather_plan, after, "gather_wait" + tag)
        state = _push_start([], lands, _forward_plan, None, "forward_start" + tag)
        return state, state[4][0, 0]

    def gather_done(state, after, tag):
        send, recv, _, lands, _ = state
        _, lands = _push_wait(send, recv, [], lands, _forward_plan, after, "forward_wait" + tag)
        return [full(g) for g in lands]

    def gather_end(state, after, tag):
        state, _ = gather_arrived(state, after, tag)
        return gather_done(state, state[4], tag)

    conv_w_sh, gn_g_sh = ev_conv_w[0], od_gn_g[0]
    cw, cs = conv_w_sh.shape[0], conv_w_sh.shape[1]
    gs = gn_g_sh.shape[0]
    conv_w_z = lax.dynamic_update_slice(jnp.zeros((cw, N_DEV * cs), F32), conv_w_sh, (0, dev * cs))
    gn_g_z = lax.dynamic_update_slice(jnp.zeros((N_DEV * gs,), F32), gn_g_sh, (dev * gs,))
    conv_w_full, gn_g_full = _unpack_rows(
        _all_reduce_small(_pack_rows([conv_w_z, gn_g_z]), "gather_small_params"),
        [conv_w_z.shape, gn_g_z.shape])

    w_ffn1, w_ffn2 = [None] * depth, [None] * depth
    pending, after = {}, conv_w_full
    for key, shards in (("0", [sh_ffn1[0]]), ("1", sh_ev), ("2", [sh_ffn2[0]]), ("3", [sh_ffn1[1]]), ("4", sh_od),
                        ("5", [sh_ffn2[1]])):
        pending[key], _ = gather_begin(shards, after, "_" + key)
        after = pending[key][4]
    start_tok = after[0, 0]
    (w_ffn1[0],) = gather_end(pending.pop("0"), after, "_0")

    def odd_mixer(u, l, tok):
        (la, lc, oml), lb_vjp = jax.vjp(functools.partial(_lb_terms, layer=l), od_lb_logits)
        gn = (gn_g_full + tok)[None]
        zb, o_raw, scores, states = _hgrn_fwd(u, la, lc, oml, gn, f"hgrn_fwd{l}")

        def backward(dz):
            dqz, dfz, diz, dgz, dla, dlc, doml, dgn = _hgrn_bwd(
                u, la, lc, oml, gn, o_raw, scores, states, dz, f"hgrn_bwd{l}")
            (g_lb,) = lb_vjp((dla, dlc, doml))
            return jnp.concatenate([dqz, dfz, diz, dgz], axis=1), [g_lb, dgn[0]]

        return zb, backward

    saved = []
    h = x[0]
    hn = _rms_fwd(h, (norm_ffn1[0] + start_tok)[None], "rms_a0")
    for l in range(depth):
        ffn, gu = _ffn_fwd(hn, w_ffn1[l], f"ffn_fwd_a{l}")
        s1 = (h, hn, gu)
        if l == 0:
            w_in, w_out = gather_end(pending.pop("1"), ffn, "_1")
        else:
            w_in, w_out = gather_done(pending.pop("4"), ffn, "_4")
        w_in_t, w_out = w_in[0], w_out[0]
        h, hn = _resid_rms(h, ffn, norm_mix[l][None], f"rms_mix{l}")
        u = _mm(hn, w_in_t, "nt", F32, f"mix_in{l}")
        key = "2" if l == 0 else "5"
        passing, tok = gather_arrived(pending.pop(key), u, "_" + key)
        if l % 2 == 0:
            zb, core_vjp = _even_mixer(u, conv_w_full, ev_conv_b + tok, ev_cn_g, ev_cn_b, ev_qn_g, ev_kn_g, str(l))
        else:
            zb, core_vjp = odd_mixer(u, l, tok)
        h_mix = h
        h = _mm(zb, w_out, "nn", F32, f"mix_out{l}", res=h)
        sm = (h_mix, hn, zb, core_vjp, w_in_t, w_out)
        (w_ffn2[l],) = gather_done(passing, h, "_" + key)
        tok = 0.0
        if l + 1 < depth:
            passing, tok = gather_arrived(pending.pop("3"), w_ffn2[l], "_3")
        hn = _rms_fwd(h, (norm_ffn2[l] + tok)[None], f"rms_b{l}")
        ffn, gu = _ffn_fwd(hn, w_ffn2[l], f"ffn_fwd_b{l}")
        saved.append((s1, sm, (h, hn, gu)))
        if l + 1 < depth:
            (w_ffn1[l + 1],) = gather_done(passing, ffn, "_3")
            pending["4"], tok = gather_arrived(pending.pop("4"), w_ffn1[l + 1], "_4")
            h, hn = _resid_rms(h, ffn, (norm_ffn1[l + 1] + tok)[None], f"rms_a{l + 1}")

    dy, loss_part = _loss_grad(h, ffn, loss_target[0], "loss_grad")

    def halves_begin(parts, tag):
        parts = [g.reshape(g.shape[0], 4, 2, g.shape[1] // N_DEV, g.shape[2]) for g in parts]
        lands = [lax.empty(g.shape[:2] + g.shape[3:], BF16) for g in parts]
        state = _push_start(parts, lands, _halves_plan, None, "halves_start" + tag)
        return state, state[4][0, 0]

    def chips_begin(state, after, tag):
        send, recv, srcs, lands, _ = state
        parts, got = _push_wait(send, recv, srcs, lands, _halves_plan, after, "halves_wait" + tag)
        sums = [_add_core_halves(g, r, c_idx, f"add_core_halves{tag}_{a}") for a, (g, r) in enumerate(zip(parts, got))]
        lands = [lax.empty((3, s.shape[0]) + s.shape[2:], BF16) for s in sums]
        state = _push_start(sums, lands, _chip_plan, None, "reduce_start" + tag)
        return state, state[4][0, 0]

    def reduce_end(state, after, tag):
        send, recv, srcs, lands, _ = state
        sums, got = _push_wait(send, recv, srcs, lands, _chip_plan, after, "reduce_wait" + tag)
        return [_sum_chip_blocks(s, r, k_idx, f"sum_chip_blocks{tag}_{a}") for a, (s, r) in enumerate(zip(sums, got))]

    def ffn_backward(dy, gain, w, sv, tag, on_dw, on_dx=None):
        h_in, hn, gu = sv
        dxn, dout, t = _ffn_bwd_dx(dy, w, gu, "ffn_bwd_dx_" + tag)
        tok = 0.0 if on_dx is None else on_dx(dxn)
        tok = tok + on_dw(_ffn_bwd_dw(hn, dout, t, "ffn_bwd_dw_" + tag))
        dx, dgain = _rms_bwd(h_in, (gain + tok)[None], dxn, dy, "rms_bwd_" + tag)
        return dx, dgain[0]

    g_norm1, g_norm2, g_normm = [None] * depth, [None] * depth, [None] * depth
    small, halves, groups = [None, None], {}, {}

    def start_halves(key, make_parts):
        def hook(dw):
            halves[key], tok = halves_begin(make_parts(dw), "_" + key)
            return tok
        return hook

    def start_chips(key):
        def hook(after):
            groups[key], tok = chips_begin(halves.pop(key), after, "_" + key)
            return tok
        return hook

    for l in reversed(range(depth)):
        s1, (h_mix, hn, zb, core_vjp, w_in_t, w_out), s2 = saved[l]
        if l == 1:
            dy, g_norm2[l] = ffn_backward(dy, norm_ffn2[l], w_ffn2[l], s2, f"b{l}", start_halves("1", lambda dw: [dw]))
        else:
            dy, g_norm2[l] = ffn_backward(dy, norm_ffn2[l], w_ffn2[l], s2, f"b{l}", start_halves("3", lambda dw: [dw]),
                                          start_chips("2"))
        dyb = dy.astype(BF16)
        dz = _mm(dyb, w_out, "nt", F32, f"mix_out_dz{l}")
        dw_out = _mm(zb, dyb, "tn", BF16, f"mix_out_dw{l}")
        dub, small[l % 2] = core_vjp(dz)
        dw_in_t = _mm(dub, hn, "tn", BF16, f"mix_in_dw{l}")
        mix_parts = [dw_in_t[None], dw_out[None]]
        if l == 1:
            tok = start_chips("1")(dw_in_t)
        else:
            tok = start_chips("3")(dw_in_t) + start_halves("4", lambda _: mix_parts)(None)
        dhn = _mm(dub, w_in_t, "nn", F32, f"mix_in_dx{l}")
        dy, gm = _rms_bwd(h_mix, (norm_mix[l] + tok)[None], dhn, dy, f"rms_bwd_mix{l}")
        g_normm[l] = gm[0]
        if l == 1:
            dy, g_norm1[l] = ffn_backward(dy, norm_ffn1[l], w_ffn1[l], s1, f"a{l}",
                                          start_halves("2", lambda dw, od=mix_parts: od + [dw]))
        else:
            dy, g_norm1[l] = ffn_backward(dy, norm_ffn1[l], w_ffn1[l], s1, f"a{l}", start_halves("5", lambda dw: [dw]),
                                          start_chips("4"))
    grad_x = dy[None]
    start_chips("5")(dy)

    done = [dy, groups["5"][4]]
    (g_ffn2_1,) = reduce_end(groups["1"], done, "_1")
    g_od_in_t, g_od_out, g_ffn1_1 = reduce_end(groups["2"], done, "_2")
    (g_ffn2_0,) = reduce_end(groups["3"], done, "_3")
    g_ev_in_t, g_ev_out = reduce_end(groups["4"], done, "_4")
    g_ffn2 = [g_ffn2_0, g_ffn2_1]

    def ffn_grads(gl):
        return (jnp.stack([g[0].T for g in gl]), jnp.stack([g[1].T for g in gl]), jnp.stack([g[2] for g in gl]))

    g_ffn2_wg, g_ffn2_wu, g_ffn2_wd = ffn_grads(g_ffn2)
    grads = [None, None, None, None, None, None, g_ffn2_wg, g_ffn2_wu, g_ffn2_wd,
             g_ev_in_t[0].T[None], None, None, None, None, None,
             None, g_ev_out, g_od_in_t[0].T[None], None, None, g_od_out]
    weights = [norm_ffn1, ffn1_wg, ffn1_wu, ffn1_wd, norm_mix, norm_ffn2, ffn2_wg, ffn2_wu, ffn2_wd, ev_w_in,
               ev_conv_w, ev_conv_b, ev_cn_g, ev_cn_b, ev_qn_g, ev_kn_g, ev_w_out, od_w_in, od_lb_logits,
               od_gn_g, od_w_out]
    moms = [m_norm_ffn1, m_ffn1_wg, m_ffn1_wu, m_ffn1_wd, m_norm_mix, m_norm_ffn2, m_ffn2_wg, m_ffn2_wu,
            m_ffn2_wd, m_ev_w_in, m_ev_conv_w, m_ev_conv_b, m_ev_cn_g, m_ev_cn_b, m_ev_qn_g, m_ev_kn_g,
            m_ev_w_out, m_od_w_in, m_od_lb_logits, m_od_gn_g, m_od_w_out]
    vars_ = [v_norm_ffn1, v_ffn1_wg, v_ffn1_wu, v_ffn1_wd, v_norm_mix, v_norm_ffn2, v_ffn2_wg, v_ffn2_wu,
             v_ffn2_wd, v_ev_w_in, v_ev_conv_w, v_ev_conv_b, v_ev_cn_g, v_ev_cn_b, v_ev_qn_g, v_ev_kn_g,
             v_ev_w_out, v_od_w_in, v_od_lb_logits, v_od_gn_g, v_od_w_out]
    n_w = len(weights)
    deltas, new_m, new_v = [None] * n_w, [None] * n_w, [None] * n_w

    def update(idx):
        for i in idx:
            deltas[i], new_m[i], new_v[i] = _adamw(weights[i], grads[i], moms[i], vars_[i], f"adamw{i}")

    update([i for i in range(n_w) if grads[i] is not None])
    g_conv_w, g_conv_b, g_cn_g, g_cn_b, g_qn_g, g_kn_g = small[0]
    g_lb, g_gn = small[1]
    parts = [jnp.stack(g_norm1), jnp.stack(g_normm), jnp.stack(g_norm2), g_conv_b, g_cn_g, g_cn_b,
             g_qn_g, g_kn_g, g_lb, g_conv_w, g_gn, loss_part[0, :1]]
    red = _unpack_rows(_all_reduce_small(_pack_rows(parts), "reduce_small_grads", [d for d in deltas if d is not None]),
                       [p.shape for p in parts])
    g_norm1, g_normm, g_norm2, g_conv_b, g_cn_g, g_cn_b, g_qn_g, g_kn_g, g_lb, g_conv_w, g_gn, loss = red
    g_conv_w = lax.dynamic_slice(g_conv_w, (0, dev * cs), (cw, cs))
    g_gn = lax.dynamic_slice(g_gn, (dev * gs,), (gs,))
    small_idx = {0: g_norm1, 4: g_normm, 5: g_norm2, 10: g_conv_w[None], 11: g_conv_b[None], 12: g_cn_g[None],
                 13: g_cn_b[None], 14: g_qn_g[None], 15: g_kn_g[None], 18: g_lb, 19: g_gn[None]}
    for i, g in small_idx.items():
        grads[i] = g
    update(small_idx)
    (g_ffn1_0,) = reduce_end(groups["5"], [d for d in deltas if d is not None], "_5")
    grads[1], grads[2], grads[3] = ffn_grads([g_ffn1_0, g_ffn1_1])
    update((1, 2, 3))
    return (loss[0], grad_x, *grads, *deltas, *new_m, *new_v)
```

```python
import functools
import math

import jax
import jax.numpy as jnp
from jax import lax
from jax.experimental import pallas as pl
from jax.experimental.pallas import tpu as pltpu

F32 = jnp.float32
BF16 = jnp.bfloat16
MESH = pl.DeviceIdType.MESH
N_DEV = 8

EPS = 1e-6
HEAD_DIM = 128
CONV_WIDTH = 31
DIL_PATTERNS = ((128, 1), (512, 4), (2048, 16))
Q_BLOCK = 128
ROPE_THETA = 10000.0
HGRN_KDIM = 128
HGRN_CHUNK = 256

ADAM_LR = 0.001
ADAM_B1 = 0.9
ADAM_B2 = 0.999
ADAM_EPS = 1e-08
ADAM_WD = 0.01
ADAM_STEP = 10

VMEM_LIMIT_BYTES = 56 * 1024 * 1024
LANE = 128
SUBLANE_BF16 = 16

ANY = pl.BlockSpec(memory_space=pl.ANY)


def _tile(n, pref, mult):
    t = (min(pref, n) // mult) * mult
    while t > 0:
        if n % t == 0:
            return t
        t -= mult
    return n


def _params(*sem):
    return pltpu.CompilerParams(dimension_semantics=sem, vmem_limit_bytes=VMEM_LIMIT_BYTES)


_DOT_DIMS = {
    "nn": (((1,), (0,)), ((), ())),
    "nt": (((1,), (1,)), ((), ())),
    "tn": (((0,), (0,)), ((), ())),
}


def _dot(a, b, mode):
    return lax.dot_general(a, b, _DOT_DIMS[mode], preferred_element_type=F32)


def _mm(a, b, mode, out_dtype, name, res=None, tm=1024, tn=1024, tk=2048):
    if mode == "nt":
        (M, K), N = a.shape, b.shape[0]
    elif mode == "nn":
        (M, K), N = a.shape, b.shape[1]
    else:
        (K, M), N = a.shape, b.shape[1]
    tm, tn, tk = _tile(M, tm, LANE), _tile(N, tn, LANE), _tile(K, tk, LANE)
    nk = K // tk

    def body(*refs):
        if res is None:
            a_ref, b_ref, o_ref, acc = refs
        else:
            a_ref, b_ref, r_ref, o_ref, acc = refs
        k = pl.program_id(2)

        @pl.when(k == 0)
        def _():
            acc[...] = jnp.zeros_like(acc)

        acc[...] += _dot(a_ref[...].astype(BF16), b_ref[...].astype(BF16), mode)

        @pl.when(k == nk - 1)
        def _():
            r = acc[...]
            if res is not None:
                r = r_ref[...] + r
            o_ref[...] = r.astype(out_dtype)

    a_spec = {"nt": pl.BlockSpec((tm, tk), lambda i, j, k: (i, k)),
              "nn": pl.BlockSpec((tm, tk), lambda i, j, k: (i, k)),
              "tn": pl.BlockSpec((tk, tm), lambda i, j, k: (k, i))}[mode]
    b_spec = {"nt": pl.BlockSpec((tn, tk), lambda i, j, k: (j, k)),
              "nn": pl.BlockSpec((tk, tn), lambda i, j, k: (k, j)),
              "tn": pl.BlockSpec((tk, tn), lambda i, j, k: (k, j))}[mode]
    o_spec = pl.BlockSpec((tm, tn), lambda i, j, k: (i, j))
    in_specs = [a_spec, b_spec] + ([o_spec] if res is not None else [])
    args = (a, b) + ((res,) if res is not None else ())
    return pl.pallas_call(
        body, name=name, grid=(M // tm, N // tn, nk),
        in_specs=in_specs, out_specs=o_spec,
        out_shape=jax.ShapeDtypeStruct((M, N), out_dtype),
        scratch_shapes=[pltpu.VMEM((tm, tn), F32)],
        compiler_params=_params("parallel", "parallel", "arbitrary"),
    )(*args)


def _rms_fwd(x, gain, name):
    S, D = x.shape
    tm = _tile(S, 512, SUBLANE_BF16)

    def body(x_ref, g_ref, o_ref):
        xv = x_ref[...]
        r = lax.rsqrt(jnp.mean(xv * xv, axis=-1, keepdims=True) + EPS)
        o_ref[...] = (xv * r * g_ref[...]).astype(BF16)

    return pl.pallas_call(
        body, name=name, grid=(S // tm,),
        in_specs=[pl.BlockSpec((tm, D), lambda i: (i, 0)), pl.BlockSpec((1, D), lambda i: (0, 0))],
        out_specs=pl.BlockSpec((tm, D), lambda i: (i, 0)),
        out_shape=jax.ShapeDtypeStruct((S, D), BF16),
        compiler_params=_params("parallel"),
    )(x, gain)


def _resid_rms(x, ffn, gain, name):
    S, D = x.shape
    tm = _tile(S, 512, SUBLANE_BF16)

    def body(x_ref, f_ref, g_ref, h_ref, o_ref):
        hv = x_ref[...] + 0.5 * f_ref[...]
        h_ref[...] = hv
        r = lax.rsqrt(jnp.mean(hv * hv, axis=-1, keepdims=True) + EPS)
        o_ref[...] = (hv * r * g_ref[...]).astype(BF16)

    row = pl.BlockSpec((tm, D), lambda i: (i, 0))
    return pl.pallas_call(
        body, name=name, grid=(S // tm,),
        in_specs=[row, row, pl.BlockSpec((1, D), lambda i: (0, 0))], out_specs=[row, row],
        out_shape=[jax.ShapeDtypeStruct((S, D), F32), jax.ShapeDtypeStruct((S, D), BF16)],
        compiler_params=_params("parallel"),
    )(x, ffn, gain)


def _rms_bwd(x, gain, dxn, dy, name):
    S, D = x.shape
    tm = _tile(S, 512, 8)

    def body(x_ref, g_ref, dxn_ref, dy_ref, dx_ref, dg_ref):
        @pl.when(pl.program_id(0) == 0)
        def _():
            dg_ref[...] = jnp.zeros_like(dg_ref)

        xv = x_ref[...]
        r = lax.rsqrt(jnp.mean(xv * xv, axis=-1, keepdims=True) + EPS)
        xh = xv * r
        dxn_v = dxn_ref[...]
        dg_ref[...] += jnp.sum(dxn_v * xh, axis=0, keepdims=True)
        dxh = dxn_v * g_ref[...]
        dx_ref[...] = dy_ref[...] + r * (dxh - xh * jnp.mean(dxh * xh, axis=-1, keepdims=True))

    row = pl.BlockSpec((tm, D), lambda i: (i, 0))
    vec = pl.BlockSpec((1, D), lambda i: (0, 0))
    return pl.pallas_call(
        body, name=name, grid=(S // tm,),
        in_specs=[row, vec, row, row], out_specs=[row, vec],
        out_shape=[jax.ShapeDtypeStruct((S, D), F32), jax.ShapeDtypeStruct((1, D), F32)],
        compiler_params=_params("arbitrary"),
    )(x, gain, dxn, dy)


def _ffn_fwd(xn, w, name):
    S, D = xn.shape
    F = w.shape[1]
    tm, tf = _tile(S, 1024, SUBLANE_BF16), _tile(F, 512, LANE)
    nf = F // tf

    def body(xn_ref, w_ref, o_ref, gu_ref):
        @pl.when(pl.program_id(1) == 0)
        def _():
            o_ref[...] = jnp.zeros_like(o_ref)

        xnv = xn_ref[...]
        g = _dot(xnv, w_ref[0], "nt")
        u = _dot(xnv, w_ref[1], "nt")
        gu_ref[0] = g.astype(BF16)
        gu_ref[1] = u.astype(BF16)
        h = (g * jax.nn.sigmoid(g) * u).astype(BF16)
        o_ref[...] += _dot(h, w_ref[2], "nn")

    row = pl.BlockSpec((tm, D), lambda i, f: (i, 0))
    return pl.pallas_call(
        body, name=name, grid=(S // tm, nf),
        in_specs=[row, pl.BlockSpec((3, tf, D), lambda i, f: (0, f, 0))],
        out_specs=[row, pl.BlockSpec((2, tm, tf), lambda i, f: (0, i, f))],
        out_shape=[jax.ShapeDtypeStruct((S, D), F32), jax.ShapeDtypeStruct((2, S, F), BF16)],
        compiler_params=_params("parallel", "arbitrary"),
    )(xn, w)


def _ffn_bwd_dx(dy, w, gu, name):
    S, D = dy.shape
    F = w.shape[1]
    tm, tf = _tile(S, 1024, SUBLANE_BF16), _tile(F, 512, LANE)
    nf = F // tf

    def body(dy_ref, w_ref, gu_ref, dxn_ref, dout_ref, t_ref):
        @pl.when(pl.program_id(1) == 0)
        def _():
            dxn_ref[...] = jnp.zeros_like(dxn_ref)
            dout_ref[...] = (0.5 * dy_ref[...]).astype(BF16)

        g = gu_ref[0].astype(F32)
        u = gu_ref[1].astype(F32)
        sig = jax.nn.sigmoid(g)
        silu = g * sig
        t_ref[2] = (silu * u).astype(BF16)
        dh = _dot(dout_ref[...], w_ref[2], "nt")
        dg = (dh * (u * (sig * (1.0 + g * (1.0 - sig))))).astype(BF16)
        du = (dh * silu).astype(BF16)
        t_ref[0] = dg
        t_ref[1] = du
        dxn_ref[...] += _dot(dg, w_ref[0], "nn") + _dot(du, w_ref[1], "nn")

    row = pl.BlockSpec((tm, D), lambda i, f: (i, 0), pipeline_mode=pl.Buffered(1))
    return pl.pallas_call(
        body, name=name, grid=(S // tm, nf),
        in_specs=[row, pl.BlockSpec((3, tf, D), lambda i, f: (0, f, 0)),
                  pl.BlockSpec((2, tm, tf), lambda i, f: (0, i, f))],
        out_specs=[row, row, pl.BlockSpec((3, tm, tf), lambda i, f: (0, i, f))],
        out_shape=[jax.ShapeDtypeStruct((S, D), F32), jax.ShapeDtypeStruct((S, D), BF16),
                   jax.ShapeDtypeStruct((3, S, F), BF16)],
        compiler_params=_params("parallel", "arbitrary"),
    )(dy, w, gu)


def _ffn_bwd_dw(xn, dout, t, name):
    S, D = xn.shape
    F = t.shape[2]
    ts, tf = _tile(S, 1024, LANE), _tile(F, 512, LANE)
    ns = S // ts

    def body(xn_ref, dout_ref, t_ref, dw_ref, acc):
        s = pl.program_id(1)

        @pl.when(s == 0)
        def _():
            acc[...] = jnp.zeros_like(acc)

        xnv = xn_ref[...]
        acc[0] += _dot(t_ref[0], xnv, "tn")
        acc[1] += _dot(t_ref[1], xnv, "tn")
        acc[2] += _dot(t_ref[2], dout_ref[...], "tn")

        @pl.when(s == ns - 1)
        def _():
            dw_ref[...] = acc[...].astype(BF16)

    row = pl.BlockSpec((ts, D), lambda f, s: (s, 0))
    return pl.pallas_call(
        body, name=name, grid=(F // tf, ns),
        in_specs=[row, row, pl.BlockSpec((3, ts, tf), lambda f, s: (0, s, f))],
        out_specs=pl.BlockSpec((3, tf, D), lambda f, s: (0, f, 0)),
        out_shape=jax.ShapeDtypeStruct((3, F, D), BF16),
        scratch_shapes=[pltpu.VMEM((3, tf, D), F32)],
        compiler_params=_params("parallel", "arbitrary"),
    )(xn, dout, t)


def _loss_grad(x, ffn, target, name):
    S, D = x.shape
    tm = _tile(S, 512, 8)

    def body(x_ref, f_ref, t_ref, dy_ref, l_ref):
        @pl.when(pl.program_id(0) == 0)
        def _():
            l_ref[...] = jnp.zeros_like(l_ref)

        e = (x_ref[...] + 0.5 * f_ref[...]) - t_ref[...]
        dy_ref[...] = e * (1.0 / D)
        l_ref[...] += 0.5 * jnp.sum(jnp.sum(e * e, axis=-1, keepdims=True) * (1.0 / D))

    row = pl.BlockSpec((tm, D), lambda i: (i, 0))
    one = pl.BlockSpec((8, LANE), lambda i: (0, 0))
    return pl.pallas_call(
        body, name=name, grid=(S // tm,),
        in_specs=[row, row, row], out_specs=[row, one],
        out_shape=[jax.ShapeDtypeStruct((S, D), F32), jax.ShapeDtypeStruct((8, LANE), F32)],
        compiler_params=_params("arbitrary"),
    )(x, ffn, target)


def _adamw(w, g, m, v, name):
    shape = w.shape
    C = shape[-1]
    R = math.prod(shape[:-1])
    tr = _tile(R, max(8, (1 << 19) // C // 8 * 8), 8)
    c1 = 1.0 / (1.0 - ADAM_B1 ** ADAM_STEP)
    c2 = 1.0 / (1.0 - ADAM_B2 ** ADAM_STEP)

    def body(w_ref, g_ref, m_ref, v_ref, d_ref, nm_ref, nv_ref):
        gv = g_ref[...]
        nm = ADAM_B1 * m_ref[...] + (1.0 - ADAM_B1) * gv
        nv = ADAM_B2 * v_ref[...] + (1.0 - ADAM_B2) * (gv * gv)
        nm_ref[...] = nm
        nv_ref[...] = nv
        d_ref[...] = -ADAM_LR * ((nm * c1) / (jnp.sqrt(nv * c2) + ADAM_EPS) + ADAM_WD * w_ref[...])

    blk = pl.BlockSpec((tr, C), lambda i: (i, 0))
    sds = jax.ShapeDtypeStruct((R, C), F32)
    outs = pl.pallas_call(
        body, name=name, grid=(R // tr,),
        in_specs=[blk] * 4, out_specs=[blk] * 3, out_shape=[sds] * 3,
        compiler_params=_params("parallel"),
    )(*(a.reshape(R, C) for a in (w, g, m, v)))
    return tuple(o.reshape(shape) for o in outs)


def _me():
    return lax.axis_index("x"), lax.axis_index("y"), lax.axis_index("c")


def _add_core_halves(grad, got, c_idx, name):
    n, nk, _, r, C = grad.shape
    tr = _tile(r, 1024, SUBLANE_BF16)

    def body(c_ref, g_ref, r_ref, o_ref):
        o_ref[...] = (g_ref[...].astype(F32) + r_ref[...].astype(F32)).astype(BF16)

    return pl.pallas_call(
        body, name=name,
        grid_spec=pltpu.PrefetchScalarGridSpec(
            num_scalar_prefetch=1, grid=(n, nk, r // tr),
            in_specs=[pl.BlockSpec((None, None, None, tr, C), lambda i, k, t, c: (i, k, c[0], t, 0)),
                      pl.BlockSpec((None, None, tr, C), lambda i, k, t, c: (i, k, t, 0))],
            out_specs=pl.BlockSpec((None, None, tr, C), lambda i, k, t, c: (i, k, t, 0))),
        out_shape=jax.ShapeDtypeStruct((n, nk, r, C), BF16),
        compiler_params=_params("parallel", "parallel", "parallel"),
    )(c_idx, grad, got)


HBM = pl.BlockSpec(memory_space=pltpu.HBM)
SEM = pl.BlockSpec(memory_space=pltpu.SEMAPHORE)
EFFECT = pltpu.SideEffectType.DATAFLOW_SIDE_EFFECTING


def _push_start(srcs, lands, plan, after, name):
    ns, nl = len(srcs), len(lands)
    ncp = len(plan([None] * ns, [None] * nl, dry=True))
    extra = [] if after is None else [after]

    def body(*refs):
        src_refs, land_refs = refs[:ns], refs[ns:ns + nl]
        send_sems, recv_sems = refs[ns + nl + len(extra)], refs[ns + nl + len(extra) + 1]
        token = refs[-1]
        for i, (s, d, to) in enumerate(plan(src_refs, land_refs)):
            pltpu.make_async_remote_copy(src_ref=s, dst_ref=d, send_sem=send_sems.at[i], recv_sem=recv_sems.at[i],
                                         device_id=to, device_id_type=MESH).start()
        token[...] = jnp.zeros_like(token)

    out = pl.pallas_call(
        body, name=name,
        out_shape=(pltpu.SemaphoreType.DMA((ncp,)), pltpu.SemaphoreType.DMA((ncp,)),
                   *[pltpu.HBM(a.shape, a.dtype) for a in srcs], *[pltpu.HBM(a.shape, a.dtype) for a in lands],
                   jax.ShapeDtypeStruct((8, LANE), F32)),
        in_specs=[HBM] * (ns + nl) + [ANY] * len(extra),
        out_specs=(SEM, SEM, *[HBM] * (ns + nl), pl.BlockSpec(memory_space=pltpu.VMEM)),
        input_output_aliases={i: 2 + i for i in range(ns + nl)},
        compiler_params=pltpu.CompilerParams(has_side_effects=EFFECT),
    )(*[pltpu.with_memory_space_constraint(a, pltpu.HBM) for a in srcs + lands], *extra)
    return out[0], out[1], list(out[2:2 + ns]), list(out[2 + ns:2 + ns + nl]), out[-1]


def _push_wait(send_sems, recv_sems, srcs, lands, plan, after, name):
    ns, nl = len(srcs), len(lands)
    after = list(after) if isinstance(after, (list, tuple)) else [after]

    def body(*refs):
        src_refs, land_refs = refs[:ns], refs[ns:ns + nl]
        send, recv = refs[ns + nl], refs[ns + nl + 1]
        for i, (s, d, to) in enumerate(plan(src_refs, land_refs)):
            cp = pltpu.make_async_remote_copy(src_ref=s, dst_ref=d, send_sem=send.at[i], recv_sem=recv.at[i],
                                              device_id=to, device_id_type=MESH)
            cp.wait_send()
            cp.wait_recv()

    out = pl.pallas_call(
        body, name=name,
        out_shape=tuple(pltpu.HBM(a.shape, a.dtype) for a in srcs + lands),
        in_specs=[HBM] * (ns + nl) + [SEM, SEM] + [ANY] * len(after),
        out_specs=tuple([HBM] * (ns + nl)),
        input_output_aliases={i: i for i in range(ns + nl)},
        compiler_params=pltpu.CompilerParams(has_side_effects=EFFECT),
    )(*srcs, *lands, send_sems, recv_sems, *after)
    return list(out[:ns]), list(out[ns:])


def _gather_plan(src_refs, land_refs, dry=False):
    if dry:
        return [None] * (4 * len(src_refs))
    x, y, c = _me()
    me = 4 * x + 2 * y + c
    targets = [(x, y, 1 - c), (1 - x, y, c), (x, 1 - y, c), (1 - x, 1 - y, c)]
    return [(s, l.at[:, me], to) for s, l in zip(src_refs, land_refs) for to in targets]


def _halves_plan(src_refs, land_refs, dry=False):
    if dry:
        return [None] * len(src_refs)
    x, y, c = _me()
    return [(s.at[:, :, 1 - c], l, (x, y, 1 - c)) for s, l in zip(src_refs, land_refs)]


def _chip_plan(src_refs, land_refs, dry=False):
    if dry:
        return [None] * (3 * len(src_refs))
    x, y, c = _me()
    chips = [(1 - x, y), (x, 1 - y), (1 - x, 1 - y)]
    return [(s.at[:, 2 * chip[0] + chip[1]], l.at[j], (*chip, c))
            for s, l in zip(src_refs, land_refs) for j, chip in enumerate(chips)]


def _forward_plan(src_refs, land_refs, dry=False):
    if dry:
        return [None] * (3 * len(land_refs))
    x, y, c = _me()
    chips = [(1 - x, y), (x, 1 - y), (1 - x, 1 - y)]
    plan = []
    for l in land_refs:
        for chip in chips:
            blk = l.at[:, 4 * chip[0] + 2 * chip[1] + c]
            plan.append((blk, blk, (x, y, 1 - c)))
    return plan


def _sum_chip_blocks(sums, got, k_idx, name):
    n, _, r, C = sums.shape
    tr = _tile(r, 512, SUBLANE_BF16)

    def body(k_ref, s_ref, r_ref, o_ref):
        acc = s_ref[...].astype(F32)
        for j in range(3):
            acc = acc + r_ref[j].astype(F32)
        o_ref[...] = acc

    return pl.pallas_call(
        body, name=name,
        grid_spec=pltpu.PrefetchScalarGridSpec(
            num_scalar_prefetch=1, grid=(n, r // tr),
            in_specs=[pl.BlockSpec((None, None, tr, C), lambda i, t, k: (i, k[0], t, 0)),
                      pl.BlockSpec((3, None, tr, C), lambda i, t, k: (0, i, t, 0))],
            out_specs=pl.BlockSpec((None, tr, C), lambda i, t, k: (i, t, 0))),
        out_shape=jax.ShapeDtypeStruct((n, r, C), F32),
        compiler_params=_params("parallel", "parallel"),
    )(k_idx, sums, got)


def _all_reduce_small(v, name, after=()):
    R = v.shape[0]
    after = list(after)

    def body(*refs):
        v_ref = refs[0]
        o_ref, buf, send_sems, recv_sems = refs[1 + len(after):]
        x, y, c = _me()
        me = 4 * x + 2 * y + c
        buf[me] = v_ref[...]
        copies = []
        for k in range(1, N_DEV):
            peer = (x ^ (k >> 2), y ^ ((k >> 1) & 1), c ^ (k & 1))
            copies.append(pltpu.make_async_remote_copy(
                src_ref=v_ref, dst_ref=buf.at[me],
                send_sem=send_sems.at[k - 1], recv_sem=recv_sems.at[k - 1],
                device_id=peer, device_id_type=MESH))
        for cp in copies:
            cp.start()
        for cp in copies:
            cp.wait()
        acc = buf[0]
        for d in range(1, N_DEV):
            acc = acc + buf[d]
        o_ref[...] = acc

    vm = pl.BlockSpec(memory_space=pltpu.VMEM)
    return pl.pallas_call(
        body, name=name, in_specs=[vm] + [ANY] * len(after), out_specs=vm,
        out_shape=jax.ShapeDtypeStruct((R, LANE), F32),
        scratch_shapes=[pltpu.VMEM((N_DEV, R, LANE), F32),
                        pltpu.SemaphoreType.DMA((N_DEV - 1,)), pltpu.SemaphoreType.DMA((N_DEV - 1,))],
        compiler_params=pltpu.CompilerParams(vmem_limit_bytes=VMEM_LIMIT_BYTES),
    )(v, *after)


def _pack_rows(parts):
    flat = jnp.concatenate([p.reshape(-1).astype(F32) for p in parts])
    n = flat.shape[0]
    rows = -(-n // (8 * LANE)) * 8
    flat = jnp.pad(flat, (0, rows * LANE - n))
    return flat.reshape(rows, LANE)


def _unpack_rows(packed, shapes):
    flat = packed.reshape(-1)
    out, off = [], 0
    for s in shapes:
        n = math.prod(s)
        out.append(flat[off:off + n].reshape(s))
        off += n
    return out


CONV_HALO = 32


def _conv_fwd(u, conv_w, conv_b, cn_g, cn_b, name):
    S = u.shape[0]
    C = conv_w.shape[1]
    T = _tile(S, 256, CONV_HALO)
    hb = T // CONV_HALO

    def body(av_ref, ag_ref, pv_ref, pg_ref, w_ref, b_ref, g_ref, bb_ref, out_ref, y_ref, scr):
        i = pl.program_id(0)
        prev = pv_ref[...] * jax.nn.sigmoid(pg_ref[...])
        scr[0:CONV_HALO, :] = jnp.where(i > 0, prev, 0.0)
        scr[CONV_HALO:CONV_HALO + T, :] = av_ref[...] * jax.nn.sigmoid(ag_ref[...])
        for s in range(C // LANE):
            sl = slice(s * LANE, (s + 1) * LANE)
            acc = jnp.broadcast_to(b_ref[:, sl], (T, LANE))
            for j in range(CONV_WIDTH):
                acc = acc + w_ref[j:j + 1, sl] * scr[pl.ds(CONV_HALO - (CONV_WIDTH - 1) + j, T), sl]
            y_ref[:, sl] = acc
        acc = y_ref[...]
        mu = jnp.mean(acc, axis=-1, keepdims=True)
        xc = acc - mu
        var = jnp.mean(xc * xc, axis=-1, keepdims=True)
        ln = xc * lax.rsqrt(var + EPS) * g_ref[...] + bb_ref[...]
        out_ref[...] = (ln * jax.nn.sigmoid(ln)).astype(BF16)

    def cur(cb):
        return pl.BlockSpec((T, C), lambda i: (i, cb))

    def halo(cb):
        return pl.BlockSpec((CONV_HALO, C), lambda i: (jnp.maximum(i * hb - 1, 0), cb))

    vec = pl.BlockSpec((1, C), lambda i: (0, 0))
    return pl.pallas_call(
        body, name=name, grid=(S // T,),
        in_specs=[cur(0), cur(1), halo(0), halo(1), pl.BlockSpec((CONV_WIDTH, C), lambda i: (0, 0)), vec, vec, vec],
        out_specs=[pl.BlockSpec((T, C), lambda i: (i, 0))] * 2,
        out_shape=[jax.ShapeDtypeStruct((S, C), BF16), jax.ShapeDtypeStruct((S, C), F32)],
        scratch_shapes=[pltpu.VMEM((T + CONV_HALO, C), F32)],
        compiler_params=_params("parallel"),
    )(u, u, u, u, conv_w, conv_b, cn_g, cn_b)


def _conv_bwd_norm(dz, y, cn_g, cn_b, name):
    S, C = y.shape
    T = _tile(S, 256, 8)

    def body(dz_ref, y_ref, g_ref, bb_ref, dy_ref, dg_ref, db_ref):
        @pl.when(pl.program_id(0) == 0)
        def _():
            dg_ref[...] = jnp.zeros_like(dg_ref)
            db_ref[...] = jnp.zeros_like(db_ref)

        yv = y_ref[...]
        mu = jnp.mean(yv, axis=-1, keepdims=True)
        xc = yv - mu
        rstd = lax.rsqrt(jnp.mean(xc * xc, axis=-1, keepdims=True) + EPS)
        xh = xc * rstd
        ln = xh * g_ref[...] + bb_ref[...]
        sg = jax.nn.sigmoid(ln)
        dln = dz_ref[...] * (sg * (1.0 + ln * (1.0 - sg)))
        dg_ref[...] += jnp.sum(dln * xh, axis=0, keepdims=True)
        db_ref[...] += jnp.sum(dln, axis=0, keepdims=True)
        dxh = dln * g_ref[...]
        dy_ref[...] = rstd * (dxh - jnp.mean(dxh, axis=-1, keepdims=True)
                              - xh * jnp.mean(dxh * xh, axis=-1, keepdims=True))

    row = pl.BlockSpec((T, C), lambda i: (i, 0))
    vec = pl.BlockSpec((1, C), lambda i: (0, 0))
    return pl.pallas_call(
        body, name=name, grid=(S // T,),
        in_specs=[row, row, vec, vec], out_specs=[row, vec, vec],
        out_shape=[jax.ShapeDtypeStruct((S, C), F32), jax.ShapeDtypeStruct((1, C), F32),
                   jax.ShapeDtypeStruct((1, C), F32)],
        compiler_params=_params("arbitrary"),
    )(dz, y, cn_g, cn_b)


def _conv_bwd_taps(u, dy, conv_w, name):
    S, C = dy.shape
    T = _tile(S, 256, CONV_HALO)
    hb = T // CONV_HALO
    nt = S // T
    ns = C // LANE
    W1 = CONV_WIDTH - 1

    def body(av_ref, ag_ref, pv_ref, pg_ref, dy_ref, dn_ref, w_ref, dv_ref, dg_ref, dw_ref, db_ref, a_scr, d_scr):
        i = pl.program_id(1)

        @pl.when(i == 0)
        def _():
            dw_ref[...] = jnp.zeros_like(dw_ref)
            db_ref[...] = jnp.zeros_like(db_ref)

        av, sg = av_ref[...], jax.nn.sigmoid(ag_ref[...])
        prev = pv_ref[...] * jax.nn.sigmoid(pg_ref[...])
        a_scr[0:CONV_HALO, :] = jnp.where(i > 0, prev, 0.0)
        a_scr[CONV_HALO:CONV_HALO + T, :] = av * sg
        dyv = dy_ref[...]
        d_scr[0:T, :] = dyv
        d_scr[T:T + CONV_HALO, :] = jnp.where(i < nt - 1, dn_ref[...], 0.0)
        da = jnp.zeros((T, LANE), F32)
        for j in range(CONV_WIDTH):
            da = da + w_ref[j:j + 1, :] * d_scr[pl.ds(W1 - j, T), :]
            dw_ref[j:j + 1, :] += jnp.sum(dyv * a_scr[pl.ds(CONV_HALO - W1 + j, T), :], axis=0, keepdims=True)
        db_ref[...] += jnp.sum(dyv, axis=0, keepdims=True)
        dv_ref[...] = (da * sg).astype(BF16)
        dg_ref[...] = (da * av * sg * (1.0 - sg)).astype(BF16)

    def cur(part):
        return pl.BlockSpec((T, LANE), lambda cb, i: (i, part * ns + cb))

    def halo(part):
        return pl.BlockSpec((CONV_HALO, LANE), lambda cb, i: (jnp.maximum(i * hb - 1, 0), part * ns + cb))

    nxt = pl.BlockSpec((CONV_HALO, LANE), lambda cb, i: (jnp.minimum((i + 1) * hb, S // CONV_HALO - 1), cb))
    row = pl.BlockSpec((T, LANE), lambda cb, i: (i, cb))
    return pl.pallas_call(
        body, name=name, grid=(ns, nt),
        in_specs=[cur(0), cur(1), halo(0), halo(1), row, nxt, pl.BlockSpec((CONV_WIDTH, LANE), lambda cb, i: (0, cb))],
        out_specs=[row, row, pl.BlockSpec((CONV_HALO, LANE), lambda cb, i: (0, cb)),
                   pl.BlockSpec((1, LANE), lambda cb, i: (0, cb))],
        out_shape=[jax.ShapeDtypeStruct((S, C), BF16), jax.ShapeDtypeStruct((S, C), BF16),
                   jax.ShapeDtypeStruct((CONV_HALO, C), F32), jax.ShapeDtypeStruct((1, C), F32)],
        scratch_shapes=[pltpu.VMEM((T + CONV_HALO, LANE), F32), pltpu.VMEM((T + CONV_HALO, LANE), F32)],
        compiler_params=_params("parallel", "arbitrary"),
    )(u, u, u, u, dy, dy, conv_w)


def _rope_tables(S):
    half = HEAD_DIM // 2
    inv = jnp.exp(-math.log(ROPE_THETA) * jnp.arange(half, dtype=F32) / half)
    ang = jnp.arange(S, dtype=jnp.int32).astype(F32)[:, None] * inv[None, :]
    cos, sin = jnp.cos(ang), jnp.sin(ang)
    return jnp.concatenate([cos, cos], axis=1), jnp.concatenate([-sin, sin], axis=1)


def _qkv_prep(u, qn_g, kn_g, cos, sin, cb0, name):
    S = u.shape[0]
    A = (u.shape[1] // (cb0 + 3))
    H = A // HEAD_DIM
    T = _tile(S, 256, SUBLANE_BF16)
    scale = HEAD_DIM ** -0.5

    def body(q_ref, k_ref, v_ref, qg_ref, kg_ref, cos_ref, sin_ref, qo_ref, ko_ref, vo_ref):
        cosv, sinv = cos_ref[...], sin_ref[...]
        for h in range(H):
            sl = slice(h * HEAD_DIM, (h + 1) * HEAD_DIM)
            for x_ref, g_ref, o_ref, sc in ((q_ref, qg_ref, qo_ref, scale), (k_ref, kg_ref, ko_ref, 1.0)):
                xv = x_ref[:, sl]
                xn = xv * lax.rsqrt(jnp.mean(xv * xv, axis=-1, keepdims=True) + EPS) * g_ref[...]
                y = xn * cosv + pltpu.roll(xn, HEAD_DIM // 2, 1) * sinv
                o_ref[:, sl] = (y * sc).astype(BF16)
        vo_ref[...] = v_ref[...].astype(BF16)

    def col(cb):
        return pl.BlockSpec((T, A), lambda i: (i, cb))

    vec = pl.BlockSpec((1, HEAD_DIM), lambda i: (0, 0))
    tab = pl.BlockSpec((T, HEAD_DIM), lambda i: (i, 0))
    out = pl.BlockSpec((T, A), lambda i: (i, 0))
    return pl.pallas_call(
        body, name=name, grid=(S // T,),
        in_specs=[col(cb0), col(cb0 + 1), col(cb0 + 2), vec, vec, tab, tab],
        out_specs=[out] * 3, out_shape=[jax.ShapeDtypeStruct((S, A), BF16)] * 3,
        compiler_params=_params("parallel"),
    )(u, u, u, qn_g, kn_g, cos, sin)


def _qkv_prep_bwd(u, dqs, dks, dvs, qn_g, kn_g, cos, sin, cb0, name):
    S = u.shape[0]
    A = dqs[0].shape[1]
    H = A // HEAD_DIM
    T = _tile(S, 256, SUBLANE_BF16)
    nb = len(dqs)
    scale = HEAD_DIM ** -0.5

    def body(*refs):
        q_ref, k_ref, qg_ref, kg_ref, cos_ref, sin_ref = refs[:6]
        dq_refs, dk_refs, dv_refs = refs[6:6 + nb], refs[6 + nb:6 + 2 * nb], refs[6 + 2 * nb:6 + 3 * nb]
        dqo_ref, dko_ref, dvo_ref, dqg_ref, dkg_ref = refs[6 + 3 * nb:]

        @pl.when(pl.program_id(0) == 0)
        def _():
            dqg_ref[...] = jnp.zeros_like(dqg_ref)
            dkg_ref[...] = jnp.zeros_like(dkg_ref)

        cosv, sinv = cos_ref[...], sin_ref[...]
        for h in range(H):
            sl = slice(h * HEAD_DIM, (h + 1) * HEAD_DIM)
            for x_ref, g_ref, d_refs, o_ref, dg_ref, sc in ((q_ref, qg_ref, dq_refs, dqo_ref, dqg_ref, scale),
                                                          (k_ref, kg_ref, dk_refs, dko_ref, dkg_ref, 1.0)):
                dy = d_refs[0][:, sl]
                for r in d_refs[1:]:
                    dy = dy + r[:, sl]
                dy = dy * sc
                dxn = dy * cosv + pltpu.roll(dy * sinv, HEAD_DIM // 2, 1)
                xv = x_ref[:, sl]
                r = lax.rsqrt(jnp.mean(xv * xv, axis=-1, keepdims=True) + EPS)
                xh = xv * r
                dg_ref[...] += jnp.sum(dxn * xh, axis=0, keepdims=True)
                dxh = dxn * g_ref[...]
                o_ref[:, sl] = (r * (dxh - xh * jnp.mean(dxh * xh, axis=-1, keepdims=True))).astype(BF16)
        dv = dv_refs[0][...]
        for r in dv_refs[1:]:
            dv = dv + r[...]
        dvo_ref[...] = dv.astype(BF16)

    def col(cb):
        return pl.BlockSpec((T, A), lambda i: (i, cb))

    vec = pl.BlockSpec((1, HEAD_DIM), lambda i: (0, 0))
    tab = pl.BlockSpec((T, HEAD_DIM), lambda i: (i, 0))
    row = pl.BlockSpec((T, A), lambda i: (i, 0))
    return pl.pallas_call(
        body, name=name, grid=(S // T,),
        in_specs=[col(cb0), col(cb0 + 1), vec, vec, tab, tab] + [row] * (3 * nb),
        out_specs=[row, row, row, vec, vec],
        out_shape=[jax.ShapeDtypeStruct((S, A), BF16)] * 3 + [jax.ShapeDtypeStruct((1, HEAD_DIM), F32)] * 2,
        compiler_params=_params("arbitrary"),
    )(u, u, qn_g, kn_g, cos, sin, *dqs, *dks, *dvs)


ATT_TILE = 256
NEG = -1e30


def _attn_bias(tile):
    span = max(window for window, _ in DIL_PATTERNS)
    nw = -(-span // tile) + 1
    dist = (jnp.arange(nw)[:, None, None] * tile + jnp.arange(tile)[None, :, None] - jnp.arange(tile)[None, None, :])
    mult = sum(((dist >= 0) & (dist <= window) & (dist % dil == 0)).astype(F32) for window, dil in DIL_PATTERNS)
    return jnp.where(mult > 0, jnp.log(jnp.maximum(mult, 1.0)), NEG)


def _attn_fwd(q, k, v, bias, name):
    S, A = q.shape
    H = A // HEAD_DIM
    nw, T, _ = bias.shape
    nq = S // T

    def body(q_ref, k_ref, v_ref, b_ref, ob_ref, of_ref, l_ref, s_scr):
        i = pl.program_id(1)
        qv = q_ref[...]
        mx = jnp.full((T, 1), NEG, F32)
        for w in range(nw):
            blk = i - w
            start = pl.multiple_of(jnp.maximum(blk, 0) * T, T)
            s = _dot(qv, k_ref[pl.ds(start, T), :], "nt") + b_ref[w] + jnp.where(blk >= 0, 0.0, NEG)
            s_scr[w] = s
            mx = jnp.maximum(mx, jnp.max(s, axis=-1, keepdims=True))
        den = jnp.zeros((T, 1), F32)
        o = jnp.zeros((T, HEAD_DIM), F32)
        for w in range(nw):
            start = pl.multiple_of(jnp.maximum(i - w, 0) * T, T)
            p = jnp.exp(s_scr[w] - mx)
            den = den + jnp.sum(p, axis=-1, keepdims=True)
            o = o + _dot(p.astype(BF16), v_ref[pl.ds(start, T), :], "nn")
        o = o / den
        ob_ref[...] = o.astype(BF16)
        of_ref[...] = o
        l_ref[...] = mx + jnp.log(den)

    blk = pl.BlockSpec((T, HEAD_DIM), lambda h, i: (i, h))
    full = pl.BlockSpec((S, HEAD_DIM), lambda h, i: (0, h))
    return pl.pallas_call(
        body, name=name, grid=(H, nq),
        in_specs=[blk, full, full, pl.BlockSpec((nw, T, T), lambda h, i: (0, 0, 0))],
        out_specs=[blk, blk, pl.BlockSpec((None, T, 1), lambda h, i: (h, i, 0))],
        out_shape=[jax.ShapeDtypeStruct((S, A), BF16), jax.ShapeDtypeStruct((S, A), F32),
                   jax.ShapeDtypeStruct((H, S, 1), F32)],
        scratch_shapes=[pltpu.VMEM((nw, T, T), F32)],
        compiler_params=_params("parallel", "arbitrary"),
    )(q, k, v, bias)


def _attn_dq(q, k, v, dz, cb0, o, lse, bias, name):
    S, A = q.shape
    H = A // HEAD_DIM
    nw, T, _ = bias.shape
    nq = S // T

    def body(q_ref, k_ref, v_ref, do_ref, o_ref, l_ref, b_ref, dq_ref, d_ref):
        i = pl.program_id(1)
        qv, dof = q_ref[...], do_ref[...]
        dov = dof.astype(BF16)
        delta = jnp.sum(dof * o_ref[...], axis=-1, keepdims=True)
        d_ref[...] = delta
        lv = l_ref[...]
        dq = jnp.zeros((T, HEAD_DIM), F32)
        for w in range(nw):
            blk = i - w
            start = pl.multiple_of(jnp.maximum(blk, 0) * T, T)
            kv = k_ref[pl.ds(start, T), :]
            s = _dot(qv, kv, "nt") + b_ref[w] + jnp.where(blk >= 0, 0.0, NEG)
            p = jnp.exp(s - lv)
            ds = (p * (_dot(dov, v_ref[pl.ds(start, T), :], "nt") - delta)).astype(BF16)
            dq = dq + _dot(ds, kv, "nn")
        dq_ref[...] = dq

    blk = pl.BlockSpec((T, HEAD_DIM), lambda h, i: (i, h))
    full = pl.BlockSpec((S, HEAD_DIM), lambda h, i: (0, h))
    col = pl.BlockSpec((None, T, 1), lambda h, i: (h, i, 0))
    return pl.pallas_call(
        body, name=name, grid=(H, nq),
        in_specs=[blk, full, full, pl.BlockSpec((T, HEAD_DIM), lambda h, i: (i, cb0 + h)), blk, col,
                  pl.BlockSpec((nw, T, T), lambda h, i: (0, 0, 0))],
        out_specs=[blk, col],
        out_shape=[jax.ShapeDtypeStruct((S, A), F32), jax.ShapeDtypeStruct((H, S, 1), F32)],
        compiler_params=_params("parallel", "arbitrary"),
    )(q, k, v, dz, o, lse, bias)


def _attn_dkv(q, k, v, dz, cb0, lse, delta, bias, name):
    S, A = q.shape
    H = A // HEAD_DIM
    nw, T, _ = bias.shape
    nq = S // T

    def body(k_ref, v_ref, q_ref, do_ref, l_ref, d_ref, b_ref, dk_ref, dv_ref):
        m = pl.program_id(1)
        kv, vv = k_ref[...], v_ref[...]
        dk = jnp.zeros((T, HEAD_DIM), F32)
        dv = jnp.zeros((T, HEAD_DIM), F32)
        for w in range(nw):
            blk = m + w
            start = pl.multiple_of(jnp.minimum(blk, nq - 1) * T, T)
            qv = q_ref[pl.ds(start, T), :]
            dov = do_ref[pl.ds(start, T), :].astype(BF16)
            s = _dot(qv, kv, "nt") + b_ref[w] + jnp.where(blk < nq, 0.0, NEG)
            p = jnp.exp(s - l_ref[pl.ds(start, T), :])
            dv = dv + _dot(p.astype(BF16), dov, "tn")
            ds = (p * (_dot(dov, vv, "nt") - d_ref[pl.ds(start, T), :])).astype(BF16)
            dk = dk + _dot(ds, qv, "tn")
        dk_ref[...] = dk
        dv_ref[...] = dv

    blk = pl.BlockSpec((T, HEAD_DIM), lambda h, m: (m, h))
    full = pl.BlockSpec((S, HEAD_DIM), lambda h, m: (0, h))
    col = pl.BlockSpec((None, S, 1), lambda h, m: (h, 0, 0))
    sds = jax.ShapeDtypeStruct((S, A), F32)
    return pl.pallas_call(
        body, name=name, grid=(H, nq),
        in_specs=[blk, blk, full, pl.BlockSpec((S, HEAD_DIM), lambda h, m: (0, cb0 + h)), col, col,
                  pl.BlockSpec((nw, T, T), lambda h, m: (0, 0, 0))],
        out_specs=[blk, blk], out_shape=[sds, sds],
        compiler_params=_params("parallel", "arbitrary"),
    )(k, v, q, dz, lse, delta, bias)


def _even_mixer(u, conv_w, conv_b, cn_g, cn_b, qn_g, kn_g, tag):
    S = u.shape[0]
    C = conv_w.shape[1]
    A = (u.shape[1] - 2 * C) // 3
    assert A == C, "column-block addressing of u assumes equal conv and attention widths"
    T = _tile(S, ATT_TILE, LANE)
    cos, sin = _rope_tables(S)
    bias = _attn_bias(T)
    a_out, y = _conv_fwd(u, conv_w, conv_b, cn_g, cn_b, "conv_fwd" + tag)
    q, k, v = _qkv_prep(u, qn_g, kn_g, cos, sin, 2, "qkv_prep" + tag)
    ob, of, lse = _attn_fwd(q, k, v, bias, "attn_fwd" + tag)
    z = jnp.concatenate([a_out, ob], axis=1)

    def backward(dz):
        dy, d_cn_g, d_cn_b = _conv_bwd_norm(dz, y, cn_g, cn_b, "conv_bwd_norm" + tag)
        d_val, d_gate, d_w, d_b = _conv_bwd_taps(u, dy, conv_w, "conv_bwd_taps" + tag)
        dqp, delta = _attn_dq(q, k, v, dz, C // HEAD_DIM, of, lse, bias, "attn_dq" + tag)
        dkp, dvp = _attn_dkv(q, k, v, dz, C // HEAD_DIM, lse, delta, bias, "attn_dkv" + tag)
        dq, dk, dv, d_qn, d_kn = _qkv_prep_bwd(u, [dqp], [dkp], [dvp], qn_g, kn_g, cos, sin, 2, "qkv_prep_bwd" + tag)
        du = jnp.concatenate([d_val, d_gate, dq, dk, dv], axis=1)
        return du, [d_w[:CONV_WIDTH], d_b[0], d_cn_g[0], d_cn_b[0], d_qn[0], d_kn[0]]

    return z, backward


_LEVELS = (128, 64, 32, 16, 8, 4, 2, 1)


def _chunk_cumsum(g, rows, reverse=False):
    C = g.shape[0]
    d = 1
    while d < C:
        if reverse:
            g = g + jnp.where(rows < C - d, pltpu.roll(g, C - d, 0), 0.0)
        else:
            g = g + jnp.where(rows >= d, pltpu.roll(g, d, 0), 0.0)
        d *= 2
    return g


def _level_ref(b, b_scr, rows, m):
    C = b.shape[0]
    if m >= 8:
        pieces = [jnp.broadcast_to(b_scr[2 * m * j + m - 1:2 * m * j + m, :], (2 * m, LANE)) for j in range(C // (2 * m))]
        return pieces[0] if len(pieces) == 1 else jnp.concatenate(pieces, axis=0)
    pos = rows & (2 * m - 1)
    ref = b
    for p in range(2 * m):
        if p != m - 1:
            ref = jnp.where(pos == p, pltpu.roll(b, (p - (m - 1)) % C, 0), ref)
    return ref


def _level_operands(q, k, b, b_scr, rows, m):
    ref = _level_ref(b, b_scr, rows, m)
    qs = (q * jnp.exp(jnp.minimum(b - ref, 0.0))).astype(BF16)
    ks = (k * jnp.exp(jnp.minimum(ref - b, 0.0))).astype(BF16)
    return qs, ks


def _split2(x):
    hi = x.astype(BF16)
    lo = (x - hi.astype(F32)).astype(BF16)
    return jnp.concatenate([hi, lo], axis=1)


def _level_table(n):
    t = jnp.arange(n, dtype=jnp.int32)[:, None]
    s = jnp.arange(n, dtype=jnp.int32)[None, :]
    x = t ^ s
    lvl = sum((x >= (1 << j)).astype(jnp.int32) for j in range(1, n.bit_length()))
    return jnp.where(t > s, lvl, jnp.where(t == s, -1, -2))


def _hgrn_gates(qz, fz, la, lc, oml):
    sq = jax.nn.sigmoid(qz)
    q = qz * sq
    s = jax.nn.sigmoid(fz)
    c = lc + jnp.minimum(fz, 0.0) - jnp.log(1.0 + jnp.exp(-jnp.abs(fz)))
    mx = jnp.maximum(la, c)
    g = mx + jnp.log(1.0 + jnp.exp(-jnp.abs(la - c)))
    k = oml * (1.0 - s)
    return q, sq, k, s, g, c


def _hgrn_fwd(u, la, lc, oml, gn_g, name):
    S = u.shape[0]
    W = u.shape[1] // 4
    H = W // HGRN_KDIM
    C = min(HGRN_CHUNK, S)
    nc = S // C
    levels = [m for m in _LEVELS if m < C]
    HB = C // 2

    def body(qz_ref, fz_ref, iz_ref, gz_ref, la_ref, lc_ref, oml_ref, gn_ref, lvl_ref,
             z_ref, o_ref, a_ref, st_ref, state, b_scr):
        @pl.when(pl.program_id(1) == 0)
        def _():
            state[...] = jnp.zeros_like(state)

        rows = lax.broadcasted_iota(jnp.int32, (C, LANE), 0)
        q, _, k, _, g, _ = _hgrn_gates(qz_ref[...], fz_ref[...], la_ref[...], lc_ref[...], oml_ref[...])
        v = iz_ref[...].astype(BF16)
        b = _chunk_cumsum(g, rows)
        b_scr[...] = b
        lvl = lvl_ref[...]
        qk = jnp.sum(q * k, axis=-1, keepdims=True)
        diag = [jnp.where(lvl == -1, qk[r * HB:(r + 1) * HB], 0.0) for r in range(2)]
        for m in levels[1:]:
            qs, ks = _level_operands(q, k, b, b_scr, rows, m)
            for r in range(2):
                sl = slice(r * HB, (r + 1) * HB)
                diag[r] = jnp.where(lvl == m.bit_length() - 1, _dot(qs[sl], ks[sl], "nt"), diag[r])
        qs, ks = _level_operands(q, k, b, b_scr, rows, HB)
        low = _dot(qs[HB:], ks[:HB], "nt")
        a = jnp.concatenate([jnp.concatenate([diag[0], jnp.zeros((HB, HB), F32)], axis=1),
                             jnp.concatenate([low, diag[1]], axis=1)], axis=0)
        ab = a.astype(BF16)
        a_ref[...] = ab
        st = state[...]
        st_ref[...] = st
        o = _dot(ab, v, "nn") + _dot((q * jnp.exp(b)).astype(BF16), st.astype(BF16), "nt")
        bl = b_scr[C - 1:C, :]
        kh = (k * jnp.exp(bl - b)).astype(BF16)
        state[...] = st * jnp.exp(bl) + _dot(v, kh, "tn")
        o_ref[...] = o
        r = lax.rsqrt(jnp.mean(o * o, axis=-1, keepdims=True) + EPS)
        gz = gz_ref[...]
        z_ref[...] = (o * r * gn_ref[...] * (gz * jax.nn.sigmoid(gz))).astype(BF16)

    def col(off):
        return pl.BlockSpec((C, LANE), lambda h, i: (i, off * H + h))

    vec = pl.BlockSpec((1, LANE), lambda h, i: (0, h))
    tile = pl.BlockSpec((C, LANE), lambda h, i: (i, h))
    return pl.pallas_call(
        body, name=name, grid=(H, nc),
        in_specs=[col(0), col(1), col(2), col(3), vec, vec, vec, vec, pl.BlockSpec((HB, HB), lambda h, i: (0, 0))],
        out_specs=[tile, tile, pl.BlockSpec((None, C, C), lambda h, i: (h, i, 0)),
                   pl.BlockSpec((None, None, LANE, LANE), lambda h, i: (h, i, 0, 0))],
        out_shape=[jax.ShapeDtypeStruct((S, W), BF16), jax.ShapeDtypeStruct((S, W), F32),
                   jax.ShapeDtypeStruct((H, S, C), BF16), jax.ShapeDtypeStruct((H, nc, LANE, LANE), F32)],
        scratch_shapes=[pltpu.VMEM((LANE, LANE), F32), pltpu.VMEM((C, LANE), F32)],
        compiler_params=_params("parallel", "arbitrary"),
    )(u, u, u, u, la, lc, oml, gn_g, _level_table(HB))


def _hgrn_bwd(u, la, lc, oml, gn_g, o, a, st, dz, name):
    S = u.shape[0]
    W = u.shape[1] // 4
    H = W // HGRN_KDIM
    C = min(HGRN_CHUNK, S)
    nc = S // C
    levels = [m for m in _LEVELS if m < C]
    HB = C // 2

    def body(qz_ref, fz_ref, iz_ref, gz_ref, la_ref, lc_ref, oml_ref, gn_ref, o_ref, a_ref, st_ref, dz_ref, lvl_ref,
             dqz_ref, dfz_ref, diz_ref, dgz_ref, dla_ref, dlc_ref, doml_ref, dgn_ref, dstate, b_scr):
        @pl.when(pl.program_id(1) == 0)
        def _():
            dstate[...] = jnp.zeros_like(dstate)
            dla_ref[...] = jnp.zeros_like(dla_ref)
            dlc_ref[...] = jnp.zeros_like(dlc_ref)
            doml_ref[...] = jnp.zeros_like(doml_ref)
            dgn_ref[...] = jnp.zeros_like(dgn_ref)

        rows = lax.broadcasted_iota(jnp.int32, (C, LANE), 0)
        la_v, lc_v, oml_v = la_ref[...], lc_ref[...], oml_ref[...]
        qz, fz = qz_ref[...], fz_ref[...]
        q, sq, k, s, g, c = _hgrn_gates(qz, fz, la_v, lc_v, oml_v)
        vf = iz_ref[...]
        v = vf.astype(BF16)

        ov, gz, dzv, gn = o_ref[...], gz_ref[...], dz_ref[...], gn_ref[...]
        r = lax.rsqrt(jnp.mean(ov * ov, axis=-1, keepdims=True) + EPS)
        on = ov * r
        sg = jax.nn.sigmoid(gz)
        silu_g = gz * sg
        dgn_ref[...] += jnp.sum(dzv * on * silu_g, axis=0, keepdims=True)
        dgz_ref[...] = (dzv * on * gn * (sg * (1.0 + gz * (1.0 - sg)))).astype(BF16)
        don = dzv * gn * silu_g
        do_f = r * (don - on * jnp.mean(don * on, axis=-1, keepdims=True))
        do = do_f.astype(BF16)

        b = _chunk_cumsum(g, rows)
        b_scr[...] = b
        bl = b_scr[C - 1:C, :]
        e = jnp.exp(b)
        ebl = jnp.exp(bl)
        ekl = jnp.exp(bl - b)
        qh = q * e
        kh = k * ekl
        st_v = st_ref[...]
        dst = dstate[...]
        dstb = dst.astype(BF16)

        diz_ref[...] = (_dot(a_ref[...], do, "tn") + _dot(kh.astype(BF16), dstb, "nt")).astype(BF16)
        da = _dot(do, v, "nt")
        dqh = _dot(do, st_v.astype(BF16), "nn")
        dkh = _dot(v, dstb, "nn")
        dstate[...] = dst * ebl + _dot(do, qh.astype(BF16), "tn")
        dbl = jnp.sum(dkh * kh, axis=0, keepdims=True) + jnp.sum(dst * st_v, axis=0, keepdims=True) * ebl

        datt = jnp.sum(do_f * vf, axis=-1, keepdims=True)
        dqa = datt * k
        dka = datt * q
        lvl = lvl_ref[...]
        for m in levels:
            ref = _level_ref(b, b_scr, rows, m)
            eu = jnp.exp(jnp.minimum(b - ref, 0.0))
            el = jnp.exp(jnp.minimum(ref - b, 0.0))
            ks2, qs2 = _split2(k * el), _split2(q * eu)
            if m == HB:
                gm = da[HB:, :HB].astype(BF16)
                pq = jnp.concatenate([jnp.zeros((HB, 2 * LANE), F32), _dot(gm, ks2[:HB], "nn")], axis=0)
                pk = jnp.concatenate([_dot(gm, qs2[HB:], "tn"), jnp.zeros((HB, 2 * LANE), F32)], axis=0)
            else:
                gms = [jnp.where(lvl == m.bit_length() - 1, da[r * HB:(r + 1) * HB, r * HB:(r + 1) * HB], 0.0).astype(BF16)
                       for r in range(2)]
                pq = jnp.concatenate([_dot(gms[r], ks2[r * HB:(r + 1) * HB], "nn") for r in range(2)], axis=0)
                pk = jnp.concatenate([_dot(gms[r], qs2[r * HB:(r + 1) * HB], "tn") for r in range(2)], axis=0)
            dqa += (pq[:, :LANE] + pq[:, LANE:]) * eu
            dka += (pk[:, :LANE] + pk[:, LANE:]) * el
        db = q * dqa - k * dka + dqh * qh - dkh * kh
        db = db + jnp.where(rows == C - 1, dbl, 0.0)
        dq = dqa + dqh * e
        dk = dka + dkh * ekl
        dg = _chunk_cumsum(db, rows, reverse=True)

        wa = jnp.exp(la_v - g)
        wc = jnp.exp(c - g)
        dqz_ref[...] = (dq * (sq * (1.0 + qz * (1.0 - sq)))).astype(BF16)
        dfz_ref[...] = (dg * wc * (1.0 - s) - dk * oml_v * s * (1.0 - s)).astype(BF16)
        dla_ref[...] += jnp.sum(dg * wa, axis=0, keepdims=True)
        dlc_ref[...] += jnp.sum(dg * wc, axis=0, keepdims=True)
        doml_ref[...] += jnp.sum(dk * (1.0 - s), axis=0, keepdims=True)

    def col(off):
        return pl.BlockSpec((C, LANE), lambda h, i: (nc - 1 - i, off * H + h))

    vec = pl.BlockSpec((1, LANE), lambda h, i: (0, h))
    tile = pl.BlockSpec((C, LANE), lambda h, i: (nc - 1 - i, h))
    a_spec = pl.BlockSpec((None, C, C), lambda h, i: (h, nc - 1 - i, 0))
    st_spec = pl.BlockSpec((None, None, LANE, LANE), lambda h, i: (h, nc - 1 - i, 0, 0))
    sw = jax.ShapeDtypeStruct((S, W), BF16)
    vw = jax.ShapeDtypeStruct((1, W), F32)
    return pl.pallas_call(
        body, name=name, grid=(H, nc),
        in_specs=[col(0), col(1), col(2), col(3), vec, vec, vec, vec, tile, a_spec, st_spec, tile,
                  pl.BlockSpec((HB, HB), lambda h, i: (0, 0))],
        out_specs=[tile, tile, tile, tile, vec, vec, vec, vec],
        out_shape=[sw, sw, sw, sw, vw, vw, vw, vw],
        scratch_shapes=[pltpu.VMEM((LANE, LANE), F32), pltpu.VMEM((C, LANE), F32)],
        compiler_params=_params("parallel", "arbitrary"),
    )(u, u, u, u, la, lc, oml, gn_g, o, a, st, dz, _level_table(HB))


def _lb_terms(lb_logits, layer):
    p = jax.nn.softmax(lb_logits, axis=0)
    lb = (jnp.cumsum(p, axis=0) - p[0:1])[layer]
    return jnp.log(lb)[None], jnp.log1p(-lb)[None], (1.0 - lb)[None]


def kernel(x, norm_ffn1, ffn1_wg, ffn1_wu, ffn1_wd, norm_mix, norm_ffn2, ffn2_wg, ffn2_wu, ffn2_wd, ev_w_in, ev_conv_w, ev_conv_b, ev_cn_g, ev_cn_b, ev_qn_g, ev_kn_g, ev_w_out, od_w_in, od_lb_logits, od_gn_g, od_w_out, loss_target, m_norm_ffn1, m_ffn1_wg, m_ffn1_wu, m_ffn1_wd, m_norm_mix, m_norm_ffn2, m_ffn2_wg, m_ffn2_wu, m_ffn2_wd, m_ev_w_in, m_ev_conv_w, m_ev_conv_b, m_ev_cn_g, m_ev_cn_b, m_ev_qn_g, m_ev_kn_g, m_ev_w_out, m_od_w_in, m_od_lb_logits, m_od_gn_g, m_od_w_out, v_norm_ffn1, v_ffn1_wg, v_ffn1_wu, v_ffn1_wd, v_norm_mix, v_norm_ffn2, v_ffn2_wg, v_ffn2_wu, v_ffn2_wd, v_ev_w_in, v_ev_conv_w, v_ev_conv_b, v_ev_cn_g, v_ev_cn_b, v_ev_qn_g, v_ev_kn_g, v_ev_w_out, v_od_w_in, v_od_lb_logits, v_od_gn_g, v_od_w_out):
    depth = norm_ffn1.shape[0]
    S, D = x.shape[1], x.shape[2]
    xi, yi, ci = _me()
    dev = 4 * xi + 2 * yi + ci
    c_idx = jnp.reshape(ci, (1,)).astype(jnp.int32)
    k_idx = jnp.reshape(2 * xi + yi, (1,)).astype(jnp.int32)

    def ffn_shard(wg, wu, wd, l):
        return jnp.stack([wg[l].T, wu[l].T, wd[l]]).astype(BF16)

    assert depth == 2, "the exchange schedule below is written for one even and one odd layer"
    sh_ffn1 = [ffn_shard(ffn1_wg, ffn1_wu, ffn1_wd, l) for l in range(depth)]
    sh_ffn2 = [ffn_shard(ffn2_wg, ffn2_wu, ffn2_wd, l) for l in range(depth)]
    sh_ev = [ev_w_in[0].T.astype(BF16)[None], ev_w_out[0].astype(BF16)[None]]
    sh_od = [od_w_in[0].T.astype(BF16)[None], od_w_out[0].astype(BF16)[None]]

    def full(g):
        return g.reshape(g.shape[0], N_DEV * g.shape[2], g.shape[3])

    def gather_begin(shards, after, tag):
        lands = [lax.dynamic_update_slice(lax.empty((s.shape[0], N_DEV) + s.shape[1:], s.dtype), s[:, None],
                                          (0, dev, 0, 0)) for s in shards]
        state = _push_start(shards, lands, _gather_plan, after, "gather_start" + tag)
        return state, state[4][0, 0]

    def gather_arrived(state, after, tag):
        send, recv, srcs, lands, _ = state
        _, lands = _push_wait(send, recv, srcs, lands, _gather_plan, after, "gather_wait" + tag)
        state = _push_start([], lands, _forward_plan, None, "forward_start" + tag)
        return state, state[4][0, 0]

    def gather_done(state, after, tag):
        send, recv, _, lands, _ = state
        _, lands = _push_wait(send, recv, [], lands, _forward_plan, after, "forward_wait" + tag)
        return [full(g) for g in lands]

    def gather_end(state, after, tag):
        state, _ = gather_arrived(state, after, tag)
        return gather_done(state, state[4], tag)

    conv_w_sh, gn_g_sh = ev_conv_w[0], od_gn_g[0]
    cw, cs = conv_w_sh.shape[0], conv_w_sh.shape[1]
    gs = gn_g_sh.shape[0]
    conv_w_z = lax.dynamic_update_slice(jnp.zeros((cw, N_DEV * cs), F32), conv_w_sh, (0, dev * cs))
    gn_g_z = lax.dynamic_update_slice(jnp.zeros((N_DEV * gs,), F32), gn_g_sh, (dev * gs,))
    conv_w_full, gn_g_full = _unpack_rows(
        _all_reduce_small(_pack_rows([conv_w_z, gn_g_z]), "gather_small_params"),
        [conv_w_z.shape, gn_g_z.shape])

    w_ffn1, w_ffn2 = [None] * depth, [None] * depth
    pending, after = {}, conv_w_full
    for key, shards in (("0", [sh_ffn1[0]]), ("1", sh_ev), ("2", [sh_ffn2[0]]), ("3", [sh_ffn1[1]]), ("4", sh_od),
                        ("5", [sh_ffn2[1]])):
        pending[key], _ = gather_begin(shards, after, "_" + key)
        after = pending[key][4]
    start_tok = after[0, 0]
    (w_ffn1[0],) = gather_end(pending.pop("0"), after, "_0")

    def odd_mixer(u, l, tok):
        (la, lc, oml), lb_vjp = jax.vjp(functools.partial(_lb_terms, layer=l), od_lb_logits)
        gn = (gn_g_full + tok)[None]
        zb, o_raw, scores, states = _hgrn_fwd(u, la, lc, oml, gn, f"hgrn_fwd{l}")

        def backward(dz):
            dqz, dfz, diz, dgz, dla, dlc, doml, dgn = _hgrn_bwd(
                u, la, lc, oml, gn, o_raw, scores, states, dz, f"hgrn_bwd{l}")
            (g_lb,) = lb_vjp((dla, dlc, doml))
            return jnp.concatenate([dqz, dfz, diz, dgz], axis=1), [g_lb, dgn[0]]

        return zb, backward

    saved = []
    h = x[0]
    hn = _rms_fwd(h, (norm_ffn1[0] + start_tok)[None], "rms_a0")
    for l in range(depth):
        ffn, gu = _ffn_fwd(hn, w_ffn1[l], f"ffn_fwd_a{l}")
        s1 = (h, hn, gu)
        if l == 0:
            w_in, w_out = gather_end(pending.pop("1"), ffn, "_1")
        else:
            w_in, w_out = gather_done(pending.pop("4"), ffn, "_4")
        w_in_t, w_out = w_in[0], w_out[0]
        h, hn = _resid_rms(h, ffn, norm_mix[l][None], f"rms_mix{l}")
        u = _mm(hn, w_in_t, "nt", F32, f"mix_in{l}")
        key = "2" if l == 0 else "5"
        passing, tok = gather_arrived(pending.pop(key), u, "_" + key)
        if l % 2 == 0:
            zb, core_vjp = _even_mixer(u, conv_w_full, ev_conv_b + tok, ev_cn_g, ev_cn_b, ev_qn_g, ev_kn_g, str(l))
        else:
            zb, core_vjp = odd_mixer(u, l, tok)
        h_mix = h
        h = _mm(zb, w_out, "nn", F32, f"mix_out{l}", res=h)
        sm = (h_mix, hn, zb, core_vjp, w_in_t, w_out)
        (w_ffn2[l],) = gather_done(passing, h, "_" + key)
        tok = 0.0
        if l + 1 < depth:
            passing, tok = gather_arrived(pending.pop("3"), w_ffn2[l], "_3")
        hn = _rms_fwd(h, (norm_ffn2[l] + tok)[None], f"rms_b{l}")
        ffn, gu = _ffn_fwd(hn, w_ffn2[l], f"ffn_fwd_b{l}")
        saved.append((s1, sm, (h, hn, gu)))
        if l + 1 < depth:
            (w_ffn1[l + 1],) = gather_done(passing, ffn, "_3")
            pending["4"], tok = gather_arrived(pending.pop("4"), w_ffn1[l + 1], "_4")
            h, hn = _resid_rms(h, ffn, (norm_ffn1[l + 1] + tok)[None], f"rms_a{l + 1}")

    dy, loss_part = _loss_grad(h, ffn, loss_target[0], "loss_grad")

    def halves_begin(parts, tag):
        parts = [g.reshape(g.shape[0], 4, 2, g.shape[1] // N_DEV, g.shape[2]) for g in parts]
        lands = [lax.empty(g.shape[:2] + g.shape[3:], BF16) for g in parts]
        state = _push_start(parts, lands, _halves_plan, None, "halves_start" + tag)
        return state, state[4][0, 0]

    def chips_begin(state, after, tag):
        send, recv, srcs, lands, _ = state
        parts, got = _push_wait(send, recv, srcs, lands, _halves_plan, after, "halves_wait" + tag)
        sums = [_add_core_halves(g, r, c_idx, f"add_core_halves{tag}_{a}") for a, (g, r) in enumerate(zip(parts, got))]
        lands = [lax.empty((3, s.shape[0]) + s.shape[2:], BF16) for s in sums]
        state = _push_start(sums, lands, _chip_plan, None, "reduce_start" + tag)
        return state, state[4][0, 0]

    def reduce_end(state, after, tag):
        send, recv, srcs, lands, _ = state
        sums, got = _push_wait(send, recv, srcs, lands, _chip_plan, after, "reduce_wait" + tag)
        return [_sum_chip_blocks(s, r, k_idx, f"sum_chip_blocks{tag}_{a}") for a, (s, r) in enumerate(zip(sums, got))]

    def ffn_backward(dy, gain, w, sv, tag, on_dw, on_dx=None):
        h_in, hn, gu = sv
        dxn, dout, t = _ffn_bwd_dx(dy, w, gu, "ffn_bwd_dx_" + tag)
        tok = 0.0 if on_dx is None else on_dx(dxn)
        tok = tok + on_dw(_ffn_bwd_dw(hn, dout, t, "ffn_bwd_dw_" + tag))
        dx, dgain = _rms_bwd(h_in, (gain + tok)[None], dxn, dy, "rms_bwd_" + tag)
        return dx, dgain[0]

    g_norm1, g_norm2, g_normm = [None] * depth, [None] * depth, [None] * depth
    small, halves, groups = [None, None], {}, {}

    def start_halves(key, make_parts):
        def hook(dw):
            halves[key], tok = halves_begin(make_parts(dw), "_" + key)
            return tok
        return hook

    def start_chips(key):
        def hook(after):
            groups[key], tok = chips_begin(halves.pop(key), after, "_" + key)
            return tok
        return hook

    for l in reversed(range(depth)):
        s1, (h_mix, hn, zb, core_vjp, w_in_t, w_out), s2 = saved[l]
        if l == 1:
            dy, g_norm2[l] = ffn_backward(dy, norm_ffn2[l], w_ffn2[l], s2, f"b{l}", start_halves("1", lambda dw: [dw]))
        else:
            dy, g_norm2[l] = ffn_backward(dy, norm_ffn2[l], w_ffn2[l], s2, f"b{l}", start_halves("3", lambda dw: [dw]),
                                          start_chips("2"))
        dyb = dy.astype(BF16)
        dz = _mm(dyb, w_out, "nt", F32, f"mix_out_dz{l}")
        dw_out = _mm(zb, dyb, "tn", BF16, f"mix_out_dw{l}")
        dub, small[l % 2] = core_vjp(dz)
        dw_in_t = _mm(dub, hn, "tn", BF16, f"mix_in_dw{l}")
        mix_parts = [dw_in_t[None], dw_out[None]]
        if l == 1:
            tok = start_chips("1")(dw_in_t)
        else:
            tok = start_chips("3")(dw_in_t) + start_halves("4", lambda _: mix_parts)(None)
        dhn = _mm(dub, w_in_t, "nn", F32, f"mix_in_dx{l}")
        dy, gm = _rms_bwd(h_mix, (norm_mix[l] + tok)[None], dhn, dy, f"rms_bwd_mix{l}")
        g_normm[l] = gm[0]
        if l == 1:
            dy, g_norm1[l] = ffn_backward(dy, norm_ffn1[l], w_ffn1[l], s1, f"a{l}",
                                          start_halves("2", lambda dw, od=mix_parts: od + [dw]))
        else:
            dy, g_norm1[l] = ffn_backward(dy, norm_ffn1[l], w_ffn1[l], s1, f"a{l}", start_halves("5", lambda dw: [dw]),
                                          start_chips("4"))
    grad_x = dy[None]
    start_chips("5")(dy)

    done = [dy, groups["5"][4]]
    (g_ffn2_1,) = reduce_end(groups["1"], done, "_1")
    g_od_in_t, g_od_out, g_ffn1_1 = reduce_end(groups["2"], done, "_2")
    (g_ffn2_0,) = reduce_end(groups["3"], done, "_3")
    g_ev_in_t, g_ev_out = reduce_end(groups["4"], done, "_4")
    g_ffn2 = [g_ffn2_0, g_ffn2_1]

    def ffn_grads(gl):
        return (jnp.stack([g[0].T for g in gl]), jnp.stack([g[1].T for g in gl]), jnp.stack([g[2] for g in gl]))

    g_ffn2_wg, g_ffn2_wu, g_ffn2_wd = ffn_grads(g_ffn2)
    grads = [None, None, None, None, None, None, g_ffn2_wg, g_ffn2_wu, g_ffn2_wd,
             g_ev_in_t[0].T[None], None, None, None, None, None,
             None, g_ev_out, g_od_in_t[0].T[None], None, None, g_od_out]
    weights = [norm_ffn1, ffn1_wg, ffn1_wu, ffn1_wd, norm_mix, norm_ffn2, ffn2_wg, ffn2_wu, ffn2_wd, ev_w_in,
               ev_conv_w, ev_conv_b, ev_cn_g, ev_cn_b, ev_qn_g, ev_kn_g, ev_w_out, od_w_in, od_lb_logits,
               od_gn_g, od_w_out]
    moms = [m_norm_ffn1, m_ffn1_wg, m_ffn1_wu, m_ffn1_wd, m_norm_mix, m_norm_ffn2, m_ffn2_wg, m_ffn2_wu,
            m_ffn2_wd, m_ev_w_in, m_ev_conv_w, m_ev_conv_b, m_ev_cn_g, m_ev_cn_b, m_ev_qn_g, m_ev_kn_g,
            m_ev_w_out, m_od_w_in, m_od_lb_logits, m_od_gn_g, m_od_w_out]
    vars_ = [v_norm_ffn1, v_ffn1_wg, v_ffn1_wu, v_ffn1_wd, v_norm_mix, v_norm_ffn2, v_ffn2_wg, v_ffn2_wu,
             v_ffn2_wd, v_ev_w_in, v_ev_conv_w, v_ev_conv_b, v_ev_cn_g, v_ev_cn_b, v_ev_qn_g, v_ev_kn_g,
             v_ev_w_out, v_od_w_in, v_od_lb_logits, v_od_gn_g, v_od_w_out]
    n_w = len(weights)
    deltas, new_m, new_v = [None] * n_w, [None] * n_w, [None] * n_w

    def update(idx):
        for i in idx:
            deltas[i], new_m[i], new_v[i] = _adamw(weights[i], grads[i], moms[i], vars_[i], f"adamw{i}")

    update([i for i in range(n_w) if grads[i] is not None])
    g_conv_w, g_conv_b, g_cn_g, g_cn_b, g_qn_g, g_kn_g = small[0]
    g_lb, g_gn = small[1]
    parts = [jnp.stack(g_norm1), jnp.stack(g_normm), jnp.stack(g_norm2), g_conv_b, g_cn_g, g_cn_b,
             g_qn_g, g_kn_g, g_lb, g_conv_w, g_gn, loss_part[0, :1]]
    red = _unpack_rows(_all_reduce_small(_pack_rows(parts), "reduce_small_grads", [d for d in deltas if d is not None]),
                       [p.shape for p in parts])
    g_norm1, g_normm, g_norm2, g_conv_b, g_cn_g, g_cn_b, g_qn_g, g_kn_g, g_lb, g_conv_w, g_gn, loss = red
    g_conv_w = lax.dynamic_slice(g_conv_w, (0, dev * cs), (cw, cs))
    g_gn = lax.dynamic_slice(g_gn, (dev * gs,), (gs,))
    small_idx = {0: g_norm1, 4: g_normm, 5: g_norm2, 10: g_conv_w[None], 11: g_conv_b[None], 12: g_cn_g[None],
                 13: g_cn_b[None], 14: g_qn_g[None], 15: g_kn_g[None], 18: g_lb, 19: g_gn[None]}
    for i, g in small_idx.items():
        grads[i] = g
    update(small_idx)
    (g_ffn1_0,) = reduce_end(groups["5"], [d for d in deltas if d is not None], "_5")
    grads[1], grads[2], grads[3] = ffn_grads([g_ffn1_0, g_ffn1_1])
    update((1, 2, 3))
    return (loss[0], grad_x, *grads, *deltas, *new_m, *new_v)
```

```python
import functools
import math

import jax
import jax.numpy as jnp
from jax import lax
from jax.experimental import pallas as pl
from jax.experimental.pallas import tpu as pltpu

F32 = jnp.float32
BF16 = jnp.bfloat16
MESH = pl.DeviceIdType.MESH
N_DEV = 8

EPS = 1e-6
HEAD_DIM = 128
CONV_WIDTH = 31
DIL_PATTERNS = ((128, 1), (512, 4), (2048, 16))
Q_BLOCK = 128
ROPE_THETA = 10000.0
HGRN_KDIM = 128
HGRN_CHUNK = 256

ADAM_LR = 0.001
ADAM_B1 = 0.9
ADAM_B2 = 0.999
ADAM_EPS = 1e-08
ADAM_WD = 0.01
ADAM_STEP = 10

VMEM_LIMIT_BYTES = 56 * 1024 * 1024
LANE = 128
SUBLANE_BF16 = 16

ANY = pl.BlockSpec(memory_space=pl.ANY)


def _tile(n, pref, mult):
    t = (min(pref, n) // mult) * mult
    while t > 0:
        if n % t == 0:
            return t
        t -= mult
    return n


def _params(*sem):
    return pltpu.CompilerParams(dimension_semantics=sem, vmem_limit_bytes=VMEM_LIMIT_BYTES)


_DOT_DIMS = {
    "nn": (((1,), (0,)), ((), ())),
    "nt": (((1,), (1,)), ((), ())),
    "tn": (((0,), (0,)), ((), ())),
}


def _dot(a, b, mode):
    return lax.dot_general(a, b, _DOT_DIMS[mode], preferred_element_type=F32)


def _mm(a, b, mode, out_dtype, name, res=None, tm=1024, tn=1024, tk=2048):
    if mode == "nt":
        (M, K), N = a.shape, b.shape[0]
    elif mode == "nn":
        (M, K), N = a.shape, b.shape[1]
    else:
        (K, M), N = a.shape, b.shape[1]
    tm, tn, tk = _tile(M, tm, LANE), _tile(N, tn, LANE), _tile(K, tk, LANE)
    nk = K // tk

    def body(*refs):
        if res is None:
            a_ref, b_ref, o_ref, acc = refs
        else:
            a_ref, b_ref, r_ref, o_ref, acc = refs
        k = pl.program_id(2)

        @pl.when(k == 0)
        def _():
            acc[...] = jnp.zeros_like(acc)

        acc[...] += _dot(a_ref[...].astype(BF16), b_ref[...].astype(BF16), mode)

        @pl.when(k == nk - 1)
        def _():
            r = acc[...]
            if res is not None:
                r = r_ref[...] + r
            o_ref[...] = r.astype(out_dtype)

    a_spec = {"nt": pl.BlockSpec((tm, tk), lambda i, j, k: (i, k)),
              "nn": pl.BlockSpec((tm, tk), lambda i, j, k: (i, k)),
              "tn": pl.BlockSpec((tk, tm), lambda i, j, k: (k, i))}[mode]
    b_spec = {"nt": pl.BlockSpec((tn, tk), lambda i, j, k: (j, k)),
              "nn": pl.BlockSpec((tk, tn), lambda i, j, k: (k, j)),
              "tn": pl.BlockSpec((tk, tn), lambda i, j, k: (k, j))}[mode]
    o_spec = pl.BlockSpec((tm, tn), lambda i, j, k: (i, j))
    in_specs = [a_spec, b_spec] + ([o_spec] if res is not None else [])
    args = (a, b) + ((res,) if res is not None else ())
    return pl.pallas_call(
        body, name=name, grid=(M // tm, N // tn, nk),
        in_specs=in_specs, out_specs=o_spec,
        out_shape=jax.ShapeDtypeStruct((M, N), out_dtype),
        scratch_shapes=[pltpu.VMEM((tm, tn), F32)],
        compiler_params=_params("parallel", "parallel", "arbitrary"),
    )(*args)


def _rms_fwd(x, gain, name):
    S, D = x.shape
    tm = _tile(S, 512, SUBLANE_BF16)

    def body(x_ref, g_ref, o_ref):
        xv = x_ref[...]
        r = lax.rsqrt(jnp.mean(xv * xv, axis=-1, keepdims=True) + EPS)
        o_ref[...] = (xv * r * g_ref[...]).astype(BF16)

    return pl.pallas_call(
        body, name=name, grid=(S // tm,),
        in_specs=[pl.BlockSpec((tm, D), lambda i: (i, 0)), pl.BlockSpec((1, D), lambda i: (0, 0))],
        out_specs=pl.BlockSpec((tm, D), lambda i: (i, 0)),
        out_shape=jax.ShapeDtypeStruct((S, D), BF16),
        compiler_params=_params("parallel"),
    )(x, gain)


def _resid_rms(x, ffn, gain, name):
    S, D = x.shape
    tm = _tile(S, 512, SUBLANE_BF16)

    def body(x_ref, f_ref, g_ref, h_ref, o_ref):
        hv = x_ref[...] + 0.5 * f_ref[...]
        h_ref[...] = hv
        r = lax.rsqrt(jnp.mean(hv * hv, axis=-1, keepdims=True) + EPS)
        o_ref[...] = (hv * r * g_ref[...]).astype(BF16)

    row = pl.BlockSpec((tm, D), lambda i: (i, 0))
    return pl.pallas_call(
        body, name=name, grid=(S // tm,),
        in_specs=[row, row, pl.BlockSpec((1, D), lambda i: (0, 0))], out_specs=[row, row],
        out_shape=[jax.ShapeDtypeStruct((S, D), F32), jax.ShapeDtypeStruct((S, D), BF16)],
        compiler_params=_params("parallel"),
    )(x, ffn, gain)


def _rms_bwd(x, gain, dxn, dy, name):
    S, D = x.shape
    tm = _tile(S, 512, 8)

    def body(x_ref, g_ref, dxn_ref, dy_ref, dx_ref, dg_ref):
        @pl.when(pl.program_id(0) == 0)
        def _():
            dg_ref[...] = jnp.zeros_like(dg_ref)

        xv = x_ref[...]
        r = lax.rsqrt(jnp.mean(xv * xv, axis=-1, keepdims=True) + EPS)
        xh = xv * r
        dxn_v = dxn_ref[...]
        dg_ref[...] += jnp.sum(dxn_v * xh, axis=0, keepdims=True)
        dxh = dxn_v * g_ref[...]
        dx_ref[...] = dy_ref[...] + r * (dxh - xh * jnp.mean(dxh * xh, axis=-1, keepdims=True))

    row = pl.BlockSpec((tm, D), lambda i: (i, 0))
    vec = pl.BlockSpec((1, D), lambda i: (0, 0))
    return pl.pallas_call(
        body, name=name, grid=(S // tm,),
        in_specs=[row, vec, row, row], out_specs=[row, vec],
        out_shape=[jax.ShapeDtypeStruct((S, D), F32), jax.ShapeDtypeStruct((1, D), F32)],
        compiler_params=_params("arbitrary"),
    )(x, gain, dxn, dy)


def _ffn_fwd(xn, w, name):
    S, D = xn.shape
    F = w.shape[1]
    tm, tf = _tile(S, 1024, SUBLANE_BF16), _tile(F, 512, LANE)
    nf = F // tf

    def body(xn_ref, w_ref, o_ref, gu_ref):
        @pl.when(pl.program_id(1) == 0)
        def _():
            o_ref[...] = jnp.zeros_like(o_ref)

        xnv = xn_ref[...]
        g = _dot(xnv, w_ref[0], "nt")
        u = _dot(xnv, w_ref[1], "nt")
        gu_ref[0] = g.astype(BF16)
        gu_ref[1] = u.astype(BF16)
        h = (g * jax.nn.sigmoid(g) * u).astype(BF16)
        o_ref[...] += _dot(h, w_ref[2], "nn")

    row = pl.BlockSpec((tm, D), lambda i, f: (i, 0))
    return pl.pallas_call(
        body, name=name, grid=(S // tm, nf),
        in_specs=[row, pl.BlockSpec((3, tf, D), lambda i, f: (0, f, 0))],
        out_specs=[row, pl.BlockSpec((2, tm, tf), lambda i, f: (0, i, f))],
        out_shape=[jax.ShapeDtypeStruct((S, D), F32), jax.ShapeDtypeStruct((2, S, F), BF16)],
        compiler_params=_params("parallel", "arbitrary"),
    )(xn, w)


def _ffn_bwd_dx(dy, w, gu, name):
    S, D = dy.shape
    F = w.shape[1]
    tm, tf = _tile(S, 1024, SUBLANE_BF16), _tile(F, 512, LANE)
    nf = F // tf

    def body(dy_ref, w_ref, gu_ref, dxn_ref, dout_ref, t_ref):
        @pl.when(pl.program_id(1) == 0)
        def _():
            dxn_ref[...] = jnp.zeros_like(dxn_ref)
            dout_ref[...] = (0.5 * dy_ref[...]).astype(BF16)

        g = gu_ref[0].astype(F32)
        u = gu_ref[1].astype(F32)
        sig = jax.nn.sigmoid(g)
        silu = g * sig
        t_ref[2] = (silu * u).astype(BF16)
        dh = _dot(dout_ref[...], w_ref[2], "nt")
        dg = (dh * (u * (sig * (1.0 + g * (1.0 - sig))))).astype(BF16)
        du = (dh * silu).astype(BF16)
        t_ref[0] = dg
        t_ref[1] = du
        dxn_ref[...] += _dot(dg, w_ref[0], "nn") + _dot(du, w_ref[1], "nn")

    row = pl.BlockSpec((tm, D), lambda i, f: (i, 0), pipeline_mode=pl.Buffered(1))
    return pl.pallas_call(
        body, name=name, grid=(S // tm, nf),
        in_specs=[row, pl.BlockSpec((3, tf, D), lambda i, f: (0, f, 0)),
                  pl.BlockSpec((2, tm, tf), lambda i, f: (0, i, f))],
        out_specs=[row, row, pl.BlockSpec((3, tm, tf), lambda i, f: (0, i, f))],
        out_shape=[jax.ShapeDtypeStruct((S, D), F32), jax.ShapeDtypeStruct((S, D), BF16),
                   jax.ShapeDtypeStruct((3, S, F), BF16)],
        compiler_params=_params("parallel", "arbitrary"),
    )(dy, w, gu)


def _ffn_bwd_dw(xn, dout, t, name):
    S, D = xn.shape
    F = t.shape[2]
    ts, tf = _tile(S, 1024, LANE), _tile(F, 512, LANE)
    ns = S // ts

    def body(xn_ref, dout_ref, t_ref, dw_ref, acc):
        s = pl.program_id(1)

        @pl.when(s == 0)
        def _():
            acc[...] = jnp.zeros_like(acc)

        xnv = xn_ref[...]
        acc[0] += _dot(t_ref[0], xnv, "tn")
        acc[1] += _dot(t_ref[1], xnv, "tn")
        acc[2] += _dot(t_ref[2], dout_ref[...], "tn")

        @pl.when(s == ns - 1)
        def _():
            dw_ref[...] = acc[...].astype(BF16)

    row = pl.BlockSpec((ts, D), lambda f, s: (s, 0))
    return pl.pallas_call(
        body, name=name, grid=(F // tf, ns),
        in_specs=[row, row, pl.BlockSpec((3, ts, tf), lambda f, s: (0, s, f))],
        out_specs=pl.BlockSpec((3, tf, D), lambda f, s: (0, f, 0)),
        out_shape=jax.ShapeDtypeStruct((3, F, D), BF16),
        scratch_shapes=[pltpu.VMEM((3, tf, D), F32)],
        compiler_params=_params("parallel", "arbitrary"),
    )(xn, dout, t)


def _loss_grad(x, ffn, target, name):
    S, D = x.shape
    tm = _tile(S, 512, 8)

    def body(x_ref, f_ref, t_ref, dy_ref, l_ref):
        @pl.when(pl.program_id(0) == 0)
        def _():
            l_ref[...] = jnp.zeros_like(l_ref)

        e = (x_ref[...] + 0.5 * f_ref[...]) - t_ref[...]
        dy_ref[...] = e * (1.0 / D)
        l_ref[...] += 0.5 * jnp.sum(jnp.sum(e * e, axis=-1, keepdims=True) * (1.0 / D))

    row = pl.BlockSpec((tm, D), lambda i: (i, 0))
    one = pl.BlockSpec((8, LANE), lambda i: (0, 0))
    return pl.pallas_call(
        body, name=name, grid=(S // tm,),
        in_specs=[row, row, row], out_specs=[row, one],
        out_shape=[jax.ShapeDtypeStruct((S, D), F32), jax.ShapeDtypeStruct((8, LANE), F32)],
        compiler_params=_params("arbitrary"),
    )(x, ffn, target)


def _adamw(w, g, m, v, name):
    shape = w.shape
    C = shape[-1]
    R = math.prod(shape[:-1])
    tr = _tile(R, max(8, (1 << 19) // C // 8 * 8), 8)
    c1 = 1.0 / (1.0 - ADAM_B1 ** ADAM_STEP)
    c2 = 1.0 / (1.0 - ADAM_B2 ** ADAM_STEP)

    def body(w_ref, g_ref, m_ref, v_ref, d_ref, nm_ref, nv_ref):
        gv = g_ref[...]
        nm = ADAM_B1 * m_ref[...] + (1.0 - ADAM_B1) * gv
        nv = ADAM_B2 * v_ref[...] + (1.0 - ADAM_B2) * (gv * gv)
        nm_ref[...] = nm
        nv_ref[...] = nv
        d_ref[...] = -ADAM_LR * ((nm * c1) / (jnp.sqrt(nv * c2) + ADAM_EPS) + ADAM_WD * w_ref[...])

    blk = pl.BlockSpec((tr, C), lambda i: (i, 0))
    sds = jax.ShapeDtypeStruct((R, C), F32)
    outs = pl.pallas_call(
        body, name=name, grid=(R // tr,),
        in_specs=[blk] * 4, out_specs=[blk] * 3, out_shape=[sds] * 3,
        compiler_params=_params("parallel"),
    )(*(a.reshape(R, C) for a in (w, g, m, v)))
    return tuple(o.reshape(shape) for o in outs)


def _me():
    return lax.axis_index("x"), lax.axis_index("y"), lax.axis_index("c")


def _add_core_halves(grad, got, c_idx, name):
    n, nk, _, r, C = grad.shape
    tr = _tile(r, 1024, SUBLANE_BF16)

    def body(c_ref, g_ref, r_ref, o_ref):
        o_ref[...] = (g_ref[...].astype(F32) + r_ref[...].astype(F32)).astype(BF16)

    return pl.pallas_call(
        body, name=name,
        grid_spec=pltpu.PrefetchScalarGridSpec(
            num_scalar_prefetch=1, grid=(n, nk, r // tr),
            in_specs=[pl.BlockSpec((None, None, None, tr, C), lambda i, k, t, c: (i, k, c[0], t, 0)),
                      pl.BlockSpec((None, None, tr, C), lambda i, k, t, c: (i, k, t, 0))],
            out_specs=pl.BlockSpec((None, None, tr, C), lambda i, k, t, c: (i, k, t, 0))),
        out_shape=jax.ShapeDtypeStruct((n, nk, r, C), BF16),
        compiler_params=_params("parallel", "parallel", "parallel"),
    )(c_idx, grad, got)


HBM = pl.BlockSpec(memory_space=pltpu.HBM)
SEM = pl.BlockSpec(memory_space=pltpu.SEMAPHORE)
EFFECT = pltpu.SideEffectType.DATAFLOW_SIDE_EFFECTING


def _push_start(srcs, lands, plan, after, name):
    ns, nl = len(srcs), len(lands)
    ncp = len(plan([None] * ns, [None] * nl, dry=True))
    extra = [] if after is None else [after]

    def body(*refs):
        src_refs, land_refs = refs[:ns], refs[ns:ns + nl]
        send_sems, recv_sems = refs[ns + nl + len(extra)], refs[ns + nl + len(extra) + 1]
        token = refs[-1]
        for i, (s, d, to) in enumerate(plan(src_refs, land_refs)):
            pltpu.make_async_remote_copy(src_ref=s, dst_ref=d, send_sem=send_sems.at[i], recv_sem=recv_sems.at[i],
                                         device_id=to, device_id_type=MESH).start()
        token[...] = jnp.zeros_like(token)

    out = pl.pallas_call(
        body, name=name,
        out_shape=(pltpu.SemaphoreType.DMA((ncp,)), pltpu.SemaphoreType.DMA((ncp,)),
                   *[pltpu.HBM(a.shape, a.dtype) for a in srcs], *[pltpu.HBM(a.shape, a.dtype) for a in lands],
                   jax.ShapeDtypeStruct((8, LANE), F32)),
        in_specs=[HBM] * (ns + nl) + [ANY] * len(extra),
        out_specs=(SEM, SEM, *[HBM] * (ns + nl), pl.BlockSpec(memory_space=pltpu.VMEM)),
        input_output_aliases={i: 2 + i for i in range(ns + nl)},
        compiler_params=pltpu.CompilerParams(has_side_effects=EFFECT),
    )(*[pltpu.with_memory_space_constraint(a, pltpu.HBM) for a in srcs + lands], *extra)
    return out[0], out[1], list(out[2:2 + ns]), list(out[2 + ns:2 + ns + nl]), out[-1]


def _push_wait(send_sems, recv_sems, srcs, lands, plan, after, name):
    ns, nl = len(srcs), len(lands)
    after = list(after) if isinstance(after, (list, tuple)) else [after]

    def body(*refs):
        src_refs, land_refs = refs[:ns], refs[ns:ns + nl]
        send, recv = refs[ns + nl], refs[ns + nl + 1]
        for i, (s, d, to) in enumerate(plan(src_refs, land_refs)):
            cp = pltpu.make_async_remote_copy(src_ref=s, dst_ref=d, send_sem=send.at[i], recv_sem=recv.at[i],
                                              device_id=to, device_id_type=MESH)
            cp.wait_send()
            cp.wait_recv()

    out = pl.pallas_call(
        body, name=name,
        out_shape=tuple(pltpu.HBM(a.shape, a.dtype) for a in srcs + lands),
        in_specs=[HBM] * (ns + nl) + [SEM, SEM] + [ANY] * len(after),
        out_specs=tuple([HBM] * (ns + nl)),
        input_output_aliases={i: i for i in range(ns + nl)},
        compiler_params=pltpu.CompilerParams(has_side_effects=EFFECT),
    )(*srcs, *lands, send_sems, recv_sems, *after)
    return list(out[:ns]), list(out[ns:])


def _gather_plan(src_refs, land_refs, dry=False):
    if dry:
        return [None] * (4 * len(src_refs))
    x, y, c = _me()
    me = 4 * x + 2 * y + c
    targets = [(x, y, 1 - c), (1 - x, y, c), (x, 1 - y, c), (1 - x, 1 - y, c)]
    return [(s, l.at[:, me], to) for s, l in zip(src_refs, land_refs) for to in targets]


def _halves_plan(src_refs, land_refs, dry=False):
    if dry:
        return [None] * len(src_refs)
    x, y, c = _me()
    return [(s.at[:, :, 1 - c], l, (x, y, 1 - c)) for s, l in zip(src_refs, land_refs)]


def _chip_plan(src_refs, land_refs, dry=False):
    if dry:
        return [None] * (3 * len(src_refs))
    x, y, c = _me()
    chips = [(1 - x, y), (x, 1 - y), (1 - x, 1 - y)]
    return [(s.at[:, 2 * chip[0] + chip[1]], l.at[j], (*chip, c))
            for s, l in zip(src_refs, land_refs) for j, chip in enumerate(chips)]


def _forward_plan(src_refs, land_refs, dry=False):
    if dry:
        return [None] * (3 * len(land_refs))
    x, y, c = _me()
    chips = [(1 - x, y), (x, 1 - y), (1 - x, 1 - y)]
    plan = []
    for l in land_refs:
        for chip in chips:
            blk = l.at[:, 4 * chip[0] + 2 * chip[1] + c]
            plan.append((blk, blk, (x, y, 1 - c)))
    return plan


def _sum_chip_blocks(sums, got, k_idx, name):
    n, _, r, C = sums.shape
    tr = _tile(r, 512, SUBLANE_BF16)

    def body(k_ref, s_ref, r_ref, o_ref):
        acc = s_ref[...].astype(F32)
        for j in range(3):
            acc = acc + r_ref[j].astype(F32)
        o_ref[...] = acc

    return pl.pallas_call(
        body, name=name,
        grid_spec=pltpu.PrefetchScalarGridSpec(
            num_scalar_prefetch=1, grid=(n, r // tr),
            in_specs=[pl.BlockSpec((None, None, tr, C), lambda i, t, k: (i, k[0], t, 0)),
                      pl.BlockSpec((3, None, tr, C), lambda i, t, k: (0, i, t, 0))],
            out_specs=pl.BlockSpec((None, tr, C), lambda i, t, k: (i, t, 0))),
        out_shape=jax.ShapeDtypeStruct((n, r, C), F32),
        compiler_params=_params("parallel", "parallel"),
    )(k_idx, sums, got)


def _all_reduce_small(v, name, after=()):
    R = v.shape[0]
    after = list(after)

    def body(*refs):
        v_ref = refs[0]
        o_ref, buf, send_sems, recv_sems = refs[1 + len(after):]
        x, y, c = _me()
        me = 4 * x + 2 * y + c
        buf[me] = v_ref[...]
        copies = []
        for k in range(1, N_DEV):
            peer = (x ^ (k >> 2), y ^ ((k >> 1) & 1), c ^ (k & 1))
            copies.append(pltpu.make_async_remote_copy(
                src_ref=v_ref, dst_ref=buf.at[me],
                send_sem=send_sems.at[k - 1], recv_sem=recv_sems.at[k - 1],
                device_id=peer, device_id_type=MESH))
        for cp in copies:
            cp.start()
        for cp in copies:
            cp.wait()
        acc = buf[0]
        for d in range(1, N_DEV):
            acc = acc + buf[d]
        o_ref[...] = acc

    vm = pl.BlockSpec(memory_space=pltpu.VMEM)
    return pl.pallas_call(
        body, name=name, in_specs=[vm] + [ANY] * len(after), out_specs=vm,
        out_shape=jax.ShapeDtypeStruct((R, LANE), F32),
        scratch_shapes=[pltpu.VMEM((N_DEV, R, LANE), F32),
                        pltpu.SemaphoreType.DMA((N_DEV - 1,)), pltpu.SemaphoreType.DMA((N_DEV - 1,))],
        compiler_params=pltpu.CompilerParams(vmem_limit_bytes=VMEM_LIMIT_BYTES),
    )(v, *after)


def _pack_rows(parts):
    flat = jnp.concatenate([p.reshape(-1).astype(F32) for p in parts])
    n = flat.shape[0]
    rows = -(-n // (8 * LANE)) * 8
    flat = jnp.pad(flat, (0, rows * LANE - n))
    return flat.reshape(rows, LANE)


def _unpack_rows(packed, shapes):
    flat = packed.reshape(-1)
    out, off = [], 0
    for s in shapes:
        n = math.prod(s)
        out.append(flat[off:off + n].reshape(s))
        off += n
    return out


CONV_HALO = 32


def _conv_fwd(u, conv_w, conv_b, cn_g, cn_b, name):
    S = u.shape[0]
    C = conv_w.shape[1]
    T = _tile(S, 256, CONV_HALO)
    hb = T // CONV_HALO

    def body(av_ref, ag_ref, pv_ref, pg_ref, w_ref, b_ref, g_ref, bb_ref, out_ref, y_ref, scr):
        i = pl.program_id(0)
        prev = pv_ref[...] * jax.nn.sigmoid(pg_ref[...])
        scr[0:CONV_HALO, :] = jnp.where(i > 0, prev, 0.0)
        scr[CONV_HALO:CONV_HALO + T, :] = av_ref[...] * jax.nn.sigmoid(ag_ref[...])
        for s in range(C // LANE):
            sl = slice(s * LANE, (s + 1) * LANE)
            acc = jnp.broadcast_to(b_ref[:, sl], (T, LANE))
            for j in range(CONV_WIDTH):
                acc = acc + w_ref[j:j + 1, sl] * scr[pl.ds(CONV_HALO - (CONV_WIDTH - 1) + j, T), sl]
            y_ref[:, sl] = acc
        acc = y_ref[...]
        mu = jnp.mean(acc, axis=-1, keepdims=True)
        xc = acc - mu
        var = jnp.mean(xc * xc, axis=-1, keepdims=True)
        ln = xc * lax.rsqrt(var + EPS) * g_ref[...] + bb_ref[...]
        out_ref[...] = (ln * jax.nn.sigmoid(ln)).astype(BF16)

    def cur(cb):
        return pl.BlockSpec((T, C), lambda i: (i, cb))

    def halo(cb):
        return pl.BlockSpec((CONV_HALO, C), lambda i: (jnp.maximum(i * hb - 1, 0), cb))

    vec = pl.BlockSpec((1, C), lambda i: (0, 0))
    return pl.pallas_call(
        body, name=name, grid=(S // T,),
        in_specs=[cur(0), cur(1), halo(0), halo(1), pl.BlockSpec((CONV_WIDTH, C), lambda i: (0, 0)), vec, vec, vec],
        out_specs=[pl.BlockSpec((T, C), lambda i: (i, 0))] * 2,
        out_shape=[jax.ShapeDtypeStruct((S, C), BF16), jax.ShapeDtypeStruct((S, C), F32)],
        scratch_shapes=[pltpu.VMEM((T + CONV_HALO, C), F32)],
        compiler_params=_params("parallel"),
    )(u, u, u, u, conv_w, conv_b, cn_g, cn_b)


def _conv_bwd_norm(dz, y, cn_g, cn_b, name):
    S, C = y.shape
    T = _tile(S, 256, 8)

    def body(dz_ref, y_ref, g_ref, bb_ref, dy_ref, dg_ref, db_ref):
        @pl.when(pl.program_id(0) == 0)
        def _():
            dg_ref[...] = jnp.zeros_like(dg_ref)
            db_ref[...] = jnp.zeros_like(db_ref)

        yv = y_ref[...]
        mu = jnp.mean(yv, axis=-1, keepdims=True)
        xc = yv - mu
        rstd = lax.rsqrt(jnp.mean(xc * xc, axis=-1, keepdims=True) + EPS)
        xh = xc * rstd
        ln = xh * g_ref[...] + bb_ref[...]
        sg = jax.nn.sigmoid(ln)
        dln = dz_ref[...] * (sg * (1.0 + ln * (1.0 - sg)))
        dg_ref[...] += jnp.sum(dln * xh, axis=0, keepdims=True)
        db_ref[...] += jnp.sum(dln, axis=0, keepdims=True)
        dxh = dln * g_ref[...]
        dy_ref[...] = rstd * (dxh - jnp.mean(dxh, axis=-1, keepdims=True)
                              - xh * jnp.mean(dxh * xh, axis=-1, keepdims=True))

    row = pl.BlockSpec((T, C), lambda i: (i, 0))
    vec = pl.BlockSpec((1, C), lambda i: (0, 0))
    return pl.pallas_call(
        body, name=name, grid=(S // T,),
        in_specs=[row, row, vec, vec], out_specs=[row, vec, vec],
        out_shape=[jax.ShapeDtypeStruct((S, C), F32), jax.ShapeDtypeStruct((1, C), F32),
                   jax.ShapeDtypeStruct((1, C), F32)],
        compiler_params=_params("arbitrary"),
    )(dz, y, cn_g, cn_b)


def _conv_bwd_taps(u, dy, conv_w, name):
    S, C = dy.shape
    T = _tile(S, 256, CONV_HALO)
    hb = T // CONV_HALO
    nt = S // T
    ns = C // LANE
    W1 = CONV_WIDTH - 1

    def body(av_ref, ag_ref, pv_ref, pg_ref, dy_ref, dn_ref, w_ref, dv_ref, dg_ref, dw_ref, db_ref, a_scr, d_scr):
        i = pl.program_id(1)

        @pl.when(i == 0)
        def _():
            dw_ref[...] = jnp.zeros_like(dw_ref)
            db_ref[...] = jnp.zeros_like(db_ref)

        av, sg = av_ref[...], jax.nn.sigmoid(ag_ref[...])
        prev = pv_ref[...] * jax.nn.sigmoid(pg_ref[...])
        a_scr[0:CONV_HALO, :] = jnp.where(i > 0, prev, 0.0)
        a_scr[CONV_HALO:CONV_HALO + T, :] = av * sg
        dyv = dy_ref[...]
        d_scr[0:T, :] = dyv
        d_scr[T:T + CONV_HALO, :] = jnp.where(i < nt - 1, dn_ref[...], 0.0)
        da = jnp.zeros((T, LANE), F32)
        for j in range(CONV_WIDTH):
            da = da + w_ref[j:j + 1, :] * d_scr[pl.ds(W1 - j, T), :]
            dw_ref[j:j + 1, :] += jnp.sum(dyv * a_scr[pl.ds(CONV_HALO - W1 + j, T), :], axis=0, keepdims=True)
        db_ref[...] += jnp.sum(dyv, axis=0, keepdims=True)
        dv_ref[...] = (da * sg).astype(BF16)
        dg_ref[...] = (da * av * sg * (1.0 - sg)).astype(BF16)

    def cur(part):
        return pl.BlockSpec((T, LANE), lambda cb, i: (i, part * ns + cb))

    def halo(part):
        return pl.BlockSpec((CONV_HALO, LANE), lambda cb, i: (jnp.maximum(i * hb - 1, 0), part * ns + cb))

    nxt = pl.BlockSpec((CONV_HALO, LANE), lambda cb, i: (jnp.minimum((i + 1) * hb, S // CONV_HALO - 1), cb))
    row = pl.BlockSpec((T, LANE), lambda cb, i: (i, cb))
    return pl.pallas_call(
        body, name=name, grid=(ns, nt),
        in_specs=[cur(0), cur(1), halo(0), halo(1), row, nxt, pl.BlockSpec((CONV_WIDTH, LANE), lambda cb, i: (0, cb))],
        out_specs=[row, row, pl.BlockSpec((CONV_HALO, LANE), lambda cb, i: (0, cb)),
                   pl.BlockSpec((1, LANE), lambda cb, i: (0, cb))],
        out_shape=[jax.ShapeDtypeStruct((S, C), BF16), jax.ShapeDtypeStruct((S, C), BF16),
                   jax.ShapeDtypeStruct((CONV_HALO, C), F32), jax.ShapeDtypeStruct((1, C), F32)],
        scratch_shapes=[pltpu.VMEM((T + CONV_HALO, LANE), F32), pltpu.VMEM((T + CONV_HALO, LANE), F32)],
        compiler_params=_params("parallel", "arbitrary"),
    )(u, u, u, u, dy, dy, conv_w)


def _rope_tables(S):
    half = HEAD_DIM // 2
    inv = jnp.exp(-math.log(ROPE_THETA) * jnp.arange(half, dtype=F32) / half)
    ang = jnp.arange(S, dtype=jnp.int32).astype(F32)[:, None] * inv[None, :]
    cos, sin = jnp.cos(ang), jnp.sin(ang)
    return jnp.concatenate([cos, cos], axis=1), jnp.concatenate([-sin, sin], axis=1)


def _qkv_prep(u, qn_g, kn_g, cos, sin, cb0, name):
    S = u.shape[0]
    A = (u.shape[1] // (cb0 + 3))
    H = A // HEAD_DIM
    T = _tile(S, 256, SUBLANE_BF16)
    scale = HEAD_DIM ** -0.5

    def body(q_ref, k_ref, v_ref, qg_ref, kg_ref, cos_ref, sin_ref, qo_ref, ko_ref, vo_ref):
        cosv, sinv = cos_ref[...], sin_ref[...]
        for h in range(H):
            sl = slice(h * HEAD_DIM, (h + 1) * HEAD_DIM)
            for x_ref, g_ref, o_ref, sc in ((q_ref, qg_ref, qo_ref, scale), (k_ref, kg_ref, ko_ref, 1.0)):
                xv = x_ref[:, sl]
                xn = xv * lax.rsqrt(jnp.mean(xv * xv, axis=-1, keepdims=True) + EPS) * g_ref[...]
                y = xn * cosv + pltpu.roll(xn, HEAD_DIM // 2, 1) * sinv
                o_ref[:, sl] = (y * sc).astype(BF16)
        vo_ref[...] = v_ref[...].astype(BF16)

    def col(cb):
        return pl.BlockSpec((T, A), lambda i: (i, cb))

    vec = pl.BlockSpec((1, HEAD_DIM), lambda i: (0, 0))
    tab = pl.BlockSpec((T, HEAD_DIM), lambda i: (i, 0))
    out = pl.BlockSpec((T, A), lambda i: (i, 0))
    return pl.pallas_call(
        body, name=name, grid=(S // T,),
        in_specs=[col(cb0), col(cb0 + 1), col(cb0 + 2), vec, vec, tab, tab],
        out_specs=[out] * 3, out_shape=[jax.ShapeDtypeStruct((S, A), BF16)] * 3,
        compiler_params=_params("parallel"),
    )(u, u, u, qn_g, kn_g, cos, sin)


def _qkv_prep_bwd(u, dqs, dks, dvs, qn_g, kn_g, cos, sin, cb0, name):
    S = u.shape[0]
    A = dqs[0].shape[1]
    H = A // HEAD_DIM
    T = _tile(S, 256, SUBLANE_BF16)
    nb = len(dqs)
    scale = HEAD_DIM ** -0.5

    def body(*refs):
        q_ref, k_ref, qg_ref, kg_ref, cos_ref, sin_ref = refs[:6]
        dq_refs, dk_refs, dv_refs = refs[6:6 + nb], refs[6 + nb:6 + 2 * nb], refs[6 + 2 * nb:6 + 3 * nb]
        dqo_ref, dko_ref, dvo_ref, dqg_ref, dkg_ref = refs[6 + 3 * nb:]

        @pl.when(pl.program_id(0) == 0)
        def _():
            dqg_ref[...] = jnp.zeros_like(dqg_ref)
            dkg_ref[...] = jnp.zeros_like(dkg_ref)

        cosv, sinv = cos_ref[...], sin_ref[...]
        for h in range(H):
            sl = slice(h * HEAD_DIM, (h + 1) * HEAD_DIM)
            for x_ref, g_ref, d_refs, o_ref, dg_ref, sc in ((q_ref, qg_ref, dq_refs, dqo_ref, dqg_ref, scale),
                                                          (k_ref, kg_ref, dk_refs, dko_ref, dkg_ref, 1.0)):
                dy = d_refs[0][:, sl]
                for r in d_refs[1:]:
                    dy = dy + r[:, sl]
                dy = dy * sc
                dxn = dy * cosv + pltpu.roll(dy * sinv, HEAD_DIM // 2, 1)
                xv = x_ref[:, sl]
                r = lax.rsqrt(jnp.mean(xv * xv, axis=-1, keepdims=True) + EPS)
                xh = xv * r
                dg_ref[...] += jnp.sum(dxn * xh, axis=0, keepdims=True)
                dxh = dxn * g_ref[...]
                o_ref[:, sl] = (r * (dxh - xh * jnp.mean(dxh * xh, axis=-1, keepdims=True))).astype(BF16)
        dv = dv_refs[0][...]
        for r in dv_refs[1:]:
            dv = dv + r[...]
        dvo_ref[...] = dv.astype(BF16)

    def col(cb):
        return pl.BlockSpec((T, A), lambda i: (i, cb))

    vec = pl.BlockSpec((1, HEAD_DIM), lambda i: (0, 0))
    tab = pl.BlockSpec((T, HEAD_DIM), lambda i: (i, 0))
    row = pl.BlockSpec((T, A), lambda i: (i, 0))
    return pl.pallas_call(
        body, name=name, grid=(S // T,),
        in_specs=[col(cb0), col(cb0 + 1), vec, vec, tab, tab] + [row] * (3 * nb),
        out_specs=[row, row, row, vec, vec],
        out_shape=[jax.ShapeDtypeStruct((S, A), BF16)] * 3 + [jax.ShapeDtypeStruct((1, HEAD_DIM), F32)] * 2,
        compiler_params=_params("arbitrary"),
    )(u, u, qn_g, kn_g, cos, sin, *dqs, *dks, *dvs)


ATT_TILE_FWD = 256
ATT_TILE_BWD = 512
NEG = -1e30


def _attn_bias(tile):
    span = max(window for window, _ in DIL_PATTERNS)
    nw = -(-span // tile) + 1
    dist = (jnp.arange(nw)[:, None, None] * tile + jnp.arange(tile)[None, :, None] - jnp.arange(tile)[None, None, :])
    mult = sum(((dist >= 0) & (dist <= window) & (dist % dil == 0)).astype(F32) for window, dil in DIL_PATTERNS)
    return jnp.where(mult > 0, jnp.log(jnp.maximum(mult, 1.0)), NEG)


def _attn_fwd(q, k, v, bias, name):
    S, A = q.shape
    H = A // HEAD_DIM
    nw, T, _ = bias.shape
    nq = S // T

    def body(q_ref, k_ref, v_ref, b_ref, ob_ref, of_ref, l_ref, s_scr):
        i = pl.program_id(1)
        qv = q_ref[...]
        mx = jnp.full((T, 1), NEG, F32)
        for w in range(nw):
            blk = i - w
            start = pl.multiple_of(jnp.maximum(blk, 0) * T, T)
            s = _dot(qv, k_ref[pl.ds(start, T), :], "nt") + b_ref[w] + jnp.where(blk >= 0, 0.0, NEG)
            s_scr[w] = s
            mx = jnp.maximum(mx, jnp.max(s, axis=-1, keepdims=True))
        den = jnp.zeros((T, 1), F32)
        o = jnp.zeros((T, HEAD_DIM), F32)
        for w in range(nw):
            start = pl.multiple_of(jnp.maximum(i - w, 0) * T, T)
            p = jnp.exp(s_scr[w] - mx)
            den = den + jnp.sum(p, axis=-1, keepdims=True)
            o = o + _dot(p.astype(BF16), v_ref[pl.ds(start, T), :], "nn")
        o = o / den
        ob_ref[...] = o.astype(BF16)
        of_ref[...] = o
        l_ref[...] = mx + jnp.log(den)

    blk = pl.BlockSpec((T, HEAD_DIM), lambda h, i: (i, h))
    full = pl.BlockSpec((S, HEAD_DIM), lambda h, i: (0, h))
    return pl.pallas_call(
        body, name=name, grid=(H, nq),
        in_specs=[blk, full, full, pl.BlockSpec((nw, T, T), lambda h, i: (0, 0, 0))],
        out_specs=[blk, blk, pl.BlockSpec((None, T, 1), lambda h, i: (h, i, 0))],
        out_shape=[jax.ShapeDtypeStruct((S, A), BF16), jax.ShapeDtypeStruct((S, A), F32),
                   jax.ShapeDtypeStruct((H, S, 1), F32)],
        scratch_shapes=[pltpu.VMEM((nw, T, T), F32)],
        compiler_params=_params("parallel", "arbitrary"),
    )(q, k, v, bias)


def _attn_dq(q, k, v, dz, cb0, o, lse, bias, name):
    S, A = q.shape
    H = A // HEAD_DIM
    nw, T, _ = bias.shape
    nq = S // T

    def body(q_ref, k_ref, v_ref, do_ref, o_ref, l_ref, b_ref, dq_ref, d_ref):
        i = pl.program_id(1)
        qv, dof = q_ref[...], do_ref[...]
        dov = dof.astype(BF16)
        delta = jnp.sum(dof * o_ref[...], axis=-1, keepdims=True)
        d_ref[...] = delta
        lv = l_ref[...]
        dq = jnp.zeros((T, HEAD_DIM), F32)
        for w in range(nw):
            blk = i - w
            start = pl.multiple_of(jnp.maximum(blk, 0) * T, T)
            kv = k_ref[pl.ds(start, T), :]
            s = _dot(qv, kv, "nt") + b_ref[w] + jnp.where(blk >= 0, 0.0, NEG)
            p = jnp.exp(s - lv)
            ds = (p * (_dot(dov, v_ref[pl.ds(start, T), :], "nt") - delta)).astype(BF16)
            dq = dq + _dot(ds, kv, "nn")
        dq_ref[...] = dq

    blk = pl.BlockSpec((T, HEAD_DIM), lambda h, i: (i, h))
    full = pl.BlockSpec((S, HEAD_DIM), lambda h, i: (0, h))
    col = pl.BlockSpec((None, T, 1), lambda h, i: (h, i, 0))
    return pl.pallas_call(
        body, name=name, grid=(H, nq),
        in_specs=[blk, full, full, pl.BlockSpec((T, HEAD_DIM), lambda h, i: (i, cb0 + h)), blk, col,
                  pl.BlockSpec((nw, T, T), lambda h, i: (0, 0, 0))],
        out_specs=[blk, col],
        out_shape=[jax.ShapeDtypeStruct((S, A), F32), jax.ShapeDtypeStruct((H, S, 1), F32)],
        compiler_params=_params("parallel", "arbitrary"),
    )(q, k, v, dz, o, lse, bias)


def _attn_dkv(q, k, v, dz, cb0, lse, delta, bias, name):
    S, A = q.shape
    H = A // HEAD_DIM
    nw, T, _ = bias.shape
    nq = S // T

    def body(k_ref, v_ref, q_ref, do_ref, l_ref, d_ref, b_ref, dk_ref, dv_ref):
        m = pl.program_id(1)
        kv, vv = k_ref[...], v_ref[...]
        dk = jnp.zeros((T, HEAD_DIM), F32)
        dv = jnp.zeros((T, HEAD_DIM), F32)
        for w in range(nw):
            blk = m + w
            start = pl.multiple_of(jnp.minimum(blk, nq - 1) * T, T)
            qv = q_ref[pl.ds(start, T), :]
            dov = do_ref[pl.ds(start, T), :].astype(BF16)
            s = _dot(qv, kv, "nt") + b_ref[w] + jnp.where(blk < nq, 0.0, NEG)
            p = jnp.exp(s - l_ref[pl.ds(start, T), :])
            dv = dv + _dot(p.astype(BF16), dov, "tn")
            ds = (p * (_dot(dov, vv, "nt") - d_ref[pl.ds(start, T), :])).astype(BF16)
            dk = dk + _dot(ds, qv, "tn")
        dk_ref[...] = dk
        dv_ref[...] = dv

    blk = pl.BlockSpec((T, HEAD_DIM), lambda h, m: (m, h))
    full = pl.BlockSpec((S, HEAD_DIM), lambda h, m: (0, h))
    col = pl.BlockSpec((None, S, 1), lambda h, m: (h, 0, 0))
    sds = jax.ShapeDtypeStruct((S, A), F32)
    return pl.pallas_call(
        body, name=name, grid=(H, nq),
        in_specs=[blk, blk, full, pl.BlockSpec((S, HEAD_DIM), lambda h, m: (0, cb0 + h)), col, col,
                  pl.BlockSpec((nw, T, T), lambda h, m: (0, 0, 0))],
        out_specs=[blk, blk], out_shape=[sds, sds],
        compiler_params=_params("parallel", "arbitrary"),
    )(k, v, q, dz, lse, delta, bias)


def _even_mixer(u, conv_w, conv_b, cn_g, cn_b, qn_g, kn_g, tag):
    S = u.shape[0]
    C = conv_w.shape[1]
    A = (u.shape[1] - 2 * C) // 3
    assert A == C, "column-block addressing of u assumes equal conv and attention widths"
    cos, sin = _rope_tables(S)
    bias = _attn_bias(_tile(S, ATT_TILE_BWD, LANE))
    a_out, y = _conv_fwd(u, conv_w, conv_b, cn_g, cn_b, "conv_fwd" + tag)
    q, k, v = _qkv_prep(u, qn_g, kn_g, cos, sin, 2, "qkv_prep" + tag)
    ob, of, lse = _attn_fwd(q, k, v, _attn_bias(_tile(S, ATT_TILE_FWD, LANE)), "attn_fwd" + tag)
    z = jnp.concatenate([a_out, ob], axis=1)

    def backward(dz):
        dy, d_cn_g, d_cn_b = _conv_bwd_norm(dz, y, cn_g, cn_b, "conv_bwd_norm" + tag)
        d_val, d_gate, d_w, d_b = _conv_bwd_taps(u, dy, conv_w, "conv_bwd_taps" + tag)
        dqp, delta = _attn_dq(q, k, v, dz, C // HEAD_DIM, of, lse, bias, "attn_dq" + tag)
        dkp, dvp = _attn_dkv(q, k, v, dz, C // HEAD_DIM, lse, delta, bias, "attn_dkv" + tag)
        dq, dk, dv, d_qn, d_kn = _qkv_prep_bwd(u, [dqp], [dkp], [dvp], qn_g, kn_g, cos, sin, 2, "qkv_prep_bwd" + tag)
        du = jnp.concatenate([d_val, d_gate, dq, dk, dv], axis=1)
        return du, [d_w[:CONV_WIDTH], d_b[0], d_cn_g[0], d_cn_b[0], d_qn[0], d_kn[0]]

    return z, backward


_LEVELS = (128, 64, 32, 16, 8, 4, 2, 1)


def _chunk_cumsum(g, rows, reverse=False):
    C = g.shape[0]
    d = 1
    while d < C:
        if reverse:
            g = g + jnp.where(rows < C - d, pltpu.roll(g, C - d, 0), 0.0)
        else:
            g = g + jnp.where(rows >= d, pltpu.roll(g, d, 0), 0.0)
        d *= 2
    return g


def _level_ref(b, b_scr, rows, m):
    C = b.shape[0]
    if m >= 8:
        pieces = [jnp.broadcast_to(b_scr[2 * m * j + m - 1:2 * m * j + m, :], (2 * m, LANE)) for j in range(C // (2 * m))]
        return pieces[0] if len(pieces) == 1 else jnp.concatenate(pieces, axis=0)
    pos = rows & (2 * m - 1)
    ref = b
    for p in range(2 * m):
        if p != m - 1:
            ref = jnp.where(pos == p, pltpu.roll(b, (p - (m - 1)) % C, 0), ref)
    return ref


def _level_operands(q, k, b, b_scr, rows, m):
    ref = _level_ref(b, b_scr, rows, m)
    qs = (q * jnp.exp(jnp.minimum(b - ref, 0.0))).astype(BF16)
    ks = (k * jnp.exp(jnp.minimum(ref - b, 0.0))).astype(BF16)
    return qs, ks


def _split2(x):
    hi = x.astype(BF16)
    lo = (x - hi.astype(F32)).astype(BF16)
    return jnp.concatenate([hi, lo], axis=1)


def _level_table(n):
    t = jnp.arange(n, dtype=jnp.int32)[:, None]
    s = jnp.arange(n, dtype=jnp.int32)[None, :]
    x = t ^ s
    lvl = sum((x >= (1 << j)).astype(jnp.int32) for j in range(1, n.bit_length()))
    return jnp.where(t > s, lvl, jnp.where(t == s, -1, -2))


def _hgrn_gates(qz, fz, la, lc, oml):
    sq = jax.nn.sigmoid(qz)
    q = qz * sq
    s = jax.nn.sigmoid(fz)
    c = lc + jnp.minimum(fz, 0.0) - jnp.log(1.0 + jnp.exp(-jnp.abs(fz)))
    mx = jnp.maximum(la, c)
    g = mx + jnp.log(1.0 + jnp.exp(-jnp.abs(la - c)))
    k = oml * (1.0 - s)
    return q, sq, k, s, g, c


def _hgrn_fwd(u, la, lc, oml, gn_g, name):
    S = u.shape[0]
    W = u.shape[1] // 4
    H = W // HGRN_KDIM
    C = min(HGRN_CHUNK, S)
    nc = S // C
    levels = [m for m in _LEVELS if m < C]
    HB = C // 2

    def body(qz_ref, fz_ref, iz_ref, gz_ref, la_ref, lc_ref, oml_ref, gn_ref, lvl_ref,
             z_ref, o_ref, a_ref, st_ref, state, b_scr):
        @pl.when(pl.program_id(1) == 0)
        def _():
            state[...] = jnp.zeros_like(state)

        rows = lax.broadcasted_iota(jnp.int32, (C, LANE), 0)
        q, _, k, _, g, _ = _hgrn_gates(qz_ref[...], fz_ref[...], la_ref[...], lc_ref[...], oml_ref[...])
        v = iz_ref[...].astype(BF16)
        b = _chunk_cumsum(g, rows)
        b_scr[...] = b
        lvl = lvl_ref[...]
        qk = jnp.sum(q * k, axis=-1, keepdims=True)
        diag = [jnp.where(lvl == -1, qk[r * HB:(r + 1) * HB], 0.0) for r in range(2)]
        for m in levels[1:]:
            qs, ks = _level_operands(q, k, b, b_scr, rows, m)
            for r in range(2):
                sl = slice(r * HB, (r + 1) * HB)
                diag[r] = jnp.where(lvl == m.bit_length() - 1, _dot(qs[sl], ks[sl], "nt"), diag[r])
        qs, ks = _level_operands(q, k, b, b_scr, rows, HB)
        low = _dot(qs[HB:], ks[:HB], "nt")
        a = jnp.concatenate([jnp.concatenate([diag[0], jnp.zeros((HB, HB), F32)], axis=1),
                             jnp.concatenate([low, diag[1]], axis=1)], axis=0)
        ab = a.astype(BF16)
        a_ref[...] = ab
        st = state[...]
        st_ref[...] = st
        o = _dot(ab, v, "nn") + _dot((q * jnp.exp(b)).astype(BF16), st.astype(BF16), "nt")
        bl = b_scr[C - 1:C, :]
        kh = (k * jnp.exp(bl - b)).astype(BF16)
        state[...] = st * jnp.exp(bl) + _dot(v, kh, "tn")
        o_ref[...] = o
        r = lax.rsqrt(jnp.mean(o * o, axis=-1, keepdims=True) + EPS)
        gz = gz_ref[...]
        z_ref[...] = (o * r * gn_ref[...] * (gz * jax.nn.sigmoid(gz))).astype(BF16)

    def col(off):
        return pl.BlockSpec((C, LANE), lambda h, i: (i, off * H + h))

    vec = pl.BlockSpec((1, LANE), lambda h, i: (0, h))
    tile = pl.BlockSpec((C, LANE), lambda h, i: (i, h))
    return pl.pallas_call(
        body, name=name, grid=(H, nc),
        in_specs=[col(0), col(1), col(2), col(3), vec, vec, vec, vec, pl.BlockSpec((HB, HB), lambda h, i: (0, 0))],
        out_specs=[tile, tile, pl.BlockSpec((None, C, C), lambda h, i: (h, i, 0)),
                   pl.BlockSpec((None, None, LANE, LANE), lambda h, i: (h, i, 0, 0))],
        out_shape=[jax.ShapeDtypeStruct((S, W), BF16), jax.ShapeDtypeStruct((S, W), F32),
                   jax.ShapeDtypeStruct((H, S, C), BF16), jax.ShapeDtypeStruct((H, nc, LANE, LANE), F32)],
        scratch_shapes=[pltpu.VMEM((LANE, LANE), F32), pltpu.VMEM((C, LANE), F32)],
        compiler_params=_params("parallel", "arbitrary"),
    )(u, u, u, u, la, lc, oml, gn_g, _level_table(HB))


def _hgrn_bwd(u, la, lc, oml, gn_g, o, a, st, dz, name):
    S = u.shape[0]
    W = u.shape[1] // 4
    H = W // HGRN_KDIM
    C = min(HGRN_CHUNK, S)
    nc = S // C
    levels = [m for m in _LEVELS if m < C]
    HB = C // 2

    def body(qz_ref, fz_ref, iz_ref, gz_ref, la_ref, lc_ref, oml_ref, gn_ref, o_ref, a_ref, st_ref, dz_ref, lvl_ref,
             dqz_ref, dfz_ref, diz_ref, dgz_ref, dla_ref, dlc_ref, doml_ref, dgn_ref, dstate, b_scr):
        @pl.when(pl.program_id(1) == 0)
        def _():
            dstate[...] = jnp.zeros_like(dstate)
            dla_ref[...] = jnp.zeros_like(dla_ref)
            dlc_ref[...] = jnp.zeros_like(dlc_ref)
            doml_ref[...] = jnp.zeros_like(doml_ref)
            dgn_ref[...] = jnp.zeros_like(dgn_ref)

        rows = lax.broadcasted_iota(jnp.int32, (C, LANE), 0)
        la_v, lc_v, oml_v = la_ref[...], lc_ref[...], oml_ref[...]
        qz, fz = qz_ref[...], fz_ref[...]
        q, sq, k, s, g, c = _hgrn_gates(qz, fz, la_v, lc_v, oml_v)
        vf = iz_ref[...]
        v = vf.astype(BF16)

        ov, gz, dzv, gn = o_ref[...], gz_ref[...], dz_ref[...], gn_ref[...]
        r = lax.rsqrt(jnp.mean(ov * ov, axis=-1, keepdims=True) + EPS)
        on = ov * r
        sg = jax.nn.sigmoid(gz)
        silu_g = gz * sg
        dgn_ref[...] += jnp.sum(dzv * on * silu_g, axis=0, keepdims=True)
        dgz_ref[...] = (dzv * on * gn * (sg * (1.0 + gz * (1.0 - sg)))).astype(BF16)
        don = dzv * gn * silu_g
        do_f = r * (don - on * jnp.mean(don * on, axis=-1, keepdims=True))
        do = do_f.astype(BF16)

        b = _chunk_cumsum(g, rows)
        b_scr[...] = b
        bl = b_scr[C - 1:C, :]
        e = jnp.exp(b)
        ebl = jnp.exp(bl)
        ekl = jnp.exp(bl - b)
        qh = q * e
        kh = k * ekl
        st_v = st_ref[...]
        dst = dstate[...]
        dstb = dst.astype(BF16)

        diz_ref[...] = (_dot(a_ref[...], do, "tn") + _dot(kh.astype(BF16), dstb, "nt")).astype(BF16)
        da = _dot(do, v, "nt")
        dqh = _dot(do, st_v.astype(BF16), "nn")
        dkh = _dot(v, dstb, "nn")
        dstate[...] = dst * ebl + _dot(do, qh.astype(BF16), "tn")
        dbl = jnp.sum(dkh * kh, axis=0, keepdims=True) + jnp.sum(dst * st_v, axis=0, keepdims=True) * ebl

        datt = jnp.sum(do_f * vf, axis=-1, keepdims=True)
        dqa = datt * k
        dka = datt * q
        lvl = lvl_ref[...]
        for m in levels:
            ref = _level_ref(b, b_scr, rows, m)
            eu = jnp.exp(jnp.minimum(b - ref, 0.0))
            el = jnp.exp(jnp.minimum(ref - b, 0.0))
            ks2, qs2 = _split2(k * el), _split2(q * eu)
            if m == HB:
                gm = da[HB:, :HB].astype(BF16)
                pq = jnp.concatenate([jnp.zeros((HB, 2 * LANE), F32), _dot(gm, ks2[:HB], "nn")], axis=0)
                pk = jnp.concatenate([_dot(gm, qs2[HB:], "tn"), jnp.zeros((HB, 2 * LANE), F32)], axis=0)
            else:
                gms = [jnp.where(lvl == m.bit_length() - 1, da[r * HB:(r + 1) * HB, r * HB:(r + 1) * HB], 0.0).astype(BF16)
                       for r in range(2)]
                pq = jnp.concatenate([_dot(gms[r], ks2[r * HB:(r + 1) * HB], "nn") for r in range(2)], axis=0)
                pk = jnp.concatenate([_dot(gms[r], qs2[r * HB:(r + 1) * HB], "tn") for r in range(2)], axis=0)
            dqa += (pq[:, :LANE] + pq[:, LANE:]) * eu
            dka += (pk[:, :LANE] + pk[:, LANE:]) * el
        db = q * dqa - k * dka + dqh * qh - dkh * kh
        db = db + jnp.where(rows == C - 1, dbl, 0.0)
        dq = dqa + dqh * e
        dk = dka + dkh * ekl
        dg = _chunk_cumsum(db, rows, reverse=True)

        wa = jnp.exp(la_v - g)
        wc = jnp.exp(c - g)
        dqz_ref[...] = (dq * (sq * (1.0 + qz * (1.0 - sq)))).astype(BF16)
        dfz_ref[...] = (dg * wc * (1.0 - s) - dk * oml_v * s * (1.0 - s)).astype(BF16)
        dla_ref[...] += jnp.sum(dg * wa, axis=0, keepdims=True)
        dlc_ref[...] += jnp.sum(dg * wc, axis=0, keepdims=True)
        doml_ref[...] += jnp.sum(dk * (1.0 - s), axis=0, keepdims=True)

    def col(off):
        return pl.BlockSpec((C, LANE), lambda h, i: (nc - 1 - i, off * H + h))

    vec = pl.BlockSpec((1, LANE), lambda h, i: (0, h))
    tile = pl.BlockSpec((C, LANE), lambda h, i: (nc - 1 - i, h))
    a_spec = pl.BlockSpec((None, C, C), lambda h, i: (h, nc - 1 - i, 0))
    st_spec = pl.BlockSpec((None, None, LANE, LANE), lambda h, i: (h, nc - 1 - i, 0, 0))
    sw = jax.ShapeDtypeStruct((S, W), BF16)
    vw = jax.ShapeDtypeStruct((1, W), F32)
    return pl.pallas_call(
        body, name=name, grid=(H, nc),
        in_specs=[col(0), col(1), col(2), col(3), vec, vec, vec, vec, tile, a_spec, st_spec, tile,
                  pl.BlockSpec((HB, HB), lambda h, i: (0, 0))],
        out_specs=[tile, tile, tile, tile, vec, vec, vec, vec],
        out_shape=[sw, sw, sw, sw, vw, vw, vw, vw],
        scratch_shapes=[pltpu.VMEM((LANE, LANE), F32), pltpu.VMEM((C, LANE), F32)],
        compiler_params=_params("parallel", "arbitrary"),
    )(u, u, u, u, la, lc, oml, gn_g, o, a, st, dz, _level_table(HB))


def _lb_terms(lb_logits, layer):
    p = jax.nn.softmax(lb_logits, axis=0)
    lb = (jnp.cumsum(p, axis=0) - p[0:1])[layer]
    return jnp.log(lb)[None], jnp.log1p(-lb)[None], (1.0 - lb)[None]


def kernel(x, norm_ffn1, ffn1_wg, ffn1_wu, ffn1_wd, norm_mix, norm_ffn2, ffn2_wg, ffn2_wu, ffn2_wd, ev_w_in, ev_conv_w, ev_conv_b, ev_cn_g, ev_cn_b, ev_qn_g, ev_kn_g, ev_w_out, od_w_in, od_lb_logits, od_gn_g, od_w_out, loss_target, m_norm_ffn1, m_ffn1_wg, m_ffn1_wu, m_ffn1_wd, m_norm_mix, m_norm_ffn2, m_ffn2_wg, m_ffn2_wu, m_ffn2_wd, m_ev_w_in, m_ev_conv_w, m_ev_conv_b, m_ev_cn_g, m_ev_cn_b, m_ev_qn_g, m_ev_kn_g, m_ev_w_out, m_od_w_in, m_od_lb_logits, m_od_gn_g, m_od_w_out, v_norm_ffn1, v_ffn1_wg, v_ffn1_wu, v_ffn1_wd, v_norm_mix, v_norm_ffn2, v_ffn2_wg, v_ffn2_wu, v_ffn2_wd, v_ev_w_in, v_ev_conv_w, v_ev_conv_b, v_ev_cn_g, v_ev_cn_b, v_ev_qn_g, v_ev_kn_g, v_ev_w_out, v_od_w_in, v_od_lb_logits, v_od_gn_g, v_od_w_out):
    depth = norm_ffn1.shape[0]
    S, D = x.shape[1], x.shape[2]
    xi, yi, ci = _me()
    dev = 4 * xi + 2 * yi + ci
    c_idx = jnp.reshape(ci, (1,)).astype(jnp.int32)
    k_idx = jnp.reshape(2 * xi + yi, (1,)).astype(jnp.int32)

    def ffn_shard(wg, wu, wd, l):
        return jnp.stack([wg[l].T, wu[l].T, wd[l]]).astype(BF16)

    assert depth == 2, "the exchange schedule below is written for one even and one odd layer"
    sh_ffn1 = [ffn_shard(ffn1_wg, ffn1_wu, ffn1_wd, l) for l in range(depth)]
    sh_ffn2 = [ffn_shard(ffn2_wg, ffn2_wu, ffn2_wd, l) for l in range(depth)]
    sh_ev = [ev_w_in[0].T.astype(BF16)[None], ev_w_out[0].astype(BF16)[None]]
    sh_od = [od_w_in[0].T.astype(BF16)[None], od_w_out[0].astype(BF16)[None]]

    def full(g):
        return g.reshape(g.shape[0], N_DEV * g.shape[2], g.shape[3])

    def gather_begin(shards, after, tag):
        lands = [lax.dynamic_update_slice(lax.empty((s.shape[0], N_DEV) + s.shape[1:], s.dtype), s[:, None],
                                          (0, dev, 0, 0)) for s in shards]
        state = _push_start(shards, lands, _gather_plan, after, "gather_start" + tag)
        return state, state[4][0, 0]

    def gather_arrived(state, after, tag):
        send, recv, srcs, lands, _ = state
        _, lands = _push_wait(send, recv, srcs, lands, _gather_plan, after, "gather_wait" + tag)
        state = _push_start([], lands, _forward_plan, None, "forward_start" + tag)
        return state, state[4][0, 0]

    def gather_done(state, after, tag):
        send, recv, _, lands, _ = state
        _, lands = _push_wait(send, recv, [], lands, _forward_plan, after, "forward_wait" + tag)
        return [full(g) for g in lands]

    def gather_end(state, after, tag):
        state, _ = gather_arrived(state, after, tag)
        return gather_done(state, state[4], tag)

    conv_w_sh, gn_g_sh = ev_conv_w[0], od_gn_g[0]
    cw, cs = conv_w_sh.shape[0], conv_w_sh.shape[1]
    gs = gn_g_sh.shape[0]
    conv_w_z = lax.dynamic_update_slice(jnp.zeros((cw, N_DEV * cs), F32), conv_w_sh, (0, dev * cs))
    gn_g_z = lax.dynamic_update_slice(jnp.zeros((N_DEV * gs,), F32), gn_g_sh, (dev * gs,))
    conv_w_full, gn_g_full = _unpack_rows(
        _all_reduce_small(_pack_rows([conv_w_z, gn_g_z]), "gather_small_params"),
        [conv_w_z.shape, gn_g_z.shape])

    w_ffn1, w_ffn2 = [None] * depth, [None] * depth
    pending, after = {}, conv_w_full
    for key, shards in (("0", [sh_ffn1[0]]), ("1", sh_ev), ("2", [sh_ffn2[0]]), ("3", [sh_ffn1[1]]), ("4", sh_od),
                        ("5", [sh_ffn2[1]])):
        pending[key], _ = gather_begin(shards, after, "_" + key)
        after = pending[key][4]
    start_tok = after[0, 0]
    (w_ffn1[0],) = gather_end(pending.pop("0"), after, "_0")

    def odd_mixer(u, l, tok):
        (la, lc, oml), lb_vjp = jax.vjp(functools.partial(_lb_terms, layer=l), od_lb_logits)
        gn = (gn_g_full + tok)[None]
        zb, o_raw, scores, states = _hgrn_fwd(u, la, lc, oml, gn, f"hgrn_fwd{l}")

        def backward(dz):
            dqz, dfz, diz, dgz, dla, dlc, doml, dgn = _hgrn_bwd(
                u, la, lc, oml, gn, o_raw, scores, states, dz, f"hgrn_bwd{l}")
            (g_lb,) = lb_vjp((dla, dlc, doml))
            return jnp.concatenate([dqz, dfz, diz, dgz], axis=1), [g_lb, dgn[0]]

        return zb, backward

    saved = []
    h = x[0]
    hn = _rms_fwd(h, (norm_ffn1[0] + start_tok)[None], "rms_a0")
    for l in range(depth):
        ffn, gu = _ffn_fwd(hn, w_ffn1[l], f"ffn_fwd_a{l}")
        s1 = (h, hn, gu)
        if l == 0:
            w_in, w_out = gather_end(pending.pop("1"), ffn, "_1")
        else:
            w_in, w_out = gather_done(pending.pop("4"), ffn, "_4")
        w_in_t, w_out = w_in[0], w_out[0]
        h, hn = _resid_rms(h, ffn, norm_mix[l][None], f"rms_mix{l}")
        u = _mm(hn, w_in_t, "nt", F32, f"mix_in{l}")
        key = "2" if l == 0 else "5"
        passing, tok = gather_arrived(pending.pop(key), u, "_" + key)
        if l % 2 == 0:
            zb, core_vjp = _even_mixer(u, conv_w_full, ev_conv_b + tok, ev_cn_g, ev_cn_b, ev_qn_g, ev_kn_g, str(l))
        else:
            zb, core_vjp = odd_mixer(u, l, tok)
        h_mix = h
        h = _mm(zb, w_out, "nn", F32, f"mix_out{l}", res=h)
        sm = (h_mix, hn, zb, core_vjp, w_in_t, w_out)
        (w_ffn2[l],) = gather_done(passing, h, "_" + key)
        tok = 0.0
        if l + 1 < depth:
            passing, tok = gather_arrived(pending.pop("3"), w_ffn2[l], "_3")
        hn = _rms_fwd(h, (norm_ffn2[l] + tok)[None], f"rms_b{l}")
        ffn, gu = _ffn_fwd(hn, w_ffn2[l], f"ffn_fwd_b{l}")
        saved.append((s1, sm, (h, hn, gu)))
        if l + 1 < depth:
            (w_ffn1[l + 1],) = gather_done(passing, ffn, "_3")
            pending["4"], tok = gather_arrived(pending.pop("4"), w_ffn1[l + 1], "_4")
            h, hn = _resid_rms(h, ffn, (norm_ffn1[l + 1] + tok)[None], f"rms_a{l + 1}")

    dy, loss_part = _loss_grad(h, ffn, loss_target[0], "loss_grad")

    def halves_begin(parts, tag):
        parts = [g.reshape(g.shape[0], 4, 2, g.shape[1] // N_DEV, g.shape[2]) for g in parts]
        lands = [lax.empty(g.shape[:2] + g.shape[3:], BF16) for g in parts]
        state = _push_start(parts, lands, _halves_plan, None, "halves_start" + tag)
        return state, state[4][0, 0]

    def chips_begin(state, after, tag):
        send, recv, srcs, lands, _ = state
        parts, got = _push_wait(send, recv, srcs, lands, _halves_plan, after, "halves_wait" + tag)
        sums = [_add_core_halves(g, r, c_idx, f"add_core_halves{tag}_{a}") for a, (g, r) in enumerate(zip(parts, got))]
        lands = [lax.empty((3, s.shape[0]) + s.shape[2:], BF16) for s in sums]
        state = _push_start(sums, lands, _chip_plan, None, "reduce_start" + tag)
        return state, state[4][0, 0]

    def reduce_end(state, after, tag):
        send, recv, srcs, lands, _ = state
        sums, got = _push_wait(send, recv, srcs, lands, _chip_plan, after, "reduce_wait" + tag)
        return [_sum_chip_blocks(s, r, k_idx, f"sum_chip_blocks{tag}_{a}") for a, (s, r) in enumerate(zip(sums, got))]

    def ffn_backward(dy, gain, w, sv, tag, on_dw, on_dx=None):
        h_in, hn, gu = sv
        dxn, dout, t = _ffn_bwd_dx(dy, w, gu, "ffn_bwd_dx_" + tag)
        tok = 0.0 if on_dx is None else on_dx(dxn)
        tok = tok + on_dw(_ffn_bwd_dw(hn, dout, t, "ffn_bwd_dw_" + tag))
        dx, dgain = _rms_bwd(h_in, (gain + tok)[None], dxn, dy, "rms_bwd_" + tag)
        return dx, dgain[0]

    g_norm1, g_norm2, g_normm = [None] * depth, [None] * depth, [None] * depth
    small, halves, groups = [None, None], {}, {}

    def start_halves(key, make_parts):
        def hook(dw):
            halves[key], tok = halves_begin(make_parts(dw), "_" + key)
            return tok
        return hook

    def start_chips(key):
        def hook(after):
            groups[key], tok = chips_begin(halves.pop(key), after, "_" + key)
            return tok
        return hook

    for l in reversed(range(depth)):
        s1, (h_mix, hn, zb, core_vjp, w_in_t, w_out), s2 = saved[l]
        if l == 1:
            dy, g_norm2[l] = ffn_backward(dy, norm_ffn2[l], w_ffn2[l], s2, f"b{l}", start_halves("1", lambda dw: [dw]))
        else:
            dy, g_norm2[l] = ffn_backward(dy, norm_ffn2[l], w_ffn2[l], s2, f"b{l}", start_halves("3", lambda dw: [dw]),
                                          start_chips("2"))
        dyb = dy.astype(BF16)
        dz = _mm(dyb, w_out, "nt", F32, f"mix_out_dz{l}")
        dw_out = _mm(zb, dyb, "tn", BF16, f"mix_out_dw{l}")
        dub, small[l % 2] = core_vjp(dz)
        dw_in_t = _mm(dub, hn, "tn", BF16, f"mix_in_dw{l}")
        mix_parts = [dw_in_t[None], dw_out[None]]
        if l == 1:
            tok = start_chips("1")(dw_in_t)
        else:
            tok = start_chips("3")(dw_in_t) + start_halves("4", lambda _: mix_parts)(None)
        dhn = _mm(dub, w_in_t, "nn", F32, f"mix_in_dx{l}")
        dy, gm = _rms_bwd(h_mix, (norm_mix[l] + tok)[None], dhn, dy, f"rms_bwd_mix{l}")
        g_normm[l] = gm[0]
        if l == 1:
            dy, g_norm1[l] = ffn_backward(dy, norm_ffn1[l], w_ffn1[l], s1, f"a{l}",
                                          start_halves("2", lambda dw, od=mix_parts: od + [dw]))
        else:
            dy, g_norm1[l] = ffn_backward(dy, norm_ffn1[l], w_ffn1[l], s1, f"a{l}", start_halves("5", lambda dw: [dw]),
                                          start_chips("4"))
    grad_x = dy[None]
    start_chips("5")(dy)

    done = [dy, groups["5"][4]]
    (g_ffn2_1,) = reduce_end(groups["1"], done, "_1")
    g_od_in_t, g_od_out, g_ffn1_1 = reduce_end(groups["2"], done, "_2")
    (g_ffn2_0,) = reduce_end(groups["3"], done, "_3")
    g_ev_in_t, g_ev_out = reduce_end(groups["4"], done, "_4")
    g_ffn2 = [g_ffn2_0, g_ffn2_1]

    def ffn_grads(gl):
        return (jnp.stack([g[0].T for g in gl]), jnp.stack([g[1].T for g in gl]), jnp.stack([g[2] for g in gl]))

    g_ffn2_wg, g_ffn2_wu, g_ffn2_wd = ffn_grads(g_ffn2)
    grads = [None, None, None, None, None, None, g_ffn2_wg, g_ffn2_wu, g_ffn2_wd,
             g_ev_in_t[0].T[None], None, None, None, None, None,
             None, g_ev_out, g_od_in_t[0].T[None], None, None, g_od_out]
    weights = [norm_ffn1, ffn1_wg, ffn1_wu, ffn1_wd, norm_mix, norm_ffn2, ffn2_wg, ffn2_wu, ffn2_wd, ev_w_in,
               ev_conv_w, ev_conv_b, ev_cn_g, ev_cn_b, ev_qn_g, ev_kn_g, ev_w_out, od_w_in, od_lb_logits,
               od_gn_g, od_w_out]
    moms = [m_norm_ffn1, m_ffn1_wg, m_ffn1_wu, m_ffn1_wd, m_norm_mix, m_norm_ffn2, m_ffn2_wg, m_ffn2_wu,
            m_ffn2_wd, m_ev_w_in, m_ev_conv_w, m_ev_conv_b, m_ev_cn_g, m_ev_cn_b, m_ev_qn_g, m_ev_kn_g,
            m_ev_w_out, m_od_w_in, m_od_lb_logits, m_od_gn_g, m_od_w_out]
    vars_ = [v_norm_ffn1, v_ffn1_wg, v_ffn1_wu, v_ffn1_wd, v_norm_mix, v_norm_ffn2, v_ffn2_wg, v_ffn2_wu,
             v_ffn2_wd, v_ev_w_in, v_ev_conv_w, v_ev_conv_b, v_ev_cn_g, v_ev_cn_b, v_ev_qn_g, v_ev_kn_g,
             v_ev_w_out, v_od_w_in, v_od_lb_logits, v_od_gn_g, v_od_w_out]
    n_w = len(weights)
    deltas, new_m, new_v = [None] * n_w, [None] * n_w, [None] * n_w

    def update(idx):
        for i in idx:
            deltas[i], new_m[i], new_v[i] = _adamw(weights[i], grads[i], moms[i], vars_[i], f"adamw{i}")

    update([i for i in range(n_w) if grads[i] is not None])
    g_conv_w, g_conv_b, g_cn_g, g_cn_b, g_qn_g, g_kn_g = small[0]
    g_lb, g_gn = small[1]
    parts = [jnp.stack(g_norm1), jnp.stack(g_normm), jnp.stack(g_norm2), g_conv_b, g_cn_g, g_cn_b,
             g_qn_g, g_kn_g, g_lb, g_conv_w, g_gn, loss_part[0, :1]]
    red = _unpack_rows(_all_reduce_small(_pack_rows(parts), "reduce_small_grads", [d for d in deltas if d is not None]),
                       [p.shape for p in parts])
    g_norm1, g_normm, g_norm2, g_conv_b, g_cn_g, g_cn_b, g_qn_g, g_kn_g, g_lb, g_conv_w, g_gn, loss = red
    g_conv_w = lax.dynamic_slice(g_conv_w, (0, dev * cs), (cw, cs))
    g_gn = lax.dynamic_slice(g_gn, (dev * gs,), (gs,))
    small_idx = {0: g_norm1, 4: g_normm, 5: g_norm2, 10: g_conv_w[None], 11: g_conv_b[None], 12: g_cn_g[None],
                 13: g_cn_b[None], 14: g_qn_g[None], 15: g_kn_g[None], 18: g_lb, 19: g_gn[None]}
    for i, g in small_idx.items():
        grads[i] = g
    update(small_idx)
    (g_ffn1_0,) = reduce_end(groups["5"], [d for d in deltas if d is not None], "_5")
    grads[1], grads[2], grads[3] = ffn_grads([g_ffn1_0, g_ffn1_1])
    update((1, 2, 3))
    return (loss[0], grad_x, *grads, *deltas, *new_m, *new_v)
```

```python
import functools
import math

import jax
import jax.numpy as jnp
from jax import lax
from jax.experimental import pallas as pl
from jax.experimental.pallas import tpu as pltpu

F32 = jnp.float32
BF16 = jnp.bfloat16
MESH = pl.DeviceIdType.MESH
N_DEV = 8

EPS = 1e-6
HEAD_DIM = 128
CONV_WIDTH = 31
DIL_PATTERNS = ((128, 1), (512, 4), (2048, 16))
Q_BLOCK = 128
ROPE_THETA = 10000.0
HGRN_KDIM = 128
HGRN_CHUNK = 256

ADAM_LR = 0.001
ADAM_B1 = 0.9
ADAM_B2 = 0.999
ADAM_EPS = 1e-08
ADAM_WD = 0.01
ADAM_STEP = 10

VMEM_LIMIT_BYTES = 56 * 1024 * 1024
LANE = 128
SUBLANE_BF16 = 16

ANY = pl.BlockSpec(memory_space=pl.ANY)


def _tile(n, pref, mult):
    t = (min(pref, n) // mult) * mult
    while t > 0:
        if n % t == 0:
            return t
        t -= mult
    return n


def _params(*sem):
    return pltpu.CompilerParams(dimension_semantics=sem, vmem_limit_bytes=VMEM_LIMIT_BYTES)


_DOT_DIMS = {
    "nn": (((1,), (0,)), ((), ())),
    "nt": (((1,), (1,)), ((), ())),
    "tn": (((0,), (0,)), ((), ())),
}


def _dot(a, b, mode):
    return lax.dot_general(a, b, _DOT_DIMS[mode], preferred_element_type=F32)


def _mm(a, b, mode, out_dtype, name, res=None, tm=1024, tn=1024, tk=2048):
    if mode == "nt":
        (M, K), N = a.shape, b.shape[0]
    elif mode == "nn":
        (M, K), N = a.shape, b.shape[1]
    else:
        (K, M), N = a.shape, b.shape[1]
    tm, tn, tk = _tile(M, tm, LANE), _tile(N, tn, LANE), _tile(K, tk, LANE)
    nk = K // tk

    def body(*refs):
        if res is None:
            a_ref, b_ref, o_ref, acc = refs
        else:
            a_ref, b_ref, r_ref, o_ref, acc = refs
        k = pl.program_id(2)

        @pl.when(k == 0)
        def _():
            acc[...] = jnp.zeros_like(acc)

        acc[...] += _dot(a_ref[...].astype(BF16), b_ref[...].astype(BF16), mode)

        @pl.when(k == nk - 1)
        def _():
            r = acc[...]
            if res is not None:
                r = r_ref[...] + r
            o_ref[...] = r.astype(out_dtype)

    a_spec = {"nt": pl.BlockSpec((tm, tk), lambda i, j, k: (i, k)),
              "nn": pl.BlockSpec((tm, tk), lambda i, j, k: (i, k)),
              "tn": pl.BlockSpec((tk, tm), lambda i, j, k: (k, i))}[mode]
    b_spec = {"nt": pl.BlockSpec((tn, tk), lambda i, j, k: (j, k)),
              "nn": pl.BlockSpec((tk, tn), lambda i, j, k: (k, j)),
              "tn": pl.BlockSpec((tk, tn), lambda i, j, k: (k, j))}[mode]
    o_spec = pl.BlockSpec((tm, tn), lambda i, j, k: (i, j))
    in_specs = [a_spec, b_spec] + ([o_spec] if res is not None else [])
    args = (a, b) + ((res,) if res is not None else ())
    return pl.pallas_call(
        body, name=name, grid=(M // tm, N // tn, nk),
        in_specs=in_specs, out_specs=o_spec,
        out_shape=jax.ShapeDtypeStruct((M, N), out_dtype),
        scratch_shapes=[pltpu.VMEM((tm, tn), F32)],
        compiler_params=_params("parallel", "parallel", "arbitrary"),
    )(*args)


def _rms_fwd(x, gain, name):
    S, D = x.shape
    tm = _tile(S, 512, SUBLANE_BF16)

    def body(x_ref, g_ref, o_ref):
        xv = x_ref[...]
        r = lax.rsqrt(jnp.mean(xv * xv, axis=-1, keepdims=True) + EPS)
        o_ref[...] = (xv * r * g_ref[...]).astype(BF16)

    return pl.pallas_call(
        body, name=name, grid=(S // tm,),
        in_specs=[pl.BlockSpec((tm, D), lambda i: (i, 0)), pl.BlockSpec((1, D), lambda i: (0, 0))],
        out_specs=pl.BlockSpec((tm, D), lambda i: (i, 0)),
        out_shape=jax.ShapeDtypeStruct((S, D), BF16),
        compiler_params=_params("parallel"),
    )(x, gain)


def _resid_rms(x, ffn, gain, name):
    S, D = x.shape
    tm = _tile(S, 512, SUBLANE_BF16)

    def body(x_ref, f_ref, g_ref, h_ref, o_ref):
        hv = x_ref[...] + 0.5 * f_ref[...]
        h_ref[...] = hv
        r = lax.rsqrt(jnp.mean(hv * hv, axis=-1, keepdims=True) + EPS)
        o_ref[...] = (hv * r * g_ref[...]).astype(BF16)

    row = pl.BlockSpec((tm, D), lambda i: (i, 0))
    return pl.pallas_call(
        body, name=name, grid=(S // tm,),
        in_specs=[row, row, pl.BlockSpec((1, D), lambda i: (0, 0))], out_specs=[row, row],
        out_shape=[jax.ShapeDtypeStruct((S, D), F32), jax.ShapeDtypeStruct((S, D), BF16)],
        compiler_params=_params("parallel"),
    )(x, ffn, gain)


def _rms_bwd(x, gain, dxn, dy, name):
    S, D = x.shape
    tm = _tile(S, 512, 8)

    def body(x_ref, g_ref, dxn_ref, dy_ref, dx_ref, dg_ref):
        @pl.when(pl.program_id(0) == 0)
        def _():
            dg_ref[...] = jnp.zeros_like(dg_ref)

        xv = x_ref[...]
        r = lax.rsqrt(jnp.mean(xv * xv, axis=-1, keepdims=True) + EPS)
        xh = xv * r
        dxn_v = dxn_ref[...]
        dg_ref[...] += jnp.sum(dxn_v * xh, axis=0, keepdims=True)
        dxh = dxn_v * g_ref[...]
        dx_ref[...] = dy_ref[...] + r * (dxh - xh * jnp.mean(dxh * xh, axis=-1, keepdims=True))

    row = pl.BlockSpec((tm, D), lambda i: (i, 0))
    vec = pl.BlockSpec((1, D), lambda i: (0, 0))
    return pl.pallas_call(
        body, name=name, grid=(S // tm,),
        in_specs=[row, vec, row, row], out_specs=[row, vec],
        out_shape=[jax.ShapeDtypeStruct((S, D), F32), jax.ShapeDtypeStruct((1, D), F32)],
        compiler_params=_params("arbitrary"),
    )(x, gain, dxn, dy)


def _ffn_fwd(xn, w, name):
    S, D = xn.shape
    F = w.shape[1]
    tm, tf = _tile(S, 1024, SUBLANE_BF16), _tile(F, 512, LANE)
    nf = F // tf

    def body(xn_ref, w_ref, o_ref, gu_ref):
        @pl.when(pl.program_id(1) == 0)
        def _():
            o_ref[...] = jnp.zeros_like(o_ref)

        xnv = xn_ref[...]
        g = _dot(xnv, w_ref[0], "nt")
        u = _dot(xnv, w_ref[1], "nt")
        gu_ref[0] = g.astype(BF16)
        gu_ref[1] = u.astype(BF16)
        h = (g * jax.nn.sigmoid(g) * u).astype(BF16)
        o_ref[...] += _dot(h, w_ref[2], "nn")

    row = pl.BlockSpec((tm, D), lambda i, f: (i, 0))
    return pl.pallas_call(
        body, name=name, grid=(S // tm, nf),
        in_specs=[row, pl.BlockSpec((3, tf, D), lambda i, f: (0, f, 0))],
        out_specs=[row, pl.BlockSpec((2, tm, tf), lambda i, f: (0, i, f))],
        out_shape=[jax.ShapeDtypeStruct((S, D), F32), jax.ShapeDtypeStruct((2, S, F), BF16)],
        compiler_params=_params("parallel", "arbitrary"),
    )(xn, w)


def _ffn_bwd_dx(dy, w, gu, name):
    S, D = dy.shape
    F = w.shape[1]
    tm, tf = _tile(S, 1024, SUBLANE_BF16), _tile(F, 512, LANE)
    nf = F // tf

    def body(dy_ref, w_ref, gu_ref, dxn_ref, dout_ref, t_ref):
        @pl.when(pl.program_id(1) == 0)
        def _():
            dxn_ref[...] = jnp.zeros_like(dxn_ref)
            dout_ref[...] = (0.5 * dy_ref[...]).astype(BF16)

        g = gu_ref[0].astype(F32)
        u = gu_ref[1].astype(F32)
        sig = jax.nn.sigmoid(g)
        silu = g * sig
        t_ref[2] = (silu * u).astype(BF16)
        dh = _dot(dout_ref[...], w_ref[2], "nt")
        dg = (dh * (u * (sig * (1.0 + g * (1.0 - sig))))).astype(BF16)
        du = (dh * silu).astype(BF16)
        t_ref[0] = dg
        t_ref[1] = du
        dxn_ref[...] += _dot(dg, w_ref[0], "nn") + _dot(du, w_ref[1], "nn")

    row = pl.BlockSpec((tm, D), lambda i, f: (i, 0), pipeline_mode=pl.Buffered(1))
    return pl.pallas_call(
        body, name=name, grid=(S // tm, nf),
        in_specs=[row, pl.BlockSpec((3, tf, D), lambda i, f: (0, f, 0)),
                  pl.BlockSpec((2, tm, tf), lambda i, f: (0, i, f))],
        out_specs=[row, row, pl.BlockSpec((3, tm, tf), lambda i, f: (0, i, f))],
        out_shape=[jax.ShapeDtypeStruct((S, D), F32), jax.ShapeDtypeStruct((S, D), BF16),
                   jax.ShapeDtypeStruct((3, S, F), BF16)],
        compiler_params=_params("parallel", "arbitrary"),
    )(dy, w, gu)


def _ffn_bwd_dw(xn, dout, t, name):
    S, D = xn.shape
    F = t.shape[2]
    ts, tf = _tile(S, 1024, LANE), _tile(F, 512, LANE)
    ns = S // ts

    def body(xn_ref, dout_ref, t_ref, dw_ref, acc):
        s = pl.program_id(1)

        @pl.when(s == 0)
        def _():
            acc[...] = jnp.zeros_like(acc)

        xnv = xn_ref[...]
        acc[0] += _dot(t_ref[0], xnv, "tn")
        acc[1] += _dot(t_ref[1], xnv, "tn")
        acc[2] += _dot(t_ref[2], dout_ref[...], "tn")

        @pl.when(s == ns - 1)
        def _():
            dw_ref[...] = acc[...].astype(BF16)

    row = pl.BlockSpec((ts, D), lambda f, s: (s, 0))
    return pl.pallas_call(
        body, name=name, grid=(F // tf, ns),
        in_specs=[row, row, pl.BlockSpec((3, ts, tf), lambda f, s: (0, s, f))],
        out_specs=pl.BlockSpec((3, tf, D), lambda f, s: (0, f, 0)),
        out_shape=jax.ShapeDtypeStruct((3, F, D), BF16),
        scratch_shapes=[pltpu.VMEM((3, tf, D), F32)],
        compiler_params=_params("parallel", "arbitrary"),
    )(xn, dout, t)


def _loss_grad(x, ffn, target, name):
    S, D = x.shape
    tm = _tile(S, 512, 8)

    def body(x_ref, f_ref, t_ref, dy_ref, l_ref):
        @pl.when(pl.program_id(0) == 0)
        def _():
            l_ref[...] = jnp.zeros_like(l_ref)

        e = (x_ref[...] + 0.5 * f_ref[...]) - t_ref[...]
        dy_ref[...] = e * (1.0 / D)
        l_ref[...] += 0.5 * jnp.sum(jnp.sum(e * e, axis=-1, keepdims=True) * (1.0 / D))

    row = pl.BlockSpec((tm, D), lambda i: (i, 0))
    one = pl.BlockSpec((8, LANE), lambda i: (0, 0))
    return pl.pallas_call(
        body, name=name, grid=(S // tm,),
        in_specs=[row, row, row], out_specs=[row, one],
        out_shape=[jax.ShapeDtypeStruct((S, D), F32), jax.ShapeDtypeStruct((8, LANE), F32)],
        compiler_params=_params("arbitrary"),
    )(x, ffn, target)


def _adamw(w, g, m, v, name):
    shape = w.shape
    C = shape[-1]
    R = math.prod(shape[:-1])
    tr = _tile(R, max(8, (1 << 19) // C // 8 * 8), 8)
    c1 = 1.0 / (1.0 - ADAM_B1 ** ADAM_STEP)
    c2 = 1.0 / (1.0 - ADAM_B2 ** ADAM_STEP)

    def body(w_ref, g_ref, m_ref, v_ref, d_ref, nm_ref, nv_ref):
        gv = g_ref[...]
        nm = ADAM_B1 * m_ref[...] + (1.0 - ADAM_B1) * gv
        nv = ADAM_B2 * v_ref[...] + (1.0 - ADAM_B2) * (gv * gv)
        nm_ref[...] = nm
        nv_ref[...] = nv
        d_ref[...] = -ADAM_LR * ((nm * c1) / (jnp.sqrt(nv * c2) + ADAM_EPS) + ADAM_WD * w_ref[...])

    blk = pl.BlockSpec((tr, C), lambda i: (i, 0))
    sds = jax.ShapeDtypeStruct((R, C), F32)
    outs = pl.pallas_call(
        body, name=name, grid=(R // tr,),
        in_specs=[blk] * 4, out_specs=[blk] * 3, out_shape=[sds] * 3,
        compiler_params=_params("parallel"),
    )(*(a.reshape(R, C) for a in (w, g, m, v)))
    return tuple(o.reshape(shape) for o in outs)


def _me():
    return lax.axis_index("x"), lax.axis_index("y"), lax.axis_index("c")


def _add_core_halves(grad, got, c_idx, name):
    n, nk, _, r, C = grad.shape
    tr = _tile(r, 1024, SUBLANE_BF16)

    def body(c_ref, g_ref, r_ref, o_ref):
        o_ref[...] = (g_ref[...].astype(F32) + r_ref[...].astype(F32)).astype(BF16)

    return pl.pallas_call(
        body, name=name,
        grid_spec=pltpu.PrefetchScalarGridSpec(
            num_scalar_prefetch=1, grid=(n, nk, r // tr),
            in_specs=[pl.BlockSpec((None, None, None, tr, C), lambda i, k, t, c: (i, k, c[0], t, 0)),
                      pl.BlockSpec((None, None, tr, C), lambda i, k, t, c: (i, k, t, 0))],
            out_specs=pl.BlockSpec((None, None, tr, C), lambda i, k, t, c: (i, k, t, 0))),
        out_shape=jax.ShapeDtypeStruct((n, nk, r, C), BF16),
        compiler_params=_params("parallel", "parallel", "parallel"),
    )(c_idx, grad, got)


HBM = pl.BlockSpec(memory_space=pltpu.HBM)
SEM = pl.BlockSpec(memory_space=pltpu.SEMAPHORE)
EFFECT = pltpu.SideEffectType.DATAFLOW_SIDE_EFFECTING


def _push_start(srcs, lands, plan, after, name):
    ns, nl = len(srcs), len(lands)
    ncp = len(plan([None] * ns, [None] * nl, dry=True))
    extra = [] if after is None else [after]

    def body(*refs):
        src_refs, land_refs = refs[:ns], refs[ns:ns + nl]
        send_sems, recv_sems = refs[ns + nl + len(extra)], refs[ns + nl + len(extra) + 1]
        token = refs[-1]
        for i, (s, d, to) in enumerate(plan(src_refs, land_refs)):
            pltpu.make_async_remote_copy(src_ref=s, dst_ref=d, send_sem=send_sems.at[i], recv_sem=recv_sems.at[i],
                                         device_id=to, device_id_type=MESH).start()
        token[...] = jnp.zeros_like(token)

    out = pl.pallas_call(
        body, name=name,
        out_shape=(pltpu.SemaphoreType.DMA((ncp,)), pltpu.SemaphoreType.DMA((ncp,)),
                   *[pltpu.HBM(a.shape, a.dtype) for a in srcs], *[pltpu.HBM(a.shape, a.dtype) for a in lands],
                   jax.ShapeDtypeStruct((8, LANE), F32)),
        in_specs=[HBM] * (ns + nl) + [ANY] * len(extra),
        out_specs=(SEM, SEM, *[HBM] * (ns + nl), pl.BlockSpec(memory_space=pltpu.VMEM)),
        input_output_aliases={i: 2 + i for i in range(ns + nl)},
        compiler_params=pltpu.CompilerParams(has_side_effects=EFFECT),
    )(*[pltpu.with_memory_space_constraint(a, pltpu.HBM) for a in srcs + lands], *extra)
    return out[0], out[1], list(out[2:2 + ns]), list(out[2 + ns:2 + ns + nl]), out[-1]


def _push_wait(send_sems, recv_sems, srcs, lands, plan, after, name):
    ns, nl = len(srcs), len(lands)
    after = list(after) if isinstance(after, (list, tuple)) else [after]

    def body(*refs):
        src_refs, land_refs = refs[:ns], refs[ns:ns + nl]
        send, recv = refs[ns + nl], refs[ns + nl + 1]
        for i, (s, d, to) in enumerate(plan(src_refs, land_refs)):
            cp = pltpu.make_async_remote_copy(src_ref=s, dst_ref=d, send_sem=send.at[i], recv_sem=recv.at[i],
                                              device_id=to, device_id_type=MESH)
            cp.wait_send()
            cp.wait_recv()

    out = pl.pallas_call(
        body, name=name,
        out_shape=tuple(pltpu.HBM(a.shape, a.dtype) for a in srcs + lands),
        in_specs=[HBM] * (ns + nl) + [SEM, SEM] + [ANY] * len(after),
        out_specs=tuple([HBM] * (ns + nl)),
        input_output_aliases={i: i for i in range(ns + nl)},
        compiler_params=pltpu.CompilerParams(has_side_effects=EFFECT),
    )(*srcs, *lands, send_sems, recv_sems, *after)
    return list(out[:ns]), list(out[ns:])


def _gather_plan(src_refs, land_refs, dry=False):
    if dry:
        return [None] * (4 * len(src_refs))
    x, y, c = _me()
    me = 4 * x + 2 * y + c
    targets = [(x, y, 1 - c), (1 - x, y, c), (x, 1 - y, c), (1 - x, 1 - y, c)]
    return [(s, l.at[:, me], to) for s, l in zip(src_refs, land_refs) for to in targets]


def _halves_plan(src_refs, land_refs, dry=False):
    if dry:
        return [None] * len(src_refs)
    x, y, c = _me()
    return [(s.at[:, :, 1 - c], l, (x, y, 1 - c)) for s, l in zip(src_refs, land_refs)]


def _chip_plan(src_refs, land_refs, dry=False):
    if dry:
        return [None] * (3 * len(src_refs))
    x, y, c = _me()
    chips = [(1 - x, y), (x, 1 - y), (1 - x, 1 - y)]
    return [(s.at[:, 2 * chip[0] + chip[1]], l.at[j], (*chip, c))
            for s, l in zip(src_refs, land_refs) for j, chip in enumerate(chips)]


def _forward_plan(src_refs, land_refs, dry=False):
    if dry:
        return [None] * (3 * len(land_refs))
    x, y, c = _me()
    chips = [(1 - x, y), (x, 1 - y), (1 - x, 1 - y)]
    plan = []
    for l in land_refs:
        for chip in chips:
            blk = l.at[:, 4 * chip[0] + 2 * chip[1] + c]
            plan.append((blk, blk, (x, y, 1 - c)))
    return plan


def _sum_chip_blocks(sums, got, k_idx, name):
    n, _, r, C = sums.shape
    tr = _tile(r, 512, SUBLANE_BF16)

    def body(k_ref, s_ref, r_ref, o_ref):
        acc = s_ref[...].astype(F32)
        for j in range(3):
            acc = acc + r_ref[j].astype(F32)
        o_ref[...] = acc

    return pl.pallas_call(
        body, name=name,
        grid_spec=pltpu.PrefetchScalarGridSpec(
            num_scalar_prefetch=1, grid=(n, r // tr),
            in_specs=[pl.BlockSpec((None, None, tr, C), lambda i, t, k: (i, k[0], t, 0)),
                      pl.BlockSpec((3, None, tr, C), lambda i, t, k: (0, i, t, 0))],
            out_specs=pl.BlockSpec((None, tr, C), lambda i, t, k: (i, t, 0))),
        out_shape=jax.ShapeDtypeStruct((n, r, C), F32),
        compiler_params=_params("parallel", "parallel"),
    )(k_idx, sums, got)


def _all_reduce_small(v, name, after=()):
    R = v.shape[0]
    after = list(after)

    def body(*refs):
        v_ref = refs[0]
        o_ref, buf, send_sems, recv_sems = refs[1 + len(after):]
        x, y, c = _me()
        me = 4 * x + 2 * y + c
        buf[me] = v_ref[...]
        copies = []
        for k in range(1, N_DEV):
            peer = (x ^ (k >> 2), y ^ ((k >> 1) & 1), c ^ (k & 1))
            copies.append(pltpu.make_async_remote_copy(
                src_ref=v_ref, dst_ref=buf.at[me],
                send_sem=send_sems.at[k - 1], recv_sem=recv_sems.at[k - 1],
                device_id=peer, device_id_type=MESH))
        for cp in copies:
            cp.start()
        for cp in copies:
            cp.wait()
        acc = buf[0]
        for d in range(1, N_DEV):
            acc = acc + buf[d]
        o_ref[...] = acc

    vm = pl.BlockSpec(memory_space=pltpu.VMEM)
    return pl.pallas_call(
        body, name=name, in_specs=[vm] + [ANY] * len(after), out_specs=vm,
        out_shape=jax.ShapeDtypeStruct((R, LANE), F32),
        scratch_shapes=[pltpu.VMEM((N_DEV, R, LANE), F32),
                        pltpu.SemaphoreType.DMA((N_DEV - 1,)), pltpu.SemaphoreType.DMA((N_DEV - 1,))],
        compiler_params=pltpu.CompilerParams(vmem_limit_bytes=VMEM_LIMIT_BYTES),
    )(v, *after)


def _pack_rows(parts):
    flat = jnp.concatenate([p.reshape(-1).astype(F32) for p in parts])
    n = flat.shape[0]
    rows = -(-n // (8 * LANE)) * 8
    flat = jnp.pad(flat, (0, rows * LANE - n))
    return flat.reshape(rows, LANE)


def _unpack_rows(packed, shapes):
    flat = packed.reshape(-1)
    out, off = [], 0
    for s in shapes:
        n = math.prod(s)
        out.append(flat[off:off + n].reshape(s))
        off += n
    return out


CONV_HALO = 32


def _conv_fwd(u, conv_w, conv_b, cn_g, cn_b, name):
    S = u.shape[0]
    C = conv_w.shape[1]
    T = _tile(S, 256, CONV_HALO)
    hb = T // CONV_HALO

    def body(av_ref, ag_ref, pv_ref, pg_ref, w_ref, b_ref, g_ref, bb_ref, out_ref, y_ref, scr):
        i = pl.program_id(0)
        prev = pv_ref[...] * jax.nn.sigmoid(pg_ref[...])
        scr[0:CONV_HALO, :] = jnp.where(i > 0, prev, 0.0)
        scr[CONV_HALO:CONV_HALO + T, :] = av_ref[...] * jax.nn.sigmoid(ag_ref[...])
        for s in range(C // LANE):
            sl = slice(s * LANE, (s + 1) * LANE)
            acc = jnp.broadcast_to(b_ref[:, sl], (T, LANE))
            for j in range(CONV_WIDTH):
                acc = acc + w_ref[j:j + 1, sl] * scr[pl.ds(CONV_HALO - (CONV_WIDTH - 1) + j, T), sl]
            y_ref[:, sl] = acc
        acc = y_ref[...]
        mu = jnp.mean(acc, axis=-1, keepdims=True)
        xc = acc - mu
        var = jnp.mean(xc * xc, axis=-1, keepdims=True)
        ln = xc * lax.rsqrt(var + EPS) * g_ref[...] + bb_ref[...]
        out_ref[...] = (ln * jax.nn.sigmoid(ln)).astype(BF16)

    def cur(cb):
        return pl.BlockSpec((T, C), lambda i: (i, cb))

    def halo(cb):
        return pl.BlockSpec((CONV_HALO, C), lambda i: (jnp.maximum(i * hb - 1, 0), cb))

    vec = pl.BlockSpec((1, C), lambda i: (0, 0))
    return pl.pallas_call(
        body, name=name, grid=(S // T,),
        in_specs=[cur(0), cur(1), halo(0), halo(1), pl.BlockSpec((CONV_WIDTH, C), lambda i: (0, 0)), vec, vec, vec],
        out_specs=[pl.BlockSpec((T, C), lambda i: (i, 0))] * 2,
        out_shape=[jax.ShapeDtypeStruct((S, C), BF16), jax.ShapeDtypeStruct((S, C), F32)],
        scratch_shapes=[pltpu.VMEM((T + CONV_HALO, C), F32)],
        compiler_params=_params("parallel"),
    )(u, u, u, u, conv_w, conv_b, cn_g, cn_b)


def _conv_bwd_norm(dz, y, cn_g, cn_b, name):
    S, C = y.shape
    T = _tile(S, 256, 8)

    def body(dz_ref, y_ref, g_ref, bb_ref, dy_ref, dg_ref, db_ref):
        @pl.when(pl.program_id(0) == 0)
        def _():
            dg_ref[...] = jnp.zeros_like(dg_ref)
            db_ref[...] = jnp.zeros_like(db_ref)

        yv = y_ref[...]
        mu = jnp.mean(yv, axis=-1, keepdims=True)
        xc = yv - mu
        rstd = lax.rsqrt(jnp.mean(xc * xc, axis=-1, keepdims=True) + EPS)
        xh = xc * rstd
        ln = xh * g_ref[...] + bb_ref[...]
        sg = jax.nn.sigmoid(ln)
        dln = dz_ref[...] * (sg * (1.0 + ln * (1.0 - sg)))
        dg_ref[...] += jnp.sum(dln * xh, axis=0, keepdims=True)
        db_ref[...] += jnp.sum(dln, axis=0, keepdims=True)
        dxh = dln * g_ref[...]
        dy_ref[...] = rstd * (dxh - jnp.mean(dxh, axis=-1, keepdims=True)
                              - xh * jnp.mean(dxh * xh, axis=-1, keepdims=True))

    row = pl.BlockSpec((T, C), lambda i: (i, 0))
    vec = pl.BlockSpec((1, C), lambda i: (0, 0))
    return pl.pallas_call(
        body, name=name, grid=(S // T,),
        in_specs=[row, row, vec, vec], out_specs=[row, vec, vec],
        out_shape=[jax.ShapeDtypeStruct((S, C), F32), jax.ShapeDtypeStruct((1, C), F32),
                   jax.ShapeDtypeStruct((1, C), F32)],
        compiler_params=_params("arbitrary"),
    )(dz, y, cn_g, cn_b)


def _conv_bwd_taps(u, dy, conv_w, name):
    S, C = dy.shape
    T = _tile(S, 256, CONV_HALO)
    hb = T // CONV_HALO
    nt = S // T
    ns = C // LANE
    W1 = CONV_WIDTH - 1

    def body(av_ref, ag_ref, pv_ref, pg_ref, dy_ref, dn_ref, w_ref, dv_ref, dg_ref, dw_ref, db_ref, a_scr, d_scr):
        i = pl.program_id(1)

        @pl.when(i == 0)
        def _():
            dw_ref[...] = jnp.zeros_like(dw_ref)
            db_ref[...] = jnp.zeros_like(db_ref)

        av, sg = av_ref[...], jax.nn.sigmoid(ag_ref[...])
        prev = pv_ref[...] * jax.nn.sigmoid(pg_ref[...])
        a_scr[0:CONV_HALO, :] = jnp.where(i > 0, prev, 0.0)
        a_scr[CONV_HALO:CONV_HALO + T, :] = av * sg
        dyv = dy_ref[...]
        d_scr[0:T, :] = dyv
        d_scr[T:T + CONV_HALO, :] = jnp.where(i < nt - 1, dn_ref[...], 0.0)
        da = jnp.zeros((T, LANE), F32)
        for j in range(CONV_WIDTH):
            da = da + w_ref[j:j + 1, :] * d_scr[pl.ds(W1 - j, T), :]
            dw_ref[j:j + 1, :] += jnp.sum(dyv * a_scr[pl.ds(CONV_HALO - W1 + j, T), :], axis=0, keepdims=True)
        db_ref[...] += jnp.sum(dyv, axis=0, keepdims=True)
        dv_ref[...] = (da * sg).astype(BF16)
        dg_ref[...] = (da * av * sg * (1.0 - sg)).astype(BF16)

    def cur(part):
        return pl.BlockSpec((T, LANE), lambda cb, i: (i, part * ns + cb))

    def halo(part):
        return pl.BlockSpec((CONV_HALO, LANE), lambda cb, i: (jnp.maximum(i * hb - 1, 0), part * ns + cb))

    nxt = pl.BlockSpec((CONV_HALO, LANE), lambda cb, i: (jnp.minimum((i + 1) * hb, S // CONV_HALO - 1), cb))
    row = pl.BlockSpec((T, LANE), lambda cb, i: (i, cb))
    return pl.pallas_call(
        body, name=name, grid=(ns, nt),
        in_specs=[cur(0), cur(1), halo(0), halo(1), row, nxt, pl.BlockSpec((CONV_WIDTH, LANE), lambda cb, i: (0, cb))],
        out_specs=[row, row, pl.BlockSpec((CONV_HALO, LANE), lambda cb, i: (0, cb)),
                   pl.BlockSpec((1, LANE), lambda cb, i: (0, cb))],
        out_shape=[jax.ShapeDtypeStruct((S, C), BF16), jax.ShapeDtypeStruct((S, C), BF16),
                   jax.ShapeDtypeStruct((CONV_HALO, C), F32), jax.ShapeDtypeStruct((1, C), F32)],
        scratch_shapes=[pltpu.VMEM((T + CONV_HALO, LANE), F32), pltpu.VMEM((T + CONV_HALO, LANE), F32)],
        compiler_params=_params("parallel", "arbitrary"),
    )(u, u, u, u, dy, dy, conv_w)


def _rope_tables(S):
    half = HEAD_DIM // 2
    inv = jnp.exp(-math.log(ROPE_THETA) * jnp.arange(half, dtype=F32) / half)
    ang = jnp.arange(S, dtype=jnp.int32).astype(F32)[:, None] * inv[None, :]
    cos, sin = jnp.cos(ang), jnp.sin(ang)
    return jnp.concatenate([cos, cos], axis=1), jnp.concatenate([-sin, sin], axis=1)


def _qkv_prep(u, qn_g, kn_g, cos, sin, cb0, name):
    S = u.shape[0]
    A = (u.shape[1] // (cb0 + 3))
    H = A // HEAD_DIM
    T = _tile(S, 256, SUBLANE_BF16)
    scale = HEAD_DIM ** -0.5

    def body(q_ref, k_ref, v_ref, qg_ref, kg_ref, cos_ref, sin_ref, qo_ref, ko_ref, vo_ref):
        cosv, sinv = cos_ref[...], sin_ref[...]
        for h in range(H):
            sl = slice(h * HEAD_DIM, (h + 1) * HEAD_DIM)
            for x_ref, g_ref, o_ref, sc in ((q_ref, qg_ref, qo_ref, scale), (k_ref, kg_ref, ko_ref, 1.0)):
                xv = x_ref[:, sl]
                xn = xv * lax.rsqrt(jnp.mean(xv * xv, axis=-1, keepdims=True) + EPS) * g_ref[...]
                y = xn * cosv + pltpu.roll(xn, HEAD_DIM // 2, 1) * sinv
                o_ref[:, sl] = (y * sc).astype(BF16)
        vo_ref[...] = v_ref[...].astype(BF16)

    def col(cb):
        return pl.BlockSpec((T, A), lambda i: (i, cb))

    vec = pl.BlockSpec((1, HEAD_DIM), lambda i: (0, 0))
    tab = pl.BlockSpec((T, HEAD_DIM), lambda i: (i, 0))
    out = pl.BlockSpec((T, A), lambda i: (i, 0))
    return pl.pallas_call(
        body, name=name, grid=(S // T,),
        in_specs=[col(cb0), col(cb0 + 1), col(cb0 + 2), vec, vec, tab, tab],
        out_specs=[out] * 3, out_shape=[jax.ShapeDtypeStruct((S, A), BF16)] * 3,
        compiler_params=_params("parallel"),
    )(u, u, u, qn_g, kn_g, cos, sin)


def _qkv_prep_bwd(u, dqs, dks, dvs, qn_g, kn_g, cos, sin, cb0, name):
    S = u.shape[0]
    A = dqs[0].shape[1]
    H = A // HEAD_DIM
    T = _tile(S, 256, SUBLANE_BF16)
    nb = len(dqs)
    scale = HEAD_DIM ** -0.5

    def body(*refs):
        q_ref, k_ref, qg_ref, kg_ref, cos_ref, sin_ref = refs[:6]
        dq_refs, dk_refs, dv_refs = refs[6:6 + nb], refs[6 + nb:6 + 2 * nb], refs[6 + 2 * nb:6 + 3 * nb]
        dqo_ref, dko_ref, dvo_ref, dqg_ref, dkg_ref = refs[6 + 3 * nb:]

        @pl.when(pl.program_id(0) == 0)
        def _():
            dqg_ref[...] = jnp.zeros_like(dqg_ref)
            dkg_ref[...] = jnp.zeros_like(dkg_ref)

        cosv, sinv = cos_ref[...], sin_ref[...]
        for h in range(H):
            sl = slice(h * HEAD_DIM, (h + 1) * HEAD_DIM)
            for x_ref, g_ref, d_refs, o_ref, dg_ref, sc in ((q_ref, qg_ref, dq_refs, dqo_ref, dqg_ref, scale),
                                                          (k_ref, kg_ref, dk_refs, dko_ref, dkg_ref, 1.0)):
                dy = d_refs[0][:, sl]
                for r in d_refs[1:]:
                    dy = dy + r[:, sl]
                dy = dy * sc
                dxn = dy * cosv + pltpu.roll(dy * sinv, HEAD_DIM // 2, 1)
                xv = x_ref[:, sl]
                r = lax.rsqrt(jnp.mean(xv * xv, axis=-1, keepdims=True) + EPS)
                xh = xv * r
                dg_ref[...] += jnp.sum(dxn * xh, axis=0, keepdims=True)
                dxh = dxn * g_ref[...]
                o_ref[:, sl] = (r * (dxh - xh * jnp.mean(dxh * xh, axis=-1, keepdims=True))).astype(BF16)
        dv = dv_refs[0][...]
        for r in dv_refs[1:]:
            dv = dv + r[...]
        dvo_ref[...] = dv.astype(BF16)

    def col(cb):
        return pl.BlockSpec((T, A), lambda i: (i, cb))

    vec = pl.BlockSpec((1, HEAD_DIM), lambda i: (0, 0))
    tab = pl.BlockSpec((T, HEAD_DIM), lambda i: (i, 0))
    row = pl.BlockSpec((T, A), lambda i: (i, 0))
    return pl.pallas_call(
        body, name=name, grid=(S // T,),
        in_specs=[col(cb0), col(cb0 + 1), vec, vec, tab, tab] + [row] * (3 * nb),
        out_specs=[row, row, row, vec, vec],
        out_shape=[jax.ShapeDtypeStruct((S, A), BF16)] * 3 + [jax.ShapeDtypeStruct((1, HEAD_DIM), F32)] * 2,
        compiler_params=_params("arbitrary"),
    )(u, u, qn_g, kn_g, cos, sin, *dqs, *dks, *dvs)


ATT_TILE_FWD = 512
ATT_TILE_BWD = 512
NEG = -1e30


def _attn_bias(tile):
    span = max(window for window, _ in DIL_PATTERNS)
    nw = -(-span // tile) + 1
    dist = (jnp.arange(nw)[:, None, None] * tile + jnp.arange(tile)[None, :, None] - jnp.arange(tile)[None, None, :])
    mult = sum(((dist >= 0) & (dist <= window) & (dist % dil == 0)).astype(F32) for window, dil in DIL_PATTERNS)
    return jnp.where(mult > 0, jnp.log(jnp.maximum(mult, 1.0)), NEG)


def _attn_fwd(q, k, v, bias, name):
    S, A = q.shape
    H = A // HEAD_DIM
    nw, T, _ = bias.shape
    nq = S // T

    def body(q_ref, k_ref, v_ref, b_ref, ob_ref, of_ref, l_ref):
        i = pl.program_id(1)
        qv = q_ref[...]
        for w in range(nw):
            blk = i - w
            start = pl.multiple_of(jnp.maximum(blk, 0) * T, T)
            s = _dot(qv, k_ref[pl.ds(start, T), :], "nt") + b_ref[w]
            if w == 0:
                mx = jnp.max(s, axis=-1, keepdims=True)
                p = jnp.exp(s - mx)
                den = jnp.sum(p, axis=-1, keepdims=True)
                o = _dot(p.astype(BF16), v_ref[pl.ds(start, T), :], "nn")
            else:
                s = s + jnp.where(blk >= 0, 0.0, NEG)
                new = jnp.maximum(mx, jnp.max(s, axis=-1, keepdims=True))
                scale = jnp.exp(mx - new)
                p = jnp.exp(s - new)
                den = scale * den + jnp.sum(p, axis=-1, keepdims=True)
                o = scale * o + _dot(p.astype(BF16), v_ref[pl.ds(start, T), :], "nn")
                mx = new
        o = o / den
        ob_ref[...] = o.astype(BF16)
        of_ref[...] = o
        l_ref[...] = mx + jnp.log(den)

    blk = pl.BlockSpec((T, HEAD_DIM), lambda h, i: (i, h))
    full = pl.BlockSpec((S, HEAD_DIM), lambda h, i: (0, h))
    return pl.pallas_call(
        body, name=name, grid=(H, nq),
        in_specs=[blk, full, full, pl.BlockSpec((nw, T, T), lambda h, i: (0, 0, 0))],
        out_specs=[blk, blk, pl.BlockSpec((None, T, 1), lambda h, i: (h, i, 0))],
        out_shape=[jax.ShapeDtypeStruct((S, A), BF16), jax.ShapeDtypeStruct((S, A), F32),
                   jax.ShapeDtypeStruct((H, S, 1), F32)],
        compiler_params=_params("parallel", "arbitrary"),
    )(q, k, v, bias)


def _attn_dq(q, k, v, dz, cb0, o, lse, bias, name):
    S, A = q.shape
    H = A // HEAD_DIM
    nw, T, _ = bias.shape
    nq = S // T

    def body(q_ref, k_ref, v_ref, do_ref, o_ref, l_ref, b_ref, dq_ref, d_ref):
        i = pl.program_id(1)
        qv, dof = q_ref[...], do_ref[...]
        dov = dof.astype(BF16)
        delta = jnp.sum(dof * o_ref[...], axis=-1, keepdims=True)
        d_ref[...] = delta
        lv = l_ref[...]
        dq = jnp.zeros((T, HEAD_DIM), F32)
        for w in range(nw):
            blk = i - w
            start = pl.multiple_of(jnp.maximum(blk, 0) * T, T)
            kv = k_ref[pl.ds(start, T), :]
            s = _dot(qv, kv, "nt") + b_ref[w] + jnp.where(blk >= 0, 0.0, NEG)
            p = jnp.exp(s - lv)
            ds = (p * (_dot(dov, v_ref[pl.ds(start, T), :], "nt") - delta)).astype(BF16)
            dq = dq + _dot(ds, kv, "nn")
        dq_ref[...] = dq

    blk = pl.BlockSpec((T, HEAD_DIM), lambda h, i: (i, h))
    full = pl.BlockSpec((S, HEAD_DIM), lambda h, i: (0, h))
    col = pl.BlockSpec((None, T, 1), lambda h, i: (h, i, 0))
    return pl.pallas_call(
        body, name=name, grid=(H, nq),
        in_specs=[blk, full, full, pl.BlockSpec((T, HEAD_DIM), lambda h, i: (i, cb0 + h)), blk, col,
                  pl.BlockSpec((nw, T, T), lambda h, i: (0, 0, 0))],
        out_specs=[blk, col],
        out_shape=[jax.ShapeDtypeStruct((S, A), F32), jax.ShapeDtypeStruct((H, S, 1), F32)],
        compiler_params=_params("parallel", "arbitrary"),
    )(q, k, v, dz, o, lse, bias)


def _attn_dkv(q, k, v, dz, cb0, lse, delta, bias, name):
    S, A = q.shape
    H = A // HEAD_DIM
    nw, T, _ = bias.shape
    nq = S // T

    def body(k_ref, v_ref, q_ref, do_ref, l_ref, d_ref, b_ref, dk_ref, dv_ref):
        m = pl.program_id(1)
        kv, vv = k_ref[...], v_ref[...]
        dk = jnp.zeros((T, HEAD_DIM), F32)
        dv = jnp.zeros((T, HEAD_DIM), F32)
        for w in range(nw):
            blk = m + w
            start = pl.multiple_of(jnp.minimum(blk, nq - 1) * T, T)
            qv = q_ref[pl.ds(start, T), :]
            dov = do_ref[pl.ds(start, T), :].astype(BF16)
            s = _dot(qv, kv, "nt") + b_ref[w] + jnp.where(blk < nq, 0.0, NEG)
            p = jnp.exp(s - l_ref[pl.ds(start, T), :])
            dv = dv + _dot(p.astype(BF16), dov, "tn")
            ds = (p * (_dot(dov, vv, "nt") - d_ref[pl.ds(start, T), :])).astype(BF16)
            dk = dk + _dot(ds, qv, "tn")
        dk_ref[...] = dk
        dv_ref[...] = dv

    blk = pl.BlockSpec((T, HEAD_DIM), lambda h, m: (m, h))
    full = pl.BlockSpec((S, HEAD_DIM), lambda h, m: (0, h))
    col = pl.BlockSpec((None, S, 1), lambda h, m: (h, 0, 0))
    sds = jax.ShapeDtypeStruct((S, A), F32)
    return pl.pallas_call(
        body, name=name, grid=(H, nq),
        in_specs=[blk, blk, full, pl.BlockSpec((S, HEAD_DIM), lambda h, m: (0, cb0 + h)), col, col,
                  pl.BlockSpec((nw, T, T), lambda h, m: (0, 0, 0))],
        out_specs=[blk, blk], out_shape=[sds, sds],
        compiler_params=_params("parallel", "arbitrary"),
    )(k, v, q, dz, lse, delta, bias)


def _even_mixer(u, conv_w, conv_b, cn_g, cn_b, qn_g, kn_g, tag):
    S = u.shape[0]
    C = conv_w.shape[1]
    A = (u.shape[1] - 2 * C) // 3
    assert A == C, "column-block addressing of u assumes equal conv and attention widths"
    cos, sin = _rope_tables(S)
    bias = _attn_bias(_tile(S, ATT_TILE_BWD, LANE))
    a_out, y = _conv_fwd(u, conv_w, conv_b, cn_g, cn_b, "conv_fwd" + tag)
    q, k, v = _qkv_prep(u, qn_g, kn_g, cos, sin, 2, "qkv_prep" + tag)
    ob, of, lse = _attn_fwd(q, k, v, _attn_bias(_tile(S, ATT_TILE_FWD, LANE)), "attn_fwd" + tag)
    z = jnp.concatenate([a_out, ob], axis=1)

    def backward(dz):
        dy, d_cn_g, d_cn_b = _conv_bwd_norm(dz, y, cn_g, cn_b, "conv_bwd_norm" + tag)
        d_val, d_gate, d_w, d_b = _conv_bwd_taps(u, dy, conv_w, "conv_bwd_taps" + tag)
        dqp, delta = _attn_dq(q, k, v, dz, C // HEAD_DIM, of, lse, bias, "attn_dq" + tag)
        dkp, dvp = _attn_dkv(q, k, v, dz, C // HEAD_DIM, lse, delta, bias, "attn_dkv" + tag)
        dq, dk, dv, d_qn, d_kn = _qkv_prep_bwd(u, [dqp], [dkp], [dvp], qn_g, kn_g, cos, sin, 2, "qkv_prep_bwd" + tag)
        du = jnp.concatenate([d_val, d_gate, dq, dk, dv], axis=1)
        return du, [d_w[:CONV_WIDTH], d_b[0], d_cn_g[0], d_cn_b[0], d_qn[0], d_kn[0]]

    return z, backward


_LEVELS = (128, 64, 32, 16, 8, 4, 2, 1)


def _chunk_cumsum(g, rows, reverse=False):
    C = g.shape[0]
    d = 1
    while d < C:
        if reverse:
            g = g + jnp.where(rows < C - d, pltpu.roll(g, C - d, 0), 0.0)
        else:
            g = g + jnp.where(rows >= d, pltpu.roll(g, d, 0), 0.0)
        d *= 2
    return g


def _level_ref(b, b_scr, rows, m):
    C = b.shape[0]
    if m >= 8:
        pieces = [jnp.broadcast_to(b_scr[2 * m * j + m - 1:2 * m * j + m, :], (2 * m, LANE)) for j in range(C // (2 * m))]
        return pieces[0] if len(pieces) == 1 else jnp.concatenate(pieces, axis=0)
    pos = rows & (2 * m - 1)
    ref = b
    for p in range(2 * m):
        if p != m - 1:
            ref = jnp.where(pos == p, pltpu.roll(b, (p - (m - 1)) % C, 0), ref)
    return ref


def _level_operands(q, k, b, b_scr, rows, m):
    e = jnp.exp(-jnp.abs(b - _level_ref(b, b_scr, rows, m)))
    return (q * e).astype(BF16), (k * e).astype(BF16)


def _split2(x):
    hi = x.astype(BF16)
    lo = (x - hi.astype(F32)).astype(BF16)
    return jnp.concatenate([hi, lo], axis=1)


def _level_table(n):
    t = jnp.arange(n, dtype=jnp.int32)[:, None]
    s = jnp.arange(n, dtype=jnp.int32)[None, :]
    x = t ^ s
    lvl = sum((x >= (1 << j)).astype(jnp.int32) for j in range(1, n.bit_length()))
    return jnp.where(t > s, lvl, jnp.where(t == s, -1, -2))


def _hgrn_gates(qz, fz, la, lc, oml):
    sq = jax.nn.sigmoid(qz)
    q = qz * sq
    s = jax.nn.sigmoid(fz)
    c = lc + jnp.minimum(fz, 0.0) - jnp.log(1.0 + jnp.exp(-jnp.abs(fz)))
    mx = jnp.maximum(la, c)
    g = mx + jnp.log(1.0 + jnp.exp(-jnp.abs(la - c)))
    k = oml * (1.0 - s)
    return q, sq, k, s, g, c


def _hgrn_fwd(u, la, lc, oml, gn_g, name):
    S = u.shape[0]
    W = u.shape[1] // 4
    H = W // HGRN_KDIM
    C = min(HGRN_CHUNK, S)
    nc = S // C
    levels = [m for m in _LEVELS if m < C]
    HB = C // 2

    def body(qz_ref, fz_ref, iz_ref, gz_ref, la_ref, lc_ref, oml_ref, gn_ref, lvl_ref,
             z_ref, o_ref, a_ref, st_ref, state, b_scr):
        @pl.when(pl.program_id(1) == 0)
        def _():
            state[...] = jnp.zeros_like(state)

        rows = lax.broadcasted_iota(jnp.int32, (C, LANE), 0)
        q, _, k, _, g, _ = _hgrn_gates(qz_ref[...], fz_ref[...], la_ref[...], lc_ref[...], oml_ref[...])
        v = iz_ref[...].astype(BF16)
        b = _chunk_cumsum(g, rows)
        b_scr[...] = b
        lvl = lvl_ref[...]
        qk = jnp.sum(q * k, axis=-1, keepdims=True)
        diag = [jnp.where(lvl == -1, qk[r * HB:(r + 1) * HB], 0.0) for r in range(2)]
        for m in levels[1:]:
            qs, ks = _level_operands(q, k, b, b_scr, rows, m)
            for r in range(2):
                sl = slice(r * HB, (r + 1) * HB)
                diag[r] = jnp.where(lvl == m.bit_length() - 1, _dot(qs[sl], ks[sl], "nt"), diag[r])
        qs, ks = _level_operands(q, k, b, b_scr, rows, HB)
        low = _dot(qs[HB:], ks[:HB], "nt")
        a = jnp.concatenate([jnp.concatenate([diag[0], jnp.zeros((HB, HB), F32)], axis=1),
                             jnp.concatenate([low, diag[1]], axis=1)], axis=0)
        ab = a.astype(BF16)
        a_ref[...] = ab
        st = state[...]
        st_ref[...] = st
        o = _dot(ab, v, "nn") + _dot((q * jnp.exp(b)).astype(BF16), st.astype(BF16), "nt")
        bl = b_scr[C - 1:C, :]
        kh = (k * jnp.exp(bl - b)).astype(BF16)
        state[...] = st * jnp.exp(bl) + _dot(v, kh, "tn")
        o_ref[...] = o
        r = lax.rsqrt(jnp.mean(o * o, axis=-1, keepdims=True) + EPS)
        gz = gz_ref[...]
        z_ref[...] = (o * r * gn_ref[...] * (gz * jax.nn.sigmoid(gz))).astype(BF16)

    def col(off):
        return pl.BlockSpec((C, LANE), lambda h, i: (i, off * H + h))

    vec = pl.BlockSpec((1, LANE), lambda h, i: (0, h))
    tile = pl.BlockSpec((C, LANE), lambda h, i: (i, h))
    return pl.pallas_call(
        body, name=name, grid=(H, nc),
        in_specs=[col(0), col(1), col(2), col(3), vec, vec, vec, vec, pl.BlockSpec((HB, HB), lambda h, i: (0, 0))],
        out_specs=[tile, tile, pl.BlockSpec((None, C, C), lambda h, i: (h, i, 0)),
                   pl.BlockSpec((None, None, LANE, LANE), lambda h, i: (h, i, 0, 0))],
        out_shape=[jax.ShapeDtypeStruct((S, W), BF16), jax.ShapeDtypeStruct((S, W), F32),
                   jax.ShapeDtypeStruct((H, S, C), BF16), jax.ShapeDtypeStruct((H, nc, LANE, LANE), F32)],
        scratch_shapes=[pltpu.VMEM((LANE, LANE), F32), pltpu.VMEM((C, LANE), F32)],
        compiler_params=_params("parallel", "arbitrary"),
    )(u, u, u, u, la, lc, oml, gn_g, _level_table(HB))


def _hgrn_bwd(u, la, lc, oml, gn_g, o, a, st, dz, name):
    S = u.shape[0]
    W = u.shape[1] // 4
    H = W // HGRN_KDIM
    C = min(HGRN_CHUNK, S)
    nc = S // C
    levels = [m for m in _LEVELS if m < C]
    HB = C // 2

    def body(qz_ref, fz_ref, iz_ref, gz_ref, la_ref, lc_ref, oml_ref, gn_ref, o_ref, a_ref, st_ref, dz_ref, lvl_ref,
             dqz_ref, dfz_ref, diz_ref, dgz_ref, dla_ref, dlc_ref, doml_ref, dgn_ref, dstate, b_scr):
        @pl.when(pl.program_id(1) == 0)
        def _():
            dstate[...] = jnp.zeros_like(dstate)
            dla_ref[...] = jnp.zeros_like(dla_ref)
            dlc_ref[...] = jnp.zeros_like(dlc_ref)
            doml_ref[...] = jnp.zeros_like(doml_ref)
            dgn_ref[...] = jnp.zeros_like(dgn_ref)

        rows = lax.broadcasted_iota(jnp.int32, (C, LANE), 0)
        la_v, lc_v, oml_v = la_ref[...], lc_ref[...], oml_ref[...]
        qz, fz = qz_ref[...], fz_ref[...]
        q, sq, k, s, g, c = _hgrn_gates(qz, fz, la_v, lc_v, oml_v)
        vf = iz_ref[...]
        v = vf.astype(BF16)

        ov, gz, dzv, gn = o_ref[...], gz_ref[...], dz_ref[...], gn_ref[...]
        r = lax.rsqrt(jnp.mean(ov * ov, axis=-1, keepdims=True) + EPS)
        on = ov * r
        sg = jax.nn.sigmoid(gz)
        silu_g = gz * sg
        dgn_ref[...] += jnp.sum(dzv * on * silu_g, axis=0, keepdims=True)
        dgz_ref[...] = (dzv * on * gn * (sg * (1.0 + gz * (1.0 - sg)))).astype(BF16)
        don = dzv * gn * silu_g
        do_f = r * (don - on * jnp.mean(don * on, axis=-1, keepdims=True))
        do = do_f.astype(BF16)

        b = _chunk_cumsum(g, rows)
        b_scr[...] = b
        bl = b_scr[C - 1:C, :]
        e = jnp.exp(b)
        ebl = jnp.exp(bl)
        ekl = jnp.exp(bl - b)
        qh = q * e
        kh = k * ekl
        st_v = st_ref[...]
        dst = dstate[...]
        dstb = dst.astype(BF16)

        diz_ref[...] = (_dot(a_ref[...], do, "tn") + _dot(kh.astype(BF16), dstb, "nt")).astype(BF16)
        da = _dot(do, v, "nt")
        dqh = _dot(do, st_v.astype(BF16), "nn")
        dkh = _dot(v, dstb, "nn")
        dstate[...] = dst * ebl + _dot(do, qh.astype(BF16), "tn")
        dbl = jnp.sum(dkh * kh, axis=0, keepdims=True) + jnp.sum(dst * st_v, axis=0, keepdims=True) * ebl

        datt = jnp.sum(do_f * vf, axis=-1, keepdims=True)
        dqa = datt * k
        dka = datt * q
        lvl = lvl_ref[...]
        for m in levels:
            ez = jnp.exp(-jnp.abs(b - _level_ref(b, b_scr, rows, m)))
            ks2, qs2 = _split2(k * ez), _split2(q * ez)
            if m == HB:
                gm = da[HB:, :HB].astype(BF16)
                pq = jnp.concatenate([jnp.zeros((HB, 2 * LANE), F32), _dot(gm, ks2[:HB], "nn")], axis=0)
                pk = jnp.concatenate([_dot(gm, qs2[HB:], "tn"), jnp.zeros((HB, 2 * LANE), F32)], axis=0)
            else:
                gms = [jnp.where(lvl == m.bit_length() - 1, da[r * HB:(r + 1) * HB, r * HB:(r + 1) * HB], 0.0).astype(BF16)
                       for r in range(2)]
                pq = jnp.concatenate([_dot(gms[r], ks2[r * HB:(r + 1) * HB], "nn") for r in range(2)], axis=0)
                pk = jnp.concatenate([_dot(gms[r], qs2[r * HB:(r + 1) * HB], "tn") for r in range(2)], axis=0)
            dqa += (pq[:, :LANE] + pq[:, LANE:]) * ez
            dka += (pk[:, :LANE] + pk[:, LANE:]) * ez
        db = q * dqa - k * dka + dqh * qh - dkh * kh
        db = db + jnp.where(rows == C - 1, dbl, 0.0)
        dq = dqa + dqh * e
        dk = dka + dkh * ekl
        dg = _chunk_cumsum(db, rows, reverse=True)

        wa = jnp.exp(la_v - g)
        wc = jnp.exp(c - g)
        dqz_ref[...] = (dq * (sq * (1.0 + qz * (1.0 - sq)))).astype(BF16)
        dfz_ref[...] = (dg * wc * (1.0 - s) - dk * oml_v * s * (1.0 - s)).astype(BF16)
        dla_ref[...] += jnp.sum(dg * wa, axis=0, keepdims=True)
        dlc_ref[...] += jnp.sum(dg * wc, axis=0, keepdims=True)
        doml_ref[...] += jnp.sum(dk * (1.0 - s), axis=0, keepdims=True)

    def col(off):
        return pl.BlockSpec((C, LANE), lambda h, i: (nc - 1 - i, off * H + h))

    vec = pl.BlockSpec((1, LANE), lambda h, i: (0, h))
    tile = pl.BlockSpec((C, LANE), lambda h, i: (nc - 1 - i, h))
    a_spec = pl.BlockSpec((None, C, C), lambda h, i: (h, nc - 1 - i, 0))
    st_spec = pl.BlockSpec((None, None, LANE, LANE), lambda h, i: (h, nc - 1 - i, 0, 0))
    sw = jax.ShapeDtypeStruct((S, W), BF16)
    vw = jax.ShapeDtypeStruct((1, W), F32)
    return pl.pallas_call(
        body, name=name, grid=(H, nc),
        in_specs=[col(0), col(1), col(2), col(3), vec, vec, vec, vec, tile, a_spec, st_spec, tile,
                  pl.BlockSpec((HB, HB), lambda h, i: (0, 0))],
        out_specs=[tile, tile, tile, tile, vec, vec, vec, vec],
        out_shape=[sw, sw, sw, sw, vw, vw, vw, vw],
        scratch_shapes=[pltpu.VMEM((LANE, LANE), F32), pltpu.VMEM((C, LANE), F32)],
        compiler_params=_params("parallel", "arbitrary"),
    )(u, u, u, u, la, lc, oml, gn_g, o, a, st, dz, _level_table(HB))


def _lb_terms(lb_logits, layer):
    p = jax.nn.softmax(lb_logits, axis=0)
    lb = (jnp.cumsum(p, axis=0) - p[0:1])[layer]
    return jnp.log(lb)[None], jnp.log1p(-lb)[None], (1.0 - lb)[None]


def kernel(x, norm_ffn1, ffn1_wg, ffn1_wu, ffn1_wd, norm_mix, norm_ffn2, ffn2_wg, ffn2_wu, ffn2_wd, ev_w_in, ev_conv_w, ev_conv_b, ev_cn_g, ev_cn_b, ev_qn_g, ev_kn_g, ev_w_out, od_w_in, od_lb_logits, od_gn_g, od_w_out, loss_target, m_norm_ffn1, m_ffn1_wg, m_ffn1_wu, m_ffn1_wd, m_norm_mix, m_norm_ffn2, m_ffn2_wg, m_ffn2_wu, m_ffn2_wd, m_ev_w_in, m_ev_conv_w, m_ev_conv_b, m_ev_cn_g, m_ev_cn_b, m_ev_qn_g, m_ev_kn_g, m_ev_w_out, m_od_w_in, m_od_lb_logits, m_od_gn_g, m_od_w_out, v_norm_ffn1, v_ffn1_wg, v_ffn1_wu, v_ffn1_wd, v_norm_mix, v_norm_ffn2, v_ffn2_wg, v_ffn2_wu, v_ffn2_wd, v_ev_w_in, v_ev_conv_w, v_ev_conv_b, v_ev_cn_g, v_ev_cn_b, v_ev_qn_g, v_ev_kn_g, v_ev_w_out, v_od_w_in, v_od_lb_logits, v_od_gn_g, v_od_w_out):
    depth = norm_ffn1.shape[0]
    S, D = x.shape[1], x.shape[2]
    xi, yi, ci = _me()
    dev = 4 * xi + 2 * yi + ci
    c_idx = jnp.reshape(ci, (1,)).astype(jnp.int32)
    k_idx = jnp.reshape(2 * xi + yi, (1,)).astype(jnp.int32)

    def ffn_shard(wg, wu, wd, l):
        return jnp.stack([wg[l].T, wu[l].T, wd[l]]).astype(BF16)

    assert depth == 2, "the exchange schedule below is written for one even and one odd layer"
    sh_ffn1 = [ffn_shard(ffn1_wg, ffn1_wu, ffn1_wd, l) for l in range(depth)]
    sh_ffn2 = [ffn_shard(ffn2_wg, ffn2_wu, ffn2_wd, l) for l in range(depth)]
    sh_ev = [ev_w_in[0].T.astype(BF16)[None], ev_w_out[0].astype(BF16)[None]]
    sh_od = [od_w_in[0].T.astype(BF16)[None], od_w_out[0].astype(BF16)[None]]

    def full(g):
        return g.reshape(g.shape[0], N_DEV * g.shape[2], g.shape[3])

    def gather_begin(shards, after, tag):
        lands = [lax.dynamic_update_slice(lax.empty((s.shape[0], N_DEV) + s.shape[1:], s.dtype), s[:, None],
                                          (0, dev, 0, 0)) for s in shards]
        state = _push_start(shards, lands, _gather_plan, after, "gather_start" + tag)
        return state, state[4][0, 0]

    def gather_arrived(state, after, tag):
        send, recv, srcs, lands, _ = state
        _, lands = _push_wait(send, recv, srcs, lands, _gather_plan, after, "gather_wait" + tag)
        state = _push_start([], lands, _forward_plan, None, "forward_start" + tag)
        return state, state[4][0, 0]

    def gather_done(state, after, tag):
        send, recv, _, lands, _ = state
        _, lands = _push_wait(send, recv, [], lands, _forward_plan, after, "forward_wait" + tag)
        return [full(g) for g in lands]

    def gather_end(state, after, tag):
        state, _ = gather_arrived(state, after, tag)
        return gather_done(state, state[4], tag)

    conv_w_sh, gn_g_sh = ev_conv_w[0], od_gn_g[0]
    cw, cs = conv_w_sh.shape[0], conv_w_sh.shape[1]
    gs = gn_g_sh.shape[0]
    conv_w_z = lax.dynamic_update_slice(jnp.zeros((cw, N_DEV * cs), F32), conv_w_sh, (0, dev * cs))
    gn_g_z = lax.dynamic_update_slice(jnp.zeros((N_DEV * gs,), F32), gn_g_sh, (dev * gs,))
    conv_w_full, gn_g_full = _unpack_rows(
        _all_reduce_small(_pack_rows([conv_w_z, gn_g_z]), "gather_small_params"),
        [conv_w_z.shape, gn_g_z.shape])

    w_ffn1, w_ffn2 = [None] * depth, [None] * depth
    pending, after = {}, conv_w_full
    for key, shards in (("0", [sh_ffn1[0]]), ("1", sh_ev), ("2", [sh_ffn2[0]]), ("3", [sh_ffn1[1]]), ("4", sh_od),
                        ("5", [sh_ffn2[1]])):
        pending[key], _ = gather_begin(shards, after, "_" + key)
        after = pending[key][4]
    start_tok = after[0, 0]
    (w_ffn1[0],) = gather_end(pending.pop("0"), after, "_0")

    def odd_mixer(u, l, tok):
        (la, lc, oml), lb_vjp = jax.vjp(functools.partial(_lb_terms, layer=l), od_lb_logits)
        gn = (gn_g_full + tok)[None]
        zb, o_raw, scores, states = _hgrn_fwd(u, la, lc, oml, gn, f"hgrn_fwd{l}")

        def backward(dz):
            dqz, dfz, diz, dgz, dla, dlc, doml, dgn = _hgrn_bwd(
                u, la, lc, oml, gn, o_raw, scores, states, dz, f"hgrn_bwd{l}")
            (g_lb,) = lb_vjp((dla, dlc, doml))
            return jnp.concatenate([dqz, dfz, diz, dgz], axis=1), [g_lb, dgn[0]]

        return zb, backward

    saved = []
    h = x[0]
    hn = _rms_fwd(h, (norm_ffn1[0] + start_tok)[None], "rms_a0")
    for l in range(depth):
        ffn, gu = _ffn_fwd(hn, w_ffn1[l], f"ffn_fwd_a{l}")
        s1 = (h, hn, gu)
        if l == 0:
            w_in, w_out = gather_end(pending.pop("1"), ffn, "_1")
        else:
            w_in, w_out = gather_done(pending.pop("4"), ffn, "_4")
        w_in_t, w_out = w_in[0], w_out[0]
        h, hn = _resid_rms(h, ffn, norm_mix[l][None], f"rms_mix{l}")
        u = _mm(hn, w_in_t, "nt", F32, f"mix_in{l}")
        key = "2" if l == 0 else "5"
        passing, tok = gather_arrived(pending.pop(key), u, "_" + key)
        if l % 2 == 0:
            zb, core_vjp = _even_mixer(u, conv_w_full, ev_conv_b + tok, ev_cn_g, ev_cn_b, ev_qn_g, ev_kn_g, str(l))
        else:
            zb, core_vjp = odd_mixer(u, l, tok)
        h_mix = h
        h = _mm(zb, w_out, "nn", F32, f"mix_out{l}", res=h)
        sm = (h_mix, hn, zb, core_vjp, w_in_t, w_out)
        (w_ffn2[l],) = gather_done(passing, h, "_" + key)
        tok = 0.0
        if l + 1 < depth:
            passing, tok = gather_arrived(pending.pop("3"), w_ffn2[l], "_3")
        hn = _rms_fwd(h, (norm_ffn2[l] + tok)[None], f"rms_b{l}")
        ffn, gu = _ffn_fwd(hn, w_ffn2[l], f"ffn_fwd_b{l}")
        saved.append((s1, sm, (h, hn, gu)))
        if l + 1 < depth:
            (w_ffn1[l + 1],) = gather_done(passing, ffn, "_3")
            pending["4"], tok = gather_arrived(pending.pop("4"), w_ffn1[l + 1], "_4")
            h, hn = _resid_rms(h, ffn, (norm_ffn1[l + 1] + tok)[None], f"rms_a{l + 1}")

    dy, loss_part = _loss_grad(h, ffn, loss_target[0], "loss_grad")

    def halves_begin(parts, tag):
        parts = [g.reshape(g.shape[0], 4, 2, g.shape[1] // N_DEV, g.shape[2]) for g in parts]
        lands = [lax.empty(g.shape[:2] + g.shape[3:], BF16) for g in parts]
        state = _push_start(parts, lands, _halves_plan, None, "halves_start" + tag)
        return state, state[4][0, 0]

    def chips_begin(state, after, tag):
        send, recv, srcs, lands, _ = state
        parts, got = _push_wait(send, recv, srcs, lands, _halves_plan, after, "halves_wait" + tag)
        sums = [_add_core_halves(g, r, c_idx, f"add_core_halves{tag}_{a}") for a, (g, r) in enumerate(zip(parts, got))]
        lands = [lax.empty((3, s.shape[0]) + s.shape[2:], BF16) for s in sums]
        state = _push_start(sums, lands, _chip_plan, None, "reduce_start" + tag)
        return state, state[4][0, 0]

    def reduce_end(state, after, tag):
        send, recv, srcs, lands, _ = state
        sums, got = _push_wait(send, recv, srcs, lands, _chip_plan, after, "reduce_wait" + tag)
        return [_sum_chip_blocks(s, r, k_idx, f"sum_chip_blocks{tag}_{a}") for a, (s, r) in enumerate(zip(sums, got))]

    def ffn_backward(dy, gain, w, sv, tag, on_dw, on_dx=None):
        h_in, hn, gu = sv
        dxn, dout, t = _ffn_bwd_dx(dy, w, gu, "ffn_bwd_dx_" + tag)
        tok = 0.0 if on_dx is None else on_dx(dxn)
        tok = tok + on_dw(_ffn_bwd_dw(hn, dout, t, "ffn_bwd_dw_" + tag))
        dx, dgain = _rms_bwd(h_in, (gain + tok)[None], dxn, dy, "rms_bwd_" + tag)
        return dx, dgain[0]

    g_norm1, g_norm2, g_normm = [None] * depth, [None] * depth, [None] * depth
    small, halves, groups = [None, None], {}, {}

    def start_halves(key, make_parts):
        def hook(dw):
            halves[key], tok = halves_begin(make_parts(dw), "_" + key)
            return tok
        return hook

    def start_chips(key):
        def hook(after):
            groups[key], tok = chips_begin(halves.pop(key), after, "_" + key)
            return tok
        return hook

    for l in reversed(range(depth)):
        s1, (h_mix, hn, zb, core_vjp, w_in_t, w_out), s2 = saved[l]
        if l == 1:
            dy, g_norm2[l] = ffn_backward(dy, norm_ffn2[l], w_ffn2[l], s2, f"b{l}", start_halves("1", lambda dw: [dw]))
        else:
            dy, g_norm2[l] = ffn_backward(dy, norm_ffn2[l], w_ffn2[l], s2, f"b{l}", start_halves("3", lambda dw: [dw]),
                                          start_chips("2"))
        dyb = dy.astype(BF16)
        dz = _mm(dyb, w_out, "nt", F32, f"mix_out_dz{l}")
        dw_out = _mm(zb, dyb, "tn", BF16, f"mix_out_dw{l}")
        dub, small[l % 2] = core_vjp(dz)
        dw_in_t = _mm(dub, hn, "tn", BF16, f"mix_in_dw{l}")
        mix_parts = [dw_in_t[None], dw_out[None]]
        if l == 1:
            tok = start_chips("1")(dw_in_t)
        else:
            tok = start_chips("3")(dw_in_t) + start_halves("4", lambda _: mix_parts)(None)
        dhn = _mm(dub, w_in_t, "nn", F32, f"mix_in_dx{l}")
        dy, gm = _rms_bwd(h_mix, (norm_mix[l] + tok)[None], dhn, dy, f"rms_bwd_mix{l}")
        g_normm[l] = gm[0]
        if l == 1:
            dy, g_norm1[l] = ffn_backward(dy, norm_ffn1[l], w_ffn1[l], s1, f"a{l}",
                                          start_halves("2", lambda dw, od=mix_parts: od + [dw]))
        else:
            dy, g_norm1[l] = ffn_backward(dy, norm_ffn1[l], w_ffn1[l], s1, f"a{l}", start_halves("5", lambda dw: [dw]),
                                          start_chips("4"))
    grad_x = dy[None]
    start_chips("5")(dy)

    done = [dy, groups["5"][4]]
    (g_ffn2_1,) = reduce_end(groups["1"], done, "_1")
    g_od_in_t, g_od_out, g_ffn1_1 = reduce_end(groups["2"], done, "_2")
    (g_ffn2_0,) = reduce_end(groups["3"], done, "_3")
    g_ev_in_t, g_ev_out = reduce_end(groups["4"], done, "_4")
    g_ffn2 = [g_ffn2_0, g_ffn2_1]

    def ffn_grads(gl):
        return (jnp.stack([g[0].T for g in gl]), jnp.stack([g[1].T for g in gl]), jnp.stack([g[2] for g in gl]))

    g_ffn2_wg, g_ffn2_wu, g_ffn2_wd = ffn_grads(g_ffn2)
    grads = [None, None, None, None, None, None, g_ffn2_wg, g_ffn2_wu, g_ffn2_wd,
             g_ev_in_t[0].T[None], None, None, None, None, None,
             None, g_ev_out, g_od_in_t[0].T[None], None, None, g_od_out]
    weights = [norm_ffn1, ffn1_wg, ffn1_wu, ffn1_wd, norm_mix, norm_ffn2, ffn2_wg, ffn2_wu, ffn2_wd, ev_w_in,
               ev_conv_w, ev_conv_b, ev_cn_g, ev_cn_b, ev_qn_g, ev_kn_g, ev_w_out, od_w_in, od_lb_logits,
               od_gn_g, od_w_out]
    moms = [m_norm_ffn1, m_ffn1_wg, m_ffn1_wu, m_ffn1_wd, m_norm_mix, m_norm_ffn2, m_ffn2_wg, m_ffn2_wu,
            m_ffn2_wd, m_ev_w_in, m_ev_conv_w, m_ev_conv_b, m_ev_cn_g, m_ev_cn_b, m_ev_qn_g, m_ev_kn_g,
            m_ev_w_out, m_od_w_in, m_od_lb_logits, m_od_gn_g, m_od_w_out]
    vars_ = [v_norm_ffn1, v_ffn1_wg, v_ffn1_wu, v_ffn1_wd, v_norm_mix, v_norm_ffn2, v_ffn2_wg, v_ffn2_wu,
             v_ffn2_wd, v_ev_w_in, v_ev_conv_w, v_ev_conv_b, v_ev_cn_g, v_ev_cn_b, v_ev_qn_g, v_ev_kn_g,
             v_ev_w_out, v_od_w_in, v_od_lb_logits, v_od_gn_g, v_od_w_out]
    n_w = len(weights)
    deltas, new_m, new_v = [None] * n_w, [None] * n_w, [None] * n_w

    def update(idx):
        for i in idx:
            deltas[i], new_m[i], new_v[i] = _adamw(weights[i], grads[i], moms[i], vars_[i], f"adamw{i}")

    update([i for i in range(n_w) if grads[i] is not None])
    g_conv_w, g_conv_b, g_cn_g, g_cn_b, g_qn_g, g_kn_g = small[0]
    g_lb, g_gn = small[1]
    parts = [jnp.stack(g_norm1), jnp.stack(g_normm), jnp.stack(g_norm2), g_conv_b, g_cn_g, g_cn_b,
             g_qn_g, g_kn_g, g_lb, g_conv_w, g_gn, loss_part[0, :1]]
    red = _unpack_rows(_all_reduce_small(_pack_rows(parts), "reduce_small_grads", [d for d in deltas if d is not None]),
                       [p.shape for p in parts])
    g_norm1, g_normm, g_norm2, g_conv_b, g_cn_g, g_cn_b, g_qn_g, g_kn_g, g_lb, g_conv_w, g_gn, loss = red
    g_conv_w = lax.dynamic_slice(g_conv_w, (0, dev * cs), (cw, cs))
    g_gn = lax.dynamic_slice(g_gn, (dev * gs,), (gs,))
    small_idx = {0: g_norm1, 4: g_normm, 5: g_norm2, 10: g_conv_w[None], 11: g_conv_b[None], 12: g_cn_g[None],
                 13: g_cn_b[None], 14: g_qn_g[None], 15: g_kn_g[None], 18: g_lb, 19: g_gn[None]}
    for i, g in small_idx.items():
        grads[i] = g
    update(small_idx)
    (g_ffn1_0,) = reduce_end(groups["5"], [d for d in deltas if d is not None], "_5")
    grads[1], grads[2], grads[3] = ffn_grads([g_ffn1_0, g_ffn1_1])
    update((1, 2, 3))
    return (loss[0], grad_x, *grads, *deltas, *new_m, *new_v)
```

```python
import functools
import math

import jax
import jax.numpy as jnp
from jax import lax
from jax.experimental import pallas as pl
from jax.experimental.pallas import tpu as pltpu

F32 = jnp.float32
BF16 = jnp.bfloat16
MESH = pl.DeviceIdType.MESH
N_DEV = 8

EPS = 1e-6
HEAD_DIM = 128
CONV_WIDTH = 31
DIL_PATTERNS = ((128, 1), (512, 4), (2048, 16))
Q_BLOCK = 128
ROPE_THETA = 10000.0
HGRN_KDIM = 128
HGRN_CHUNK = 256

ADAM_LR = 0.001
ADAM_B1 = 0.9
ADAM_B2 = 0.999
ADAM_EPS = 1e-08
ADAM_WD = 0.01
ADAM_STEP = 10

VMEM_LIMIT_BYTES = 56 * 1024 * 1024
LANE = 128
SUBLANE_BF16 = 16

ANY = pl.BlockSpec(memory_space=pl.ANY)


def _tile(n, pref, mult):
    t = (min(pref, n) // mult) * mult
    while t > 0:
        if n % t == 0:
            return t
        t -= mult
    return n


def _params(*sem):
    return pltpu.CompilerParams(dimension_semantics=sem, vmem_limit_bytes=VMEM_LIMIT_BYTES)


_DOT_DIMS = {
    "nn": (((1,), (0,)), ((), ())),
    "nt": (((1,), (1,)), ((), ())),
    "tn": (((0,), (0,)), ((), ())),
}


def _dot(a, b, mode):
    return lax.dot_general(a, b, _DOT_DIMS[mode], preferred_element_type=F32)


def _mm(a, b, mode, out_dtype, name, res=None, tm=1024, tn=1024, tk=2048):
    parts, a_shape = (a.shape[0], (a.shape[1], a.shape[0] * a.shape[2])) if a.ndim == 3 else (1, a.shape)
    if mode == "nt":
        (M, K), N = a_shape, b.shape[0]
    elif mode == "nn":
        (M, K), N = a_shape, b.shape[1]
    else:
        (K, M), N = a_shape, b.shape[1]
    tm, tn, tk = _tile(M, tm, LANE), _tile(N, tn, LANE), _tile(K, tk, LANE)
    nk = K // tk

    def body(*refs):
        if res is None:
            a_ref, b_ref, o_ref, acc = refs
        else:
            a_ref, b_ref, r_ref, o_ref, acc = refs
        k = pl.program_id(2)

        @pl.when(k == 0)
        def _():
            acc[...] = jnp.zeros_like(acc)

        acc[...] += _dot(a_ref[...].astype(BF16), b_ref[...].astype(BF16), mode)

        @pl.when(k == nk - 1)
        def _():
            r = acc[...]
            if res is not None:
                r = r_ref[...] + r
            o_ref[...] = r.astype(out_dtype)

    if parts > 1:
        assert mode in ("nn", "tn") and a.shape[2] % (tk if mode == "nn" else tm) == 0
        per = a.shape[2] // (tk if mode == "nn" else tm)
        a_spec = {"nn": pl.BlockSpec((None, tm, tk), lambda i, j, k: (k // per, i, k % per)),
                  "tn": pl.BlockSpec((None, tk, tm), lambda i, j, k: (i // per, k, i % per))}[mode]
    else:
        a_spec = {"nt": pl.BlockSpec((tm, tk), lambda i, j, k: (i, k)),
                  "nn": pl.BlockSpec((tm, tk), lambda i, j, k: (i, k)),
                  "tn": pl.BlockSpec((tk, tm), lambda i, j, k: (k, i))}[mode]
    b_spec = {"nt": pl.BlockSpec((tn, tk), lambda i, j, k: (j, k)),
              "nn": pl.BlockSpec((tk, tn), lambda i, j, k: (k, j)),
              "tn": pl.BlockSpec((tk, tn), lambda i, j, k: (k, j))}[mode]
    o_spec = pl.BlockSpec((tm, tn), lambda i, j, k: (i, j))
    in_specs = [a_spec, b_spec] + ([o_spec] if res is not None else [])
    args = (a, b) + ((res,) if res is not None else ())
    return pl.pallas_call(
        body, name=name, grid=(M // tm, N // tn, nk),
        in_specs=in_specs, out_specs=o_spec,
        out_shape=jax.ShapeDtypeStruct((M, N), out_dtype),
        scratch_shapes=[pltpu.VMEM((tm, tn), F32)],
        compiler_params=_params("parallel", "parallel", "arbitrary"),
    )(*args)


def _rms_fwd(x, gain, name):
    S, D = x.shape
    tm = _tile(S, 512, SUBLANE_BF16)

    def body(x_ref, g_ref, o_ref):
        xv = x_ref[...]
        r = lax.rsqrt(jnp.mean(xv * xv, axis=-1, keepdims=True) + EPS)
        o_ref[...] = (xv * r * g_ref[...]).astype(BF16)

    return pl.pallas_call(
        body, name=name, grid=(S // tm,),
        in_specs=[pl.BlockSpec((tm, D), lambda i: (i, 0)), pl.BlockSpec((1, D), lambda i: (0, 0))],
        out_specs=pl.BlockSpec((tm, D), lambda i: (i, 0)),
        out_shape=jax.ShapeDtypeStruct((S, D), BF16),
        compiler_params=_params("parallel"),
    )(x, gain)


def _resid_rms(x, ffn, gain, name):
    S, D = x.shape
    tm = _tile(S, 512, SUBLANE_BF16)

    def body(x_ref, f_ref, g_ref, h_ref, o_ref):
        hv = x_ref[...] + 0.5 * f_ref[...]
        h_ref[...] = hv
        r = lax.rsqrt(jnp.mean(hv * hv, axis=-1, keepdims=True) + EPS)
        o_ref[...] = (hv * r * g_ref[...]).astype(BF16)

    row = pl.BlockSpec((tm, D), lambda i: (i, 0))
    return pl.pallas_call(
        body, name=name, grid=(S // tm,),
        in_specs=[row, row, pl.BlockSpec((1, D), lambda i: (0, 0))], out_specs=[row, row],
        out_shape=[jax.ShapeDtypeStruct((S, D), F32), jax.ShapeDtypeStruct((S, D), BF16)],
        compiler_params=_params("parallel"),
    )(x, ffn, gain)


def _rms_bwd(x, gain, dxn, dy, name):
    S, D = x.shape
    tm = _tile(S, 512, 8)

    def body(x_ref, g_ref, dxn_ref, dy_ref, dx_ref, dg_ref):
        @pl.when(pl.program_id(0) == 0)
        def _():
            dg_ref[...] = jnp.zeros_like(dg_ref)

        xv = x_ref[...]
        r = lax.rsqrt(jnp.mean(xv * xv, axis=-1, keepdims=True) + EPS)
        xh = xv * r
        dxn_v = dxn_ref[...]
        dg_ref[...] += jnp.sum(dxn_v * xh, axis=0, keepdims=True)
        dxh = dxn_v * g_ref[...]
        dx_ref[...] = dy_ref[...] + r * (dxh - xh * jnp.mean(dxh * xh, axis=-1, keepdims=True))

    row = pl.BlockSpec((tm, D), lambda i: (i, 0))
    vec = pl.BlockSpec((1, D), lambda i: (0, 0))
    return pl.pallas_call(
        body, name=name, grid=(S // tm,),
        in_specs=[row, vec, row, row], out_specs=[row, vec],
        out_shape=[jax.ShapeDtypeStruct((S, D), F32), jax.ShapeDtypeStruct((1, D), F32)],
        compiler_params=_params("arbitrary"),
    )(x, gain, dxn, dy)


def _ffn_fwd(xn, w, name):
    S, D = xn.shape
    F = w.shape[1]
    tm, tf = _tile(S, 1024, SUBLANE_BF16), _tile(F, 512, LANE)
    nf = F // tf

    def body(xn_ref, w_ref, o_ref, gu_ref):
        @pl.when(pl.program_id(1) == 0)
        def _():
            o_ref[...] = jnp.zeros_like(o_ref)

        xnv = xn_ref[...]
        g = _dot(xnv, w_ref[0], "nt")
        u = _dot(xnv, w_ref[1], "nt")
        gu_ref[0] = g.astype(BF16)
        gu_ref[1] = u.astype(BF16)
        h = (g * jax.nn.sigmoid(g) * u).astype(BF16)
        o_ref[...] += _dot(h, w_ref[2], "nn")

    row = pl.BlockSpec((tm, D), lambda i, f: (i, 0))
    return pl.pallas_call(
        body, name=name, grid=(S // tm, nf),
        in_specs=[row, pl.BlockSpec((3, tf, D), lambda i, f: (0, f, 0))],
        out_specs=[row, pl.BlockSpec((2, tm, tf), lambda i, f: (0, i, f))],
        out_shape=[jax.ShapeDtypeStruct((S, D), F32), jax.ShapeDtypeStruct((2, S, F), BF16)],
        compiler_params=_params("parallel", "arbitrary"),
    )(xn, w)


def _ffn_bwd_dx(dy, w, gu, name):
    S, D = dy.shape
    F = w.shape[1]
    tm, tf = _tile(S, 1024, SUBLANE_BF16), _tile(F, 512, LANE)
    nf = F // tf

    def body(dy_ref, w_ref, gu_ref, dxn_ref, dout_ref, t_ref):
        @pl.when(pl.program_id(1) == 0)
        def _():
            dxn_ref[...] = jnp.zeros_like(dxn_ref)
            dout_ref[...] = (0.5 * dy_ref[...]).astype(BF16)

        g = gu_ref[0].astype(F32)
        u = gu_ref[1].astype(F32)
        sig = jax.nn.sigmoid(g)
        silu = g * sig
        t_ref[2] = (silu * u).astype(BF16)
        dh = _dot(dout_ref[...], w_ref[2], "nt")
        dg = (dh * (u * (sig * (1.0 + g * (1.0 - sig))))).astype(BF16)
        du = (dh * silu).astype(BF16)
        t_ref[0] = dg
        t_ref[1] = du
        dxn_ref[...] += _dot(dg, w_ref[0], "nn") + _dot(du, w_ref[1], "nn")

    row = pl.BlockSpec((tm, D), lambda i, f: (i, 0), pipeline_mode=pl.Buffered(1))
    return pl.pallas_call(
        body, name=name, grid=(S // tm, nf),
        in_specs=[row, pl.BlockSpec((3, tf, D), lambda i, f: (0, f, 0)),
                  pl.BlockSpec((2, tm, tf), lambda i, f: (0, i, f))],
        out_specs=[row, row, pl.BlockSpec((3, tm, tf), lambda i, f: (0, i, f))],
        out_shape=[jax.ShapeDtypeStruct((S, D), F32), jax.ShapeDtypeStruct((S, D), BF16),
                   jax.ShapeDtypeStruct((3, S, F), BF16)],
        compiler_params=_params("parallel", "arbitrary"),
    )(dy, w, gu)


def _ffn_bwd_dw(xn, dout, t, name):
    S, D = xn.shape
    F = t.shape[2]
    ts, tf = _tile(S, 1024, LANE), _tile(F, 512, LANE)
    ns = S // ts

    def body(xn_ref, dout_ref, t_ref, dw_ref, acc):
        s = pl.program_id(1)

        @pl.when(s == 0)
        def _():
            acc[...] = jnp.zeros_like(acc)

        xnv = xn_ref[...]
        acc[0] += _dot(t_ref[0], xnv, "tn")
        acc[1] += _dot(t_ref[1], xnv, "tn")
        acc[2] += _dot(t_ref[2], dout_ref[...], "tn")

        @pl.when(s == ns - 1)
        def _():
            dw_ref[...] = acc[...].astype(BF16)

    row = pl.BlockSpec((ts, D), lambda f, s: (s, 0))
    return pl.pallas_call(
        body, name=name, grid=(F // tf, ns),
        in_specs=[row, row, pl.BlockSpec((3, ts, tf), lambda f, s: (0, s, f))],
        out_specs=pl.BlockSpec((3, tf, D), lambda f, s: (0, f, 0)),
        out_shape=jax.ShapeDtypeStruct((3, F, D), BF16),
        scratch_shapes=[pltpu.VMEM((3, tf, D), F32)],
        compiler_params=_params("parallel", "arbitrary"),
    )(xn, dout, t)


def _loss_grad(x, ffn, target, name):
    S, D = x.shape
    tm = _tile(S, 512, 8)

    def body(x_ref, f_ref, t_ref, dy_ref, l_ref):
        @pl.when(pl.program_id(0) == 0)
        def _():
            l_ref[...] = jnp.zeros_like(l_ref)

        e = (x_ref[...] + 0.5 * f_ref[...]) - t_ref[...]
        dy_ref[...] = e * (1.0 / D)
        l_ref[...] += 0.5 * jnp.sum(jnp.sum(e * e, axis=-1, keepdims=True) * (1.0 / D))

    row = pl.BlockSpec((tm, D), lambda i: (i, 0))
    one = pl.BlockSpec((8, LANE), lambda i: (0, 0))
    return pl.pallas_call(
        body, name=name, grid=(S // tm,),
        in_specs=[row, row, row], out_specs=[row, one],
        out_shape=[jax.ShapeDtypeStruct((S, D), F32), jax.ShapeDtypeStruct((8, LANE), F32)],
        compiler_params=_params("arbitrary"),
    )(x, ffn, target)


def _adamw(w, g, m, v, name):
    shape = w.shape
    C = shape[-1]
    R = math.prod(shape[:-1])
    tr = _tile(R, max(8, (1 << 19) // C // 8 * 8), 8)
    c1 = 1.0 / (1.0 - ADAM_B1 ** ADAM_STEP)
    c2 = 1.0 / (1.0 - ADAM_B2 ** ADAM_STEP)

    def body(w_ref, g_ref, m_ref, v_ref, d_ref, nm_ref, nv_ref):
        gv = g_ref[...]
        nm = ADAM_B1 * m_ref[...] + (1.0 - ADAM_B1) * gv
        nv = ADAM_B2 * v_ref[...] + (1.0 - ADAM_B2) * (gv * gv)
        nm_ref[...] = nm
        nv_ref[...] = nv
        d_ref[...] = -ADAM_LR * ((nm * c1) / (jnp.sqrt(nv * c2) + ADAM_EPS) + ADAM_WD * w_ref[...])

    blk = pl.BlockSpec((tr, C), lambda i: (i, 0))
    sds = jax.ShapeDtypeStruct((R, C), F32)
    outs = pl.pallas_call(
        body, name=name, grid=(R // tr,),
        in_specs=[blk] * 4, out_specs=[blk] * 3, out_shape=[sds] * 3,
        compiler_params=_params("parallel"),
    )(*(a.reshape(R, C) for a in (w, g, m, v)))
    return tuple(o.reshape(shape) for o in outs)


def _me():
    return lax.axis_index("x"), lax.axis_index("y"), lax.axis_index("c")


def _add_core_halves(grad, got, c_idx, name):
    n, nk, _, r, C = grad.shape
    tr = _tile(r, 1024, SUBLANE_BF16)

    def body(c_ref, g_ref, r_ref, o_ref):
        o_ref[...] = (g_ref[...].astype(F32) + r_ref[...].astype(F32)).astype(BF16)

    return pl.pallas_call(
        body, name=name,
        grid_spec=pltpu.PrefetchScalarGridSpec(
            num_scalar_prefetch=1, grid=(n, nk, r // tr),
            in_specs=[pl.BlockSpec((None, None, None, tr, C), lambda i, k, t, c: (i, k, c[0], t, 0)),
                      pl.BlockSpec((None, None, tr, C), lambda i, k, t, c: (i, k, t, 0))],
            out_specs=pl.BlockSpec((None, None, tr, C), lambda i, k, t, c: (i, k, t, 0))),
        out_shape=jax.ShapeDtypeStruct((n, nk, r, C), BF16),
        compiler_params=_params("parallel", "parallel", "parallel"),
    )(c_idx, grad, got)


HBM = pl.BlockSpec(memory_space=pltpu.HBM)
SEM = pl.BlockSpec(memory_space=pltpu.SEMAPHORE)
EFFECT = pltpu.SideEffectType.DATAFLOW_SIDE_EFFECTING


def _push_start(srcs, lands, plan, after, name):
    ns, nl = len(srcs), len(lands)
    ncp = len(plan([None] * ns, [None] * nl, dry=True))
    extra = [] if after is None else [after]

    def body(*refs):
        src_refs, land_refs = refs[:ns], refs[ns:ns + nl]
        send_sems, recv_sems = refs[ns + nl + len(extra)], refs[ns + nl + len(extra) + 1]
        token = refs[-1]
        for i, (s, d, to) in enumerate(plan(src_refs, land_refs)):
            pltpu.make_async_remote_copy(src_ref=s, dst_ref=d, send_sem=send_sems.at[i], recv_sem=recv_sems.at[i],
                                         device_id=to, device_id_type=MESH).start()
        token[...] = jnp.zeros_like(token)

    out = pl.pallas_call(
        body, name=name,
        out_shape=(pltpu.SemaphoreType.DMA((ncp,)), pltpu.SemaphoreType.DMA((ncp,)),
                   *[pltpu.HBM(a.shape, a.dtype) for a in srcs], *[pltpu.HBM(a.shape, a.dtype) for a in lands],
                   jax.ShapeDtypeStruct((8, LANE), F32)),
        in_specs=[HBM] * (ns + nl) + [ANY] * len(extra),
        out_specs=(SEM, SEM, *[HBM] * (ns + nl), pl.BlockSpec(memory_space=pltpu.VMEM)),
        input_output_aliases={i: 2 + i for i in range(ns + nl)},
        compiler_params=pltpu.CompilerParams(has_side_effects=EFFECT),
    )(*[pltpu.with_memory_space_constraint(a, pltpu.HBM) for a in srcs + lands], *extra)
    return out[0], out[1], list(out[2:2 + ns]), list(out[2 + ns:2 + ns + nl]), out[-1]


def _push_wait(send_sems, recv_sems, srcs, lands, plan, after, name):
    ns, nl = len(srcs), len(lands)
    after = list(after) if isinstance(after, (list, tuple)) else [after]

    def body(*refs):
        src_refs, land_refs = refs[:ns], refs[ns:ns + nl]
        send, recv = refs[ns + nl], refs[ns + nl + 1]
        for i, (s, d, to) in enumerate(plan(src_refs, land_refs)):
            cp = pltpu.make_async_remote_copy(src_ref=s, dst_ref=d, send_sem=send.at[i], recv_sem=recv.at[i],
                                              device_id=to, device_id_type=MESH)
            cp.wait_send()
            cp.wait_recv()

    out = pl.pallas_call(
        body, name=name,
        out_shape=tuple(pltpu.HBM(a.shape, a.dtype) for a in srcs + lands),
        in_specs=[HBM] * (ns + nl) + [SEM, SEM] + [ANY] * len(after),
        out_specs=tuple([HBM] * (ns + nl)),
        input_output_aliases={i: i for i in range(ns + nl)},
        compiler_params=pltpu.CompilerParams(has_side_effects=EFFECT),
    )(*srcs, *lands, send_sems, recv_sems, *after)
    return list(out[:ns]), list(out[ns:])


def _gather_plan(src_refs, land_refs, dry=False):
    if dry:
        return [None] * (4 * len(src_refs))
    x, y, c = _me()
    me = 4 * x + 2 * y + c
    targets = [(x, y, 1 - c), (1 - x, y, c), (x, 1 - y, c), (1 - x, 1 - y, c)]
    return [(s, l.at[:, me], to) for s, l in zip(src_refs, land_refs) for to in targets]


def _halves_plan(src_refs, land_refs, dry=False):
    if dry:
        return [None] * len(src_refs)
    x, y, c = _me()
    return [(s.at[:, :, 1 - c], l, (x, y, 1 - c)) for s, l in zip(src_refs, land_refs)]


def _chip_plan(src_refs, land_refs, dry=False):
    if dry:
        return [None] * (3 * len(src_refs))
    x, y, c = _me()
    chips = [(1 - x, y), (x, 1 - y), (1 - x, 1 - y)]
    return [(s.at[:, 2 * chip[0] + chip[1]], l.at[j], (*chip, c))
            for s, l in zip(src_refs, land_refs) for j, chip in enumerate(chips)]


def _forward_plan(src_refs, land_refs, dry=False):
    if dry:
        return [None] * (3 * len(land_refs))
    x, y, c = _me()
    chips = [(1 - x, y), (x, 1 - y), (1 - x, 1 - y)]
    plan = []
    for l in land_refs:
        for chip in chips:
            blk = l.at[:, 4 * chip[0] + 2 * chip[1] + c]
            plan.append((blk, blk, (x, y, 1 - c)))
    return plan


def _sum_chip_blocks(sums, got, k_idx, name):
    n, _, r, C = sums.shape
    tr = _tile(r, 512, SUBLANE_BF16)

    def body(k_ref, s_ref, r_ref, o_ref):
        acc = s_ref[...].astype(F32)
        for j in range(3):
            acc = acc + r_ref[j].astype(F32)
        o_ref[...] = acc

    return pl.pallas_call(
        body, name=name,
        grid_spec=pltpu.PrefetchScalarGridSpec(
            num_scalar_prefetch=1, grid=(n, r // tr),
            in_specs=[pl.BlockSpec((None, None, tr, C), lambda i, t, k: (i, k[0], t, 0)),
                      pl.BlockSpec((3, None, tr, C), lambda i, t, k: (0, i, t, 0))],
            out_specs=pl.BlockSpec((None, tr, C), lambda i, t, k: (i, t, 0))),
        out_shape=jax.ShapeDtypeStruct((n, r, C), F32),
        compiler_params=_params("parallel", "parallel"),
    )(k_idx, sums, got)


def _all_reduce_small(v, name, after=()):
    R = v.shape[0]
    after = list(after)

    def body(*refs):
        v_ref = refs[0]
        o_ref, buf, send_sems, recv_sems = refs[1 + len(after):]
        x, y, c = _me()
        me = 4 * x + 2 * y + c
        buf[me] = v_ref[...]
        copies = []
        for k in range(1, N_DEV):
            peer = (x ^ (k >> 2), y ^ ((k >> 1) & 1), c ^ (k & 1))
            copies.append(pltpu.make_async_remote_copy(
                src_ref=v_ref, dst_ref=buf.at[me],
                send_sem=send_sems.at[k - 1], recv_sem=recv_sems.at[k - 1],
                device_id=peer, device_id_type=MESH))
        for cp in copies:
            cp.start()
        for cp in copies:
            cp.wait()
        acc = buf[0]
        for d in range(1, N_DEV):
            acc = acc + buf[d]
        o_ref[...] = acc

    vm = pl.BlockSpec(memory_space=pltpu.VMEM)
    return pl.pallas_call(
        body, name=name, in_specs=[vm] + [ANY] * len(after), out_specs=vm,
        out_shape=jax.ShapeDtypeStruct((R, LANE), F32),
        scratch_shapes=[pltpu.VMEM((N_DEV, R, LANE), F32),
                        pltpu.SemaphoreType.DMA((N_DEV - 1,)), pltpu.SemaphoreType.DMA((N_DEV - 1,))],
        compiler_params=pltpu.CompilerParams(vmem_limit_bytes=VMEM_LIMIT_BYTES),
    )(v, *after)


def _pack_rows(parts):
    flat = jnp.concatenate([p.reshape(-1).astype(F32) for p in parts])
    n = flat.shape[0]
    rows = -(-n // (8 * LANE)) * 8
    flat = jnp.pad(flat, (0, rows * LANE - n))
    return flat.reshape(rows, LANE)


def _unpack_rows(packed, shapes):
    flat = packed.reshape(-1)
    out, off = [], 0
    for s in shapes:
        n = math.prod(s)
        out.append(flat[off:off + n].reshape(s))
        off += n
    return out


CONV_HALO = 32


def _conv_fwd(u, conv_w, conv_b, cn_g, cn_b, name):
    S = u.shape[0]
    C = conv_w.shape[1]
    T = _tile(S, 256, CONV_HALO)
    hb = T // CONV_HALO

    def body(av_ref, ag_ref, pv_ref, pg_ref, w_ref, b_ref, g_ref, bb_ref, out_ref, y_ref, scr):
        i = pl.program_id(0)
        prev = pv_ref[...] * jax.nn.sigmoid(pg_ref[...])
        scr[0:CONV_HALO, :] = jnp.where(i > 0, prev, 0.0)
        scr[CONV_HALO:CONV_HALO + T, :] = av_ref[...] * jax.nn.sigmoid(ag_ref[...])
        for s in range(C // LANE):
            sl = slice(s * LANE, (s + 1) * LANE)
            acc = jnp.broadcast_to(b_ref[:, sl], (T, LANE))
            for j in range(CONV_WIDTH):
                acc = acc + w_ref[j:j + 1, sl] * scr[pl.ds(CONV_HALO - (CONV_WIDTH - 1) + j, T), sl]
            y_ref[:, sl] = acc
        acc = y_ref[...]
        mu = jnp.mean(acc, axis=-1, keepdims=True)
        xc = acc - mu
        var = jnp.mean(xc * xc, axis=-1, keepdims=True)
        ln = xc * lax.rsqrt(var + EPS) * g_ref[...] + bb_ref[...]
        out_ref[...] = (ln * jax.nn.sigmoid(ln)).astype(BF16)

    def cur(cb):
        return pl.BlockSpec((T, C), lambda i: (i, cb))

    def halo(cb):
        return pl.BlockSpec((CONV_HALO, C), lambda i: (jnp.maximum(i * hb - 1, 0), cb))

    vec = pl.BlockSpec((1, C), lambda i: (0, 0))
    return pl.pallas_call(
        body, name=name, grid=(S // T,),
        in_specs=[cur(0), cur(1), halo(0), halo(1), pl.BlockSpec((CONV_WIDTH, C), lambda i: (0, 0)), vec, vec, vec],
        out_specs=[pl.BlockSpec((T, C), lambda i: (i, 0))] * 2,
        out_shape=[jax.ShapeDtypeStruct((S, C), BF16), jax.ShapeDtypeStruct((S, C), F32)],
        scratch_shapes=[pltpu.VMEM((T + CONV_HALO, C), F32)],
        compiler_params=_params("parallel"),
    )(u, u, u, u, conv_w, conv_b, cn_g, cn_b)


def _conv_bwd_norm(dz, y, cn_g, cn_b, name):
    S, C = y.shape
    T = _tile(S, 256, 8)

    def body(dz_ref, y_ref, g_ref, bb_ref, dy_ref, dg_ref, db_ref):
        @pl.when(pl.program_id(0) == 0)
        def _():
            dg_ref[...] = jnp.zeros_like(dg_ref)
            db_ref[...] = jnp.zeros_like(db_ref)

        yv = y_ref[...]
        mu = jnp.mean(yv, axis=-1, keepdims=True)
        xc = yv - mu
        rstd = lax.rsqrt(jnp.mean(xc * xc, axis=-1, keepdims=True) + EPS)
        xh = xc * rstd
        ln = xh * g_ref[...] + bb_ref[...]
        sg = jax.nn.sigmoid(ln)
        dln = dz_ref[...] * (sg * (1.0 + ln * (1.0 - sg)))
        dg_ref[...] += jnp.sum(dln * xh, axis=0, keepdims=True)
        db_ref[...] += jnp.sum(dln, axis=0, keepdims=True)
        dxh = dln * g_ref[...]
        dy_ref[...] = rstd * (dxh - jnp.mean(dxh, axis=-1, keepdims=True)
                              - xh * jnp.mean(dxh * xh, axis=-1, keepdims=True))

    row = pl.BlockSpec((T, C), lambda i: (i, 0))
    vec = pl.BlockSpec((1, C), lambda i: (0, 0))
    return pl.pallas_call(
        body, name=name, grid=(S // T,),
        in_specs=[row, row, vec, vec], out_specs=[row, vec, vec],
        out_shape=[jax.ShapeDtypeStruct((S, C), F32), jax.ShapeDtypeStruct((1, C), F32),
                   jax.ShapeDtypeStruct((1, C), F32)],
        compiler_params=_params("arbitrary"),
    )(dz, y, cn_g, cn_b)


def _conv_bwd_taps(u, dy, conv_w, name):
    S, C = dy.shape
    T = _tile(S, 256, CONV_HALO)
    hb = T // CONV_HALO
    nt = S // T
    ns = C // LANE
    W1 = CONV_WIDTH - 1

    def body(av_ref, ag_ref, pv_ref, pg_ref, dy_ref, dn_ref, w_ref, dv_ref, dg_ref, dw_ref, db_ref, a_scr, d_scr):
        i = pl.program_id(1)

        @pl.when(i == 0)
        def _():
            dw_ref[...] = jnp.zeros_like(dw_ref)
            db_ref[...] = jnp.zeros_like(db_ref)

        av, sg = av_ref[...], jax.nn.sigmoid(ag_ref[...])
        prev = pv_ref[...] * jax.nn.sigmoid(pg_ref[...])
        a_scr[0:CONV_HALO, :] = jnp.where(i > 0, prev, 0.0)
        a_scr[CONV_HALO:CONV_HALO + T, :] = av * sg
        dyv = dy_ref[...]
        d_scr[0:T, :] = dyv
        d_scr[T:T + CONV_HALO, :] = jnp.where(i < nt - 1, dn_ref[...], 0.0)
        da = jnp.zeros((T, LANE), F32)
        for j in range(CONV_WIDTH):
            da = da + w_ref[j:j + 1, :] * d_scr[pl.ds(W1 - j, T), :]
            dw_ref[j:j + 1, :] += jnp.sum(dyv * a_scr[pl.ds(CONV_HALO - W1 + j, T), :], axis=0, keepdims=True)
        db_ref[...] += jnp.sum(dyv, axis=0, keepdims=True)
        dv_ref[...] = (da * sg).astype(BF16)
        dg_ref[...] = (da * av * sg * (1.0 - sg)).astype(BF16)

    def cur(part):
        return pl.BlockSpec((T, LANE), lambda cb, i: (i, part * ns + cb))

    def halo(part):
        return pl.BlockSpec((CONV_HALO, LANE), lambda cb, i: (jnp.maximum(i * hb - 1, 0), part * ns + cb))

    nxt = pl.BlockSpec((CONV_HALO, LANE), lambda cb, i: (jnp.minimum((i + 1) * hb, S // CONV_HALO - 1), cb))
    row = pl.BlockSpec((T, LANE), lambda cb, i: (i, cb))
    return pl.pallas_call(
        body, name=name, grid=(ns, nt),
        in_specs=[cur(0), cur(1), halo(0), halo(1), row, nxt, pl.BlockSpec((CONV_WIDTH, LANE), lambda cb, i: (0, cb))],
        out_specs=[row, row, pl.BlockSpec((CONV_HALO, LANE), lambda cb, i: (0, cb)),
                   pl.BlockSpec((1, LANE), lambda cb, i: (0, cb))],
        out_shape=[jax.ShapeDtypeStruct((S, C), BF16), jax.ShapeDtypeStruct((S, C), BF16),
                   jax.ShapeDtypeStruct((CONV_HALO, C), F32), jax.ShapeDtypeStruct((1, C), F32)],
        scratch_shapes=[pltpu.VMEM((T + CONV_HALO, LANE), F32), pltpu.VMEM((T + CONV_HALO, LANE), F32)],
        compiler_params=_params("parallel", "arbitrary"),
    )(u, u, u, u, dy, dy, conv_w)


def _rope_tables(S):
    half = HEAD_DIM // 2
    inv = jnp.exp(-math.log(ROPE_THETA) * jnp.arange(half, dtype=F32) / half)
    ang = jnp.arange(S, dtype=jnp.int32).astype(F32)[:, None] * inv[None, :]
    cos, sin = jnp.cos(ang), jnp.sin(ang)
    return jnp.concatenate([cos, cos], axis=1), jnp.concatenate([-sin, sin], axis=1)


def _qkv_prep(u, qn_g, kn_g, cos, sin, cb0, name):
    S = u.shape[0]
    A = (u.shape[1] // (cb0 + 3))
    H = A // HEAD_DIM
    T = _tile(S, 256, SUBLANE_BF16)
    scale = HEAD_DIM ** -0.5

    def body(q_ref, k_ref, v_ref, qg_ref, kg_ref, cos_ref, sin_ref, qo_ref, ko_ref, vo_ref):
        cosv, sinv = cos_ref[...], sin_ref[...]
        for h in range(H):
            sl = slice(h * HEAD_DIM, (h + 1) * HEAD_DIM)
            for x_ref, g_ref, o_ref, sc in ((q_ref, qg_ref, qo_ref, scale), (k_ref, kg_ref, ko_ref, 1.0)):
                xv = x_ref[:, sl]
                xn = xv * lax.rsqrt(jnp.mean(xv * xv, axis=-1, keepdims=True) + EPS) * g_ref[...]
                y = xn * cosv + pltpu.roll(xn, HEAD_DIM // 2, 1) * sinv
                o_ref[:, sl] = (y * sc).astype(BF16)
        vo_ref[...] = v_ref[...].astype(BF16)

    def col(cb):
        return pl.BlockSpec((T, A), lambda i: (i, cb))

    vec = pl.BlockSpec((1, HEAD_DIM), lambda i: (0, 0))
    tab = pl.BlockSpec((T, HEAD_DIM), lambda i: (i, 0))
    out = pl.BlockSpec((T, A), lambda i: (i, 0))
    return pl.pallas_call(
        body, name=name, grid=(S // T,),
        in_specs=[col(cb0), col(cb0 + 1), col(cb0 + 2), vec, vec, tab, tab],
        out_specs=[out] * 3, out_shape=[jax.ShapeDtypeStruct((S, A), BF16)] * 3,
        compiler_params=_params("parallel"),
    )(u, u, u, qn_g, kn_g, cos, sin)


def _qkv_prep_bwd(u, dqs, dks, dvs, qn_g, kn_g, cos, sin, cb0, name):
    S = u.shape[0]
    A = dqs[0].shape[1]
    H = A // HEAD_DIM
    T = _tile(S, 256, SUBLANE_BF16)
    nb = len(dqs)
    scale = HEAD_DIM ** -0.5

    def body(*refs):
        q_ref, k_ref, qg_ref, kg_ref, cos_ref, sin_ref = refs[:6]
        dq_refs, dk_refs, dv_refs = refs[6:6 + nb], refs[6 + nb:6 + 2 * nb], refs[6 + 2 * nb:6 + 3 * nb]
        dqo_ref, dko_ref, dvo_ref, dqg_ref, dkg_ref = refs[6 + 3 * nb:]

        @pl.when(pl.program_id(0) == 0)
        def _():
            dqg_ref[...] = jnp.zeros_like(dqg_ref)
            dkg_ref[...] = jnp.zeros_like(dkg_ref)

        cosv, sinv = cos_ref[...], sin_ref[...]
        for h in range(H):
            sl = slice(h * HEAD_DIM, (h + 1) * HEAD_DIM)
            for x_ref, g_ref, d_refs, o_ref, dg_ref, sc in ((q_ref, qg_ref, dq_refs, dqo_ref, dqg_ref, scale),
                                                          (k_ref, kg_ref, dk_refs, dko_ref, dkg_ref, 1.0)):
                dy = d_refs[0][:, sl]
                for r in d_refs[1:]:
                    dy = dy + r[:, sl]
                dy = dy * sc
                dxn = dy * cosv + pltpu.roll(dy * sinv, HEAD_DIM // 2, 1)
                xv = x_ref[:, sl]
                r = lax.rsqrt(jnp.mean(xv * xv, axis=-1, keepdims=True) + EPS)
                xh = xv * r
                dg_ref[...] += jnp.sum(dxn * xh, axis=0, keepdims=True)
                dxh = dxn * g_ref[...]
                o_ref[:, sl] = (r * (dxh - xh * jnp.mean(dxh * xh, axis=-1, keepdims=True))).astype(BF16)
        dv = dv_refs[0][...]
        for r in dv_refs[1:]:
            dv = dv + r[...]
        dvo_ref[...] = dv.astype(BF16)

    def col(cb):
        return pl.BlockSpec((T, A), lambda i: (i, cb))

    vec = pl.BlockSpec((1, HEAD_DIM), lambda i: (0, 0))
    tab = pl.BlockSpec((T, HEAD_DIM), lambda i: (i, 0))
    row = pl.BlockSpec((T, A), lambda i: (i, 0))
    return pl.pallas_call(
        body, name=name, grid=(S // T,),
        in_specs=[col(cb0), col(cb0 + 1), vec, vec, tab, tab] + [row] * (3 * nb),
        out_specs=[row, row, row, vec, vec],
        out_shape=[jax.ShapeDtypeStruct((S, A), BF16)] * 3 + [jax.ShapeDtypeStruct((1, HEAD_DIM), F32)] * 2,
        compiler_params=_params("arbitrary"),
    )(u, u, qn_g, kn_g, cos, sin, *dqs, *dks, *dvs)


ATT_TILE_FWD = 512
ATT_TILE_BWD = 512
NEG = -1e30


def _attn_bias(tile):
    span = max(window for window, _ in DIL_PATTERNS)
    nw = -(-span // tile) + 1
    dist = (jnp.arange(nw)[:, None, None] * tile + jnp.arange(tile)[None, :, None] - jnp.arange(tile)[None, None, :])
    mult = sum(((dist >= 0) & (dist <= window) & (dist % dil == 0)).astype(F32) for window, dil in DIL_PATTERNS)
    return jnp.where(mult > 0, jnp.log(jnp.maximum(mult, 1.0)), NEG)


def _attn_fwd(q, k, v, bias, name):
    S, A = q.shape
    H = A // HEAD_DIM
    nw, T, _ = bias.shape
    nq = S // T

    def body(q_ref, k_ref, v_ref, b_ref, ob_ref, of_ref, l_ref):
        i = pl.program_id(1)
        qv = q_ref[...]
        for w in range(nw):
            blk = i - w
            start = pl.multiple_of(jnp.maximum(blk, 0) * T, T)
            s = _dot(qv, k_ref[pl.ds(start, T), :], "nt") + b_ref[w]
            if w == 0:
                mx = jnp.max(s, axis=-1, keepdims=True)
                p = jnp.exp(s - mx)
                den = jnp.sum(p, axis=-1, keepdims=True)
                o = _dot(p.astype(BF16), v_ref[pl.ds(start, T), :], "nn")
            else:
                s = s + jnp.where(blk >= 0, 0.0, NEG)
                new = jnp.maximum(mx, jnp.max(s, axis=-1, keepdims=True))
                scale = jnp.exp(mx - new)
                p = jnp.exp(s - new)
                den = scale * den + jnp.sum(p, axis=-1, keepdims=True)
                o = scale * o + _dot(p.astype(BF16), v_ref[pl.ds(start, T), :], "nn")
                mx = new
        o = o / den
        ob_ref[...] = o.astype(BF16)
        of_ref[...] = o
        l_ref[...] = mx + jnp.log(den)

    blk = pl.BlockSpec((T, HEAD_DIM), lambda h, i: (i, h))
    full = pl.BlockSpec((S, HEAD_DIM), lambda h, i: (0, h))
    return pl.pallas_call(
        body, name=name, grid=(H, nq),
        in_specs=[blk, full, full, pl.BlockSpec((nw, T, T), lambda h, i: (0, 0, 0))],
        out_specs=[blk, blk, pl.BlockSpec((None, T, 1), lambda h, i: (h, i, 0))],
        out_shape=[jax.ShapeDtypeStruct((S, A), BF16), jax.ShapeDtypeStruct((S, A), F32),
                   jax.ShapeDtypeStruct((H, S, 1), F32)],
        compiler_params=_params("parallel", "arbitrary"),
    )(q, k, v, bias)


def _attn_dq(q, k, v, dz, cb0, o, lse, bias, name):
    S, A = q.shape
    H = A // HEAD_DIM
    nw, T, _ = bias.shape
    nq = S // T

    def body(q_ref, k_ref, v_ref, do_ref, o_ref, l_ref, b_ref, dq_ref, d_ref):
        i = pl.program_id(1)
        qv, dof = q_ref[...], do_ref[...]
        dov = dof.astype(BF16)
        delta = jnp.sum(dof * o_ref[...], axis=-1, keepdims=True)
        d_ref[...] = delta
        lv = l_ref[...]
        dq = jnp.zeros((T, HEAD_DIM), F32)
        for w in range(nw):
            blk = i - w
            start = pl.multiple_of(jnp.maximum(blk, 0) * T, T)
            kv = k_ref[pl.ds(start, T), :]
            s = _dot(qv, kv, "nt") + b_ref[w] + jnp.where(blk >= 0, 0.0, NEG)
            p = jnp.exp(s - lv)
            ds = (p * (_dot(dov, v_ref[pl.ds(start, T), :], "nt") - delta)).astype(BF16)
            dq = dq + _dot(ds, kv, "nn")
        dq_ref[...] = dq

    blk = pl.BlockSpec((T, HEAD_DIM), lambda h, i: (i, h))
    full = pl.BlockSpec((S, HEAD_DIM), lambda h, i: (0, h))
    col = pl.BlockSpec((None, T, 1), lambda h, i: (h, i, 0))
    return pl.pallas_call(
        body, name=name, grid=(H, nq),
        in_specs=[blk, full, full, pl.BlockSpec((T, HEAD_DIM), lambda h, i: (i, cb0 + h)), blk, col,
                  pl.BlockSpec((nw, T, T), lambda h, i: (0, 0, 0))],
        out_specs=[blk, col],
        out_shape=[jax.ShapeDtypeStruct((S, A), F32), jax.ShapeDtypeStruct((H, S, 1), F32)],
        compiler_params=_params("parallel", "arbitrary"),
    )(q, k, v, dz, o, lse, bias)


def _attn_dkv(q, k, v, dz, cb0, lse, delta, bias, name):
    S, A = q.shape
    H = A // HEAD_DIM
    nw, T, _ = bias.shape
    nq = S // T

    def body(k_ref, v_ref, q_ref, do_ref, l_ref, d_ref, b_ref, dk_ref, dv_ref):
        m = pl.program_id(1)
        kv, vv = k_ref[...], v_ref[...]
        dk = jnp.zeros((T, HEAD_DIM), F32)
        dv = jnp.zeros((T, HEAD_DIM), F32)
        for w in range(nw):
            blk = m + w
            start = pl.multiple_of(jnp.minimum(blk, nq - 1) * T, T)
            qv = q_ref[pl.ds(start, T), :]
            dov = do_ref[pl.ds(start, T), :].astype(BF16)
            s = _dot(qv, kv, "nt") + b_ref[w] + jnp.where(blk < nq, 0.0, NEG)
            p = jnp.exp(s - l_ref[pl.ds(start, T), :])
            dv = dv + _dot(p.astype(BF16), dov, "tn")
            ds = (p * (_dot(dov, vv, "nt") - d_ref[pl.ds(start, T), :])).astype(BF16)
            dk = dk + _dot(ds, qv, "tn")
        dk_ref[...] = dk
        dv_ref[...] = dv

    blk = pl.BlockSpec((T, HEAD_DIM), lambda h, m: (m, h))
    full = pl.BlockSpec((S, HEAD_DIM), lambda h, m: (0, h))
    col = pl.BlockSpec((None, S, 1), lambda h, m: (h, 0, 0))
    sds = jax.ShapeDtypeStruct((S, A), F32)
    return pl.pallas_call(
        body, name=name, grid=(H, nq),
        in_specs=[blk, blk, full, pl.BlockSpec((S, HEAD_DIM), lambda h, m: (0, cb0 + h)), col, col,
                  pl.BlockSpec((nw, T, T), lambda h, m: (0, 0, 0))],
        out_specs=[blk, blk], out_shape=[sds, sds],
        compiler_params=_params("parallel", "arbitrary"),
    )(k, v, q, dz, lse, delta, bias)


def _even_mixer(u, conv_w, conv_b, cn_g, cn_b, qn_g, kn_g, tag):
    S = u.shape[0]
    C = conv_w.shape[1]
    A = (u.shape[1] - 2 * C) // 3
    assert A == C, "column-block addressing of u assumes equal conv and attention widths"
    cos, sin = _rope_tables(S)
    bias = _attn_bias(_tile(S, ATT_TILE_BWD, LANE))
    a_out, y = _conv_fwd(u, conv_w, conv_b, cn_g, cn_b, "conv_fwd" + tag)
    q, k, v = _qkv_prep(u, qn_g, kn_g, cos, sin, 2, "qkv_prep" + tag)
    ob, of, lse = _attn_fwd(q, k, v, _attn_bias(_tile(S, ATT_TILE_FWD, LANE)), "attn_fwd" + tag)
    z = jnp.concatenate([a_out, ob], axis=1)

    def backward(dz):
        dy, d_cn_g, d_cn_b = _conv_bwd_norm(dz, y, cn_g, cn_b, "conv_bwd_norm" + tag)
        d_val, d_gate, d_w, d_b = _conv_bwd_taps(u, dy, conv_w, "conv_bwd_taps" + tag)
        dqp, delta = _attn_dq(q, k, v, dz, C // HEAD_DIM, of, lse, bias, "attn_dq" + tag)
        dkp, dvp = _attn_dkv(q, k, v, dz, C // HEAD_DIM, lse, delta, bias, "attn_dkv" + tag)
        dq, dk, dv, d_qn, d_kn = _qkv_prep_bwd(u, [dqp], [dkp], [dvp], qn_g, kn_g, cos, sin, 2, "qkv_prep_bwd" + tag)
        du = jnp.concatenate([d_val, d_gate, dq, dk, dv], axis=1)
        return du, [d_w[:CONV_WIDTH], d_b[0], d_cn_g[0], d_cn_b[0], d_qn[0], d_kn[0]]

    return z, backward


_LEVELS = (128, 64, 32, 16, 8, 4, 2, 1)


def _chunk_cumsum(g, rows, reverse=False):
    C = g.shape[0]
    d = 1
    while d < C:
        if reverse:
            g = g + jnp.where(rows < C - d, pltpu.roll(g, C - d, 0), 0.0)
        else:
            g = g + jnp.where(rows >= d, pltpu.roll(g, d, 0), 0.0)
        d *= 2
    return g


def _level_ref(b, b_scr, rows, m):
    C = b.shape[0]
    if m >= 8:
        pieces = [jnp.broadcast_to(b_scr[2 * m * j + m - 1:2 * m * j + m, :], (2 * m, LANE)) for j in range(C // (2 * m))]
        return pieces[0] if len(pieces) == 1 else jnp.concatenate(pieces, axis=0)
    pos = rows & (2 * m - 1)
    ref = b
    for p in range(2 * m):
        if p != m - 1:
            ref = jnp.where(pos == p, pltpu.roll(b, (p - (m - 1)) % C, 0), ref)
    return ref


def _level_operands(q, k, b, b_scr, rows, m):
    e = jnp.exp(-jnp.abs(b - _level_ref(b, b_scr, rows, m)))
    return (q * e).astype(BF16), (k * e).astype(BF16)


def _split2(x):
    hi = x.astype(BF16)
    lo = (x - hi.astype(F32)).astype(BF16)
    return jnp.concatenate([hi, lo], axis=1)


def _level_table(n):
    t = jnp.arange(n, dtype=jnp.int32)[:, None]
    s = jnp.arange(n, dtype=jnp.int32)[None, :]
    x = t ^ s
    lvl = sum((x >= (1 << j)).astype(jnp.int32) for j in range(1, n.bit_length()))
    return jnp.where(t > s, lvl, jnp.where(t == s, -1, -2))


def _hgrn_gates(qz, fz, la, lc, oml):
    sq = jax.nn.sigmoid(qz)
    q = qz * sq
    s = jax.nn.sigmoid(fz)
    c = lc + jnp.minimum(fz, 0.0) - jnp.log(1.0 + jnp.exp(-jnp.abs(fz)))
    mx = jnp.maximum(la, c)
    g = mx + jnp.log(1.0 + jnp.exp(-jnp.abs(la - c)))
    k = oml * (1.0 - s)
    return q, sq, k, s, g, c


def _hgrn_fwd(u, la, lc, oml, gn_g, name):
    S = u.shape[0]
    W = u.shape[1] // 4
    H = W // HGRN_KDIM
    C = min(HGRN_CHUNK, S)
    nc = S // C
    levels = [m for m in _LEVELS if m < C]
    HB = C // 2

    def body(qz_ref, fz_ref, iz_ref, gz_ref, la_ref, lc_ref, oml_ref, gn_ref, lvl_ref,
             z_ref, o_ref, a_ref, st_ref, state, b_scr):
        @pl.when(pl.program_id(1) == 0)
        def _():
            state[...] = jnp.zeros_like(state)

        rows = lax.broadcasted_iota(jnp.int32, (C, LANE), 0)
        q, _, k, _, g, _ = _hgrn_gates(qz_ref[...], fz_ref[...], la_ref[...], lc_ref[...], oml_ref[...])
        v = iz_ref[...].astype(BF16)
        b = _chunk_cumsum(g, rows)
        b_scr[...] = b
        lvl = lvl_ref[...]
        qk = jnp.sum(q * k, axis=-1, keepdims=True)
        diag = [jnp.where(lvl == -1, qk[r * HB:(r + 1) * HB], 0.0) for r in range(2)]
        for m in levels[1:]:
            qs, ks = _level_operands(q, k, b, b_scr, rows, m)
            for r in range(2):
                sl = slice(r * HB, (r + 1) * HB)
                diag[r] = jnp.where(lvl == m.bit_length() - 1, _dot(qs[sl], ks[sl], "nt"), diag[r])
        qs, ks = _level_operands(q, k, b, b_scr, rows, HB)
        low = _dot(qs[HB:], ks[:HB], "nt")
        a = jnp.concatenate([jnp.concatenate([diag[0], jnp.zeros((HB, HB), F32)], axis=1),
                             jnp.concatenate([low, diag[1]], axis=1)], axis=0)
        ab = a.astype(BF16)
        a_ref[...] = ab
        st = state[...]
        st_ref[...] = st
        o = _dot(ab, v, "nn") + _dot((q * jnp.exp(b)).astype(BF16), st.astype(BF16), "nt")
        bl = b_scr[C - 1:C, :]
        kh = (k * jnp.exp(bl - b)).astype(BF16)
        state[...] = st * jnp.exp(bl) + _dot(v, kh, "tn")
        o_ref[...] = o
        r = lax.rsqrt(jnp.mean(o * o, axis=-1, keepdims=True) + EPS)
        gz = gz_ref[...]
        z_ref[...] = (o * r * gn_ref[...] * (gz * jax.nn.sigmoid(gz))).astype(BF16)

    def col(off):
        return pl.BlockSpec((C, LANE), lambda h, i: (i, off * H + h))

    vec = pl.BlockSpec((1, LANE), lambda h, i: (0, h))
    tile = pl.BlockSpec((C, LANE), lambda h, i: (i, h))
    return pl.pallas_call(
        body, name=name, grid=(H, nc),
        in_specs=[col(0), col(1), col(2), col(3), vec, vec, vec, vec, pl.BlockSpec((HB, HB), lambda h, i: (0, 0))],
        out_specs=[tile, tile, pl.BlockSpec((None, C, C), lambda h, i: (h, i, 0)),
                   pl.BlockSpec((None, None, LANE, LANE), lambda h, i: (h, i, 0, 0))],
        out_shape=[jax.ShapeDtypeStruct((S, W), BF16), jax.ShapeDtypeStruct((S, W), F32),
                   jax.ShapeDtypeStruct((H, S, C), BF16), jax.ShapeDtypeStruct((H, nc, LANE, LANE), F32)],
        scratch_shapes=[pltpu.VMEM((LANE, LANE), F32), pltpu.VMEM((C, LANE), F32)],
        compiler_params=_params("parallel", "arbitrary"),
    )(u, u, u, u, la, lc, oml, gn_g, _level_table(HB))


def _hgrn_bwd(u, la, lc, oml, gn_g, o, a, st, dz, name):
    S = u.shape[0]
    W = u.shape[1] // 4
    H = W // HGRN_KDIM
    C = min(HGRN_CHUNK, S)
    nc = S // C
    levels = [m for m in _LEVELS if m < C]
    HB = C // 2

    def body(qz_ref, fz_ref, iz_ref, gz_ref, la_ref, lc_ref, oml_ref, gn_ref, o_ref, a_ref, st_ref, dz_ref, lvl_ref,
             du_ref, dla_ref, dlc_ref, doml_ref, dgn_ref, dstate, b_scr):
        @pl.when(pl.program_id(1) == 0)
        def _():
            dstate[...] = jnp.zeros_like(dstate)
            dla_ref[...] = jnp.zeros_like(dla_ref)
            dlc_ref[...] = jnp.zeros_like(dlc_ref)
            doml_ref[...] = jnp.zeros_like(doml_ref)
            dgn_ref[...] = jnp.zeros_like(dgn_ref)

        rows = lax.broadcasted_iota(jnp.int32, (C, LANE), 0)
        la_v, lc_v, oml_v = la_ref[...], lc_ref[...], oml_ref[...]
        qz, fz = qz_ref[...], fz_ref[...]
        q, sq, k, s, g, c = _hgrn_gates(qz, fz, la_v, lc_v, oml_v)
        vf = iz_ref[...]
        v = vf.astype(BF16)

        ov, gz, dzv, gn = o_ref[...], gz_ref[...], dz_ref[...], gn_ref[...]
        r = lax.rsqrt(jnp.mean(ov * ov, axis=-1, keepdims=True) + EPS)
        on = ov * r
        sg = jax.nn.sigmoid(gz)
        silu_g = gz * sg
        dgn_ref[...] += jnp.sum(dzv * on * silu_g, axis=0, keepdims=True)
        du_ref[3] = (dzv * on * gn * (sg * (1.0 + gz * (1.0 - sg)))).astype(BF16)
        don = dzv * gn * silu_g
        do_f = r * (don - on * jnp.mean(don * on, axis=-1, keepdims=True))
        do = do_f.astype(BF16)

        b = _chunk_cumsum(g, rows)
        b_scr[...] = b
        bl = b_scr[C - 1:C, :]
        e = jnp.exp(b)
        ebl = jnp.exp(bl)
        ekl = jnp.exp(bl - b)
        qh = q * e
        kh = k * ekl
        st_v = st_ref[...]
        dst = dstate[...]
        dstb = dst.astype(BF16)

        du_ref[2] = (_dot(a_ref[...], do, "tn") + _dot(kh.astype(BF16), dstb, "nt")).astype(BF16)
        da = _dot(do, v, "nt")
        dqh = _dot(do, st_v.astype(BF16), "nn")
        dkh = _dot(v, dstb, "nn")
        dstate[...] = dst * ebl + _dot(do, qh.astype(BF16), "tn")
        dbl = jnp.sum(dkh * kh, axis=0, keepdims=True) + jnp.sum(dst * st_v, axis=0, keepdims=True) * ebl

        datt = jnp.sum(do_f * vf, axis=-1, keepdims=True)
        dqa = datt * k
        dka = datt * q
        lvl = lvl_ref[...]
        for m in levels:
            ez = jnp.exp(-jnp.abs(b - _level_ref(b, b_scr, rows, m)))
            ks2, qs2 = _split2(k * ez), _split2(q * ez)
            if m == HB:
                gm = da[HB:, :HB].astype(BF16)
                pq = jnp.concatenate([jnp.zeros((HB, 2 * LANE), F32), _dot(gm, ks2[:HB], "nn")], axis=0)
                pk = jnp.concatenate([_dot(gm, qs2[HB:], "tn"), jnp.zeros((HB, 2 * LANE), F32)], axis=0)
            else:
                gms = [jnp.where(lvl == m.bit_length() - 1, da[r * HB:(r + 1) * HB, r * HB:(r + 1) * HB], 0.0).astype(BF16)
                       for r in range(2)]
                pq = jnp.concatenate([_dot(gms[r], ks2[r * HB:(r + 1) * HB], "nn") for r in range(2)], axis=0)
                pk = jnp.concatenate([_dot(gms[r], qs2[r * HB:(r + 1) * HB], "tn") for r in range(2)], axis=0)
            dqa += (pq[:, :LANE] + pq[:, LANE:]) * ez
            dka += (pk[:, :LANE] + pk[:, LANE:]) * ez
        db = q * dqa - k * dka + dqh * qh - dkh * kh
        db = db + jnp.where(rows == C - 1, dbl, 0.0)
        dq = dqa + dqh * e
        dk = dka + dkh * ekl
        dg = _chunk_cumsum(db, rows, reverse=True)

        wa = jnp.exp(la_v - g)
        wc = jnp.exp(c - g)
        du_ref[0] = (dq * (sq * (1.0 + qz * (1.0 - sq)))).astype(BF16)
        du_ref[1] = (dg * wc * (1.0 - s) - dk * oml_v * s * (1.0 - s)).astype(BF16)
        dla_ref[...] += jnp.sum(dg * wa, axis=0, keepdims=True)
        dlc_ref[...] += jnp.sum(dg * wc, axis=0, keepdims=True)
        doml_ref[...] += jnp.sum(dk * (1.0 - s), axis=0, keepdims=True)

    def col(off):
        return pl.BlockSpec((C, LANE), lambda h, i: (nc - 1 - i, off * H + h))

    vec = pl.BlockSpec((1, LANE), lambda h, i: (0, h))
    tile = pl.BlockSpec((C, LANE), lambda h, i: (nc - 1 - i, h))
    a_spec = pl.BlockSpec((None, C, C), lambda h, i: (h, nc - 1 - i, 0))
    st_spec = pl.BlockSpec((None, None, LANE, LANE), lambda h, i: (h, nc - 1 - i, 0, 0))
    vw = jax.ShapeDtypeStruct((1, W), F32)
    return pl.pallas_call(
        body, name=name, grid=(H, nc),
        in_specs=[col(0), col(1), col(2), col(3), vec, vec, vec, vec, tile, a_spec, st_spec, tile,
                  pl.BlockSpec((HB, HB), lambda h, i: (0, 0))],
        out_specs=[pl.BlockSpec((4, C, LANE), lambda h, i: (0, nc - 1 - i, h)), vec, vec, vec, vec],
        out_shape=[jax.ShapeDtypeStruct((4, S, W), BF16), vw, vw, vw, vw],
        scratch_shapes=[pltpu.VMEM((LANE, LANE), F32), pltpu.VMEM((C, LANE), F32)],
        compiler_params=_params("parallel", "arbitrary"),
    )(u, u, u, u, la, lc, oml, gn_g, o, a, st, dz, _level_table(HB))


def _lb_terms(lb_logits, layer):
    p = jax.nn.softmax(lb_logits, axis=0)
    lb = (jnp.cumsum(p, axis=0) - p[0:1])[layer]
    return jnp.log(lb)[None], jnp.log1p(-lb)[None], (1.0 - lb)[None]


def kernel(x, norm_ffn1, ffn1_wg, ffn1_wu, ffn1_wd, norm_mix, norm_ffn2, ffn2_wg, ffn2_wu, ffn2_wd, ev_w_in, ev_conv_w, ev_conv_b, ev_cn_g, ev_cn_b, ev_qn_g, ev_kn_g, ev_w_out, od_w_in, od_lb_logits, od_gn_g, od_w_out, loss_target, m_norm_ffn1, m_ffn1_wg, m_ffn1_wu, m_ffn1_wd, m_norm_mix, m_norm_ffn2, m_ffn2_wg, m_ffn2_wu, m_ffn2_wd, m_ev_w_in, m_ev_conv_w, m_ev_conv_b, m_ev_cn_g, m_ev_cn_b, m_ev_qn_g, m_ev_kn_g, m_ev_w_out, m_od_w_in, m_od_lb_logits, m_od_gn_g, m_od_w_out, v_norm_ffn1, v_ffn1_wg, v_ffn1_wu, v_ffn1_wd, v_norm_mix, v_norm_ffn2, v_ffn2_wg, v_ffn2_wu, v_ffn2_wd, v_ev_w_in, v_ev_conv_w, v_ev_conv_b, v_ev_cn_g, v_ev_cn_b, v_ev_qn_g, v_ev_kn_g, v_ev_w_out, v_od_w_in, v_od_lb_logits, v_od_gn_g, v_od_w_out):
    depth = norm_ffn1.shape[0]
    S, D = x.shape[1], x.shape[2]
    xi, yi, ci = _me()
    dev = 4 * xi + 2 * yi + ci
    c_idx = jnp.reshape(ci, (1,)).astype(jnp.int32)
    k_idx = jnp.reshape(2 * xi + yi, (1,)).astype(jnp.int32)

    def ffn_shard(wg, wu, wd, l):
        return jnp.stack([wg[l].T, wu[l].T, wd[l]]).astype(BF16)

    assert depth == 2, "the exchange schedule below is written for one even and one odd layer"
    sh_ffn1 = [ffn_shard(ffn1_wg, ffn1_wu, ffn1_wd, l) for l in range(depth)]
    sh_ffn2 = [ffn_shard(ffn2_wg, ffn2_wu, ffn2_wd, l) for l in range(depth)]
    sh_ev = [ev_w_in[0].T.astype(BF16)[None], ev_w_out[0].astype(BF16)[None]]
    sh_od = [od_w_in[0].T.astype(BF16)[None], od_w_out[0].astype(BF16)[None]]

    def full(g):
        return g.reshape(g.shape[0], N_DEV * g.shape[2], g.shape[3])

    def gather_begin(shards, after, tag):
        lands = [lax.dynamic_update_slice(lax.empty((s.shape[0], N_DEV) + s.shape[1:], s.dtype), s[:, None],
                                          (0, dev, 0, 0)) for s in shards]
        state = _push_start(shards, lands, _gather_plan, after, "gather_start" + tag)
        return state, state[4][0, 0]

    def gather_arrived(state, after, tag):
        send, recv, srcs, lands, _ = state
        _, lands = _push_wait(send, recv, srcs, lands, _gather_plan, after, "gather_wait" + tag)
        state = _push_start([], lands, _forward_plan, None, "forward_start" + tag)
        return state, state[4][0, 0]

    def gather_done(state, after, tag):
        send, recv, _, lands, _ = state
        _, lands = _push_wait(send, recv, [], lands, _forward_plan, after, "forward_wait" + tag)
        return [full(g) for g in lands]

    def gather_end(state, after, tag):
        state, _ = gather_arrived(state, after, tag)
        return gather_done(state, state[4], tag)

    conv_w_sh, gn_g_sh = ev_conv_w[0], od_gn_g[0]
    cw, cs = conv_w_sh.shape[0], conv_w_sh.shape[1]
    gs = gn_g_sh.shape[0]
    conv_w_z = lax.dynamic_update_slice(jnp.zeros((cw, N_DEV * cs), F32), conv_w_sh, (0, dev * cs))
    gn_g_z = lax.dynamic_update_slice(jnp.zeros((N_DEV * gs,), F32), gn_g_sh, (dev * gs,))
    conv_w_full, gn_g_full = _unpack_rows(
        _all_reduce_small(_pack_rows([conv_w_z, gn_g_z]), "gather_small_params"),
        [conv_w_z.shape, gn_g_z.shape])

    w_ffn1, w_ffn2 = [None] * depth, [None] * depth
    pending, after = {}, conv_w_full
    for key, shards in (("0", [sh_ffn1[0]]), ("1", sh_ev), ("2", [sh_ffn2[0]]), ("3", [sh_ffn1[1]]), ("4", sh_od),
                        ("5", [sh_ffn2[1]])):
        pending[key], _ = gather_begin(shards, after, "_" + key)
        after = pending[key][4]
    start_tok = after[0, 0]
    (w_ffn1[0],) = gather_end(pending.pop("0"), after, "_0")

    def odd_mixer(u, l, tok):
        (la, lc, oml), lb_vjp = jax.vjp(functools.partial(_lb_terms, layer=l), od_lb_logits)
        gn = (gn_g_full + tok)[None]
        zb, o_raw, scores, states = _hgrn_fwd(u, la, lc, oml, gn, f"hgrn_fwd{l}")

        def backward(dz):
            du, dla, dlc, doml, dgn = _hgrn_bwd(u, la, lc, oml, gn, o_raw, scores, states, dz, f"hgrn_bwd{l}")
            (g_lb,) = lb_vjp((dla, dlc, doml))
            return du, [g_lb, dgn[0]]

        return zb, backward

    saved = []
    h = x[0]
    hn = _rms_fwd(h, (norm_ffn1[0] + start_tok)[None], "rms_a0")
    for l in range(depth):
        ffn, gu = _ffn_fwd(hn, w_ffn1[l], f"ffn_fwd_a{l}")
        s1 = (h, hn, gu)
        if l == 0:
            w_in, w_out = gather_end(pending.pop("1"), ffn, "_1")
        else:
            w_in, w_out = gather_done(pending.pop("4"), ffn, "_4")
        w_in_t, w_out = w_in[0], w_out[0]
        h, hn = _resid_rms(h, ffn, norm_mix[l][None], f"rms_mix{l}")
        u = _mm(hn, w_in_t, "nt", F32, f"mix_in{l}")
        key = "2" if l == 0 else "5"
        passing, tok = gather_arrived(pending.pop(key), u, "_" + key)
        if l % 2 == 0:
            zb, core_vjp = _even_mixer(u, conv_w_full, ev_conv_b + tok, ev_cn_g, ev_cn_b, ev_qn_g, ev_kn_g, str(l))
        else:
            zb, core_vjp = odd_mixer(u, l, tok)
        h_mix = h
        h = _mm(zb, w_out, "nn", F32, f"mix_out{l}", res=h)
        sm = (h_mix, hn, zb, core_vjp, w_in_t, w_out)
        (w_ffn2[l],) = gather_done(passing, h, "_" + key)
        tok = 0.0
        if l + 1 < depth:
            passing, tok = gather_arrived(pending.pop("3"), w_ffn2[l], "_3")
        hn = _rms_fwd(h, (norm_ffn2[l] + tok)[None], f"rms_b{l}")
        ffn, gu = _ffn_fwd(hn, w_ffn2[l], f"ffn_fwd_b{l}")
        saved.append((s1, sm, (h, hn, gu)))
        if l + 1 < depth:
            (w_ffn1[l + 1],) = gather_done(passing, ffn, "_3")
            pending["4"], tok = gather_arrived(pending.pop("4"), w_ffn1[l + 1], "_4")
            h, hn = _resid_rms(h, ffn, (norm_ffn1[l + 1] + tok)[None], f"rms_a{l + 1}")

    dy, loss_part = _loss_grad(h, ffn, loss_target[0], "loss_grad")

    def halves_begin(parts, tag):
        parts = [g.reshape(g.shape[0], 4, 2, g.shape[1] // N_DEV, g.shape[2]) for g in parts]
        lands = [lax.empty(g.shape[:2] + g.shape[3:], BF16) for g in parts]
        state = _push_start(parts, lands, _halves_plan, None, "halves_start" + tag)
        return state, state[4][0, 0]

    def chips_begin(state, after, tag):
        send, recv, srcs, lands, _ = state
        parts, got = _push_wait(send, recv, srcs, lands, _halves_plan, after, "halves_wait" + tag)
        sums = [_add_core_halves(g, r, c_idx, f"add_core_halves{tag}_{a}") for a, (g, r) in enumerate(zip(parts, got))]
        lands = [lax.empty((3, s.shape[0]) + s.shape[2:], BF16) for s in sums]
        state = _push_start(sums, lands, _chip_plan, None, "reduce_start" + tag)
        return state, state[4][0, 0]

    def reduce_end(state, after, tag):
        send, recv, srcs, lands, _ = state
        sums, got = _push_wait(send, recv, srcs, lands, _chip_plan, after, "reduce_wait" + tag)
        return [_sum_chip_blocks(s, r, k_idx, f"sum_chip_blocks{tag}_{a}") for a, (s, r) in enumerate(zip(sums, got))]

    def ffn_backward(dy, gain, w, sv, tag, on_dw, on_dx=None):
        h_in, hn, gu = sv
        dxn, dout, t = _ffn_bwd_dx(dy, w, gu, "ffn_bwd_dx_" + tag)
        tok = 0.0 if on_dx is None else on_dx(dxn)
        tok = tok + on_dw(_ffn_bwd_dw(hn, dout, t, "ffn_bwd_dw_" + tag))
        dx, dgain = _rms_bwd(h_in, (gain + tok)[None], dxn, dy, "rms_bwd_" + tag)
        return dx, dgain[0]

    g_norm1, g_norm2, g_normm = [None] * depth, [None] * depth, [None] * depth
    small, halves, groups = [None, None], {}, {}

    def start_halves(key, make_parts):
        def hook(dw):
            halves[key], tok = halves_begin(make_parts(dw), "_" + key)
            return tok
        return hook

    def start_chips(key):
        def hook(after):
            groups[key], tok = chips_begin(halves.pop(key), after, "_" + key)
            return tok
        return hook

    for l in reversed(range(depth)):
        s1, (h_mix, hn, zb, core_vjp, w_in_t, w_out), s2 = saved[l]
        if l == 1:
            dy, g_norm2[l] = ffn_backward(dy, norm_ffn2[l], w_ffn2[l], s2, f"b{l}", start_halves("1", lambda dw: [dw]))
        else:
            dy, g_norm2[l] = ffn_backward(dy, norm_ffn2[l], w_ffn2[l], s2, f"b{l}", start_halves("3", lambda dw: [dw]),
                                          start_chips("2"))
        dyb = dy.astype(BF16)
        dz = _mm(dyb, w_out, "nt", F32, f"mix_out_dz{l}")
        dw_out = _mm(zb, dyb, "tn", BF16, f"mix_out_dw{l}")
        dub, small[l % 2] = core_vjp(dz)
        dw_in_t = _mm(dub, hn, "tn", BF16, f"mix_in_dw{l}")
        mix_parts = [dw_in_t[None], dw_out[None]]
        if l == 1:
            tok = start_chips("1")(dw_in_t)
        else:
            tok = start_chips("3")(dw_in_t) + start_halves("4", lambda _: mix_parts)(None)
        dhn = _mm(dub, w_in_t, "nn", F32, f"mix_in_dx{l}")
        dy, gm = _rms_bwd(h_mix, (norm_mix[l] + tok)[None], dhn, dy, f"rms_bwd_mix{l}")
        g_normm[l] = gm[0]
        if l == 1:
            dy, g_norm1[l] = ffn_backward(dy, norm_ffn1[l], w_ffn1[l], s1, f"a{l}",
                                          start_halves("2", lambda dw, od=mix_parts: od + [dw]))
        else:
            dy, g_norm1[l] = ffn_backward(dy, norm_ffn1[l], w_ffn1[l], s1, f"a{l}", start_halves("5", lambda dw: [dw]),
                                          start_chips("4"))
    grad_x = dy[None]
    start_chips("5")(dy)

    done = [dy, groups["5"][4]]
    (g_ffn2_1,) = reduce_end(groups["1"], done, "_1")
    g_od_in_t, g_od_out, g_ffn1_1 = reduce_end(groups["2"], done, "_2")
    (g_ffn2_0,) = reduce_end(groups["3"], done, "_3")
    g_ev_in_t, g_ev_out = reduce_end(groups["4"], done, "_4")
    g_ffn2 = [g_ffn2_0, g_ffn2_1]

    def ffn_grads(gl):
        return (jnp.stack([g[0].T for g in gl]), jnp.stack([g[1].T for g in gl]), jnp.stack([g[2] for g in gl]))

    g_ffn2_wg, g_ffn2_wu, g_ffn2_wd = ffn_grads(g_ffn2)
    grads = [None, None, None, None, None, None, g_ffn2_wg, g_ffn2_wu, g_ffn2_wd,
             g_ev_in_t[0].T[None], None, None, None, None, None,
             None, g_ev_out, g_od_in_t[0].T[None], None, None, g_od_out]
    weights = [norm_ffn1, ffn1_wg, ffn1_wu, ffn1_wd, norm_mix, norm_ffn2, ffn2_wg, ffn2_wu, ffn2_wd, ev_w_in,
               ev_conv_w, ev_conv_b, ev_cn_g, ev_cn_b, ev_qn_g, ev_kn_g, ev_w_out, od_w_in, od_lb_logits,
               od_gn_g, od_w_out]
    moms = [m_norm_ffn1, m_ffn1_wg, m_ffn1_wu, m_ffn1_wd, m_norm_mix, m_norm_ffn2, m_ffn2_wg, m_ffn2_wu,
            m_ffn2_wd, m_ev_w_in, m_ev_conv_w, m_ev_conv_b, m_ev_cn_g, m_ev_cn_b, m_ev_qn_g, m_ev_kn_g,
            m_ev_w_out, m_od_w_in, m_od_lb_logits, m_od_gn_g, m_od_w_out]
    vars_ = [v_norm_ffn1, v_ffn1_wg, v_ffn1_wu, v_ffn1_wd, v_norm_mix, v_norm_ffn2, v_ffn2_wg, v_ffn2_wu,
             v_ffn2_wd, v_ev_w_in, v_ev_conv_w, v_ev_conv_b, v_ev_cn_g, v_ev_cn_b, v_ev_qn_g, v_ev_kn_g,
             v_ev_w_out, v_od_w_in, v_od_lb_logits, v_od_gn_g, v_od_w_out]
    n_w = len(weights)
    deltas, new_m, new_v = [None] * n_w, [None] * n_w, [None] * n_w

    def update(idx):
        for i in idx:
            deltas[i], new_m[i], new_v[i] = _adamw(weights[i], grads[i], moms[i], vars_[i], f"adamw{i}")

    update([i for i in range(n_w) if grads[i] is not None])
    g_conv_w, g_conv_b, g_cn_g, g_cn_b, g_qn_g, g_kn_g = small[0]
    g_lb, g_gn = small[1]
    parts = [jnp.stack(g_norm1), jnp.stack(g_normm), jnp.stack(g_norm2), g_conv_b, g_cn_g, g_cn_b,
             g_qn_g, g_kn_g, g_lb, g_conv_w, g_gn, loss_part[0, :1]]
    red = _unpack_rows(_all_reduce_small(_pack_rows(parts), "reduce_small_grads", [d for d in deltas if d is not None]),
                       [p.shape for p in parts])
    g_norm1, g_normm, g_norm2, g_conv_b, g_cn_g, g_cn_b, g_qn_g, g_kn_g, g_lb, g_conv_w, g_gn, loss = red
    g_conv_w = lax.dynamic_slice(g_conv_w, (0, dev * cs), (cw, cs))
    g_gn = lax.dynamic_slice(g_gn, (dev * gs,), (gs,))
    small_idx = {0: g_norm1, 4: g_normm, 5: g_norm2, 10: g_conv_w[None], 11: g_conv_b[None], 12: g_cn_g[None],
                 13: g_cn_b[None], 14: g_qn_g[None], 15: g_kn_g[None], 18: g_lb, 19: g_gn[None]}
    for i, g in small_idx.items():
        grads[i] = g
    update(small_idx)
    (g_ffn1_0,) = reduce_end(groups["5"], [d for d in deltas if d is not None], "_5")
    grads[1], grads[2], grads[3] = ffn_grads([g_ffn1_0, g_ffn1_1])
    update((1, 2, 3))
    return (loss[0], grad_x, *grads, *deltas, *new_m, *new_v)
```

```python
import functools
import math

import jax
import jax.numpy as jnp
from jax import lax
from jax.experimental import pallas as pl
from jax.experimental.pallas import tpu as pltpu

F32 = jnp.float32
BF16 = jnp.bfloat16
MESH = pl.DeviceIdType.MESH
N_DEV = 8

EPS = 1e-6
HEAD_DIM = 128
CONV_WIDTH = 31
DIL_PATTERNS = ((128, 1), (512, 4), (2048, 16))
Q_BLOCK = 128
ROPE_THETA = 10000.0
HGRN_KDIM = 128
HGRN_CHUNK = 256

ADAM_LR = 0.001
ADAM_B1 = 0.9
ADAM_B2 = 0.999
ADAM_EPS = 1e-08
ADAM_WD = 0.01
ADAM_STEP = 10

VMEM_LIMIT_BYTES = 56 * 1024 * 1024
LANE = 128
SUBLANE_BF16 = 16

ANY = pl.BlockSpec(memory_space=pl.ANY)


def _tile(n, pref, mult):
    t = (min(pref, n) // mult) * mult
    while t > 0:
        if n % t == 0:
            return t
        t -= mult
    return n


def _params(*sem):
    return pltpu.CompilerParams(dimension_semantics=sem, vmem_limit_bytes=VMEM_LIMIT_BYTES)


_DOT_DIMS = {
    "nn": (((1,), (0,)), ((), ())),
    "nt": (((1,), (1,)), ((), ())),
    "tn": (((0,), (0,)), ((), ())),
}


def _dot(a, b, mode):
    return lax.dot_general(a, b, _DOT_DIMS[mode], preferred_element_type=F32)


def _mm(a, b, mode, out_dtype, name, res=None, tm=1024, tn=1024, tk=2048):
    parts, a_shape = (a.shape[0], (a.shape[1], a.shape[0] * a.shape[2])) if a.ndim == 3 else (1, a.shape)
    if mode == "nt":
        (M, K), N = a_shape, b.shape[0]
    elif mode == "nn":
        (M, K), N = a_shape, b.shape[1]
    else:
        (K, M), N = a_shape, b.shape[1]
    tm, tn, tk = _tile(M, tm, LANE), _tile(N, tn, LANE), _tile(K, tk, LANE)
    nk = K // tk

    def body(*refs):
        if res is None:
            a_ref, b_ref, o_ref, acc = refs
        else:
            a_ref, b_ref, r_ref, o_ref, acc = refs
        k = pl.program_id(2)

        @pl.when(k == 0)
        def _():
            acc[...] = jnp.zeros_like(acc)

        acc[...] += _dot(a_ref[...].astype(BF16), b_ref[...].astype(BF16), mode)

        @pl.when(k == nk - 1)
        def _():
            r = acc[...]
            if res is not None:
                r = r_ref[...] + r
            o_ref[...] = r.astype(out_dtype)

    if parts > 1:
        assert mode in ("nn", "tn") and a.shape[2] % (tk if mode == "nn" else tm) == 0
        per = a.shape[2] // (tk if mode == "nn" else tm)
        a_spec = {"nn": pl.BlockSpec((None, tm, tk), lambda i, j, k: (k // per, i, k % per)),
                  "tn": pl.BlockSpec((None, tk, tm), lambda i, j, k: (i // per, k, i % per))}[mode]
    else:
        a_spec = {"nt": pl.BlockSpec((tm, tk), lambda i, j, k: (i, k)),
                  "nn": pl.BlockSpec((tm, tk), lambda i, j, k: (i, k)),
                  "tn": pl.BlockSpec((tk, tm), lambda i, j, k: (k, i))}[mode]
    b_spec = {"nt": pl.BlockSpec((tn, tk), lambda i, j, k: (j, k)),
              "nn": pl.BlockSpec((tk, tn), lambda i, j, k: (k, j)),
              "tn": pl.BlockSpec((tk, tn), lambda i, j, k: (k, j))}[mode]
    o_spec = pl.BlockSpec((tm, tn), lambda i, j, k: (i, j))
    in_specs = [a_spec, b_spec] + ([o_spec] if res is not None else [])
    args = (a, b) + ((res,) if res is not None else ())
    return pl.pallas_call(
        body, name=name, grid=(M // tm, N // tn, nk),
        in_specs=in_specs, out_specs=o_spec,
        out_shape=jax.ShapeDtypeStruct((M, N), out_dtype),
        scratch_shapes=[pltpu.VMEM((tm, tn), F32)],
        compiler_params=_params("parallel", "parallel", "arbitrary"),
    )(*args)


def _rms_fwd(x, gain, name):
    S, D = x.shape
    tm = _tile(S, 512, SUBLANE_BF16)

    def body(x_ref, g_ref, o_ref):
        xv = x_ref[...]
        r = lax.rsqrt(jnp.mean(xv * xv, axis=-1, keepdims=True) + EPS)
        o_ref[...] = (xv * r * g_ref[...]).astype(BF16)

    return pl.pallas_call(
        body, name=name, grid=(S // tm,),
        in_specs=[pl.BlockSpec((tm, D), lambda i: (i, 0)), pl.BlockSpec((1, D), lambda i: (0, 0))],
        out_specs=pl.BlockSpec((tm, D), lambda i: (i, 0)),
        out_shape=jax.ShapeDtypeStruct((S, D), BF16),
        compiler_params=_params("parallel"),
    )(x, gain)


def _resid_rms(x, ffn, gain, name):
    S, D = x.shape
    tm = _tile(S, 512, SUBLANE_BF16)

    def body(x_ref, f_ref, g_ref, h_ref, o_ref):
        hv = x_ref[...] + 0.5 * f_ref[...]
        h_ref[...] = hv
        r = lax.rsqrt(jnp.mean(hv * hv, axis=-1, keepdims=True) + EPS)
        o_ref[...] = (hv * r * g_ref[...]).astype(BF16)

    row = pl.BlockSpec((tm, D), lambda i: (i, 0))
    return pl.pallas_call(
        body, name=name, grid=(S // tm,),
        in_specs=[row, row, pl.BlockSpec((1, D), lambda i: (0, 0))], out_specs=[row, row],
        out_shape=[jax.ShapeDtypeStruct((S, D), F32), jax.ShapeDtypeStruct((S, D), BF16)],
        compiler_params=_params("parallel"),
    )(x, ffn, gain)


def _rms_bwd(x, gain, dxn, dy, name, with_bf16=False):
    S, D = x.shape
    tm = _tile(S, 512, SUBLANE_BF16)

    def body(x_ref, g_ref, dxn_ref, dy_ref, dx_ref, dg_ref, *dxb_ref):
        @pl.when(pl.program_id(0) == 0)
        def _():
            dg_ref[...] = jnp.zeros_like(dg_ref)

        xv = x_ref[...]
        r = lax.rsqrt(jnp.mean(xv * xv, axis=-1, keepdims=True) + EPS)
        xh = xv * r
        dxn_v = dxn_ref[...]
        dg_ref[...] += jnp.sum(dxn_v * xh, axis=0, keepdims=True)
        dxh = dxn_v * g_ref[...]
        dx = dy_ref[...] + r * (dxh - xh * jnp.mean(dxh * xh, axis=-1, keepdims=True))
        dx_ref[...] = dx
        if with_bf16:
            dxb_ref[0][...] = dx.astype(BF16)

    row = pl.BlockSpec((tm, D), lambda i: (i, 0))
    vec = pl.BlockSpec((1, D), lambda i: (0, 0))
    return pl.pallas_call(
        body, name=name, grid=(S // tm,),
        in_specs=[row, vec, row, row], out_specs=[row, vec] + [row] * with_bf16,
        out_shape=[jax.ShapeDtypeStruct((S, D), F32), jax.ShapeDtypeStruct((1, D), F32)]
        + [jax.ShapeDtypeStruct((S, D), BF16)] * with_bf16,
        compiler_params=_params("arbitrary"),
    )(x, gain, dxn, dy)


def _ffn_fwd(xn, w, name):
    S, D = xn.shape
    F = w.shape[1]
    tm, tf = _tile(S, 1024, SUBLANE_BF16), _tile(F, 512, LANE)
    nf = F // tf

    def body(xn_ref, w_ref, o_ref, gu_ref):
        @pl.when(pl.program_id(1) == 0)
        def _():
            o_ref[...] = jnp.zeros_like(o_ref)

        xnv = xn_ref[...]
        g = _dot(xnv, w_ref[0], "nt")
        u = _dot(xnv, w_ref[1], "nt")
        gu_ref[0] = g.astype(BF16)
        gu_ref[1] = u.astype(BF16)
        h = (g * jax.nn.sigmoid(g) * u).astype(BF16)
        o_ref[...] += _dot(h, w_ref[2], "nn")

    row = pl.BlockSpec((tm, D), lambda i, f: (i, 0))
    return pl.pallas_call(
        body, name=name, grid=(S // tm, nf),
        in_specs=[row, pl.BlockSpec((3, tf, D), lambda i, f: (0, f, 0))],
        out_specs=[row, pl.BlockSpec((2, tm, tf), lambda i, f: (0, i, f))],
        out_shape=[jax.ShapeDtypeStruct((S, D), F32), jax.ShapeDtypeStruct((2, S, F), BF16)],
        compiler_params=_params("parallel", "arbitrary"),
    )(xn, w)


def _ffn_bwd_dx(dy, w, gu, name):
    S, D = dy.shape
    F = w.shape[1]
    tm, tf = _tile(S, 1024, SUBLANE_BF16), _tile(F, 512, LANE)
    nf = F // tf

    def body(dy_ref, w_ref, gu_ref, dxn_ref, dout_ref, t_ref):
        @pl.when(pl.program_id(1) == 0)
        def _():
            dxn_ref[...] = jnp.zeros_like(dxn_ref)
            dout_ref[...] = (0.5 * dy_ref[...]).astype(BF16)

        g = gu_ref[0].astype(F32)
        u = gu_ref[1].astype(F32)
        sig = jax.nn.sigmoid(g)
        silu = g * sig
        t_ref[2] = (silu * u).astype(BF16)
        dh = _dot(dout_ref[...], w_ref[2], "nt")
        dg = (dh * (u * (sig * (1.0 + g * (1.0 - sig))))).astype(BF16)
        du = (dh * silu).astype(BF16)
        t_ref[0] = dg
        t_ref[1] = du
        dxn_ref[...] += _dot(dg, w_ref[0], "nn") + _dot(du, w_ref[1], "nn")

    row = pl.BlockSpec((tm, D), lambda i, f: (i, 0), pipeline_mode=pl.Buffered(1))
    return pl.pallas_call(
        body, name=name, grid=(S // tm, nf),
        in_specs=[row, pl.BlockSpec((3, tf, D), lambda i, f: (0, f, 0)),
                  pl.BlockSpec((2, tm, tf), lambda i, f: (0, i, f))],
        out_specs=[row, row, pl.BlockSpec((3, tm, tf), lambda i, f: (0, i, f))],
        out_shape=[jax.ShapeDtypeStruct((S, D), F32), jax.ShapeDtypeStruct((S, D), BF16),
                   jax.ShapeDtypeStruct((3, S, F), BF16)],
        compiler_params=_params("parallel", "arbitrary"),
    )(dy, w, gu)


def _ffn_bwd_dw(xn, dout, t, name):
    S, D = xn.shape
    F = t.shape[2]
    ts, tf = _tile(S, 1024, LANE), _tile(F, 512, LANE)
    ns = S // ts

    def body(xn_ref, dout_ref, t_ref, dw_ref, acc):
        s = pl.program_id(1)

        @pl.when(s == 0)
        def _():
            acc[...] = jnp.zeros_like(acc)

        xnv = xn_ref[...]
        acc[0] += _dot(t_ref[0], xnv, "tn")
        acc[1] += _dot(t_ref[1], xnv, "tn")
        acc[2] += _dot(t_ref[2], dout_ref[...], "tn")

        @pl.when(s == ns - 1)
        def _():
            dw_ref[...] = acc[...].astype(BF16)

    row = pl.BlockSpec((ts, D), lambda f, s: (s, 0))
    return pl.pallas_call(
        body, name=name, grid=(F // tf, ns),
        in_specs=[row, row, pl.BlockSpec((3, ts, tf), lambda f, s: (0, s, f))],
        out_specs=pl.BlockSpec((3, tf, D), lambda f, s: (0, f, 0)),
        out_shape=jax.ShapeDtypeStruct((3, F, D), BF16),
        scratch_shapes=[pltpu.VMEM((3, tf, D), F32)],
        compiler_params=_params("parallel", "arbitrary"),
    )(xn, dout, t)


def _loss_grad(x, ffn, target, name):
    S, D = x.shape
    tm = _tile(S, 512, 8)

    def body(x_ref, f_ref, t_ref, dy_ref, l_ref):
        @pl.when(pl.program_id(0) == 0)
        def _():
            l_ref[...] = jnp.zeros_like(l_ref)

        e = (x_ref[...] + 0.5 * f_ref[...]) - t_ref[...]
        dy_ref[...] = e * (1.0 / D)
        l_ref[...] += 0.5 * jnp.sum(jnp.sum(e * e, axis=-1, keepdims=True) * (1.0 / D))

    row = pl.BlockSpec((tm, D), lambda i: (i, 0))
    one = pl.BlockSpec((8, LANE), lambda i: (0, 0))
    return pl.pallas_call(
        body, name=name, grid=(S // tm,),
        in_specs=[row, row, row], out_specs=[row, one],
        out_shape=[jax.ShapeDtypeStruct((S, D), F32), jax.ShapeDtypeStruct((8, LANE), F32)],
        compiler_params=_params("arbitrary"),
    )(x, ffn, target)


def _adamw(w, g, m, v, name):
    shape = w.shape
    C = shape[-1]
    R = math.prod(shape[:-1])
    tr = _tile(R, max(8, (1 << 19) // C // 8 * 8), 8)
    c1 = 1.0 / (1.0 - ADAM_B1 ** ADAM_STEP)
    c2 = 1.0 / (1.0 - ADAM_B2 ** ADAM_STEP)

    def body(w_ref, g_ref, m_ref, v_ref, d_ref, nm_ref, nv_ref):
        gv = g_ref[...]
        nm = ADAM_B1 * m_ref[...] + (1.0 - ADAM_B1) * gv
        nv = ADAM_B2 * v_ref[...] + (1.0 - ADAM_B2) * (gv * gv)
        nm_ref[...] = nm
        nv_ref[...] = nv
        d_ref[...] = -ADAM_LR * ((nm * c1) / (jnp.sqrt(nv * c2) + ADAM_EPS) + ADAM_WD * w_ref[...])

    blk = pl.BlockSpec((tr, C), lambda i: (i, 0))
    sds = jax.ShapeDtypeStruct((R, C), F32)
    outs = pl.pallas_call(
        body, name=name, grid=(R // tr,),
        in_specs=[blk] * 4, out_specs=[blk] * 3, out_shape=[sds] * 3,
        compiler_params=_params("parallel"),
    )(*(a.reshape(R, C) for a in (w, g, m, v)))
    return tuple(o.reshape(shape) for o in outs)


def _me():
    return lax.axis_index("x"), lax.axis_index("y"), lax.axis_index("c")


def _add_core_halves(grad, got, c_idx, name):
    n, nk, _, r, C = grad.shape
    tr = _tile(r, 1024, SUBLANE_BF16)

    def body(c_ref, g_ref, r_ref, o_ref):
        o_ref[...] = (g_ref[...].astype(F32) + r_ref[...].astype(F32)).astype(BF16)

    return pl.pallas_call(
        body, name=name,
        grid_spec=pltpu.PrefetchScalarGridSpec(
            num_scalar_prefetch=1, grid=(n, nk, r // tr),
            in_specs=[pl.BlockSpec((None, None, None, tr, C), lambda i, k, t, c: (i, k, c[0], t, 0)),
                      pl.BlockSpec((None, None, tr, C), lambda i, k, t, c: (i, k, t, 0))],
            out_specs=pl.BlockSpec((None, None, tr, C), lambda i, k, t, c: (i, k, t, 0))),
        out_shape=jax.ShapeDtypeStruct((n, nk, r, C), BF16),
        compiler_params=_params("parallel", "parallel", "parallel"),
    )(c_idx, grad, got)


HBM = pl.BlockSpec(memory_space=pltpu.HBM)
SEM = pl.BlockSpec(memory_space=pltpu.SEMAPHORE)
EFFECT = pltpu.SideEffectType.DATAFLOW_SIDE_EFFECTING


def _push_start(srcs, lands, plan, after, name):
    ns, nl = len(srcs), len(lands)
    ncp = len(plan([None] * ns, [None] * nl, dry=True))
    extra = [] if after is None else [after]

    def body(*refs):
        src_refs, land_refs = refs[:ns], refs[ns:ns + nl]
        send_sems, recv_sems = refs[ns + nl + len(extra)], refs[ns + nl + len(extra) + 1]
        token = refs[-1]
        for i, (s, d, to) in enumerate(plan(src_refs, land_refs)):
            pltpu.make_async_remote_copy(src_ref=s, dst_ref=d, send_sem=send_sems.at[i], recv_sem=recv_sems.at[i],
                                         device_id=to, device_id_type=MESH).start()
        token[...] = jnp.zeros_like(token)

    out = pl.pallas_call(
        body, name=name,
        out_shape=(pltpu.SemaphoreType.DMA((ncp,)), pltpu.SemaphoreType.DMA((ncp,)),
                   *[pltpu.HBM(a.shape, a.dtype) for a in srcs], *[pltpu.HBM(a.shape, a.dtype) for a in lands],
                   jax.ShapeDtypeStruct((8, LANE), F32)),
        in_specs=[HBM] * (ns + nl) + [ANY] * len(extra),
        out_specs=(SEM, SEM, *[HBM] * (ns + nl), pl.BlockSpec(memory_space=pltpu.VMEM)),
        input_output_aliases={i: 2 + i for i in range(ns + nl)},
        compiler_params=pltpu.CompilerParams(has_side_effects=EFFECT),
    )(*[pltpu.with_memory_space_constraint(a, pltpu.HBM) for a in srcs + lands], *extra)
    return out[0], out[1], list(out[2:2 + ns]), list(out[2 + ns:2 + ns + nl]), out[-1]


def _push_wait(send_sems, recv_sems, srcs, lands, plan, after, name):
    ns, nl = len(srcs), len(lands)
    after = list(after) if isinstance(after, (list, tuple)) else [after]

    def body(*refs):
        src_refs, land_refs = refs[:ns], refs[ns:ns + nl]
        send, recv = refs[ns + nl], refs[ns + nl + 1]
        for i, (s, d, to) in enumerate(plan(src_refs, land_refs)):
            cp = pltpu.make_async_remote_copy(src_ref=s, dst_ref=d, send_sem=send.at[i], recv_sem=recv.at[i],
                                              device_id=to, device_id_type=MESH)
            cp.wait_send()
            cp.wait_recv()

    out = pl.pallas_call(
        body, name=name,
        out_shape=tuple(pltpu.HBM(a.shape, a.dtype) for a in srcs + lands),
        in_specs=[HBM] * (ns + nl) + [SEM, SEM] + [ANY] * len(after),
        out_specs=tuple([HBM] * (ns + nl)),
        input_output_aliases={i: i for i in range(ns + nl)},
        compiler_params=pltpu.CompilerParams(has_side_effects=EFFECT),
    )(*srcs, *lands, send_sems, recv_sems, *after)
    return list(out[:ns]), list(out[ns:])


def _gather_plan(src_refs, land_refs, dry=False):
    if dry:
        return [None] * (4 * len(src_refs))
    x, y, c = _me()
    me = 4 * x + 2 * y + c
    targets = [(x, y, 1 - c), (1 - x, y, c), (x, 1 - y, c), (1 - x, 1 - y, c)]
    return [(s, l.at[:, me], to) for s, l in zip(src_refs, land_refs) for to in targets]


def _halves_plan(src_refs, land_refs, dry=False):
    if dry:
        return [None] * len(src_refs)
    x, y, c = _me()
    return [(s.at[:, :, 1 - c], l, (x, y, 1 - c)) for s, l in zip(src_refs, land_refs)]


def _chip_plan(src_refs, land_refs, dry=False):
    if dry:
        return [None] * (3 * len(src_refs))
    x, y, c = _me()
    chips = [(1 - x, y), (x, 1 - y), (1 - x, 1 - y)]
    return [(s.at[:, 2 * chip[0] + chip[1]], l.at[j], (*chip, c))
            for s, l in zip(src_refs, land_refs) for j, chip in enumerate(chips)]


def _forward_plan(src_refs, land_refs, dry=False):
    if dry:
        return [None] * (3 * len(land_refs))
    x, y, c = _me()
    chips = [(1 - x, y), (x, 1 - y), (1 - x, 1 - y)]
    plan = []
    for l in land_refs:
        for chip in chips:
            blk = l.at[:, 4 * chip[0] + 2 * chip[1] + c]
            plan.append((blk, blk, (x, y, 1 - c)))
    return plan


def _sum_chip_blocks(sums, got, k_idx, name):
    n, _, r, C = sums.shape
    tr = _tile(r, 512, SUBLANE_BF16)

    def body(k_ref, s_ref, r_ref, o_ref):
        acc = s_ref[...].astype(F32)
        for j in range(3):
            acc = acc + r_ref[j].astype(F32)
        o_ref[...] = acc

    return pl.pallas_call(
        body, name=name,
        grid_spec=pltpu.PrefetchScalarGridSpec(
            num_scalar_prefetch=1, grid=(n, r // tr),
            in_specs=[pl.BlockSpec((None, None, tr, C), lambda i, t, k: (i, k[0], t, 0)),
                      pl.BlockSpec((3, None, tr, C), lambda i, t, k: (0, i, t, 0))],
            out_specs=pl.BlockSpec((None, tr, C), lambda i, t, k: (i, t, 0))),
        out_shape=jax.ShapeDtypeStruct((n, r, C), F32),
        compiler_params=_params("parallel", "parallel"),
    )(k_idx, sums, got)


def _all_reduce_small(v, name, after=()):
    R = v.shape[0]
    after = list(after)

    def body(*refs):
        v_ref = refs[0]
        o_ref, buf, send_sems, recv_sems = refs[1 + len(after):]
        x, y, c = _me()
        me = 4 * x + 2 * y + c
        buf[me] = v_ref[...]
        copies = []
        for k in range(1, N_DEV):
            peer = (x ^ (k >> 2), y ^ ((k >> 1) & 1), c ^ (k & 1))
            copies.append(pltpu.make_async_remote_copy(
                src_ref=v_ref, dst_ref=buf.at[me],
                send_sem=send_sems.at[k - 1], recv_sem=recv_sems.at[k - 1],
                device_id=peer, device_id_type=MESH))
        for cp in copies:
            cp.start()
        for cp in copies:
            cp.wait()
        acc = buf[0]
        for d in range(1, N_DEV):
            acc = acc + buf[d]
        o_ref[...] = acc

    vm = pl.BlockSpec(memory_space=pltpu.VMEM)
    return pl.pallas_call(
        body, name=name, in_specs=[vm] + [ANY] * len(after), out_specs=vm,
        out_shape=jax.ShapeDtypeStruct((R, LANE), F32),
        scratch_shapes=[pltpu.VMEM((N_DEV, R, LANE), F32),
                        pltpu.SemaphoreType.DMA((N_DEV - 1,)), pltpu.SemaphoreType.DMA((N_DEV - 1,))],
        compiler_params=pltpu.CompilerParams(vmem_limit_bytes=VMEM_LIMIT_BYTES),
    )(v, *after)


def _pack_rows(parts):
    flat = jnp.concatenate([p.reshape(-1).astype(F32) for p in parts])
    n = flat.shape[0]
    rows = -(-n // (8 * LANE)) * 8
    flat = jnp.pad(flat, (0, rows * LANE - n))
    return flat.reshape(rows, LANE)


def _unpack_rows(packed, shapes):
    flat = packed.reshape(-1)
    out, off = [], 0
    for s in shapes:
        n = math.prod(s)
        out.append(flat[off:off + n].reshape(s))
        off += n
    return out


CONV_HALO = 32


def _conv_fwd(u, conv_w, conv_b, cn_g, cn_b, name):
    S = u.shape[0]
    C = conv_w.shape[1]
    T = _tile(S, 256, CONV_HALO)
    hb = T // CONV_HALO

    def body(av_ref, ag_ref, pv_ref, pg_ref, w_ref, b_ref, g_ref, bb_ref, out_ref, y_ref, scr):
        i = pl.program_id(0)
        prev = pv_ref[...] * jax.nn.sigmoid(pg_ref[...])
        scr[0:CONV_HALO, :] = jnp.where(i > 0, prev, 0.0)
        scr[CONV_HALO:CONV_HALO + T, :] = av_ref[...] * jax.nn.sigmoid(ag_ref[...])
        for s in range(C // LANE):
            sl = slice(s * LANE, (s + 1) * LANE)
            acc = jnp.broadcast_to(b_ref[:, sl], (T, LANE))
            for j in range(CONV_WIDTH):
                acc = acc + w_ref[j:j + 1, sl] * scr[pl.ds(CONV_HALO - (CONV_WIDTH - 1) + j, T), sl]
            y_ref[:, sl] = acc
        acc = y_ref[...]
        mu = jnp.mean(acc, axis=-1, keepdims=True)
        xc = acc - mu
        var = jnp.mean(xc * xc, axis=-1, keepdims=True)
        ln = xc * lax.rsqrt(var + EPS) * g_ref[...] + bb_ref[...]
        out_ref[...] = (ln * jax.nn.sigmoid(ln)).astype(BF16)

    def cur(cb):
        return pl.BlockSpec((T, C), lambda i: (i, cb))

    def halo(cb):
        return pl.BlockSpec((CONV_HALO, C), lambda i: (jnp.maximum(i * hb - 1, 0), cb))

    vec = pl.BlockSpec((1, C), lambda i: (0, 0))
    return pl.pallas_call(
        body, name=name, grid=(S // T,),
        in_specs=[cur(0), cur(1), halo(0), halo(1), pl.BlockSpec((CONV_WIDTH, C), lambda i: (0, 0)), vec, vec, vec],
        out_specs=[pl.BlockSpec((T, C), lambda i: (i, 0))] * 2,
        out_shape=[jax.ShapeDtypeStruct((S, C), BF16), jax.ShapeDtypeStruct((S, C), F32)],
        scratch_shapes=[pltpu.VMEM((T + CONV_HALO, C), F32)],
        compiler_params=_params("parallel"),
    )(u, u, u, u, conv_w, conv_b, cn_g, cn_b)


def _conv_bwd_norm(dz, y, cn_g, cn_b, name):
    S, C = y.shape
    T = _tile(S, 256, 8)

    def body(dz_ref, y_ref, g_ref, bb_ref, dy_ref, dg_ref, db_ref):
        @pl.when(pl.program_id(0) == 0)
        def _():
            dg_ref[...] = jnp.zeros_like(dg_ref)
            db_ref[...] = jnp.zeros_like(db_ref)

        yv = y_ref[...]
        mu = jnp.mean(yv, axis=-1, keepdims=True)
        xc = yv - mu
        rstd = lax.rsqrt(jnp.mean(xc * xc, axis=-1, keepdims=True) + EPS)
        xh = xc * rstd
        ln = xh * g_ref[...] + bb_ref[...]
        sg = jax.nn.sigmoid(ln)
        dln = dz_ref[...] * (sg * (1.0 + ln * (1.0 - sg)))
        dg_ref[...] += jnp.sum(dln * xh, axis=0, keepdims=True)
        db_ref[...] += jnp.sum(dln, axis=0, keepdims=True)
        dxh = dln * g_ref[...]
        dy_ref[...] = rstd * (dxh - jnp.mean(dxh, axis=-1, keepdims=True)
                              - xh * jnp.mean(dxh * xh, axis=-1, keepdims=True))

    row = pl.BlockSpec((T, C), lambda i: (i, 0))
    vec = pl.BlockSpec((1, C), lambda i: (0, 0))
    return pl.pallas_call(
        body, name=name, grid=(S // T,),
        in_specs=[row, row, vec, vec], out_specs=[row, vec, vec],
        out_shape=[jax.ShapeDtypeStruct((S, C), F32), jax.ShapeDtypeStruct((1, C), F32),
                   jax.ShapeDtypeStruct((1, C), F32)],
        compiler_params=_params("arbitrary"),
    )(dz, y, cn_g, cn_b)


def _conv_bwd_taps(u, dy, conv_w, name):
    S, C = dy.shape
    T = _tile(S, 256, CONV_HALO)
    hb = T // CONV_HALO
    nt = S // T
    ns = C // LANE
    W1 = CONV_WIDTH - 1

    def body(av_ref, ag_ref, pv_ref, pg_ref, dy_ref, dn_ref, w_ref, dv_ref, dg_ref, dw_ref, db_ref, a_scr, d_scr):
        i = pl.program_id(1)

        @pl.when(i == 0)
        def _():
            dw_ref[...] = jnp.zeros_like(dw_ref)
            db_ref[...] = jnp.zeros_like(db_ref)

        av, sg = av_ref[...], jax.nn.sigmoid(ag_ref[...])
        prev = pv_ref[...] * jax.nn.sigmoid(pg_ref[...])
        a_scr[0:CONV_HALO, :] = jnp.where(i > 0, prev, 0.0)
        a_scr[CONV_HALO:CONV_HALO + T, :] = av * sg
        dyv = dy_ref[...]
        d_scr[0:T, :] = dyv
        d_scr[T:T + CONV_HALO, :] = jnp.where(i < nt - 1, dn_ref[...], 0.0)
        da = jnp.zeros((T, LANE), F32)
        for j in range(CONV_WIDTH):
            da = da + w_ref[j:j + 1, :] * d_scr[pl.ds(W1 - j, T), :]
            dw_ref[j:j + 1, :] += jnp.sum(dyv * a_scr[pl.ds(CONV_HALO - W1 + j, T), :], axis=0, keepdims=True)
        db_ref[...] += jnp.sum(dyv, axis=0, keepdims=True)
        dv_ref[...] = (da * sg).astype(BF16)
        dg_ref[...] = (da * av * sg * (1.0 - sg)).astype(BF16)

    def cur(part):
        return pl.BlockSpec((T, LANE), lambda cb, i: (i, part * ns + cb))

    def halo(part):
        return pl.BlockSpec((CONV_HALO, LANE), lambda cb, i: (jnp.maximum(i * hb - 1, 0), part * ns + cb))

    nxt = pl.BlockSpec((CONV_HALO, LANE), lambda cb, i: (jnp.minimum((i + 1) * hb, S // CONV_HALO - 1), cb))
    row = pl.BlockSpec((T, LANE), lambda cb, i: (i, cb))
    return pl.pallas_call(
        body, name=name, grid=(ns, nt),
        in_specs=[cur(0), cur(1), halo(0), halo(1), row, nxt, pl.BlockSpec((CONV_WIDTH, LANE), lambda cb, i: (0, cb))],
        out_specs=[row, row, pl.BlockSpec((CONV_HALO, LANE), lambda cb, i: (0, cb)),
                   pl.BlockSpec((1, LANE), lambda cb, i: (0, cb))],
        out_shape=[jax.ShapeDtypeStruct((S, C), BF16), jax.ShapeDtypeStruct((S, C), BF16),
                   jax.ShapeDtypeStruct((CONV_HALO, C), F32), jax.ShapeDtypeStruct((1, C), F32)],
        scratch_shapes=[pltpu.VMEM((T + CONV_HALO, LANE), F32), pltpu.VMEM((T + CONV_HALO, LANE), F32)],
        compiler_params=_params("parallel", "arbitrary"),
    )(u, u, u, u, dy, dy, conv_w)


def _rope_tables(S):
    half = HEAD_DIM // 2
    inv = jnp.exp(-math.log(ROPE_THETA) * jnp.arange(half, dtype=F32) / half)
    ang = jnp.arange(S, dtype=jnp.int32).astype(F32)[:, None] * inv[None, :]
    cos, sin = jnp.cos(ang), jnp.sin(ang)
    return jnp.concatenate([cos, cos], axis=1), jnp.concatenate([-sin, sin], axis=1)


def _qkv_prep(u, qn_g, kn_g, cos, sin, cb0, name):
    S = u.shape[0]
    A = (u.shape[1] // (cb0 + 3))
    H = A // HEAD_DIM
    T = _tile(S, 256, SUBLANE_BF16)
    scale = HEAD_DIM ** -0.5

    def body(q_ref, k_ref, v_ref, qg_ref, kg_ref, cos_ref, sin_ref, qo_ref, ko_ref, vo_ref):
        cosv, sinv = cos_ref[...], sin_ref[...]
        for h in range(H):
            sl = slice(h * HEAD_DIM, (h + 1) * HEAD_DIM)
            for x_ref, g_ref, o_ref, sc in ((q_ref, qg_ref, qo_ref, scale), (k_ref, kg_ref, ko_ref, 1.0)):
                xv = x_ref[:, sl]
                xn = xv * lax.rsqrt(jnp.mean(xv * xv, axis=-1, keepdims=True) + EPS) * g_ref[...]
                y = xn * cosv + pltpu.roll(xn, HEAD_DIM // 2, 1) * sinv
                o_ref[:, sl] = (y * sc).astype(BF16)
        vo_ref[...] = v_ref[...].astype(BF16)

    def col(cb):
        return pl.BlockSpec((T, A), lambda i: (i, cb))

    vec = pl.BlockSpec((1, HEAD_DIM), lambda i: (0, 0))
    tab = pl.BlockSpec((T, HEAD_DIM), lambda i: (i, 0))
    out = pl.BlockSpec((T, A), lambda i: (i, 0))
    return pl.pallas_call(
        body, name=name, grid=(S // T,),
        in_specs=[col(cb0), col(cb0 + 1), col(cb0 + 2), vec, vec, tab, tab],
        out_specs=[out] * 3, out_shape=[jax.ShapeDtypeStruct((S, A), BF16)] * 3,
        compiler_params=_params("parallel"),
    )(u, u, u, qn_g, kn_g, cos, sin)


def _qkv_prep_bwd(u, dqs, dks, dvs, qn_g, kn_g, cos, sin, cb0, name):
    S = u.shape[0]
    A = dqs[0].shape[1]
    H = A // HEAD_DIM
    T = _tile(S, 256, SUBLANE_BF16)
    nb = len(dqs)
    scale = HEAD_DIM ** -0.5

    def body(*refs):
        q_ref, k_ref, qg_ref, kg_ref, cos_ref, sin_ref = refs[:6]
        dq_refs, dk_refs, dv_refs = refs[6:6 + nb], refs[6 + nb:6 + 2 * nb], refs[6 + 2 * nb:6 + 3 * nb]
        dqo_ref, dko_ref, dvo_ref, dqg_ref, dkg_ref = refs[6 + 3 * nb:]

        @pl.when(pl.program_id(0) == 0)
        def _():
            dqg_ref[...] = jnp.zeros_like(dqg_ref)
            dkg_ref[...] = jnp.zeros_like(dkg_ref)

        cosv, sinv = cos_ref[...], sin_ref[...]
        for h in range(H):
            sl = slice(h * HEAD_DIM, (h + 1) * HEAD_DIM)
            for x_ref, g_ref, d_refs, o_ref, dg_ref, sc in ((q_ref, qg_ref, dq_refs, dqo_ref, dqg_ref, scale),
                                                          (k_ref, kg_ref, dk_refs, dko_ref, dkg_ref, 1.0)):
                dy = d_refs[0][:, sl]
                for r in d_refs[1:]:
                    dy = dy + r[:, sl]
                dy = dy * sc
                dxn = dy * cosv + pltpu.roll(dy * sinv, HEAD_DIM // 2, 1)
                xv = x_ref[:, sl]
                r = lax.rsqrt(jnp.mean(xv * xv, axis=-1, keepdims=True) + EPS)
                xh = xv * r
                dg_ref[...] += jnp.sum(dxn * xh, axis=0, keepdims=True)
                dxh = dxn * g_ref[...]
                o_ref[:, sl] = (r * (dxh - xh * jnp.mean(dxh * xh, axis=-1, keepdims=True))).astype(BF16)
        dv = dv_refs[0][...]
        for r in dv_refs[1:]:
            dv = dv + r[...]
        dvo_ref[...] = dv.astype(BF16)

    def col(cb):
        return pl.BlockSpec((T, A), lambda i: (i, cb))

    vec = pl.BlockSpec((1, HEAD_DIM), lambda i: (0, 0))
    tab = pl.BlockSpec((T, HEAD_DIM), lambda i: (i, 0))
    row = pl.BlockSpec((T, A), lambda i: (i, 0))
    return pl.pallas_call(
        body, name=name, grid=(S // T,),
        in_specs=[col(cb0), col(cb0 + 1), vec, vec, tab, tab] + [row] * (3 * nb),
        out_specs=[row, row, row, vec, vec],
        out_shape=[jax.ShapeDtypeStruct((S, A), BF16)] * 3 + [jax.ShapeDtypeStruct((1, HEAD_DIM), F32)] * 2,
        compiler_params=_params("arbitrary"),
    )(u, u, qn_g, kn_g, cos, sin, *dqs, *dks, *dvs)


ATT_TILE_FWD = 512
ATT_TILE_BWD = 512
NEG = -1e30


def _attn_bias(tile):
    span = max(window for window, _ in DIL_PATTERNS)
    nw = -(-span // tile) + 1
    dist = (jnp.arange(nw)[:, None, None] * tile + jnp.arange(tile)[None, :, None] - jnp.arange(tile)[None, None, :])
    mult = sum(((dist >= 0) & (dist <= window) & (dist % dil == 0)).astype(F32) for window, dil in DIL_PATTERNS)
    return jnp.where(mult > 0, jnp.log(jnp.maximum(mult, 1.0)), NEG)


def _attn_fwd(q, k, v, bias, name):
    S, A = q.shape
    H = A // HEAD_DIM
    nw, T, _ = bias.shape
    nq = S // T

    def body(q_ref, k_ref, v_ref, b_ref, ob_ref, of_ref, l_ref):
        i = pl.program_id(1)
        qv = q_ref[...]
        for w in range(nw):
            blk = i - w
            start = pl.multiple_of(jnp.maximum(blk, 0) * T, T)
            s = _dot(qv, k_ref[pl.ds(start, T), :], "nt") + b_ref[w]
            if w == 0:
                mx = jnp.max(s, axis=-1, keepdims=True)
                p = jnp.exp(s - mx)
                den = jnp.sum(p, axis=-1, keepdims=True)
                o = _dot(p.astype(BF16), v_ref[pl.ds(start, T), :], "nn")
            else:
                s = s + jnp.where(blk >= 0, 0.0, NEG)
                new = jnp.maximum(mx, jnp.max(s, axis=-1, keepdims=True))
                scale = jnp.exp(mx - new)
                p = jnp.exp(s - new)
                den = scale * den + jnp.sum(p, axis=-1, keepdims=True)
                o = scale * o + _dot(p.astype(BF16), v_ref[pl.ds(start, T), :], "nn")
                mx = new
        o = o / den
        ob_ref[...] = o.astype(BF16)
        of_ref[...] = o
        l_ref[...] = mx + jnp.log(den)

    blk = pl.BlockSpec((T, HEAD_DIM), lambda h, i: (i, h))
    full = pl.BlockSpec((S, HEAD_DIM), lambda h, i: (0, h))
    return pl.pallas_call(
        body, name=name, grid=(H, nq),
        in_specs=[blk, full, full, pl.BlockSpec((nw, T, T), lambda h, i: (0, 0, 0))],
        out_specs=[blk, blk, pl.BlockSpec((None, T, 1), lambda h, i: (h, i, 0))],
        out_shape=[jax.ShapeDtypeStruct((S, A), BF16), jax.ShapeDtypeStruct((S, A), F32),
                   jax.ShapeDtypeStruct((H, S, 1), F32)],
        compiler_params=_params("parallel", "arbitrary"),
    )(q, k, v, bias)


def _attn_dq(q, k, v, dz, cb0, o, lse, bias, name):
    S, A = q.shape
    H = A // HEAD_DIM
    nw, T, _ = bias.shape
    nq = S // T

    def body(q_ref, k_ref, v_ref, do_ref, o_ref, l_ref, b_ref, dq_ref, d_ref):
        i = pl.program_id(1)
        qv, dof = q_ref[...], do_ref[...]
        dov = dof.astype(BF16)
        delta = jnp.sum(dof * o_ref[...], axis=-1, keepdims=True)
        d_ref[...] = delta
        lv = l_ref[...]
        dq = jnp.zeros((T, HEAD_DIM), F32)
        for w in range(nw):
            blk = i - w
            start = pl.multiple_of(jnp.maximum(blk, 0) * T, T)
            kv = k_ref[pl.ds(start, T), :]
            s = _dot(qv, kv, "nt") + b_ref[w] + jnp.where(blk >= 0, 0.0, NEG)
            p = jnp.exp(s - lv)
            ds = (p * (_dot(dov, v_ref[pl.ds(start, T), :], "nt") - delta)).astype(BF16)
            dq = dq + _dot(ds, kv, "nn")
        dq_ref[...] = dq

    blk = pl.BlockSpec((T, HEAD_DIM), lambda h, i: (i, h))
    full = pl.BlockSpec((S, HEAD_DIM), lambda h, i: (0, h))
    col = pl.BlockSpec((None, T, 1), lambda h, i: (h, i, 0))
    return pl.pallas_call(
        body, name=name, grid=(H, nq),
        in_specs=[blk, full, full, pl.BlockSpec((T, HEAD_DIM), lambda h, i: (i, cb0 + h)), blk, col,
                  pl.BlockSpec((nw, T, T), lambda h, i: (0, 0, 0))],
        out_specs=[blk, col],
        out_shape=[jax.ShapeDtypeStruct((S, A), F32), jax.ShapeDtypeStruct((H, S, 1), F32)],
        compiler_params=_params("parallel", "arbitrary"),
    )(q, k, v, dz, o, lse, bias)


def _attn_dkv(q, k, v, dz, cb0, lse, delta, bias, name):
    S, A = q.shape
    H = A // HEAD_DIM
    nw, T, _ = bias.shape
    nq = S // T

    def body(k_ref, v_ref, q_ref, do_ref, l_ref, d_ref, b_ref, dk_ref, dv_ref):
        m = pl.program_id(1)
        kv, vv = k_ref[...], v_ref[...]
        dk = jnp.zeros((T, HEAD_DIM), F32)
        dv = jnp.zeros((T, HEAD_DIM), F32)
        for w in range(nw):
            blk = m + w
            start = pl.multiple_of(jnp.minimum(blk, nq - 1) * T, T)
            qv = q_ref[pl.ds(start, T), :]
            dov = do_ref[pl.ds(start, T), :].astype(BF16)
            s = _dot(qv, kv, "nt") + b_ref[w] + jnp.where(blk < nq, 0.0, NEG)
            p = jnp.exp(s - l_ref[pl.ds(start, T), :])
            dv = dv + _dot(p.astype(BF16), dov, "tn")
            ds = (p * (_dot(dov, vv, "nt") - d_ref[pl.ds(start, T), :])).astype(BF16)
            dk = dk + _dot(ds, qv, "tn")
        dk_ref[...] = dk
        dv_ref[...] = dv

    blk = pl.BlockSpec((T, HEAD_DIM), lambda h, m: (m, h))
    full = pl.BlockSpec((S, HEAD_DIM), lambda h, m: (0, h))
    col = pl.BlockSpec((None, S, 1), lambda h, m: (h, 0, 0))
    sds = jax.ShapeDtypeStruct((S, A), F32)
    return pl.pallas_call(
        body, name=name, grid=(H, nq),
        in_specs=[blk, blk, full, pl.BlockSpec((S, HEAD_DIM), lambda h, m: (0, cb0 + h)), col, col,
                  pl.BlockSpec((nw, T, T), lambda h, m: (0, 0, 0))],
        out_specs=[blk, blk], out_shape=[sds, sds],
        compiler_params=_params("parallel", "arbitrary"),
    )(k, v, q, dz, lse, delta, bias)


def _even_mixer(u, conv_w, conv_b, cn_g, cn_b, qn_g, kn_g, tag):
    S = u.shape[0]
    C = conv_w.shape[1]
    A = (u.shape[1] - 2 * C) // 3
    assert A == C, "column-block addressing of u assumes equal conv and attention widths"
    cos, sin = _rope_tables(S)
    bias = _attn_bias(_tile(S, ATT_TILE_BWD, LANE))
    a_out, y = _conv_fwd(u, conv_w, conv_b, cn_g, cn_b, "conv_fwd" + tag)
    q, k, v = _qkv_prep(u, qn_g, kn_g, cos, sin, 2, "qkv_prep" + tag)
    ob, of, lse = _attn_fwd(q, k, v, _attn_bias(_tile(S, ATT_TILE_FWD, LANE)), "attn_fwd" + tag)
    z = jnp.concatenate([a_out, ob], axis=1)

    def backward(dz):
        dy, d_cn_g, d_cn_b = _conv_bwd_norm(dz, y, cn_g, cn_b, "conv_bwd_norm" + tag)
        d_val, d_gate, d_w, d_b = _conv_bwd_taps(u, dy, conv_w, "conv_bwd_taps" + tag)
        dqp, delta = _attn_dq(q, k, v, dz, C // HEAD_DIM, of, lse, bias, "attn_dq" + tag)
        dkp, dvp = _attn_dkv(q, k, v, dz, C // HEAD_DIM, lse, delta, bias, "attn_dkv" + tag)
        dq, dk, dv, d_qn, d_kn = _qkv_prep_bwd(u, [dqp], [dkp], [dvp], qn_g, kn_g, cos, sin, 2, "qkv_prep_bwd" + tag)
        du = jnp.concatenate([d_val, d_gate, dq, dk, dv], axis=1)
        return du, [d_w[:CONV_WIDTH], d_b[0], d_cn_g[0], d_cn_b[0], d_qn[0], d_kn[0]]

    return z, backward


_LEVELS = (128, 64, 32, 16, 8, 4, 2, 1)


def _chunk_cumsum(g, rows, reverse=False):
    C = g.shape[0]
    d = 1
    while d < C:
        if reverse:
            g = g + jnp.where(rows < C - d, pltpu.roll(g, C - d, 0), 0.0)
        else:
            g = g + jnp.where(rows >= d, pltpu.roll(g, d, 0), 0.0)
        d *= 2
    return g


def _level_ref(b, b_scr, rows, m):
    C = b.shape[0]
    if m >= 8:
        pieces = [jnp.broadcast_to(b_scr[2 * m * j + m - 1:2 * m * j + m, :], (2 * m, LANE)) for j in range(C // (2 * m))]
        return pieces[0] if len(pieces) == 1 else jnp.concatenate(pieces, axis=0)
    pos = rows & (2 * m - 1)
    ref = b
    for p in range(2 * m):
        if p != m - 1:
            ref = jnp.where(pos == p, pltpu.roll(b, (p - (m - 1)) % C, 0), ref)
    return ref


def _level_operands(q, k, b, b_scr, rows, m):
    e = jnp.exp(-jnp.abs(b - _level_ref(b, b_scr, rows, m)))
    return (q * e).astype(BF16), (k * e).astype(BF16)


def _split2(x):
    hi = x.astype(BF16)
    lo = (x - hi.astype(F32)).astype(BF16)
    return jnp.concatenate([hi, lo], axis=1)


def _level_table(n):
    t = jnp.arange(n, dtype=jnp.int32)[:, None]
    s = jnp.arange(n, dtype=jnp.int32)[None, :]
    x = t ^ s
    lvl = sum((x >= (1 << j)).astype(jnp.int32) for j in range(1, n.bit_length()))
    return jnp.where(t > s, lvl, jnp.where(t == s, -1, -2))


def _hgrn_gates(qz, fz, la, lc, oml):
    sq = jax.nn.sigmoid(qz)
    q = qz * sq
    s = jax.nn.sigmoid(fz)
    c = lc + jnp.minimum(fz, 0.0) - jnp.log(1.0 + jnp.exp(-jnp.abs(fz)))
    mx = jnp.maximum(la, c)
    g = mx + jnp.log(1.0 + jnp.exp(-jnp.abs(la - c)))
    k = oml * (1.0 - s)
    return q, sq, k, s, g, c


def _hgrn_fwd(u, la, lc, oml, gn_g, name):
    S = u.shape[0]
    W = u.shape[1] // 4
    H = W // HGRN_KDIM
    C = min(HGRN_CHUNK, S)
    nc = S // C
    levels = [m for m in _LEVELS if m < C]
    HB = C // 2

    def body(qz_ref, fz_ref, iz_ref, gz_ref, la_ref, lc_ref, oml_ref, gn_ref, lvl_ref,
             z_ref, o_ref, a_ref, st_ref, state, b_scr):
        @pl.when(pl.program_id(1) == 0)
        def _():
            state[...] = jnp.zeros_like(state)

        rows = lax.broadcasted_iota(jnp.int32, (C, LANE), 0)
        q, _, k, _, g, _ = _hgrn_gates(qz_ref[...], fz_ref[...], la_ref[...], lc_ref[...], oml_ref[...])
        v = iz_ref[...].astype(BF16)
        b = _chunk_cumsum(g, rows)
        b_scr[...] = b
        lvl = lvl_ref[...]
        qk = jnp.sum(q * k, axis=-1, keepdims=True)
        diag = [jnp.where(lvl == -1, qk[r * HB:(r + 1) * HB], 0.0) for r in range(2)]
        for m in levels[1:]:
            qs, ks = _level_operands(q, k, b, b_scr, rows, m)
            for r in range(2):
                sl = slice(r * HB, (r + 1) * HB)
                diag[r] = jnp.where(lvl == m.bit_length() - 1, _dot(qs[sl], ks[sl], "nt"), diag[r])
        qs, ks = _level_operands(q, k, b, b_scr, rows, HB)
        low = _dot(qs[HB:], ks[:HB], "nt")
        a = jnp.concatenate([jnp.concatenate([diag[0], jnp.zeros((HB, HB), F32)], axis=1),
                             jnp.concatenate([low, diag[1]], axis=1)], axis=0)
        ab = a.astype(BF16)
        a_ref[...] = ab
        st = state[...]
        st_ref[...] = st
        o = _dot(ab, v, "nn") + _dot((q * jnp.exp(b)).astype(BF16), st.astype(BF16), "nt")
        bl = b_scr[C - 1:C, :]
        kh = (k * jnp.exp(bl - b)).astype(BF16)
        state[...] = st * jnp.exp(bl) + _dot(v, kh, "tn")
        o_ref[...] = o
        r = lax.rsqrt(jnp.mean(o * o, axis=-1, keepdims=True) + EPS)
        gz = gz_ref[...]
        z_ref[...] = (o * r * gn_ref[...] * (gz * jax.nn.sigmoid(gz))).astype(BF16)

    def col(off):
        return pl.BlockSpec((C, LANE), lambda h, i: (i, off * H + h))

    vec = pl.BlockSpec((1, LANE), lambda h, i: (0, h))
    tile = pl.BlockSpec((C, LANE), lambda h, i: (i, h))
    return pl.pallas_call(
        body, name=name, grid=(H, nc),
        in_specs=[col(0), col(1), col(2), col(3), vec, vec, vec, vec, pl.BlockSpec((HB, HB), lambda h, i: (0, 0))],
        out_specs=[tile, tile, pl.BlockSpec((None, C, C), lambda h, i: (h, i, 0)),
                   pl.BlockSpec((None, None, LANE, LANE), lambda h, i: (h, i, 0, 0))],
        out_shape=[jax.ShapeDtypeStruct((S, W), BF16), jax.ShapeDtypeStruct((S, W), F32),
                   jax.ShapeDtypeStruct((H, S, C), BF16), jax.ShapeDtypeStruct((H, nc, LANE, LANE), F32)],
        scratch_shapes=[pltpu.VMEM((LANE, LANE), F32), pltpu.VMEM((C, LANE), F32)],
        compiler_params=_params("parallel", "arbitrary"),
    )(u, u, u, u, la, lc, oml, gn_g, _level_table(HB))


def _hgrn_bwd(u, la, lc, oml, gn_g, o, a, st, dz, name):
    S = u.shape[0]
    W = u.shape[1] // 4
    H = W // HGRN_KDIM
    C = min(HGRN_CHUNK, S)
    nc = S // C
    levels = [m for m in _LEVELS if m < C]
    HB = C // 2

    def body(qz_ref, fz_ref, iz_ref, gz_ref, la_ref, lc_ref, oml_ref, gn_ref, o_ref, a_ref, st_ref, dz_ref, lvl_ref,
             du_ref, dla_ref, dlc_ref, doml_ref, dgn_ref, dstate, b_scr):
        @pl.when(pl.program_id(1) == 0)
        def _():
            dstate[...] = jnp.zeros_like(dstate)
            dla_ref[...] = jnp.zeros_like(dla_ref)
            dlc_ref[...] = jnp.zeros_like(dlc_ref)
            doml_ref[...] = jnp.zeros_like(doml_ref)
            dgn_ref[...] = jnp.zeros_like(dgn_ref)

        rows = lax.broadcasted_iota(jnp.int32, (C, LANE), 0)
        la_v, lc_v, oml_v = la_ref[...], lc_ref[...], oml_ref[...]
        qz, fz = qz_ref[...], fz_ref[...]
        q, sq, k, s, g, c = _hgrn_gates(qz, fz, la_v, lc_v, oml_v)
        vf = iz_ref[...]
        v = vf.astype(BF16)

        ov, gz, dzv, gn = o_ref[...], gz_ref[...], dz_ref[...], gn_ref[...]
        r = lax.rsqrt(jnp.mean(ov * ov, axis=-1, keepdims=True) + EPS)
        on = ov * r
        sg = jax.nn.sigmoid(gz)
        silu_g = gz * sg
        dgn_ref[...] += jnp.sum(dzv * on * silu_g, axis=0, keepdims=True)
        du_ref[3] = (dzv * on * gn * (sg * (1.0 + gz * (1.0 - sg)))).astype(BF16)
        don = dzv * gn * silu_g
        do_f = r * (don - on * jnp.mean(don * on, axis=-1, keepdims=True))
        do = do_f.astype(BF16)

        b = _chunk_cumsum(g, rows)
        b_scr[...] = b
        bl = b_scr[C - 1:C, :]
        e = jnp.exp(b)
        ebl = jnp.exp(bl)
        ekl = jnp.exp(bl - b)
        qh = q * e
        kh = k * ekl
        st_v = st_ref[...]
        dst = dstate[...]
        dstb = dst.astype(BF16)

        du_ref[2] = (_dot(a_ref[...], do, "tn") + _dot(kh.astype(BF16), dstb, "nt")).astype(BF16)
        da = _dot(do, v, "nt")
        dqh = _dot(do, st_v.astype(BF16), "nn")
        dkh = _dot(v, dstb, "nn")
        dstate[...] = dst * ebl + _dot(do, qh.astype(BF16), "tn")
        dbl = jnp.sum(dkh * kh, axis=0, keepdims=True) + jnp.sum(dst * st_v, axis=0, keepdims=True) * ebl

        datt = jnp.sum(do_f * vf, axis=-1, keepdims=True)
        dqa = datt * k
        dka = datt * q
        lvl = lvl_ref[...]
        for m in levels:
            ez = jnp.exp(-jnp.abs(b - _level_ref(b, b_scr, rows, m)))
            ks2, qs2 = _split2(k * ez), _split2(q * ez)
            if m == HB:
                gm = da[HB:, :HB].astype(BF16)
                pq = jnp.concatenate([jnp.zeros((HB, 2 * LANE), F32), _dot(gm, ks2[:HB], "nn")], axis=0)
                pk = jnp.concatenate([_dot(gm, qs2[HB:], "tn"), jnp.zeros((HB, 2 * LANE), F32)], axis=0)
            else:
                gms = [jnp.where(lvl == m.bit_length() - 1, da[r * HB:(r + 1) * HB, r * HB:(r + 1) * HB], 0.0).astype(BF16)
                       for r in range(2)]
                pq = jnp.concatenate([_dot(gms[r], ks2[r * HB:(r + 1) * HB], "nn") for r in range(2)], axis=0)
                pk = jnp.concatenate([_dot(gms[r], qs2[r * HB:(r + 1) * HB], "tn") for r in range(2)], axis=0)
            dqa += (pq[:, :LANE] + pq[:, LANE:]) * ez
            dka += (pk[:, :LANE] + pk[:, LANE:]) * ez
        db = q * dqa - k * dka + dqh * qh - dkh * kh
        db = db + jnp.where(rows == C - 1, dbl, 0.0)
        dq = dqa + dqh * e
        dk = dka + dkh * ekl
        dg = _chunk_cumsum(db, rows, reverse=True)

        wa = jnp.exp(la_v - g)
        wc = jnp.exp(c - g)
        du_ref[0] = (dq * (sq * (1.0 + qz * (1.0 - sq)))).astype(BF16)
        du_ref[1] = (dg * wc * (1.0 - s) - dk * oml_v * s * (1.0 - s)).astype(BF16)
        dla_ref[...] += jnp.sum(dg * wa, axis=0, keepdims=True)
        dlc_ref[...] += jnp.sum(dg * wc, axis=0, keepdims=True)
        doml_ref[...] += jnp.sum(dk * (1.0 - s), axis=0, keepdims=True)

    def col(off):
        return pl.BlockSpec((C, LANE), lambda h, i: (nc - 1 - i, off * H + h))

    vec = pl.BlockSpec((1, LANE), lambda h, i: (0, h))
    tile = pl.BlockSpec((C, LANE), lambda h, i: (nc - 1 - i, h))
    a_spec = pl.BlockSpec((None, C, C), lambda h, i: (h, nc - 1 - i, 0))
    st_spec = pl.BlockSpec((None, None, LANE, LANE), lambda h, i: (h, nc - 1 - i, 0, 0))
    vw = jax.ShapeDtypeStruct((1, W), F32)
    return pl.pallas_call(
        body, name=name, grid=(H, nc),
        in_specs=[col(0), col(1), col(2), col(3), vec, vec, vec, vec, tile, a_spec, st_spec, tile,
                  pl.BlockSpec((HB, HB), lambda h, i: (0, 0))],
        out_specs=[pl.BlockSpec((4, C, LANE), lambda h, i: (0, nc - 1 - i, h)), vec, vec, vec, vec],
        out_shape=[jax.ShapeDtypeStruct((4, S, W), BF16), vw, vw, vw, vw],
        scratch_shapes=[pltpu.VMEM((LANE, LANE), F32), pltpu.VMEM((C, LANE), F32)],
        compiler_params=_params("parallel", "arbitrary"),
    )(u, u, u, u, la, lc, oml, gn_g, o, a, st, dz, _level_table(HB))


def _lb_terms(lb_logits, layer):
    p = jax.nn.softmax(lb_logits, axis=0)
    lb = (jnp.cumsum(p, axis=0) - p[0:1])[layer]
    return jnp.log(lb)[None], jnp.log1p(-lb)[None], (1.0 - lb)[None]


def kernel(x, norm_ffn1, ffn1_wg, ffn1_wu, ffn1_wd, norm_mix, norm_ffn2, ffn2_wg, ffn2_wu, ffn2_wd, ev_w_in, ev_conv_w, ev_conv_b, ev_cn_g, ev_cn_b, ev_qn_g, ev_kn_g, ev_w_out, od_w_in, od_lb_logits, od_gn_g, od_w_out, loss_target, m_norm_ffn1, m_ffn1_wg, m_ffn1_wu, m_ffn1_wd, m_norm_mix, m_norm_ffn2, m_ffn2_wg, m_ffn2_wu, m_ffn2_wd, m_ev_w_in, m_ev_conv_w, m_ev_conv_b, m_ev_cn_g, m_ev_cn_b, m_ev_qn_g, m_ev_kn_g, m_ev_w_out, m_od_w_in, m_od_lb_logits, m_od_gn_g, m_od_w_out, v_norm_ffn1, v_ffn1_wg, v_ffn1_wu, v_ffn1_wd, v_norm_mix, v_norm_ffn2, v_ffn2_wg, v_ffn2_wu, v_ffn2_wd, v_ev_w_in, v_ev_conv_w, v_ev_conv_b, v_ev_cn_g, v_ev_cn_b, v_ev_qn_g, v_ev_kn_g, v_ev_w_out, v_od_w_in, v_od_lb_logits, v_od_gn_g, v_od_w_out):
    depth = norm_ffn1.shape[0]
    S, D = x.shape[1], x.shape[2]
    xi, yi, ci = _me()
    dev = 4 * xi + 2 * yi + ci
    c_idx = jnp.reshape(ci, (1,)).astype(jnp.int32)
    k_idx = jnp.reshape(2 * xi + yi, (1,)).astype(jnp.int32)

    def ffn_shard(wg, wu, wd, l):
        return jnp.stack([wg[l].T, wu[l].T, wd[l]]).astype(BF16)

    assert depth == 2, "the exchange schedule below is written for one even and one odd layer"
    sh_ffn1 = [ffn_shard(ffn1_wg, ffn1_wu, ffn1_wd, l) for l in range(depth)]
    sh_ffn2 = [ffn_shard(ffn2_wg, ffn2_wu, ffn2_wd, l) for l in range(depth)]
    sh_ev = [ev_w_in[0].T.astype(BF16)[None], ev_w_out[0].astype(BF16)[None]]
    sh_od = [od_w_in[0].T.astype(BF16)[None], od_w_out[0].astype(BF16)[None]]

    def full(g):
        return g.reshape(g.shape[0], N_DEV * g.shape[2], g.shape[3])

    def gather_begin(shards, after, tag):
        lands = [lax.dynamic_update_slice(lax.empty((s.shape[0], N_DEV) + s.shape[1:], s.dtype), s[:, None],
                                          (0, dev, 0, 0)) for s in shards]
        state = _push_start(shards, lands, _gather_plan, after, "gather_start" + tag)
        return state, state[4][0, 0]

    def gather_arrived(state, after, tag):
        send, recv, srcs, lands, _ = state
        _, lands = _push_wait(send, recv, srcs, lands, _gather_plan, after, "gather_wait" + tag)
        state = _push_start([], lands, _forward_plan, None, "forward_start" + tag)
        return state, state[4][0, 0]

    def gather_done(state, after, tag):
        send, recv, _, lands, _ = state
        _, lands = _push_wait(send, recv, [], lands, _forward_plan, after, "forward_wait" + tag)
        return [full(g) for g in lands]

    def gather_end(state, after, tag):
        state, _ = gather_arrived(state, after, tag)
        return gather_done(state, state[4], tag)

    conv_w_sh, gn_g_sh = ev_conv_w[0], od_gn_g[0]
    cw, cs = conv_w_sh.shape[0], conv_w_sh.shape[1]
    gs = gn_g_sh.shape[0]
    conv_w_z = lax.dynamic_update_slice(jnp.zeros((cw, N_DEV * cs), F32), conv_w_sh, (0, dev * cs))
    gn_g_z = lax.dynamic_update_slice(jnp.zeros((N_DEV * gs,), F32), gn_g_sh, (dev * gs,))
    conv_w_full, gn_g_full = _unpack_rows(
        _all_reduce_small(_pack_rows([conv_w_z, gn_g_z]), "gather_small_params"),
        [conv_w_z.shape, gn_g_z.shape])

    w_ffn1, w_ffn2 = [None] * depth, [None] * depth
    pending, after = {}, conv_w_full
    for key, shards in (("0", [sh_ffn1[0]]), ("1", sh_ev), ("2", [sh_ffn2[0]]), ("3", [sh_ffn1[1]]), ("4", sh_od),
                        ("5", [sh_ffn2[1]])):
        pending[key], _ = gather_begin(shards, after, "_" + key)
        after = pending[key][4]
    start_tok = after[0, 0]
    (w_ffn1[0],) = gather_end(pending.pop("0"), after, "_0")

    def odd_mixer(u, l, tok):
        (la, lc, oml), lb_vjp = jax.vjp(functools.partial(_lb_terms, layer=l), od_lb_logits)
        gn = (gn_g_full + tok)[None]
        zb, o_raw, scores, states = _hgrn_fwd(u, la, lc, oml, gn, f"hgrn_fwd{l}")

        def backward(dz):
            du, dla, dlc, doml, dgn = _hgrn_bwd(u, la, lc, oml, gn, o_raw, scores, states, dz, f"hgrn_bwd{l}")
            (g_lb,) = lb_vjp((dla, dlc, doml))
            return du, [g_lb, dgn[0]]

        return zb, backward

    saved = []
    h = x[0]
    hn = _rms_fwd(h, (norm_ffn1[0] + start_tok)[None], "rms_a0")
    for l in range(depth):
        ffn, gu = _ffn_fwd(hn, w_ffn1[l], f"ffn_fwd_a{l}")
        s1 = (h, hn, gu)
        if l == 0:
            w_in, w_out = gather_end(pending.pop("1"), ffn, "_1")
        else:
            w_in, w_out = gather_done(pending.pop("4"), ffn, "_4")
        w_in_t, w_out = w_in[0], w_out[0]
        h, hn = _resid_rms(h, ffn, norm_mix[l][None], f"rms_mix{l}")
        u = _mm(hn, w_in_t, "nt", F32, f"mix_in{l}")
        key = "2" if l == 0 else "5"
        passing, tok = gather_arrived(pending.pop(key), u, "_" + key)
        if l % 2 == 0:
            zb, core_vjp = _even_mixer(u, conv_w_full, ev_conv_b + tok, ev_cn_g, ev_cn_b, ev_qn_g, ev_kn_g, str(l))
        else:
            zb, core_vjp = odd_mixer(u, l, tok)
        h_mix = h
        h = _mm(zb, w_out, "nn", F32, f"mix_out{l}", res=h)
        sm = (h_mix, hn, zb, core_vjp, w_in_t, w_out)
        (w_ffn2[l],) = gather_done(passing, h, "_" + key)
        tok = 0.0
        if l + 1 < depth:
            passing, tok = gather_arrived(pending.pop("3"), w_ffn2[l], "_3")
        hn = _rms_fwd(h, (norm_ffn2[l] + tok)[None], f"rms_b{l}")
        ffn, gu = _ffn_fwd(hn, w_ffn2[l], f"ffn_fwd_b{l}")
        saved.append((s1, sm, (h, hn, gu)))
        if l + 1 < depth:
            (w_ffn1[l + 1],) = gather_done(passing, ffn, "_3")
            pending["4"], tok = gather_arrived(pending.pop("4"), w_ffn1[l + 1], "_4")
            h, hn = _resid_rms(h, ffn, (norm_ffn1[l + 1] + tok)[None], f"rms_a{l + 1}")

    dy, loss_part = _loss_grad(h, ffn, loss_target[0], "loss_grad")

    def halves_begin(parts, tag):
        parts = [g.reshape(g.shape[0], 4, 2, g.shape[1] // N_DEV, g.shape[2]) for g in parts]
        lands = [lax.empty(g.shape[:2] + g.shape[3:], BF16) for g in parts]
        state = _push_start(parts, lands, _halves_plan, None, "halves_start" + tag)
        return state, state[4][0, 0]

    def chips_begin(state, after, tag):
        send, recv, srcs, lands, _ = state
        parts, got = _push_wait(send, recv, srcs, lands, _halves_plan, after, "halves_wait" + tag)
        sums = [_add_core_halves(g, r, c_idx, f"add_core_halves{tag}_{a}") for a, (g, r) in enumerate(zip(parts, got))]
        lands = [lax.empty((3, s.shape[0]) + s.shape[2:], BF16) for s in sums]
        state = _push_start(sums, lands, _chip_plan, None, "reduce_start" + tag)
        return state, state[4][0, 0]

    def reduce_end(state, after, tag):
        send, recv, srcs, lands, _ = state
        sums, got = _push_wait(send, recv, srcs, lands, _chip_plan, after, "reduce_wait" + tag)
        return [_sum_chip_blocks(s, r, k_idx, f"sum_chip_blocks{tag}_{a}") for a, (s, r) in enumerate(zip(sums, got))]

    def ffn_backward(dy, gain, w, sv, tag, on_dw, on_dx=None, with_bf16=False):
        h_in, hn, gu = sv
        dxn, dout, t = _ffn_bwd_dx(dy, w, gu, "ffn_bwd_dx_" + tag)
        tok = 0.0 if on_dx is None else on_dx(dxn)
        tok = tok + on_dw(_ffn_bwd_dw(hn, dout, t, "ffn_bwd_dw_" + tag))
        dx, dgain, *dxb = _rms_bwd(h_in, (gain + tok)[None], dxn, dy, "rms_bwd_" + tag, with_bf16)
        return (dx, dgain[0], *dxb)

    g_norm1, g_norm2, g_normm = [None] * depth, [None] * depth, [None] * depth
    small, halves, groups = [None, None], {}, {}

    def start_halves(key, make_parts):
        def hook(dw):
            halves[key], tok = halves_begin(make_parts(dw), "_" + key)
            return tok
        return hook

    def start_chips(key):
        def hook(after):
            groups[key], tok = chips_begin(halves.pop(key), after, "_" + key)
            return tok
        return hook

    for l in reversed(range(depth)):
        s1, (h_mix, hn, zb, core_vjp, w_in_t, w_out), s2 = saved[l]
        if l == 1:
            dy, g_norm2[l], dyb = ffn_backward(dy, norm_ffn2[l], w_ffn2[l], s2, f"b{l}",
                                               start_halves("1", lambda dw: [dw]), None, True)
        else:
            dy, g_norm2[l], dyb = ffn_backward(dy, norm_ffn2[l], w_ffn2[l], s2, f"b{l}",
                                               start_halves("3", lambda dw: [dw]), start_chips("2"), True)
        dz = _mm(dyb, w_out, "nt", F32, f"mix_out_dz{l}")
        dw_out = _mm(zb, dyb, "tn", BF16, f"mix_out_dw{l}")
        dub, small[l % 2] = core_vjp(dz)
        dw_in_t = _mm(dub, hn, "tn", BF16, f"mix_in_dw{l}")
        mix_parts = [dw_in_t[None], dw_out[None]]
        if l == 1:
            tok = start_chips("1")(dw_in_t)
        else:
            tok = start_chips("3")(dw_in_t) + start_halves("4", lambda _: mix_parts)(None)
        dhn = _mm(dub, w_in_t, "nn", F32, f"mix_in_dx{l}")
        dy, gm = _rms_bwd(h_mix, (norm_mix[l] + tok)[None], dhn, dy, f"rms_bwd_mix{l}")
        g_normm[l] = gm[0]
        if l == 1:
            dy, g_norm1[l] = ffn_backward(dy, norm_ffn1[l], w_ffn1[l], s1, f"a{l}",
                                          start_halves("2", lambda dw, od=mix_parts: od + [dw]))
        else:
            dy, g_norm1[l] = ffn_backward(dy, norm_ffn1[l], w_ffn1[l], s1, f"a{l}", start_halves("5", lambda dw: [dw]),
                                          start_chips("4"))
    grad_x = dy[None]
    start_chips("5")(dy)

    done = [dy, groups["5"][4]]
    (g_ffn2_1,) = reduce_end(groups["1"], done, "_1")
    g_od_in_t, g_od_out, g_ffn1_1 = reduce_end(groups["2"], done, "_2")
    (g_ffn2_0,) = reduce_end(groups["3"], done, "_3")
    g_ev_in_t, g_ev_out = reduce_end(groups["4"], done, "_4")
    g_ffn2 = [g_ffn2_0, g_ffn2_1]

    def ffn_grads(gl):
        return (jnp.stack([g[0].T for g in gl]), jnp.stack([g[1].T for g in gl]), jnp.stack([g[2] for g in gl]))

    g_ffn2_wg, g_ffn2_wu, g_ffn2_wd = ffn_grads(g_ffn2)
    grads = [None, None, None, None, None, None, g_ffn2_wg, g_ffn2_wu, g_ffn2_wd,
             g_ev_in_t[0].T[None], None, None, None, None, None,
             None, g_ev_out, g_od_in_t[0].T[None], None, None, g_od_out]
    weights = [norm_ffn1, ffn1_wg, ffn1_wu, ffn1_wd, norm_mix, norm_ffn2, ffn2_wg, ffn2_wu, ffn2_wd, ev_w_in,
               ev_conv_w, ev_conv_b, ev_cn_g, ev_cn_b, ev_qn_g, ev_kn_g, ev_w_out, od_w_in, od_lb_logits,
               od_gn_g, od_w_out]
    moms = [m_norm_ffn1, m_ffn1_wg, m_ffn1_wu, m_ffn1_wd, m_norm_mix, m_norm_ffn2, m_ffn2_wg, m_ffn2_wu,
            m_ffn2_wd, m_ev_w_in, m_ev_conv_w, m_ev_conv_b, m_ev_cn_g, m_ev_cn_b, m_ev_qn_g, m_ev_kn_g,
            m_ev_w_out, m_od_w_in, m_od_lb_logits, m_od_gn_g, m_od_w_out]
    vars_ = [v_norm_ffn1, v_ffn1_wg, v_ffn1_wu, v_ffn1_wd, v_norm_mix, v_norm_ffn2, v_ffn2_wg, v_ffn2_wu,
             v_ffn2_wd, v_ev_w_in, v_ev_conv_w, v_ev_conv_b, v_ev_cn_g, v_ev_cn_b, v_ev_qn_g, v_ev_kn_g,
             v_ev_w_out, v_od_w_in, v_od_lb_logits, v_od_gn_g, v_od_w_out]
    n_w = len(weights)
    deltas, new_m, new_v = [None] * n_w, [None] * n_w, [None] * n_w

    def update(idx):
        for i in idx:
            deltas[i], new_m[i], new_v[i] = _adamw(weights[i], grads[i], moms[i], vars_[i], f"adamw{i}")

    update([i for i in range(n_w) if grads[i] is not None])
    g_conv_w, g_conv_b, g_cn_g, g_cn_b, g_qn_g, g_kn_g = small[0]
    g_lb, g_gn = small[1]
    parts = [jnp.stack(g_norm1), jnp.stack(g_normm), jnp.stack(g_norm2), g_conv_b, g_cn_g, g_cn_b,
             g_qn_g, g_kn_g, g_lb, g_conv_w, g_gn, loss_part[0, :1]]
    red = _unpack_rows(_all_reduce_small(_pack_rows(parts), "reduce_small_grads", [d for d in deltas if d is not None]),
                       [p.shape for p in parts])
    g_norm1, g_normm, g_norm2, g_conv_b, g_cn_g, g_cn_b, g_qn_g, g_kn_g, g_lb, g_conv_w, g_gn, loss = red
    g_conv_w = lax.dynamic_slice(g_conv_w, (0, dev * cs), (cw, cs))
    g_gn = lax.dynamic_slice(g_gn, (dev * gs,), (gs,))
    small_idx = {0: g_norm1, 4: g_normm, 5: g_norm2, 10: g_conv_w[None], 11: g_conv_b[None], 12: g_cn_g[None],
                 13: g_cn_b[None], 14: g_qn_g[None], 15: g_kn_g[None], 18: g_lb, 19: g_gn[None]}
    for i, g in small_idx.items():
        grads[i] = g
    update(small_idx)
    (g_ffn1_0,) = reduce_end(groups["5"], [d for d in deltas if d is not None], "_5")
    grads[1], grads[2], grads[3] = ffn_grads([g_ffn1_0, g_ffn1_1])
    update((1, 2, 3))
    return (loss[0], grad_x, *grads, *deltas, *new_m, *new_v)
```

```python
import functools
import math

import jax
import jax.numpy as jnp
from jax import lax
from jax.experimental import pallas as pl
from jax.experimental.pallas import tpu as pltpu

F32 = jnp.float32
BF16 = jnp.bfloat16
MESH = pl.DeviceIdType.MESH
N_DEV = 8

EPS = 1e-6
HEAD_DIM = 128
CONV_WIDTH = 31
DIL_PATTERNS = ((128, 1), (512, 4), (2048, 16))
Q_BLOCK = 128
ROPE_THETA = 10000.0
HGRN_KDIM = 128
HGRN_CHUNK = 256

ADAM_LR = 0.001
ADAM_B1 = 0.9
ADAM_B2 = 0.999
ADAM_EPS = 1e-08
ADAM_WD = 0.01
ADAM_STEP = 10

VMEM_LIMIT_BYTES = 56 * 1024 * 1024
LANE = 128
SUBLANE_BF16 = 16

ANY = pl.BlockSpec(memory_space=pl.ANY)


def _tile(n, pref, mult):
    t = (min(pref, n) // mult) * mult
    while t > 0:
        if n % t == 0:
            return t
        t -= mult
    return n


def _params(*sem):
    return pltpu.CompilerParams(dimension_semantics=sem, vmem_limit_bytes=VMEM_LIMIT_BYTES)


_DOT_DIMS = {
    "nn": (((1,), (0,)), ((), ())),
    "nt": (((1,), (1,)), ((), ())),
    "tn": (((0,), (0,)), ((), ())),
}


def _dot(a, b, mode):
    return lax.dot_general(a, b, _DOT_DIMS[mode], preferred_element_type=F32)


def _mm(a, b, mode, out_dtype, name, res=None, tm=1024, tn=1024, tk=2048):
    parts, a_shape = (a.shape[0], (a.shape[1], a.shape[0] * a.shape[2])) if a.ndim == 3 else (1, a.shape)
    if mode == "nt":
        (M, K), N = a_shape, b.shape[0]
    elif mode == "nn":
        (M, K), N = a_shape, b.shape[1]
    else:
        (K, M), N = a_shape, b.shape[1]
    tm, tn, tk = _tile(M, tm, LANE), _tile(N, tn, LANE), _tile(K, tk, LANE)
    nk = K // tk

    def body(*refs):
        if res is None:
            a_ref, b_ref, o_ref, acc = refs
        else:
            a_ref, b_ref, r_ref, o_ref, acc = refs
        k = pl.program_id(2)

        @pl.when(k == 0)
        def _():
            acc[...] = jnp.zeros_like(acc)

        acc[...] += _dot(a_ref[...].astype(BF16), b_ref[...].astype(BF16), mode)

        @pl.when(k == nk - 1)
        def _():
            r = acc[...]
            if res is not None:
                r = r_ref[...] + r
            o_ref[...] = r.astype(out_dtype)

    if parts > 1:
        assert mode in ("nn", "tn") and a.shape[2] % (tk if mode == "nn" else tm) == 0
        per = a.shape[2] // (tk if mode == "nn" else tm)
        a_spec = {"nn": pl.BlockSpec((None, tm, tk), lambda i, j, k: (k // per, i, k % per)),
                  "tn": pl.BlockSpec((None, tk, tm), lambda i, j, k: (i // per, k, i % per))}[mode]
    else:
        a_spec = {"nt": pl.BlockSpec((tm, tk), lambda i, j, k: (i, k)),
                  "nn": pl.BlockSpec((tm, tk), lambda i, j, k: (i, k)),
                  "tn": pl.BlockSpec((tk, tm), lambda i, j, k: (k, i))}[mode]
    b_spec = {"nt": pl.BlockSpec((tn, tk), lambda i, j, k: (j, k)),
              "nn": pl.BlockSpec((tk, tn), lambda i, j, k: (k, j)),
              "tn": pl.BlockSpec((tk, tn), lambda i, j, k: (k, j))}[mode]
    o_spec = pl.BlockSpec((tm, tn), lambda i, j, k: (i, j))
    in_specs = [a_spec, b_spec] + ([o_spec] if res is not None else [])
    args = (a, b) + ((res,) if res is not None else ())
    return pl.pallas_call(
        body, name=name, grid=(M // tm, N // tn, nk),
        in_specs=in_specs, out_specs=o_spec,
        out_shape=jax.ShapeDtypeStruct((M, N), out_dtype),
        scratch_shapes=[pltpu.VMEM((tm, tn), F32)],
        compiler_params=_params("parallel", "parallel", "arbitrary"),
    )(*args)


def _rms_fwd(x, gain, name):
    S, D = x.shape
    tm = _tile(S, 512, SUBLANE_BF16)

    def body(x_ref, g_ref, o_ref):
        xv = x_ref[...]
        r = lax.rsqrt(jnp.mean(xv * xv, axis=-1, keepdims=True) + EPS)
        o_ref[...] = (xv * r * g_ref[...]).astype(BF16)

    return pl.pallas_call(
        body, name=name, grid=(S // tm,),
        in_specs=[pl.BlockSpec((tm, D), lambda i: (i, 0)), pl.BlockSpec((1, D), lambda i: (0, 0))],
        out_specs=pl.BlockSpec((tm, D), lambda i: (i, 0)),
        out_shape=jax.ShapeDtypeStruct((S, D), BF16),
        compiler_params=_params("parallel"),
    )(x, gain)


def _resid_rms(x, ffn, gain, name):
    S, D = x.shape
    tm = _tile(S, 512, SUBLANE_BF16)

    def body(x_ref, f_ref, g_ref, h_ref, o_ref):
        hv = x_ref[...] + 0.5 * f_ref[...]
        h_ref[...] = hv
        r = lax.rsqrt(jnp.mean(hv * hv, axis=-1, keepdims=True) + EPS)
        o_ref[...] = (hv * r * g_ref[...]).astype(BF16)

    row = pl.BlockSpec((tm, D), lambda i: (i, 0))
    return pl.pallas_call(
        body, name=name, grid=(S // tm,),
        in_specs=[row, row, pl.BlockSpec((1, D), lambda i: (0, 0))], out_specs=[row, row],
        out_shape=[jax.ShapeDtypeStruct((S, D), F32), jax.ShapeDtypeStruct((S, D), BF16)],
        compiler_params=_params("parallel"),
    )(x, ffn, gain)


def _rms_bwd(x, gain, dxn, dy, name):
    S, D = x.shape
    tm = _tile(S, 512, 8)

    def body(x_ref, g_ref, dxn_ref, dy_ref, dx_ref, dg_ref):
        @pl.when(pl.program_id(0) == 0)
        def _():
            dg_ref[...] = jnp.zeros_like(dg_ref)

        xv = x_ref[...]
        r = lax.rsqrt(jnp.mean(xv * xv, axis=-1, keepdims=True) + EPS)
        xh = xv * r
        dxn_v = dxn_ref[...]
        dg_ref[...] += jnp.sum(dxn_v * xh, axis=0, keepdims=True)
        dxh = dxn_v * g_ref[...]
        dx_ref[...] = dy_ref[...] + r * (dxh - xh * jnp.mean(dxh * xh, axis=-1, keepdims=True))

    row = pl.BlockSpec((tm, D), lambda i: (i, 0))
    vec = pl.BlockSpec((1, D), lambda i: (0, 0))
    return pl.pallas_call(
        body, name=name, grid=(S // tm,),
        in_specs=[row, vec, row, row], out_specs=[row, vec],
        out_shape=[jax.ShapeDtypeStruct((S, D), F32), jax.ShapeDtypeStruct((1, D), F32)],
        compiler_params=_params("arbitrary"),
    )(x, gain, dxn, dy)


def _ffn_fwd(xn, w, name):
    S, D = xn.shape
    F = w.shape[1]
    tm, tf = _tile(S, 1024, SUBLANE_BF16), _tile(F, 512, LANE)
    nf = F // tf

    def body(xn_ref, w_ref, o_ref, gu_ref):
        @pl.when(pl.program_id(1) == 0)
        def _():
            o_ref[...] = jnp.zeros_like(o_ref)

        xnv = xn_ref[...]
        g = _dot(xnv, w_ref[0], "nt")
        u = _dot(xnv, w_ref[1], "nt")
        gu_ref[0] = g.astype(BF16)
        gu_ref[1] = u.astype(BF16)
        h = (g * jax.nn.sigmoid(g) * u).astype(BF16)
        o_ref[...] += _dot(h, w_ref[2], "nn")

    row = pl.BlockSpec((tm, D), lambda i, f: (i, 0))
    return pl.pallas_call(
        body, name=name, grid=(S // tm, nf),
        in_specs=[row, pl.BlockSpec((3, tf, D), lambda i, f: (0, f, 0))],
        out_specs=[row, pl.BlockSpec((2, tm, tf), lambda i, f: (0, i, f))],
        out_shape=[jax.ShapeDtypeStruct((S, D), F32), jax.ShapeDtypeStruct((2, S, F), BF16)],
        compiler_params=_params("parallel", "arbitrary"),
    )(xn, w)


def _ffn_bwd_dx(dy, w, gu, name):
    S, D = dy.shape
    F = w.shape[1]
    tm, tf = _tile(S, 1024, SUBLANE_BF16), _tile(F, 512, LANE)
    nf = F // tf

    def body(dy_ref, w_ref, gu_ref, dxn_ref, dout_ref, t_ref):
        @pl.when(pl.program_id(1) == 0)
        def _():
            dxn_ref[...] = jnp.zeros_like(dxn_ref)
            dout_ref[...] = (0.5 * dy_ref[...]).astype(BF16)

        g = gu_ref[0].astype(F32)
        u = gu_ref[1].astype(F32)
        sig = jax.nn.sigmoid(g)
        silu = g * sig
        t_ref[2] = (silu * u).astype(BF16)
        dh = _dot(dout_ref[...], w_ref[2], "nt")
        dg = (dh * (u * (sig * (1.0 + g * (1.0 - sig))))).astype(BF16)
        du = (dh * silu).astype(BF16)
        t_ref[0] = dg
        t_ref[1] = du
        dxn_ref[...] += _dot(dg, w_ref[0], "nn") + _dot(du, w_ref[1], "nn")

    row = pl.BlockSpec((tm, D), lambda i, f: (i, 0), pipeline_mode=pl.Buffered(1))
    return pl.pallas_call(
        body, name=name, grid=(S // tm, nf),
        in_specs=[row, pl.BlockSpec((3, tf, D), lambda i, f: (0, f, 0)),
                  pl.BlockSpec((2, tm, tf), lambda i, f: (0, i, f))],
        out_specs=[row, row, pl.BlockSpec((3, tm, tf), lambda i, f: (0, i, f))],
        out_shape=[jax.ShapeDtypeStruct((S, D), F32), jax.ShapeDtypeStruct((S, D), BF16),
                   jax.ShapeDtypeStruct((3, S, F), BF16)],
        compiler_params=_params("parallel", "arbitrary"),
    )(dy, w, gu)


def _ffn_bwd_dw(xn, dout, t, name):
    S, D = xn.shape
    F = t.shape[2]
    ts, tf = _tile(S, 1024, LANE), _tile(F, 512, LANE)
    ns = S // ts

    def body(xn_ref, dout_ref, t_ref, dw_ref, acc):
        s = pl.program_id(1)

        @pl.when(s == 0)
        def _():
            acc[...] = jnp.zeros_like(acc)

        xnv = xn_ref[...]
        acc[0] += _dot(t_ref[0], xnv, "tn")
        acc[1] += _dot(t_ref[1], xnv, "tn")
        acc[2] += _dot(t_ref[2], dout_ref[...], "tn")

        @pl.when(s == ns - 1)
        def _():
            dw_ref[...] = acc[...].astype(BF16)

    row = pl.BlockSpec((ts, D), lambda f, s: (s, 0))
    return pl.pallas_call(
        body, name=name, grid=(F // tf, ns),
        in_specs=[row, row, pl.BlockSpec((3, ts, tf), lambda f, s: (0, s, f))],
        out_specs=pl.BlockSpec((3, tf, D), lambda f, s: (0, f, 0)),
        out_shape=jax.ShapeDtypeStruct((3, F, D), BF16),
        scratch_shapes=[pltpu.VMEM((3, tf, D), F32)],
        compiler_params=_params("parallel", "arbitrary"),
    )(xn, dout, t)


def _loss_grad(x, ffn, target, name):
    S, D = x.shape
    tm = _tile(S, 512, 8)

    def body(x_ref, f_ref, t_ref, dy_ref, l_ref):
        @pl.when(pl.program_id(0) == 0)
        def _():
            l_ref[...] = jnp.zeros_like(l_ref)

        e = (x_ref[...] + 0.5 * f_ref[...]) - t_ref[...]
        dy_ref[...] = e * (1.0 / D)
        l_ref[...] += 0.5 * jnp.sum(jnp.sum(e * e, axis=-1, keepdims=True) * (1.0 / D))

    row = pl.BlockSpec((tm, D), lambda i: (i, 0))
    one = pl.BlockSpec((8, LANE), lambda i: (0, 0))
    return pl.pallas_call(
        body, name=name, grid=(S // tm,),
        in_specs=[row, row, row], out_specs=[row, one],
        out_shape=[jax.ShapeDtypeStruct((S, D), F32), jax.ShapeDtypeStruct((8, LANE), F32)],
        compiler_params=_params("arbitrary"),
    )(x, ffn, target)


def _adamw(w, g, m, v, name):
    shape = w.shape
    C = shape[-1]
    R = math.prod(shape[:-1])
    tr = _tile(R, max(8, (3 << 18) // C // 8 * 8), 8)
    c1 = 1.0 / (1.0 - ADAM_B1 ** ADAM_STEP)
    c2 = 1.0 / (1.0 - ADAM_B2 ** ADAM_STEP)

    def body(w_ref, g_ref, m_ref, v_ref, d_ref, nm_ref, nv_ref):
        gv = g_ref[...]
        nm = ADAM_B1 * m_ref[...] + (1.0 - ADAM_B1) * gv
        nv = ADAM_B2 * v_ref[...] + (1.0 - ADAM_B2) * (gv * gv)
        nm_ref[...] = nm
        nv_ref[...] = nv
        d_ref[...] = -ADAM_LR * ((nm * c1) / (jnp.sqrt(nv * c2) + ADAM_EPS) + ADAM_WD * w_ref[...])

    blk = pl.BlockSpec((tr, C), lambda i: (i, 0))
    sds = jax.ShapeDtypeStruct((R, C), F32)
    outs = pl.pallas_call(
        body, name=name, grid=(R // tr,),
        in_specs=[blk] * 4, out_specs=[blk] * 3, out_shape=[sds] * 3,
        compiler_params=_params("parallel"),
    )(*(a.reshape(R, C) for a in (w, g, m, v)))
    return tuple(o.reshape(shape) for o in outs)


def _me():
    return lax.axis_index("x"), lax.axis_index("y"), lax.axis_index("c")


def _add_core_halves(grad, got, c_idx, name):
    n, nk, _, r, C = grad.shape
    tr = _tile(r, 1024, SUBLANE_BF16)

    def body(c_ref, g_ref, r_ref, o_ref):
        o_ref[...] = (g_ref[...].astype(F32) + r_ref[...].astype(F32)).astype(BF16)

    return pl.pallas_call(
        body, name=name,
        grid_spec=pltpu.PrefetchScalarGridSpec(
            num_scalar_prefetch=1, grid=(n, nk, r // tr),
            in_specs=[pl.BlockSpec((None, None, None, tr, C), lambda i, k, t, c: (i, k, c[0], t, 0)),
                      pl.BlockSpec((None, None, tr, C), lambda i, k, t, c: (i, k, t, 0))],
            out_specs=pl.BlockSpec((None, None, tr, C), lambda i, k, t, c: (i, k, t, 0))),
        out_shape=jax.ShapeDtypeStruct((n, nk, r, C), BF16),
        compiler_params=_params("parallel", "parallel", "parallel"),
    )(c_idx, grad, got)


HBM = pl.BlockSpec(memory_space=pltpu.HBM)
SEM = pl.BlockSpec(memory_space=pltpu.SEMAPHORE)
EFFECT = pltpu.SideEffectType.DATAFLOW_SIDE_EFFECTING


def _push_start(srcs, lands, plan, after, name):
    ns, nl = len(srcs), len(lands)
    ncp = len(plan([None] * ns, [None] * nl, dry=True))
    extra = [] if after is None else [after]

    def body(*refs):
        src_refs, land_refs = refs[:ns], refs[ns:ns + nl]
        send_sems, recv_sems = refs[ns + nl + len(extra)], refs[ns + nl + len(extra) + 1]
        token = refs[-1]
        for i, (s, d, to) in enumerate(plan(src_refs, land_refs)):
            pltpu.make_async_remote_copy(src_ref=s, dst_ref=d, send_sem=send_sems.at[i], recv_sem=recv_sems.at[i],
                                         device_id=to, device_id_type=MESH).start()
        token[...] = jnp.zeros_like(token)

    out = pl.pallas_call(
        body, name=name,
        out_shape=(pltpu.SemaphoreType.DMA((ncp,)), pltpu.SemaphoreType.DMA((ncp,)),
                   *[pltpu.HBM(a.shape, a.dtype) for a in srcs], *[pltpu.HBM(a.shape, a.dtype) for a in lands],
                   jax.ShapeDtypeStruct((8, LANE), F32)),
        in_specs=[HBM] * (ns + nl) + [ANY] * len(extra),
        out_specs=(SEM, SEM, *[HBM] * (ns + nl), pl.BlockSpec(memory_space=pltpu.VMEM)),
        input_output_aliases={i: 2 + i for i in range(ns + nl)},
        compiler_params=pltpu.CompilerParams(has_side_effects=EFFECT),
    )(*[pltpu.with_memory_space_constraint(a, pltpu.HBM) for a in srcs + lands], *extra)
    return out[0], out[1], list(out[2:2 + ns]), list(out[2 + ns:2 + ns + nl]), out[-1]


def _push_wait(send_sems, recv_sems, srcs, lands, plan, after, name):
    ns, nl = len(srcs), len(lands)
    after = list(after) if isinstance(after, (list, tuple)) else [after]

    def body(*refs):
        src_refs, land_refs = refs[:ns], refs[ns:ns + nl]
        send, recv = refs[ns + nl], refs[ns + nl + 1]
        for i, (s, d, to) in enumerate(plan(src_refs, land_refs)):
            cp = pltpu.make_async_remote_copy(src_ref=s, dst_ref=d, send_sem=send.at[i], recv_sem=recv.at[i],
                                              device_id=to, device_id_type=MESH)
            cp.wait_send()
            cp.wait_recv()

    out = pl.pallas_call(
        body, name=name,
        out_shape=tuple(pltpu.HBM(a.shape, a.dtype) for a in srcs + lands),
        in_specs=[HBM] * (ns + nl) + [SEM, SEM] + [ANY] * len(after),
        out_specs=tuple([HBM] * (ns + nl)),
        input_output_aliases={i: i for i in range(ns + nl)},
        compiler_params=pltpu.CompilerParams(has_side_effects=EFFECT),
    )(*srcs, *lands, send_sems, recv_sems, *after)
    return list(out[:ns]), list(out[ns:])


def _gather_plan(src_refs, land_refs, dry=False):
    if dry:
        return [None] * (4 * len(src_refs))
    x, y, c = _me()
    me = 4 * x + 2 * y + c
    targets = [(x, y, 1 - c), (1 - x, y, c), (x, 1 - y, c), (1 - x, 1 - y, c)]
    return [(s, l.at[:, me], to) for s, l in zip(src_refs, land_refs) for to in targets]


def _halves_plan(src_refs, land_refs, dry=False):
    if dry:
        return [None] * len(src_refs)
    x, y, c = _me()
    return [(s.at[:, :, 1 - c], l, (x, y, 1 - c)) for s, l in zip(src_refs, land_refs)]


def _chip_plan(src_refs, land_refs, dry=False):
    if dry:
        return [None] * (3 * len(src_refs))
    x, y, c = _me()
    chips = [(1 - x, y), (x, 1 - y), (1 - x, 1 - y)]
    return [(s.at[:, 2 * chip[0] + chip[1]], l.at[j], (*chip, c))
            for s, l in zip(src_refs, land_refs) for j, chip in enumerate(chips)]


def _forward_plan(src_refs, land_refs, dry=False):
    if dry:
        return [None] * (3 * len(land_refs))
    x, y, c = _me()
    chips = [(1 - x, y), (x, 1 - y), (1 - x, 1 - y)]
    plan = []
    for l in land_refs:
        for chip in chips:
            blk = l.at[:, 4 * chip[0] + 2 * chip[1] + c]
            plan.append((blk, blk, (x, y, 1 - c)))
    return plan


def _sum_chip_blocks(sums, got, k_idx, name):
    n, _, r, C = sums.shape
    tr = _tile(r, 512, SUBLANE_BF16)

    def body(k_ref, s_ref, r_ref, o_ref):
        acc = s_ref[...].astype(F32)
        for j in range(3):
            acc = acc + r_ref[j].astype(F32)
        o_ref[...] = acc

    return pl.pallas_call(
        body, name=name,
        grid_spec=pltpu.PrefetchScalarGridSpec(
            num_scalar_prefetch=1, grid=(n, r // tr),
            in_specs=[pl.BlockSpec((None, None, tr, C), lambda i, t, k: (i, k[0], t, 0)),
                      pl.BlockSpec((3, None, tr, C), lambda i, t, k: (0, i, t, 0))],
            out_specs=pl.BlockSpec((None, tr, C), lambda i, t, k: (i, t, 0))),
        out_shape=jax.ShapeDtypeStruct((n, r, C), F32),
        compiler_params=_params("parallel", "parallel"),
    )(k_idx, sums, got)


def _all_reduce_small(v, name, after=()):
    R = v.shape[0]
    after = list(after)

    def body(*refs):
        v_ref = refs[0]
        o_ref, buf, send_sems, recv_sems = refs[1 + len(after):]
        x, y, c = _me()
        me = 4 * x + 2 * y + c
        buf[me] = v_ref[...]
        copies = []
        for k in range(1, N_DEV):
            peer = (x ^ (k >> 2), y ^ ((k >> 1) & 1), c ^ (k & 1))
            copies.append(pltpu.make_async_remote_copy(
                src_ref=v_ref, dst_ref=buf.at[me],
                send_sem=send_sems.at[k - 1], recv_sem=recv_sems.at[k - 1],
                device_id=peer, device_id_type=MESH))
        for cp in copies:
            cp.start()
        for cp in copies:
            cp.wait()
        acc = buf[0]
        for d in range(1, N_DEV):
            acc = acc + buf[d]
        o_ref[...] = acc

    vm = pl.BlockSpec(memory_space=pltpu.VMEM)
    return pl.pallas_call(
        body, name=name, in_specs=[vm] + [ANY] * len(after), out_specs=vm,
        out_shape=jax.ShapeDtypeStruct((R, LANE), F32),
        scratch_shapes=[pltpu.VMEM((N_DEV, R, LANE), F32),
                        pltpu.SemaphoreType.DMA((N_DEV - 1,)), pltpu.SemaphoreType.DMA((N_DEV - 1,))],
        compiler_params=pltpu.CompilerParams(vmem_limit_bytes=VMEM_LIMIT_BYTES),
    )(v, *after)


def _pack_rows(parts):
    flat = jnp.concatenate([p.reshape(-1).astype(F32) for p in parts])
    n = flat.shape[0]
    rows = -(-n // (8 * LANE)) * 8
    flat = jnp.pad(flat, (0, rows * LANE - n))
    return flat.reshape(rows, LANE)


def _unpack_rows(packed, shapes):
    flat = packed.reshape(-1)
    out, off = [], 0
    for s in shapes:
        n = math.prod(s)
        out.append(flat[off:off + n].reshape(s))
        off += n
    return out


CONV_HALO = 32


def _conv_fwd(u, conv_w, conv_b, cn_g, cn_b, name):
    S = u.shape[0]
    C = conv_w.shape[1]
    T = _tile(S, 256, CONV_HALO)
    hb = T // CONV_HALO

    def body(av_ref, ag_ref, pv_ref, pg_ref, w_ref, b_ref, g_ref, bb_ref, out_ref, y_ref, scr):
        i = pl.program_id(0)
        prev = pv_ref[...] * jax.nn.sigmoid(pg_ref[...])
        scr[0:CONV_HALO, :] = jnp.where(i > 0, prev, 0.0)
        scr[CONV_HALO:CONV_HALO + T, :] = av_ref[...] * jax.nn.sigmoid(ag_ref[...])
        for s in range(C // LANE):
            sl = slice(s * LANE, (s + 1) * LANE)
            acc = jnp.broadcast_to(b_ref[:, sl], (T, LANE))
            for j in range(CONV_WIDTH):
                acc = acc + w_ref[j:j + 1, sl] * scr[pl.ds(CONV_HALO - (CONV_WIDTH - 1) + j, T), sl]
            y_ref[:, sl] = acc
        acc = y_ref[...]
        mu = jnp.mean(acc, axis=-1, keepdims=True)
        xc = acc - mu
        var = jnp.mean(xc * xc, axis=-1, keepdims=True)
        ln = xc * lax.rsqrt(var + EPS) * g_ref[...] + bb_ref[...]
        out_ref[...] = (ln * jax.nn.sigmoid(ln)).astype(BF16)

    def cur(cb):
        return pl.BlockSpec((T, C), lambda i: (i, cb))

    def halo(cb):
        return pl.BlockSpec((CONV_HALO, C), lambda i: (jnp.maximum(i * hb - 1, 0), cb))

    vec = pl.BlockSpec((1, C), lambda i: (0, 0))
    return pl.pallas_call(
        body, name=name, grid=(S // T,),
        in_specs=[cur(0), cur(1), halo(0), halo(1), pl.BlockSpec((CONV_WIDTH, C), lambda i: (0, 0)), vec, vec, vec],
        out_specs=[pl.BlockSpec((T, C), lambda i: (i, 0))] * 2,
        out_shape=[jax.ShapeDtypeStruct((S, C), BF16), jax.ShapeDtypeStruct((S, C), F32)],
        scratch_shapes=[pltpu.VMEM((T + CONV_HALO, C), F32)],
        compiler_params=_params("parallel"),
    )(u, u, u, u, conv_w, conv_b, cn_g, cn_b)


def _conv_bwd_norm(dz, y, cn_g, cn_b, name):
    S, C = y.shape
    T = _tile(S, 256, 8)

    def body(dz_ref, y_ref, g_ref, bb_ref, dy_ref, dg_ref, db_ref):
        @pl.when(pl.program_id(0) == 0)
        def _():
            dg_ref[...] = jnp.zeros_like(dg_ref)
            db_ref[...] = jnp.zeros_like(db_ref)

        yv = y_ref[...]
        mu = jnp.mean(yv, axis=-1, keepdims=True)
        xc = yv - mu
        rstd = lax.rsqrt(jnp.mean(xc * xc, axis=-1, keepdims=True) + EPS)
        xh = xc * rstd
        ln = xh * g_ref[...] + bb_ref[...]
        sg = jax.nn.sigmoid(ln)
        dln = dz_ref[...] * (sg * (1.0 + ln * (1.0 - sg)))
        dg_ref[...] += jnp.sum(dln * xh, axis=0, keepdims=True)
        db_ref[...] += jnp.sum(dln, axis=0, keepdims=True)
        dxh = dln * g_ref[...]
        dy_ref[...] = rstd * (dxh - jnp.mean(dxh, axis=-1, keepdims=True)
                              - xh * jnp.mean(dxh * xh, axis=-1, keepdims=True))

    row = pl.BlockSpec((T, C), lambda i: (i, 0))
    vec = pl.BlockSpec((1, C), lambda i: (0, 0))
    return pl.pallas_call(
        body, name=name, grid=(S // T,),
        in_specs=[row, row, vec, vec], out_specs=[row, vec, vec],
        out_shape=[jax.ShapeDtypeStruct((S, C), F32), jax.ShapeDtypeStruct((1, C), F32),
                   jax.ShapeDtypeStruct((1, C), F32)],
        compiler_params=_params("arbitrary"),
    )(dz, y, cn_g, cn_b)


def _conv_bwd_taps(u, dy, conv_w, name):
    S, C = dy.shape
    T = _tile(S, 256, CONV_HALO)
    hb = T // CONV_HALO
    nt = S // T
    ns = C // LANE
    W1 = CONV_WIDTH - 1

    def body(av_ref, ag_ref, pv_ref, pg_ref, dy_ref, dn_ref, w_ref, dv_ref, dg_ref, dw_ref, db_ref, a_scr, d_scr):
        i = pl.program_id(1)

        @pl.when(i == 0)
        def _():
            dw_ref[...] = jnp.zeros_like(dw_ref)
            db_ref[...] = jnp.zeros_like(db_ref)

        av, sg = av_ref[...], jax.nn.sigmoid(ag_ref[...])
        prev = pv_ref[...] * jax.nn.sigmoid(pg_ref[...])
        a_scr[0:CONV_HALO, :] = jnp.where(i > 0, prev, 0.0)
        a_scr[CONV_HALO:CONV_HALO + T, :] = av * sg
        dyv = dy_ref[...]
        d_scr[0:T, :] = dyv
        d_scr[T:T + CONV_HALO, :] = jnp.where(i < nt - 1, dn_ref[...], 0.0)
        da = jnp.zeros((T, LANE), F32)
        for j in range(CONV_WIDTH):
            da = da + w_ref[j:j + 1, :] * d_scr[pl.ds(W1 - j, T), :]
            dw_ref[j:j + 1, :] += jnp.sum(dyv * a_scr[pl.ds(CONV_HALO - W1 + j, T), :], axis=0, keepdims=True)
        db_ref[...] += jnp.sum(dyv, axis=0, keepdims=True)
        dv_ref[...] = (da * sg).astype(BF16)
        dg_ref[...] = (da * av * sg * (1.0 - sg)).astype(BF16)

    def cur(part):
        return pl.BlockSpec((T, LANE), lambda cb, i: (i, part * ns + cb))

    def halo(part):
        return pl.BlockSpec((CONV_HALO, LANE), lambda cb, i: (jnp.maximum(i * hb - 1, 0), part * ns + cb))

    nxt = pl.BlockSpec((CONV_HALO, LANE), lambda cb, i: (jnp.minimum((i + 1) * hb, S // CONV_HALO - 1), cb))
    row = pl.BlockSpec((T, LANE), lambda cb, i: (i, cb))
    return pl.pallas_call(
        body, name=name, grid=(ns, nt),
        in_specs=[cur(0), cur(1), halo(0), halo(1), row, nxt, pl.BlockSpec((CONV_WIDTH, LANE), lambda cb, i: (0, cb))],
        out_specs=[row, row, pl.BlockSpec((CONV_HALO, LANE), lambda cb, i: (0, cb)),
                   pl.BlockSpec((1, LANE), lambda cb, i: (0, cb))],
        out_shape=[jax.ShapeDtypeStruct((S, C), BF16), jax.ShapeDtypeStruct((S, C), BF16),
                   jax.ShapeDtypeStruct((CONV_HALO, C), F32), jax.ShapeDtypeStruct((1, C), F32)],
        scratch_shapes=[pltpu.VMEM((T + CONV_HALO, LANE), F32), pltpu.VMEM((T + CONV_HALO, LANE), F32)],
        compiler_params=_params("parallel", "arbitrary"),
    )(u, u, u, u, dy, dy, conv_w)


def _rope_tables(S):
    half = HEAD_DIM // 2
    inv = jnp.exp(-math.log(ROPE_THETA) * jnp.arange(half, dtype=F32) / half)
    ang = jnp.arange(S, dtype=jnp.int32).astype(F32)[:, None] * inv[None, :]
    cos, sin = jnp.cos(ang), jnp.sin(ang)
    return jnp.concatenate([cos, cos], axis=1), jnp.concatenate([-sin, sin], axis=1)


def _qkv_prep(u, qn_g, kn_g, cos, sin, cb0, name):
    S = u.shape[0]
    A = (u.shape[1] // (cb0 + 3))
    H = A // HEAD_DIM
    T = _tile(S, 256, SUBLANE_BF16)
    scale = HEAD_DIM ** -0.5

    def body(q_ref, k_ref, v_ref, qg_ref, kg_ref, cos_ref, sin_ref, qo_ref, ko_ref, vo_ref):
        cosv, sinv = cos_ref[...], sin_ref[...]
        for h in range(H):
            sl = slice(h * HEAD_DIM, (h + 1) * HEAD_DIM)
            for x_ref, g_ref, o_ref, sc in ((q_ref, qg_ref, qo_ref, scale), (k_ref, kg_ref, ko_ref, 1.0)):
                xv = x_ref[:, sl]
                xn = xv * lax.rsqrt(jnp.mean(xv * xv, axis=-1, keepdims=True) + EPS) * g_ref[...]
                y = xn * cosv + pltpu.roll(xn, HEAD_DIM // 2, 1) * sinv
                o_ref[:, sl] = (y * sc).astype(BF16)
        vo_ref[...] = v_ref[...].astype(BF16)

    def col(cb):
        return pl.BlockSpec((T, A), lambda i: (i, cb))

    vec = pl.BlockSpec((1, HEAD_DIM), lambda i: (0, 0))
    tab = pl.BlockSpec((T, HEAD_DIM), lambda i: (i, 0))
    out = pl.BlockSpec((T, A), lambda i: (i, 0))
    return pl.pallas_call(
        body, name=name, grid=(S // T,),
        in_specs=[col(cb0), col(cb0 + 1), col(cb0 + 2), vec, vec, tab, tab],
        out_specs=[out] * 3, out_shape=[jax.ShapeDtypeStruct((S, A), BF16)] * 3,
        compiler_params=_params("parallel"),
    )(u, u, u, qn_g, kn_g, cos, sin)


def _qkv_prep_bwd(u, dqs, dks, dvs, qn_g, kn_g, cos, sin, cb0, name):
    S = u.shape[0]
    A = dqs[0].shape[1]
    H = A // HEAD_DIM
    T = _tile(S, 256, SUBLANE_BF16)
    nb = len(dqs)
    scale = HEAD_DIM ** -0.5

    def body(*refs):
        q_ref, k_ref, qg_ref, kg_ref, cos_ref, sin_ref = refs[:6]
        dq_refs, dk_refs, dv_refs = refs[6:6 + nb], refs[6 + nb:6 + 2 * nb], refs[6 + 2 * nb:6 + 3 * nb]
        dqo_ref, dko_ref, dvo_ref, dqg_ref, dkg_ref = refs[6 + 3 * nb:]

        @pl.when(pl.program_id(0) == 0)
        def _():
            dqg_ref[...] = jnp.zeros_like(dqg_ref)
            dkg_ref[...] = jnp.zeros_like(dkg_ref)

        cosv, sinv = cos_ref[...], sin_ref[...]
        for h in range(H):
            sl = slice(h * HEAD_DIM, (h + 1) * HEAD_DIM)
            for x_ref, g_ref, d_refs, o_ref, dg_ref, sc in ((q_ref, qg_ref, dq_refs, dqo_ref, dqg_ref, scale),
                                                          (k_ref, kg_ref, dk_refs, dko_ref, dkg_ref, 1.0)):
                dy = d_refs[0][:, sl]
                for r in d_refs[1:]:
                    dy = dy + r[:, sl]
                dy = dy * sc
                dxn = dy * cosv + pltpu.roll(dy * sinv, HEAD_DIM // 2, 1)
                xv = x_ref[:, sl]
                r = lax.rsqrt(jnp.mean(xv * xv, axis=-1, keepdims=True) + EPS)
                xh = xv * r
                dg_ref[...] += jnp.sum(dxn * xh, axis=0, keepdims=True)
                dxh = dxn * g_ref[...]
                o_ref[:, sl] = (r * (dxh - xh * jnp.mean(dxh * xh, axis=-1, keepdims=True))).astype(BF16)
        dv = dv_refs[0][...]
        for r in dv_refs[1:]:
            dv = dv + r[...]
        dvo_ref[...] = dv.astype(BF16)

    def col(cb):
        return pl.BlockSpec((T, A), lambda i: (i, cb))

    vec = pl.BlockSpec((1, HEAD_DIM), lambda i: (0, 0))
    tab = pl.BlockSpec((T, HEAD_DIM), lambda i: (i, 0))
    row = pl.BlockSpec((T, A), lambda i: (i, 0))
    return pl.pallas_call(
        body, name=name, grid=(S // T,),
        in_specs=[col(cb0), col(cb0 + 1), vec, vec, tab, tab] + [row] * (3 * nb),
        out_specs=[row, row, row, vec, vec],
        out_shape=[jax.ShapeDtypeStruct((S, A), BF16)] * 3 + [jax.ShapeDtypeStruct((1, HEAD_DIM), F32)] * 2,
        compiler_params=_params("arbitrary"),
    )(u, u, qn_g, kn_g, cos, sin, *dqs, *dks, *dvs)


ATT_TILE_FWD = 512
ATT_TILE_BWD = 512
NEG = -1e30


def _attn_bias(tile):
    span = max(window for window, _ in DIL_PATTERNS)
    nw = -(-span // tile) + 1
    dist = (jnp.arange(nw)[:, None, None] * tile + jnp.arange(tile)[None, :, None] - jnp.arange(tile)[None, None, :])
    mult = sum(((dist >= 0) & (dist <= window) & (dist % dil == 0)).astype(F32) for window, dil in DIL_PATTERNS)
    return jnp.where(mult > 0, jnp.log(jnp.maximum(mult, 1.0)), NEG)


def _attn_fwd(q, k, v, bias, name):
    S, A = q.shape
    H = A // HEAD_DIM
    nw, T, _ = bias.shape
    nq = S // T

    def body(q_ref, k_ref, v_ref, b_ref, ob_ref, of_ref, l_ref):
        i = pl.program_id(1)
        qv = q_ref[...]
        for w in range(nw):
            blk = i - w
            start = pl.multiple_of(jnp.maximum(blk, 0) * T, T)
            s = _dot(qv, k_ref[pl.ds(start, T), :], "nt") + b_ref[w]
            if w == 0:
                mx = jnp.max(s, axis=-1, keepdims=True)
                p = jnp.exp(s - mx)
                den = jnp.sum(p, axis=-1, keepdims=True)
                o = _dot(p.astype(BF16), v_ref[pl.ds(start, T), :], "nn")
            else:
                s = s + jnp.where(blk >= 0, 0.0, NEG)
                new = jnp.maximum(mx, jnp.max(s, axis=-1, keepdims=True))
                scale = jnp.exp(mx - new)
                p = jnp.exp(s - new)
                den = scale * den + jnp.sum(p, axis=-1, keepdims=True)
                o = scale * o + _dot(p.astype(BF16), v_ref[pl.ds(start, T), :], "nn")
                mx = new
        o = o / den
        ob_ref[...] = o.astype(BF16)
        of_ref[...] = o
        l_ref[...] = mx + jnp.log(den)

    blk = pl.BlockSpec((T, HEAD_DIM), lambda h, i: (i, h))
    full = pl.BlockSpec((S, HEAD_DIM), lambda h, i: (0, h))
    return pl.pallas_call(
        body, name=name, grid=(H, nq),
        in_specs=[blk, full, full, pl.BlockSpec((nw, T, T), lambda h, i: (0, 0, 0))],
        out_specs=[blk, blk, pl.BlockSpec((None, T, 1), lambda h, i: (h, i, 0))],
        out_shape=[jax.ShapeDtypeStruct((S, A), BF16), jax.ShapeDtypeStruct((S, A), F32),
                   jax.ShapeDtypeStruct((H, S, 1), F32)],
        compiler_params=_params("parallel", "arbitrary"),
    )(q, k, v, bias)


def _attn_dq(q, k, v, dz, cb0, o, lse, bias, name):
    S, A = q.shape
    H = A // HEAD_DIM
    nw, T, _ = bias.shape
    nq = S // T

    def body(q_ref, k_ref, v_ref, do_ref, o_ref, l_ref, b_ref, dq_ref, d_ref):
        i = pl.program_id(1)
        qv, dof = q_ref[...], do_ref[...]
        dov = dof.astype(BF16)
        delta = jnp.sum(dof * o_ref[...], axis=-1, keepdims=True)
        d_ref[...] = delta
        lv = l_ref[...]
        dq = jnp.zeros((T, HEAD_DIM), F32)
        for w in range(nw):
            blk = i - w
            start = pl.multiple_of(jnp.maximum(blk, 0) * T, T)
            kv = k_ref[pl.ds(start, T), :]
            s = _dot(qv, kv, "nt") + b_ref[w] + jnp.where(blk >= 0, 0.0, NEG)
            p = jnp.exp(s - lv)
            ds = (p * (_dot(dov, v_ref[pl.ds(start, T), :], "nt") - delta)).astype(BF16)
            dq = dq + _dot(ds, kv, "nn")
        dq_ref[...] = dq

    blk = pl.BlockSpec((T, HEAD_DIM), lambda h, i: (i, h))
    full = pl.BlockSpec((S, HEAD_DIM), lambda h, i: (0, h))
    col = pl.BlockSpec((None, T, 1), lambda h, i: (h, i, 0))
    return pl.pallas_call(
        body, name=name, grid=(H, nq),
        in_specs=[blk, full, full, pl.BlockSpec((T, HEAD_DIM), lambda h, i: (i, cb0 + h)), blk, col,
                  pl.BlockSpec((nw, T, T), lambda h, i: (0, 0, 0))],
        out_specs=[blk, col],
        out_shape=[jax.ShapeDtypeStruct((S, A), F32), jax.ShapeDtypeStruct((H, S, 1), F32)],
        compiler_params=_params("parallel", "arbitrary"),
    )(q, k, v, dz, o, lse, bias)


def _attn_dkv(q, k, v, dz, cb0, lse, delta, bias, name):
    S, A = q.shape
    H = A // HEAD_DIM
    nw, T, _ = bias.shape
    nq = S // T

    def body(k_ref, v_ref, q_ref, do_ref, l_ref, d_ref, b_ref, dk_ref, dv_ref):
        m = pl.program_id(1)
        kv, vv = k_ref[...], v_ref[...]
        dk = jnp.zeros((T, HEAD_DIM), F32)
        dv = jnp.zeros((T, HEAD_DIM), F32)
        for w in range(nw):
            blk = m + w
            start = pl.multiple_of(jnp.minimum(blk, nq - 1) * T, T)
            qv = q_ref[pl.ds(start, T), :]
            dov = do_ref[pl.ds(start, T), :].astype(BF16)
            s = _dot(qv, kv, "nt") + b_ref[w] + jnp.where(blk < nq, 0.0, NEG)
            p = jnp.exp(s - l_ref[pl.ds(start, T), :])
            dv = dv + _dot(p.astype(BF16), dov, "tn")
            ds = (p * (_dot(dov, vv, "nt") - d_ref[pl.ds(start, T), :])).astype(BF16)
            dk = dk + _dot(ds, qv, "tn")
        dk_ref[...] = dk
        dv_ref[...] = dv

    blk = pl.BlockSpec((T, HEAD_DIM), lambda h, m: (m, h))
    full = pl.BlockSpec((S, HEAD_DIM), lambda h, m: (0, h))
    col = pl.BlockSpec((None, S, 1), lambda h, m: (h, 0, 0))
    sds = jax.ShapeDtypeStruct((S, A), F32)
    return pl.pallas_call(
        body, name=name, grid=(H, nq),
        in_specs=[blk, blk, full, pl.BlockSpec((S, HEAD_DIM), lambda h, m: (0, cb0 + h)), col, col,
                  pl.BlockSpec((nw, T, T), lambda h, m: (0, 0, 0))],
        out_specs=[blk, blk], out_shape=[sds, sds],
        compiler_params=_params("parallel", "arbitrary"),
    )(k, v, q, dz, lse, delta, bias)


def _even_mixer(u, conv_w, conv_b, cn_g, cn_b, qn_g, kn_g, tag):
    S = u.shape[0]
    C = conv_w.shape[1]
    A = (u.shape[1] - 2 * C) // 3
    assert A == C, "column-block addressing of u assumes equal conv and attention widths"
    cos, sin = _rope_tables(S)
    bias = _attn_bias(_tile(S, ATT_TILE_BWD, LANE))
    a_out, y = _conv_fwd(u, conv_w, conv_b, cn_g, cn_b, "conv_fwd" + tag)
    q, k, v = _qkv_prep(u, qn_g, kn_g, cos, sin, 2, "qkv_prep" + tag)
    ob, of, lse = _attn_fwd(q, k, v, _attn_bias(_tile(S, ATT_TILE_FWD, LANE)), "attn_fwd" + tag)
    z = jnp.concatenate([a_out, ob], axis=1)

    def backward(dz):
        dy, d_cn_g, d_cn_b = _conv_bwd_norm(dz, y, cn_g, cn_b, "conv_bwd_norm" + tag)
        d_val, d_gate, d_w, d_b = _conv_bwd_taps(u, dy, conv_w, "conv_bwd_taps" + tag)
        dqp, delta = _attn_dq(q, k, v, dz, C // HEAD_DIM, of, lse, bias, "attn_dq" + tag)
        dkp, dvp = _attn_dkv(q, k, v, dz, C // HEAD_DIM, lse, delta, bias, "attn_dkv" + tag)
        dq, dk, dv, d_qn, d_kn = _qkv_prep_bwd(u, [dqp], [dkp], [dvp], qn_g, kn_g, cos, sin, 2, "qkv_prep_bwd" + tag)
        du = jnp.concatenate([d_val, d_gate, dq, dk, dv], axis=1)
        return du, [d_w[:CONV_WIDTH], d_b[0], d_cn_g[0], d_cn_b[0], d_qn[0], d_kn[0]]

    return z, backward


_LEVELS = (128, 64, 32, 16, 8, 4, 2, 1)


def _chunk_cumsum(g, rows, reverse=False):
    C = g.shape[0]
    d = 1
    while d < C:
        if reverse:
            g = g + jnp.where(rows < C - d, pltpu.roll(g, C - d, 0), 0.0)
        else:
            g = g + jnp.where(rows >= d, pltpu.roll(g, d, 0), 0.0)
        d *= 2
    return g


def _level_ref(b, b_scr, rows, m):
    C = b.shape[0]
    if m >= 8:
        pieces = [jnp.broadcast_to(b_scr[2 * m * j + m - 1:2 * m * j + m, :], (2 * m, LANE)) for j in range(C // (2 * m))]
        return pieces[0] if len(pieces) == 1 else jnp.concatenate(pieces, axis=0)
    pos = rows & (2 * m - 1)
    ref = b
    for p in range(2 * m):
        if p != m - 1:
            ref = jnp.where(pos == p, pltpu.roll(b, (p - (m - 1)) % C, 0), ref)
    return ref


def _level_operands(q, k, b, b_scr, rows, m):
    e = jnp.exp(-jnp.abs(b - _level_ref(b, b_scr, rows, m)))
    return (q * e).astype(BF16), (k * e).astype(BF16)


def _split2(x):
    hi = x.astype(BF16)
    lo = (x - hi.astype(F32)).astype(BF16)
    return jnp.concatenate([hi, lo], axis=1)


def _level_table(n):
    t = jnp.arange(n, dtype=jnp.int32)[:, None]
    s = jnp.arange(n, dtype=jnp.int32)[None, :]
    x = t ^ s
    lvl = sum((x >= (1 << j)).astype(jnp.int32) for j in range(1, n.bit_length()))
    return jnp.where(t > s, lvl, jnp.where(t == s, -1, -2))


def _hgrn_gates(qz, fz, la, lc, oml):
    sq = jax.nn.sigmoid(qz)
    q = qz * sq
    s = jax.nn.sigmoid(fz)
    c = lc + jnp.minimum(fz, 0.0) - jnp.log(1.0 + jnp.exp(-jnp.abs(fz)))
    mx = jnp.maximum(la, c)
    g = mx + jnp.log(1.0 + jnp.exp(-jnp.abs(la - c)))
    k = oml * (1.0 - s)
    return q, sq, k, s, g, c


def _hgrn_fwd(u, la, lc, oml, gn_g, name):
    S = u.shape[0]
    W = u.shape[1] // 4
    H = W // HGRN_KDIM
    C = min(HGRN_CHUNK, S)
    nc = S // C
    levels = [m for m in _LEVELS if m < C]
    HB = C // 2

    def body(qz_ref, fz_ref, iz_ref, gz_ref, la_ref, lc_ref, oml_ref, gn_ref, lvl_ref,
             z_ref, o_ref, a_ref, st_ref, state, b_scr):
        @pl.when(pl.program_id(1) == 0)
        def _():
            state[...] = jnp.zeros_like(state)

        rows = lax.broadcasted_iota(jnp.int32, (C, LANE), 0)
        q, _, k, _, g, _ = _hgrn_gates(qz_ref[...], fz_ref[...], la_ref[...], lc_ref[...], oml_ref[...])
        v = iz_ref[...].astype(BF16)
        b = _chunk_cumsum(g, rows)
        b_scr[...] = b
        lvl = lvl_ref[...]
        qk = jnp.sum(q * k, axis=-1, keepdims=True)
        diag = [jnp.where(lvl == -1, qk[r * HB:(r + 1) * HB], 0.0) for r in range(2)]
        for m in levels[1:]:
            qs, ks = _level_operands(q, k, b, b_scr, rows, m)
            for r in range(2):
                sl = slice(r * HB, (r + 1) * HB)
                diag[r] = jnp.where(lvl == m.bit_length() - 1, _dot(qs[sl], ks[sl], "nt"), diag[r])
        qs, ks = _level_operands(q, k, b, b_scr, rows, HB)
        low = _dot(qs[HB:], ks[:HB], "nt")
        a = jnp.concatenate([jnp.concatenate([diag[0], jnp.zeros((HB, HB), F32)], axis=1),
                             jnp.concatenate([low, diag[1]], axis=1)], axis=0)
        ab = a.astype(BF16)
        a_ref[...] = ab
        st = state[...]
        st_ref[...] = st
        o = _dot(ab, v, "nn") + _dot((q * jnp.exp(b)).astype(BF16), st.astype(BF16), "nt")
        bl = b_scr[C - 1:C, :]
        kh = (k * jnp.exp(bl - b)).astype(BF16)
        state[...] = st * jnp.exp(bl) + _dot(v, kh, "tn")
        o_ref[...] = o
        r = lax.rsqrt(jnp.mean(o * o, axis=-1, keepdims=True) + EPS)
        gz = gz_ref[...]
        z_ref[...] = (o * r * gn_ref[...] * (gz * jax.nn.sigmoid(gz))).astype(BF16)

    def col(off):
        return pl.BlockSpec((C, LANE), lambda h, i: (i, off * H + h))

    vec = pl.BlockSpec((1, LANE), lambda h, i: (0, h))
    tile = pl.BlockSpec((C, LANE), lambda h, i: (i, h))
    return pl.pallas_call(
        body, name=name, grid=(H, nc),
        in_specs=[col(0), col(1), col(2), col(3), vec, vec, vec, vec, pl.BlockSpec((HB, HB), lambda h, i: (0, 0))],
        out_specs=[tile, tile, pl.BlockSpec((None, C, C), lambda h, i: (h, i, 0)),
                   pl.BlockSpec((None, None, LANE, LANE), lambda h, i: (h, i, 0, 0))],
        out_shape=[jax.ShapeDtypeStruct((S, W), BF16), jax.ShapeDtypeStruct((S, W), F32),
                   jax.ShapeDtypeStruct((H, S, C), BF16), jax.ShapeDtypeStruct((H, nc, LANE, LANE), F32)],
        scratch_shapes=[pltpu.VMEM((LANE, LANE), F32), pltpu.VMEM((C, LANE), F32)],
        compiler_params=_params("parallel", "arbitrary"),
    )(u, u, u, u, la, lc, oml, gn_g, _level_table(HB))


def _hgrn_bwd(u, la, lc, oml, gn_g, o, a, st, dz, name):
    S = u.shape[0]
    W = u.shape[1] // 4
    H = W // HGRN_KDIM
    C = min(HGRN_CHUNK, S)
    nc = S // C
    levels = [m for m in _LEVELS if m < C]
    HB = C // 2

    def body(qz_ref, fz_ref, iz_ref, gz_ref, la_ref, lc_ref, oml_ref, gn_ref, o_ref, a_ref, st_ref, dz_ref, lvl_ref,
             du_ref, dla_ref, dlc_ref, doml_ref, dgn_ref, dstate, b_scr):
        @pl.when(pl.program_id(1) == 0)
        def _():
            dstate[...] = jnp.zeros_like(dstate)
            dla_ref[...] = jnp.zeros_like(dla_ref)
            dlc_ref[...] = jnp.zeros_like(dlc_ref)
            doml_ref[...] = jnp.zeros_like(doml_ref)
            dgn_ref[...] = jnp.zeros_like(dgn_ref)

        rows = lax.broadcasted_iota(jnp.int32, (C, LANE), 0)
        la_v, lc_v, oml_v = la_ref[...], lc_ref[...], oml_ref[...]
        qz, fz = qz_ref[...], fz_ref[...]
        q, sq, k, s, g, c = _hgrn_gates(qz, fz, la_v, lc_v, oml_v)
        vf = iz_ref[...]
        v = vf.astype(BF16)

        ov, gz, dzv, gn = o_ref[...], gz_ref[...], dz_ref[...], gn_ref[...]
        r = lax.rsqrt(jnp.mean(ov * ov, axis=-1, keepdims=True) + EPS)
        on = ov * r
        sg = jax.nn.sigmoid(gz)
        silu_g = gz * sg
        dgn_ref[...] += jnp.sum(dzv * on * silu_g, axis=0, keepdims=True)
        du_ref[3] = (dzv * on * gn * (sg * (1.0 + gz * (1.0 - sg)))).astype(BF16)
        don = dzv * gn * silu_g
        do_f = r * (don - on * jnp.mean(don * on, axis=-1, keepdims=True))
        do = do_f.astype(BF16)

        b = _chunk_cumsum(g, rows)
        b_scr[...] = b
        bl = b_scr[C - 1:C, :]
        e = jnp.exp(b)
        ebl = jnp.exp(bl)
        ekl = jnp.exp(bl - b)
        qh = q * e
        kh = k * ekl
        st_v = st_ref[...]
        dst = dstate[...]
        dstb = dst.astype(BF16)

        du_ref[2] = (_dot(a_ref[...], do, "tn") + _dot(kh.astype(BF16), dstb, "nt")).astype(BF16)
        da = _dot(do, v, "nt")
        dqh = _dot(do, st_v.astype(BF16), "nn")
        dkh = _dot(v, dstb, "nn")
        dstate[...] = dst * ebl + _dot(do, qh.astype(BF16), "tn")
        dbl = jnp.sum(dkh * kh, axis=0, keepdims=True) + jnp.sum(dst * st_v, axis=0, keepdims=True) * ebl

        datt = jnp.sum(do_f * vf, axis=-1, keepdims=True)
        dqa = datt * k
        dka = datt * q
        lvl = lvl_ref[...]
        for m in levels:
            ez = jnp.exp(-jnp.abs(b - _level_ref(b, b_scr, rows, m)))
            ks2, qs2 = _split2(k * ez), _split2(q * ez)
            if m == HB:
                gm = da[HB:, :HB].astype(BF16)
                pq = jnp.concatenate([jnp.zeros((HB, 2 * LANE), F32), _dot(gm, ks2[:HB], "nn")], axis=0)
                pk = jnp.concatenate([_dot(gm, qs2[HB:], "tn"), jnp.zeros((HB, 2 * LANE), F32)], axis=0)
            else:
                gms = [jnp.where(lvl == m.bit_length() - 1, da[r * HB:(r + 1) * HB, r * HB:(r + 1) * HB], 0.0).astype(BF16)
                       for r in range(2)]
                pq = jnp.concatenate([_dot(gms[r], ks2[r * HB:(r + 1) * HB], "nn") for r in range(2)], axis=0)
                pk = jnp.concatenate([_dot(gms[r], qs2[r * HB:(r + 1) * HB], "tn") for r in range(2)], axis=0)
            dqa += (pq[:, :LANE] + pq[:, LANE:]) * ez
            dka += (pk[:, :LANE] + pk[:, LANE:]) * ez
        db = q * dqa - k * dka + dqh * qh - dkh * kh
        db = db + jnp.where(rows == C - 1, dbl, 0.0)
        dq = dqa + dqh * e
        dk = dka + dkh * ekl
        dg = _chunk_cumsum(db, rows, reverse=True)

        wa = jnp.exp(la_v - g)
        wc = jnp.exp(c - g)
        du_ref[0] = (dq * (sq * (1.0 + qz * (1.0 - sq)))).astype(BF16)
        du_ref[1] = (dg * wc * (1.0 - s) - dk * oml_v * s * (1.0 - s)).astype(BF16)
        dla_ref[...] += jnp.sum(dg * wa, axis=0, keepdims=True)
        dlc_ref[...] += jnp.sum(dg * wc, axis=0, keepdims=True)
        doml_ref[...] += jnp.sum(dk * (1.0 - s), axis=0, keepdims=True)

    def col(off):
        return pl.BlockSpec((C, LANE), lambda h, i: (nc - 1 - i, off * H + h))

    vec = pl.BlockSpec((1, LANE), lambda h, i: (0, h))
    tile = pl.BlockSpec((C, LANE), lambda h, i: (nc - 1 - i, h))
    a_spec = pl.BlockSpec((None, C, C), lambda h, i: (h, nc - 1 - i, 0))
    st_spec = pl.BlockSpec((None, None, LANE, LANE), lambda h, i: (h, nc - 1 - i, 0, 0))
    vw = jax.ShapeDtypeStruct((1, W), F32)
    return pl.pallas_call(
        body, name=name, grid=(H, nc),
        in_specs=[col(0), col(1), col(2), col(3), vec, vec, vec, vec, tile, a_spec, st_spec, tile,
                  pl.BlockSpec((HB, HB), lambda h, i: (0, 0))],
        out_specs=[pl.BlockSpec((4, C, LANE), lambda h, i: (0, nc - 1 - i, h)), vec, vec, vec, vec],
        out_shape=[jax.ShapeDtypeStruct((4, S, W), BF16), vw, vw, vw, vw],
        scratch_shapes=[pltpu.VMEM((LANE, LANE), F32), pltpu.VMEM((C, LANE), F32)],
        compiler_params=_params("parallel", "arbitrary"),
    )(u, u, u, u, la, lc, oml, gn_g, o, a, st, dz, _level_table(HB))


def _lb_terms(lb_logits, layer):
    p = jax.nn.softmax(lb_logits, axis=0)
    lb = (jnp.cumsum(p, axis=0) - p[0:1])[layer]
    return jnp.log(lb)[None], jnp.log1p(-lb)[None], (1.0 - lb)[None]


def kernel(x, norm_ffn1, ffn1_wg, ffn1_wu, ffn1_wd, norm_mix, norm_ffn2, ffn2_wg, ffn2_wu, ffn2_wd, ev_w_in, ev_conv_w, ev_conv_b, ev_cn_g, ev_cn_b, ev_qn_g, ev_kn_g, ev_w_out, od_w_in, od_lb_logits, od_gn_g, od_w_out, loss_target, m_norm_ffn1, m_ffn1_wg, m_ffn1_wu, m_ffn1_wd, m_norm_mix, m_norm_ffn2, m_ffn2_wg, m_ffn2_wu, m_ffn2_wd, m_ev_w_in, m_ev_conv_w, m_ev_conv_b, m_ev_cn_g, m_ev_cn_b, m_ev_qn_g, m_ev_kn_g, m_ev_w_out, m_od_w_in, m_od_lb_logits, m_od_gn_g, m_od_w_out, v_norm_ffn1, v_ffn1_wg, v_ffn1_wu, v_ffn1_wd, v_norm_mix, v_norm_ffn2, v_ffn2_wg, v_ffn2_wu, v_ffn2_wd, v_ev_w_in, v_ev_conv_w, v_ev_conv_b, v_ev_cn_g, v_ev_cn_b, v_ev_qn_g, v_ev_kn_g, v_ev_w_out, v_od_w_in, v_od_lb_logits, v_od_gn_g, v_od_w_out):
    depth = norm_ffn1.shape[0]
    S, D = x.shape[1], x.shape[2]
    xi, yi, ci = _me()
    dev = 4 * xi + 2 * yi + ci
    c_idx = jnp.reshape(ci, (1,)).astype(jnp.int32)
    k_idx = jnp.reshape(2 * xi + yi, (1,)).astype(jnp.int32)

    def ffn_shard(wg, wu, wd, l):
        return jnp.stack([wg[l].T, wu[l].T, wd[l]]).astype(BF16)

    assert depth == 2, "the exchange schedule below is written for one even and one odd layer"
    sh_ffn1 = [ffn_shard(ffn1_wg, ffn1_wu, ffn1_wd, l) for l in range(depth)]
    sh_ffn2 = [ffn_shard(ffn2_wg, ffn2_wu, ffn2_wd, l) for l in range(depth)]
    sh_ev = [ev_w_in[0].T.astype(BF16)[None], ev_w_out[0].astype(BF16)[None]]
    sh_od = [od_w_in[0].T.astype(BF16)[None], od_w_out[0].astype(BF16)[None]]

    def full(g):
        return g.reshape(g.shape[0], N_DEV * g.shape[2], g.shape[3])

    def gather_begin(shards, after, tag):
        lands = [lax.dynamic_update_slice(lax.empty((s.shape[0], N_DEV) + s.shape[1:], s.dtype), s[:, None],
                                          (0, dev, 0, 0)) for s in shards]
        state = _push_start(shards, lands, _gather_plan, after, "gather_start" + tag)
        return state, state[4][0, 0]

    def gather_arrived(state, after, tag):
        send, recv, srcs, lands, _ = state
        _, lands = _push_wait(send, recv, srcs, lands, _gather_plan, after, "gather_wait" + tag)
        state = _push_start([], lands, _forward_plan, None, "forward_start" + tag)
        return state, state[4][0, 0]

    def gather_done(state, after, tag):
        send, recv, _, lands, _ = state
        _, lands = _push_wait(send, recv, [], lands, _forward_plan, after, "forward_wait" + tag)
        return [full(g) for g in lands]

    def gather_end(state, after, tag):
        state, _ = gather_arrived(state, after, tag)
        return gather_done(state, state[4], tag)

    conv_w_sh, gn_g_sh = ev_conv_w[0], od_gn_g[0]
    cw, cs = conv_w_sh.shape[0], conv_w_sh.shape[1]
    gs = gn_g_sh.shape[0]
    conv_w_z = lax.dynamic_update_slice(jnp.zeros((cw, N_DEV * cs), F32), conv_w_sh, (0, dev * cs))
    gn_g_z = lax.dynamic_update_slice(jnp.zeros((N_DEV * gs,), F32), gn_g_sh, (dev * gs,))
    conv_w_full, gn_g_full = _unpack_rows(
        _all_reduce_small(_pack_rows([conv_w_z, gn_g_z]), "gather_small_params"),
        [conv_w_z.shape, gn_g_z.shape])

    w_ffn1, w_ffn2 = [None] * depth, [None] * depth
    pending, after = {}, conv_w_full
    for key, shards in (("0", [sh_ffn1[0]]), ("1", sh_ev), ("2", [sh_ffn2[0]]), ("3", [sh_ffn1[1]]), ("4", sh_od),
                        ("5", [sh_ffn2[1]])):
        pending[key], _ = gather_begin(shards, after, "_" + key)
        after = pending[key][4]
    start_tok = after[0, 0]
    (w_ffn1[0],) = gather_end(pending.pop("0"), after, "_0")

    def odd_mixer(u, l, tok):
        (la, lc, oml), lb_vjp = jax.vjp(functools.partial(_lb_terms, layer=l), od_lb_logits)
        gn = (gn_g_full + tok)[None]
        zb, o_raw, scores, states = _hgrn_fwd(u, la, lc, oml, gn, f"hgrn_fwd{l}")

        def backward(dz):
            du, dla, dlc, doml, dgn = _hgrn_bwd(u, la, lc, oml, gn, o_raw, scores, states, dz, f"hgrn_bwd{l}")
            (g_lb,) = lb_vjp((dla, dlc, doml))
            return du, [g_lb, dgn[0]]

        return zb, backward

    saved = []
    h = x[0]
    hn = _rms_fwd(h, (norm_ffn1[0] + start_tok)[None], "rms_a0")
    for l in range(depth):
        ffn, gu = _ffn_fwd(hn, w_ffn1[l], f"ffn_fwd_a{l}")
        s1 = (h, hn, gu)
        if l == 0:
            w_in, w_out = gather_end(pending.pop("1"), ffn, "_1")
        else:
            w_in, w_out = gather_done(pending.pop("4"), ffn, "_4")
        w_in_t, w_out = w_in[0], w_out[0]
        h, hn = _resid_rms(h, ffn, norm_mix[l][None], f"rms_mix{l}")
        u = _mm(hn, w_in_t, "nt", F32, f"mix_in{l}")
        key = "2" if l == 0 else "5"
        passing, tok = gather_arrived(pending.pop(key), u, "_" + key)
        if l % 2 == 0:
            zb, core_vjp = _even_mixer(u, conv_w_full, ev_conv_b + tok, ev_cn_g, ev_cn_b, ev_qn_g, ev_kn_g, str(l))
        else:
            zb, core_vjp = odd_mixer(u, l, tok)
        h_mix = h
        h = _mm(zb, w_out, "nn", F32, f"mix_out{l}", res=h)
        sm = (h_mix, hn, zb, core_vjp, w_in_t, w_out)
        (w_ffn2[l],) = gather_done(passing, h, "_" + key)
        tok = 0.0
        if l + 1 < depth:
            passing, tok = gather_arrived(pending.pop("3"), w_ffn2[l], "_3")
        hn = _rms_fwd(h, (norm_ffn2[l] + tok)[None], f"rms_b{l}")
        ffn, gu = _ffn_fwd(hn, w_ffn2[l], f"ffn_fwd_b{l}")
        saved.append((s1, sm, (h, hn, gu)))
        if l + 1 < depth:
            (w_ffn1[l + 1],) = gather_done(passing, ffn, "_3")
            pending["4"], tok = gather_arrived(pending.pop("4"), w_ffn1[l + 1], "_4")
            h, hn = _resid_rms(h, ffn, (norm_ffn1[l + 1] + tok)[None], f"rms_a{l + 1}")

    dy, loss_part = _loss_grad(h, ffn, loss_target[0], "loss_grad")

    def halves_begin(parts, tag):
        parts = [g.reshape(g.shape[0], 4, 2, g.shape[1] // N_DEV, g.shape[2]) for g in parts]
        lands = [lax.empty(g.shape[:2] + g.shape[3:], BF16) for g in parts]
        state = _push_start(parts, lands, _halves_plan, None, "halves_start" + tag)
        return state, state[4][0, 0]

    def chips_begin(state, after, tag):
        send, recv, srcs, lands, _ = state
        parts, got = _push_wait(send, recv, srcs, lands, _halves_plan, after, "halves_wait" + tag)
        sums = [_add_core_halves(g, r, c_idx, f"add_core_halves{tag}_{a}") for a, (g, r) in enumerate(zip(parts, got))]
        lands = [lax.empty((3, s.shape[0]) + s.shape[2:], BF16) for s in sums]
        state = _push_start(sums, lands, _chip_plan, None, "reduce_start" + tag)
        return state, state[4][0, 0]

    def reduce_end(state, after, tag):
        send, recv, srcs, lands, _ = state
        sums, got = _push_wait(send, recv, srcs, lands, _chip_plan, after, "reduce_wait" + tag)
        return [_sum_chip_blocks(s, r, k_idx, f"sum_chip_blocks{tag}_{a}") for a, (s, r) in enumerate(zip(sums, got))]

    def ffn_backward(dy, gain, w, sv, tag, on_dw, on_dx=None):
        h_in, hn, gu = sv
        dxn, dout, t = _ffn_bwd_dx(dy, w, gu, "ffn_bwd_dx_" + tag)
        tok = 0.0 if on_dx is None else on_dx(dxn)
        tok = tok + on_dw(_ffn_bwd_dw(hn, dout, t, "ffn_bwd_dw_" + tag))
        dx, dgain = _rms_bwd(h_in, (gain + tok)[None], dxn, dy, "rms_bwd_" + tag)
        return dx, dgain[0]

    g_norm1, g_norm2, g_normm = [None] * depth, [None] * depth, [None] * depth
    small, halves, groups = [None, None], {}, {}

    def start_halves(key, make_parts):
        def hook(dw):
            halves[key], tok = halves_begin(make_parts(dw), "_" + key)
            return tok
        return hook

    def start_chips(key):
        def hook(after):
            groups[key], tok = chips_begin(halves.pop(key), after, "_" + key)
            return tok
        return hook

    for l in reversed(range(depth)):
        s1, (h_mix, hn, zb, core_vjp, w_in_t, w_out), s2 = saved[l]
        if l == 1:
            dy, g_norm2[l] = ffn_backward(dy, norm_ffn2[l], w_ffn2[l], s2, f"b{l}", start_halves("1", lambda dw: [dw]))
        else:
            dy, g_norm2[l] = ffn_backward(dy, norm_ffn2[l], w_ffn2[l], s2, f"b{l}", start_halves("3", lambda dw: [dw]),
                                          start_chips("2"))
        dyb = dy.astype(BF16)
        dz = _mm(dyb, w_out, "nt", F32, f"mix_out_dz{l}")
        dw_out = _mm(zb, dyb, "tn", BF16, f"mix_out_dw{l}")
        dub, small[l % 2] = core_vjp(dz)
        dw_in_t = _mm(dub, hn, "tn", BF16, f"mix_in_dw{l}")
        mix_parts = [dw_in_t[None], dw_out[None]]
        if l == 1:
            tok = start_chips("1")(dw_in_t)
        else:
            tok = start_chips("3")(dw_in_t) + start_halves("4", lambda _: mix_parts)(None)
        dhn = _mm(dub, w_in_t, "nn", F32, f"mix_in_dx{l}")
        dy, gm = _rms_bwd(h_mix, (norm_mix[l] + tok)[None], dhn, dy, f"rms_bwd_mix{l}")
        g_normm[l] = gm[0]
        if l == 1:
            dy, g_norm1[l] = ffn_backward(dy, norm_ffn1[l], w_ffn1[l], s1, f"a{l}",
                                          start_halves("2", lambda dw, od=mix_parts: od + [dw]))
        else:
            dy, g_norm1[l] = ffn_backward(dy, norm_ffn1[l], w_ffn1[l], s1, f"a{l}", start_halves("5", lambda dw: [dw]),
                                          start_chips("4"))
    grad_x = dy[None]
    start_chips("5")(dy)

    done = [dy, groups["5"][4]]
    (g_ffn2_1,) = reduce_end(groups["1"], done, "_1")
    g_od_in_t, g_od_out, g_ffn1_1 = reduce_end(groups["2"], done, "_2")
    (g_ffn2_0,) = reduce_end(groups["3"], done, "_3")
    g_ev_in_t, g_ev_out = reduce_end(groups["4"], done, "_4")
    g_ffn2 = [g_ffn2_0, g_ffn2_1]

    def ffn_grads(gl):
        return (jnp.stack([g[0].T for g in gl]), jnp.stack([g[1].T for g in gl]), jnp.stack([g[2] for g in gl]))

    g_ffn2_wg, g_ffn2_wu, g_ffn2_wd = ffn_grads(g_ffn2)
    grads = [None, None, None, None, None, None, g_ffn2_wg, g_ffn2_wu, g_ffn2_wd,
             g_ev_in_t[0].T[None], None, None, None, None, None,
             None, g_ev_out, g_od_in_t[0].T[None], None, None, g_od_out]
    weights = [norm_ffn1, ffn1_wg, ffn1_wu, ffn1_wd, norm_mix, norm_ffn2, ffn2_wg, ffn2_wu, ffn2_wd, ev_w_in,
               ev_conv_w, ev_conv_b, ev_cn_g, ev_cn_b, ev_qn_g, ev_kn_g, ev_w_out, od_w_in, od_lb_logits,
               od_gn_g, od_w_out]
    moms = [m_norm_ffn1, m_ffn1_wg, m_ffn1_wu, m_ffn1_wd, m_norm_mix, m_norm_ffn2, m_ffn2_wg, m_ffn2_wu,
            m_ffn2_wd, m_ev_w_in, m_ev_conv_w, m_ev_conv_b, m_ev_cn_g, m_ev_cn_b, m_ev_qn_g, m_ev_kn_g,
            m_ev_w_out, m_od_w_in, m_od_lb_logits, m_od_gn_g, m_od_w_out]
    vars_ = [v_norm_ffn1, v_ffn1_wg, v_ffn1_wu, v_ffn1_wd, v_norm_mix, v_norm_ffn2, v_ffn2_wg, v_ffn2_wu,
             v_ffn2_wd, v_ev_w_in, v_ev_conv_w, v_ev_conv_b, v_ev_cn_g, v_ev_cn_b, v_ev_qn_g, v_ev_kn_g,
             v_ev_w_out, v_od_w_in, v_od_lb_logits, v_od_gn_g, v_od_w_out]
    n_w = len(weights)
    deltas, new_m, new_v = [None] * n_w, [None] * n_w, [None] * n_w

    def update(idx):
        for i in idx:
            deltas[i], new_m[i], new_v[i] = _adamw(weights[i], grads[i], moms[i], vars_[i], f"adamw{i}")

    update([i for i in range(n_w) if grads[i] is not None])
    g_conv_w, g_conv_b, g_cn_g, g_cn_b, g_qn_g, g_kn_g = small[0]
    g_lb, g_gn = small[1]
    parts = [jnp.stack(g_norm1), jnp.stack(g_normm), jnp.stack(g_norm2), g_conv_b, g_cn_g, g_cn_b,
             g_qn_g, g_kn_g, g_lb, g_conv_w, g_gn, loss_part[0, :1]]
    red = _unpack_rows(_all_reduce_small(_pack_rows(parts), "reduce_small_grads", [d for d in deltas if d is not None]),
                       [p.shape for p in parts])
    g_norm1, g_normm, g_norm2, g_conv_b, g_cn_g, g_cn_b, g_qn_g, g_kn_g, g_lb, g_conv_w, g_gn, loss = red
    g_conv_w = lax.dynamic_slice(g_conv_w, (0, dev * cs), (cw, cs))
    g_gn = lax.dynamic_slice(g_gn, (dev * gs,), (gs,))
    small_idx = {0: g_norm1, 4: g_normm, 5: g_norm2, 10: g_conv_w[None], 11: g_conv_b[None], 12: g_cn_g[None],
                 13: g_cn_b[None], 14: g_qn_g[None], 15: g_kn_g[None], 18: g_lb, 19: g_gn[None]}
    for i, g in small_idx.items():
        grads[i] = g
    update(small_idx)
    (g_ffn1_0,) = reduce_end(groups["5"], [d for d in deltas if d is not None], "_5")
    grads[1], grads[2], grads[3] = ffn_grads([g_ffn1_0, g_ffn1_1])
    update((1, 2, 3))
    return (loss[0], grad_x, *grads, *deltas, *new_m, *new_v)
```
